```python
import math
import jax, jax.numpy as jnp
from jax import lax
import numpy as np

D_MODEL = 4096
BATCH = 8
SEQ = 4096
DEPTH = 1

HEAD_DIM = 128
D_MIX = D_MODEL
W_A = D_MIX // 2
W_B = D_MIX - W_A
G_A = W_A // HEAD_DIM
H_B = W_B // HEAD_DIM
CHUNK = 128
BLK = 128
DILATED = ((128, 1), (512, 4), (2048, 16))
NUM_BUCKETS = 32
MAX_DISTANCE = 2048
D_PLE = 256
EPS = 1e-6
NEG_INF = -1e30
PROJ_SPLITS = (W_A, W_A, W_A, W_B, W_B, W_B, W_B)
D_IN = W_A * 3 + W_B * 4

kernel_name = "hybrid_gmlp_dilated_attn_layer"


def _rmsnorm(x, g):
    xf = x.astype(jnp.float32)
    y = xf * lax.rsqrt(jnp.mean(xf * xf, axis=-1, keepdims=True) + EPS)
    return (y * g.astype(jnp.float32)).astype(x.dtype)


def _layernorm(x, g, b):
    xf = x.astype(jnp.float32)
    mu = jnp.mean(xf, axis=-1, keepdims=True)
    xc = xf - mu
    y = xc * lax.rsqrt(jnp.mean(xc * xc, axis=-1, keepdims=True) + EPS)
    return (y * g.astype(jnp.float32) + b.astype(jnp.float32)).astype(x.dtype)


def _rel_bucket(dist):
    max_exact = NUM_BUCKETS // 2
    d = jnp.maximum(dist, 1).astype(jnp.float32)
    large = max_exact + (jnp.log(d / max_exact) / math.log(MAX_DISTANCE / max_exact)
                         * (NUM_BUCKETS - max_exact)).astype(jnp.int32)
    large = jnp.minimum(large, NUM_BUCKETS - 1)
    return jnp.where(dist < max_exact, dist, large)


def _dilated_window(q, k, v, rel_bias, window, dil):
    b, s, h, dh = q.shape
    span = dil * BLK
    s_pad = -(-s // span) * span
    L = s_pad // dil
    nb = L // BLK
    pad = ((0, 0), (0, s_pad - s), (0, 0), (0, 0))

    def to_sub(t):
        t = jnp.pad(t, pad).reshape(b, L, dil, h, dh).transpose(0, 2, 1, 3, 4)
        return t.reshape(b, dil, nb, BLK, h, dh)

    def with_prev(t):
        prev = jnp.pad(t, ((0, 0), (0, 0), (1, 0), (0, 0), (0, 0), (0, 0)))[:, :, :-1]
        return jnp.concatenate([prev, t], axis=3)

    qs = to_sub(q)
    kc = with_prev(to_sub(k))
    vc = with_prev(to_sub(v))
    logits = jnp.einsum('brnqhd,brnkhd->brnhqk', qs, kc)

    qi = BLK + jnp.arange(BLK)
    kj = jnp.arange(2 * BLK)
    delta = qi[:, None] - kj[None, :]
    band = (delta >= 0) & (delta <= window // dil)
    bucket = _rel_bucket(jnp.clip(delta, 0, None) * dil)
    bias = rel_bias[bucket].astype(jnp.float32).transpose(2, 0, 1)
    has_prev = (jnp.arange(nb)[:, None, None] > 0) | (kj[None, None, :] >= BLK)
    mask = band[None] & has_prev

    logits = jnp.where(mask[None, None, :, None], logits + bias[None, None, None], NEG_INF)
    m = jnp.max(logits, axis=-1, keepdims=True)
    e = jnp.exp(logits - m)
    den = jnp.sum(e, axis=-1, keepdims=True)
    lse = (m + jnp.log(den))[..., 0].transpose(0, 1, 2, 4, 3)
    o = jnp.einsum('brnhqk,brnkhd->brnqhd', e, vc) / den[..., 0].transpose(0, 1, 2, 4, 3)[..., None]

    def from_sub(t):
        rest = t.shape[4:]
        t = t.reshape((b, dil, L) + rest).swapaxes(1, 2)
        return t.reshape((b, s_pad) + rest)[:, :s]

    return from_sub(o), from_sub(lse)


def _fwd_setup_inputs(seed: int = 0) -> dict:
    key = jax.random.key(seed)
    ks = jax.random.split(key, 17)
    n = lambda k, shape: jax.random.normal(k, shape, dtype=jnp.float32)
    return {
        "x": n(ks[0], (BATCH, SEQ, D_MODEL)),
        "p": n(ks[1], (DEPTH, BATCH, SEQ, D_PLE)),
        "g_pre": 1.0 + 0.02 * n(ks[2], (DEPTH, D_MODEL)),
        "w_in": n(ks[3], (DEPTH, D_MODEL, D_IN)) * D_MODEL ** -0.5,
        "w_s": n(ks[4], (DEPTH, G_A, CHUNK, CHUNK)) * CHUNK ** -0.5,
        "b_s": 1.0 + 0.1 * n(ks[5], (DEPTH, G_A, CHUNK)),
        "ln_v_g": 1.0 + 0.02 * n(ks[6], (DEPTH, W_A)),
        "ln_v_b": 0.02 * n(ks[7], (DEPTH, W_A)),
        "g_q": 1.0 + 0.02 * n(ks[8], (DEPTH, HEAD_DIM)),
        "g_k": 1.0 + 0.02 * n(ks[9], (DEPTH, HEAD_DIM)),
        "rel_bias": 0.1 * n(ks[10], (NUM_BUCKETS, H_B)),
        "g_out_a": 1.0 + 0.02 * n(ks[11], (DEPTH, W_A)),
        "g_out_b": 1.0 + 0.02 * n(ks[12], (DEPTH, W_B)),
        "w_out": n(ks[13], (DEPTH, D_MIX, D_MODEL)) * D_MIX ** -0.5,
        "g_ple": 1.0 + 0.02 * n(ks[14], (DEPTH, D_MODEL)),
        "w_ple_gate": n(ks[15], (DEPTH, D_MODEL, D_MODEL)) * D_MODEL ** -0.5,
        "w_ple_up": n(ks[16], (DEPTH, D_PLE, D_MODEL)) * D_PLE ** -0.5,
    }


def _fwd_reference(x, p, g_pre, w_in, w_s, b_s, ln_v_g, ln_v_b, g_q, g_k, rel_bias,
              g_out_a, g_out_b, w_out, g_ple, w_ple_gate, w_ple_up):
    b, s, _ = x.shape
    nc = s // CHUNK
    split_idx = [W_A, 2 * W_A, 3 * W_A, 3 * W_A + W_B, 3 * W_A + 2 * W_B, 3 * W_A + 3 * W_B]
    causal = jnp.tril(jnp.ones((CHUNK, CHUNK), dtype=w_s.dtype))
    for i in range(DEPTH):
        hn = _rmsnorm(x, g_pre[i])
        proj = hn @ w_in[i]
        a_u, a_v, a_z, q, k, v, b_z = jnp.split(proj, split_idx, axis=-1)

        a_u = jax.nn.gelu(a_u, approximate=False)
        a_v = _layernorm(jax.nn.gelu(a_v, approximate=False), ln_v_g[i], ln_v_b[i])
        vg = a_v.reshape(b, nc, CHUNK, G_A, HEAD_DIM)
        z = jnp.einsum('gts,bnsgc->bntgc', w_s[i] * causal, vg) \
            + b_s[i].T[None, None, :, :, None]
        y_a = a_u * z.reshape(b, s, W_A)

        qh = _rmsnorm(q.reshape(b, s, H_B, HEAD_DIM), g_q[i]).astype(jnp.float32) * (HEAD_DIM ** -0.5)
        kh = _rmsnorm(k.reshape(b, s, H_B, HEAD_DIM), g_k[i]).astype(jnp.float32)
        vh = v.reshape(b, s, H_B, HEAD_DIM).astype(jnp.float32)
        outs, lses = [], []
        for window, dil in DILATED:
            o_g, l_g = _dilated_window(qh, kh, vh, rel_bias, window, dil)
            outs.append(o_g)
            lses.append(l_g)
        alpha = jax.nn.softmax(jnp.stack(lses, axis=0), axis=0)
        y_b = jnp.sum(alpha[..., None] * jnp.stack(outs, axis=0), axis=0)
        y_b = y_b.reshape(b, s, W_B).astype(x.dtype)

        y = jnp.concatenate([_rmsnorm(y_a, g_out_a[i]) * jax.nn.silu(a_z),
                             _rmsnorm(y_b, g_out_b[i]) * jax.nn.silu(b_z)], axis=-1)
        h = x + y @ w_out[i]

        gate = jax.nn.sigmoid(_rmsnorm(h, g_ple[i]) @ w_ple_gate[i])
        x = h + gate * (p[i] @ w_ple_up[i])
    return x


import jax as _jax
import jax.numpy as _jnp

TWIN_FORMAT = 'train_step'
FWD_PARAMS = ['x', 'p', 'g_pre', 'w_in', 'w_s', 'b_s', 'ln_v_g', 'ln_v_b', 'g_q', 'g_k', 'rel_bias', 'g_out_a', 'g_out_b', 'w_out', 'g_ple', 'w_ple_gate', 'w_ple_up']
TWIN_WEIGHTS = ['g_pre', 'w_in', 'w_s', 'b_s', 'ln_v_g', 'ln_v_b', 'g_q', 'g_k', 'rel_bias', 'g_out_a', 'g_out_b', 'w_out', 'g_ple', 'w_ple_gate', 'w_ple_up']
TWIN_DIFF_INPUT = 'x'
TWIN_INPUTS = ['x', 'p', 'g_pre', 'w_in', 'w_s', 'b_s', 'ln_v_g', 'ln_v_b', 'g_q', 'g_k', 'rel_bias', 'g_out_a', 'g_out_b', 'w_out', 'g_ple', 'w_ple_gate', 'w_ple_up', 'loss_target', 'm_g_pre', 'm_w_in', 'm_w_s', 'm_b_s', 'm_ln_v_g', 'm_ln_v_b', 'm_g_q', 'm_g_k', 'm_rel_bias', 'm_g_out_a', 'm_g_out_b', 'm_w_out', 'm_g_ple', 'm_w_ple_gate', 'm_w_ple_up', 'v_g_pre', 'v_w_in', 'v_w_s', 'v_b_s', 'v_ln_v_g', 'v_ln_v_b', 'v_g_q', 'v_g_k', 'v_rel_bias', 'v_g_out_a', 'v_g_out_b', 'v_w_out', 'v_g_ple', 'v_w_ple_gate', 'v_w_ple_up']
TWIN_OUTPUTS = ['loss', 'grad_x', 'grad_g_pre', 'grad_w_in', 'grad_w_s', 'grad_b_s', 'grad_ln_v_g', 'grad_ln_v_b', 'grad_g_q', 'grad_g_k', 'grad_rel_bias', 'grad_g_out_a', 'grad_g_out_b', 'grad_w_out', 'grad_g_ple', 'grad_w_ple_gate', 'grad_w_ple_up', 'delta_g_pre', 'delta_w_in', 'delta_w_s', 'delta_b_s', 'delta_ln_v_g', 'delta_ln_v_b', 'delta_g_q', 'delta_g_k', 'delta_rel_bias', 'delta_g_out_a', 'delta_g_out_b', 'delta_w_out', 'delta_g_ple', 'delta_w_ple_gate', 'delta_w_ple_up', 'new_m_g_pre', 'new_m_w_in', 'new_m_w_s', 'new_m_b_s', 'new_m_ln_v_g', 'new_m_ln_v_b', 'new_m_g_q', 'new_m_g_k', 'new_m_rel_bias', 'new_m_g_out_a', 'new_m_g_out_b', 'new_m_w_out', 'new_m_g_ple', 'new_m_w_ple_gate', 'new_m_w_ple_up', 'new_v_g_pre', 'new_v_w_in', 'new_v_w_s', 'new_v_b_s', 'new_v_ln_v_g', 'new_v_ln_v_b', 'new_v_g_q', 'new_v_g_k', 'new_v_rel_bias', 'new_v_g_out_a', 'new_v_g_out_b', 'new_v_w_out', 'new_v_g_ple', 'new_v_w_ple_gate', 'new_v_w_ple_up']
TWIN_LEAF_KINDS = {'loss': 'loss', 'grad_x': 'grad_x', 'grad_g_pre': 'grad_w', 'grad_w_in': 'grad_w', 'grad_w_s': 'grad_w', 'grad_b_s': 'grad_w', 'grad_ln_v_g': 'grad_w', 'grad_ln_v_b': 'grad_w', 'grad_g_q': 'grad_w', 'grad_g_k': 'grad_w', 'grad_rel_bias': 'grad_w', 'grad_g_out_a': 'grad_w', 'grad_g_out_b': 'grad_w', 'grad_w_out': 'grad_w', 'grad_g_ple': 'grad_w', 'grad_w_ple_gate': 'grad_w', 'grad_w_ple_up': 'grad_w', 'delta_g_pre': 'delta_w', 'delta_w_in': 'delta_w', 'delta_w_s': 'delta_w', 'delta_b_s': 'delta_w', 'delta_ln_v_g': 'delta_w', 'delta_ln_v_b': 'delta_w', 'delta_g_q': 'delta_w', 'delta_g_k': 'delta_w', 'delta_rel_bias': 'delta_w', 'delta_g_out_a': 'delta_w', 'delta_g_out_b': 'delta_w', 'delta_w_out': 'delta_w', 'delta_g_ple': 'delta_w', 'delta_w_ple_gate': 'delta_w', 'delta_w_ple_up': 'delta_w', 'new_m_g_pre': 'new_m', 'new_m_w_in': 'new_m', 'new_m_w_s': 'new_m', 'new_m_b_s': 'new_m', 'new_m_ln_v_g': 'new_m', 'new_m_ln_v_b': 'new_m', 'new_m_g_q': 'new_m', 'new_m_g_k': 'new_m', 'new_m_rel_bias': 'new_m', 'new_m_g_out_a': 'new_m', 'new_m_g_out_b': 'new_m', 'new_m_w_out': 'new_m', 'new_m_g_ple': 'new_m', 'new_m_w_ple_gate': 'new_m', 'new_m_w_ple_up': 'new_m', 'new_v_g_pre': 'new_v', 'new_v_w_in': 'new_v', 'new_v_w_s': 'new_v', 'new_v_b_s': 'new_v', 'new_v_ln_v_g': 'new_v', 'new_v_ln_v_b': 'new_v', 'new_v_g_q': 'new_v', 'new_v_g_k': 'new_v', 'new_v_rel_bias': 'new_v', 'new_v_g_out_a': 'new_v', 'new_v_g_out_b': 'new_v', 'new_v_w_out': 'new_v', 'new_v_g_ple': 'new_v', 'new_v_w_ple_gate': 'new_v', 'new_v_w_ple_up': 'new_v'}


def _forward(args):
    return _fwd_reference(*[args[k] for k in FWD_PARAMS])


def _output_shape():
    out = _jax.eval_shape(lambda: _forward(_fwd_setup_inputs(0)))
    return out.shape, out.dtype

N_MICROBATCH = 1
ADAM_LR = 0.001
ADAM_B1 = 0.9
ADAM_B2 = 0.999
ADAM_EPS = 1e-08
ADAM_WD = 0.01
ADAM_STEP = 10
PER_EXAMPLE_BATCH_AXIS = {'x': 0, 'p': 1, 'loss_target': 0}
SHARED_INPUTS = []
_WEIGHT_DTYPES = {'g_pre': _jnp.float32, 'w_in': _jnp.float32, 'w_s': _jnp.float32, 'b_s': _jnp.float32, 'ln_v_g': _jnp.float32, 'ln_v_b': _jnp.float32, 'g_q': _jnp.float32, 'g_k': _jnp.float32, 'rel_bias': _jnp.float32, 'g_out_a': _jnp.float32, 'g_out_b': _jnp.float32, 'w_out': _jnp.float32, 'g_ple': _jnp.float32, 'w_ple_gate': _jnp.float32, 'w_ple_up': _jnp.float32}
MOMENT_SCALE = {'g_pre': 3.242457e+00, 'w_in': 8.013373e-02, 'w_s': 6.485647e-02, 'b_s': 9.343013e-02, 'ln_v_g': 8.210199e-02, 'ln_v_b': 6.379336e-02, 'g_q': 2.315721e-01, 'g_k': 2.317333e-01, 'rel_bias': 1.097490e-01, 'g_out_a': 2.685251e+00, 'g_out_b': 2.797562e+00, 'w_out': 5.720139e-02, 'g_ple': 2.370929e-01, 'w_ple_gate': 2.830269e-02, 'w_ple_up': 1.052111e-01}


def _to_microbatches(a, axis):
    t = _jnp.moveaxis(a, axis, 0)
    t = t.reshape((N_MICROBATCH, t.shape[0] // N_MICROBATCH) + t.shape[1:])
    return _jnp.moveaxis(t, 1, axis + 1)


def setup_inputs(seed: int = 0) -> dict:
    inp = _fwd_setup_inputs(seed)
    key = _jax.random.fold_in(_jax.random.key(seed), 7919)
    shape, _ = _output_shape()
    out = dict(inp)
    out["loss_target"] = _jax.random.normal(_jax.random.fold_in(key, 0), shape, _jnp.float32)
    for i, name in enumerate(TWIN_WEIGHTS):
        w = inp[name].astype(_jnp.float32)
        if MOMENT_SCALE is None:
            s = _jnp.sqrt(_jnp.mean(_jnp.square(w)) + 1e-30)
        else:
            s = MOMENT_SCALE[name]
        km, kv = _jax.random.split(_jax.random.fold_in(key, i + 1))
        out[name] = w
        out["m_" + name] = s * _jax.random.normal(km, w.shape, _jnp.float32)
        out["v_" + name] = (s * s) * _jax.random.uniform(kv, w.shape, _jnp.float32, 0.5, 1.5)
    if N_MICROBATCH > 1:
        for name, axis in PER_EXAMPLE_BATCH_AXIS.items():
            out[name] = _to_microbatches(out[name], axis)
    return {'x': out['x'], 'p': out['p'], 'g_pre': out['g_pre'], 'w_in': out['w_in'], 'w_s': out['w_s'], 'b_s': out['b_s'], 'ln_v_g': out['ln_v_g'], 'ln_v_b': out['ln_v_b'], 'g_q': out['g_q'], 'g_k': out['g_k'], 'rel_bias': out['rel_bias'], 'g_out_a': out['g_out_a'], 'g_out_b': out['g_out_b'], 'w_out': out['w_out'], 'g_ple': out['g_ple'], 'w_ple_gate': out['w_ple_gate'], 'w_ple_up': out['w_ple_up'], 'loss_target': out['loss_target'], 'm_g_pre': out['m_g_pre'], 'm_w_in': out['m_w_in'], 'm_w_s': out['m_w_s'], 'm_b_s': out['m_b_s'], 'm_ln_v_g': out['m_ln_v_g'], 'm_ln_v_b': out['m_ln_v_b'], 'm_g_q': out['m_g_q'], 'm_g_k': out['m_g_k'], 'm_rel_bias': out['m_rel_bias'], 'm_g_out_a': out['m_g_out_a'], 'm_g_out_b': out['m_g_out_b'], 'm_w_out': out['m_w_out'], 'm_g_ple': out['m_g_ple'], 'm_w_ple_gate': out['m_w_ple_gate'], 'm_w_ple_up': out['m_w_ple_up'], 'v_g_pre': out['v_g_pre'], 'v_w_in': out['v_w_in'], 'v_w_s': out['v_w_s'], 'v_b_s': out['v_b_s'], 'v_ln_v_g': out['v_ln_v_g'], 'v_ln_v_b': out['v_ln_v_b'], 'v_g_q': out['v_g_q'], 'v_g_k': out['v_g_k'], 'v_rel_bias': out['v_rel_bias'], 'v_g_out_a': out['v_g_out_a'], 'v_g_out_b': out['v_g_out_b'], 'v_w_out': out['v_w_out'], 'v_g_ple': out['v_g_ple'], 'v_w_ple_gate': out['v_w_ple_gate'], 'v_w_ple_up': out['v_w_ple_up']}


def _loss(weights, diff, rest, loss_target):
    with _jax.named_scope("forward"):
        args = {**rest, TWIN_DIFF_INPUT: diff, **{k: w.astype(_WEIGHT_DTYPES[k]) for k, w in weights.items()}}
        y = _forward(args)
    with _jax.named_scope("loss_head"):
        err = _jnp.square(y.astype(_jnp.float32) - loss_target)
        return 0.5 * _jnp.sum(_jnp.mean(err, axis=-1)) if err.ndim else 0.5 * err


def _adamw(w, g, m, v):
    m = ADAM_B1 * m + (1.0 - ADAM_B1) * g
    v = ADAM_B2 * v + (1.0 - ADAM_B2) * _jnp.square(g)
    m_hat = m / (1.0 - ADAM_B1 ** ADAM_STEP)
    v_hat = v / (1.0 - ADAM_B2 ** ADAM_STEP)
    delta = -ADAM_LR * (m_hat / (_jnp.sqrt(v_hat) + ADAM_EPS) + ADAM_WD * w)
    return delta, m, v


def reference(x, p, g_pre, w_in, w_s, b_s, ln_v_g, ln_v_b, g_q, g_k, rel_bias, g_out_a, g_out_b, w_out, g_ple, w_ple_gate, w_ple_up, loss_target, m_g_pre, m_w_in, m_w_s, m_b_s, m_ln_v_g, m_ln_v_b, m_g_q, m_g_k, m_rel_bias, m_g_out_a, m_g_out_b, m_w_out, m_g_ple, m_w_ple_gate, m_w_ple_up, v_g_pre, v_w_in, v_w_s, v_b_s, v_ln_v_g, v_ln_v_b, v_g_q, v_g_k, v_rel_bias, v_g_out_a, v_g_out_b, v_w_out, v_g_ple, v_w_ple_gate, v_w_ple_up):
    given = dict(x=x, p=p, g_pre=g_pre, w_in=w_in, w_s=w_s, b_s=b_s, ln_v_g=ln_v_g, ln_v_b=ln_v_b, g_q=g_q, g_k=g_k, rel_bias=rel_bias, g_out_a=g_out_a, g_out_b=g_out_b, w_out=w_out, g_ple=g_ple, w_ple_gate=w_ple_gate, w_ple_up=w_ple_up, loss_target=loss_target, m_g_pre=m_g_pre, m_w_in=m_w_in, m_w_s=m_w_s, m_b_s=m_b_s, m_ln_v_g=m_ln_v_g, m_ln_v_b=m_ln_v_b, m_g_q=m_g_q, m_g_k=m_g_k, m_rel_bias=m_rel_bias, m_g_out_a=m_g_out_a, m_g_out_b=m_g_out_b, m_w_out=m_w_out, m_g_ple=m_g_ple, m_w_ple_gate=m_w_ple_gate, m_w_ple_up=m_w_ple_up, v_g_pre=v_g_pre, v_w_in=v_w_in, v_w_s=v_w_s, v_b_s=v_b_s, v_ln_v_g=v_ln_v_g, v_ln_v_b=v_ln_v_b, v_g_q=v_g_q, v_g_k=v_g_k, v_rel_bias=v_rel_bias, v_g_out_a=v_g_out_a, v_g_out_b=v_g_out_b, v_w_out=v_w_out, v_g_ple=v_g_ple, v_w_ple_gate=v_w_ple_gate, v_w_ple_up=v_w_ple_up)
    weights = {n: given[n] for n in TWIN_WEIGHTS}
    shared = {n: given[n] for n in SHARED_INPUTS}
    per_example = {n: given[n] for n in ['x', 'p']}
    grad_fn = _jax.value_and_grad(_loss, argnums=(0, 1))

    def one_microbatch(ex, loss_target):
        ex = dict(ex)
        diff = ex.pop(TWIN_DIFF_INPUT)
        return grad_fn(weights, diff, {**shared, **ex}, loss_target)

    if N_MICROBATCH == 1:
        loss, (grad_w, grad_x) = one_microbatch(per_example, given["loss_target"])
    else:
        def body(carry, xs):
            loss_sum, grad_sum = carry
            l_k, (gw_k, gx_k) = one_microbatch(xs[0], xs[1])
            with _jax.named_scope("update"):
                return (loss_sum + l_k, _jax.tree.map(_jnp.add, grad_sum, gw_k)), gx_k

        init = (_jnp.zeros((), _jnp.float32), _jax.tree.map(_jnp.zeros_like, weights))
        (loss, grad_w), grad_x = _jax.lax.scan(body, init, (per_example, given["loss_target"]))
    with _jax.named_scope("update"):
        delta_w, new_m, new_v = {}, {}, {}
        for n in TWIN_WEIGHTS:
            delta_w[n], new_m[n], new_v[n] = _adamw(weights[n], grad_w[n], given["m_" + n], given["v_" + n])
    return (loss, grad_x, *[grad_w[n] for n in TWIN_WEIGHTS], *[delta_w[n] for n in TWIN_WEIGHTS],
            *[new_m[n] for n in TWIN_WEIGHTS], *[new_v[n] for n in TWIN_WEIGHTS])
```

```python
import functools
import math

import jax
import jax.numpy as jnp
from jax import lax
from jax.experimental import pallas as pl
from jax.experimental.pallas import tpu as pltpu

F32 = jnp.float32
BF16 = jnp.bfloat16
EPS = 1e-6
NEG_INF = -1e30
HEAD = 128
DILATIONS = (1, 4, 16)
NUM_BUCKETS = 32
MAX_DISTANCE = 2048
N_SEG = 7
ADAM_LR = 0.001
ADAM_B1 = 0.9
ADAM_B2 = 0.999
ADAM_EPS = 1e-08
ADAM_WD = 0.01
ADAM_STEP = 10
AXES = ("x", "y", "c")
N_DEV = 8
MIB = 1 << 20

NT_DIMS = (((1,), (1,)), ((), ()))
TN_DIMS = (((0,), (0,)), ((), ()))
NN_DIMS = (((1,), (0,)), ((), ()))


def _params(semantics, vmem_mib):
    return pltpu.CompilerParams(dimension_semantics=semantics, vmem_limit_bytes=vmem_mib * MIB)


def _gelu(a):
    return 0.5 * a * (1.0 + lax.erf(a * (2.0 ** -0.5)))


def _gelu_grad(a):
    return 0.5 * (1.0 + lax.erf(a * (2.0 ** -0.5))) + a * jnp.exp(-0.5 * a * a) * ((2.0 * math.pi) ** -0.5)


def _silu_and_grad(a):
    s = jax.nn.sigmoid(a)
    return a * s, s * (1.0 + a * (1.0 - s))


def _rms(v):
    return lax.rsqrt(jnp.mean(v * v, axis=-1, keepdims=True) + EPS)


def _rms_bwd(dy, v, r, g):
    gy = dy * g
    return r * gy - v * (r * r * r) * jnp.mean(gy * v, axis=-1, keepdims=True)


def _dot(a, b, dims=NN_DIMS):
    return lax.dot_general(a, b, dims, preferred_element_type=F32)


def _lane_pick(cols, width):
    rows = cols[0].shape[0]
    lane = lax.broadcasted_iota(jnp.int32, (rows, width), 1)
    out = jnp.zeros((rows, width), F32)
    for h, col in enumerate(cols):
        out = jnp.where(lane == h, col, out)
    return out


def _rmsnorm_fwd(x, g, name):
    s, d = x.shape
    tm = min(256, s)

    def body(x_ref, g_ref, o_ref):
        v = x_ref[...]
        o_ref[...] = (v * _rms(v) * g_ref[...]).astype(BF16)

    return pl.pallas_call(
        body, name=name, grid=(s // tm,),
        out_shape=jax.ShapeDtypeStruct((s, d), BF16),
        in_specs=[pl.BlockSpec((tm, d), lambda i: (i, 0)), pl.BlockSpec((1, d), lambda i: (0, 0))],
        out_specs=pl.BlockSpec((tm, d), lambda i: (i, 0)),
        compiler_params=_params(("arbitrary",), 40),
    )(x, g)


def _rmsnorm_bwd(dy, v, g, res, name, with_bf16):
    s, d = v.shape
    tm = min(128, s)

    def body(dy_ref, v_ref, g_ref, res_ref, *outs):
        dx_ref, dg_ref = outs[0], outs[-1]
        i = pl.program_id(0)
        vv, dyv = v_ref[...], dy_ref[...]
        r = _rms(vv)
        dx = res_ref[...] + _rms_bwd(dyv, vv, r, g_ref[...])
        dx_ref[...] = dx
        if with_bf16:
            outs[1][...] = dx.astype(BF16)

        @pl.when(i == 0)
        def _():
            dg_ref[...] = jnp.zeros_like(dg_ref)

        dg_ref[...] += jnp.sum(dyv * vv * r, axis=0, keepdims=True)

    row = pl.BlockSpec((tm, d), lambda i: (i, 0))
    vec = pl.BlockSpec((1, d), lambda i: (0, 0))
    shapes = [jax.ShapeDtypeStruct((s, d), F32)]
    specs = [row]
    if with_bf16:
        shapes.append(jax.ShapeDtypeStruct((s, d), BF16))
        specs.append(row)
    shapes.append(jax.ShapeDtypeStruct((1, d), F32))
    specs.append(vec)
    return pl.pallas_call(
        body, name=name, grid=(s // tm,), out_shape=shapes,
        in_specs=[row, row, vec, row], out_specs=specs,
        compiler_params=_params(("arbitrary",), 40),
    )(dy, v, g, res)


def _matmul(a, b, *, name, grid, a_spec, b_spec, dims, acc_shape, out_shapes, out_specs,
            extra=(), extra_specs=(), epilogue=None, vmem_mib=48):
    nk = grid[2]
    n_extra, n_out = len(extra), len(out_shapes)

    def body(*refs):
        a_ref, b_ref = refs[0], refs[1]
        ex = refs[2:2 + n_extra]
        outs = refs[2 + n_extra:2 + n_extra + n_out]
        acc = refs[-1]
        k = pl.program_id(2)

        @pl.when(k == 0)
        def _():
            acc[...] = jnp.zeros_like(acc)

        av = a_ref[...]
        if av.dtype != BF16:
            av = av.astype(BF16)
        acc[...] += _dot(av, b_ref[...], dims)

        @pl.when(k == nk - 1)
        def _():
            if epilogue is None:
                outs[0][...] = acc[...].astype(outs[0].dtype)
            else:
                epilogue(acc, ex, outs)

    return pl.pallas_call(
        body, name=name, grid=grid, out_shape=list(out_shapes),
        in_specs=[a_spec, b_spec, *extra_specs], out_specs=list(out_specs),
        scratch_shapes=[pltpu.VMEM(acc_shape, F32)],
        compiler_params=_params(("parallel", "parallel", "arbitrary"), vmem_mib),
    )(a, b, *extra)


def _tile(n, want):
    t = min(n, want)
    while n % t:
        t //= 2
    return t


def _rel_buckets(dil):
    qi = HEAD + jnp.arange(HEAD)
    kj = jnp.arange(2 * HEAD)
    delta = qi[:, None] - kj[None, :]
    band = (delta >= 0) & (delta <= HEAD)
    dist = jnp.clip(delta, 0, None) * dil
    max_exact = NUM_BUCKETS // 2
    dd = jnp.maximum(dist, 1).astype(F32)
    large = max_exact + (jnp.log(dd / max_exact) / math.log(MAX_DISTANCE / max_exact)
                         * (NUM_BUCKETS - max_exact)).astype(jnp.int32)
    large = jnp.minimum(large, NUM_BUCKETS - 1)
    bucket = jnp.where(dist < max_exact, dist, large)
    return jnp.where(band, bucket, -1).astype(jnp.int32)


def _bias_build(rel_bias, buckets, n_heads):
    nd = buckets.shape[0]

    def body(rb_ref, bk_ref, o_ref):
        for c in range(nd):
            def per_head(h, carry, c=c):
                bk = bk_ref[c]
                acc = jnp.where(bk < 0, NEG_INF, 0.0).astype(F32)
                for b in range(NUM_BUCKETS):
                    acc = jnp.where(bk == b, rb_ref[b, h], acc)
                o_ref[c, h] = acc
                return carry

            lax.fori_loop(0, n_heads, per_head, 0)

    return pl.pallas_call(
        body, name="bias_build",
        out_shape=jax.ShapeDtypeStruct((nd, n_heads, HEAD, 2 * HEAD), F32),
        in_specs=[pl.BlockSpec(memory_space=pltpu.SMEM), pl.BlockSpec(memory_space=pltpu.VMEM)],
        out_specs=pl.BlockSpec(memory_space=pltpu.VMEM),
    )(rel_bias, buckets)


def _bias_grad(ds_all, buckets, n_heads):
    nd = buckets.shape[0]

    def body(ds_ref, bk_ref, o_ref):
        def per_pair(i, carry):
            b, h = i // n_heads, i % n_heads
            tot = jnp.zeros((), F32)
            for c in range(nd):
                tot = tot + jnp.sum(jnp.where(bk_ref[c] == b, ds_ref[c, h], 0.0))
            o_ref[b, h] = tot
            return carry

        lax.fori_loop(0, NUM_BUCKETS * n_heads, per_pair, 0)

    return pl.pallas_call(
        body, name="bias_grad",
        out_shape=jax.ShapeDtypeStruct((NUM_BUCKETS, n_heads), F32),
        in_specs=[pl.BlockSpec(memory_space=pltpu.VMEM), pl.BlockSpec(memory_space=pltpu.VMEM)],
        out_specs=pl.BlockSpec(memory_space=pltpu.SMEM),
    )(ds_all, buckets)


def _qkv_prep(proj, g_q, g_k, w):
    s = proj.shape[0]
    n_heads = w // HEAD
    tm = HEAD

    def body(q_ref, k_ref, v_ref, gq_ref, gk_ref, qn_ref, kn_ref, vb_ref):
        gq = gq_ref[...] * (HEAD ** -0.5)
        gk = gk_ref[...]
        for h in range(n_heads):
            sl = slice(h * HEAD, (h + 1) * HEAD)
            q = q_ref[:, sl]
            k = k_ref[:, sl]
            qn_ref[:, sl] = (q * _rms(q) * gq).astype(BF16)
            kn_ref[:, sl] = (k * _rms(k) * gk).astype(BF16)
        vb_ref[...] = v_ref[...].astype(BF16)

    seg = lambda j: pl.BlockSpec((tm, w), lambda i, j=j: (i, j))
    vec = pl.BlockSpec((1, HEAD), lambda i: (0, 0))
    out = pl.BlockSpec((tm, w), lambda i: (i, 0))
    return pl.pallas_call(
        body, name="qkv_prep", grid=(s // tm,),
        out_shape=[jax.ShapeDtypeStruct((s, w), BF16)] * 3,
        in_specs=[seg(3), seg(4), seg(5), vec, vec], out_specs=[out, out, out],
        compiler_params=_params(("arbitrary",), 40),
    )(proj, proj, proj, g_q, g_k)


def _attn_fwd(qn, kn, vb, bias, dil, name):
    s, w = qn.shape
    n_heads = w // HEAD
    length = s // dil
    nb = length // HEAD
    view = lambda t: t.reshape(length, dil * w)

    def body(q_ref, kc_ref, kp_ref, vc_ref, vp_ref, bias_ref, o_ref, lse_ref):
        n = pl.program_id(1)
        lses = []
        for h in range(n_heads):
            sl = slice(h * HEAD, (h + 1) * HEAD)
            q = q_ref[:, sl]
            s_p = _dot(q, kp_ref[:, sl], NT_DIMS) + bias_ref[h, :, :HEAD]
            s_p = jnp.where(n > 0, s_p, NEG_INF)
            s_c = _dot(q, kc_ref[:, sl], NT_DIMS) + bias_ref[h, :, HEAD:]
            m = jnp.maximum(jnp.max(s_p, axis=-1, keepdims=True), jnp.max(s_c, axis=-1, keepdims=True))
            e_p = jnp.exp(s_p - m)
            e_c = jnp.exp(s_c - m)
            den = jnp.sum(e_p, axis=-1, keepdims=True) + jnp.sum(e_c, axis=-1, keepdims=True)
            o = _dot(e_p.astype(BF16), vp_ref[:, sl]) + _dot(e_c.astype(BF16), vc_ref[:, sl])
            o_ref[:, sl] = (o / den).astype(BF16)
            lses.append(m + jnp.log(den))
        lse_ref[...] = _lane_pick(lses, HEAD)

    cur = pl.BlockSpec((HEAD, w), lambda r, n: (n, r))
    prev = pl.BlockSpec((HEAD, w), lambda r, n: (jnp.maximum(n - 1, 0), r))
    return pl.pallas_call(
        body, name=name, grid=(dil, nb),
        out_shape=[jax.ShapeDtypeStruct((length, dil * w), BF16),
                   jax.ShapeDtypeStruct((length, dil * HEAD), F32)],
        in_specs=[cur, cur, prev, cur, prev,
                  pl.BlockSpec((n_heads, HEAD, 2 * HEAD), lambda r, n: (0, 0, 0))],
        out_specs=[cur, pl.BlockSpec((HEAD, HEAD), lambda r, n: (n, r))],
        compiler_params=_params(("arbitrary", "arbitrary"), 40),
    )(view(qn), view(kn), view(kn), view(vb), view(vb), bias)


def _attn_bwd(qn, kn, vb, dyb, lse, delta, bias, dil, name):
    s, w = qn.shape
    n_heads = w // HEAD
    length = s // dil
    nb = length // HEAD
    view = lambda t: t.reshape(length, dil * t.shape[1])

    def body(q_ref, kc_ref, kp_ref, vc_ref, vp_ref, dy_ref, lse_ref, dl_ref, bias_ref,
             dq_ref, dk_ref, dv_ref, ds_ref, carry_k, carry_v):
        r = pl.program_id(0)
        step = pl.program_id(1)
        blk = nb - 1 - step

        @pl.when((r == 0) & (step == 0))
        def _():
            ds_ref[...] = jnp.zeros_like(ds_ref)

        @pl.when(step == 0)
        def _():
            carry_k[...] = jnp.zeros_like(carry_k)
            carry_v[...] = jnp.zeros_like(carry_v)

        for h in range(n_heads):
            sl = slice(h * HEAD, (h + 1) * HEAD)
            q, dy = q_ref[:, sl], dy_ref[:, sl]
            kp, kc, vp, vc = kp_ref[:, sl], kc_ref[:, sl], vp_ref[:, sl], vc_ref[:, sl]
            tot = lse_ref[:, h:h + 1]
            dl = dl_ref[:, h:h + 1]
            s_p = _dot(q, kp, NT_DIMS) + bias_ref[h, :, :HEAD]
            s_p = jnp.where(blk > 0, s_p, NEG_INF)
            s_c = _dot(q, kc, NT_DIMS) + bias_ref[h, :, HEAD:]
            p_p = jnp.exp(s_p - tot)
            p_c = jnp.exp(s_c - tot)
            ds_p = p_p * (_dot(dy, vp, NT_DIMS) - dl)
            ds_c = p_c * (_dot(dy, vc, NT_DIMS) - dl)
            ds_ref[h, :, :HEAD] += ds_p
            ds_ref[h, :, HEAD:] += ds_c
            ds_pb, ds_cb = ds_p.astype(BF16), ds_c.astype(BF16)
            dq_ref[:, sl] = (_dot(ds_pb, kp) + _dot(ds_cb, kc)).astype(BF16)
            dk_ref[:, sl] = (_dot(ds_cb, q, TN_DIMS) + carry_k[:, sl]).astype(BF16)
            carry_k[:, sl] = _dot(ds_pb, q, TN_DIMS)
            dv_ref[:, sl] = (_dot(p_c.astype(BF16), dy, TN_DIMS) + carry_v[:, sl]).astype(BF16)
            carry_v[:, sl] = _dot(p_p.astype(BF16), dy, TN_DIMS)

    cur = pl.BlockSpec((HEAD, w), lambda r, n: (nb - 1 - n, r))
    prev = pl.BlockSpec((HEAD, w), lambda r, n: (jnp.maximum(nb - 2 - n, 0), r))
    stat = pl.BlockSpec((HEAD, HEAD), lambda r, n: (nb - 1 - n, r))
    whole = pl.BlockSpec((n_heads, HEAD, 2 * HEAD), lambda r, n: (0, 0, 0))
    big = jax.ShapeDtypeStruct((length, dil * w), BF16)
    return pl.pallas_call(
        body, name=name, grid=(dil, nb),
        out_shape=[big, big, big, jax.ShapeDtypeStruct((n_heads, HEAD, 2 * HEAD), F32)],
        in_specs=[cur, cur, prev, cur, prev, cur, stat, stat, whole],
        out_specs=[cur, cur, cur, whole],
        scratch_shapes=[pltpu.VMEM((HEAD, w), F32), pltpu.VMEM((HEAD, w), F32)],
        compiler_params=_params(("arbitrary", "arbitrary"), 40),
    )(view(qn), view(kn), view(kn), view(vb), view(vb), view(dyb), view(lse), view(delta), bias)


def _qkv_bwd(dproj, proj, dqs, dks, dvs, g_q, g_k, w):
    s = proj.shape[0]
    n_heads = w // HEAD
    tm = HEAD

    def body(dproj_hbm, q_ref, k_ref, gq_ref, gk_ref, *rest):
        dq_refs, dk_refs, dv_refs = rest[0:3], rest[3:6], rest[6:9]
        out_ref, dgq_ref, dgk_ref = rest[9:12]
        i = pl.program_id(0)
        gq = gq_ref[...] * (HEAD ** -0.5)
        gk = gk_ref[...]
        acc_q = jnp.zeros((1, HEAD), F32)
        acc_k = jnp.zeros((1, HEAD), F32)
        for h in range(n_heads):
            sl = slice(h * HEAD, (h + 1) * HEAD)
            q, k = q_ref[:, sl], k_ref[:, sl]
            dqn = sum(t[:, sl].astype(F32) for t in dq_refs)
            dkn = sum(t[:, sl].astype(F32) for t in dk_refs)
            rq, rk = _rms(q), _rms(k)
            out_ref[:, h * HEAD:(h + 1) * HEAD] = _rms_bwd(dqn, q, rq, gq).astype(BF16)
            out_ref[:, w + h * HEAD:w + (h + 1) * HEAD] = _rms_bwd(dkn, k, rk, gk).astype(BF16)
            acc_q += jnp.sum(dqn * q * rq, axis=0, keepdims=True)
            acc_k += jnp.sum(dkn * k * rk, axis=0, keepdims=True)
        out_ref[:, 2 * w:] = sum(t[...].astype(F32) for t in dv_refs).astype(BF16)

        @pl.when(i == 0)
        def _():
            dgq_ref[...] = jnp.zeros_like(dgq_ref)
            dgk_ref[...] = jnp.zeros_like(dgk_ref)

        dgq_ref[...] += acc_q * (HEAD ** -0.5)
        dgk_ref[...] += acc_k

    seg = lambda j: pl.BlockSpec((tm, w), lambda i, j=j: (i, j))
    vec = pl.BlockSpec((1, HEAD), lambda i: (0, 0))
    row = pl.BlockSpec((tm, w), lambda i: (i, 0))
    return pl.pallas_call(
        body, name="qkv_bwd", grid=(s // tm,),
        out_shape=[jax.ShapeDtypeStruct(dproj.shape, BF16),
                   jax.ShapeDtypeStruct((1, HEAD), F32), jax.ShapeDtypeStruct((1, HEAD), F32)],
        in_specs=[pl.BlockSpec(memory_space=pl.ANY), seg(3), seg(4), vec, vec] + [row] * 9,
        out_specs=[pl.BlockSpec((tm, 3 * w), lambda i: (i, 1)), vec, vec],
        input_output_aliases={0: 0},
        compiler_params=_params(("arbitrary",), 48),
    )(dproj, proj, proj, g_q, g_k, *dqs, *dks, *dvs)


def _mixer_a(au, av, ws_ref, bst_ref, lng, lnb, z_scr, ln_scr):
    n_groups = au.shape[1] // HEAD
    u = _gelu(au)
    gv = _gelu(av)
    mu = jnp.mean(gv, axis=-1, keepdims=True)
    xc = gv - mu
    rs = lax.rsqrt(jnp.mean(xc * xc, axis=-1, keepdims=True) + EPS)
    xhat = xc * rs
    ln_scr[...] = (xhat * lng + lnb).astype(BF16)
    row = lax.broadcasted_iota(jnp.int32, (HEAD, HEAD), 0)
    col = lax.broadcasted_iota(jnp.int32, (HEAD, HEAD), 1)
    for g in range(n_groups):
        sl = slice(g * HEAD, (g + 1) * HEAD)
        wm = jnp.where(col <= row, ws_ref[g], 0.0).astype(BF16)
        z_scr[:, sl] = _dot(wm, ln_scr[:, sl]) + bst_ref[:, g:g + 1]
    return u, xhat, rs


def _merge_b(o_refs, lse_refs, yb_scr):
    n_heads = yb_scr.shape[1] // HEAD
    lses = [t[...] for t in lse_refs]
    m = jnp.maximum(jnp.maximum(lses[0], lses[1]), lses[2])
    tot = m + jnp.log(sum(jnp.exp(t - m) for t in lses))
    alphas = [jnp.exp(t - tot) for t in lses]
    for h in range(n_heads):
        sl = slice(h * HEAD, (h + 1) * HEAD)
        yb_scr[:, sl] = sum(a[:, h:h + 1] * o[:, sl].astype(F32) for a, o in zip(alphas, o_refs))
    return tot


def _mix_fwd(proj, outs, lses, w_s, bst, ln_g, ln_b, g_a, g_b, w):
    s = proj.shape[0]
    n_groups = w // HEAD

    def body(au_ref, av_ref, az_ref, bz_ref, o1, o2, o3, l1, l2, l3, ws_ref, bst_ref,
             lng_ref, lnb_ref, ga_ref, gb_ref, y_ref, z_scr, ln_scr, yb_scr):
        u, _, _ = _mixer_a(au_ref[...], av_ref[...], ws_ref, bst_ref, lng_ref[...], lnb_ref[...], z_scr, ln_scr)
        ya = u * z_scr[...]
        silu_a, _ = _silu_and_grad(az_ref[...])
        y_ref[:, :w] = (ya * _rms(ya) * ga_ref[...] * silu_a).astype(BF16)
        _merge_b((o1, o2, o3), (l1, l2, l3), yb_scr)
        yb = yb_scr[...]
        silu_b, _ = _silu_and_grad(bz_ref[...])
        y_ref[:, w:] = (yb * _rms(yb) * gb_ref[...] * silu_b).astype(BF16)

    seg = lambda j: pl.BlockSpec((HEAD, w), lambda i, j=j: (i, j))
    row = pl.BlockSpec((HEAD, w), lambda i: (i, 0))
    stat = pl.BlockSpec((HEAD, HEAD), lambda i: (i, 0))
    vec = pl.BlockSpec((1, w), lambda i: (0, 0))
    return pl.pallas_call(
        body, name="mix_fwd", grid=(s // HEAD,),
        out_shape=jax.ShapeDtypeStruct((s, 2 * w), BF16),
        in_specs=[seg(0), seg(1), seg(2), seg(6), row, row, row, stat, stat, stat,
                  pl.BlockSpec((n_groups, HEAD, HEAD), lambda i: (0, 0, 0)),
                  pl.BlockSpec((HEAD, n_groups), lambda i: (0, 0)), vec, vec, vec, vec],
        out_specs=pl.BlockSpec((HEAD, 2 * w), lambda i: (i, 0)),
        scratch_shapes=[pltpu.VMEM((HEAD, w), F32), pltpu.VMEM((HEAD, w), BF16), pltpu.VMEM((HEAD, w), F32)],
        compiler_params=_params(("arbitrary",), 48),
    )(proj, proj, proj, proj, *outs, *lses, w_s, bst, ln_g, ln_b, g_a, g_b)


def _mix_bwd(proj, dy, outs, lses, w_s, bst, ln_g, ln_b, g_a, g_b, w):
    s = proj.shape[0]
    n_groups = w // HEAD

    def body(au_ref, av_ref, az_ref, bz_ref, dy_ref, o1, o2, o3, l1, l2, l3, ws_ref, bst_ref,
             lng_ref, lnb_ref, ga_ref, gb_ref,
             dproj_ref, dyb_ref, tot_ref, dl_ref, dws_ref, dbst_ref, dlng_ref, dlnb_ref, dga_ref, dgb_ref,
             z_scr, ln_scr, yb_scr, dz_scr, dln_scr):
        i = pl.program_id(0)

        @pl.when(i == 0)
        def _():
            for t in (dws_ref, dbst_ref, dlng_ref, dlnb_ref, dga_ref, dgb_ref):
                t[...] = jnp.zeros_like(t)

        au, av, az = au_ref[...], av_ref[...], az_ref[...]
        lng = lng_ref[...]
        u, xhat, rs = _mixer_a(au, av, ws_ref, bst_ref, lng, lnb_ref[...], z_scr, ln_scr)
        z = z_scr[...]
        ya = u * z
        ra = _rms(ya)
        silu_a, dsilu_a = _silu_and_grad(az)
        dya_all = dy_ref[:, :w]
        na = ya * ra * ga_ref[...]
        dna = dya_all * silu_a
        dproj_ref[:, 2 * w:3 * w] = (dya_all * na * dsilu_a).astype(BF16)
        dga_ref[...] += jnp.sum(dna * ya * ra, axis=0, keepdims=True)
        dya = _rms_bwd(dna, ya, ra, ga_ref[...])
        dproj_ref[:, :w] = (dya * z * _gelu_grad(au)).astype(BF16)
        dz_scr[...] = (dya * u).astype(BF16)

        row = lax.broadcasted_iota(jnp.int32, (HEAD, HEAD), 0)
        col = lax.broadcasted_iota(jnp.int32, (HEAD, HEAD), 1)
        for g in range(n_groups):
            sl = slice(g * HEAD, (g + 1) * HEAD)
            wm = jnp.where(col <= row, ws_ref[g], 0.0).astype(BF16)
            dz = dz_scr[:, sl]
            dln_scr[:, sl] = _dot(wm, dz, TN_DIMS)
            dws_ref[g] += jnp.where(col <= row, _dot(dz, ln_scr[:, sl], NT_DIMS), 0.0)
            dbst_ref[:, g:g + 1] += jnp.sum(dz.astype(F32), axis=-1, keepdims=True)
        dln = dln_scr[...]
        dlng_ref[...] += jnp.sum(dln * xhat, axis=0, keepdims=True)
        dlnb_ref[...] += jnp.sum(dln, axis=0, keepdims=True)
        gy = dln * lng
        dgv = rs * (gy - jnp.mean(gy, axis=-1, keepdims=True)
                    - xhat * jnp.mean(gy * xhat, axis=-1, keepdims=True))
        dproj_ref[:, w:2 * w] = (dgv * _gelu_grad(av)).astype(BF16)
        dproj_ref[:, 3 * w:6 * w] = jnp.zeros((HEAD, 3 * w), BF16)

        tot_ref[...] = _merge_b((o1, o2, o3), (l1, l2, l3), yb_scr)
        yb = yb_scr[...]
        rb = _rms(yb)
        bz = bz_ref[...]
        silu_b, dsilu_b = _silu_and_grad(bz)
        dyb_all = dy_ref[:, w:]
        dnb = dyb_all * silu_b
        dproj_ref[:, 6 * w:] = (dyb_all * yb * rb * gb_ref[...] * dsilu_b).astype(BF16)
        dgb_ref[...] += jnp.sum(dnb * yb * rb, axis=0, keepdims=True)
        dyb = _rms_bwd(dnb, yb, rb, gb_ref[...])
        dyb_ref[...] = dyb.astype(BF16)
        prod = dyb * yb
        dl_ref[...] = _lane_pick(
            [jnp.sum(prod[:, h * HEAD:(h + 1) * HEAD], axis=-1, keepdims=True) for h in range(n_groups)], HEAD)

    seg = lambda j: pl.BlockSpec((HEAD, w), lambda i, j=j: (i, j))
    row_w = pl.BlockSpec((HEAD, w), lambda i: (i, 0))
    stat = pl.BlockSpec((HEAD, HEAD), lambda i: (i, 0))
    vec = pl.BlockSpec((1, w), lambda i: (0, 0))
    ws_spec = pl.BlockSpec((n_groups, HEAD, HEAD), lambda i: (0, 0, 0))
    bst_spec = pl.BlockSpec((HEAD, n_groups), lambda i: (0, 0))
    vec_shape = jax.ShapeDtypeStruct((1, w), F32)
    return pl.pallas_call(
        body, name="mix_bwd", grid=(s // HEAD,),
        out_shape=[jax.ShapeDtypeStruct((s, N_SEG * w), BF16), jax.ShapeDtypeStruct((s, w), BF16),
                   jax.ShapeDtypeStruct((s, HEAD), F32), jax.ShapeDtypeStruct((s, HEAD), F32),
                   jax.ShapeDtypeStruct((n_groups, HEAD, HEAD), F32), jax.ShapeDtypeStruct((HEAD, n_groups), F32),
                   vec_shape, vec_shape, vec_shape, vec_shape],
        in_specs=[seg(0), seg(1), seg(2), seg(6), pl.BlockSpec((HEAD, 2 * w), lambda i: (i, 0)),
                  row_w, row_w, row_w, stat, stat, stat, ws_spec, bst_spec, vec, vec, vec, vec],
        out_specs=[pl.BlockSpec((HEAD, N_SEG * w), lambda i: (i, 0)), row_w, stat, stat,
                   ws_spec, bst_spec, vec, vec, vec, vec],
        scratch_shapes=[pltpu.VMEM((HEAD, w), F32), pltpu.VMEM((HEAD, w), BF16), pltpu.VMEM((HEAD, w), F32),
                        pltpu.VMEM((HEAD, w), BF16), pltpu.VMEM((HEAD, w), F32)],
        compiler_params=_params(("arbitrary",), 56),
    )(proj, proj, proj, proj, dy, *outs, *lses, w_s, bst, ln_g, ln_b, g_a, g_b)


def _local_step(x, p, tgt, small, win_g, wout_g, wgate_g, wup_g):
    s, d = x.shape
    n, _, c_in = win_g.shape
    d_in = n * c_in
    w = d_in // N_SEG
    n_heads = w // HEAD
    p_dim, c_up = wup_g.shape[1], wup_g.shape[2]
    assert s % (HEAD * DILATIONS[-1]) == 0 and w % HEAD == 0 and d == n * c_up == 2 * w

    wout_f = wout_g.reshape(2 * w, d)
    wgate_f = wgate_g.reshape(d, d)

    hn = _rmsnorm_fwd(x, small["g_pre"], "pre_norm")
    tm, tk = _tile(s, 1024), _tile(d, 1024)
    tn_in = c_in
    proj = _matmul(
        hn, win_g, name="in_proj", grid=(s // tm, n, d // tk), dims=NN_DIMS,
        a_spec=pl.BlockSpec((tm, tk), lambda i, j, k: (i, k)),
        b_spec=pl.BlockSpec((None, tk, tn_in), lambda i, j, k: (j, k, 0)),
        acc_shape=(tm, tn_in), out_shapes=[jax.ShapeDtypeStruct((s, d_in), F32)],
        out_specs=[pl.BlockSpec((tm, tn_in), lambda i, j, k: (i, j))])[0]

    qn, kn, vb = _qkv_prep(proj, small["g_q"], small["g_k"], w)
    buckets = jnp.stack([_rel_buckets(dil) for dil in DILATIONS])
    bias = _bias_build(small["rel_bias"], buckets, n_heads)
    outs, lses = [], []
    for c, dil in enumerate(DILATIONS):
        o, l = _attn_fwd(qn, kn, vb, bias[c], dil, "attn_fwd_d%d" % dil)
        outs.append(o.reshape(s, w))
        lses.append(l.reshape(s, HEAD))

    bst = small["b_s"].T
    mix_args = (outs, lses, small["w_s"], bst, small["ln_v_g"], small["ln_v_b"], small["g_out_a"], small["g_out_b"], w)
    y = _mix_fwd(proj, *mix_args)

    tn = _tile(d, 1024)
    tk2 = _tile(2 * w, 1024)

    def resid_epilogue(acc, ex, outs_):
        outs_[0][...] = ex[0][...] + acc[...]

    h = _matmul(
        y, wout_f, name="out_proj", grid=(s // tm, d // tn, (2 * w) // tk2), dims=NN_DIMS,
        a_spec=pl.BlockSpec((tm, tk2), lambda i, j, k: (i, k)),
        b_spec=pl.BlockSpec((tk2, tn), lambda i, j, k: (k, j)),
        acc_shape=(tm, tn), out_shapes=[jax.ShapeDtypeStruct((s, d), F32)],
        out_specs=[pl.BlockSpec((tm, tn), lambda i, j, k: (i, j))],
        extra=(x,), extra_specs=(pl.BlockSpec((tm, tn), lambda i, j, k: (i, j)),),
        epilogue=resid_epilogue)[0]

    hn2 = _rmsnorm_fwd(h, small["g_ple"], "ple_norm")

    tmg = _tile(s, 512)

    def ple_epilogue(acc, ex, outs_):
        h_ref, p_ref, wup_ref, tgt_ref = ex
        dout_ref, dpre_ref, dup_ref, loss_ref = outs_
        gate = jax.nn.sigmoid(acc[...])
        up = _dot(p_ref[...].astype(BF16), wup_ref[...])
        err = h_ref[...] + gate * up - tgt_ref[...]
        dout = err * (1.0 / d)
        dout_ref[...] = dout
        dpre_ref[...] = (dout * up * gate * (1.0 - gate)).astype(BF16)
        dup_ref[...] = (dout * gate).astype(BF16)
        part = 0.5 * jnp.sum(err * err) * (1.0 / d)
        rr = lax.broadcasted_iota(jnp.int32, (8, HEAD), 0)
        cc = lax.broadcasted_iota(jnp.int32, (8, HEAD), 1)
        loss_ref[...] = jnp.where((rr == 0) & (cc == 0), part, 0.0)

    tile_ij = pl.BlockSpec((tmg, c_up), lambda i, j, k: (i, j))
    dout, dpre, dup, loss_parts = _matmul(
        hn2, wgate_f, name="ple_gate", grid=(s // tmg, n, d // tk), dims=NN_DIMS,
        a_spec=pl.BlockSpec((tmg, tk), lambda i, j, k: (i, k)),
        b_spec=pl.BlockSpec((tk, c_up), lambda i, j, k: (k, j)),
        acc_shape=(tmg, c_up),
        out_shapes=[jax.ShapeDtypeStruct((s, d), F32), jax.ShapeDtypeStruct((s, d), BF16),
                    jax.ShapeDtypeStruct((s, d), BF16), jax.ShapeDtypeStruct((s // tmg * 8, n * HEAD), F32)],
        out_specs=[tile_ij, tile_ij, tile_ij, pl.BlockSpec((8, HEAD), lambda i, j, k: (i, j))],
        extra=(h, p, wup_g, tgt),
        extra_specs=(tile_ij, pl.BlockSpec((tmg, p_dim), lambda i, j, k: (i, 0)),
                     pl.BlockSpec((None, p_dim, c_up), lambda i, j, k: (j, 0, 0)), tile_ij),
        epilogue=ple_epilogue)
    loss = jnp.sum(loss_parts)

    tks = _tile(s, 1024)
    g_wup = _matmul(
        p, dup, name="grad_w_up", grid=(1, n, s // tks), dims=TN_DIMS,
        a_spec=pl.BlockSpec((tks, p_dim), lambda i, j, k: (k, 0)),
        b_spec=pl.BlockSpec((tks, c_up), lambda i, j, k: (k, j)),
        acc_shape=(p_dim, c_up), out_shapes=[jax.ShapeDtypeStruct((n, p_dim, c_up), BF16)],
        out_specs=[pl.BlockSpec((None, p_dim, c_up), lambda i, j, k: (j, 0, 0))])[0]

    def tn_matmul(a, b, name):
        m_, n_ = a.shape[1], b.shape[1]
        bm, bn = _tile(m_, 1024), _tile(n_, 1024)
        return _matmul(
            a, b, name=name, grid=(m_ // bm, n_ // bn, s // tks), dims=TN_DIMS,
            a_spec=pl.BlockSpec((tks, bm), lambda i, j, k: (k, i)),
            b_spec=pl.BlockSpec((tks, bn), lambda i, j, k: (k, j)),
            acc_shape=(bm, bn), out_shapes=[jax.ShapeDtypeStruct((m_, n_), BF16)],
            out_specs=[pl.BlockSpec((bm, bn), lambda i, j, k: (i, j))])[0]

    def nt_matmul(a, b, name, out_dtype):
        k_, n_ = a.shape[1], b.shape[0]
        bm, bn, bk = _tile(s, 1024), _tile(n_, 1024), _tile(k_, 1024)
        return _matmul(
            a, b, name=name, grid=(s // bm, n_ // bn, k_ // bk), dims=NT_DIMS,
            a_spec=pl.BlockSpec((bm, bk), lambda i, j, k: (i, k)),
            b_spec=pl.BlockSpec((bn, bk), lambda i, j, k: (j, k)),
            acc_shape=(bm, bn), out_shapes=[jax.ShapeDtypeStruct((s, n_), out_dtype)],
            out_specs=[pl.BlockSpec((bm, bn), lambda i, j, k: (i, j))])[0]

    g_wgate = tn_matmul(hn2, dpre, "grad_w_gate").reshape(wgate_g.shape)
    dhn2 = nt_matmul(dpre, wgate_f, "ple_gate_bwd", F32)
    dh, dh_b, dg_ple = _rmsnorm_bwd(dhn2, h, small["g_ple"], dout, "ple_norm_bwd", True)
    g_wout = tn_matmul(y, dh_b, "grad_w_out").reshape(wout_g.shape)
    dy = nt_matmul(dh_b, wout_f, "out_proj_bwd", F32)

    (dproj, dyb, lse_tot, delta, dws, dbst, dlng, dlnb, dga, dgb) = _mix_bwd(proj, dy, *mix_args)

    dqs, dks, dvs, dss = [], [], [], []
    for c, dil in enumerate(DILATIONS):
        dq, dk, dv, ds = _attn_bwd(qn, kn, vb, dyb, lse_tot, delta, bias[c], dil, "attn_bwd_d%d" % dil)
        dqs.append(dq.reshape(s, w))
        dks.append(dk.reshape(s, w))
        dvs.append(dv.reshape(s, w))
        dss.append(ds)
    d_rel = _bias_grad(jnp.stack(dss), buckets, n_heads)
    dproj, dgq, dgk = _qkv_bwd(dproj, proj, dqs, dks, dvs, small["g_q"], small["g_k"], w)

    bm = _tile(d, 1024)
    g_win = _matmul(
        hn, dproj, name="grad_w_in", grid=(d // bm, n, s // tks), dims=TN_DIMS,
        a_spec=pl.BlockSpec((tks, bm), lambda i, j, k: (k, i)),
        b_spec=pl.BlockSpec((tks, c_in), lambda i, j, k: (k, j)),
        acc_shape=(bm, c_in), out_shapes=[jax.ShapeDtypeStruct((n, d, c_in), BF16)],
        out_specs=[pl.BlockSpec((None, bm, c_in), lambda i, j, k: (j, i, 0))])[0]
    dhn = _matmul(
        dproj, win_g, name="in_proj_bwd", grid=(s // tm, d // tn, n), dims=NT_DIMS,
        a_spec=pl.BlockSpec((tm, c_in), lambda i, j, k: (i, k)),
        b_spec=pl.BlockSpec((None, tn, c_in), lambda i, j, k: (k, j, 0)),
        acc_shape=(tm, tn), out_shapes=[jax.ShapeDtypeStruct((s, d), F32)],
        out_specs=[pl.BlockSpec((tm, tn), lambda i, j, k: (i, j))])[0]
    grad_x, dg_pre = _rmsnorm_bwd(dhn, x, small["g_pre"], dh, "pre_norm_bwd", False)

    small_grads = {
        "g_pre": dg_pre, "w_s": dws, "b_s": dbst.T, "ln_v_g": dlng, "ln_v_b": dlnb, "g_q": dgq, "g_k": dgk,
        "rel_bias": d_rel, "g_out_a": dga, "g_out_b": dgb, "g_ple": dg_ple,
    }
    return loss, grad_x, (g_win, g_wout, g_wgate, g_wup), small_grads


SMALL_NAMES = ("g_pre", "w_s", "b_s", "ln_v_g", "ln_v_b", "g_q", "g_k", "rel_bias", "g_out_a", "g_out_b", "g_ple")


def _pack_small(tree):
    parts = []
    for name in SMALL_NAMES:
        flat = tree[name].astype(F32).reshape(-1)
        pad = (-flat.shape[0]) % HEAD
        parts.append(jnp.pad(flat, (0, pad)) if pad else flat)
    slab = jnp.concatenate(parts).reshape(-1, HEAD)
    pad_rows = (-slab.shape[0]) % 8
    return jnp.pad(slab, ((0, pad_rows), (0, 0))) if pad_rows else slab


def _unpack_small(slab, like):
    flat = slab.reshape(-1)
    out, off = {}, 0
    for name in SMALL_NAMES:
        size = like[name].size
        out[name] = flat[off:off + size].reshape(like[name].shape)
        off += size + (-size) % HEAD
    return out


def _peer(k):
    x, y, c = (lax.axis_index(a) for a in AXES)
    bits = ((k >> 2) & 1, (k >> 1) & 1, k & 1)
    px, py, pc = (1 - v if b else v for v, b in zip((x, y, c), bits))
    return (px, py, pc), 4 * px + 2 * py + pc


def _my_index():
    x, y, c = (lax.axis_index(a) for a in AXES)
    return 4 * x + 2 * y + c


def _exchange(arrays, scatter, name):
    n_arr = len(arrays)
    assert len(scatter) == n_arr

    def body(*refs):
        ins, outs = refs[:n_arr], refs[n_arr:2 * n_arr]
        send_sems, recv_sems, local_sems = refs[2 * n_arr:]
        me = _my_index()
        local = []
        for a in range(n_arr):
            src = ins[a].at[me] if scatter[a] else ins[a]
            cp = pltpu.make_async_copy(src, outs[a].at[me], local_sems.at[a])
            cp.start()
            local.append(cp)
        sends = []
        for k in range(1, N_DEV):
            peer, peer_idx = _peer(k)
            for a in range(n_arr):
                src = ins[a].at[peer_idx] if scatter[a] else ins[a]
                cp = pltpu.make_async_remote_copy(
                    src_ref=src, dst_ref=outs[a].at[me],
                    send_sem=send_sems.at[a, k - 1], recv_sem=recv_sems.at[a, k - 1],
                    device_id=peer, device_id_type=pl.DeviceIdType.MESH)
                cp.start()
                sends.append(cp)
        for k in range(1, N_DEV):
            peer, peer_idx = _peer(k)
            for a in range(n_arr):
                src = ins[a].at[me] if scatter[a] else ins[a]
                pltpu.make_async_remote_copy(
                    src_ref=src, dst_ref=outs[a].at[peer_idx],
                    send_sem=send_sems.at[a, k - 1], recv_sem=recv_sems.at[a, k - 1],
                    device_id=peer, device_id_type=pl.DeviceIdType.MESH).wait_recv()
        for cp in sends:
            cp.wait_send()
        for cp in local:
            cp.wait()

    def out_shape(a, flag):
        shard = a.shape[1:] if flag else a.shape
        return jax.ShapeDtypeStruct((N_DEV,) + tuple(shard), a.dtype)

    hbm = pl.BlockSpec(memory_space=pl.ANY)
    return pl.pallas_call(
        body, name=name, out_shape=[out_shape(a, f) for a, f in zip(arrays, scatter)],
        in_specs=[hbm] * n_arr, out_specs=[hbm] * n_arr,
        scratch_shapes=[pltpu.SemaphoreType.DMA((n_arr, N_DEV - 1)), pltpu.SemaphoreType.DMA((n_arr, N_DEV - 1)),
                        pltpu.SemaphoreType.DMA((n_arr,))],
        compiler_params=pltpu.CompilerParams(has_side_effects=True),
    )(*arrays)


def _adamw_math(w, g, m, v):
    m = ADAM_B1 * m + (1.0 - ADAM_B1) * g
    v = ADAM_B2 * v + (1.0 - ADAM_B2) * (g * g)
    m_hat = m / (1.0 - ADAM_B1 ** ADAM_STEP)
    v_hat = v / (1.0 - ADAM_B2 ** ADAM_STEP)
    delta = -ADAM_LR * (m_hat / (jnp.sqrt(v_hat) + ADAM_EPS) + ADAM_WD * w)
    return delta, m, v


def _adamw(parts, w, m, v, name):
    r, c = w.shape
    budget = 280 * 1024
    tr = r if r * c <= budget else 1 << ((budget // c).bit_length() - 1)
    assert r % tr == 0

    def body(p_ref, w_ref, m_ref, v_ref, g_ref, d_ref, nm_ref, nv_ref):
        g = p_ref[0].astype(F32)
        for i in range(1, N_DEV):
            g = g + p_ref[i].astype(F32)
        delta, nm, nv = _adamw_math(w_ref[...], g, m_ref[...], v_ref[...])
        g_ref[...] = g
        d_ref[...] = delta
        nm_ref[...] = nm
        nv_ref[...] = nv

    blk = pl.BlockSpec((tr, c), lambda i: (i, 0))
    shape = jax.ShapeDtypeStruct((r, c), F32)
    return pl.pallas_call(
        body, name=name, grid=(r // tr,), out_shape=[shape] * 4,
        in_specs=[pl.BlockSpec((N_DEV, tr, c), lambda i: (0, i, 0)), blk, blk, blk],
        out_specs=[blk] * 4,
        compiler_params=_params(("arbitrary",), 48),
    )(parts, w, m, v)


def kernel(x, p, g_pre, w_in, w_s, b_s, ln_v_g, ln_v_b, g_q, g_k, rel_bias, g_out_a, g_out_b, w_out, g_ple, w_ple_gate, w_ple_up, loss_target, m_g_pre, m_w_in, m_w_s, m_b_s, m_ln_v_g, m_ln_v_b, m_g_q, m_g_k, m_rel_bias, m_g_out_a, m_g_out_b, m_w_out, m_g_ple, m_w_ple_gate, m_w_ple_up, v_g_pre, v_w_in, v_w_s, v_b_s, v_ln_v_g, v_ln_v_b, v_g_q, v_g_k, v_rel_bias, v_g_out_a, v_g_out_b, v_w_out, v_g_ple, v_w_ple_gate, v_w_ple_up):
    args = dict(locals())
    small = {"g_pre": g_pre, "w_s": w_s[0], "b_s": b_s[0], "ln_v_g": ln_v_g, "ln_v_b": ln_v_b, "g_q": g_q,
             "g_k": g_k, "rel_bias": rel_bias, "g_out_a": g_out_a, "g_out_b": g_out_b, "g_ple": g_ple}
    big_names = ("w_in", "w_out", "w_ple_gate", "w_ple_up")
    big = {k: args[k][0] for k in big_names}

    gathered = _exchange([big[k].astype(BF16) for k in big_names], (False,) * 4, "gather_weights")
    loss, grad_x, big_parts, small_parts = _local_step(x[0], p[0, 0], loss_target[0], small, *gathered)

    landed = _exchange(list(big_parts) + [_pack_small(small_parts)], (True,) * 4 + (False,), "exchange_grads")

    results = {}
    for k, parts in zip(big_names, landed[:4]):
        results[k] = [t[None] for t in _adamw(parts, big[k], args["m_" + k][0], args["v_" + k][0], "adamw_" + k)]
    squeeze = lambda t: {k: (t[k][0] if k in ("w_s", "b_s") else t[k]) for k in SMALL_NAMES}
    small_m = squeeze({k: args["m_" + k] for k in SMALL_NAMES})
    small_v = squeeze({k: args["v_" + k] for k in SMALL_NAMES})
    packed = _adamw(landed[4], _pack_small(small), _pack_small(small_m), _pack_small(small_v), "adamw_small")
    for idx in range(4):
        tree = _unpack_small(packed[idx], small)
        for k in SMALL_NAMES:
            results.setdefault(k, [None] * 4)[idx] = tree[k].reshape(args[k].shape)

    names = ("g_pre", "w_in", "w_s", "b_s", "ln_v_g", "ln_v_b", "g_q", "g_k", "rel_bias", "g_out_a", "g_out_b",
             "w_out", "g_ple", "w_ple_gate", "w_ple_up")
    total = lax.psum(loss, AXES)
    out = [total, grad_x[None]]
    for idx in range(4):
        out += [results[k][idx] for k in names]
    return tuple(out)
```

```python
import functools
import math

import numpy as np
import jax
import jax.numpy as jnp
from jax import lax
from jax.experimental import pallas as pl
from jax.experimental.pallas import tpu as pltpu

F32 = jnp.float32
BF16 = jnp.bfloat16
EPS = 1e-6
NEG_INF = -1e30
HEAD = 128
DILATIONS = (1, 4, 16)
NUM_BUCKETS = 32
MAX_DISTANCE = 2048
N_SEG = 7
ADAM_LR = 0.001
ADAM_B1 = 0.9
ADAM_B2 = 0.999
ADAM_EPS = 1e-08
ADAM_WD = 0.01
ADAM_STEP = 10
AXES = ("x", "y", "c")
N_DEV = 8
MIB = 1 << 20

SUB = 8

CHUNK_ORDER = np.array([16 * (r % SUB) + r // SUB for r in range(HEAD)])
BLOCK_ORDER = {
    1: CHUNK_ORDER,
    4: np.array([32 * (r // 32) + 4 * (r % SUB) + (r // SUB) % 4 for r in range(HEAD)]),
    16: np.arange(HEAD),
}

NT_DIMS = (((1,), (1,)), ((), ()))
TN_DIMS = (((0,), (0,)), ((), ()))
NN_DIMS = (((1,), (0,)), ((), ()))


def _params(semantics, vmem_mib):
    return pltpu.CompilerParams(dimension_semantics=semantics, vmem_limit_bytes=vmem_mib * MIB)


def _gelu(a):
    return 0.5 * a * (1.0 + lax.erf(a * (2.0 ** -0.5)))


def _gelu_grad(a):
    return 0.5 * (1.0 + lax.erf(a * (2.0 ** -0.5))) + a * jnp.exp(-0.5 * a * a) * ((2.0 * math.pi) ** -0.5)


def _silu_and_grad(a):
    s = jax.nn.sigmoid(a)
    return a * s, s * (1.0 + a * (1.0 - s))


def _rms(v):
    return lax.rsqrt(jnp.mean(v * v, axis=-1, keepdims=True) + EPS)


def _rms_bwd(dy, v, r, g):
    gy = dy * g
    return r * gy - v * (r * r * r) * jnp.mean(gy * v, axis=-1, keepdims=True)


def _dot(a, b, dims=NN_DIMS):
    return lax.dot_general(a, b, dims, preferred_element_type=F32)


def _lane_pick(cols, width):
    rows = cols[0].shape[0]
    lane = lax.broadcasted_iota(jnp.int32, (rows, width), 1)
    out = jnp.zeros((rows, width), F32)
    for h, col in enumerate(cols):
        out = jnp.where(lane == h, col, out)
    return out


def _chunk_perm():
    return jnp.asarray(np.eye(HEAD, dtype=np.float32)[CHUNK_ORDER], BF16)


def _unpermute_f32(p, v):
    hi = v.astype(BF16)
    rest = v - hi.astype(F32)
    mid = rest.astype(BF16)
    lo = (rest - mid.astype(F32)).astype(BF16)
    return _dot(p, hi, TN_DIMS) + _dot(p, mid, TN_DIMS) + _dot(p, lo, TN_DIMS)


def _rmsnorm_fwd(x, g, name, permute):
    s, d = x.shape
    tm = HEAD
    perm = _chunk_perm()

    def body(x_ref, g_ref, p_ref, o_ref):
        v = x_ref[...]
        out = (v * _rms(v) * g_ref[...]).astype(BF16)
        if permute:
            out = _dot(p_ref[...], out).astype(BF16)
        o_ref[...] = out

    return pl.pallas_call(
        body, name=name, grid=(s // tm,),
        out_shape=jax.ShapeDtypeStruct((s, d), BF16),
        in_specs=[pl.BlockSpec((tm, d), lambda i: (i, 0)), pl.BlockSpec((1, d), lambda i: (0, 0)),
                  pl.BlockSpec((HEAD, HEAD), lambda i: (0, 0))],
        out_specs=pl.BlockSpec((tm, d), lambda i: (i, 0)),
        compiler_params=_params(("arbitrary",), 40),
    )(x, g, perm)


def _rmsnorm_bwd(dy, v, g, res, name, dy_permuted, with_bf16):
    s, d = v.shape
    tm = HEAD
    perm = _chunk_perm()

    def body(dy_ref, v_ref, g_ref, res_ref, p_ref, *outs):
        dx_ref, dg_ref = outs[0], outs[-1]
        i = pl.program_id(0)
        vv, dyv = v_ref[...], dy_ref[...]
        if dy_permuted:
            dyv = _unpermute_f32(p_ref[...], dyv)
        r = _rms(vv)
        dx = res_ref[...] + _rms_bwd(dyv, vv, r, g_ref[...])
        dx_ref[...] = dx
        if with_bf16:
            dxb = dx.astype(BF16)
            outs[1][...] = dxb
            outs[2][...] = _dot(p_ref[...], dxb).astype(BF16)

        @pl.when(i == 0)
        def _():
            dg_ref[...] = jnp.zeros_like(dg_ref)

        dg_ref[...] += jnp.sum(dyv * vv * r, axis=0, keepdims=True)

    row = pl.BlockSpec((tm, d), lambda i: (i, 0))
    vec = pl.BlockSpec((1, d), lambda i: (0, 0))
    shapes = [jax.ShapeDtypeStruct((s, d), F32)]
    specs = [row]
    if with_bf16:
        shapes += [jax.ShapeDtypeStruct((s, d), BF16)] * 2
        specs += [row, row]
    shapes.append(jax.ShapeDtypeStruct((1, d), F32))
    specs.append(vec)
    return pl.pallas_call(
        body, name=name, grid=(s // tm,), out_shape=shapes,
        in_specs=[row, row, vec, row, pl.BlockSpec((HEAD, HEAD), lambda i: (0, 0))], out_specs=specs,
        compiler_params=_params(("arbitrary",), 40),
    )(dy, v, g, res, perm)


def _matmul(a, b, *, name, grid, a_spec, b_spec, dims, acc_shape, out_shapes, out_specs,
            extra=(), extra_specs=(), epilogue=None, vmem_mib=48):
    nk = grid[2]
    n_extra, n_out = len(extra), len(out_shapes)

    def body(*refs):
        a_ref, b_ref = refs[0], refs[1]
        ex = refs[2:2 + n_extra]
        outs = refs[2 + n_extra:2 + n_extra + n_out]
        acc = refs[-1]
        k = pl.program_id(2)

        @pl.when(k == 0)
        def _():
            acc[...] = jnp.zeros_like(acc)

        av = a_ref[...]
        if av.dtype != BF16:
            av = av.astype(BF16)
        acc[...] += _dot(av, b_ref[...], dims)

        @pl.when(k == nk - 1)
        def _():
            if epilogue is None:
                outs[0][...] = acc[...].astype(outs[0].dtype)
            else:
                epilogue(acc, ex, outs)

    return pl.pallas_call(
        body, name=name, grid=grid, out_shape=list(out_shapes),
        in_specs=[a_spec, b_spec, *extra_specs], out_specs=list(out_specs),
        scratch_shapes=[pltpu.VMEM(acc_shape, F32)],
        compiler_params=_params(("parallel", "parallel", "arbitrary"), vmem_mib),
    )(a, b, *extra)


def _tile(n, want):
    t = min(n, want)
    while n % t:
        t //= 2
    return t


def _rel_buckets(dil):
    order = BLOCK_ORDER[dil]
    qi = jnp.asarray(HEAD + order)
    kj = jnp.asarray(np.concatenate([order, HEAD + order]))
    delta = qi[:, None] - kj[None, :]
    band = (delta >= 0) & (delta <= HEAD)
    dist = jnp.clip(delta, 0, None) * dil
    max_exact = NUM_BUCKETS // 2
    dd = jnp.maximum(dist, 1).astype(F32)
    large = max_exact + (jnp.log(dd / max_exact) / math.log(MAX_DISTANCE / max_exact)
                         * (NUM_BUCKETS - max_exact)).astype(jnp.int32)
    large = jnp.minimum(large, NUM_BUCKETS - 1)
    bucket = jnp.where(dist < max_exact, dist, large)
    return jnp.where(band, bucket, -1).astype(jnp.int32)


def _bias_build(rel_bias, buckets, n_heads):
    nd = buckets.shape[0]

    def body(rb_ref, bk_ref, o_ref):
        for c in range(nd):
            def per_head(h, carry, c=c):
                bk = bk_ref[c]
                acc = jnp.where(bk < 0, NEG_INF, 0.0).astype(F32)
                for b in range(NUM_BUCKETS):
                    acc = jnp.where(bk == b, rb_ref[b, h], acc)
                o_ref[c, h] = acc
                return carry

            lax.fori_loop(0, n_heads, per_head, 0)

    return pl.pallas_call(
        body, name="bias_build",
        out_shape=jax.ShapeDtypeStruct((nd, n_heads, HEAD, 2 * HEAD), F32),
        in_specs=[pl.BlockSpec(memory_space=pltpu.SMEM), pl.BlockSpec(memory_space=pltpu.VMEM)],
        out_specs=pl.BlockSpec(memory_space=pltpu.VMEM),
    )(rel_bias, buckets)


def _bias_grad(ds_all, buckets, n_heads):
    nd = buckets.shape[0]

    def body(ds_ref, bk_ref, o_ref):
        def per_pair(i, carry):
            b, h = i // n_heads, i % n_heads
            tot = jnp.zeros((), F32)
            for c in range(nd):
                tot = tot + jnp.sum(jnp.where(bk_ref[c] == b, ds_ref[c, h], 0.0))
            o_ref[b, h] = tot
            return carry

        lax.fori_loop(0, NUM_BUCKETS * n_heads, per_pair, 0)

    return pl.pallas_call(
        body, name="bias_grad",
        out_shape=jax.ShapeDtypeStruct((NUM_BUCKETS, n_heads), F32),
        in_specs=[pl.BlockSpec(memory_space=pltpu.VMEM), pl.BlockSpec(memory_space=pltpu.VMEM)],
        out_specs=pl.BlockSpec(memory_space=pltpu.SMEM),
    )(ds_all, buckets)


def _qkv_prep(proj, g_q, g_k, w):
    s = proj.shape[0]
    n_heads = w // HEAD
    tm = HEAD

    def body(q_ref, k_ref, v_ref, gq_ref, gk_ref, qn_ref, kn_ref, vb_ref):
        gq = gq_ref[...] * (HEAD ** -0.5)
        gk = gk_ref[...]
        for h in range(n_heads):
            sl = slice(h * HEAD, (h + 1) * HEAD)
            q = q_ref[:, sl]
            k = k_ref[:, sl]
            qn_ref[:, sl] = q * _rms(q) * gq
            kn_ref[:, sl] = k * _rms(k) * gk
        vb_ref[...] = v_ref[...]

    seg = lambda j: pl.BlockSpec((tm, w), lambda i, j=j: (i, j))
    vec = pl.BlockSpec((1, HEAD), lambda i: (0, 0))
    out = pl.BlockSpec((tm, w), lambda i: (i, 0))
    return pl.pallas_call(
        body, name="qkv_prep", grid=(s // tm,),
        out_shape=[jax.ShapeDtypeStruct((s, w), F32)] * 3,
        in_specs=[seg(3), seg(4), seg(5), vec, vec], out_specs=[out, out, out],
        compiler_params=_params(("arbitrary",), 40),
    )(proj, proj, proj, g_q, g_k)


class _BlockView:
    def __init__(self, s, dil):
        assert s % (HEAD * dil) == 0 and dil in BLOCK_ORDER
        self.nb = s // (HEAD * dil)
        if dil == 1:
            self.lead, self.block = (s,), (HEAD,)
            self.index = lambda r, n: (n,)
        elif dil == 4:
            self.lead, self.block = (s // 512, 4, 4, 4, SUB), (None, 4, 4, None, SUB)
            self.index = lambda r, n: (n, 0, 0, r, 0)
        else:
            self.lead, self.block = (s // 2048, 16, 16, SUB), (None, 16, None, SUB)
            self.index = lambda r, n: (n, 0, r, 0)

    def view(self, t):
        return t.reshape(self.lead + (t.shape[-1],))

    def spec(self, width, block_of):
        return pl.BlockSpec(self.block + (width,), lambda r, n: self.index(r, block_of(r, n)) + (0,))


def _rows(ref, lanes=slice(None)):
    v = ref[(slice(None),) * (len(ref.shape) - 1) + (lanes,)]
    return v.reshape(HEAD, v.shape[-1])


def _set_rows(ref, lanes, value):
    ref[(slice(None),) * (len(ref.shape) - 1) + (lanes,)] = value.reshape(ref.shape[:-1] + (value.shape[-1],))


def _attn_fwd(qn, kn, vb, bias, dil, name):
    s, w = qn.shape
    n_heads = w // HEAD
    bv = _BlockView(s, dil)

    def body(q_ref, kc_ref, kp_ref, vc_ref, vp_ref, bias_ref, o_ref, lse_ref):
        n = pl.program_id(1)
        lses = []
        for h in range(n_heads):
            sl = slice(h * HEAD, (h + 1) * HEAD)
            q = _rows(q_ref, sl).astype(BF16)
            kp, kc = _rows(kp_ref, sl).astype(BF16), _rows(kc_ref, sl).astype(BF16)
            vp, vc = _rows(vp_ref, sl).astype(BF16), _rows(vc_ref, sl).astype(BF16)
            s_p = _dot(q, kp, NT_DIMS) + bias_ref[h, :, :HEAD]
            s_p = jnp.where(n > 0, s_p, NEG_INF)
            s_c = _dot(q, kc, NT_DIMS) + bias_ref[h, :, HEAD:]
            m = jnp.maximum(jnp.max(s_p, axis=-1, keepdims=True), jnp.max(s_c, axis=-1, keepdims=True))
            e_p = jnp.exp(s_p - m)
            e_c = jnp.exp(s_c - m)
            den = jnp.sum(e_p, axis=-1, keepdims=True) + jnp.sum(e_c, axis=-1, keepdims=True)
            o = _dot(e_p.astype(BF16), vp) + _dot(e_c.astype(BF16), vc)
            _set_rows(o_ref, sl, o / den)
            lses.append(m + jnp.log(den))
        _set_rows(lse_ref, slice(None), _lane_pick(lses, HEAD))

    cur = bv.spec(w, lambda r, n: n)
    prev = bv.spec(w, lambda r, n: jnp.maximum(n - 1, 0))
    o, lse = pl.pallas_call(
        body, name=name, grid=(dil, bv.nb),
        out_shape=[jax.ShapeDtypeStruct(bv.lead + (w,), F32), jax.ShapeDtypeStruct(bv.lead + (HEAD,), F32)],
        in_specs=[cur, cur, prev, cur, prev,
                  pl.BlockSpec((n_heads, HEAD, 2 * HEAD), lambda r, n: (0, 0, 0))],
        out_specs=[cur, bv.spec(HEAD, lambda r, n: n)],
        compiler_params=_params(("arbitrary", "arbitrary"), 48),
    )(bv.view(qn), bv.view(kn), bv.view(kn), bv.view(vb), bv.view(vb), bias)
    return o.reshape(s, w), lse.reshape(s, HEAD)


def _attn_bwd(qn, kn, vb, dyb, lse, delta, bias, dil, name):
    s, w = qn.shape
    n_heads = w // HEAD
    bv = _BlockView(s, dil)
    nb = bv.nb

    def body(q_ref, kc_ref, kp_ref, vc_ref, vp_ref, dy_ref, lse_ref, dl_ref, bias_ref,
             dq_ref, dk_ref, dv_ref, ds_ref, carry_k, carry_v):
        r = pl.program_id(0)
        step = pl.program_id(1)
        blk = nb - 1 - step

        @pl.when((r == 0) & (step == 0))
        def _():
            ds_ref[...] = jnp.zeros_like(ds_ref)

        @pl.when(step == 0)
        def _():
            carry_k[...] = jnp.zeros_like(carry_k)
            carry_v[...] = jnp.zeros_like(carry_v)

        tots = _rows(lse_ref)
        dls = _rows(dl_ref)
        for h in range(n_heads):
            sl = slice(h * HEAD, (h + 1) * HEAD)
            q, dy = _rows(q_ref, sl).astype(BF16), _rows(dy_ref, sl).astype(BF16)
            kp, kc = _rows(kp_ref, sl).astype(BF16), _rows(kc_ref, sl).astype(BF16)
            vp, vc = _rows(vp_ref, sl).astype(BF16), _rows(vc_ref, sl).astype(BF16)
            tot = tots[:, h:h + 1]
            dl = dls[:, h:h + 1]
            s_p = _dot(q, kp, NT_DIMS) + bias_ref[h, :, :HEAD]
            s_p = jnp.where(blk > 0, s_p, NEG_INF)
            s_c = _dot(q, kc, NT_DIMS) + bias_ref[h, :, HEAD:]
            p_p = jnp.exp(s_p - tot)
            p_c = jnp.exp(s_c - tot)
            ds_p = p_p * (_dot(dy, vp, NT_DIMS) - dl)
            ds_c = p_c * (_dot(dy, vc, NT_DIMS) - dl)
            ds_ref[h, :, :HEAD] += ds_p
            ds_ref[h, :, HEAD:] += ds_c
            ds_pb, ds_cb = ds_p.astype(BF16), ds_c.astype(BF16)
            _set_rows(dq_ref, sl, _dot(ds_pb, kp) + _dot(ds_cb, kc))
            _set_rows(dk_ref, sl, _dot(ds_cb, q, TN_DIMS) + carry_k[:, sl])
            carry_k[:, sl] = _dot(ds_pb, q, TN_DIMS)
            _set_rows(dv_ref, sl, _dot(p_c.astype(BF16), dy, TN_DIMS) + carry_v[:, sl])
            carry_v[:, sl] = _dot(p_p.astype(BF16), dy, TN_DIMS)

    cur = bv.spec(w, lambda r, n: nb - 1 - n)
    prev = bv.spec(w, lambda r, n: jnp.maximum(nb - 2 - n, 0))
    stat = bv.spec(HEAD, lambda r, n: nb - 1 - n)
    whole = pl.BlockSpec((n_heads, HEAD, 2 * HEAD), lambda r, n: (0, 0, 0))
    big = jax.ShapeDtypeStruct(bv.lead + (w,), F32)
    dq, dk, dv, ds = pl.pallas_call(
        body, name=name, grid=(dil, nb),
        out_shape=[big, big, big, jax.ShapeDtypeStruct((n_heads, HEAD, 2 * HEAD), F32)],
        in_specs=[cur, cur, prev, cur, prev, cur, stat, stat, whole],
        out_specs=[cur, cur, cur, whole],
        scratch_shapes=[pltpu.VMEM((HEAD, w), F32), pltpu.VMEM((HEAD, w), F32)],
        compiler_params=_params(("arbitrary", "arbitrary"), 56),
    )(bv.view(qn), bv.view(kn), bv.view(kn), bv.view(vb), bv.view(vb), bv.view(dyb), bv.view(lse),
      bv.view(delta), bias)
    return dq.reshape(s, w), dk.reshape(s, w), dv.reshape(s, w), ds


def _qkv_bwd(dproj, proj, dqs, dks, dvs, g_q, g_k, w):
    s = proj.shape[0]
    n_heads = w // HEAD
    tm = HEAD

    def body(dproj_hbm, q_ref, k_ref, gq_ref, gk_ref, *rest):
        dq_refs, dk_refs, dv_refs = rest[0:3], rest[3:6], rest[6:9]
        out_ref, dgq_ref, dgk_ref = rest[9:12]
        i = pl.program_id(0)
        gq = gq_ref[...] * (HEAD ** -0.5)
        gk = gk_ref[...]
        acc_q = jnp.zeros((1, HEAD), F32)
        acc_k = jnp.zeros((1, HEAD), F32)
        for h in range(n_heads):
            sl = slice(h * HEAD, (h + 1) * HEAD)
            q, k = q_ref[:, sl], k_ref[:, sl]
            dqn = sum(t[:, sl].astype(F32) for t in dq_refs)
            dkn = sum(t[:, sl].astype(F32) for t in dk_refs)
            rq, rk = _rms(q), _rms(k)
            out_ref[:, h * HEAD:(h + 1) * HEAD] = _rms_bwd(dqn, q, rq, gq).astype(BF16)
            out_ref[:, w + h * HEAD:w + (h + 1) * HEAD] = _rms_bwd(dkn, k, rk, gk).astype(BF16)
            acc_q += jnp.sum(dqn * q * rq, axis=0, keepdims=True)
            acc_k += jnp.sum(dkn * k * rk, axis=0, keepdims=True)
        out_ref[:, 2 * w:] = sum(t[...].astype(F32) for t in dv_refs).astype(BF16)

        @pl.when(i == 0)
        def _():
            dgq_ref[...] = jnp.zeros_like(dgq_ref)
            dgk_ref[...] = jnp.zeros_like(dgk_ref)

        dgq_ref[...] += acc_q * (HEAD ** -0.5)
        dgk_ref[...] += acc_k

    seg = lambda j: pl.BlockSpec((tm, w), lambda i, j=j: (i, j))
    vec = pl.BlockSpec((1, HEAD), lambda i: (0, 0))
    row = pl.BlockSpec((tm, w), lambda i: (i, 0))
    return pl.pallas_call(
        body, name="qkv_bwd", grid=(s // tm,),
        out_shape=[jax.ShapeDtypeStruct(dproj.shape, BF16),
                   jax.ShapeDtypeStruct((1, HEAD), F32), jax.ShapeDtypeStruct((1, HEAD), F32)],
        in_specs=[pl.BlockSpec(memory_space=pl.ANY), seg(3), seg(4), vec, vec] + [row] * 9,
        out_specs=[pl.BlockSpec((tm, 3 * w), lambda i: (i, 1)), vec, vec],
        input_output_aliases={0: 0},
        compiler_params=_params(("arbitrary",), 48),
    )(dproj, proj, proj, g_q, g_k, *dqs, *dks, *dvs)


def _mixer_a(au, av, ws_ref, bst_ref, lng, lnb, z_scr, ln_scr):
    n_groups = au.shape[1] // HEAD
    u = _gelu(au)
    gv = _gelu(av)
    mu = jnp.mean(gv, axis=-1, keepdims=True)
    xc = gv - mu
    rs = lax.rsqrt(jnp.mean(xc * xc, axis=-1, keepdims=True) + EPS)
    xhat = xc * rs
    ln_scr[...] = (xhat * lng + lnb).astype(BF16)
    causal = _causal_mask()
    for g in range(n_groups):
        sl = slice(g * HEAD, (g + 1) * HEAD)
        wm = jnp.where(causal, ws_ref[g], 0.0).astype(BF16)
        z_scr[:, sl] = _dot(wm, ln_scr[:, sl]) + bst_ref[:, g:g + 1]
    return u, xhat, rs


def _causal_mask():
    token = lambda r: 16 * (r % SUB) + r // SUB
    row = lax.broadcasted_iota(jnp.int32, (HEAD, HEAD), 0)
    col = lax.broadcasted_iota(jnp.int32, (HEAD, HEAD), 1)
    return token(col) <= token(row)


def _merge_b(o_refs, lse_refs, yb_scr):
    n_heads = yb_scr.shape[1] // HEAD
    lses = [t[...] for t in lse_refs]
    m = jnp.maximum(jnp.maximum(lses[0], lses[1]), lses[2])
    tot = m + jnp.log(sum(jnp.exp(t - m) for t in lses))
    alphas = [jnp.exp(t - tot) for t in lses]
    for h in range(n_heads):
        sl = slice(h * HEAD, (h + 1) * HEAD)
        yb_scr[:, sl] = sum(a[:, h:h + 1] * o[:, sl].astype(F32) for a, o in zip(alphas, o_refs))
    return tot


def _mix_fwd(proj, outs, lses, w_s, bst, ln_g, ln_b, g_a, g_b, w):
    s = proj.shape[0]
    n_groups = w // HEAD

    def body(au_ref, av_ref, az_ref, bz_ref, o1, o2, o3, l1, l2, l3, ws_ref, bst_ref,
             lng_ref, lnb_ref, ga_ref, gb_ref, p_ref, y_ref, z_scr, ln_scr, yb_scr):
        u, _, _ = _mixer_a(au_ref[...], av_ref[...], ws_ref, bst_ref, lng_ref[...], lnb_ref[...], z_scr, ln_scr)
        ya = u * z_scr[...]
        silu_a, _ = _silu_and_grad(az_ref[...])
        perm = p_ref[...]
        y_ref[:, :w] = _dot(perm, (ya * _rms(ya) * ga_ref[...] * silu_a).astype(BF16), TN_DIMS).astype(BF16)
        _merge_b((o1, o2, o3), (l1, l2, l3), yb_scr)
        yb = yb_scr[...]
        silu_b, _ = _silu_and_grad(bz_ref[...])
        y_ref[:, w:] = _dot(perm, (yb * _rms(yb) * gb_ref[...] * silu_b).astype(BF16), TN_DIMS).astype(BF16)

    seg = lambda j: pl.BlockSpec((HEAD, w), lambda i, j=j: (i, j))
    row = pl.BlockSpec((HEAD, w), lambda i: (i, 0))
    stat = pl.BlockSpec((HEAD, HEAD), lambda i: (i, 0))
    vec = pl.BlockSpec((1, w), lambda i: (0, 0))
    return pl.pallas_call(
        body, name="mix_fwd", grid=(s // HEAD,),
        out_shape=jax.ShapeDtypeStruct((s, 2 * w), BF16),
        in_specs=[seg(0), seg(1), seg(2), seg(6), row, row, row, stat, stat, stat,
                  pl.BlockSpec((n_groups, HEAD, HEAD), lambda i: (0, 0, 0)),
                  pl.BlockSpec((HEAD, n_groups), lambda i: (0, 0)), vec, vec, vec, vec,
                  pl.BlockSpec((HEAD, HEAD), lambda i: (0, 0))],
        out_specs=pl.BlockSpec((HEAD, 2 * w), lambda i: (i, 0)),
        scratch_shapes=[pltpu.VMEM((HEAD, w), F32), pltpu.VMEM((HEAD, w), BF16), pltpu.VMEM((HEAD, w), F32)],
        compiler_params=_params(("arbitrary",), 48),
    )(proj, proj, proj, proj, *outs, *lses, w_s, bst, ln_g, ln_b, g_a, g_b, _chunk_perm())


def _mix_bwd(proj, dy, outs, lses, w_s, bst, ln_g, ln_b, g_a, g_b, w):
    s = proj.shape[0]
    n_groups = w // HEAD

    def body(au_ref, av_ref, az_ref, bz_ref, dy_ref, o1, o2, o3, l1, l2, l3, ws_ref, bst_ref,
             lng_ref, lnb_ref, ga_ref, gb_ref,
             dproj_ref, dyb_ref, tot_ref, dl_ref, dws_ref, dbst_ref, dlng_ref, dlnb_ref, dga_ref, dgb_ref,
             z_scr, ln_scr, yb_scr, dz_scr, dln_scr):
        i = pl.program_id(0)

        @pl.when(i == 0)
        def _():
            for t in (dws_ref, dbst_ref, dlng_ref, dlnb_ref, dga_ref, dgb_ref):
                t[...] = jnp.zeros_like(t)

        au, av, az = au_ref[...], av_ref[...], az_ref[...]
        lng = lng_ref[...]
        u, xhat, rs = _mixer_a(au, av, ws_ref, bst_ref, lng, lnb_ref[...], z_scr, ln_scr)
        z = z_scr[...]
        ya = u * z
        ra = _rms(ya)
        silu_a, dsilu_a = _silu_and_grad(az)
        dya_all = dy_ref[:, :w]
        na = ya * ra * ga_ref[...]
        dna = dya_all * silu_a
        dproj_ref[:, 2 * w:3 * w] = (dya_all * na * dsilu_a).astype(BF16)
        dga_ref[...] += jnp.sum(dna * ya * ra, axis=0, keepdims=True)
        dya = _rms_bwd(dna, ya, ra, ga_ref[...])
        dproj_ref[:, :w] = (dya * z * _gelu_grad(au)).astype(BF16)
        dz_scr[...] = (dya * u).astype(BF16)

        causal = _causal_mask()
        for g in range(n_groups):
            sl = slice(g * HEAD, (g + 1) * HEAD)
            wm = jnp.where(causal, ws_ref[g], 0.0).astype(BF16)
            dz = dz_scr[:, sl]
            dln_scr[:, sl] = _dot(wm, dz, TN_DIMS)
            dws_ref[g] += jnp.where(causal, _dot(dz, ln_scr[:, sl], NT_DIMS), 0.0)
            dbst_ref[:, g:g + 1] += jnp.sum(dz.astype(F32), axis=-1, keepdims=True)
        dln = dln_scr[...]
        dlng_ref[...] += jnp.sum(dln * xhat, axis=0, keepdims=True)
        dlnb_ref[...] += jnp.sum(dln, axis=0, keepdims=True)
        gy = dln * lng
        dgv = rs * (gy - jnp.mean(gy, axis=-1, keepdims=True)
                    - xhat * jnp.mean(gy * xhat, axis=-1, keepdims=True))
        dproj_ref[:, w:2 * w] = (dgv * _gelu_grad(av)).astype(BF16)
        dproj_ref[:, 3 * w:6 * w] = jnp.zeros((HEAD, 3 * w), BF16)

        tot_ref[...] = _merge_b((o1, o2, o3), (l1, l2, l3), yb_scr)
        yb = yb_scr[...]
        rb = _rms(yb)
        bz = bz_ref[...]
        silu_b, dsilu_b = _silu_and_grad(bz)
        dyb_all = dy_ref[:, w:]
        dnb = dyb_all * silu_b
        dproj_ref[:, 6 * w:] = (dyb_all * yb * rb * gb_ref[...] * dsilu_b).astype(BF16)
        dgb_ref[...] += jnp.sum(dnb * yb * rb, axis=0, keepdims=True)
        dyb = _rms_bwd(dnb, yb, rb, gb_ref[...])
        dyb_ref[...] = dyb
        prod = dyb * yb
        dl_ref[...] = _lane_pick(
            [jnp.sum(prod[:, h * HEAD:(h + 1) * HEAD], axis=-1, keepdims=True) for h in range(n_groups)], HEAD)

    seg = lambda j: pl.BlockSpec((HEAD, w), lambda i, j=j: (i, j))
    row_w = pl.BlockSpec((HEAD, w), lambda i: (i, 0))
    stat = pl.BlockSpec((HEAD, HEAD), lambda i: (i, 0))
    vec = pl.BlockSpec((1, w), lambda i: (0, 0))
    ws_spec = pl.BlockSpec((n_groups, HEAD, HEAD), lambda i: (0, 0, 0))
    bst_spec = pl.BlockSpec((HEAD, n_groups), lambda i: (0, 0))
    vec_shape = jax.ShapeDtypeStruct((1, w), F32)
    return pl.pallas_call(
        body, name="mix_bwd", grid=(s // HEAD,),
        out_shape=[jax.ShapeDtypeStruct((s, N_SEG * w), BF16), jax.ShapeDtypeStruct((s, w), F32),
                   jax.ShapeDtypeStruct((s, HEAD), F32), jax.ShapeDtypeStruct((s, HEAD), F32),
                   jax.ShapeDtypeStruct((n_groups, HEAD, HEAD), F32), jax.ShapeDtypeStruct((HEAD, n_groups), F32),
                   vec_shape, vec_shape, vec_shape, vec_shape],
        in_specs=[seg(0), seg(1), seg(2), seg(6), pl.BlockSpec((HEAD, 2 * w), lambda i: (i, 0)),
                  row_w, row_w, row_w, stat, stat, stat, ws_spec, bst_spec, vec, vec, vec, vec],
        out_specs=[pl.BlockSpec((HEAD, N_SEG * w), lambda i: (i, 0)), row_w, stat, stat,
                   ws_spec, bst_spec, vec, vec, vec, vec],
        scratch_shapes=[pltpu.VMEM((HEAD, w), F32), pltpu.VMEM((HEAD, w), BF16), pltpu.VMEM((HEAD, w), F32),
                        pltpu.VMEM((HEAD, w), BF16), pltpu.VMEM((HEAD, w), F32)],
        compiler_params=_params(("arbitrary",), 56),
    )(proj, proj, proj, proj, dy, *outs, *lses, w_s, bst, ln_g, ln_b, g_a, g_b)


def _local_step(x, p, tgt, small, win_g, wout_g, wgate_g, wup_g):
    s, d = x.shape
    n, _, c_in = win_g.shape
    d_in = n * c_in
    w = d_in // N_SEG
    n_heads = w // HEAD
    p_dim, c_up = wup_g.shape[1], wup_g.shape[2]
    assert s % (HEAD * DILATIONS[-1]) == 0 and w % HEAD == 0 and d == n * c_up == 2 * w

    wout_f = wout_g.reshape(2 * w, d)
    wgate_f = wgate_g.reshape(d, d)

    hn = _rmsnorm_fwd(x, small["g_pre"], "pre_norm", True)
    tm, tk = _tile(s, 1024), _tile(d, 1024)
    tn_in = c_in
    proj = _matmul(
        hn, win_g, name="in_proj", grid=(s // tm, n, d // tk), dims=NN_DIMS,
        a_spec=pl.BlockSpec((tm, tk), lambda i, j, k: (i, k)),
        b_spec=pl.BlockSpec((None, tk, tn_in), lambda i, j, k: (j, k, 0)),
        acc_shape=(tm, tn_in), out_shapes=[jax.ShapeDtypeStruct((s, d_in), F32)],
        out_specs=[pl.BlockSpec((tm, tn_in), lambda i, j, k: (i, j))])[0]

    qn, kn, vb = _qkv_prep(proj, small["g_q"], small["g_k"], w)
    buckets = jnp.stack([_rel_buckets(dil) for dil in DILATIONS])
    bias = _bias_build(small["rel_bias"], buckets, n_heads)
    outs, lses = [], []
    for c, dil in enumerate(DILATIONS):
        o, l = _attn_fwd(qn, kn, vb, bias[c], dil, "attn_fwd_d%d" % dil)
        outs.append(o)
        lses.append(l)

    ws_p = small["w_s"][:, CHUNK_ORDER][:, :, CHUNK_ORDER]
    bst = small["b_s"].T[CHUNK_ORDER]
    mix_args = (outs, lses, ws_p, bst, small["ln_v_g"], small["ln_v_b"], small["g_out_a"], small["g_out_b"], w)
    y = _mix_fwd(proj, *mix_args)

    tn = _tile(d, 1024)
    tk2 = _tile(2 * w, 1024)

    def resid_epilogue(acc, ex, outs_):
        outs_[0][...] = ex[0][...] + acc[...]

    h = _matmul(
        y, wout_f, name="out_proj", grid=(s // tm, d // tn, (2 * w) // tk2), dims=NN_DIMS,
        a_spec=pl.BlockSpec((tm, tk2), lambda i, j, k: (i, k)),
        b_spec=pl.BlockSpec((tk2, tn), lambda i, j, k: (k, j)),
        acc_shape=(tm, tn), out_shapes=[jax.ShapeDtypeStruct((s, d), F32)],
        out_specs=[pl.BlockSpec((tm, tn), lambda i, j, k: (i, j))],
        extra=(x,), extra_specs=(pl.BlockSpec((tm, tn), lambda i, j, k: (i, j)),),
        epilogue=resid_epilogue)[0]

    hn2 = _rmsnorm_fwd(h, small["g_ple"], "ple_norm", False)

    tmg = _tile(s, 512)

    def ple_epilogue(acc, ex, outs_):
        h_ref, p_ref, wup_ref, tgt_ref = ex
        dout_ref, dpre_ref, dup_ref, loss_ref = outs_
        gate = jax.nn.sigmoid(acc[...])
        up = _dot(p_ref[...].astype(BF16), wup_ref[...])
        err = h_ref[...] + gate * up - tgt_ref[...]
        dout = err * (1.0 / d)
        dout_ref[...] = dout
        dpre_ref[...] = (dout * up * gate * (1.0 - gate)).astype(BF16)
        dup_ref[...] = (dout * gate).astype(BF16)
        part = 0.5 * jnp.sum(err * err) * (1.0 / d)
        rr = lax.broadcasted_iota(jnp.int32, (8, HEAD), 0)
        cc = lax.broadcasted_iota(jnp.int32, (8, HEAD), 1)
        loss_ref[...] = jnp.where((rr == 0) & (cc == 0), part, 0.0)

    tile_ij = pl.BlockSpec((tmg, c_up), lambda i, j, k: (i, j))
    dout, dpre, dup, loss_parts = _matmul(
        hn2, wgate_f, name="ple_gate", grid=(s // tmg, n, d // tk), dims=NN_DIMS,
        a_spec=pl.BlockSpec((tmg, tk), lambda i, j, k: (i, k)),
        b_spec=pl.BlockSpec((tk, c_up), lambda i, j, k: (k, j)),
        acc_shape=(tmg, c_up),
        out_shapes=[jax.ShapeDtypeStruct((s, d), F32), jax.ShapeDtypeStruct((s, d), BF16),
                    jax.ShapeDtypeStruct((s, d), BF16), jax.ShapeDtypeStruct((s // tmg * 8, n * HEAD), F32)],
        out_specs=[tile_ij, tile_ij, tile_ij, pl.BlockSpec((8, HEAD), lambda i, j, k: (i, j))],
        extra=(h, p, wup_g, tgt),
        extra_specs=(tile_ij, pl.BlockSpec((tmg, p_dim), lambda i, j, k: (i, 0)),
                     pl.BlockSpec((None, p_dim, c_up), lambda i, j, k: (j, 0, 0)), tile_ij),
        epilogue=ple_epilogue)
    loss = jnp.sum(loss_parts)

    tks = _tile(s, 1024)
    g_wup = _matmul(
        p, dup, name="grad_w_up", grid=(1, n, s // tks), dims=TN_DIMS,
        a_spec=pl.BlockSpec((tks, p_dim), lambda i, j, k: (k, 0)),
        b_spec=pl.BlockSpec((tks, c_up), lambda i, j, k: (k, j)),
        acc_shape=(p_dim, c_up), out_shapes=[jax.ShapeDtypeStruct((n, p_dim, c_up), BF16)],
        out_specs=[pl.BlockSpec((None, p_dim, c_up), lambda i, j, k: (j, 0, 0))])[0]

    def tn_matmul(a, b, name):
        m_, n_ = a.shape[1], b.shape[1]
        bm, bn = _tile(m_, 1024), _tile(n_, 1024)
        return _matmul(
            a, b, name=name, grid=(m_ // bm, n_ // bn, s // tks), dims=TN_DIMS,
            a_spec=pl.BlockSpec((tks, bm), lambda i, j, k: (k, i)),
            b_spec=pl.BlockSpec((tks, bn), lambda i, j, k: (k, j)),
            acc_shape=(bm, bn), out_shapes=[jax.ShapeDtypeStruct((m_, n_), BF16)],
            out_specs=[pl.BlockSpec((bm, bn), lambda i, j, k: (i, j))])[0]

    def nt_matmul(a, b, name, out_dtype):
        k_, n_ = a.shape[1], b.shape[0]
        bm, bn, bk = _tile(s, 1024), _tile(n_, 1024), _tile(k_, 1024)
        return _matmul(
            a, b, name=name, grid=(s // bm, n_ // bn, k_ // bk), dims=NT_DIMS,
            a_spec=pl.BlockSpec((bm, bk), lambda i, j, k: (i, k)),
            b_spec=pl.BlockSpec((bn, bk), lambda i, j, k: (j, k)),
            acc_shape=(bm, bn), out_shapes=[jax.ShapeDtypeStruct((s, n_), out_dtype)],
            out_specs=[pl.BlockSpec((bm, bn), lambda i, j, k: (i, j))])[0]

    g_wgate = tn_matmul(hn2, dpre, "grad_w_gate").reshape(wgate_g.shape)
    dhn2 = nt_matmul(dpre, wgate_f, "ple_gate_bwd", F32)
    dh, dh_b, dh_bp, dg_ple = _rmsnorm_bwd(dhn2, h, small["g_ple"], dout, "ple_norm_bwd", False, True)
    g_wout = tn_matmul(y, dh_b, "grad_w_out").reshape(wout_g.shape)
    dy = nt_matmul(dh_bp, wout_f, "out_proj_bwd", F32)

    (dproj, dyb, lse_tot, delta, dws, dbst, dlng, dlnb, dga, dgb) = _mix_bwd(proj, dy, *mix_args)

    dqs, dks, dvs, dss = [], [], [], []
    for c, dil in enumerate(DILATIONS):
        dq, dk, dv, ds = _attn_bwd(qn, kn, vb, dyb, lse_tot, delta, bias[c], dil, "attn_bwd_d%d" % dil)
        dqs.append(dq)
        dks.append(dk)
        dvs.append(dv)
        dss.append(ds)
    d_rel = _bias_grad(jnp.stack(dss), buckets, n_heads)
    dproj, dgq, dgk = _qkv_bwd(dproj, proj, dqs, dks, dvs, small["g_q"], small["g_k"], w)

    bm = _tile(d, 1024)
    g_win = _matmul(
        hn, dproj, name="grad_w_in", grid=(d // bm, n, s // tks), dims=TN_DIMS,
        a_spec=pl.BlockSpec((tks, bm), lambda i, j, k: (k, i)),
        b_spec=pl.BlockSpec((tks, c_in), lambda i, j, k: (k, j)),
        acc_shape=(bm, c_in), out_shapes=[jax.ShapeDtypeStruct((n, d, c_in), BF16)],
        out_specs=[pl.BlockSpec((None, bm, c_in), lambda i, j, k: (j, i, 0))])[0]
    dhn = _matmul(
        dproj, win_g, name="in_proj_bwd", grid=(s // tm, d // tn, n), dims=NT_DIMS,
        a_spec=pl.BlockSpec((tm, c_in), lambda i, j, k: (i, k)),
        b_spec=pl.BlockSpec((None, tn, c_in), lambda i, j, k: (k, j, 0)),
        acc_shape=(tm, tn), out_shapes=[jax.ShapeDtypeStruct((s, d), F32)],
        out_specs=[pl.BlockSpec((tm, tn), lambda i, j, k: (i, j))])[0]
    grad_x, dg_pre = _rmsnorm_bwd(dhn, x, small["g_pre"], dh, "pre_norm_bwd", True, False)

    token_row = np.argsort(CHUNK_ORDER)
    dws = dws[:, token_row][:, :, token_row]
    dbst = dbst[token_row]
    small_grads = {
        "g_pre": dg_pre, "w_s": dws, "b_s": dbst.T, "ln_v_g": dlng, "ln_v_b": dlnb, "g_q": dgq, "g_k": dgk,
        "rel_bias": d_rel, "g_out_a": dga, "g_out_b": dgb, "g_ple": dg_ple,
    }
    return loss, grad_x, (g_win, g_wout, g_wgate, g_wup), small_grads


SMALL_NAMES = ("g_pre", "w_s", "b_s", "ln_v_g", "ln_v_b", "g_q", "g_k", "rel_bias", "g_out_a", "g_out_b", "g_ple")


def _pack_small(tree):
    parts = []
    for name in SMALL_NAMES:
        flat = tree[name].astype(F32).reshape(-1)
        pad = (-flat.shape[0]) % HEAD
        parts.append(jnp.pad(flat, (0, pad)) if pad else flat)
    slab = jnp.concatenate(parts).reshape(-1, HEAD)
    pad_rows = (-slab.shape[0]) % 8
    return jnp.pad(slab, ((0, pad_rows), (0, 0))) if pad_rows else slab


def _unpack_small(slab, like):
    flat = slab.reshape(-1)
    out, off = {}, 0
    for name in SMALL_NAMES:
        size = like[name].size
        out[name] = flat[off:off + size].reshape(like[name].shape)
        off += size + (-size) % HEAD
    return out


def _peer(k):
    x, y, c = (lax.axis_index(a) for a in AXES)
    bits = ((k >> 2) & 1, (k >> 1) & 1, k & 1)
    px, py, pc = (1 - v if b else v for v, b in zip((x, y, c), bits))
    return (px, py, pc), 4 * px + 2 * py + pc


def _my_index():
    x, y, c = (lax.axis_index(a) for a in AXES)
    return 4 * x + 2 * y + c


N_CHIP = 4
HBM_SPEC = pl.BlockSpec(memory_space=pl.ANY)


def _remote(src, dst, send_sem, recv_sem, peer):
    return pltpu.make_async_remote_copy(src_ref=src, dst_ref=dst, send_sem=send_sem, recv_sem=recv_sem,
                                        device_id=peer, device_id_type=pl.DeviceIdType.MESH)


def _gather_weights(shards):
    n_arr = len(shards)

    def body(*refs):
        ins, outs = refs[:n_arr], refs[n_arr:2 * n_arr]
        send_sems, recv_sems, local_sems = refs[2 * n_arr:]
        me = _my_index()
        sibling, sib_idx = _peer(1)
        chips = [_peer(2), _peer(4), _peer(6)]
        local = [pltpu.make_async_copy(ins[a], outs[a].at[me], local_sems.at[a]) for a in range(n_arr)]
        for cp in local:
            cp.start()
        sends = []
        for j, (chip, _) in enumerate(chips):
            for a in range(n_arr):
                sends.append(_remote(ins[a], outs[a].at[me], send_sems.at[a, 1 + j], recv_sems.at[a, 1 + j], chip))
        for a in range(n_arr):
            sends.append(_remote(ins[a], outs[a].at[me], send_sems.at[a, 0], recv_sems.at[a, 0], sibling))
        for cp in sends:
            cp.start()
        for j, (chip, chip_idx) in enumerate(chips):
            for a in range(n_arr):
                slot = outs[a].at[chip_idx]
                _remote(slot, slot, send_sems.at[a, 1 + j], recv_sems.at[a, 1 + j], chip).wait_recv()
                fwd = _remote(slot, slot, send_sems.at[a, 4 + j], recv_sems.at[a, 4 + j], sibling)
                fwd.start()
                sends.append(fwd)
        for a in range(n_arr):
            slot = outs[a].at[sib_idx]
            _remote(slot, slot, send_sems.at[a, 0], recv_sems.at[a, 0], sibling).wait_recv()
        for j, (chip, chip_idx) in enumerate(chips):
            for a in range(n_arr):
                slot = outs[a].at[chip_idx ^ 1]
                _remote(slot, slot, send_sems.at[a, 4 + j], recv_sems.at[a, 4 + j], sibling).wait_recv()
        for cp in sends:
            cp.wait_send()
        for cp in local:
            cp.wait()

    return pl.pallas_call(
        body, name="gather_weights",
        out_shape=[jax.ShapeDtypeStruct((N_DEV,) + a.shape, a.dtype) for a in shards],
        in_specs=[HBM_SPEC] * n_arr, out_specs=[HBM_SPEC] * n_arr,
        scratch_shapes=[pltpu.SemaphoreType.DMA((n_arr, N_DEV - 1)), pltpu.SemaphoreType.DMA((n_arr, N_DEV - 1)),
                        pltpu.SemaphoreType.DMA((n_arr,))],
        compiler_params=pltpu.CompilerParams(has_side_effects=True),
    )(*shards)


def _pair_exchange(parts):
    n_arr = len(parts)

    def body(*refs):
        ins, outs = refs[:n_arr], refs[n_arr:2 * n_arr]
        send_sems, recv_sems = refs[2 * n_arr:]
        c = lax.axis_index("c")
        sibling, _ = _peer(1)
        copies = []
        for a in range(n_arr):
            for chip in range(N_CHIP):
                copies.append(_remote(ins[a].at[2 * chip + 1 - c], outs[a].at[chip],
                                      send_sems.at[a, chip], recv_sems.at[a, chip], sibling))
        for cp in copies:
            cp.start()
        for cp in copies:
            cp.wait()

    return pl.pallas_call(
        body, name="pair_exchange",
        out_shape=[jax.ShapeDtypeStruct((N_CHIP,) + a.shape[1:], a.dtype) for a in parts],
        in_specs=[HBM_SPEC] * n_arr, out_specs=[HBM_SPEC] * n_arr,
        scratch_shapes=[pltpu.SemaphoreType.DMA((n_arr, N_CHIP)), pltpu.SemaphoreType.DMA((n_arr, N_CHIP))],
        compiler_params=pltpu.CompilerParams(has_side_effects=True),
    )(*parts)


def _pair_add(parts, landed, name):
    _, r, c_dim = parts.shape
    tr = r if r * c_dim <= MIB else 1 << ((MIB // c_dim).bit_length() - 1)
    assert r % tr == 0
    core = lax.axis_index("c").astype(jnp.int32).reshape(1)

    def body(core_ref, a_ref, b_ref, o_ref):
        o_ref[...] = (a_ref[...].astype(F32) + b_ref[...].astype(F32)).astype(BF16)

    blk = (None, tr, c_dim)
    return pl.pallas_call(
        body, name=name, out_shape=jax.ShapeDtypeStruct(landed.shape, BF16),
        grid_spec=pltpu.PrefetchScalarGridSpec(
            num_scalar_prefetch=1, grid=(N_CHIP, r // tr),
            in_specs=[pl.BlockSpec(blk, lambda ch, i, core_ref: (2 * ch + core_ref[0], i, 0)),
                      pl.BlockSpec(blk, lambda ch, i, core_ref: (ch, i, 0))],
            out_specs=pl.BlockSpec(blk, lambda ch, i, core_ref: (ch, i, 0))),
        compiler_params=_params(("arbitrary", "arbitrary"), 40),
    )(core, parts, landed)


def _chip_exchange(sums, slab):
    n_arr = len(sums)

    def body(*refs):
        ins, slab_in = refs[:n_arr], refs[n_arr]
        outs, slab_out = refs[n_arr + 1:2 * n_arr + 1], refs[2 * n_arr + 1]
        send_sems, recv_sems, local_sems, slab_send, slab_recv = refs[2 * n_arr + 2:]
        x, y = lax.axis_index("x"), lax.axis_index("y")
        me = _my_index()
        my_chip = 2 * x + y
        local = [pltpu.make_async_copy(ins[a].at[my_chip], outs[a].at[my_chip], local_sems.at[a])
                 for a in range(n_arr)]
        local.append(pltpu.make_async_copy(slab_in, slab_out.at[me], local_sems.at[n_arr]))
        for cp in local:
            cp.start()
        sends = []
        for k in range(1, N_DEV):
            peer, _ = _peer(k)
            sends.append(_remote(slab_in, slab_out.at[me], slab_send.at[k - 1], slab_recv.at[k - 1], peer))
        for j, k in enumerate((2, 4, 6)):
            peer, peer_idx = _peer(k)
            for a in range(n_arr):
                sends.append(_remote(ins[a].at[peer_idx // 2], outs[a].at[my_chip],
                                     send_sems.at[a, j], recv_sems.at[a, j], peer))
        for cp in sends:
            cp.start()
        for k in range(1, N_DEV):
            peer, peer_idx = _peer(k)
            slot = slab_out.at[peer_idx]
            _remote(slot, slot, slab_send.at[k - 1], slab_recv.at[k - 1], peer).wait_recv()
        for j, k in enumerate((2, 4, 6)):
            peer, peer_idx = _peer(k)
            for a in range(n_arr):
                slot = outs[a].at[peer_idx // 2]
                _remote(slot, slot, send_sems.at[a, j], recv_sems.at[a, j], peer).wait_recv()
        for cp in sends:
            cp.wait_send()
        for cp in local:
            cp.wait()

    return pl.pallas_call(
        body, name="chip_exchange",
        out_shape=[jax.ShapeDtypeStruct(a.shape, a.dtype) for a in sums]
        + [jax.ShapeDtypeStruct((N_DEV,) + slab.shape, slab.dtype)],
        in_specs=[HBM_SPEC] * (n_arr + 1), out_specs=[HBM_SPEC] * (n_arr + 1),
        scratch_shapes=[pltpu.SemaphoreType.DMA((n_arr, N_CHIP - 1)), pltpu.SemaphoreType.DMA((n_arr, N_CHIP - 1)),
                        pltpu.SemaphoreType.DMA((n_arr + 1,)),
                        pltpu.SemaphoreType.DMA((N_DEV - 1,)), pltpu.SemaphoreType.DMA((N_DEV - 1,))],
        compiler_params=pltpu.CompilerParams(has_side_effects=True),
    )(*sums, slab)


def _adamw_math(w, g, m, v):
    m = ADAM_B1 * m + (1.0 - ADAM_B1) * g
    v = ADAM_B2 * v + (1.0 - ADAM_B2) * (g * g)
    m_hat = m / (1.0 - ADAM_B1 ** ADAM_STEP)
    v_hat = v / (1.0 - ADAM_B2 ** ADAM_STEP)
    delta = -ADAM_LR * (m_hat / (jnp.sqrt(v_hat) + ADAM_EPS) + ADAM_WD * w)
    return delta, m, v


def _adamw(parts, w, m, v, name):
    n_parts = parts.shape[0]
    r, c = w.shape
    budget = 280 * 1024
    tr = r if r * c <= budget else 1 << ((budget // c).bit_length() - 1)
    assert r % tr == 0

    def body(p_ref, w_ref, m_ref, v_ref, g_ref, d_ref, nm_ref, nv_ref):
        g = p_ref[0].astype(F32)
        for i in range(1, n_parts):
            g = g + p_ref[i].astype(F32)
        delta, nm, nv = _adamw_math(w_ref[...], g, m_ref[...], v_ref[...])
        g_ref[...] = g
        d_ref[...] = delta
        nm_ref[...] = nm
        nv_ref[...] = nv

    blk = pl.BlockSpec((tr, c), lambda i: (i, 0))
    shape = jax.ShapeDtypeStruct((r, c), F32)
    return pl.pallas_call(
        body, name=name, grid=(r // tr,), out_shape=[shape] * 4,
        in_specs=[pl.BlockSpec((n_parts, tr, c), lambda i: (0, i, 0)), blk, blk, blk],
        out_specs=[blk] * 4,
        compiler_params=_params(("arbitrary",), 48),
    )(parts, w, m, v)


def kernel(x, p, g_pre, w_in, w_s, b_s, ln_v_g, ln_v_b, g_q, g_k, rel_bias, g_out_a, g_out_b, w_out, g_ple, w_ple_gate, w_ple_up, loss_target, m_g_pre, m_w_in, m_w_s, m_b_s, m_ln_v_g, m_ln_v_b, m_g_q, m_g_k, m_rel_bias, m_g_out_a, m_g_out_b, m_w_out, m_g_ple, m_w_ple_gate, m_w_ple_up, v_g_pre, v_w_in, v_w_s, v_b_s, v_ln_v_g, v_ln_v_b, v_g_q, v_g_k, v_rel_bias, v_g_out_a, v_g_out_b, v_w_out, v_g_ple, v_w_ple_gate, v_w_ple_up):
    args = dict(locals())
    small = {"g_pre": g_pre, "w_s": w_s[0], "b_s": b_s[0], "ln_v_g": ln_v_g, "ln_v_b": ln_v_b, "g_q": g_q,
             "g_k": g_k, "rel_bias": rel_bias, "g_out_a": g_out_a, "g_out_b": g_out_b, "g_ple": g_ple}
    big_names = ("w_in", "w_out", "w_ple_gate", "w_ple_up")
    big = {k: args[k][0] for k in big_names}

    gathered = _gather_weights([big[k].astype(BF16) for k in big_names])
    loss, grad_x, big_parts, small_parts = _local_step(x[0], p[0, 0], loss_target[0], small, *gathered)

    from_sibling = _pair_exchange(list(big_parts))
    chip_sums = [_pair_add(mine, theirs, "pair_add_" + k) for k, mine, theirs in zip(big_names, big_parts, from_sibling)]
    landed = _chip_exchange(chip_sums, _pack_small(small_parts))

    results = {}
    for k, parts in zip(big_names, landed[:4]):
        results[k] = [t[None] for t in _adamw(parts, big[k], args["m_" + k][0], args["v_" + k][0], "adamw_" + k)]
    squeeze = lambda t: {k: (t[k][0] if k in ("w_s", "b_s") else t[k]) for k in SMALL_NAMES}
    small_m = squeeze({k: args["m_" + k] for k in SMALL_NAMES})
    small_v = squeeze({k: args["v_" + k] for k in SMALL_NAMES})
    packed = _adamw(landed[4], _pack_small(small), _pack_small(small_m), _pack_small(small_v), "adamw_small")
    for idx in range(4):
        tree = _unpack_small(packed[idx], small)
        for k in SMALL_NAMES:
            results.setdefault(k, [None] * 4)[idx] = tree[k].reshape(args[k].shape)

    names = ("g_pre", "w_in", "w_s", "b_s", "ln_v_g", "ln_v_b", "g_q", "g_k", "rel_bias", "g_out_a", "g_out_b",
             "w_out", "g_ple", "w_ple_gate", "w_ple_up")
    total = lax.psum(loss, AXES)
    out = [total, grad_x[None]]
    for idx in range(4):
        out += [results[k][idx] for k in names]
    return tuple(out)
```

```python
import functools
import math

import numpy as np
import jax
import jax.numpy as jnp
from jax import lax
from jax.experimental import pallas as pl
from jax.experimental.pallas import tpu as pltpu

F32 = jnp.float32
BF16 = jnp.bfloat16
EPS = 1e-6
NEG_INF = -1e30
HEAD = 128
DILATIONS = (1, 4, 16)
NUM_BUCKETS = 32
MAX_DISTANCE = 2048
N_SEG = 7
ADAM_LR = 0.001
ADAM_B1 = 0.9
ADAM_B2 = 0.999
ADAM_EPS = 1e-08
ADAM_WD = 0.01
ADAM_STEP = 10
AXES = ("x", "y", "c")
N_DEV = 8
MIB = 1 << 20

SUB = 8

CHUNK_ORDER = np.array([16 * (r % SUB) + r // SUB for r in range(HEAD)])
BLOCK_ORDER = {
    1: CHUNK_ORDER,
    4: np.array([32 * (r // 32) + 4 * (r % SUB) + (r // SUB) % 4 for r in range(HEAD)]),
    16: np.arange(HEAD),
}

NT_DIMS = (((1,), (1,)), ((), ()))
TN_DIMS = (((0,), (0,)), ((), ()))
NN_DIMS = (((1,), (0,)), ((), ()))


def _params(semantics, vmem_mib):
    return pltpu.CompilerParams(dimension_semantics=semantics, vmem_limit_bytes=vmem_mib * MIB)


def _gelu(a):
    return 0.5 * a * (1.0 + lax.erf(a * (2.0 ** -0.5)))


def _gelu_grad(a):
    return 0.5 * (1.0 + lax.erf(a * (2.0 ** -0.5))) + a * jnp.exp(-0.5 * a * a) * ((2.0 * math.pi) ** -0.5)


def _silu_and_grad(a):
    s = jax.nn.sigmoid(a)
    return a * s, s * (1.0 + a * (1.0 - s))


def _rms(v):
    return lax.rsqrt(jnp.mean(v * v, axis=-1, keepdims=True) + EPS)


def _rms_bwd(dy, v, r, g):
    gy = dy * g
    return r * gy - v * (r * r * r) * jnp.mean(gy * v, axis=-1, keepdims=True)


def _dot(a, b, dims=NN_DIMS):
    return lax.dot_general(a, b, dims, preferred_element_type=F32)


def _lane_pick(cols, width):
    rows = cols[0].shape[0]
    lane = lax.broadcasted_iota(jnp.int32, (rows, width), 1)
    out = jnp.zeros((rows, width), F32)
    for h, col in enumerate(cols):
        out = jnp.where(lane == h, col, out)
    return out


def _chunk_perm():
    return jnp.asarray(np.eye(HEAD, dtype=np.float32)[CHUNK_ORDER], BF16)


def _unpermute_f32(p, v):
    hi = v.astype(BF16)
    rest = v - hi.astype(F32)
    mid = rest.astype(BF16)
    lo = (rest - mid.astype(F32)).astype(BF16)
    return _dot(p, hi, TN_DIMS) + _dot(p, mid, TN_DIMS) + _dot(p, lo, TN_DIMS)


def _rmsnorm_fwd(x, g, name, permute):
    s, d = x.shape
    tm = HEAD
    perm = _chunk_perm()

    def body(x_ref, g_ref, p_ref, o_ref):
        v = x_ref[...]
        out = (v * _rms(v) * g_ref[...]).astype(BF16)
        if permute:
            out = _dot(p_ref[...], out).astype(BF16)
        o_ref[...] = out

    return pl.pallas_call(
        body, name=name, grid=(s // tm,),
        out_shape=jax.ShapeDtypeStruct((s, d), BF16),
        in_specs=[pl.BlockSpec((tm, d), lambda i: (i, 0)), pl.BlockSpec((1, d), lambda i: (0, 0)),
                  pl.BlockSpec((HEAD, HEAD), lambda i: (0, 0))],
        out_specs=pl.BlockSpec((tm, d), lambda i: (i, 0)),
        compiler_params=_params(("arbitrary",), 40),
    )(x, g, perm)


def _rmsnorm_bwd(dy, v, g, res, name, dy_permuted, with_bf16):
    s, d = v.shape
    tm = HEAD
    perm = _chunk_perm()

    def body(dy_ref, v_ref, g_ref, res_ref, p_ref, *outs):
        dx_ref, dg_ref = outs[0], outs[-1]
        i = pl.program_id(0)
        vv, dyv = v_ref[...], dy_ref[...]
        if dy_permuted:
            dyv = _unpermute_f32(p_ref[...], dyv)
        r = _rms(vv)
        dx = res_ref[...] + _rms_bwd(dyv, vv, r, g_ref[...])
        dx_ref[...] = dx
        if with_bf16:
            dxb = dx.astype(BF16)
            outs[1][...] = dxb
            outs[2][...] = _dot(p_ref[...], dxb).astype(BF16)

        @pl.when(i == 0)
        def _():
            dg_ref[...] = jnp.zeros_like(dg_ref)

        dg_ref[...] += jnp.sum(dyv * vv * r, axis=0, keepdims=True)

    row = pl.BlockSpec((tm, d), lambda i: (i, 0))
    vec = pl.BlockSpec((1, d), lambda i: (0, 0))
    shapes = [jax.ShapeDtypeStruct((s, d), F32)]
    specs = [row]
    if with_bf16:
        shapes += [jax.ShapeDtypeStruct((s, d), BF16)] * 2
        specs += [row, row]
    shapes.append(jax.ShapeDtypeStruct((1, d), F32))
    specs.append(vec)
    return pl.pallas_call(
        body, name=name, grid=(s // tm,), out_shape=shapes,
        in_specs=[row, row, vec, row, pl.BlockSpec((HEAD, HEAD), lambda i: (0, 0))], out_specs=specs,
        compiler_params=_params(("arbitrary",), 40),
    )(dy, v, g, res, perm)


DEP_SPEC = pl.BlockSpec((SUB, HEAD), lambda *_: (0, 0))


def _drop_arg(body, pos):
    return lambda *refs: body(*refs[:pos], *refs[pos + 1:])


def _matmul(a, b, *, name, grid, a_spec, b_spec, dims, acc_shape, out_shapes, out_specs,
            extra=(), extra_specs=(), epilogue=None, vmem_mib=48, dep=None, prefetch=None):
    nk = grid[2]
    if dep is not None:
        extra, extra_specs = tuple(extra) + (dep,), tuple(extra_specs) + (DEP_SPEC,)
    n_extra, n_out = len(extra), len(out_shapes)
    n_pre = 0 if prefetch is None else 1

    def body(*refs):
        refs = refs[n_pre:]
        a_ref, b_ref = refs[0], refs[1]
        ex = refs[2:2 + n_extra - (dep is not None)]
        outs = refs[2 + n_extra:2 + n_extra + n_out]
        acc = refs[-1]
        k = pl.program_id(2)

        @pl.when(k == 0)
        def _():
            acc[...] = jnp.zeros_like(acc)

        av = a_ref[...]
        if av.dtype != BF16:
            av = av.astype(BF16)
        acc[...] += _dot(av, b_ref[...], dims)

        @pl.when(k == nk - 1)
        def _():
            if epilogue is None:
                outs[0][...] = acc[...].astype(outs[0].dtype)
            else:
                epilogue(acc, ex, outs)

    scratch = [pltpu.VMEM(acc_shape, F32)]
    params = _params(("parallel", "parallel", "arbitrary"), vmem_mib)
    if prefetch is None:
        return pl.pallas_call(
            body, name=name, grid=grid, out_shape=list(out_shapes),
            in_specs=[a_spec, b_spec, *extra_specs], out_specs=list(out_specs),
            scratch_shapes=scratch, compiler_params=params,
        )(a, b, *extra)
    return pl.pallas_call(
        body, name=name, out_shape=list(out_shapes),
        grid_spec=pltpu.PrefetchScalarGridSpec(
            num_scalar_prefetch=1, grid=grid, in_specs=[a_spec, b_spec, *extra_specs],
            out_specs=list(out_specs), scratch_shapes=scratch),
        compiler_params=params,
    )(prefetch, a, b, *extra)


def _tile(n, want):
    t = min(n, want)
    while n % t:
        t //= 2
    return t


def _rel_buckets(dil):
    order = BLOCK_ORDER[dil]
    qi = jnp.asarray(HEAD + order)
    kj = jnp.asarray(np.concatenate([order, HEAD + order]))
    delta = qi[:, None] - kj[None, :]
    band = (delta >= 0) & (delta <= HEAD)
    dist = jnp.clip(delta, 0, None) * dil
    max_exact = NUM_BUCKETS // 2
    dd = jnp.maximum(dist, 1).astype(F32)
    large = max_exact + (jnp.log(dd / max_exact) / math.log(MAX_DISTANCE / max_exact)
                         * (NUM_BUCKETS - max_exact)).astype(jnp.int32)
    large = jnp.minimum(large, NUM_BUCKETS - 1)
    bucket = jnp.where(dist < max_exact, dist, large)
    return jnp.where(band, bucket, -1).astype(jnp.int32)


def _bias_build(rel_bias, buckets, n_heads):
    nd = buckets.shape[0]

    def body(rb_ref, bk_ref, o_ref):
        for c in range(nd):
            def per_head(h, carry, c=c):
                bk = bk_ref[c]
                acc = jnp.where(bk < 0, NEG_INF, 0.0).astype(F32)
                for b in range(NUM_BUCKETS):
                    acc = jnp.where(bk == b, rb_ref[b, h], acc)
                o_ref[c, h] = acc
                return carry

            lax.fori_loop(0, n_heads, per_head, 0)

    return pl.pallas_call(
        body, name="bias_build",
        out_shape=jax.ShapeDtypeStruct((nd, n_heads, HEAD, 2 * HEAD), F32),
        in_specs=[pl.BlockSpec(memory_space=pltpu.SMEM), pl.BlockSpec(memory_space=pltpu.VMEM)],
        out_specs=pl.BlockSpec(memory_space=pltpu.VMEM),
    )(rel_bias, buckets)


def _bias_grad(ds_all, buckets, n_heads):
    nd = buckets.shape[0]

    def body(ds_ref, bk_ref, o_ref):
        def per_pair(i, carry):
            b, h = i // n_heads, i % n_heads
            tot = jnp.zeros((), F32)
            for c in range(nd):
                tot = tot + jnp.sum(jnp.where(bk_ref[c] == b, ds_ref[c, h], 0.0))
            o_ref[b, h] = tot
            return carry

        lax.fori_loop(0, NUM_BUCKETS * n_heads, per_pair, 0)

    return pl.pallas_call(
        body, name="bias_grad",
        out_shape=jax.ShapeDtypeStruct((NUM_BUCKETS, n_heads), F32),
        in_specs=[pl.BlockSpec(memory_space=pltpu.VMEM), pl.BlockSpec(memory_space=pltpu.VMEM)],
        out_specs=pl.BlockSpec(memory_space=pltpu.SMEM),
    )(ds_all, buckets)


def _qkv_prep(proj, g_q, g_k, w):
    s = proj.shape[0]
    n_heads = w // HEAD
    tm = HEAD

    def body(q_ref, k_ref, v_ref, gq_ref, gk_ref, qn_ref, kn_ref, vb_ref):
        gq = gq_ref[...] * (HEAD ** -0.5)
        gk = gk_ref[...]
        for h in range(n_heads):
            sl = slice(h * HEAD, (h + 1) * HEAD)
            q = q_ref[:, sl]
            k = k_ref[:, sl]
            qn_ref[:, sl] = q * _rms(q) * gq
            kn_ref[:, sl] = k * _rms(k) * gk
        vb_ref[...] = v_ref[...]

    seg = lambda j: pl.BlockSpec((tm, w), lambda i, j=j: (i, j))
    vec = pl.BlockSpec((1, HEAD), lambda i: (0, 0))
    out = pl.BlockSpec((tm, w), lambda i: (i, 0))
    return pl.pallas_call(
        body, name="qkv_prep", grid=(s // tm,),
        out_shape=[jax.ShapeDtypeStruct((s, w), F32)] * 3,
        in_specs=[seg(3), seg(4), seg(5), vec, vec], out_specs=[out, out, out],
        compiler_params=_params(("arbitrary",), 40),
    )(proj, proj, proj, g_q, g_k)


class _BlockView:
    def __init__(self, s, dil):
        assert s % (HEAD * dil) == 0 and dil in BLOCK_ORDER
        self.nb = s // (HEAD * dil)
        if dil == 1:
            self.lead, self.block = (s,), (HEAD,)
            self.index = lambda r, n: (n,)
        elif dil == 4:
            self.lead, self.block = (s // 512, 4, 4, 4, SUB), (None, 4, 4, None, SUB)
            self.index = lambda r, n: (n, 0, 0, r, 0)
        else:
            self.lead, self.block = (s // 2048, 16, 16, SUB), (None, 16, None, SUB)
            self.index = lambda r, n: (n, 0, r, 0)

    def view(self, t):
        return t.reshape(self.lead + (t.shape[-1],))

    def spec(self, width, block_of):
        return pl.BlockSpec(self.block + (width,), lambda r, n: self.index(r, block_of(r, n)) + (0,))


def _rows(ref, lanes=slice(None)):
    v = ref[(slice(None),) * (len(ref.shape) - 1) + (lanes,)]
    return v.reshape(HEAD, v.shape[-1])


def _set_rows(ref, lanes, value):
    ref[(slice(None),) * (len(ref.shape) - 1) + (lanes,)] = value.reshape(ref.shape[:-1] + (value.shape[-1],))


def _attn_fwd(qn, kn, vb, bias, dil, name):
    s, w = qn.shape
    n_heads = w // HEAD
    bv = _BlockView(s, dil)

    def body(q_ref, kc_ref, kp_ref, vc_ref, vp_ref, bias_ref, o_ref, lse_ref):
        n = pl.program_id(1)
        lses = []
        for h in range(n_heads):
            sl = slice(h * HEAD, (h + 1) * HEAD)
            q = _rows(q_ref, sl).astype(BF16)
            kp, kc = _rows(kp_ref, sl).astype(BF16), _rows(kc_ref, sl).astype(BF16)
            vp, vc = _rows(vp_ref, sl).astype(BF16), _rows(vc_ref, sl).astype(BF16)
            s_p = _dot(q, kp, NT_DIMS) + bias_ref[h, :, :HEAD]
            s_p = jnp.where(n > 0, s_p, NEG_INF)
            s_c = _dot(q, kc, NT_DIMS) + bias_ref[h, :, HEAD:]
            m = jnp.maximum(jnp.max(s_p, axis=-1, keepdims=True), jnp.max(s_c, axis=-1, keepdims=True))
            e_p = jnp.exp(s_p - m)
            e_c = jnp.exp(s_c - m)
            den = jnp.sum(e_p, axis=-1, keepdims=True) + jnp.sum(e_c, axis=-1, keepdims=True)
            o = _dot(e_p.astype(BF16), vp) + _dot(e_c.astype(BF16), vc)
            _set_rows(o_ref, sl, o / den)
            lses.append(m + jnp.log(den))
        _set_rows(lse_ref, slice(None), _lane_pick(lses, HEAD))

    cur = bv.spec(w, lambda r, n: n)
    prev = bv.spec(w, lambda r, n: jnp.maximum(n - 1, 0))
    o, lse = pl.pallas_call(
        body, name=name, grid=(dil, bv.nb),
        out_shape=[jax.ShapeDtypeStruct(bv.lead + (w,), F32), jax.ShapeDtypeStruct(bv.lead + (HEAD,), F32)],
        in_specs=[cur, cur, prev, cur, prev,
                  pl.BlockSpec((n_heads, HEAD, 2 * HEAD), lambda r, n: (0, 0, 0))],
        out_specs=[cur, bv.spec(HEAD, lambda r, n: n)],
        compiler_params=_params(("arbitrary", "arbitrary"), 48),
    )(bv.view(qn), bv.view(kn), bv.view(kn), bv.view(vb), bv.view(vb), bias)
    return o.reshape(s, w), lse.reshape(s, HEAD)


def _attn_bwd(qn, kn, vb, dyb, lse, delta, bias, dil, name, dep=None):
    s, w = qn.shape
    n_heads = w // HEAD
    bv = _BlockView(s, dil)
    nb = bv.nb

    def body(q_ref, kc_ref, kp_ref, vc_ref, vp_ref, dy_ref, lse_ref, dl_ref, bias_ref,
             dq_ref, dk_ref, dv_ref, ds_ref, carry_k, carry_v):
        r = pl.program_id(0)
        step = pl.program_id(1)
        blk = nb - 1 - step

        @pl.when((r == 0) & (step == 0))
        def _():
            ds_ref[...] = jnp.zeros_like(ds_ref)

        @pl.when(step == 0)
        def _():
            carry_k[...] = jnp.zeros_like(carry_k)
            carry_v[...] = jnp.zeros_like(carry_v)

        tots = _rows(lse_ref)
        dls = _rows(dl_ref)
        for h in range(n_heads):
            sl = slice(h * HEAD, (h + 1) * HEAD)
            q, dy = _rows(q_ref, sl).astype(BF16), _rows(dy_ref, sl).astype(BF16)
            kp, kc = _rows(kp_ref, sl).astype(BF16), _rows(kc_ref, sl).astype(BF16)
            vp, vc = _rows(vp_ref, sl).astype(BF16), _rows(vc_ref, sl).astype(BF16)
            tot = tots[:, h:h + 1]
            dl = dls[:, h:h + 1]
            s_p = _dot(q, kp, NT_DIMS) + bias_ref[h, :, :HEAD]
            s_p = jnp.where(blk > 0, s_p, NEG_INF)
            s_c = _dot(q, kc, NT_DIMS) + bias_ref[h, :, HEAD:]
            p_p = jnp.exp(s_p - tot)
            p_c = jnp.exp(s_c - tot)
            ds_p = p_p * (_dot(dy, vp, NT_DIMS) - dl)
            ds_c = p_c * (_dot(dy, vc, NT_DIMS) - dl)
            ds_ref[h, :, :HEAD] += ds_p
            ds_ref[h, :, HEAD:] += ds_c
            ds_pb, ds_cb = ds_p.astype(BF16), ds_c.astype(BF16)
            _set_rows(dq_ref, sl, _dot(ds_pb, kp) + _dot(ds_cb, kc))
            _set_rows(dk_ref, sl, _dot(ds_cb, q, TN_DIMS) + carry_k[:, sl])
            carry_k[:, sl] = _dot(ds_pb, q, TN_DIMS)
            _set_rows(dv_ref, sl, _dot(p_c.astype(BF16), dy, TN_DIMS) + carry_v[:, sl])
            carry_v[:, sl] = _dot(p_p.astype(BF16), dy, TN_DIMS)

    cur = bv.spec(w, lambda r, n: nb - 1 - n)
    prev = bv.spec(w, lambda r, n: jnp.maximum(nb - 2 - n, 0))
    stat = bv.spec(HEAD, lambda r, n: nb - 1 - n)
    whole = pl.BlockSpec((n_heads, HEAD, 2 * HEAD), lambda r, n: (0, 0, 0))
    big = jax.ShapeDtypeStruct(bv.lead + (w,), F32)
    in_specs = [cur, cur, prev, cur, prev, cur, stat, stat, whole]
    operands = [bv.view(qn), bv.view(kn), bv.view(kn), bv.view(vb), bv.view(vb), bv.view(dyb), bv.view(lse),
                bv.view(delta), bias]
    if dep is not None:
        body = _drop_arg(body, len(operands))
        in_specs.append(DEP_SPEC)
        operands.append(dep)
    dq, dk, dv, ds = pl.pallas_call(
        body, name=name, grid=(dil, nb),
        out_shape=[big, big, big, jax.ShapeDtypeStruct((n_heads, HEAD, 2 * HEAD), F32)],
        in_specs=in_specs, out_specs=[cur, cur, cur, whole],
        scratch_shapes=[pltpu.VMEM((HEAD, w), F32), pltpu.VMEM((HEAD, w), F32)],
        compiler_params=_params(("arbitrary", "arbitrary"), 56),
    )(*operands)
    return dq.reshape(s, w), dk.reshape(s, w), dv.reshape(s, w), ds


def _qkv_bwd(dproj, proj, dqs, dks, dvs, g_q, g_k, w):
    s = proj.shape[0]
    n_heads = w // HEAD
    tm = HEAD

    def body(dproj_hbm, q_ref, k_ref, gq_ref, gk_ref, *rest):
        dq_refs, dk_refs, dv_refs = rest[0:3], rest[3:6], rest[6:9]
        out_ref, dgq_ref, dgk_ref = rest[9:12]
        i = pl.program_id(0)
        gq = gq_ref[...] * (HEAD ** -0.5)
        gk = gk_ref[...]
        acc_q = jnp.zeros((1, HEAD), F32)
        acc_k = jnp.zeros((1, HEAD), F32)
        for h in range(n_heads):
            sl = slice(h * HEAD, (h + 1) * HEAD)
            q, k = q_ref[:, sl], k_ref[:, sl]
            dqn = sum(t[:, sl].astype(F32) for t in dq_refs)
            dkn = sum(t[:, sl].astype(F32) for t in dk_refs)
            rq, rk = _rms(q), _rms(k)
            out_ref[:, h * HEAD:(h + 1) * HEAD] = _rms_bwd(dqn, q, rq, gq).astype(BF16)
            out_ref[:, w + h * HEAD:w + (h + 1) * HEAD] = _rms_bwd(dkn, k, rk, gk).astype(BF16)
            acc_q += jnp.sum(dqn * q * rq, axis=0, keepdims=True)
            acc_k += jnp.sum(dkn * k * rk, axis=0, keepdims=True)
        out_ref[:, 2 * w:] = sum(t[...].astype(F32) for t in dv_refs).astype(BF16)

        @pl.when(i == 0)
        def _():
            dgq_ref[...] = jnp.zeros_like(dgq_ref)
            dgk_ref[...] = jnp.zeros_like(dgk_ref)

        dgq_ref[...] += acc_q * (HEAD ** -0.5)
        dgk_ref[...] += acc_k

    seg = lambda j: pl.BlockSpec((tm, w), lambda i, j=j: (i, j))
    vec = pl.BlockSpec((1, HEAD), lambda i: (0, 0))
    row = pl.BlockSpec((tm, w), lambda i: (i, 0))
    return pl.pallas_call(
        body, name="qkv_bwd", grid=(s // tm,),
        out_shape=[jax.ShapeDtypeStruct(dproj.shape, BF16),
                   jax.ShapeDtypeStruct((1, HEAD), F32), jax.ShapeDtypeStruct((1, HEAD), F32)],
        in_specs=[pl.BlockSpec(memory_space=pl.ANY), seg(3), seg(4), vec, vec] + [row] * 9,
        out_specs=[pl.BlockSpec((tm, 3 * w), lambda i: (i, 1)), vec, vec],
        input_output_aliases={0: 0},
        compiler_params=_params(("arbitrary",), 48),
    )(dproj, proj, proj, g_q, g_k, *dqs, *dks, *dvs)


def _mixer_a(au, av, ws_ref, bst_ref, lng, lnb, z_scr, ln_scr):
    n_groups = au.shape[1] // HEAD
    u = _gelu(au)
    gv = _gelu(av)
    mu = jnp.mean(gv, axis=-1, keepdims=True)
    xc = gv - mu
    rs = lax.rsqrt(jnp.mean(xc * xc, axis=-1, keepdims=True) + EPS)
    xhat = xc * rs
    ln_scr[...] = (xhat * lng + lnb).astype(BF16)
    causal = _causal_mask()
    for g in range(n_groups):
        sl = slice(g * HEAD, (g + 1) * HEAD)
        wm = jnp.where(causal, ws_ref[g], 0.0).astype(BF16)
        z_scr[:, sl] = _dot(wm, ln_scr[:, sl]) + bst_ref[:, g:g + 1]
    return u, xhat, rs


def _causal_mask():
    token = lambda r: 16 * (r % SUB) + r // SUB
    row = lax.broadcasted_iota(jnp.int32, (HEAD, HEAD), 0)
    col = lax.broadcasted_iota(jnp.int32, (HEAD, HEAD), 1)
    return token(col) <= token(row)


def _merge_b(o_refs, lse_refs, yb_scr):
    n_heads = yb_scr.shape[1] // HEAD
    lses = [t[...] for t in lse_refs]
    m = jnp.maximum(jnp.maximum(lses[0], lses[1]), lses[2])
    tot = m + jnp.log(sum(jnp.exp(t - m) for t in lses))
    alphas = [jnp.exp(t - tot) for t in lses]
    for h in range(n_heads):
        sl = slice(h * HEAD, (h + 1) * HEAD)
        yb_scr[:, sl] = sum(a[:, h:h + 1] * o[:, sl].astype(F32) for a, o in zip(alphas, o_refs))
    return tot


def _mix_fwd(proj, outs, lses, w_s, bst, ln_g, ln_b, g_a, g_b, w):
    s = proj.shape[0]
    n_groups = w // HEAD

    def body(au_ref, av_ref, az_ref, bz_ref, o1, o2, o3, l1, l2, l3, ws_ref, bst_ref,
             lng_ref, lnb_ref, ga_ref, gb_ref, p_ref, y_ref, z_scr, ln_scr, yb_scr):
        u, _, _ = _mixer_a(au_ref[...], av_ref[...], ws_ref, bst_ref, lng_ref[...], lnb_ref[...], z_scr, ln_scr)
        ya = u * z_scr[...]
        silu_a, _ = _silu_and_grad(az_ref[...])
        perm = p_ref[...]
        y_ref[:, :w] = _dot(perm, (ya * _rms(ya) * ga_ref[...] * silu_a).astype(BF16), TN_DIMS).astype(BF16)
        _merge_b((o1, o2, o3), (l1, l2, l3), yb_scr)
        yb = yb_scr[...]
        silu_b, _ = _silu_and_grad(bz_ref[...])
        y_ref[:, w:] = _dot(perm, (yb * _rms(yb) * gb_ref[...] * silu_b).astype(BF16), TN_DIMS).astype(BF16)

    seg = lambda j: pl.BlockSpec((HEAD, w), lambda i, j=j: (i, j))
    row = pl.BlockSpec((HEAD, w), lambda i: (i, 0))
    stat = pl.BlockSpec((HEAD, HEAD), lambda i: (i, 0))
    vec = pl.BlockSpec((1, w), lambda i: (0, 0))
    return pl.pallas_call(
        body, name="mix_fwd", grid=(s // HEAD,),
        out_shape=jax.ShapeDtypeStruct((s, 2 * w), BF16),
        in_specs=[seg(0), seg(1), seg(2), seg(6), row, row, row, stat, stat, stat,
                  pl.BlockSpec((n_groups, HEAD, HEAD), lambda i: (0, 0, 0)),
                  pl.BlockSpec((HEAD, n_groups), lambda i: (0, 0)), vec, vec, vec, vec,
                  pl.BlockSpec((HEAD, HEAD), lambda i: (0, 0))],
        out_specs=pl.BlockSpec((HEAD, 2 * w), lambda i: (i, 0)),
        scratch_shapes=[pltpu.VMEM((HEAD, w), F32), pltpu.VMEM((HEAD, w), BF16), pltpu.VMEM((HEAD, w), F32)],
        compiler_params=_params(("arbitrary",), 48),
    )(proj, proj, proj, proj, *outs, *lses, w_s, bst, ln_g, ln_b, g_a, g_b, _chunk_perm())


def _mix_bwd(proj, dy, outs, lses, w_s, bst, ln_g, ln_b, g_a, g_b, w):
    s = proj.shape[0]
    n_groups = w // HEAD

    def body(au_ref, av_ref, az_ref, bz_ref, dy_ref, o1, o2, o3, l1, l2, l3, ws_ref, bst_ref,
             lng_ref, lnb_ref, ga_ref, gb_ref,
             dproj_ref, dyb_ref, tot_ref, dl_ref, dws_ref, dbst_ref, dlng_ref, dlnb_ref, dga_ref, dgb_ref,
             z_scr, ln_scr, yb_scr, dz_scr, dln_scr):
        i = pl.program_id(0)

        @pl.when(i == 0)
        def _():
            for t in (dws_ref, dbst_ref, dlng_ref, dlnb_ref, dga_ref, dgb_ref):
                t[...] = jnp.zeros_like(t)

        au, av, az = au_ref[...], av_ref[...], az_ref[...]
        lng = lng_ref[...]
        u, xhat, rs = _mixer_a(au, av, ws_ref, bst_ref, lng, lnb_ref[...], z_scr, ln_scr)
        z = z_scr[...]
        ya = u * z
        ra = _rms(ya)
        silu_a, dsilu_a = _silu_and_grad(az)
        dya_all = dy_ref[:, :w]
        na = ya * ra * ga_ref[...]
        dna = dya_all * silu_a
        dproj_ref[:, 2 * w:3 * w] = (dya_all * na * dsilu_a).astype(BF16)
        dga_ref[...] += jnp.sum(dna * ya * ra, axis=0, keepdims=True)
        dya = _rms_bwd(dna, ya, ra, ga_ref[...])
        dproj_ref[:, :w] = (dya * z * _gelu_grad(au)).astype(BF16)
        dz_scr[...] = (dya * u).astype(BF16)

        causal = _causal_mask()
        for g in range(n_groups):
            sl = slice(g * HEAD, (g + 1) * HEAD)
            wm = jnp.where(causal, ws_ref[g], 0.0).astype(BF16)
            dz = dz_scr[:, sl]
            dln_scr[:, sl] = _dot(wm, dz, TN_DIMS)
            dws_ref[g] += jnp.where(causal, _dot(dz, ln_scr[:, sl], NT_DIMS), 0.0)
            dbst_ref[:, g:g + 1] += jnp.sum(dz.astype(F32), axis=-1, keepdims=True)
        dln = dln_scr[...]
        dlng_ref[...] += jnp.sum(dln * xhat, axis=0, keepdims=True)
        dlnb_ref[...] += jnp.sum(dln, axis=0, keepdims=True)
        gy = dln * lng
        dgv = rs * (gy - jnp.mean(gy, axis=-1, keepdims=True)
                    - xhat * jnp.mean(gy * xhat, axis=-1, keepdims=True))
        dproj_ref[:, w:2 * w] = (dgv * _gelu_grad(av)).astype(BF16)
        dproj_ref[:, 3 * w:6 * w] = jnp.zeros((HEAD, 3 * w), BF16)

        tot_ref[...] = _merge_b((o1, o2, o3), (l1, l2, l3), yb_scr)
        yb = yb_scr[...]
        rb = _rms(yb)
        bz = bz_ref[...]
        silu_b, dsilu_b = _silu_and_grad(bz)
        dyb_all = dy_ref[:, w:]
        dnb = dyb_all * silu_b
        dproj_ref[:, 6 * w:] = (dyb_all * yb * rb * gb_ref[...] * dsilu_b).astype(BF16)
        dgb_ref[...] += jnp.sum(dnb * yb * rb, axis=0, keepdims=True)
        dyb = _rms_bwd(dnb, yb, rb, gb_ref[...])
        dyb_ref[...] = dyb
        prod = dyb * yb
        dl_ref[...] = _lane_pick(
            [jnp.sum(prod[:, h * HEAD:(h + 1) * HEAD], axis=-1, keepdims=True) for h in range(n_groups)], HEAD)

    seg = lambda j: pl.BlockSpec((HEAD, w), lambda i, j=j: (i, j))
    row_w = pl.BlockSpec((HEAD, w), lambda i: (i, 0))
    stat = pl.BlockSpec((HEAD, HEAD), lambda i: (i, 0))
    vec = pl.BlockSpec((1, w), lambda i: (0, 0))
    ws_spec = pl.BlockSpec((n_groups, HEAD, HEAD), lambda i: (0, 0, 0))
    bst_spec = pl.BlockSpec((HEAD, n_groups), lambda i: (0, 0))
    vec_shape = jax.ShapeDtypeStruct((1, w), F32)
    return pl.pallas_call(
        body, name="mix_bwd", grid=(s // HEAD,),
        out_shape=[jax.ShapeDtypeStruct((s, N_SEG * w), BF16), jax.ShapeDtypeStruct((s, w), F32),
                   jax.ShapeDtypeStruct((s, HEAD), F32), jax.ShapeDtypeStruct((s, HEAD), F32),
                   jax.ShapeDtypeStruct((n_groups, HEAD, HEAD), F32), jax.ShapeDtypeStruct((HEAD, n_groups), F32),
                   vec_shape, vec_shape, vec_shape, vec_shape],
        in_specs=[seg(0), seg(1), seg(2), seg(6), pl.BlockSpec((HEAD, 2 * w), lambda i: (i, 0)),
                  row_w, row_w, row_w, stat, stat, stat, ws_spec, bst_spec, vec, vec, vec, vec],
        out_specs=[pl.BlockSpec((HEAD, N_SEG * w), lambda i: (i, 0)), row_w, stat, stat,
                   ws_spec, bst_spec, vec, vec, vec, vec],
        scratch_shapes=[pltpu.VMEM((HEAD, w), F32), pltpu.VMEM((HEAD, w), BF16), pltpu.VMEM((HEAD, w), F32),
                        pltpu.VMEM((HEAD, w), BF16), pltpu.VMEM((HEAD, w), F32)],
        compiler_params=_params(("arbitrary",), 56),
    )(proj, proj, proj, proj, dy, *outs, *lses, w_s, bst, ln_g, ln_b, g_a, g_b)


def _local_step(x, p, tgt, small, win_g, wout_g, wgate_g, wup_g, ex, while_last_travels=None):
    s, d = x.shape
    n, _, c_in = win_g.shape
    assert n == N_DEV
    d_in = n * c_in
    w = d_in // N_SEG
    n_heads = w // HEAD
    p_dim, c_up = wup_g.shape[1], wup_g.shape[2]
    assert s % (HEAD * DILATIONS[-1]) == 0 and w % HEAD == 0 and d == n * c_up == 2 * w

    wout_f = wout_g.reshape(2 * w, d)
    wgate_f = wgate_g.reshape(d, d)

    hn = _rmsnorm_fwd(x, small["g_pre"], "pre_norm", True)
    tm, tk = _tile(s, 1024), _tile(d, 1024)
    tn_in = c_in
    proj = _matmul(
        hn, win_g, name="in_proj", grid=(s // tm, n, d // tk), dims=NN_DIMS,
        a_spec=pl.BlockSpec((tm, tk), lambda i, j, k: (i, k)),
        b_spec=pl.BlockSpec((None, tk, tn_in), lambda i, j, k: (j, k, 0)),
        acc_shape=(tm, tn_in), out_shapes=[jax.ShapeDtypeStruct((s, d_in), F32)],
        out_specs=[pl.BlockSpec((tm, tn_in), lambda i, j, k: (i, j))])[0]

    qn, kn, vb = _qkv_prep(proj, small["g_q"], small["g_k"], w)
    buckets = jnp.stack([_rel_buckets(dil) for dil in DILATIONS])
    bias = _bias_build(small["rel_bias"], buckets, n_heads)
    outs, lses = [], []
    for c, dil in enumerate(DILATIONS):
        o, l = _attn_fwd(qn, kn, vb, bias[c], dil, "attn_fwd_d%d" % dil)
        outs.append(o)
        lses.append(l)

    ws_p = small["w_s"][:, CHUNK_ORDER][:, :, CHUNK_ORDER]
    bst = small["b_s"].T[CHUNK_ORDER]
    mix_args = (outs, lses, ws_p, bst, small["ln_v_g"], small["ln_v_b"], small["g_out_a"], small["g_out_b"], w)
    y = _mix_fwd(proj, *mix_args)

    tn = _tile(d, 1024)
    tk2 = _tile(2 * w, 1024)

    def resid_epilogue(acc, ex, outs_):
        outs_[0][...] = ex[0][...] + acc[...]

    h = _matmul(
        y, wout_f, name="out_proj", grid=(s // tm, d // tn, (2 * w) // tk2), dims=NN_DIMS,
        a_spec=pl.BlockSpec((tm, tk2), lambda i, j, k: (i, k)),
        b_spec=pl.BlockSpec((tk2, tn), lambda i, j, k: (k, j)),
        acc_shape=(tm, tn), out_shapes=[jax.ShapeDtypeStruct((s, d), F32)],
        out_specs=[pl.BlockSpec((tm, tn), lambda i, j, k: (i, j))],
        extra=(x,), extra_specs=(pl.BlockSpec((tm, tn), lambda i, j, k: (i, j)),),
        epilogue=resid_epilogue)[0]

    hn2 = _rmsnorm_fwd(h, small["g_ple"], "ple_norm", False)

    tmg = _tile(s, 512)

    def ple_epilogue(acc, ex, outs_):
        h_ref, p_ref, wup_ref, tgt_ref = ex
        dout_ref, dpre_ref, dup_ref, loss_ref = outs_
        gate = jax.nn.sigmoid(acc[...])
        up = _dot(p_ref[...].astype(BF16), wup_ref[...])
        err = h_ref[...] + gate * up - tgt_ref[...]
        dout = err * (1.0 / d)
        dout_ref[...] = dout
        dpre_ref[...] = (dout * up * gate * (1.0 - gate)).astype(BF16)
        dup_ref[...] = (dout * gate).astype(BF16)
        part = 0.5 * jnp.sum(err * err) * (1.0 / d)
        rr = lax.broadcasted_iota(jnp.int32, (8, HEAD), 0)
        cc = lax.broadcasted_iota(jnp.int32, (8, HEAD), 1)
        loss_ref[...] = jnp.where((rr == 0) & (cc == 0), part, 0.0)

    tile_ij = pl.BlockSpec((tmg, c_up), lambda i, j, k: (i, j))
    dout, dpre, dup, loss_parts = _matmul(
        hn2, wgate_f, name="ple_gate", grid=(s // tmg, n, d // tk), dims=NN_DIMS,
        a_spec=pl.BlockSpec((tmg, tk), lambda i, j, k: (i, k)),
        b_spec=pl.BlockSpec((tk, c_up), lambda i, j, k: (k, j)),
        acc_shape=(tmg, c_up),
        out_shapes=[jax.ShapeDtypeStruct((s, d), F32), jax.ShapeDtypeStruct((s, d), BF16),
                    jax.ShapeDtypeStruct((s, d), BF16), jax.ShapeDtypeStruct((s // tmg * 8, n * HEAD), F32)],
        out_specs=[tile_ij, tile_ij, tile_ij, pl.BlockSpec((8, HEAD), lambda i, j, k: (i, j))],
        extra=(h, p, wup_g, tgt),
        extra_specs=(tile_ij, pl.BlockSpec((tmg, p_dim), lambda i, j, k: (i, 0)),
                     pl.BlockSpec((None, p_dim, c_up), lambda i, j, k: (j, 0, 0)), tile_ij),
        epilogue=ple_epilogue)
    loss = jnp.sum(loss_parts)

    tks = _tile(s, 1024)
    g_wup = _matmul(
        p, dup, name="grad_w_up", grid=(1, n, s // tks), dims=TN_DIMS,
        a_spec=pl.BlockSpec((tks, p_dim), lambda i, j, k: (k, 0)),
        b_spec=pl.BlockSpec((tks, c_up), lambda i, j, k: (k, j)),
        acc_shape=(p_dim, c_up), out_shapes=[jax.ShapeDtypeStruct((n, p_dim, c_up), BF16)],
        out_specs=[pl.BlockSpec((None, p_dim, c_up), lambda i, j, k: (j, 0, 0))])[0]

    def tn_matmul(a, b, name):
        m_, n_ = a.shape[1], b.shape[1]
        bm, bn = _tile(m_, 1024), _tile(n_, 1024)
        return _matmul(
            a, b, name=name, grid=(m_ // bm, n_ // bn, s // tks), dims=TN_DIMS,
            a_spec=pl.BlockSpec((tks, bm), lambda i, j, k: (k, i)),
            b_spec=pl.BlockSpec((tks, bn), lambda i, j, k: (k, j)),
            acc_shape=(bm, bn), out_shapes=[jax.ShapeDtypeStruct((m_, n_), BF16)],
            out_specs=[pl.BlockSpec((bm, bn), lambda i, j, k: (i, j))])[0]

    def nt_matmul(a, b, name, out_dtype, dep=None):
        k_, n_ = a.shape[1], b.shape[0]
        bm, bn, bk = _tile(s, 1024), _tile(n_, 1024), _tile(k_, 1024)
        return _matmul(
            a, b, name=name, grid=(s // bm, n_ // bn, k_ // bk), dims=NT_DIMS,
            a_spec=pl.BlockSpec((bm, bk), lambda i, j, k: (i, k)),
            b_spec=pl.BlockSpec((bn, bk), lambda i, j, k: (j, k)),
            acc_shape=(bm, bn), out_shapes=[jax.ShapeDtypeStruct((s, n_), out_dtype)],
            out_specs=[pl.BlockSpec((bm, bn), lambda i, j, k: (i, j))], dep=dep)[0]

    by_core = lambda g: g.reshape((N_CHIP, 2) + g.shape[-2:])
    g_wgate = tn_matmul(hn2, dpre, "grad_w_gate").reshape(wgate_g.shape)
    dhn2 = nt_matmul(dpre, wgate_f, "ple_gate_bwd", F32)
    dh, dh_b, dh_bp, dg_ple = _rmsnorm_bwd(dhn2, h, small["g_ple"], dout, "ple_norm_bwd", False, True)
    g_wout = tn_matmul(y, dh_b, "grad_w_out").reshape(wout_g.shape)

    late = ("w_out", "w_ple_gate", "w_ple_up")
    late_parts = (g_wout, g_wgate, g_wup)
    token = ex.push_pairs("pair_late", [by_core(g) for g in late_parts])
    dy = nt_matmul(dh_bp, wout_f, "out_proj_bwd", F32, dep=token)
    (dproj, dyb, lse_tot, delta, dws, dbst, dlng, dlnb, dga, dgb) = _mix_bwd(proj, dy, *mix_args)
    from_sibling = ex.pairs_done("pair_late", [dproj])
    pair_sums = [_pair_add(mine, theirs, "pair_add_" + k, ex.core)
                 for k, mine, theirs in zip(late, late_parts, from_sibling)]
    token = ex.push_chips("chip_late", pair_sums)

    dqs, dks, dvs, dss = [], [], [], []
    for c, dil in enumerate(DILATIONS):
        dq, dk, dv, ds = _attn_bwd(qn, kn, vb, dyb, lse_tot, delta, bias[c], dil, "attn_bwd_d%d" % dil,
                                   dep=token if c == 0 else None)
        dqs.append(dq)
        dks.append(dk)
        dvs.append(dv)
        dss.append(ds)
    d_rel = _bias_grad(jnp.stack(dss), buckets, n_heads)
    dproj, dgq, dgk = _qkv_bwd(dproj, proj, dqs, dks, dvs, small["g_q"], small["g_k"], w)
    pair_sums, landed, _ = ex.chips_done("chip_late", [dproj])
    delivered = {k: (mine, theirs) for k, mine, theirs in zip(late, pair_sums, landed)}

    token_row = np.argsort(CHUNK_ORDER)
    dws = dws[:, token_row][:, :, token_row]
    dbst = dbst[token_row]
    small_grads = {
        "w_s": dws, "b_s": dbst.T, "ln_v_g": dlng, "ln_v_b": dlnb, "g_q": dgq, "g_k": dgk,
        "rel_bias": d_rel, "g_out_a": dga, "g_out_b": dgb, "g_ple": dg_ple,
    }

    bm = _tile(d, 1024)

    def grad_w_in(core, name, dep=None):
        return _matmul(
            hn, dproj, name=name, grid=(d // bm, N_CHIP, s // tks), dims=TN_DIMS, prefetch=core.reshape(1),
            a_spec=pl.BlockSpec((tks, bm), lambda i, j, k, core_ref: (k, i)),
            b_spec=pl.BlockSpec((tks, c_in), lambda i, j, k, core_ref: (k, 2 * j + core_ref[0])),
            acc_shape=(bm, c_in), out_shapes=[jax.ShapeDtypeStruct((N_CHIP, d, c_in), BF16)],
            out_specs=[pl.BlockSpec((None, bm, c_in), lambda i, j, k, core_ref: (j, i, 0))], dep=dep)[0]

    for_sibling = grad_w_in(1 - ex.core, "grad_w_in_sibling")
    token = ex.push_pairs("pair_in", [for_sibling])
    mine = grad_w_in(ex.core, "grad_w_in_mine", dep=token)
    from_sibling = ex.pairs_done("pair_in", [mine])
    pair_sum = _pair_add(mine, from_sibling[0], "pair_add_w_in")
    token = ex.push_chips("chip_in", [pair_sum], _pack_small(small_grads, SMALL_EARLY))

    dhn = _matmul(
        dproj, win_g, name="in_proj_bwd", grid=(s // tm, d // tn, n), dims=NT_DIMS,
        a_spec=pl.BlockSpec((tm, c_in), lambda i, j, k: (i, k)),
        b_spec=pl.BlockSpec((None, tn, c_in), lambda i, j, k: (k, j, 0)),
        acc_shape=(tm, tn), out_shapes=[jax.ShapeDtypeStruct((s, d), F32)],
        out_specs=[pl.BlockSpec((tm, tn), lambda i, j, k: (i, j))], dep=token)[0]
    grad_x, dg_pre = _rmsnorm_bwd(dhn, x, small["g_pre"], dh, "pre_norm_bwd", True, False)
    extra = while_last_travels(token, delivered) if while_last_travels is not None else []
    pair_sums, landed, slabs = ex.chips_done("chip_in", [grad_x] + list(extra))
    delivered["w_in"] = (pair_sums[0], landed[0])
    small_grads["g_pre"] = dg_pre
    return loss, grad_x, small_grads, delivered, slabs, extra


SMALL_EARLY = ("w_s", "b_s", "ln_v_g", "ln_v_b", "g_q", "g_k", "rel_bias", "g_out_a", "g_out_b", "g_ple")
SMALL_LAST = ("g_pre",)
SMALL_NAMES = SMALL_LAST + SMALL_EARLY


def _pack_small(tree, names):
    parts = []
    for name in names:
        flat = tree[name].astype(F32).reshape(-1)
        pad = (-flat.shape[0]) % HEAD
        parts.append(jnp.pad(flat, (0, pad)) if pad else flat)
    slab = jnp.concatenate(parts).reshape(-1, HEAD)
    pad_rows = (-slab.shape[0]) % 8
    return jnp.pad(slab, ((0, pad_rows), (0, 0))) if pad_rows else slab


def _unpack_small(slab, like, names):
    flat = slab.reshape(-1)
    out, off = {}, 0
    for name in names:
        size = like[name].size
        out[name] = flat[off:off + size].reshape(like[name].shape)
        off += size + (-size) % HEAD
    return out


def _peer(k):
    x, y, c = (lax.axis_index(a) for a in AXES)
    bits = ((k >> 2) & 1, (k >> 1) & 1, k & 1)
    px, py, pc = (1 - v if b else v for v, b in zip((x, y, c), bits))
    return (px, py, pc), 4 * px + 2 * py + pc


def _my_index():
    x, y, c = (lax.axis_index(a) for a in AXES)
    return 4 * x + 2 * y + c


N_CHIP = 4
HBM_SPEC = pl.BlockSpec(memory_space=pl.ANY)


def _remote(src, dst, send_sem, recv_sem, peer):
    return pltpu.make_async_remote_copy(src_ref=src, dst_ref=dst, send_sem=send_sem, recv_sem=recv_sem,
                                        device_id=peer, device_id_type=pl.DeviceIdType.MESH)


def _gather_weights(shards):
    n_arr = len(shards)

    def body(*refs):
        ins, outs = refs[:n_arr], refs[n_arr:2 * n_arr]
        send_sems, recv_sems, local_sems = refs[2 * n_arr:]
        me = _my_index()
        sibling, sib_idx = _peer(1)
        chips = [_peer(2), _peer(4), _peer(6)]
        local = [pltpu.make_async_copy(ins[a], outs[a].at[me], local_sems.at[a]) for a in range(n_arr)]
        for cp in local:
            cp.start()
        sends = []
        for j, (chip, _) in enumerate(chips):
            for a in range(n_arr):
                sends.append(_remote(ins[a], outs[a].at[me], send_sems.at[a, 1 + j], recv_sems.at[a, 1 + j], chip))
        for a in range(n_arr):
            sends.append(_remote(ins[a], outs[a].at[me], send_sems.at[a, 0], recv_sems.at[a, 0], sibling))
        for cp in sends:
            cp.start()
        for j, (chip, chip_idx) in enumerate(chips):
            for a in range(n_arr):
                slot = outs[a].at[chip_idx]
                _remote(slot, slot, send_sems.at[a, 1 + j], recv_sems.at[a, 1 + j], chip).wait_recv()
                fwd = _remote(slot, slot, send_sems.at[a, 4 + j], recv_sems.at[a, 4 + j], sibling)
                fwd.start()
                sends.append(fwd)
        for a in range(n_arr):
            slot = outs[a].at[sib_idx]
            _remote(slot, slot, send_sems.at[a, 0], recv_sems.at[a, 0], sibling).wait_recv()
        for j, (chip, chip_idx) in enumerate(chips):
            for a in range(n_arr):
                slot = outs[a].at[chip_idx ^ 1]
                _remote(slot, slot, send_sems.at[a, 4 + j], recv_sems.at[a, 4 + j], sibling).wait_recv()
        for cp in sends:
            cp.wait_send()
        for cp in local:
            cp.wait()

    return pl.pallas_call(
        body, name="gather_weights",
        out_shape=[jax.ShapeDtypeStruct((N_DEV,) + a.shape, a.dtype) for a in shards],
        in_specs=[HBM_SPEC] * n_arr, out_specs=[HBM_SPEC] * n_arr,
        scratch_shapes=[pltpu.SemaphoreType.DMA((n_arr, N_DEV - 1)), pltpu.SemaphoreType.DMA((n_arr, N_DEV - 1)),
                        pltpu.SemaphoreType.DMA((n_arr,))],
        compiler_params=pltpu.CompilerParams(has_side_effects=True),
    )(*shards)


SEM_SPEC = pl.BlockSpec(memory_space=pltpu.SEMAPHORE)
HBM_ONLY = pl.BlockSpec(memory_space=pltpu.HBM)
DATAFLOW = pltpu.SideEffectType.DATAFLOW_SIDE_EFFECTING


def _comm_call(name, arrays, *, wait=None, start=None, after=()):
    n, n_after = len(arrays), len(after)

    def body(*refs):
        ins = refs[:n]
        pos = n
        if wait is not None:
            for cp in wait[2](ins, refs[pos], refs[pos + 1]):
                cp.wait()
            pos += 2
        outs = refs[pos + n_after:]
        if start is not None:
            for cp in start[1](ins, outs[0], outs[1]):
                cp.start()
        outs[-1][...] = jnp.zeros_like(outs[-1])

    operands = [pltpu.with_memory_space_constraint(a, pltpu.HBM) for a in arrays]
    in_specs = [HBM_ONLY] * n
    if wait is not None:
        operands += [wait[0], wait[1]]
        in_specs += [SEM_SPEC, SEM_SPEC]
    operands += list(after)
    in_specs += [HBM_SPEC] * n_after
    out_shape, out_specs = [], []
    if start is not None:
        out_shape += [pltpu.SemaphoreType.DMA((start[0],))] * 2
        out_specs += [SEM_SPEC, SEM_SPEC]
    first = len(out_shape)
    out_shape += [pltpu.HBM(a.shape, a.dtype) for a in arrays] + [jax.ShapeDtypeStruct((SUB, HEAD), F32)]
    out_specs += [HBM_ONLY] * n + [pl.BlockSpec(memory_space=pltpu.VMEM)]
    res = pl.pallas_call(
        body, name=name, out_shape=tuple(out_shape), in_specs=tuple(in_specs), out_specs=tuple(out_specs),
        input_output_aliases={i: first + i for i in range(n)},
        compiler_params=pltpu.CompilerParams(has_side_effects=DATAFLOW),
    )(*operands)
    sems = (res[0], res[1]) if start is not None else None
    return list(res[first:first + n]), sems, res[-1]


class _GradExchange:
    def __init__(self):
        x, y, c = (lax.axis_index(a) for a in AXES)
        self.core = c.astype(jnp.int32)
        self.chip = (2 * x + y).astype(jnp.int32)
        self.pending = {}

    def _pair_copies(self, n_arr):
        def make(refs, send_sems, recv_sems):
            sibling, _ = _peer(1)
            other = 1 - lax.axis_index("c")
            srcs, lands = refs[:n_arr], refs[n_arr:]
            pick = lambda ref, ch: ref.at[ch, other] if len(ref.shape) == 4 else ref.at[ch]
            return [_remote(pick(srcs[a], ch), lands[a].at[ch], send_sems.at[a * N_CHIP + ch],
                            recv_sems.at[a * N_CHIP + ch], sibling)
                    for a in range(n_arr) for ch in range(N_CHIP)]
        return make

    def _chip_copies(self, n_arr, with_slab):
        def make(refs, send_sems, recv_sems):
            x, y = lax.axis_index("x"), lax.axis_index("y")
            my_chip = 2 * x + y
            srcs, lands = refs[:n_arr], refs[n_arr:2 * n_arr]
            copies = []
            for j, k in enumerate((2, 4, 6)):
                peer, peer_idx = _peer(k)
                for a in range(n_arr):
                    copies.append(_remote(srcs[a].at[peer_idx // 2], lands[a].at[my_chip],
                                          send_sems.at[3 * a + j], recv_sems.at[3 * a + j], peer))
            if with_slab:
                slab, slab_land = refs[2 * n_arr], refs[2 * n_arr + 1]
                for k in range(1, N_DEV):
                    peer, _ = _peer(k)
                    copies.append(_remote(slab, slab_land.at[_my_index()], send_sems.at[3 * n_arr + k - 1],
                                          recv_sems.at[3 * n_arr + k - 1], peer))
            return copies
        return make

    def push_pairs(self, tag, for_sibling):
        n_arr = len(for_sibling)
        lands = [lax.empty((N_CHIP,) + a.shape[-2:], a.dtype) for a in for_sibling]
        make = self._pair_copies(n_arr)
        arrays, sems, token = _comm_call(tag + "_start", list(for_sibling) + lands, start=(n_arr * N_CHIP, make))
        self.pending[tag] = (arrays, sems, make, n_arr)
        return token

    def pairs_done(self, tag, after):
        arrays, sems, make, n_arr = self.pending.pop(tag)
        arrays, _, _ = _comm_call(tag + "_wait", arrays, wait=(sems[0], sems[1], make), after=after)
        return arrays[n_arr:]

    def push_chips(self, tag, pair_sums, slab=None):
        n_arr = len(pair_sums)
        arrays = list(pair_sums) + [lax.empty(a.shape, a.dtype) for a in pair_sums]
        n_copies = 3 * n_arr
        if slab is not None:
            arrays += [slab, lax.empty((N_DEV,) + slab.shape, slab.dtype)]
            n_copies += N_DEV - 1
        make = self._chip_copies(n_arr, slab is not None)
        arrays, sems, token = _comm_call(tag + "_start", arrays, start=(n_copies, make))
        self.pending[tag] = (arrays, sems, make, n_arr)
        return token

    def chips_done(self, tag, after):
        arrays, sems, make, n_arr = self.pending.pop(tag)
        arrays, _, _ = _comm_call(tag + "_wait", arrays, wait=(sems[0], sems[1], make), after=after)
        return arrays[:n_arr], arrays[n_arr:2 * n_arr], arrays[2 * n_arr:]


def _pair_add(mine, theirs, name, core=None):
    _, r, c_dim = theirs.shape
    tr = r if r * c_dim <= MIB else 1 << ((MIB // c_dim).bit_length() - 1)
    assert r % tr == 0
    stride = 1 if core is None else 2
    offset = jnp.zeros((1,), jnp.int32) if core is None else core.reshape(1)

    def body(off_ref, a_ref, b_ref, o_ref):
        o_ref[...] = (a_ref[...].astype(F32) + b_ref[...].astype(F32)).astype(BF16)

    blk = (None, tr, c_dim)
    return pl.pallas_call(
        body, name=name, out_shape=jax.ShapeDtypeStruct(theirs.shape, BF16),
        grid_spec=pltpu.PrefetchScalarGridSpec(
            num_scalar_prefetch=1, grid=(N_CHIP, r // tr),
            in_specs=[pl.BlockSpec(blk, lambda ch, i, off_ref: (stride * ch + off_ref[0], i, 0)),
                      pl.BlockSpec(blk, lambda ch, i, off_ref: (ch, i, 0))],
            out_specs=pl.BlockSpec(blk, lambda ch, i, off_ref: (ch, i, 0))),
        compiler_params=_params(("arbitrary", "arbitrary"), 40),
    )(offset, mine, theirs)


def _slab_exchange(slab, name):
    def body(slab_in, slab_out, send_sems, recv_sems, local_sem):
        me = _my_index()
        local = pltpu.make_async_copy(slab_in, slab_out.at[me], local_sem)
        local.start()
        sends = []
        for k in range(1, N_DEV):
            peer, _ = _peer(k)
            sends.append(_remote(slab_in, slab_out.at[me], send_sems.at[k - 1], recv_sems.at[k - 1], peer))
        for cp in sends:
            cp.start()
        for k in range(1, N_DEV):
            peer, peer_idx = _peer(k)
            slot = slab_out.at[peer_idx]
            _remote(slot, slot, send_sems.at[k - 1], recv_sems.at[k - 1], peer).wait_recv()
        for cp in sends:
            cp.wait_send()
        local.wait()

    return pl.pallas_call(
        body, name=name, out_shape=jax.ShapeDtypeStruct((N_DEV,) + slab.shape, slab.dtype),
        in_specs=[HBM_SPEC], out_specs=HBM_SPEC,
        scratch_shapes=[pltpu.SemaphoreType.DMA((N_DEV - 1,)), pltpu.SemaphoreType.DMA((N_DEV - 1,)),
                        pltpu.SemaphoreType.DMA],
        compiler_params=pltpu.CompilerParams(has_side_effects=True),
    )(slab)


def _adamw_math(w, g, m, v):
    m = ADAM_B1 * m + (1.0 - ADAM_B1) * g
    v = ADAM_B2 * v + (1.0 - ADAM_B2) * (g * g)
    m_hat = m / (1.0 - ADAM_B1 ** ADAM_STEP)
    v_hat = v / (1.0 - ADAM_B2 ** ADAM_STEP)
    delta = -ADAM_LR * (m_hat / (jnp.sqrt(v_hat) + ADAM_EPS) + ADAM_WD * w)
    return delta, m, v


def _adamw(parts, own, place, w, m, v, name, dep=None):
    n_parts = parts.shape[0]
    r, c = w.shape
    budget = 280 * 1024
    tr = r if r * c <= budget else 1 << ((budget // c).bit_length() - 1)
    assert r % tr == 0

    def body(place_ref, p_ref, own_ref, w_ref, m_ref, v_ref, g_ref, d_ref, nm_ref, nv_ref):
        mine = own_ref[...].astype(F32)
        g = None
        for i in range(n_parts):
            term = jnp.where(place_ref[0] == i, mine, p_ref[i].astype(F32))
            g = term if g is None else g + term
        delta, nm, nv = _adamw_math(w_ref[...], g, m_ref[...], v_ref[...])
        g_ref[...] = g
        d_ref[...] = delta
        nm_ref[...] = nm
        nv_ref[...] = nv

    blk = pl.BlockSpec((tr, c), lambda i, place_ref: (i, 0))
    shape = jax.ShapeDtypeStruct((r, c), F32)
    in_specs = [pl.BlockSpec((n_parts, tr, c), lambda i, place_ref: (0, i, 0)),
                pl.BlockSpec((None, tr, c), lambda i, place_ref: (place_ref[1], i, 0)), blk, blk, blk]
    operands = [parts, own, w, m, v]
    if dep is not None:
        body = _drop_arg(body, 1 + len(operands))
        in_specs.append(pl.BlockSpec((SUB, HEAD), lambda i, place_ref: (0, 0)))
        operands.append(dep)
    return pl.pallas_call(
        body, name=name, out_shape=[shape] * 4,
        grid_spec=pltpu.PrefetchScalarGridSpec(
            num_scalar_prefetch=1, grid=(r // tr,), in_specs=in_specs, out_specs=[blk] * 4),
        compiler_params=_params(("arbitrary",), 48),
    )(place, *operands)


def kernel(x, p, g_pre, w_in, w_s, b_s, ln_v_g, ln_v_b, g_q, g_k, rel_bias, g_out_a, g_out_b, w_out, g_ple, w_ple_gate, w_ple_up, loss_target, m_g_pre, m_w_in, m_w_s, m_b_s, m_ln_v_g, m_ln_v_b, m_g_q, m_g_k, m_rel_bias, m_g_out_a, m_g_out_b, m_w_out, m_g_ple, m_w_ple_gate, m_w_ple_up, v_g_pre, v_w_in, v_w_s, v_b_s, v_ln_v_g, v_ln_v_b, v_g_q, v_g_k, v_rel_bias, v_g_out_a, v_g_out_b, v_w_out, v_g_ple, v_w_ple_gate, v_w_ple_up):
    args = dict(locals())
    small = {"g_pre": g_pre, "w_s": w_s[0], "b_s": b_s[0], "ln_v_g": ln_v_g, "ln_v_b": ln_v_b, "g_q": g_q,
             "g_k": g_k, "rel_bias": rel_bias, "g_out_a": g_out_a, "g_out_b": g_out_b, "g_ple": g_ple}
    big_names = ("w_in", "w_out", "w_ple_gate", "w_ple_up")
    big = {k: args[k][0] for k in big_names}

    gathered = _gather_weights([big[k].astype(BF16) for k in big_names])
    ex = _GradExchange()
    results = {}

    def big_adamw(k, delivered, dep=None):
        mine, theirs = delivered[k]
        place = jnp.stack([ex.chip, ex.chip])
        return _adamw(theirs, mine, place, big[k], args["m_" + k][0], args["v_" + k][0], "adamw_" + k, dep=dep)

    def while_last_travels(token, delivered):
        done = []
        for k in big_names[1:]:
            results[k] = big_adamw(k, delivered, dep=token)
            done.append(results[k][0])
        return done

    loss, grad_x, small_parts, delivered, slabs, _ = _local_step(
        x[0], p[0, 0], loss_target[0], small, *gathered, ex, while_last_travels)
    results["w_in"] = big_adamw("w_in", delivered)
    for k in big_names:
        results[k] = [t[None] for t in results[k]]

    squeeze = lambda t: {k: (t[k][0] if k in ("w_s", "b_s") else t[k]) for k in SMALL_NAMES}
    small_m = squeeze({k: args["m_" + k] for k in SMALL_NAMES})
    small_v = squeeze({k: args["v_" + k] for k in SMALL_NAMES})
    me = _my_index().astype(jnp.int32)
    place = jnp.stack([me, jnp.zeros((), jnp.int32)])
    last_slab = _pack_small(small_parts, SMALL_LAST)
    groups = ((SMALL_EARLY, slabs[1], slabs[0], "adamw_small"),
              (SMALL_LAST, _slab_exchange(last_slab, "last_exchange"), last_slab, "adamw_last"))
    for names_, parts, own, call_name in groups:
        packed = _adamw(parts, own[None], place, _pack_small(small, names_), _pack_small(small_m, names_),
                        _pack_small(small_v, names_), call_name)
        for idx in range(4):
            tree = _unpack_small(packed[idx], small, names_)
            for k in names_:
                results.setdefault(k, [None] * 4)[idx] = tree[k].reshape(args[k].shape)

    names = ("g_pre", "w_in", "w_s", "b_s", "ln_v_g", "ln_v_b", "g_q", "g_k", "rel_bias", "g_out_a", "g_out_b",
             "w_out", "g_ple", "w_ple_gate", "w_ple_up")
    total = lax.psum(loss, AXES)
    out = [total, grad_x[None]]
    for idx in range(4):
        out += [results[k][idx] for k in names]
    return tuple(out)
```

```python
import functools
import math

import numpy as np
import jax
import jax.numpy as jnp
from jax import lax
from jax.experimental import pallas as pl
from jax.experimental.pallas import tpu as pltpu

F32 = jnp.float32
BF16 = jnp.bfloat16
EPS = 1e-6
NEG_INF = -1e30
HEAD = 128
DILATIONS = (1, 4, 16)
NUM_BUCKETS = 32
MAX_DISTANCE = 2048
N_SEG = 7
ADAM_LR = 0.001
ADAM_B1 = 0.9
ADAM_B2 = 0.999
ADAM_EPS = 1e-08
ADAM_WD = 0.01
ADAM_STEP = 10
AXES = ("x", "y", "c")
N_DEV = 8
MIB = 1 << 20

SUB = 8

CHUNK_ORDER = np.array([16 * (r % SUB) + r // SUB for r in range(HEAD)])
BLOCK_ORDER = {
    1: CHUNK_ORDER,
    4: np.array([32 * (r // 32) + 4 * (r % SUB) + (r // SUB) % 4 for r in range(HEAD)]),
    16: np.arange(HEAD),
}

NT_DIMS = (((1,), (1,)), ((), ()))
TN_DIMS = (((0,), (0,)), ((), ()))
NN_DIMS = (((1,), (0,)), ((), ()))


def _params(semantics, vmem_mib):
    return pltpu.CompilerParams(dimension_semantics=semantics, vmem_limit_bytes=vmem_mib * MIB)


def _gelu(a):
    return 0.5 * a * (1.0 + lax.erf(a * (2.0 ** -0.5)))


def _gelu_grad(a):
    return 0.5 * (1.0 + lax.erf(a * (2.0 ** -0.5))) + a * jnp.exp(-0.5 * a * a) * ((2.0 * math.pi) ** -0.5)


def _silu_and_grad(a):
    s = jax.nn.sigmoid(a)
    return a * s, s * (1.0 + a * (1.0 - s))


def _rms(v):
    return lax.rsqrt(jnp.mean(v * v, axis=-1, keepdims=True) + EPS)


def _rms_bwd(dy, v, r, g):
    gy = dy * g
    return r * gy - v * (r * r * r) * jnp.mean(gy * v, axis=-1, keepdims=True)


def _dot(a, b, dims=NN_DIMS):
    return lax.dot_general(a, b, dims, preferred_element_type=F32)


def _lane_pick(cols, width):
    rows = cols[0].shape[0]
    lane = lax.broadcasted_iota(jnp.int32, (rows, width), 1)
    out = jnp.zeros((rows, width), F32)
    for h, col in enumerate(cols):
        out = jnp.where(lane == h, col, out)
    return out


def _chunk_perm():
    return jnp.asarray(np.eye(HEAD, dtype=np.float32)[CHUNK_ORDER], BF16)


def _unpermute_f32(p, v):
    hi = v.astype(BF16)
    rest = v - hi.astype(F32)
    mid = rest.astype(BF16)
    lo = (rest - mid.astype(F32)).astype(BF16)
    return _dot(p, hi, TN_DIMS) + _dot(p, mid, TN_DIMS) + _dot(p, lo, TN_DIMS)


def _rmsnorm_fwd(x, g, name, permute, dep=None):
    s, d = x.shape
    tm = HEAD

    def body(x_ref, g_ref, p_ref, o_ref):
        v = x_ref[...]
        out = (v * _rms(v) * g_ref[...]).astype(BF16)
        if permute:
            out = _dot(p_ref[...], out).astype(BF16)
        o_ref[...] = out

    in_specs = [pl.BlockSpec((tm, d), lambda i: (i, 0)), pl.BlockSpec((1, d), lambda i: (0, 0)),
                pl.BlockSpec((HEAD, HEAD), lambda i: (0, 0))]
    operands = [x, g, _chunk_perm()]
    if dep is not None:
        body = _drop_arg(body, len(operands))
        in_specs.append(DEP_SPEC)
        operands.append(dep)
    return pl.pallas_call(
        body, name=name, grid=(s // tm,),
        out_shape=jax.ShapeDtypeStruct((s, d), BF16), in_specs=in_specs,
        out_specs=pl.BlockSpec((tm, d), lambda i: (i, 0)),
        compiler_params=_params(("arbitrary",), 40),
    )(*operands)


def _rmsnorm_bwd(dy, v, g, res, name, dy_permuted, with_bf16):
    s, d = v.shape
    tm = HEAD
    perm = _chunk_perm()

    def body(dy_ref, v_ref, g_ref, res_ref, p_ref, *outs):
        dx_ref, dg_ref = outs[0], outs[-1]
        i = pl.program_id(0)
        vv, dyv = v_ref[...], dy_ref[...]
        if dy_permuted:
            dyv = _unpermute_f32(p_ref[...], dyv)
        r = _rms(vv)
        dx = res_ref[...] + _rms_bwd(dyv, vv, r, g_ref[...])
        dx_ref[...] = dx
        if with_bf16:
            dxb = dx.astype(BF16)
            outs[1][...] = dxb
            outs[2][...] = _dot(p_ref[...], dxb).astype(BF16)

        @pl.when(i == 0)
        def _():
            dg_ref[...] = jnp.zeros_like(dg_ref)

        dg_ref[...] += jnp.sum(dyv * vv * r, axis=0, keepdims=True)

    row = pl.BlockSpec((tm, d), lambda i: (i, 0))
    vec = pl.BlockSpec((1, d), lambda i: (0, 0))
    shapes = [jax.ShapeDtypeStruct((s, d), F32)]
    specs = [row]
    if with_bf16:
        shapes += [jax.ShapeDtypeStruct((s, d), BF16)] * 2
        specs += [row, row]
    shapes.append(jax.ShapeDtypeStruct((1, d), F32))
    specs.append(vec)
    return pl.pallas_call(
        body, name=name, grid=(s // tm,), out_shape=shapes,
        in_specs=[row, row, vec, row, pl.BlockSpec((HEAD, HEAD), lambda i: (0, 0))], out_specs=specs,
        compiler_params=_params(("arbitrary",), 40),
    )(dy, v, g, res, perm)


DEP_SPEC = pl.BlockSpec((SUB, HEAD), lambda *_: (0, 0))


def _drop_arg(body, pos):
    return lambda *refs: body(*refs[:pos], *refs[pos + 1:])


def _matmul(a, b, *, name, grid, a_spec, b_spec, dims, acc_shape, out_shapes, out_specs,
            extra=(), extra_specs=(), epilogue=None, vmem_mib=48, dep=None, prefetch=None, carry=None):
    nk = grid[2]
    n_user = len(extra)
    for unread, spec in ((dep, DEP_SPEC), (carry, HBM_SPEC)):
        if unread is not None:
            extra, extra_specs = tuple(extra) + (unread,), tuple(extra_specs) + (spec,)
    n_extra, n_out = len(extra), len(out_shapes)
    n_pre = 0 if prefetch is None else 1
    aliases = {} if carry is None else {n_pre + 2 + n_extra - 1: 0}

    def body(*refs):
        refs = refs[n_pre:]
        a_ref, b_ref = refs[0], refs[1]
        ex = refs[2:2 + n_user]
        outs = refs[2 + n_extra:2 + n_extra + n_out]
        acc = refs[-1]
        k = pl.program_id(2)

        @pl.when(k == 0)
        def _():
            acc[...] = jnp.zeros_like(acc)

        av = a_ref[...]
        if av.dtype != BF16:
            av = av.astype(BF16)
        acc[...] += _dot(av, b_ref[...], dims)

        @pl.when(k == nk - 1)
        def _():
            if epilogue is None:
                outs[0][...] = acc[...].astype(outs[0].dtype)
            else:
                epilogue(acc, ex, outs)

    scratch = [pltpu.VMEM(acc_shape, F32)]
    params = _params(("parallel", "parallel", "arbitrary"), vmem_mib)
    if prefetch is None:
        return pl.pallas_call(
            body, name=name, grid=grid, out_shape=list(out_shapes),
            in_specs=[a_spec, b_spec, *extra_specs], out_specs=list(out_specs),
            scratch_shapes=scratch, compiler_params=params, input_output_aliases=aliases,
        )(a, b, *extra)
    return pl.pallas_call(
        body, name=name, out_shape=list(out_shapes),
        grid_spec=pltpu.PrefetchScalarGridSpec(
            num_scalar_prefetch=1, grid=grid, in_specs=[a_spec, b_spec, *extra_specs],
            out_specs=list(out_specs), scratch_shapes=scratch),
        compiler_params=params, input_output_aliases=aliases,
    )(prefetch, a, b, *extra)


def _tile(n, want):
    t = min(n, want)
    while n % t:
        t //= 2
    return t


def _rel_buckets(dil):
    order = BLOCK_ORDER[dil]
    qi = jnp.asarray(HEAD + order)
    kj = jnp.asarray(np.concatenate([order, HEAD + order]))
    delta = qi[:, None] - kj[None, :]
    band = (delta >= 0) & (delta <= HEAD)
    dist = jnp.clip(delta, 0, None) * dil
    max_exact = NUM_BUCKETS // 2
    dd = jnp.maximum(dist, 1).astype(F32)
    large = max_exact + (jnp.log(dd / max_exact) / math.log(MAX_DISTANCE / max_exact)
                         * (NUM_BUCKETS - max_exact)).astype(jnp.int32)
    large = jnp.minimum(large, NUM_BUCKETS - 1)
    bucket = jnp.where(dist < max_exact, dist, large)
    return jnp.where(band, bucket, -1).astype(jnp.int32)


def _bias_build(rel_bias, buckets, n_heads):
    nd = buckets.shape[0]

    def body(rb_ref, bk_ref, o_ref):
        for c in range(nd):
            def per_head(h, carry, c=c):
                bk = bk_ref[c]
                acc = jnp.where(bk < 0, NEG_INF, 0.0).astype(F32)
                for b in range(NUM_BUCKETS):
                    acc = jnp.where(bk == b, rb_ref[b, h], acc)
                o_ref[c, h] = acc
                return carry

            lax.fori_loop(0, n_heads, per_head, 0)

    return pl.pallas_call(
        body, name="bias_build",
        out_shape=jax.ShapeDtypeStruct((nd, n_heads, HEAD, 2 * HEAD), F32),
        in_specs=[pl.BlockSpec(memory_space=pltpu.SMEM), pl.BlockSpec(memory_space=pltpu.VMEM)],
        out_specs=pl.BlockSpec(memory_space=pltpu.VMEM),
    )(rel_bias, buckets)


def _bias_grad(ds_all, buckets, n_heads):
    nd = buckets.shape[0]

    def body(ds_ref, bk_ref, o_ref):
        def per_pair(i, carry):
            b, h = i // n_heads, i % n_heads
            tot = jnp.zeros((), F32)
            for c in range(nd):
                tot = tot + jnp.sum(jnp.where(bk_ref[c] == b, ds_ref[c, h], 0.0))
            o_ref[b, h] = tot
            return carry

        lax.fori_loop(0, NUM_BUCKETS * n_heads, per_pair, 0)

    return pl.pallas_call(
        body, name="bias_grad",
        out_shape=jax.ShapeDtypeStruct((NUM_BUCKETS, n_heads), F32),
        in_specs=[pl.BlockSpec(memory_space=pltpu.VMEM), pl.BlockSpec(memory_space=pltpu.VMEM)],
        out_specs=pl.BlockSpec(memory_space=pltpu.SMEM),
    )(ds_all, buckets)


def _qkv_prep(proj, g_q, g_k, w, dep=None):
    s = proj.shape[0]
    n_heads = w // HEAD
    tm = HEAD

    def body(q_ref, k_ref, v_ref, gq_ref, gk_ref, qn_ref, kn_ref, vb_ref):
        gq = gq_ref[...] * (HEAD ** -0.5)
        gk = gk_ref[...]
        for h in range(n_heads):
            sl = slice(h * HEAD, (h + 1) * HEAD)
            q = q_ref[:, sl]
            k = k_ref[:, sl]
            qn_ref[:, sl] = q * _rms(q) * gq
            kn_ref[:, sl] = k * _rms(k) * gk
        vb_ref[...] = v_ref[...]

    seg = lambda j: pl.BlockSpec((tm, w), lambda i, j=j: (i, j))
    vec = pl.BlockSpec((1, HEAD), lambda i: (0, 0))
    out = pl.BlockSpec((tm, w), lambda i: (i, 0))
    in_specs = [seg(3), seg(4), seg(5), vec, vec]
    operands = [proj, proj, proj, g_q, g_k]
    if dep is not None:
        body = _drop_arg(body, len(operands))
        in_specs.append(DEP_SPEC)
        operands.append(dep)
    return pl.pallas_call(
        body, name="qkv_prep", grid=(s // tm,),
        out_shape=[jax.ShapeDtypeStruct((s, w), F32)] * 3,
        in_specs=in_specs, out_specs=[out, out, out],
        compiler_params=_params(("arbitrary",), 40),
    )(*operands)


class _BlockView:
    def __init__(self, s, dil):
        assert s % (HEAD * dil) == 0 and dil in BLOCK_ORDER
        self.nb = s // (HEAD * dil)
        if dil == 1:
            self.lead, self.block = (s,), (HEAD,)
            self.index = lambda r, n: (n,)
        elif dil == 4:
            self.lead, self.block = (s // 512, 4, 4, 4, SUB), (None, 4, 4, None, SUB)
            self.index = lambda r, n: (n, 0, 0, r, 0)
        else:
            self.lead, self.block = (s // 2048, 16, 16, SUB), (None, 16, None, SUB)
            self.index = lambda r, n: (n, 0, r, 0)

    def view(self, t):
        return t.reshape(self.lead + (t.shape[-1],))

    def spec(self, width, block_of):
        return pl.BlockSpec(self.block + (width,), lambda r, n: self.index(r, block_of(r, n)) + (0,))


def _rows(ref, lanes=slice(None)):
    v = ref[(slice(None),) * (len(ref.shape) - 1) + (lanes,)]
    return v.reshape(HEAD, v.shape[-1])


def _set_rows(ref, lanes, value):
    ref[(slice(None),) * (len(ref.shape) - 1) + (lanes,)] = value.reshape(ref.shape[:-1] + (value.shape[-1],))


def _attn_fwd(qn, kn, vb, bias, dil, name):
    s, w = qn.shape
    n_heads = w // HEAD
    bv = _BlockView(s, dil)

    def body(q_ref, kc_ref, kp_ref, vc_ref, vp_ref, bias_ref, o_ref, lse_ref):
        n = pl.program_id(1)
        lses = []
        for h in range(n_heads):
            sl = slice(h * HEAD, (h + 1) * HEAD)
            q = _rows(q_ref, sl).astype(BF16)
            kp, kc = _rows(kp_ref, sl).astype(BF16), _rows(kc_ref, sl).astype(BF16)
            vp, vc = _rows(vp_ref, sl).astype(BF16), _rows(vc_ref, sl).astype(BF16)
            s_p = _dot(q, kp, NT_DIMS) + bias_ref[h, :, :HEAD]
            s_p = jnp.where(n > 0, s_p, NEG_INF)
            s_c = _dot(q, kc, NT_DIMS) + bias_ref[h, :, HEAD:]
            m = jnp.maximum(jnp.max(s_p, axis=-1, keepdims=True), jnp.max(s_c, axis=-1, keepdims=True))
            e_p = jnp.exp(s_p - m)
            e_c = jnp.exp(s_c - m)
            den = jnp.sum(e_p, axis=-1, keepdims=True) + jnp.sum(e_c, axis=-1, keepdims=True)
            o = _dot(e_p.astype(BF16), vp) + _dot(e_c.astype(BF16), vc)
            _set_rows(o_ref, sl, o / den)
            lses.append(m + jnp.log(den))
        _set_rows(lse_ref, slice(None), _lane_pick(lses, HEAD))

    cur = bv.spec(w, lambda r, n: n)
    prev = bv.spec(w, lambda r, n: jnp.maximum(n - 1, 0))
    o, lse = pl.pallas_call(
        body, name=name, grid=(dil, bv.nb),
        out_shape=[jax.ShapeDtypeStruct(bv.lead + (w,), F32), jax.ShapeDtypeStruct(bv.lead + (HEAD,), F32)],
        in_specs=[cur, cur, prev, cur, prev,
                  pl.BlockSpec((n_heads, HEAD, 2 * HEAD), lambda r, n: (0, 0, 0))],
        out_specs=[cur, bv.spec(HEAD, lambda r, n: n)],
        compiler_params=_params(("arbitrary", "arbitrary"), 48),
    )(bv.view(qn), bv.view(kn), bv.view(kn), bv.view(vb), bv.view(vb), bias)
    return o.reshape(s, w), lse.reshape(s, HEAD)


def _attn_bwd(qn, kn, vb, dyb, lse, delta, bias, dil, name, dep=None):
    s, w = qn.shape
    n_heads = w // HEAD
    bv = _BlockView(s, dil)
    nb = bv.nb

    def body(q_ref, kc_ref, kp_ref, vc_ref, vp_ref, dy_ref, lse_ref, dl_ref, bias_ref,
             dq_ref, dk_ref, dv_ref, ds_ref, carry_k, carry_v):
        r = pl.program_id(0)
        step = pl.program_id(1)
        blk = nb - 1 - step

        @pl.when((r == 0) & (step == 0))
        def _():
            ds_ref[...] = jnp.zeros_like(ds_ref)

        @pl.when(step == 0)
        def _():
            carry_k[...] = jnp.zeros_like(carry_k)
            carry_v[...] = jnp.zeros_like(carry_v)

        tots = _rows(lse_ref)
        dls = _rows(dl_ref)
        for h in range(n_heads):
            sl = slice(h * HEAD, (h + 1) * HEAD)
            q, dy = _rows(q_ref, sl).astype(BF16), _rows(dy_ref, sl).astype(BF16)
            kp, kc = _rows(kp_ref, sl).astype(BF16), _rows(kc_ref, sl).astype(BF16)
            vp, vc = _rows(vp_ref, sl).astype(BF16), _rows(vc_ref, sl).astype(BF16)
            tot = tots[:, h:h + 1]
            dl = dls[:, h:h + 1]
            s_p = _dot(q, kp, NT_DIMS) + bias_ref[h, :, :HEAD]
            s_p = jnp.where(blk > 0, s_p, NEG_INF)
            s_c = _dot(q, kc, NT_DIMS) + bias_ref[h, :, HEAD:]
            p_p = jnp.exp(s_p - tot)
            p_c = jnp.exp(s_c - tot)
            ds_p = p_p * (_dot(dy, vp, NT_DIMS) - dl)
            ds_c = p_c * (_dot(dy, vc, NT_DIMS) - dl)
            ds_ref[h, :, :HEAD] += ds_p
            ds_ref[h, :, HEAD:] += ds_c
            ds_pb, ds_cb = ds_p.astype(BF16), ds_c.astype(BF16)
            _set_rows(dq_ref, sl, _dot(ds_pb, kp) + _dot(ds_cb, kc))
            _set_rows(dk_ref, sl, _dot(ds_cb, q, TN_DIMS) + carry_k[:, sl])
            carry_k[:, sl] = _dot(ds_pb, q, TN_DIMS)
            _set_rows(dv_ref, sl, _dot(p_c.astype(BF16), dy, TN_DIMS) + carry_v[:, sl])
            carry_v[:, sl] = _dot(p_p.astype(BF16), dy, TN_DIMS)

    cur = bv.spec(w, lambda r, n: nb - 1 - n)
    prev = bv.spec(w, lambda r, n: jnp.maximum(nb - 2 - n, 0))
    stat = bv.spec(HEAD, lambda r, n: nb - 1 - n)
    whole = pl.BlockSpec((n_heads, HEAD, 2 * HEAD), lambda r, n: (0, 0, 0))
    big = jax.ShapeDtypeStruct(bv.lead + (w,), F32)
    in_specs = [cur, cur, prev, cur, prev, cur, stat, stat, whole]
    operands = [bv.view(qn), bv.view(kn), bv.view(kn), bv.view(vb), bv.view(vb), bv.view(dyb), bv.view(lse),
                bv.view(delta), bias]
    if dep is not None:
        body = _drop_arg(body, len(operands))
        in_specs.append(DEP_SPEC)
        operands.append(dep)
    dq, dk, dv, ds = pl.pallas_call(
        body, name=name, grid=(dil, nb),
        out_shape=[big, big, big, jax.ShapeDtypeStruct((n_heads, HEAD, 2 * HEAD), F32)],
        in_specs=in_specs, out_specs=[cur, cur, cur, whole],
        scratch_shapes=[pltpu.VMEM((HEAD, w), F32), pltpu.VMEM((HEAD, w), F32)],
        compiler_params=_params(("arbitrary", "arbitrary"), 56),
    )(*operands)
    return dq.reshape(s, w), dk.reshape(s, w), dv.reshape(s, w), ds


def _qkv_bwd(dproj, proj, dqs, dks, dvs, g_q, g_k, w):
    s = proj.shape[0]
    n_heads = w // HEAD
    tm = HEAD

    def body(dproj_hbm, q_ref, k_ref, gq_ref, gk_ref, *rest):
        dq_refs, dk_refs, dv_refs = rest[0:3], rest[3:6], rest[6:9]
        out_ref, dgq_ref, dgk_ref = rest[9:12]
        i = pl.program_id(0)
        gq = gq_ref[...] * (HEAD ** -0.5)
        gk = gk_ref[...]
        acc_q = jnp.zeros((1, HEAD), F32)
        acc_k = jnp.zeros((1, HEAD), F32)
        for h in range(n_heads):
            sl = slice(h * HEAD, (h + 1) * HEAD)
            q, k = q_ref[:, sl], k_ref[:, sl]
            dqn = sum(t[:, sl].astype(F32) for t in dq_refs)
            dkn = sum(t[:, sl].astype(F32) for t in dk_refs)
            rq, rk = _rms(q), _rms(k)
            out_ref[:, h * HEAD:(h + 1) * HEAD] = _rms_bwd(dqn, q, rq, gq).astype(BF16)
            out_ref[:, w + h * HEAD:w + (h + 1) * HEAD] = _rms_bwd(dkn, k, rk, gk).astype(BF16)
            acc_q += jnp.sum(dqn * q * rq, axis=0, keepdims=True)
            acc_k += jnp.sum(dkn * k * rk, axis=0, keepdims=True)
        out_ref[:, 2 * w:] = sum(t[...].astype(F32) for t in dv_refs).astype(BF16)

        @pl.when(i == 0)
        def _():
            dgq_ref[...] = jnp.zeros_like(dgq_ref)
            dgk_ref[...] = jnp.zeros_like(dgk_ref)

        dgq_ref[...] += acc_q * (HEAD ** -0.5)
        dgk_ref[...] += acc_k

    seg = lambda j: pl.BlockSpec((tm, w), lambda i, j=j: (i, j))
    vec = pl.BlockSpec((1, HEAD), lambda i: (0, 0))
    row = pl.BlockSpec((tm, w), lambda i: (i, 0))
    return pl.pallas_call(
        body, name="qkv_bwd", grid=(s // tm,),
        out_shape=[jax.ShapeDtypeStruct(dproj.shape, BF16),
                   jax.ShapeDtypeStruct((1, HEAD), F32), jax.ShapeDtypeStruct((1, HEAD), F32)],
        in_specs=[pl.BlockSpec(memory_space=pl.ANY), seg(3), seg(4), vec, vec] + [row] * 9,
        out_specs=[pl.BlockSpec((tm, 3 * w), lambda i: (i, 1)), vec, vec],
        input_output_aliases={0: 0},
        compiler_params=_params(("arbitrary",), 48),
    )(dproj, proj, proj, g_q, g_k, *dqs, *dks, *dvs)


def _mixer_a(au, av, ws_ref, bst_ref, lng, lnb, z_scr, ln_scr):
    n_groups = au.shape[1] // HEAD
    u = _gelu(au)
    gv = _gelu(av)
    mu = jnp.mean(gv, axis=-1, keepdims=True)
    xc = gv - mu
    rs = lax.rsqrt(jnp.mean(xc * xc, axis=-1, keepdims=True) + EPS)
    xhat = xc * rs
    ln_scr[...] = (xhat * lng + lnb).astype(BF16)
    causal = _causal_mask()
    for g in range(n_groups):
        sl = slice(g * HEAD, (g + 1) * HEAD)
        wm = jnp.where(causal, ws_ref[g], 0.0).astype(BF16)
        z_scr[:, sl] = _dot(wm, ln_scr[:, sl]) + bst_ref[:, g:g + 1]
    return u, xhat, rs


def _causal_mask():
    token = lambda r: 16 * (r % SUB) + r // SUB
    row = lax.broadcasted_iota(jnp.int32, (HEAD, HEAD), 0)
    col = lax.broadcasted_iota(jnp.int32, (HEAD, HEAD), 1)
    return token(col) <= token(row)


def _merge_b(o_refs, lse_refs, yb_scr):
    n_heads = yb_scr.shape[1] // HEAD
    lses = [t[...] for t in lse_refs]
    m = jnp.maximum(jnp.maximum(lses[0], lses[1]), lses[2])
    tot = m + jnp.log(sum(jnp.exp(t - m) for t in lses))
    alphas = [jnp.exp(t - tot) for t in lses]
    for h in range(n_heads):
        sl = slice(h * HEAD, (h + 1) * HEAD)
        yb_scr[:, sl] = sum(a[:, h:h + 1] * o[:, sl].astype(F32) for a, o in zip(alphas, o_refs))
    return tot


def _mix_fwd(proj, outs, lses, w_s, bst, ln_g, ln_b, g_a, g_b, w):
    s = proj.shape[0]
    n_groups = w // HEAD

    def body(au_ref, av_ref, az_ref, bz_ref, o1, o2, o3, l1, l2, l3, ws_ref, bst_ref,
             lng_ref, lnb_ref, ga_ref, gb_ref, p_ref, y_ref, z_scr, ln_scr, yb_scr):
        u, _, _ = _mixer_a(au_ref[...], av_ref[...], ws_ref, bst_ref, lng_ref[...], lnb_ref[...], z_scr, ln_scr)
        ya = u * z_scr[...]
        silu_a, _ = _silu_and_grad(az_ref[...])
        perm = p_ref[...]
        y_ref[:, :w] = _dot(perm, (ya * _rms(ya) * ga_ref[...] * silu_a).astype(BF16), TN_DIMS).astype(BF16)
        _merge_b((o1, o2, o3), (l1, l2, l3), yb_scr)
        yb = yb_scr[...]
        silu_b, _ = _silu_and_grad(bz_ref[...])
        y_ref[:, w:] = _dot(perm, (yb * _rms(yb) * gb_ref[...] * silu_b).astype(BF16), TN_DIMS).astype(BF16)

    seg = lambda j: pl.BlockSpec((HEAD, w), lambda i, j=j: (i, j))
    row = pl.BlockSpec((HEAD, w), lambda i: (i, 0))
    stat = pl.BlockSpec((HEAD, HEAD), lambda i: (i, 0))
    vec = pl.BlockSpec((1, w), lambda i: (0, 0))
    return pl.pallas_call(
        body, name="mix_fwd", grid=(s // HEAD,),
        out_shape=jax.ShapeDtypeStruct((s, 2 * w), BF16),
        in_specs=[seg(0), seg(1), seg(2), seg(6), row, row, row, stat, stat, stat,
                  pl.BlockSpec((n_groups, HEAD, HEAD), lambda i: (0, 0, 0)),
                  pl.BlockSpec((HEAD, n_groups), lambda i: (0, 0)), vec, vec, vec, vec,
                  pl.BlockSpec((HEAD, HEAD), lambda i: (0, 0))],
        out_specs=pl.BlockSpec((HEAD, 2 * w), lambda i: (i, 0)),
        scratch_shapes=[pltpu.VMEM((HEAD, w), F32), pltpu.VMEM((HEAD, w), BF16), pltpu.VMEM((HEAD, w), F32)],
        compiler_params=_params(("arbitrary",), 48),
    )(proj, proj, proj, proj, *outs, *lses, w_s, bst, ln_g, ln_b, g_a, g_b, _chunk_perm())


def _mix_bwd(proj, dy, outs, lses, w_s, bst, ln_g, ln_b, g_a, g_b, w):
    s = proj.shape[0]
    n_groups = w // HEAD

    def body(au_ref, av_ref, az_ref, bz_ref, dy_ref, o1, o2, o3, l1, l2, l3, ws_ref, bst_ref,
             lng_ref, lnb_ref, ga_ref, gb_ref,
             dproj_ref, dyb_ref, tot_ref, dl_ref, dws_ref, dbst_ref, dlng_ref, dlnb_ref, dga_ref, dgb_ref,
             z_scr, ln_scr, yb_scr, dz_scr, dln_scr):
        i = pl.program_id(0)

        @pl.when(i == 0)
        def _():
            for t in (dws_ref, dbst_ref, dlng_ref, dlnb_ref, dga_ref, dgb_ref):
                t[...] = jnp.zeros_like(t)

        au, av, az = au_ref[...], av_ref[...], az_ref[...]
        lng = lng_ref[...]
        u, xhat, rs = _mixer_a(au, av, ws_ref, bst_ref, lng, lnb_ref[...], z_scr, ln_scr)
        z = z_scr[...]
        ya = u * z
        ra = _rms(ya)
        silu_a, dsilu_a = _silu_and_grad(az)
        dya_all = dy_ref[:, :w]
        na = ya * ra * ga_ref[...]
        dna = dya_all * silu_a
        dproj_ref[:, 2 * w:3 * w] = (dya_all * na * dsilu_a).astype(BF16)
        dga_ref[...] += jnp.sum(dna * ya * ra, axis=0, keepdims=True)
        dya = _rms_bwd(dna, ya, ra, ga_ref[...])
        dproj_ref[:, :w] = (dya * z * _gelu_grad(au)).astype(BF16)
        dz_scr[...] = (dya * u).astype(BF16)

        causal = _causal_mask()
        for g in range(n_groups):
            sl = slice(g * HEAD, (g + 1) * HEAD)
            wm = jnp.where(causal, ws_ref[g], 0.0).astype(BF16)
            dz = dz_scr[:, sl]
            dln_scr[:, sl] = _dot(wm, dz, TN_DIMS)
            dws_ref[g] += jnp.where(causal, _dot(dz, ln_scr[:, sl], NT_DIMS), 0.0)
            dbst_ref[:, g:g + 1] += jnp.sum(dz.astype(F32), axis=-1, keepdims=True)
        dln = dln_scr[...]
        dlng_ref[...] += jnp.sum(dln * xhat, axis=0, keepdims=True)
        dlnb_ref[...] += jnp.sum(dln, axis=0, keepdims=True)
        gy = dln * lng
        dgv = rs * (gy - jnp.mean(gy, axis=-1, keepdims=True)
                    - xhat * jnp.mean(gy * xhat, axis=-1, keepdims=True))
        dproj_ref[:, w:2 * w] = (dgv * _gelu_grad(av)).astype(BF16)
        dproj_ref[:, 3 * w:6 * w] = jnp.zeros((HEAD, 3 * w), BF16)

        tot_ref[...] = _merge_b((o1, o2, o3), (l1, l2, l3), yb_scr)
        yb = yb_scr[...]
        rb = _rms(yb)
        bz = bz_ref[...]
        silu_b, dsilu_b = _silu_and_grad(bz)
        dyb_all = dy_ref[:, w:]
        dnb = dyb_all * silu_b
        dproj_ref[:, 6 * w:] = (dyb_all * yb * rb * gb_ref[...] * dsilu_b).astype(BF16)
        dgb_ref[...] += jnp.sum(dnb * yb * rb, axis=0, keepdims=True)
        dyb = _rms_bwd(dnb, yb, rb, gb_ref[...])
        dyb_ref[...] = dyb
        prod = dyb * yb
        dl_ref[...] = _lane_pick(
            [jnp.sum(prod[:, h * HEAD:(h + 1) * HEAD], axis=-1, keepdims=True) for h in range(n_groups)], HEAD)

    seg = lambda j: pl.BlockSpec((HEAD, w), lambda i, j=j: (i, j))
    row_w = pl.BlockSpec((HEAD, w), lambda i: (i, 0))
    stat = pl.BlockSpec((HEAD, HEAD), lambda i: (i, 0))
    vec = pl.BlockSpec((1, w), lambda i: (0, 0))
    ws_spec = pl.BlockSpec((n_groups, HEAD, HEAD), lambda i: (0, 0, 0))
    bst_spec = pl.BlockSpec((HEAD, n_groups), lambda i: (0, 0))
    vec_shape = jax.ShapeDtypeStruct((1, w), F32)
    return pl.pallas_call(
        body, name="mix_bwd", grid=(s // HEAD,),
        out_shape=[jax.ShapeDtypeStruct((s, N_SEG * w), BF16), jax.ShapeDtypeStruct((s, w), F32),
                   jax.ShapeDtypeStruct((s, HEAD), F32), jax.ShapeDtypeStruct((s, HEAD), F32),
                   jax.ShapeDtypeStruct((n_groups, HEAD, HEAD), F32), jax.ShapeDtypeStruct((HEAD, n_groups), F32),
                   vec_shape, vec_shape, vec_shape, vec_shape],
        in_specs=[seg(0), seg(1), seg(2), seg(6), pl.BlockSpec((HEAD, 2 * w), lambda i: (i, 0)),
                  row_w, row_w, row_w, stat, stat, stat, ws_spec, bst_spec, vec, vec, vec, vec],
        out_specs=[pl.BlockSpec((HEAD, N_SEG * w), lambda i: (i, 0)), row_w, stat, stat,
                   ws_spec, bst_spec, vec, vec, vec, vec],
        scratch_shapes=[pltpu.VMEM((HEAD, w), F32), pltpu.VMEM((HEAD, w), BF16), pltpu.VMEM((HEAD, w), F32),
                        pltpu.VMEM((HEAD, w), BF16), pltpu.VMEM((HEAD, w), F32)],
        compiler_params=_params(("arbitrary",), 56),
    )(proj, proj, proj, proj, dy, *outs, *lses, w_s, bst, ln_g, ln_b, g_a, g_b)


def _local_step(x, p, tgt, small, wg, ex, while_last_travels=None):
    s, d = x.shape
    n, _, c_in = wg.buffers["w_in"].shape
    assert n == N_DEV
    d_in = n * c_in
    w = d_in // N_SEG
    n_heads = w // HEAD
    p_dim, c_up = wg.buffers["w_ple_up"].shape[1:]
    assert s % (HEAD * DILATIONS[-1]) == 0 and w % HEAD == 0 and d == n * c_up == 2 * w

    wg.start("gather_in_pair", ["w_in"], (1,))
    token = wg.start("gather_in_chips", ["w_in"], wg.CHIPS)
    rest = ["w_out", "w_ple_gate", "w_ple_up"]
    token = wg.start("gather_rest", rest, (1,) + wg.CHIPS, after=[token])

    hn = _rmsnorm_fwd(x, small["g_pre"], "pre_norm", True, dep=token)
    tm, tk = _tile(s, 1024), _tile(d, 1024)

    def in_proj(shards, name, carry, dep=None):
        return _matmul(
            hn, wg.buffers["w_in"], name=name, grid=(s // tm, len(shards), d // tk), dims=NN_DIMS,
            prefetch=jnp.stack(shards).astype(jnp.int32),
            a_spec=pl.BlockSpec((tm, tk), lambda i, j, k, sh: (i, k)),
            b_spec=pl.BlockSpec((None, tk, c_in), lambda i, j, k, sh: (sh[j], k, 0)),
            acc_shape=(tm, c_in), out_shapes=[jax.ShapeDtypeStruct((s, d_in), F32)],
            out_specs=[pl.BlockSpec((tm, c_in), lambda i, j, k, sh: (i, sh[j]))], carry=carry, dep=dep)[0]

    me = wg.me
    wg.arrived("gather_in_pair", [hn])
    proj = in_proj([me, me ^ 1], "in_proj_pair", None)
    token = wg.forward("gather_in_chips", [proj])
    proj = in_proj([me ^ k for k in wg.CHIPS], "in_proj_chips", proj, dep=token)
    wg.forwarded("gather_in_chips", [proj])
    proj = in_proj([me ^ k ^ 1 for k in wg.CHIPS], "in_proj_forwarded", proj)
    win_g = wg.buffers["w_in"]
    token = wg.forward("gather_rest", [proj])

    qn, kn, vb = _qkv_prep(proj, small["g_q"], small["g_k"], w, dep=token)
    buckets = jnp.stack([_rel_buckets(dil) for dil in DILATIONS])
    bias = _bias_build(small["rel_bias"], buckets, n_heads)
    outs, lses = [], []
    for c, dil in enumerate(DILATIONS):
        o, l = _attn_fwd(qn, kn, vb, bias[c], dil, "attn_fwd_d%d" % dil)
        outs.append(o)
        lses.append(l)
    wg.forwarded("gather_rest", [outs[-1]])
    wout_g, wgate_g, wup_g = (wg.buffers[k] for k in rest)
    wout_f = wout_g.reshape(2 * w, d)
    wgate_f = wgate_g.reshape(d, d)

    ws_p = small["w_s"][:, CHUNK_ORDER][:, :, CHUNK_ORDER]
    bst = small["b_s"].T[CHUNK_ORDER]
    mix_args = (outs, lses, ws_p, bst, small["ln_v_g"], small["ln_v_b"], small["g_out_a"], small["g_out_b"], w)
    y = _mix_fwd(proj, *mix_args)

    tn = _tile(d, 1024)
    tk2 = _tile(2 * w, 1024)

    def resid_epilogue(acc, ex, outs_):
        outs_[0][...] = ex[0][...] + acc[...]

    h = _matmul(
        y, wout_f, name="out_proj", grid=(s // tm, d // tn, (2 * w) // tk2), dims=NN_DIMS,
        a_spec=pl.BlockSpec((tm, tk2), lambda i, j, k: (i, k)),
        b_spec=pl.BlockSpec((tk2, tn), lambda i, j, k: (k, j)),
        acc_shape=(tm, tn), out_shapes=[jax.ShapeDtypeStruct((s, d), F32)],
        out_specs=[pl.BlockSpec((tm, tn), lambda i, j, k: (i, j))],
        extra=(x,), extra_specs=(pl.BlockSpec((tm, tn), lambda i, j, k: (i, j)),),
        epilogue=resid_epilogue)[0]

    hn2 = _rmsnorm_fwd(h, small["g_ple"], "ple_norm", False)

    tmg = _tile(s, 512)

    def ple_epilogue(acc, ex, outs_):
        h_ref, p_ref, wup_ref, tgt_ref = ex
        dout_ref, dpre_ref, dup_ref, loss_ref = outs_
        gate = jax.nn.sigmoid(acc[...])
        up = _dot(p_ref[...].astype(BF16), wup_ref[...])
        err = h_ref[...] + gate * up - tgt_ref[...]
        dout = err * (1.0 / d)
        dout_ref[...] = dout
        dpre_ref[...] = (dout * up * gate * (1.0 - gate)).astype(BF16)
        dup_ref[...] = (dout * gate).astype(BF16)
        part = 0.5 * jnp.sum(err * err) * (1.0 / d)
        rr = lax.broadcasted_iota(jnp.int32, (8, HEAD), 0)
        cc = lax.broadcasted_iota(jnp.int32, (8, HEAD), 1)
        loss_ref[...] = jnp.where((rr == 0) & (cc == 0), part, 0.0)

    tile_ij = pl.BlockSpec((tmg, c_up), lambda i, j, k: (i, j))
    dout, dpre, dup, loss_parts = _matmul(
        hn2, wgate_f, name="ple_gate", grid=(s // tmg, n, d // tk), dims=NN_DIMS,
        a_spec=pl.BlockSpec((tmg, tk), lambda i, j, k: (i, k)),
        b_spec=pl.BlockSpec((tk, c_up), lambda i, j, k: (k, j)),
        acc_shape=(tmg, c_up),
        out_shapes=[jax.ShapeDtypeStruct((s, d), F32), jax.ShapeDtypeStruct((s, d), BF16),
                    jax.ShapeDtypeStruct((s, d), BF16), jax.ShapeDtypeStruct((s // tmg * 8, n * HEAD), F32)],
        out_specs=[tile_ij, tile_ij, tile_ij, pl.BlockSpec((8, HEAD), lambda i, j, k: (i, j))],
        extra=(h, p, wup_g, tgt),
        extra_specs=(tile_ij, pl.BlockSpec((tmg, p_dim), lambda i, j, k: (i, 0)),
                     pl.BlockSpec((None, p_dim, c_up), lambda i, j, k: (j, 0, 0)), tile_ij),
        epilogue=ple_epilogue)
    loss = jnp.sum(loss_parts)

    tks = _tile(s, 1024)
    g_wup = _matmul(
        p, dup, name="grad_w_up", grid=(1, n, s // tks), dims=TN_DIMS,
        a_spec=pl.BlockSpec((tks, p_dim), lambda i, j, k: (k, 0)),
        b_spec=pl.BlockSpec((tks, c_up), lambda i, j, k: (k, j)),
        acc_shape=(p_dim, c_up), out_shapes=[jax.ShapeDtypeStruct((n, p_dim, c_up), BF16)],
        out_specs=[pl.BlockSpec((None, p_dim, c_up), lambda i, j, k: (j, 0, 0))])[0]

    def tn_matmul(a, b, name):
        m_, n_ = a.shape[1], b.shape[1]
        bm, bn = _tile(m_, 1024), _tile(n_, 1024)
        return _matmul(
            a, b, name=name, grid=(m_ // bm, n_ // bn, s // tks), dims=TN_DIMS,
            a_spec=pl.BlockSpec((tks, bm), lambda i, j, k: (k, i)),
            b_spec=pl.BlockSpec((tks, bn), lambda i, j, k: (k, j)),
            acc_shape=(bm, bn), out_shapes=[jax.ShapeDtypeStruct((m_, n_), BF16)],
            out_specs=[pl.BlockSpec((bm, bn), lambda i, j, k: (i, j))])[0]

    def nt_matmul(a, b, name, out_dtype, dep=None):
        k_, n_ = a.shape[1], b.shape[0]
        bm, bn, bk = _tile(s, 1024), _tile(n_, 1024), _tile(k_, 1024)
        return _matmul(
            a, b, name=name, grid=(s // bm, n_ // bn, k_ // bk), dims=NT_DIMS,
            a_spec=pl.BlockSpec((bm, bk), lambda i, j, k: (i, k)),
            b_spec=pl.BlockSpec((bn, bk), lambda i, j, k: (j, k)),
            acc_shape=(bm, bn), out_shapes=[jax.ShapeDtypeStruct((s, n_), out_dtype)],
            out_specs=[pl.BlockSpec((bm, bn), lambda i, j, k: (i, j))], dep=dep)[0]

    by_core = lambda g: g.reshape((N_CHIP, 2) + g.shape[-2:])
    g_wgate = tn_matmul(hn2, dpre, "grad_w_gate").reshape(wgate_g.shape)
    dhn2 = nt_matmul(dpre, wgate_f, "ple_gate_bwd", F32)
    dh, dh_b, dh_bp, dg_ple = _rmsnorm_bwd(dhn2, h, small["g_ple"], dout, "ple_norm_bwd", False, True)
    g_wout = tn_matmul(y, dh_b, "grad_w_out").reshape(wout_g.shape)

    late = ("w_out", "w_ple_gate", "w_ple_up")
    late_parts = (g_wout, g_wgate, g_wup)
    token = ex.push_pairs("pair_late", [by_core(g) for g in late_parts])
    dy = nt_matmul(dh_bp, wout_f, "out_proj_bwd", F32, dep=token)
    (dproj, dyb, lse_tot, delta, dws, dbst, dlng, dlnb, dga, dgb) = _mix_bwd(proj, dy, *mix_args)
    both_columns, from_sibling = ex.pairs_done("pair_late", [dproj])
    pair_sums = [_pair_add(mine.reshape((N_DEV,) + mine.shape[-2:]), theirs, "pair_add_" + k, ex.core)
                 for k, mine, theirs in zip(late, both_columns, from_sibling)]
    token = ex.push_chips("chip_late", pair_sums)

    dqs, dks, dvs, dss = [], [], [], []
    for c, dil in enumerate(DILATIONS):
        dq, dk, dv, ds = _attn_bwd(qn, kn, vb, dyb, lse_tot, delta, bias[c], dil, "attn_bwd_d%d" % dil,
                                   dep=token if c == 0 else None)
        dqs.append(dq)
        dks.append(dk)
        dvs.append(dv)
        dss.append(ds)
    d_rel = _bias_grad(jnp.stack(dss), buckets, n_heads)
    dproj, dgq, dgk = _qkv_bwd(dproj, proj, dqs, dks, dvs, small["g_q"], small["g_k"], w)
    pair_sums, landed, _ = ex.chips_done("chip_late", [dproj])
    delivered = {k: (mine, theirs) for k, mine, theirs in zip(late, pair_sums, landed)}

    token_row = np.argsort(CHUNK_ORDER)
    dws = dws[:, token_row][:, :, token_row]
    dbst = dbst[token_row]
    small_grads = {
        "w_s": dws, "b_s": dbst.T, "ln_v_g": dlng, "ln_v_b": dlnb, "g_q": dgq, "g_k": dgk,
        "rel_bias": d_rel, "g_out_a": dga, "g_out_b": dgb, "g_ple": dg_ple,
    }

    bm = _tile(d, 1024)

    def grad_w_in(core, name, dep=None):
        return _matmul(
            hn, dproj, name=name, grid=(d // bm, N_CHIP, s // tks), dims=TN_DIMS, prefetch=core.reshape(1),
            a_spec=pl.BlockSpec((tks, bm), lambda i, j, k, core_ref: (k, i)),
            b_spec=pl.BlockSpec((tks, c_in), lambda i, j, k, core_ref: (k, 2 * j + core_ref[0])),
            acc_shape=(bm, c_in), out_shapes=[jax.ShapeDtypeStruct((N_CHIP, d, c_in), BF16)],
            out_specs=[pl.BlockSpec((None, bm, c_in), lambda i, j, k, core_ref: (j, i, 0))], dep=dep)[0]

    for_sibling = grad_w_in(1 - ex.core, "grad_w_in_sibling")
    token = ex.push_pairs("pair_in", [for_sibling])
    mine = grad_w_in(ex.core, "grad_w_in_mine", dep=token)
    _, from_sibling = ex.pairs_done("pair_in", [mine])
    pair_sum = _pair_add(mine, from_sibling[0], "pair_add_w_in")
    token = ex.push_chips("chip_in", [pair_sum], _pack_small(small_grads, SMALL_EARLY))

    dhn = _matmul(
        dproj, win_g, name="in_proj_bwd", grid=(s // tm, d // tn, n), dims=NT_DIMS,
        a_spec=pl.BlockSpec((tm, c_in), lambda i, j, k: (i, k)),
        b_spec=pl.BlockSpec((None, tn, c_in), lambda i, j, k: (k, j, 0)),
        acc_shape=(tm, tn), out_shapes=[jax.ShapeDtypeStruct((s, d), F32)],
        out_specs=[pl.BlockSpec((tm, tn), lambda i, j, k: (i, j))], dep=token)[0]
    grad_x, dg_pre = _rmsnorm_bwd(dhn, x, small["g_pre"], dh, "pre_norm_bwd", True, False)
    extra = while_last_travels(token, delivered) if while_last_travels is not None else []
    pair_sums, landed, slabs = ex.chips_done("chip_in", [grad_x] + list(extra))
    delivered["w_in"] = (pair_sums[0], landed[0])
    small_grads["g_pre"] = dg_pre
    return loss, grad_x, small_grads, delivered, slabs, extra


SMALL_EARLY = ("w_s", "b_s", "ln_v_g", "ln_v_b", "g_q", "g_k", "rel_bias", "g_out_a", "g_out_b", "g_ple")
SMALL_LAST = ("g_pre",)
SMALL_NAMES = SMALL_LAST + SMALL_EARLY


def _pack_small(tree, names):
    parts = []
    for name in names:
        flat = tree[name].astype(F32).reshape(-1)
        pad = (-flat.shape[0]) % HEAD
        parts.append(jnp.pad(flat, (0, pad)) if pad else flat)
    slab = jnp.concatenate(parts).reshape(-1, HEAD)
    pad_rows = (-slab.shape[0]) % 8
    return jnp.pad(slab, ((0, pad_rows), (0, 0))) if pad_rows else slab


def _unpack_small(slab, like, names):
    flat = slab.reshape(-1)
    out, off = {}, 0
    for name in names:
        size = like[name].size
        out[name] = flat[off:off + size].reshape(like[name].shape)
        off += size + (-size) % HEAD
    return out


def _peer(k):
    x, y, c = (lax.axis_index(a) for a in AXES)
    bits = ((k >> 2) & 1, (k >> 1) & 1, k & 1)
    px, py, pc = (1 - v if b else v for v, b in zip((x, y, c), bits))
    return (px, py, pc), 4 * px + 2 * py + pc


def _my_index():
    x, y, c = (lax.axis_index(a) for a in AXES)
    return 4 * x + 2 * y + c


N_CHIP = 4
HBM_SPEC = pl.BlockSpec(memory_space=pl.ANY)


def _remote(src, dst, send_sem, recv_sem, peer):
    return pltpu.make_async_remote_copy(src_ref=src, dst_ref=dst, send_sem=send_sem, recv_sem=recv_sem,
                                        device_id=peer, device_id_type=pl.DeviceIdType.MESH)


SEM_SPEC = pl.BlockSpec(memory_space=pltpu.SEMAPHORE)
HBM_ONLY = pl.BlockSpec(memory_space=pltpu.HBM)
DATAFLOW = pltpu.SideEffectType.DATAFLOW_SIDE_EFFECTING


def _comm_call(name, arrays, *, wait=None, start=None, after=()):
    n, n_after = len(arrays), len(after)

    def body(*refs):
        ins = refs[:n]
        pos = n
        if wait is not None:
            for cp in wait[2](ins, refs[pos], refs[pos + 1]):
                cp.wait()
            pos += 2
        outs = refs[pos + n_after:]
        if start is not None:
            for cp in start[1](ins, outs[0], outs[1]):
                cp.start()
        outs[-1][...] = jnp.zeros_like(outs[-1])

    operands = [pltpu.with_memory_space_constraint(a, pltpu.HBM) for a in arrays]
    in_specs = [HBM_ONLY] * n
    if wait is not None:
        operands += [wait[0], wait[1]]
        in_specs += [SEM_SPEC, SEM_SPEC]
    operands += list(after)
    in_specs += [HBM_SPEC] * n_after
    out_shape, out_specs = [], []
    if start is not None:
        out_shape += [pltpu.SemaphoreType.DMA((start[0],))] * 2
        out_specs += [SEM_SPEC, SEM_SPEC]
    first = len(out_shape)
    out_shape += [pltpu.HBM(a.shape, a.dtype) for a in arrays] + [jax.ShapeDtypeStruct((SUB, HEAD), F32)]
    out_specs += [HBM_ONLY] * n + [pl.BlockSpec(memory_space=pltpu.VMEM)]
    res = pl.pallas_call(
        body, name=name, out_shape=tuple(out_shape), in_specs=tuple(in_specs), out_specs=tuple(out_specs),
        input_output_aliases={i: first + i for i in range(n)},
        compiler_params=pltpu.CompilerParams(has_side_effects=DATAFLOW),
    )(*operands)
    sems = (res[0], res[1]) if start is not None else None
    return list(res[first:first + n]), sems, res[-1]


class _GradExchange:
    def __init__(self):
        x, y, c = (lax.axis_index(a) for a in AXES)
        self.core = c.astype(jnp.int32)
        self.chip = (2 * x + y).astype(jnp.int32)
        self.pending = {}

    def _pair_copies(self, n_arr):
        def make(refs, send_sems, recv_sems):
            sibling, _ = _peer(1)
            other = 1 - lax.axis_index("c")
            srcs, lands = refs[:n_arr], refs[n_arr:]
            pick = lambda ref, ch: ref.at[ch, other] if len(ref.shape) == 4 else ref.at[ch]
            return [_remote(pick(srcs[a], ch), lands[a].at[ch], send_sems.at[a * N_CHIP + ch],
                            recv_sems.at[a * N_CHIP + ch], sibling)
                    for a in range(n_arr) for ch in range(N_CHIP)]
        return make

    def _chip_copies(self, n_arr, with_slab):
        def make(refs, send_sems, recv_sems):
            x, y = lax.axis_index("x"), lax.axis_index("y")
            my_chip = 2 * x + y
            srcs, lands = refs[:n_arr], refs[n_arr:2 * n_arr]
            copies = []
            for j, k in enumerate((2, 4, 6)):
                peer, peer_idx = _peer(k)
                for a in range(n_arr):
                    copies.append(_remote(srcs[a].at[peer_idx // 2], lands[a].at[my_chip],
                                          send_sems.at[3 * a + j], recv_sems.at[3 * a + j], peer))
            if with_slab:
                slab, slab_land = refs[2 * n_arr], refs[2 * n_arr + 1]
                for k in range(1, N_DEV):
                    peer, _ = _peer(k)
                    copies.append(_remote(slab, slab_land.at[_my_index()], send_sems.at[3 * n_arr + k - 1],
                                          recv_sems.at[3 * n_arr + k - 1], peer))
            return copies
        return make

    def push_pairs(self, tag, for_sibling):
        n_arr = len(for_sibling)
        lands = [lax.empty((N_CHIP,) + a.shape[-2:], a.dtype) for a in for_sibling]
        make = self._pair_copies(n_arr)
        arrays, sems, token = _comm_call(tag + "_start", list(for_sibling) + lands, start=(n_arr * N_CHIP, make))
        self.pending[tag] = (arrays, sems, make, n_arr)
        return token

    def pairs_done(self, tag, after):
        arrays, sems, make, n_arr = self.pending.pop(tag)
        arrays, _, _ = _comm_call(tag + "_wait", arrays, wait=(sems[0], sems[1], make), after=after)
        return arrays[:n_arr], arrays[n_arr:]

    def push_chips(self, tag, pair_sums, slab=None):
        n_arr = len(pair_sums)
        arrays = list(pair_sums) + [lax.empty(a.shape, a.dtype) for a in pair_sums]
        n_copies = 3 * n_arr
        if slab is not None:
            arrays += [slab, lax.empty((N_DEV,) + slab.shape, slab.dtype)]
            n_copies += N_DEV - 1
        make = self._chip_copies(n_arr, slab is not None)
        arrays, sems, token = _comm_call(tag + "_start", arrays, start=(n_copies, make))
        self.pending[tag] = (arrays, sems, make, n_arr)
        return token

    def chips_done(self, tag, after):
        arrays, sems, make, n_arr = self.pending.pop(tag)
        arrays, _, _ = _comm_call(tag + "_wait", arrays, wait=(sems[0], sems[1], make), after=after)
        return arrays[:n_arr], arrays[n_arr:2 * n_arr], arrays[2 * n_arr:]


def _cast_place(w, name):
    r, c = w.shape
    tr = r if r * c <= MIB else 1 << ((MIB // c).bit_length() - 1)
    assert r % tr == 0

    def body(me_ref, w_ref, o_ref):
        o_ref[...] = w_ref[...].astype(BF16)

    return pl.pallas_call(
        body, name=name, out_shape=jax.ShapeDtypeStruct((N_DEV, r, c), BF16),
        grid_spec=pltpu.PrefetchScalarGridSpec(
            num_scalar_prefetch=1, grid=(r // tr,),
            in_specs=[pl.BlockSpec((tr, c), lambda i, me_ref: (i, 0))],
            out_specs=pl.BlockSpec((None, tr, c), lambda i, me_ref: (me_ref[0], i, 0))),
        compiler_params=_params(("arbitrary",), 40),
    )(_my_index().astype(jnp.int32).reshape(1), w)


class _WeightGather:
    CHIPS = (2, 4, 6)

    def __init__(self, buffers):
        self.buffers = dict(buffers)
        self.pending = {}
        self.me = _my_index().astype(jnp.int32)

    def _own_slot_to(self, peers):
        def make(refs, send_sems, recv_sems):
            me = _my_index()
            return [_remote(ref.at[me], ref.at[me], send_sems.at[len(peers) * a + j],
                            recv_sems.at[len(peers) * a + j], _peer(k)[0])
                    for a, ref in enumerate(refs) for j, k in enumerate(peers)]
        return make

    def _forward(self, refs, send_sems, recv_sems):
        sibling, _ = _peer(1)
        copies = []
        for a, ref in enumerate(refs):
            for j, k in enumerate(self.CHIPS):
                slot = ref.at[_peer(k)[1]]
                copies.append(_remote(slot, slot, send_sems.at[3 * a + j], recv_sems.at[3 * a + j], sibling))
        return copies

    def _run(self, call, names, **kw):
        arrays, sems, token = _comm_call(call, [self.buffers[k] for k in names], **kw)
        self.buffers.update(zip(names, arrays))
        return sems, token

    def start(self, tag, names, peers, after=()):
        make = self._own_slot_to(peers)
        sems, token = self._run(tag + "_start", names, start=(len(names) * len(peers), make), after=after)
        self.pending[tag] = (names, sems, make)
        return token

    def arrived(self, tag, after):
        names, sems, make = self.pending.pop(tag)
        self._run(tag + "_wait", names, wait=(sems[0], sems[1], make), after=after)

    def forward(self, tag, after):
        names, sems, make = self.pending.pop(tag)
        new_sems, token = self._run(tag + "_forward", names, wait=(sems[0], sems[1], make),
                                    start=(3 * len(names), self._forward), after=after)
        self.pending[tag + "/fwd"] = (names, new_sems, self._forward)
        return token

    def forwarded(self, tag, after):
        names, sems, make = self.pending.pop(tag + "/fwd")
        self._run(tag + "_done", names, wait=(sems[0], sems[1], make), after=after)


def _pair_add(mine, theirs, name, core=None):
    _, r, c_dim = theirs.shape
    tr = r if r * c_dim <= MIB else 1 << ((MIB // c_dim).bit_length() - 1)
    assert r % tr == 0
    stride = 1 if core is None else 2
    offset = jnp.zeros((1,), jnp.int32) if core is None else core.reshape(1)

    def body(off_ref, a_ref, b_ref, o_ref):
        o_ref[...] = (a_ref[...].astype(F32) + b_ref[...].astype(F32)).astype(BF16)

    blk = (None, tr, c_dim)
    return pl.pallas_call(
        body, name=name, out_shape=jax.ShapeDtypeStruct(theirs.shape, BF16),
        grid_spec=pltpu.PrefetchScalarGridSpec(
            num_scalar_prefetch=1, grid=(N_CHIP, r // tr),
            in_specs=[pl.BlockSpec(blk, lambda ch, i, off_ref: (stride * ch + off_ref[0], i, 0)),
                      pl.BlockSpec(blk, lambda ch, i, off_ref: (ch, i, 0))],
            out_specs=pl.BlockSpec(blk, lambda ch, i, off_ref: (ch, i, 0))),
        compiler_params=_params(("arbitrary", "arbitrary"), 40),
    )(offset, mine, theirs)


def _slab_exchange(slab, name):
    def body(slab_in, slab_out, send_sems, recv_sems, local_sem):
        me = _my_index()
        local = pltpu.make_async_copy(slab_in, slab_out.at[me], local_sem)
        local.start()
        sends = []
        for k in range(1, N_DEV):
            peer, _ = _peer(k)
            sends.append(_remote(slab_in, slab_out.at[me], send_sems.at[k - 1], recv_sems.at[k - 1], peer))
        for cp in sends:
            cp.start()
        for k in range(1, N_DEV):
            peer, peer_idx = _peer(k)
            slot = slab_out.at[peer_idx]
            _remote(slot, slot, send_sems.at[k - 1], recv_sems.at[k - 1], peer).wait_recv()
        for cp in sends:
            cp.wait_send()
        local.wait()

    return pl.pallas_call(
        body, name=name, out_shape=jax.ShapeDtypeStruct((N_DEV,) + slab.shape, slab.dtype),
        in_specs=[HBM_SPEC], out_specs=HBM_SPEC,
        scratch_shapes=[pltpu.SemaphoreType.DMA((N_DEV - 1,)), pltpu.SemaphoreType.DMA((N_DEV - 1,)),
                        pltpu.SemaphoreType.DMA],
        compiler_params=pltpu.CompilerParams(has_side_effects=True),
    )(slab)


def _adamw_math(w, g, m, v):
    m = ADAM_B1 * m + (1.0 - ADAM_B1) * g
    v = ADAM_B2 * v + (1.0 - ADAM_B2) * (g * g)
    m_hat = m / (1.0 - ADAM_B1 ** ADAM_STEP)
    v_hat = v / (1.0 - ADAM_B2 ** ADAM_STEP)
    delta = -ADAM_LR * (m_hat / (jnp.sqrt(v_hat) + ADAM_EPS) + ADAM_WD * w)
    return delta, m, v


def _adamw(parts, own, place, w, m, v, name, dep=None):
    n_parts = parts.shape[0]
    r, c = w.shape
    budget = 280 * 1024
    tr = r if r * c <= budget else 1 << ((budget // c).bit_length() - 1)
    assert r % tr == 0

    def body(place_ref, p_ref, own_ref, w_ref, m_ref, v_ref, g_ref, d_ref, nm_ref, nv_ref):
        mine = own_ref[...].astype(F32)
        g = None
        for i in range(n_parts):
            term = jnp.where(place_ref[0] == i, mine, p_ref[i].astype(F32))
            g = term if g is None else g + term
        delta, nm, nv = _adamw_math(w_ref[...], g, m_ref[...], v_ref[...])
        g_ref[...] = g
        d_ref[...] = delta
        nm_ref[...] = nm
        nv_ref[...] = nv

    blk = pl.BlockSpec((tr, c), lambda i, place_ref: (i, 0))
    shape = jax.ShapeDtypeStruct((r, c), F32)
    in_specs = [pl.BlockSpec((n_parts, tr, c), lambda i, place_ref: (0, i, 0)),
                pl.BlockSpec((None, tr, c), lambda i, place_ref: (place_ref[1], i, 0)), blk, blk, blk]
    operands = [parts, own, w, m, v]
    if dep is not None:
        body = _drop_arg(body, 1 + len(operands))
        in_specs.append(pl.BlockSpec((SUB, HEAD), lambda i, place_ref: (0, 0)))
        operands.append(dep)
    return pl.pallas_call(
        body, name=name, out_shape=[shape] * 4,
        grid_spec=pltpu.PrefetchScalarGridSpec(
            num_scalar_prefetch=1, grid=(r // tr,), in_specs=in_specs, out_specs=[blk] * 4),
        compiler_params=_params(("arbitrary",), 48),
    )(place, *operands)


def kernel(x, p, g_pre, w_in, w_s, b_s, ln_v_g, ln_v_b, g_q, g_k, rel_bias, g_out_a, g_out_b, w_out, g_ple, w_ple_gate, w_ple_up, loss_target, m_g_pre, m_w_in, m_w_s, m_b_s, m_ln_v_g, m_ln_v_b, m_g_q, m_g_k, m_rel_bias, m_g_out_a, m_g_out_b, m_w_out, m_g_ple, m_w_ple_gate, m_w_ple_up, v_g_pre, v_w_in, v_w_s, v_b_s, v_ln_v_g, v_ln_v_b, v_g_q, v_g_k, v_rel_bias, v_g_out_a, v_g_out_b, v_w_out, v_g_ple, v_w_ple_gate, v_w_ple_up):
    args = dict(locals())
    small = {"g_pre": g_pre, "w_s": w_s[0], "b_s": b_s[0], "ln_v_g": ln_v_g, "ln_v_b": ln_v_b, "g_q": g_q,
             "g_k": g_k, "rel_bias": rel_bias, "g_out_a": g_out_a, "g_out_b": g_out_b, "g_ple": g_ple}
    big_names = ("w_in", "w_out", "w_ple_gate", "w_ple_up")
    big = {k: args[k][0] for k in big_names}

    wg = _WeightGather({k: _cast_place(big[k], "place_" + k) for k in big_names})
    ex = _GradExchange()
    results = {}

    def big_adamw(k, delivered, dep=None):
        mine, theirs = delivered[k]
        place = jnp.stack([ex.chip, ex.chip])
        return _adamw(theirs, mine, place, big[k], args["m_" + k][0], args["v_" + k][0], "adamw_" + k, dep=dep)

    def while_last_travels(token, delivered):
        done = []
        for k in big_names[1:]:
            results[k] = big_adamw(k, delivered, dep=token)
            done.append(results[k][0])
        return done

    loss, grad_x, small_parts, delivered, slabs, _ = _local_step(
        x[0], p[0, 0], loss_target[0], small, wg, ex, while_last_travels)
    results["w_in"] = big_adamw("w_in", delivered)
    for k in big_names:
        results[k] = [t[None] for t in results[k]]

    squeeze = lambda t: {k: (t[k][0] if k in ("w_s", "b_s") else t[k]) for k in SMALL_NAMES}
    small_m = squeeze({k: args["m_" + k] for k in SMALL_NAMES})
    small_v = squeeze({k: args["v_" + k] for k in SMALL_NAMES})
    me = _my_index().astype(jnp.int32)
    place = jnp.stack([me, jnp.zeros((), jnp.int32)])
    last_slab = _pack_small(small_parts, SMALL_LAST)
    groups = ((SMALL_EARLY, slabs[1], slabs[0], "adamw_small"),
              (SMALL_LAST, _slab_exchange(last_slab, "last_exchange"), last_slab, "adamw_last"))
    for names_, parts, own, call_name in groups:
        packed = _adamw(parts, own[None], place, _pack_small(small, names_), _pack_small(small_m, names_),
                        _pack_small(small_v, names_), call_name)
        for idx in range(4):
            tree = _unpack_small(packed[idx], small, names_)
            for k in names_:
                results.setdefault(k, [None] * 4)[idx] = tree[k].reshape(args[k].shape)

    names = ("g_pre", "w_in", "w_s", "b_s", "ln_v_g", "ln_v_b", "g_q", "g_k", "rel_bias", "g_out_a", "g_out_b",
             "w_out", "g_ple", "w_ple_gate", "w_ple_up")
    total = lax.psum(loss, AXES)
    out = [total, grad_x[None]]
    for idx in range(4):
        out += [results[k][idx] for k in names]
    return tuple(out)
```

```python
import functools
import math

import numpy as np
import jax
import jax.numpy as jnp
from jax import lax
from jax.experimental import pallas as pl
from jax.experimental.pallas import tpu as pltpu

F32 = jnp.float32
BF16 = jnp.bfloat16
EPS = 1e-6
NEG_INF = -1e30
HEAD = 128
DILATIONS = (1, 4, 16)
NUM_BUCKETS = 32
MAX_DISTANCE = 2048
N_SEG = 7
ADAM_LR = 0.001
ADAM_B1 = 0.9
ADAM_B2 = 0.999
ADAM_EPS = 1e-08
ADAM_WD = 0.01
ADAM_STEP = 10
AXES = ("x", "y", "c")
N_DEV = 8
MIB = 1 << 20

SUB = 8

CHUNK_ORDER = np.array([16 * (r % SUB) + r // SUB for r in range(HEAD)])
BLOCK_ORDER = {
    1: CHUNK_ORDER,
    4: np.array([32 * (r // 32) + 4 * (r % SUB) + (r // SUB) % 4 for r in range(HEAD)]),
    16: np.arange(HEAD),
}

NT_DIMS = (((1,), (1,)), ((), ()))
TN_DIMS = (((0,), (0,)), ((), ()))
NN_DIMS = (((1,), (0,)), ((), ()))


def _params(semantics, vmem_mib):
    return pltpu.CompilerParams(dimension_semantics=semantics, vmem_limit_bytes=vmem_mib * MIB)


def _gelu(a):
    return 0.5 * a * (1.0 + lax.erf(a * (2.0 ** -0.5)))


def _gelu_and_grad(a):
    cdf = 0.5 * (1.0 + lax.erf(a * (2.0 ** -0.5)))
    return a * cdf, cdf + a * jnp.exp(-0.5 * a * a) * ((2.0 * math.pi) ** -0.5)


def _silu_and_grad(a):
    s = jax.nn.sigmoid(a)
    return a * s, s * (1.0 + a * (1.0 - s))


def _rms(v):
    return lax.rsqrt(jnp.mean(v * v, axis=-1, keepdims=True) + EPS)


def _rms_bwd(dy, v, r, g):
    gy = dy * g
    return r * gy - v * (r * r * r) * jnp.mean(gy * v, axis=-1, keepdims=True)


def _dot(a, b, dims=NN_DIMS):
    return lax.dot_general(a, b, dims, preferred_element_type=F32)


def _lane_pick(cols, width):
    rows = cols[0].shape[0]
    lane = lax.broadcasted_iota(jnp.int32, (rows, width), 1)
    out = jnp.zeros((rows, width), F32)
    for h, col in enumerate(cols):
        out = jnp.where(lane == h, col, out)
    return out


def _chunk_perm():
    return jnp.asarray(np.eye(HEAD, dtype=np.float32)[CHUNK_ORDER], BF16)


def _unpermute_f32(p, v):
    hi = v.astype(BF16)
    rest = v - hi.astype(F32)
    mid = rest.astype(BF16)
    lo = (rest - mid.astype(F32)).astype(BF16)
    return _dot(p, hi, TN_DIMS) + _dot(p, mid, TN_DIMS) + _dot(p, lo, TN_DIMS)


def _rmsnorm_fwd(x, g, name, permute, dep=None):
    s, d = x.shape
    tm = HEAD

    def body(x_ref, g_ref, p_ref, o_ref):
        v = x_ref[...]
        out = (v * _rms(v) * g_ref[...]).astype(BF16)
        if permute:
            out = _dot(p_ref[...], out).astype(BF16)
        o_ref[...] = out

    in_specs = [pl.BlockSpec((tm, d), lambda i: (i, 0)), pl.BlockSpec((1, d), lambda i: (0, 0)),
                pl.BlockSpec((HEAD, HEAD), lambda i: (0, 0))]
    operands = [x, g, _chunk_perm()]
    if dep is not None:
        body = _drop_arg(body, len(operands))
        in_specs.append(DEP_SPEC)
        operands.append(dep)
    return pl.pallas_call(
        body, name=name, grid=(s // tm,),
        out_shape=jax.ShapeDtypeStruct((s, d), BF16), in_specs=in_specs,
        out_specs=pl.BlockSpec((tm, d), lambda i: (i, 0)),
        compiler_params=_params(("arbitrary",), 40),
    )(*operands)


def _rmsnorm_bwd(dy, v, g, res, name, dy_permuted, with_bf16):
    s, d = v.shape
    tm = HEAD
    perm = _chunk_perm()

    def body(dy_ref, v_ref, g_ref, res_ref, p_ref, *outs):
        dx_ref, dg_ref = outs[0], outs[-1]
        i = pl.program_id(0)
        vv, dyv = v_ref[...], dy_ref[...]
        if dy_permuted:
            dyv = _unpermute_f32(p_ref[...], dyv)
        r = _rms(vv)
        dx = res_ref[...] + _rms_bwd(dyv, vv, r, g_ref[...])
        dx_ref[...] = dx
        if with_bf16:
            dxb = dx.astype(BF16)
            outs[1][...] = dxb
            outs[2][...] = _dot(p_ref[...], dxb).astype(BF16)

        @pl.when(i == 0)
        def _():
            dg_ref[...] = jnp.zeros_like(dg_ref)

        dg_ref[...] += jnp.sum(dyv * vv * r, axis=0, keepdims=True)

    row = pl.BlockSpec((tm, d), lambda i: (i, 0))
    vec = pl.BlockSpec((1, d), lambda i: (0, 0))
    shapes = [jax.ShapeDtypeStruct((s, d), F32)]
    specs = [row]
    if with_bf16:
        shapes += [jax.ShapeDtypeStruct((s, d), BF16)] * 2
        specs += [row, row]
    shapes.append(jax.ShapeDtypeStruct((1, d), F32))
    specs.append(vec)
    return pl.pallas_call(
        body, name=name, grid=(s // tm,), out_shape=shapes,
        in_specs=[row, row, vec, row, pl.BlockSpec((HEAD, HEAD), lambda i: (0, 0))], out_specs=specs,
        compiler_params=_params(("arbitrary",), 40),
    )(dy, v, g, res, perm)


DEP_SPEC = pl.BlockSpec((SUB, HEAD), lambda *_: (0, 0))


def _drop_arg(body, pos):
    return lambda *refs: body(*refs[:pos], *refs[pos + 1:])


def _matmul(a, b, *, name, grid, a_spec, b_spec, dims, acc_shape, out_shapes, out_specs,
            extra=(), extra_specs=(), epilogue=None, vmem_mib=48, dep=None, prefetch=None, carry=None):
    nk = grid[2]
    n_user = len(extra)
    for unread, spec in ((dep, DEP_SPEC), (carry, HBM_SPEC)):
        if unread is not None:
            extra, extra_specs = tuple(extra) + (unread,), tuple(extra_specs) + (spec,)
    n_extra, n_out = len(extra), len(out_shapes)
    n_pre = 0 if prefetch is None else 1
    aliases = {} if carry is None else {n_pre + 2 + n_extra - 1: 0}

    def body(*refs):
        refs = refs[n_pre:]
        a_ref, b_ref = refs[0], refs[1]
        ex = refs[2:2 + n_user]
        outs = refs[2 + n_extra:2 + n_extra + n_out]
        acc = refs[-1]
        k = pl.program_id(2)

        @pl.when(k == 0)
        def _():
            acc[...] = jnp.zeros_like(acc)

        av = a_ref[...]
        if av.dtype != BF16:
            av = av.astype(BF16)
        acc[...] += _dot(av, b_ref[...], dims)

        @pl.when(k == nk - 1)
        def _():
            if epilogue is None:
                outs[0][...] = acc[...].astype(outs[0].dtype)
            else:
                epilogue(acc, ex, outs)

    scratch = [pltpu.VMEM(acc_shape, F32)]
    params = _params(("parallel", "parallel", "arbitrary"), vmem_mib)
    if prefetch is None:
        return pl.pallas_call(
            body, name=name, grid=grid, out_shape=list(out_shapes),
            in_specs=[a_spec, b_spec, *extra_specs], out_specs=list(out_specs),
            scratch_shapes=scratch, compiler_params=params, input_output_aliases=aliases,
        )(a, b, *extra)
    return pl.pallas_call(
        body, name=name, out_shape=list(out_shapes),
        grid_spec=pltpu.PrefetchScalarGridSpec(
            num_scalar_prefetch=1, grid=grid, in_specs=[a_spec, b_spec, *extra_specs],
            out_specs=list(out_specs), scratch_shapes=scratch),
        compiler_params=params, input_output_aliases=aliases,
    )(prefetch, a, b, *extra)


def _tile(n, want):
    t = min(n, want)
    while n % t:
        t //= 2
    return t


def _rel_buckets(dil):
    order = BLOCK_ORDER[dil]
    qi = jnp.asarray(HEAD + order)
    kj = jnp.asarray(np.concatenate([order, HEAD + order]))
    delta = qi[:, None] - kj[None, :]
    band = (delta >= 0) & (delta <= HEAD)
    dist = jnp.clip(delta, 0, None) * dil
    max_exact = NUM_BUCKETS // 2
    dd = jnp.maximum(dist, 1).astype(F32)
    large = max_exact + (jnp.log(dd / max_exact) / math.log(MAX_DISTANCE / max_exact)
                         * (NUM_BUCKETS - max_exact)).astype(jnp.int32)
    large = jnp.minimum(large, NUM_BUCKETS - 1)
    bucket = jnp.where(dist < max_exact, dist, large)
    return jnp.where(band, bucket, -1).astype(jnp.int32)


def _bias_build(rel_bias, buckets, n_heads):
    nd = buckets.shape[0]

    def body(rb_ref, bk_ref, o_ref):
        for c in range(nd):
            def per_head(h, carry, c=c):
                bk = bk_ref[c]
                acc = jnp.where(bk < 0, NEG_INF, 0.0).astype(F32)
                for b in range(NUM_BUCKETS):
                    acc = jnp.where(bk == b, rb_ref[b, h], acc)
                o_ref[c, h] = acc
                return carry

            lax.fori_loop(0, n_heads, per_head, 0)

    return pl.pallas_call(
        body, name="bias_build",
        out_shape=jax.ShapeDtypeStruct((nd, n_heads, HEAD, 2 * HEAD), F32),
        in_specs=[pl.BlockSpec(memory_space=pltpu.SMEM), pl.BlockSpec(memory_space=pltpu.VMEM)],
        out_specs=pl.BlockSpec(memory_space=pltpu.VMEM),
    )(rel_bias, buckets)


def _bias_grad(ds_all, buckets, n_heads):
    nd = buckets.shape[0]
    pairs = HEAD * 2 * HEAD

    def body(ds_ref, bk_ref, o_ref):
        rows = lax.broadcasted_iota(jnp.int32, (NUM_BUCKETS, pairs), 0)
        tot = jnp.zeros((n_heads, NUM_BUCKETS), F32)
        for c in range(nd):
            onehot = (rows == bk_ref[c]).astype(BF16)
            ds = ds_ref[c]
            hi = ds.astype(BF16)
            lo = (ds - hi.astype(F32)).astype(BF16)
            tot = tot + _dot(hi, onehot, NT_DIMS) + _dot(lo, onehot, NT_DIMS)
        o_ref[...] = tot

    out = pl.pallas_call(
        body, name="bias_grad",
        out_shape=jax.ShapeDtypeStruct((n_heads, NUM_BUCKETS), F32),
        in_specs=[pl.BlockSpec(memory_space=pltpu.VMEM), pl.BlockSpec(memory_space=pltpu.VMEM)],
        out_specs=pl.BlockSpec(memory_space=pltpu.VMEM),
        compiler_params=pltpu.CompilerParams(vmem_limit_bytes=40 * MIB),
    )(ds_all.reshape(nd, n_heads, pairs), buckets.reshape(nd, 1, pairs))
    return out.T


def _qkv_prep(proj, g_q, g_k, w, dep=None):
    s = proj.shape[0]
    n_heads = w // HEAD
    tm = HEAD

    def body(q_ref, k_ref, v_ref, gq_ref, gk_ref, qn_ref, kn_ref, vb_ref):
        gq = gq_ref[...] * (HEAD ** -0.5)
        gk = gk_ref[...]
        for h in range(n_heads):
            sl = slice(h * HEAD, (h + 1) * HEAD)
            q = q_ref[:, sl]
            k = k_ref[:, sl]
            qn_ref[:, sl] = q * _rms(q) * gq
            kn_ref[:, sl] = k * _rms(k) * gk
        vb_ref[...] = v_ref[...]

    seg = lambda j: pl.BlockSpec((tm, w), lambda i, j=j: (i, j))
    vec = pl.BlockSpec((1, HEAD), lambda i: (0, 0))
    out = pl.BlockSpec((tm, w), lambda i: (i, 0))
    in_specs = [seg(3), seg(4), seg(5), vec, vec]
    operands = [proj, proj, proj, g_q, g_k]
    if dep is not None:
        body = _drop_arg(body, len(operands))
        in_specs.append(DEP_SPEC)
        operands.append(dep)
    return pl.pallas_call(
        body, name="qkv_prep", grid=(s // tm,),
        out_shape=[jax.ShapeDtypeStruct((s, w), F32)] * 3,
        in_specs=in_specs, out_specs=[out, out, out],
        compiler_params=_params(("arbitrary",), 40),
    )(*operands)


class _BlockView:
    def __init__(self, s, dil):
        assert s % (HEAD * dil) == 0 and dil in BLOCK_ORDER
        self.nb = s // (HEAD * dil)
        if dil == 1:
            self.lead, self.block = (s,), (HEAD,)
            self.index = lambda r, n: (n,)
        elif dil == 4:
            self.lead, self.block = (s // 512, 4, 4, 4, SUB), (None, 4, 4, None, SUB)
            self.index = lambda r, n: (n, 0, 0, r, 0)
        else:
            self.lead, self.block = (s // 2048, 16, 16, SUB), (None, 16, None, SUB)
            self.index = lambda r, n: (n, 0, r, 0)

    def view(self, t):
        return t.reshape(self.lead + (t.shape[-1],))

    def spec(self, width, block_of):
        return pl.BlockSpec(self.block + (width,), lambda r, n: self.index(r, block_of(r, n)) + (0,))


def _rows(ref, lanes=slice(None)):
    v = ref[(slice(None),) * (len(ref.shape) - 1) + (lanes,)]
    return v.reshape(HEAD, v.shape[-1])


def _set_rows(ref, lanes, value):
    ref[(slice(None),) * (len(ref.shape) - 1) + (lanes,)] = value.reshape(ref.shape[:-1] + (value.shape[-1],))


def _attn_fwd(qn, kn, vb, bias, dil, name):
    s, w = qn.shape
    n_heads = w // HEAD
    bv = _BlockView(s, dil)

    def body(q_ref, kc_ref, kp_ref, vc_ref, vp_ref, bias_ref, o_ref, lse_ref, s_scr, e_scr, lse_scr, inv_scr):
        n = pl.program_id(1)
        heads = [slice(h * HEAD, (h + 1) * HEAD) for h in range(n_heads)]
        lse_scr[...] = jnp.zeros_like(lse_scr)
        for h, sl in enumerate(heads):
            q = _rows(q_ref, sl).astype(BF16)
            s_p = _dot(q, _rows(kp_ref, sl).astype(BF16), NT_DIMS) + bias_ref[h, :, :HEAD]
            s_scr[h, :, :HEAD] = jnp.where(n > 0, s_p, NEG_INF)
            s_scr[h, :, HEAD:] = _dot(q, _rows(kc_ref, sl).astype(BF16), NT_DIMS) + bias_ref[h, :, HEAD:]
        for h in range(n_heads):
            sc = s_scr[h]
            m = jnp.max(sc, axis=-1, keepdims=True)
            e = jnp.exp(sc - m)
            den = jnp.sum(e, axis=-1, keepdims=True)
            e_scr[h] = e.astype(BF16)
            lse_scr[:, h:h + 1] = m + jnp.log(den)
            inv_scr[:, h:h + 1] = 1.0 / den
        for h, sl in enumerate(heads):
            o = (_dot(e_scr[h, :, :HEAD], _rows(vp_ref, sl).astype(BF16))
                 + _dot(e_scr[h, :, HEAD:], _rows(vc_ref, sl).astype(BF16)))
            _set_rows(o_ref, sl, o * inv_scr[:, h:h + 1])
        _set_rows(lse_ref, slice(None), lse_scr[...])

    cur = bv.spec(w, lambda r, n: n)
    prev = bv.spec(w, lambda r, n: jnp.maximum(n - 1, 0))
    o, lse = pl.pallas_call(
        body, name=name, grid=(dil, bv.nb),
        out_shape=[jax.ShapeDtypeStruct(bv.lead + (w,), F32), jax.ShapeDtypeStruct(bv.lead + (HEAD,), F32)],
        in_specs=[cur, cur, prev, cur, prev,
                  pl.BlockSpec((n_heads, HEAD, 2 * HEAD), lambda r, n: (0, 0, 0))],
        out_specs=[cur, bv.spec(HEAD, lambda r, n: n)],
        scratch_shapes=[pltpu.VMEM((n_heads, HEAD, 2 * HEAD), F32), pltpu.VMEM((n_heads, HEAD, 2 * HEAD), BF16),
                        pltpu.VMEM((HEAD, HEAD), F32), pltpu.VMEM((HEAD, HEAD), F32)],
        compiler_params=_params(("arbitrary", "arbitrary"), 48),
    )(bv.view(qn), bv.view(kn), bv.view(kn), bv.view(vb), bv.view(vb), bias)
    return o.reshape(s, w), lse.reshape(s, HEAD)


def _attn_bwd(qn, kn, vb, dyb, lse, delta, bias, dil, name, dep=None):
    s, w = qn.shape
    n_heads = w // HEAD
    bv = _BlockView(s, dil)
    nb = bv.nb

    def body(q_ref, kc_ref, kp_ref, vc_ref, vp_ref, dy_ref, lse_ref, dl_ref, bias_ref,
             dq_ref, dk_ref, dv_ref, ds_ref, carry_k, carry_v, s_scr, dp_scr, p_scr, dsb_scr):
        r = pl.program_id(0)
        step = pl.program_id(1)
        blk = nb - 1 - step

        @pl.when((r == 0) & (step == 0))
        def _():
            ds_ref[...] = jnp.zeros_like(ds_ref)

        @pl.when(step == 0)
        def _():
            carry_k[...] = jnp.zeros_like(carry_k)
            carry_v[...] = jnp.zeros_like(carry_v)

        heads = [slice(h * HEAD, (h + 1) * HEAD) for h in range(n_heads)]
        tots = _rows(lse_ref)
        dls = _rows(dl_ref)
        for h, sl in enumerate(heads):
            q, dy = _rows(q_ref, sl).astype(BF16), _rows(dy_ref, sl).astype(BF16)
            kp, kc = _rows(kp_ref, sl).astype(BF16), _rows(kc_ref, sl).astype(BF16)
            vp, vc = _rows(vp_ref, sl).astype(BF16), _rows(vc_ref, sl).astype(BF16)
            s_p = _dot(q, kp, NT_DIMS) + bias_ref[h, :, :HEAD]
            s_scr[h, :, :HEAD] = jnp.where(blk > 0, s_p, NEG_INF)
            s_scr[h, :, HEAD:] = _dot(q, kc, NT_DIMS) + bias_ref[h, :, HEAD:]
            dp_scr[h, :, :HEAD] = _dot(dy, vp, NT_DIMS)
            dp_scr[h, :, HEAD:] = _dot(dy, vc, NT_DIMS)
        for h in range(n_heads):
            prob = jnp.exp(s_scr[h] - tots[:, h:h + 1])
            ds = prob * (dp_scr[h] - dls[:, h:h + 1])
            ds_ref[h] += ds
            p_scr[h] = prob.astype(BF16)
            dsb_scr[h] = ds.astype(BF16)
        for h, sl in enumerate(heads):
            q, dy = _rows(q_ref, sl).astype(BF16), _rows(dy_ref, sl).astype(BF16)
            kp, kc = _rows(kp_ref, sl).astype(BF16), _rows(kc_ref, sl).astype(BF16)
            ds_pb, ds_cb = dsb_scr[h, :, :HEAD], dsb_scr[h, :, HEAD:]
            _set_rows(dq_ref, sl, _dot(ds_pb, kp) + _dot(ds_cb, kc))
            _set_rows(dk_ref, sl, _dot(ds_cb, q, TN_DIMS) + carry_k[:, sl])
            carry_k[:, sl] = _dot(ds_pb, q, TN_DIMS)
            _set_rows(dv_ref, sl, _dot(p_scr[h, :, HEAD:], dy, TN_DIMS) + carry_v[:, sl])
            carry_v[:, sl] = _dot(p_scr[h, :, :HEAD], dy, TN_DIMS)

    cur = bv.spec(w, lambda r, n: nb - 1 - n)
    prev = bv.spec(w, lambda r, n: jnp.maximum(nb - 2 - n, 0))
    stat = bv.spec(HEAD, lambda r, n: nb - 1 - n)
    whole = pl.BlockSpec((n_heads, HEAD, 2 * HEAD), lambda r, n: (0, 0, 0))
    big = jax.ShapeDtypeStruct(bv.lead + (w,), F32)
    in_specs = [cur, cur, prev, cur, prev, cur, stat, stat, whole]
    operands = [bv.view(qn), bv.view(kn), bv.view(kn), bv.view(vb), bv.view(vb), bv.view(dyb), bv.view(lse),
                bv.view(delta), bias]
    if dep is not None:
        body = _drop_arg(body, len(operands))
        in_specs.append(DEP_SPEC)
        operands.append(dep)
    dq, dk, dv, ds = pl.pallas_call(
        body, name=name, grid=(dil, nb),
        out_shape=[big, big, big, jax.ShapeDtypeStruct((n_heads, HEAD, 2 * HEAD), F32)],
        in_specs=in_specs, out_specs=[cur, cur, cur, whole],
        scratch_shapes=[pltpu.VMEM((HEAD, w), F32), pltpu.VMEM((HEAD, w), F32),
                        pltpu.VMEM((n_heads, HEAD, 2 * HEAD), F32), pltpu.VMEM((n_heads, HEAD, 2 * HEAD), F32),
                        pltpu.VMEM((n_heads, HEAD, 2 * HEAD), BF16), pltpu.VMEM((n_heads, HEAD, 2 * HEAD), BF16)],
        compiler_params=_params(("arbitrary", "arbitrary"), 56),
    )(*operands)
    return dq.reshape(s, w), dk.reshape(s, w), dv.reshape(s, w), ds


def _qkv_bwd(dproj, proj, dqs, dks, dvs, g_q, g_k, w):
    s = proj.shape[0]
    n_heads = w // HEAD
    tm = HEAD

    def body(dproj_hbm, q_ref, k_ref, gq_ref, gk_ref, *rest):
        dq_refs, dk_refs, dv_refs = rest[0:3], rest[3:6], rest[6:9]
        out_ref, dgq_ref, dgk_ref = rest[9:12]
        i = pl.program_id(0)
        gq = gq_ref[...] * (HEAD ** -0.5)
        gk = gk_ref[...]
        acc_q = jnp.zeros((1, HEAD), F32)
        acc_k = jnp.zeros((1, HEAD), F32)
        for h in range(n_heads):
            sl = slice(h * HEAD, (h + 1) * HEAD)
            q, k = q_ref[:, sl], k_ref[:, sl]
            dqn = sum(t[:, sl].astype(F32) for t in dq_refs)
            dkn = sum(t[:, sl].astype(F32) for t in dk_refs)
            rq, rk = _rms(q), _rms(k)
            out_ref[:, h * HEAD:(h + 1) * HEAD] = _rms_bwd(dqn, q, rq, gq).astype(BF16)
            out_ref[:, w + h * HEAD:w + (h + 1) * HEAD] = _rms_bwd(dkn, k, rk, gk).astype(BF16)
            acc_q += jnp.sum(dqn * q * rq, axis=0, keepdims=True)
            acc_k += jnp.sum(dkn * k * rk, axis=0, keepdims=True)
        out_ref[:, 2 * w:] = sum(t[...].astype(F32) for t in dv_refs).astype(BF16)

        @pl.when(i == 0)
        def _():
            dgq_ref[...] = jnp.zeros_like(dgq_ref)
            dgk_ref[...] = jnp.zeros_like(dgk_ref)

        dgq_ref[...] += acc_q * (HEAD ** -0.5)
        dgk_ref[...] += acc_k

    seg = lambda j: pl.BlockSpec((tm, w), lambda i, j=j: (i, j))
    vec = pl.BlockSpec((1, HEAD), lambda i: (0, 0))
    row = pl.BlockSpec((tm, w), lambda i: (i, 0))
    return pl.pallas_call(
        body, name="qkv_bwd", grid=(s // tm,),
        out_shape=[jax.ShapeDtypeStruct(dproj.shape, BF16),
                   jax.ShapeDtypeStruct((1, HEAD), F32), jax.ShapeDtypeStruct((1, HEAD), F32)],
        in_specs=[pl.BlockSpec(memory_space=pl.ANY), seg(3), seg(4), vec, vec] + [row] * 9,
        out_specs=[pl.BlockSpec((tm, 3 * w), lambda i: (i, 1)), vec, vec],
        input_output_aliases={0: 0},
        compiler_params=_params(("arbitrary",), 48),
    )(dproj, proj, proj, g_q, g_k, *dqs, *dks, *dvs)


def _mixer_a(u, gv, ws_ref, bst_ref, lng, lnb, z_scr, ln_scr):
    n_groups = u.shape[1] // HEAD
    mu = jnp.mean(gv, axis=-1, keepdims=True)
    xc = gv - mu
    rs = lax.rsqrt(jnp.mean(xc * xc, axis=-1, keepdims=True) + EPS)
    xhat = xc * rs
    ln_scr[...] = (xhat * lng + lnb).astype(BF16)
    causal = _causal_mask()
    for g in range(n_groups):
        sl = slice(g * HEAD, (g + 1) * HEAD)
        wm = jnp.where(causal, ws_ref[g], 0.0).astype(BF16)
        z_scr[:, sl] = _dot(wm, ln_scr[:, sl]) + bst_ref[:, g:g + 1]
    return u, xhat, rs


def _causal_mask():
    token = lambda r: 16 * (r % SUB) + r // SUB
    row = lax.broadcasted_iota(jnp.int32, (HEAD, HEAD), 0)
    col = lax.broadcasted_iota(jnp.int32, (HEAD, HEAD), 1)
    return token(col) <= token(row)


def _merge_b(o_refs, lse_refs, yb_scr):
    n_heads = yb_scr.shape[1] // HEAD
    lses = [t[...] for t in lse_refs]
    m = jnp.maximum(jnp.maximum(lses[0], lses[1]), lses[2])
    tot = m + jnp.log(sum(jnp.exp(t - m) for t in lses))
    alphas = [jnp.exp(t - tot) for t in lses]
    for h in range(n_heads):
        sl = slice(h * HEAD, (h + 1) * HEAD)
        yb_scr[:, sl] = sum(a[:, h:h + 1] * o[:, sl].astype(F32) for a, o in zip(alphas, o_refs))
    return tot


def _mix_fwd(proj, outs, lses, w_s, bst, ln_g, ln_b, g_a, g_b, w):
    s = proj.shape[0]
    n_groups = w // HEAD

    def body(au_ref, av_ref, az_ref, bz_ref, o1, o2, o3, l1, l2, l3, ws_ref, bst_ref,
             lng_ref, lnb_ref, ga_ref, gb_ref, p_ref, y_ref, z_scr, ln_scr, yb_scr):
        u, _, _ = _mixer_a(_gelu(au_ref[...]), _gelu(av_ref[...]), ws_ref, bst_ref, lng_ref[...], lnb_ref[...],
                           z_scr, ln_scr)
        ya = u * z_scr[...]
        silu_a, _ = _silu_and_grad(az_ref[...])
        perm = p_ref[...]
        y_ref[:, :w] = _dot(perm, (ya * _rms(ya) * ga_ref[...] * silu_a).astype(BF16), TN_DIMS).astype(BF16)
        _merge_b((o1, o2, o3), (l1, l2, l3), yb_scr)
        yb = yb_scr[...]
        silu_b, _ = _silu_and_grad(bz_ref[...])
        y_ref[:, w:] = _dot(perm, (yb * _rms(yb) * gb_ref[...] * silu_b).astype(BF16), TN_DIMS).astype(BF16)

    seg = lambda j: pl.BlockSpec((HEAD, w), lambda i, j=j: (i, j))
    row = pl.BlockSpec((HEAD, w), lambda i: (i, 0))
    stat = pl.BlockSpec((HEAD, HEAD), lambda i: (i, 0))
    vec = pl.BlockSpec((1, w), lambda i: (0, 0))
    return pl.pallas_call(
        body, name="mix_fwd", grid=(s // HEAD,),
        out_shape=jax.ShapeDtypeStruct((s, 2 * w), BF16),
        in_specs=[seg(0), seg(1), seg(2), seg(6), row, row, row, stat, stat, stat,
                  pl.BlockSpec((n_groups, HEAD, HEAD), lambda i: (0, 0, 0)),
                  pl.BlockSpec((HEAD, n_groups), lambda i: (0, 0)), vec, vec, vec, vec,
                  pl.BlockSpec((HEAD, HEAD), lambda i: (0, 0))],
        out_specs=pl.BlockSpec((HEAD, 2 * w), lambda i: (i, 0)),
        scratch_shapes=[pltpu.VMEM((HEAD, w), F32), pltpu.VMEM((HEAD, w), BF16), pltpu.VMEM((HEAD, w), F32)],
        compiler_params=_params(("arbitrary",), 48),
    )(proj, proj, proj, proj, *outs, *lses, w_s, bst, ln_g, ln_b, g_a, g_b, _chunk_perm())


def _mix_bwd(proj, dy, outs, lses, w_s, bst, ln_g, ln_b, g_a, g_b, w):
    s = proj.shape[0]
    n_groups = w // HEAD

    def body(au_ref, av_ref, az_ref, bz_ref, dy_ref, o1, o2, o3, l1, l2, l3, ws_ref, bst_ref,
             lng_ref, lnb_ref, ga_ref, gb_ref,
             dproj_ref, dyb_ref, tot_ref, dl_ref, dws_ref, dbst_ref, dlng_ref, dlnb_ref, dga_ref, dgb_ref,
             z_scr, ln_scr, yb_scr, dz_scr, dln_scr):
        i = pl.program_id(0)

        @pl.when(i == 0)
        def _():
            for t in (dws_ref, dbst_ref, dlng_ref, dlnb_ref, dga_ref, dgb_ref):
                t[...] = jnp.zeros_like(t)

        az = az_ref[...]
        lng = lng_ref[...]
        u, du_dau = _gelu_and_grad(au_ref[...])
        gv, dgv_dav = _gelu_and_grad(av_ref[...])
        u, xhat, rs = _mixer_a(u, gv, ws_ref, bst_ref, lng, lnb_ref[...], z_scr, ln_scr)
        z = z_scr[...]
        ya = u * z
        ra = _rms(ya)
        silu_a, dsilu_a = _silu_and_grad(az)
        dya_all = dy_ref[:, :w]
        na = ya * ra * ga_ref[...]
        dna = dya_all * silu_a
        dproj_ref[:, 2 * w:3 * w] = (dya_all * na * dsilu_a).astype(BF16)
        dga_ref[...] += jnp.sum(dna * ya * ra, axis=0, keepdims=True)
        dya = _rms_bwd(dna, ya, ra, ga_ref[...])
        dproj_ref[:, :w] = (dya * z * du_dau).astype(BF16)
        dz_scr[...] = (dya * u).astype(BF16)

        causal = _causal_mask()
        for g in range(n_groups):
            sl = slice(g * HEAD, (g + 1) * HEAD)
            wm = jnp.where(causal, ws_ref[g], 0.0).astype(BF16)
            dz = dz_scr[:, sl]
            dln_scr[:, sl] = _dot(wm, dz, TN_DIMS)
            dws_ref[g] += jnp.where(causal, _dot(dz, ln_scr[:, sl], NT_DIMS), 0.0)
            dbst_ref[:, g:g + 1] += jnp.sum(dz.astype(F32), axis=-1, keepdims=True)
        dln = dln_scr[...]
        dlng_ref[...] += jnp.sum(dln * xhat, axis=0, keepdims=True)
        dlnb_ref[...] += jnp.sum(dln, axis=0, keepdims=True)
        gy = dln * lng
        dgv = rs * (gy - jnp.mean(gy, axis=-1, keepdims=True)
                    - xhat * jnp.mean(gy * xhat, axis=-1, keepdims=True))
        dproj_ref[:, w:2 * w] = (dgv * dgv_dav).astype(BF16)
        dproj_ref[:, 3 * w:6 * w] = jnp.zeros((HEAD, 3 * w), BF16)

        tot_ref[...] = _merge_b((o1, o2, o3), (l1, l2, l3), yb_scr)
        yb = yb_scr[...]
        rb = _rms(yb)
        bz = bz_ref[...]
        silu_b, dsilu_b = _silu_and_grad(bz)
        dyb_all = dy_ref[:, w:]
        dnb = dyb_all * silu_b
        dproj_ref[:, 6 * w:] = (dyb_all * yb * rb * gb_ref[...] * dsilu_b).astype(BF16)
        dgb_ref[...] += jnp.sum(dnb * yb * rb, axis=0, keepdims=True)
        dyb = _rms_bwd(dnb, yb, rb, gb_ref[...])
        dyb_ref[...] = dyb
        prod = dyb * yb
        dl_ref[...] = _lane_pick(
            [jnp.sum(prod[:, h * HEAD:(h + 1) * HEAD], axis=-1, keepdims=True) for h in range(n_groups)], HEAD)

    seg = lambda j: pl.BlockSpec((HEAD, w), lambda i, j=j: (i, j))
    row_w = pl.BlockSpec((HEAD, w), lambda i: (i, 0))
    stat = pl.BlockSpec((HEAD, HEAD), lambda i: (i, 0))
    vec = pl.BlockSpec((1, w), lambda i: (0, 0))
    ws_spec = pl.BlockSpec((n_groups, HEAD, HEAD), lambda i: (0, 0, 0))
    bst_spec = pl.BlockSpec((HEAD, n_groups), lambda i: (0, 0))
    vec_shape = jax.ShapeDtypeStruct((1, w), F32)
    return pl.pallas_call(
        body, name="mix_bwd", grid=(s // HEAD,),
        out_shape=[jax.ShapeDtypeStruct((s, N_SEG * w), BF16), jax.ShapeDtypeStruct((s, w), F32),
                   jax.ShapeDtypeStruct((s, HEAD), F32), jax.ShapeDtypeStruct((s, HEAD), F32),
                   jax.ShapeDtypeStruct((n_groups, HEAD, HEAD), F32), jax.ShapeDtypeStruct((HEAD, n_groups), F32),
                   vec_shape, vec_shape, vec_shape, vec_shape],
        in_specs=[seg(0), seg(1), seg(2), seg(6), pl.BlockSpec((HEAD, 2 * w), lambda i: (i, 0)),
                  row_w, row_w, row_w, stat, stat, stat, ws_spec, bst_spec, vec, vec, vec, vec],
        out_specs=[pl.BlockSpec((HEAD, N_SEG * w), lambda i: (i, 0)), row_w, stat, stat,
                   ws_spec, bst_spec, vec, vec, vec, vec],
        scratch_shapes=[pltpu.VMEM((HEAD, w), F32), pltpu.VMEM((HEAD, w), BF16), pltpu.VMEM((HEAD, w), F32),
                        pltpu.VMEM((HEAD, w), BF16), pltpu.VMEM((HEAD, w), F32)],
        compiler_params=_params(("arbitrary",), 56),
    )(proj, proj, proj, proj, dy, *outs, *lses, w_s, bst, ln_g, ln_b, g_a, g_b)


def _local_step(x, p, tgt, small, wg, ex, while_last_travels=None):
    s, d = x.shape
    n, _, c_in = wg.buffers["w_in"].shape
    assert n == N_DEV
    d_in = n * c_in
    w = d_in // N_SEG
    n_heads = w // HEAD
    p_dim, c_up = wg.buffers["w_ple_up"].shape[1:]
    assert s % (HEAD * DILATIONS[-1]) == 0 and w % HEAD == 0 and d == n * c_up == 2 * w

    near, far = (2, 4), (6,)
    wg.start("gather_in_pair", ["w_in"], (1,))
    wg.start("gather_in_near", ["w_in"], near)
    token = wg.start("gather_in_far", ["w_in"], far)
    rest = ["w_out", "w_ple_gate", "w_ple_up"]
    token = wg.start("gather_rest", rest, (1,) + near + far, after=[token])

    hn = _rmsnorm_fwd(x, small["g_pre"], "pre_norm", True, dep=token)
    tm, tk = _tile(s, 1024), _tile(d, 1024)

    def in_proj(shards, name, carry, dep=None):
        return _matmul(
            hn, wg.buffers["w_in"], name=name, grid=(s // tm, len(shards), d // tk), dims=NN_DIMS,
            prefetch=jnp.stack(shards).astype(jnp.int32),
            a_spec=pl.BlockSpec((tm, tk), lambda i, j, k, sh: (i, k)),
            b_spec=pl.BlockSpec((None, tk, c_in), lambda i, j, k, sh: (sh[j], k, 0)),
            acc_shape=(tm, c_in), out_shapes=[jax.ShapeDtypeStruct((s, d_in), F32)],
            out_specs=[pl.BlockSpec((tm, c_in), lambda i, j, k, sh: (i, sh[j]))], carry=carry, dep=dep)[0]

    me = wg.me
    wg.arrived("gather_in_pair", [hn])
    proj = in_proj([me, me ^ 1], "in_proj_pair", None)
    for tag, chips in (("near", near), ("far", far)):
        token = wg.forward("gather_in_" + tag, [proj])
        proj = in_proj([me ^ k for k in chips], "in_proj_" + tag, proj, dep=token)
        wg.forwarded("gather_in_" + tag, [proj])
        proj = in_proj([me ^ k ^ 1 for k in chips], "in_proj_%s_forwarded" % tag, proj)
    win_g = wg.buffers["w_in"]
    token = wg.forward("gather_rest", [proj])

    qn, kn, vb = _qkv_prep(proj, small["g_q"], small["g_k"], w, dep=token)
    buckets = jnp.stack([_rel_buckets(dil) for dil in DILATIONS])
    bias = _bias_build(small["rel_bias"], buckets, n_heads)
    outs, lses = [], []
    for c, dil in enumerate(DILATIONS):
        o, l = _attn_fwd(qn, kn, vb, bias[c], dil, "attn_fwd_d%d" % dil)
        outs.append(o)
        lses.append(l)
    wg.forwarded("gather_rest", [outs[-1]])
    wout_g, wgate_g, wup_g = (wg.buffers[k] for k in rest)
    wout_f = wout_g.reshape(2 * w, d)
    wgate_f = wgate_g.reshape(d, d)

    ws_p = small["w_s"][:, CHUNK_ORDER][:, :, CHUNK_ORDER]
    bst = small["b_s"].T[CHUNK_ORDER]
    mix_args = (outs, lses, ws_p, bst, small["ln_v_g"], small["ln_v_b"], small["g_out_a"], small["g_out_b"], w)
    y = _mix_fwd(proj, *mix_args)

    tn = _tile(d, 1024)
    tk2 = _tile(2 * w, 1024)

    def resid_epilogue(acc, ex, outs_):
        outs_[0][...] = ex[0][...] + acc[...]

    h = _matmul(
        y, wout_f, name="out_proj", grid=(s // tm, d // tn, (2 * w) // tk2), dims=NN_DIMS,
        a_spec=pl.BlockSpec((tm, tk2), lambda i, j, k: (i, k)),
        b_spec=pl.BlockSpec((tk2, tn), lambda i, j, k: (k, j)),
        acc_shape=(tm, tn), out_shapes=[jax.ShapeDtypeStruct((s, d), F32)],
        out_specs=[pl.BlockSpec((tm, tn), lambda i, j, k: (i, j))],
        extra=(x,), extra_specs=(pl.BlockSpec((tm, tn), lambda i, j, k: (i, j)),),
        epilogue=resid_epilogue)[0]

    hn2 = _rmsnorm_fwd(h, small["g_ple"], "ple_norm", False)

    tmg = _tile(s, 512)

    def ple_epilogue(acc, ex, outs_):
        h_ref, p_ref, wup_ref, tgt_ref = ex
        dout_ref, dpre_ref, dup_ref, loss_ref = outs_
        gate = jax.nn.sigmoid(acc[...])
        up = _dot(p_ref[...].astype(BF16), wup_ref[...])
        err = h_ref[...] + gate * up - tgt_ref[...]
        dout = err * (1.0 / d)
        dout_ref[...] = dout
        dpre_ref[...] = (dout * up * gate * (1.0 - gate)).astype(BF16)
        dup_ref[...] = (dout * gate).astype(BF16)
        part = 0.5 * jnp.sum(err * err) * (1.0 / d)
        rr = lax.broadcasted_iota(jnp.int32, (8, HEAD), 0)
        cc = lax.broadcasted_iota(jnp.int32, (8, HEAD), 1)
        loss_ref[...] = jnp.where((rr == 0) & (cc == 0), part, 0.0)

    tile_ij = pl.BlockSpec((tmg, c_up), lambda i, j, k: (i, j))
    dout, dpre, dup, loss_parts = _matmul(
        hn2, wgate_f, name="ple_gate", grid=(s // tmg, n, 1), dims=NN_DIMS,
        a_spec=pl.BlockSpec((tmg, d), lambda i, j, k: (i, 0)),
        b_spec=pl.BlockSpec((d, c_up), lambda i, j, k: (0, j)),
        acc_shape=(tmg, c_up),
        out_shapes=[jax.ShapeDtypeStruct((s, d), F32), jax.ShapeDtypeStruct((s, d), BF16),
                    jax.ShapeDtypeStruct((s, d), BF16), jax.ShapeDtypeStruct((s // tmg * 8, n * HEAD), F32)],
        out_specs=[tile_ij, tile_ij, tile_ij, pl.BlockSpec((8, HEAD), lambda i, j, k: (i, j))],
        extra=(h, p, wup_g, tgt),
        extra_specs=(tile_ij, pl.BlockSpec((tmg, p_dim), lambda i, j, k: (i, 0)),
                     pl.BlockSpec((None, p_dim, c_up), lambda i, j, k: (j, 0, 0)), tile_ij),
        epilogue=ple_epilogue)
    loss = jnp.sum(loss_parts)

    tks = _tile(s, 1024)
    g_wup = _matmul(
        p, dup, name="grad_w_up", grid=(1, n, s // tks), dims=TN_DIMS,
        a_spec=pl.BlockSpec((tks, p_dim), lambda i, j, k: (k, 0)),
        b_spec=pl.BlockSpec((tks, c_up), lambda i, j, k: (k, j)),
        acc_shape=(p_dim, c_up), out_shapes=[jax.ShapeDtypeStruct((n, p_dim, c_up), BF16)],
        out_specs=[pl.BlockSpec((None, p_dim, c_up), lambda i, j, k: (j, 0, 0))])[0]

    def tn_matmul(a, b, name):
        m_, n_ = a.shape[1], b.shape[1]
        bm, bn = _tile(m_, 1024), _tile(n_, 1024)
        return _matmul(
            a, b, name=name, grid=(m_ // bm, n_ // bn, s // tks), dims=TN_DIMS,
            a_spec=pl.BlockSpec((tks, bm), lambda i, j, k: (k, i)),
            b_spec=pl.BlockSpec((tks, bn), lambda i, j, k: (k, j)),
            acc_shape=(bm, bn), out_shapes=[jax.ShapeDtypeStruct((m_, n_), BF16)],
            out_specs=[pl.BlockSpec((bm, bn), lambda i, j, k: (i, j))])[0]

    def nt_matmul(a, b, name, out_dtype, dep=None):
        k_, n_ = a.shape[1], b.shape[0]
        bm, bn, bk = _tile(s, 1024), _tile(n_, 1024), _tile(k_, 1024)
        return _matmul(
            a, b, name=name, grid=(s // bm, n_ // bn, k_ // bk), dims=NT_DIMS,
            a_spec=pl.BlockSpec((bm, bk), lambda i, j, k: (i, k)),
            b_spec=pl.BlockSpec((bn, bk), lambda i, j, k: (j, k)),
            acc_shape=(bm, bn), out_shapes=[jax.ShapeDtypeStruct((s, n_), out_dtype)],
            out_specs=[pl.BlockSpec((bm, bn), lambda i, j, k: (i, j))], dep=dep)[0]

    by_core = lambda g: g.reshape((N_CHIP, 2) + g.shape[-2:])
    g_wgate = tn_matmul(hn2, dpre, "grad_w_gate").reshape(wgate_g.shape)
    dhn2 = nt_matmul(dpre, wgate_f, "ple_gate_bwd", F32)
    dh, dh_b, dh_bp, dg_ple = _rmsnorm_bwd(dhn2, h, small["g_ple"], dout, "ple_norm_bwd", False, True)
    g_wout = tn_matmul(y, dh_b, "grad_w_out").reshape(wout_g.shape)

    late = ("w_out", "w_ple_gate", "w_ple_up")
    late_parts = (g_wout, g_wgate, g_wup)
    token = ex.push_pairs("pair_late", [by_core(g) for g in late_parts])
    dy = nt_matmul(dh_bp, wout_f, "out_proj_bwd", F32, dep=token)
    (dproj, dyb, lse_tot, delta, dws, dbst, dlng, dlnb, dga, dgb) = _mix_bwd(proj, dy, *mix_args)
    both_columns, from_sibling = ex.pairs_done("pair_late", [dproj])
    pair_sums = [_pair_add(mine.reshape((N_DEV,) + mine.shape[-2:]), theirs, "pair_add_" + k, ex.core)
                 for k, mine, theirs in zip(late, both_columns, from_sibling)]
    token = ex.push_chips("chip_late", pair_sums)

    dqs, dks, dvs, dss = [], [], [], []
    for c, dil in enumerate(DILATIONS):
        dq, dk, dv, ds = _attn_bwd(qn, kn, vb, dyb, lse_tot, delta, bias[c], dil, "attn_bwd_d%d" % dil,
                                   dep=token if c == 0 else None)
        dqs.append(dq)
        dks.append(dk)
        dvs.append(dv)
        dss.append(ds)
    d_rel = _bias_grad(jnp.stack(dss), buckets, n_heads)
    dproj, dgq, dgk = _qkv_bwd(dproj, proj, dqs, dks, dvs, small["g_q"], small["g_k"], w)
    pair_sums, landed, _ = ex.chips_done("chip_late", [dproj])
    delivered = {k: (mine, theirs) for k, mine, theirs in zip(late, pair_sums, landed)}

    token_row = np.argsort(CHUNK_ORDER)
    dws = dws[:, token_row][:, :, token_row]
    dbst = dbst[token_row]
    small_grads = {
        "w_s": dws, "b_s": dbst.T, "ln_v_g": dlng, "ln_v_b": dlnb, "g_q": dgq, "g_k": dgk,
        "rel_bias": d_rel, "g_out_a": dga, "g_out_b": dgb, "g_ple": dg_ple,
    }

    bm = _tile(d, 1024)

    def grad_w_in(core, name, dep=None):
        return _matmul(
            hn, dproj, name=name, grid=(d // bm, N_CHIP, s // tks), dims=TN_DIMS, prefetch=core.reshape(1),
            a_spec=pl.BlockSpec((tks, bm), lambda i, j, k, core_ref: (k, i)),
            b_spec=pl.BlockSpec((tks, c_in), lambda i, j, k, core_ref: (k, 2 * j + core_ref[0])),
            acc_shape=(bm, c_in), out_shapes=[jax.ShapeDtypeStruct((N_CHIP, d, c_in), BF16)],
            out_specs=[pl.BlockSpec((None, bm, c_in), lambda i, j, k, core_ref: (j, i, 0))], dep=dep)[0]

    for_sibling = grad_w_in(1 - ex.core, "grad_w_in_sibling")
    token = ex.push_pairs("pair_in", [for_sibling])
    mine = grad_w_in(ex.core, "grad_w_in_mine", dep=token)
    _, from_sibling = ex.pairs_done("pair_in", [mine])
    pair_sum = _pair_add(mine, from_sibling[0], "pair_add_w_in")
    token = ex.push_chips("chip_in", [pair_sum], _pack_small(small_grads, SMALL_EARLY))

    dhn = _matmul(
        dproj, win_g, name="in_proj_bwd", grid=(s // tm, d // tn, n), dims=NT_DIMS,
        a_spec=pl.BlockSpec((tm, c_in), lambda i, j, k: (i, k)),
        b_spec=pl.BlockSpec((None, tn, c_in), lambda i, j, k: (k, j, 0)),
        acc_shape=(tm, tn), out_shapes=[jax.ShapeDtypeStruct((s, d), F32)],
        out_specs=[pl.BlockSpec((tm, tn), lambda i, j, k: (i, j))], dep=token)[0]
    grad_x, dg_pre = _rmsnorm_bwd(dhn, x, small["g_pre"], dh, "pre_norm_bwd", True, False)
    extra = while_last_travels(token, delivered) if while_last_travels is not None else []
    pair_sums, landed, slabs = ex.chips_done("chip_in", [grad_x] + list(extra))
    delivered["w_in"] = (pair_sums[0], landed[0])
    small_grads["g_pre"] = dg_pre
    return loss, grad_x, small_grads, delivered, slabs, extra


SMALL_EARLY = ("w_s", "b_s", "ln_v_g", "ln_v_b", "g_q", "g_k", "rel_bias", "g_out_a", "g_out_b", "g_ple")
SMALL_LAST = ("g_pre",)
SMALL_NAMES = SMALL_LAST + SMALL_EARLY


def _pack_small(tree, names):
    parts = []
    for name in names:
        flat = tree[name].astype(F32).reshape(-1)
        pad = (-flat.shape[0]) % HEAD
        parts.append(jnp.pad(flat, (0, pad)) if pad else flat)
    slab = jnp.concatenate(parts).reshape(-1, HEAD)
    pad_rows = (-slab.shape[0]) % 8
    return jnp.pad(slab, ((0, pad_rows), (0, 0))) if pad_rows else slab


def _unpack_small(slab, like, names):
    flat = slab.reshape(-1)
    out, off = {}, 0
    for name in names:
        size = like[name].size
        out[name] = flat[off:off + size].reshape(like[name].shape)
        off += size + (-size) % HEAD
    return out


def _peer(k):
    x, y, c = (lax.axis_index(a) for a in AXES)
    bits = ((k >> 2) & 1, (k >> 1) & 1, k & 1)
    px, py, pc = (1 - v if b else v for v, b in zip((x, y, c), bits))
    return (px, py, pc), 4 * px + 2 * py + pc


def _my_index():
    x, y, c = (lax.axis_index(a) for a in AXES)
    return 4 * x + 2 * y + c


N_CHIP = 4
HBM_SPEC = pl.BlockSpec(memory_space=pl.ANY)


def _remote(src, dst, send_sem, recv_sem, peer):
    return pltpu.make_async_remote_copy(src_ref=src, dst_ref=dst, send_sem=send_sem, recv_sem=recv_sem,
                                        device_id=peer, device_id_type=pl.DeviceIdType.MESH)


SEM_SPEC = pl.BlockSpec(memory_space=pltpu.SEMAPHORE)
HBM_ONLY = pl.BlockSpec(memory_space=pltpu.HBM)
DATAFLOW = pltpu.SideEffectType.DATAFLOW_SIDE_EFFECTING


def _comm_call(name, arrays, *, wait=None, start=None, after=()):
    n, n_after = len(arrays), len(after)

    def body(*refs):
        ins = refs[:n]
        pos = n
        if wait is not None:
            for cp in wait[2](ins, refs[pos], refs[pos + 1]):
                cp.wait()
            pos += 2
        outs = refs[pos + n_after:]
        if start is not None:
            for cp in start[1](ins, outs[0], outs[1]):
                cp.start()
        outs[-1][...] = jnp.zeros_like(outs[-1])

    operands = [pltpu.with_memory_space_constraint(a, pltpu.HBM) for a in arrays]
    in_specs = [HBM_ONLY] * n
    if wait is not None:
        operands += [wait[0], wait[1]]
        in_specs += [SEM_SPEC, SEM_SPEC]
    operands += list(after)
    in_specs += [HBM_SPEC] * n_after
    out_shape, out_specs = [], []
    if start is not None:
        out_shape += [pltpu.SemaphoreType.DMA((start[0],))] * 2
        out_specs += [SEM_SPEC, SEM_SPEC]
    first = len(out_shape)
    out_shape += [pltpu.HBM(a.shape, a.dtype) for a in arrays] + [jax.ShapeDtypeStruct((SUB, HEAD), F32)]
    out_specs += [HBM_ONLY] * n + [pl.BlockSpec(memory_space=pltpu.VMEM)]
    res = pl.pallas_call(
        body, name=name, out_shape=tuple(out_shape), in_specs=tuple(in_specs), out_specs=tuple(out_specs),
        input_output_aliases={i: first + i for i in range(n)},
        compiler_params=pltpu.CompilerParams(has_side_effects=DATAFLOW),
    )(*operands)
    sems = (res[0], res[1]) if start is not None else None
    return list(res[first:first + n]), sems, res[-1]


class _GradExchange:
    def __init__(self):
        x, y, c = (lax.axis_index(a) for a in AXES)
        self.core = c.astype(jnp.int32)
        self.chip = (2 * x + y).astype(jnp.int32)
        self.pending = {}

    def _pair_copies(self, n_arr):
        def make(refs, send_sems, recv_sems):
            sibling, _ = _peer(1)
            other = 1 - lax.axis_index("c")
            srcs, lands = refs[:n_arr], refs[n_arr:]
            pick = lambda ref, ch: ref.at[ch, other] if len(ref.shape) == 4 else ref.at[ch]
            return [_remote(pick(srcs[a], ch), lands[a].at[ch], send_sems.at[a * N_CHIP + ch],
                            recv_sems.at[a * N_CHIP + ch], sibling)
                    for a in range(n_arr) for ch in range(N_CHIP)]
        return make

    def _chip_copies(self, n_arr, with_slab):
        def make(refs, send_sems, recv_sems):
            x, y = lax.axis_index("x"), lax.axis_index("y")
            my_chip = 2 * x + y
            srcs, lands = refs[:n_arr], refs[n_arr:2 * n_arr]
            copies = []
            for j, k in enumerate((2, 4, 6)):
                peer, peer_idx = _peer(k)
                for a in range(n_arr):
                    copies.append(_remote(srcs[a].at[peer_idx // 2], lands[a].at[my_chip],
                                          send_sems.at[3 * a + j], recv_sems.at[3 * a + j], peer))
            if with_slab:
                slab, slab_land = refs[2 * n_arr], refs[2 * n_arr + 1]
                for k in range(1, N_DEV):
                    peer, _ = _peer(k)
                    copies.append(_remote(slab, slab_land.at[_my_index()], send_sems.at[3 * n_arr + k - 1],
                                          recv_sems.at[3 * n_arr + k - 1], peer))
            return copies
        return make

    def push_pairs(self, tag, for_sibling):
        n_arr = len(for_sibling)
        lands = [lax.empty((N_CHIP,) + a.shape[-2:], a.dtype) for a in for_sibling]
        make = self._pair_copies(n_arr)
        arrays, sems, token = _comm_call(tag + "_start", list(for_sibling) + lands, start=(n_arr * N_CHIP, make))
        self.pending[tag] = (arrays, sems, make, n_arr)
        return token

    def pairs_done(self, tag, after):
        arrays, sems, make, n_arr = self.pending.pop(tag)
        arrays, _, _ = _comm_call(tag + "_wait", arrays, wait=(sems[0], sems[1], make), after=after)
        return arrays[:n_arr], arrays[n_arr:]

    def push_chips(self, tag, pair_sums, slab=None):
        n_arr = len(pair_sums)
        arrays = list(pair_sums) + [lax.empty(a.shape, a.dtype) for a in pair_sums]
        n_copies = 3 * n_arr
        if slab is not None:
            arrays += [slab, lax.empty((N_DEV,) + slab.shape, slab.dtype)]
            n_copies += N_DEV - 1
        make = self._chip_copies(n_arr, slab is not None)
        arrays, sems, token = _comm_call(tag + "_start", arrays, start=(n_copies, make))
        self.pending[tag] = (arrays, sems, make, n_arr)
        return token

    def chips_done(self, tag, after):
        arrays, sems, make, n_arr = self.pending.pop(tag)
        arrays, _, _ = _comm_call(tag + "_wait", arrays, wait=(sems[0], sems[1], make), after=after)
        return arrays[:n_arr], arrays[n_arr:2 * n_arr], arrays[2 * n_arr:]


def _cast_place(w, name):
    r, c = w.shape
    tr = r if r * c <= MIB else 1 << ((MIB // c).bit_length() - 1)
    assert r % tr == 0

    def body(me_ref, w_ref, o_ref):
        o_ref[...] = w_ref[...].astype(BF16)

    return pl.pallas_call(
        body, name=name, out_shape=jax.ShapeDtypeStruct((N_DEV, r, c), BF16),
        grid_spec=pltpu.PrefetchScalarGridSpec(
            num_scalar_prefetch=1, grid=(r // tr,),
            in_specs=[pl.BlockSpec((tr, c), lambda i, me_ref: (i, 0))],
            out_specs=pl.BlockSpec((None, tr, c), lambda i, me_ref: (me_ref[0], i, 0))),
        compiler_params=_params(("arbitrary",), 40),
    )(_my_index().astype(jnp.int32).reshape(1), w)


class _WeightGather:
    CHIPS = (2, 4, 6)

    def __init__(self, buffers):
        self.buffers = dict(buffers)
        self.pending = {}
        self.me = _my_index().astype(jnp.int32)

    def _own_slot_to(self, peers):
        def make(refs, send_sems, recv_sems):
            me = _my_index()
            return [_remote(ref.at[me], ref.at[me], send_sems.at[len(peers) * a + j],
                            recv_sems.at[len(peers) * a + j], _peer(k)[0])
                    for a, ref in enumerate(refs) for j, k in enumerate(peers)]
        return make

    def _forward_from(self, chips):
        def make(refs, send_sems, recv_sems):
            sibling, _ = _peer(1)
            copies = []
            for a, ref in enumerate(refs):
                for j, k in enumerate(chips):
                    slot = ref.at[_peer(k)[1]]
                    copies.append(_remote(slot, slot, send_sems.at[len(chips) * a + j],
                                          recv_sems.at[len(chips) * a + j], sibling))
            return copies
        return make

    def _run(self, call, names, **kw):
        arrays, sems, token = _comm_call(call, [self.buffers[k] for k in names], **kw)
        self.buffers.update(zip(names, arrays))
        return sems, token

    def start(self, tag, names, peers, after=()):
        make = self._own_slot_to(peers)
        sems, token = self._run(tag + "_start", names, start=(len(names) * len(peers), make), after=after)
        self.pending[tag] = (names, sems, make, peers)
        return token

    def arrived(self, tag, after):
        names, sems, make, _ = self.pending.pop(tag)
        self._run(tag + "_wait", names, wait=(sems[0], sems[1], make), after=after)

    def forward(self, tag, after):
        names, sems, make, peers = self.pending.pop(tag)
        chips = tuple(k for k in peers if k != 1)
        onward = self._forward_from(chips)
        new_sems, token = self._run(tag + "_forward", names, wait=(sems[0], sems[1], make),
                                    start=(len(chips) * len(names), onward), after=after)
        self.pending[tag + "/fwd"] = (names, new_sems, onward)
        return token

    def forwarded(self, tag, after):
        names, sems, make = self.pending.pop(tag + "/fwd")
        self._run(tag + "_done", names, wait=(sems[0], sems[1], make), after=after)


def _pair_add(mine, theirs, name, core=None):
    _, r, c_dim = theirs.shape
    tr = r if r * c_dim <= MIB else 1 << ((MIB // c_dim).bit_length() - 1)
    assert r % tr == 0
    stride = 1 if core is None else 2
    offset = jnp.zeros((1,), jnp.int32) if core is None else core.reshape(1)

    def body(off_ref, a_ref, b_ref, o_ref):
        o_ref[...] = (a_ref[...].astype(F32) + b_ref[...].astype(F32)).astype(BF16)

    blk = (None, tr, c_dim)
    return pl.pallas_call(
        body, name=name, out_shape=jax.ShapeDtypeStruct(theirs.shape, BF16),
        grid_spec=pltpu.PrefetchScalarGridSpec(
            num_scalar_prefetch=1, grid=(N_CHIP, r // tr),
            in_specs=[pl.BlockSpec(blk, lambda ch, i, off_ref: (stride * ch + off_ref[0], i, 0)),
                      pl.BlockSpec(blk, lambda ch, i, off_ref: (ch, i, 0))],
            out_specs=pl.BlockSpec(blk, lambda ch, i, off_ref: (ch, i, 0))),
        compiler_params=_params(("arbitrary", "arbitrary"), 40),
    )(offset, mine, theirs)


def _slab_exchange(slab, name):
    def body(slab_in, slab_out, send_sems, recv_sems, local_sem):
        me = _my_index()
        local = pltpu.make_async_copy(slab_in, slab_out.at[me], local_sem)
        local.start()
        sends = []
        for k in range(1, N_DEV):
            peer, _ = _peer(k)
            sends.append(_remote(slab_in, slab_out.at[me], send_sems.at[k - 1], recv_sems.at[k - 1], peer))
        for cp in sends:
            cp.start()
        for k in range(1, N_DEV):
            peer, peer_idx = _peer(k)
            slot = slab_out.at[peer_idx]
            _remote(slot, slot, send_sems.at[k - 1], recv_sems.at[k - 1], peer).wait_recv()
        for cp in sends:
            cp.wait_send()
        local.wait()

    return pl.pallas_call(
        body, name=name, out_shape=jax.ShapeDtypeStruct((N_DEV,) + slab.shape, slab.dtype),
        in_specs=[HBM_SPEC], out_specs=HBM_SPEC,
        scratch_shapes=[pltpu.SemaphoreType.DMA((N_DEV - 1,)), pltpu.SemaphoreType.DMA((N_DEV - 1,)),
                        pltpu.SemaphoreType.DMA],
        compiler_params=pltpu.CompilerParams(has_side_effects=True),
    )(slab)


def _adamw_math(w, g, m, v):
    m = ADAM_B1 * m + (1.0 - ADAM_B1) * g
    v = ADAM_B2 * v + (1.0 - ADAM_B2) * (g * g)
    m_hat = m / (1.0 - ADAM_B1 ** ADAM_STEP)
    v_hat = v / (1.0 - ADAM_B2 ** ADAM_STEP)
    delta = -ADAM_LR * (m_hat / (jnp.sqrt(v_hat) + ADAM_EPS) + ADAM_WD * w)
    return delta, m, v


def _adamw(parts, own, place, w, m, v, name, dep=None):
    n_parts = parts.shape[0]
    r, c = w.shape
    budget = 280 * 1024
    tr = r if r * c <= budget else 1 << ((budget // c).bit_length() - 1)
    assert r % tr == 0

    def body(place_ref, p_ref, own_ref, w_ref, m_ref, v_ref, g_ref, d_ref, nm_ref, nv_ref):
        mine = own_ref[...].astype(F32)
        g = None
        for i in range(n_parts):
            term = jnp.where(place_ref[0] == i, mine, p_ref[i].astype(F32))
            g = term if g is None else g + term
        delta, nm, nv = _adamw_math(w_ref[...], g, m_ref[...], v_ref[...])
        g_ref[...] = g
        d_ref[...] = delta
        nm_ref[...] = nm
        nv_ref[...] = nv

    blk = pl.BlockSpec((tr, c), lambda i, place_ref: (i, 0))
    shape = jax.ShapeDtypeStruct((r, c), F32)
    in_specs = [pl.BlockSpec((n_parts, tr, c), lambda i, place_ref: (0, i, 0)),
                pl.BlockSpec((None, tr, c), lambda i, place_ref: (place_ref[1], i, 0)), blk, blk, blk]
    operands = [parts, own, w, m, v]
    if dep is not None:
        body = _drop_arg(body, 1 + len(operands))
        in_specs.append(pl.BlockSpec((SUB, HEAD), lambda i, place_ref: (0, 0)))
        operands.append(dep)
    return pl.pallas_call(
        body, name=name, out_shape=[shape] * 4,
        grid_spec=pltpu.PrefetchScalarGridSpec(
            num_scalar_prefetch=1, grid=(r // tr,), in_specs=in_specs, out_specs=[blk] * 4),
        compiler_params=_params(("arbitrary",), 48),
    )(place, *operands)


def kernel(x, p, g_pre, w_in, w_s, b_s, ln_v_g, ln_v_b, g_q, g_k, rel_bias, g_out_a, g_out_b, w_out, g_ple, w_ple_gate, w_ple_up, loss_target, m_g_pre, m_w_in, m_w_s, m_b_s, m_ln_v_g, m_ln_v_b, m_g_q, m_g_k, m_rel_bias, m_g_out_a, m_g_out_b, m_w_out, m_g_ple, m_w_ple_gate, m_w_ple_up, v_g_pre, v_w_in, v_w_s, v_b_s, v_ln_v_g, v_ln_v_b, v_g_q, v_g_k, v_rel_bias, v_g_out_a, v_g_out_b, v_w_out, v_g_ple, v_w_ple_gate, v_w_ple_up):
    args = dict(locals())
    small = {"g_pre": g_pre, "w_s": w_s[0], "b_s": b_s[0], "ln_v_g": ln_v_g, "ln_v_b": ln_v_b, "g_q": g_q,
             "g_k": g_k, "rel_bias": rel_bias, "g_out_a": g_out_a, "g_out_b": g_out_b, "g_ple": g_ple}
    big_names = ("w_in", "w_out", "w_ple_gate", "w_ple_up")
    big = {k: args[k][0] for k in big_names}

    wg = _WeightGather({k: _cast_place(big[k], "place_" + k) for k in big_names})
    ex = _GradExchange()
    results = {}

    def big_adamw(k, delivered, dep=None):
        mine, theirs = delivered[k]
        place = jnp.stack([ex.chip, ex.chip])
        return _adamw(theirs, mine, place, big[k], args["m_" + k][0], args["v_" + k][0], "adamw_" + k, dep=dep)

    def while_last_travels(token, delivered):
        done = []
        for k in big_names[1:]:
            results[k] = big_adamw(k, delivered, dep=token)
            done.append(results[k][0])
        return done

    loss, grad_x, small_parts, delivered, slabs, _ = _local_step(
        x[0], p[0, 0], loss_target[0], small, wg, ex, while_last_travels)
    results["w_in"] = big_adamw("w_in", delivered)
    for k in big_names:
        results[k] = [t[None] for t in results[k]]

    squeeze = lambda t: {k: (t[k][0] if k in ("w_s", "b_s") else t[k]) for k in SMALL_NAMES}
    small_m = squeeze({k: args["m_" + k] for k in SMALL_NAMES})
    small_v = squeeze({k: args["v_" + k] for k in SMALL_NAMES})
    me = _my_index().astype(jnp.int32)
    place = jnp.stack([me, jnp.zeros((), jnp.int32)])
    last_slab = _pack_small(small_parts, SMALL_LAST)
    groups = ((SMALL_EARLY, slabs[1], slabs[0], "adamw_small"),
              (SMALL_LAST, _slab_exchange(last_slab, "last_exchange"), last_slab, "adamw_last"))
    for names_, parts, own, call_name in groups:
        packed = _adamw(parts, own[None], place, _pack_small(small, names_), _pack_small(small_m, names_),
                        _pack_small(small_v, names_), call_name)
        for idx in range(4):
            tree = _unpack_small(packed[idx], small, names_)
            for k in names_:
                results.setdefault(k, [None] * 4)[idx] = tree[k].reshape(args[k].shape)

    names = ("g_pre", "w_in", "w_s", "b_s", "ln_v_g", "ln_v_b", "g_q", "g_k", "rel_bias", "g_out_a", "g_out_b",
             "w_out", "g_ple", "w_ple_gate", "w_ple_up")
    total = lax.psum(loss, AXES)
    out = [total, grad_x[None]]
    for idx in range(4):
        out += [results[k][idx] for k in names]
    return tuple(out)
```

```python
import functools
import math

import numpy as np
import jax
import jax.numpy as jnp
from jax import lax
from jax.experimental import pallas as pl
from jax.experimental.pallas import tpu as pltpu

F32 = jnp.float32
BF16 = jnp.bfloat16
EPS = 1e-6
NEG_INF = -1e30
HEAD = 128
DILATIONS = (1, 4, 16)
NUM_BUCKETS = 32
MAX_DISTANCE = 2048
N_SEG = 7
ADAM_LR = 0.001
ADAM_B1 = 0.9
ADAM_B2 = 0.999
ADAM_EPS = 1e-08
ADAM_WD = 0.01
ADAM_STEP = 10
AXES = ("x", "y", "c")
N_DEV = 8
MIB = 1 << 20

SUB = 8

CHUNK_ORDER = np.array([16 * (r % SUB) + r // SUB for r in range(HEAD)])
BLOCK_ORDER = {
    1: CHUNK_ORDER,
    4: np.array([32 * (r // 32) + 4 * (r % SUB) + (r // SUB) % 4 for r in range(HEAD)]),
    16: np.arange(HEAD),
}

NT_DIMS = (((1,), (1,)), ((), ()))
TN_DIMS = (((0,), (0,)), ((), ()))
NN_DIMS = (((1,), (0,)), ((), ()))


def _params(semantics, vmem_mib):
    return pltpu.CompilerParams(dimension_semantics=semantics, vmem_limit_bytes=vmem_mib * MIB)


def _gelu(a):
    return 0.5 * a * (1.0 + lax.erf(a * (2.0 ** -0.5)))


def _gelu_and_grad(a):
    cdf = 0.5 * (1.0 + lax.erf(a * (2.0 ** -0.5)))
    return a * cdf, cdf + a * jnp.exp(-0.5 * a * a) * ((2.0 * math.pi) ** -0.5)


def _silu_and_grad(a):
    s = jax.nn.sigmoid(a)
    return a * s, s * (1.0 + a * (1.0 - s))


def _rms(v):
    return lax.rsqrt(jnp.mean(v * v, axis=-1, keepdims=True) + EPS)


def _rms_bwd(dy, v, r, g):
    gy = dy * g
    return r * gy - v * (r * r * r) * jnp.mean(gy * v, axis=-1, keepdims=True)


def _dot(a, b, dims=NN_DIMS):
    return lax.dot_general(a, b, dims, preferred_element_type=F32)


def _lane_pick(cols, width):
    rows = cols[0].shape[0]
    lane = lax.broadcasted_iota(jnp.int32, (rows, width), 1)
    out = jnp.zeros((rows, width), F32)
    for h, col in enumerate(cols):
        out = jnp.where(lane == h, col, out)
    return out


def _chunk_perm():
    return jnp.asarray(np.eye(HEAD, dtype=np.float32)[CHUNK_ORDER], BF16)


def _unpermute_f32(p, v):
    hi = v.astype(BF16)
    rest = v - hi.astype(F32)
    mid = rest.astype(BF16)
    lo = (rest - mid.astype(F32)).astype(BF16)
    return _dot(p, hi, TN_DIMS) + _dot(p, mid, TN_DIMS) + _dot(p, lo, TN_DIMS)


def _rmsnorm_fwd(x, g, name, permute, dep=None):
    s, d = x.shape
    tm = HEAD

    def body(x_ref, g_ref, p_ref, o_ref):
        v = x_ref[...]
        out = (v * _rms(v) * g_ref[...]).astype(BF16)
        if permute:
            out = _dot(p_ref[...], out).astype(BF16)
        o_ref[...] = out

    in_specs = [pl.BlockSpec((tm, d), lambda i: (i, 0)), pl.BlockSpec((1, d), lambda i: (0, 0)),
                pl.BlockSpec((HEAD, HEAD), lambda i: (0, 0))]
    operands = [x, g, _chunk_perm()]
    if dep is not None:
        body = _drop_arg(body, len(operands))
        in_specs.append(DEP_SPEC)
        operands.append(dep)
    return pl.pallas_call(
        body, name=name, grid=(s // tm,),
        out_shape=jax.ShapeDtypeStruct((s, d), BF16), in_specs=in_specs,
        out_specs=pl.BlockSpec((tm, d), lambda i: (i, 0)),
        compiler_params=_params(("arbitrary",), 40),
    )(*operands)


def _rmsnorm_bwd(dy, v, g, res, name, dy_permuted, with_bf16):
    s, d = v.shape
    tm = HEAD
    perm = _chunk_perm()

    def body(dy_ref, v_ref, g_ref, res_ref, p_ref, *outs):
        dx_ref, dg_ref = outs[0], outs[-1]
        i = pl.program_id(0)
        vv, dyv = v_ref[...], dy_ref[...]
        if dy_permuted:
            dyv = _unpermute_f32(p_ref[...], dyv)
        r = _rms(vv)
        dx = res_ref[...] + _rms_bwd(dyv, vv, r, g_ref[...])
        dx_ref[...] = dx
        if with_bf16:
            dxb = dx.astype(BF16)
            outs[1][...] = dxb
            outs[2][...] = _dot(p_ref[...], dxb).astype(BF16)

        @pl.when(i == 0)
        def _():
            dg_ref[...] = jnp.zeros_like(dg_ref)

        dg_ref[...] += jnp.sum(dyv * vv * r, axis=0, keepdims=True)

    row = pl.BlockSpec((tm, d), lambda i: (i, 0))
    vec = pl.BlockSpec((1, d), lambda i: (0, 0))
    shapes = [jax.ShapeDtypeStruct((s, d), F32)]
    specs = [row]
    if with_bf16:
        shapes += [jax.ShapeDtypeStruct((s, d), BF16)] * 2
        specs += [row, row]
    shapes.append(jax.ShapeDtypeStruct((1, d), F32))
    specs.append(vec)
    return pl.pallas_call(
        body, name=name, grid=(s // tm,), out_shape=shapes,
        in_specs=[row, row, vec, row, pl.BlockSpec((HEAD, HEAD), lambda i: (0, 0))], out_specs=specs,
        compiler_params=_params(("arbitrary",), 40),
    )(dy, v, g, res, perm)


DEP_SPEC = pl.BlockSpec((SUB, HEAD), lambda *_: (0, 0))


def _drop_arg(body, pos):
    return lambda *refs: body(*refs[:pos], *refs[pos + 1:])


def _matmul(a, b, *, name, grid, a_spec, b_spec, dims, acc_shape, out_shapes, out_specs,
            extra=(), extra_specs=(), epilogue=None, vmem_mib=48, dep=None, prefetch=None, carry=None):
    nk = grid[2]
    n_user = len(extra)
    for unread, spec in ((dep, DEP_SPEC), (carry, HBM_SPEC)):
        if unread is not None:
            extra, extra_specs = tuple(extra) + (unread,), tuple(extra_specs) + (spec,)
    n_extra, n_out = len(extra), len(out_shapes)
    n_pre = 0 if prefetch is None else 1
    aliases = {} if carry is None else {n_pre + 2 + n_extra - 1: 0}

    def body(*refs):
        refs = refs[n_pre:]
        a_ref, b_ref = refs[0], refs[1]
        ex = refs[2:2 + n_user]
        outs = refs[2 + n_extra:2 + n_extra + n_out]
        acc = refs[-1]
        k = pl.program_id(2)

        @pl.when(k == 0)
        def _():
            acc[...] = jnp.zeros_like(acc)

        av = a_ref[...]
        if av.dtype != BF16:
            av = av.astype(BF16)
        acc[...] += _dot(av, b_ref[...], dims)

        @pl.when(k == nk - 1)
        def _():
            if epilogue is None:
                outs[0][...] = acc[...].astype(outs[0].dtype)
            else:
                epilogue(acc, ex, outs)

    scratch = [pltpu.VMEM(acc_shape, F32)]
    params = _params(("parallel", "parallel", "arbitrary"), vmem_mib)
    if prefetch is None:
        return pl.pallas_call(
            body, name=name, grid=grid, out_shape=list(out_shapes),
            in_specs=[a_spec, b_spec, *extra_specs], out_specs=list(out_specs),
            scratch_shapes=scratch, compiler_params=params, input_output_aliases=aliases,
        )(a, b, *extra)
    return pl.pallas_call(
        body, name=name, out_shape=list(out_shapes),
        grid_spec=pltpu.PrefetchScalarGridSpec(
            num_scalar_prefetch=1, grid=grid, in_specs=[a_spec, b_spec, *extra_specs],
            out_specs=list(out_specs), scratch_shapes=scratch),
        compiler_params=params, input_output_aliases=aliases,
    )(prefetch, a, b, *extra)


def _tile(n, want):
    t = min(n, want)
    while n % t:
        t //= 2
    return t


def _rel_buckets(dil):
    order = BLOCK_ORDER[dil]
    qi = jnp.asarray(HEAD + order)
    kj = jnp.asarray(np.concatenate([order, HEAD + order]))
    delta = qi[:, None] - kj[None, :]
    band = (delta >= 0) & (delta <= HEAD)
    dist = jnp.clip(delta, 0, None) * dil
    max_exact = NUM_BUCKETS // 2
    dd = jnp.maximum(dist, 1).astype(F32)
    large = max_exact + (jnp.log(dd / max_exact) / math.log(MAX_DISTANCE / max_exact)
                         * (NUM_BUCKETS - max_exact)).astype(jnp.int32)
    large = jnp.minimum(large, NUM_BUCKETS - 1)
    bucket = jnp.where(dist < max_exact, dist, large)
    return jnp.where(band, bucket, -1).astype(jnp.int32)


def _bias_build(rel_bias, buckets, n_heads):
    nd = buckets.shape[0]

    def body(rb_ref, bk_ref, o_ref):
        for c in range(nd):
            def per_head(h, carry, c=c):
                bk = bk_ref[c]
                acc = jnp.where(bk < 0, NEG_INF, 0.0).astype(F32)
                for b in range(NUM_BUCKETS):
                    acc = jnp.where(bk == b, rb_ref[b, h], acc)
                o_ref[c, h] = acc
                return carry

            lax.fori_loop(0, n_heads, per_head, 0)

    return pl.pallas_call(
        body, name="bias_build",
        out_shape=jax.ShapeDtypeStruct((nd, n_heads, HEAD, 2 * HEAD), F32),
        in_specs=[pl.BlockSpec(memory_space=pltpu.SMEM), pl.BlockSpec(memory_space=pltpu.VMEM)],
        out_specs=pl.BlockSpec(memory_space=pltpu.VMEM),
    )(rel_bias, buckets)


def _bias_grad(ds_all, buckets, n_heads):
    nd = buckets.shape[0]
    pairs = HEAD * 2 * HEAD

    def body(ds_ref, bk_ref, o_ref):
        rows = lax.broadcasted_iota(jnp.int32, (NUM_BUCKETS, pairs), 0)
        tot = jnp.zeros((n_heads, NUM_BUCKETS), F32)
        for c in range(nd):
            onehot = (rows == bk_ref[c]).astype(BF16)
            ds = ds_ref[c]
            hi = ds.astype(BF16)
            lo = (ds - hi.astype(F32)).astype(BF16)
            tot = tot + _dot(hi, onehot, NT_DIMS) + _dot(lo, onehot, NT_DIMS)
        o_ref[...] = tot

    out = pl.pallas_call(
        body, name="bias_grad",
        out_shape=jax.ShapeDtypeStruct((n_heads, NUM_BUCKETS), F32),
        in_specs=[pl.BlockSpec(memory_space=pltpu.VMEM), pl.BlockSpec(memory_space=pltpu.VMEM)],
        out_specs=pl.BlockSpec(memory_space=pltpu.VMEM),
        compiler_params=pltpu.CompilerParams(vmem_limit_bytes=40 * MIB),
    )(ds_all.reshape(nd, n_heads, pairs), buckets.reshape(nd, 1, pairs))
    return out.T


def _qkv_prep(proj, g_q, g_k, w, dep=None):
    s = proj.shape[0]
    n_heads = w // HEAD
    tm = HEAD

    def body(q_ref, k_ref, gq_ref, gk_ref, qn_ref, kn_ref):
        gq = gq_ref[...] * (HEAD ** -0.5)
        gk = gk_ref[...]
        for h in range(n_heads):
            sl = slice(h * HEAD, (h + 1) * HEAD)
            q = q_ref[:, sl]
            k = k_ref[:, sl]
            qn_ref[:, sl] = q * _rms(q) * gq
            kn_ref[:, sl] = k * _rms(k) * gk

    seg = lambda j: pl.BlockSpec((tm, w), lambda i, j=j: (i, j))
    vec = pl.BlockSpec((1, HEAD), lambda i: (0, 0))
    out = pl.BlockSpec((tm, w), lambda i: (i, 0))
    in_specs = [seg(3), seg(4), vec, vec]
    operands = [proj, proj, g_q, g_k]
    if dep is not None:
        body = _drop_arg(body, len(operands))
        in_specs.append(DEP_SPEC)
        operands.append(dep)
    return pl.pallas_call(
        body, name="qkv_prep", grid=(s // tm,),
        out_shape=[jax.ShapeDtypeStruct((s, w), F32)] * 2,
        in_specs=in_specs, out_specs=[out, out],
        compiler_params=_params(("arbitrary",), 40),
    )(*operands)


class _BlockView:
    def __init__(self, s, dil):
        assert s % (HEAD * dil) == 0 and dil in BLOCK_ORDER
        self.nb = s // (HEAD * dil)
        if dil == 1:
            self.lead, self.block = (s,), (HEAD,)
            self.index = lambda r, n: (n,)
        elif dil == 4:
            self.lead, self.block = (s // 512, 4, 4, 4, SUB), (None, 4, 4, None, SUB)
            self.index = lambda r, n: (n, 0, 0, r, 0)
        else:
            self.lead, self.block = (s // 2048, 16, 16, SUB), (None, 16, None, SUB)
            self.index = lambda r, n: (n, 0, r, 0)

    def view(self, t):
        return t.reshape(self.lead + (t.shape[-1],))

    def spec(self, width, block_of, column=0):
        return pl.BlockSpec(self.block + (width,), lambda r, n: self.index(r, block_of(r, n)) + (column,))


def _rows(ref, lanes=slice(None)):
    v = ref[(slice(None),) * (len(ref.shape) - 1) + (lanes,)]
    return v.reshape(HEAD, v.shape[-1])


def _set_rows(ref, lanes, value):
    ref[(slice(None),) * (len(ref.shape) - 1) + (lanes,)] = value.reshape(ref.shape[:-1] + (value.shape[-1],))


V_SEGMENT = 5


def _attn_fwd(qn, kn, proj, bias, dil, name):
    s, w = qn.shape
    n_heads = w // HEAD
    bv = _BlockView(s, dil)

    def body(q_ref, kc_ref, kp_ref, vc_ref, vp_ref, bias_ref, o_ref, lse_ref, s_scr, e_scr, lse_scr, inv_scr):
        n = pl.program_id(1)
        heads = [slice(h * HEAD, (h + 1) * HEAD) for h in range(n_heads)]
        lse_scr[...] = jnp.zeros_like(lse_scr)
        for h, sl in enumerate(heads):
            q = _rows(q_ref, sl).astype(BF16)
            s_p = _dot(q, _rows(kp_ref, sl).astype(BF16), NT_DIMS) + bias_ref[h, :, :HEAD]
            s_scr[h, :, :HEAD] = jnp.where(n > 0, s_p, NEG_INF)
            s_scr[h, :, HEAD:] = _dot(q, _rows(kc_ref, sl).astype(BF16), NT_DIMS) + bias_ref[h, :, HEAD:]
        for h in range(n_heads):
            sc = s_scr[h]
            m = jnp.max(sc, axis=-1, keepdims=True)
            e = jnp.exp(sc - m)
            den = jnp.sum(e, axis=-1, keepdims=True)
            e_scr[h] = e.astype(BF16)
            lse_scr[:, h:h + 1] = m + jnp.log(den)
            inv_scr[:, h:h + 1] = 1.0 / den
        for h, sl in enumerate(heads):
            o = (_dot(e_scr[h, :, :HEAD], _rows(vp_ref, sl).astype(BF16))
                 + _dot(e_scr[h, :, HEAD:], _rows(vc_ref, sl).astype(BF16)))
            _set_rows(o_ref, sl, o * inv_scr[:, h:h + 1])
        _set_rows(lse_ref, slice(None), lse_scr[...])

    cur = bv.spec(w, lambda r, n: n)
    prev = bv.spec(w, lambda r, n: jnp.maximum(n - 1, 0))
    v_cur = bv.spec(w, lambda r, n: n, V_SEGMENT)
    v_prev = bv.spec(w, lambda r, n: jnp.maximum(n - 1, 0), V_SEGMENT)
    o, lse = pl.pallas_call(
        body, name=name, grid=(dil, bv.nb),
        out_shape=[jax.ShapeDtypeStruct(bv.lead + (w,), F32), jax.ShapeDtypeStruct(bv.lead + (HEAD,), F32)],
        in_specs=[cur, cur, prev, v_cur, v_prev,
                  pl.BlockSpec((n_heads, HEAD, 2 * HEAD), lambda r, n: (0, 0, 0))],
        out_specs=[cur, bv.spec(HEAD, lambda r, n: n)],
        scratch_shapes=[pltpu.VMEM((n_heads, HEAD, 2 * HEAD), F32), pltpu.VMEM((n_heads, HEAD, 2 * HEAD), BF16),
                        pltpu.VMEM((HEAD, HEAD), F32), pltpu.VMEM((HEAD, HEAD), F32)],
        compiler_params=_params(("arbitrary", "arbitrary"), 48),
    )(bv.view(qn), bv.view(kn), bv.view(kn), bv.view(proj), bv.view(proj), bias)
    return o.reshape(s, w), lse.reshape(s, HEAD)


def _attn_bwd(qn, kn, proj, dyb, lse, delta, bias, dil, name, running=None, dep=None):
    s, w = qn.shape
    n_heads = w // HEAD
    bv = _BlockView(s, dil)
    nb = bv.nb

    n_run = 0 if running is None else 3

    def body(q_ref, kc_ref, kp_ref, vc_ref, vp_ref, dy_ref, lse_ref, dl_ref, bias_ref, *rest):
        so_far = rest[:n_run]
        dq_ref, dk_ref, dv_ref, ds_ref, carry_k, carry_v, s_scr, dp_scr, p_scr, dsb_scr = rest[n_run:]
        base = (lambda i, sl: _rows(so_far[i], sl)) if n_run else (lambda i, sl: 0.0)
        r = pl.program_id(0)
        step = pl.program_id(1)
        blk = nb - 1 - step

        @pl.when((r == 0) & (step == 0))
        def _():
            ds_ref[...] = jnp.zeros_like(ds_ref)

        @pl.when(step == 0)
        def _():
            carry_k[...] = jnp.zeros_like(carry_k)
            carry_v[...] = jnp.zeros_like(carry_v)

        heads = [slice(h * HEAD, (h + 1) * HEAD) for h in range(n_heads)]
        tots = _rows(lse_ref)
        dls = _rows(dl_ref)
        for h, sl in enumerate(heads):
            q, dy = _rows(q_ref, sl).astype(BF16), _rows(dy_ref, sl).astype(BF16)
            kp, kc = _rows(kp_ref, sl).astype(BF16), _rows(kc_ref, sl).astype(BF16)
            vp, vc = _rows(vp_ref, sl).astype(BF16), _rows(vc_ref, sl).astype(BF16)
            s_p = _dot(q, kp, NT_DIMS) + bias_ref[h, :, :HEAD]
            s_scr[h, :, :HEAD] = jnp.where(blk > 0, s_p, NEG_INF)
            s_scr[h, :, HEAD:] = _dot(q, kc, NT_DIMS) + bias_ref[h, :, HEAD:]
            dp_scr[h, :, :HEAD] = _dot(dy, vp, NT_DIMS)
            dp_scr[h, :, HEAD:] = _dot(dy, vc, NT_DIMS)
        for h in range(n_heads):
            prob = jnp.exp(s_scr[h] - tots[:, h:h + 1])
            ds = prob * (dp_scr[h] - dls[:, h:h + 1])
            ds_ref[h] += ds
            p_scr[h] = prob.astype(BF16)
            dsb_scr[h] = ds.astype(BF16)
        for h, sl in enumerate(heads):
            q, dy = _rows(q_ref, sl).astype(BF16), _rows(dy_ref, sl).astype(BF16)
            kp, kc = _rows(kp_ref, sl).astype(BF16), _rows(kc_ref, sl).astype(BF16)
            ds_pb, ds_cb = dsb_scr[h, :, :HEAD], dsb_scr[h, :, HEAD:]
            _set_rows(dq_ref, sl, _dot(ds_pb, kp) + _dot(ds_cb, kc) + base(0, sl))
            _set_rows(dk_ref, sl, _dot(ds_cb, q, TN_DIMS) + carry_k[:, sl] + base(1, sl))
            carry_k[:, sl] = _dot(ds_pb, q, TN_DIMS)
            _set_rows(dv_ref, sl, _dot(p_scr[h, :, HEAD:], dy, TN_DIMS) + carry_v[:, sl] + base(2, sl))
            carry_v[:, sl] = _dot(p_scr[h, :, :HEAD], dy, TN_DIMS)

    cur = bv.spec(w, lambda r, n: nb - 1 - n)
    prev = bv.spec(w, lambda r, n: jnp.maximum(nb - 2 - n, 0))
    stat = bv.spec(HEAD, lambda r, n: nb - 1 - n)
    whole = pl.BlockSpec((n_heads, HEAD, 2 * HEAD), lambda r, n: (0, 0, 0))
    big = jax.ShapeDtypeStruct(bv.lead + (w,), F32)
    v_cur = bv.spec(w, lambda r, n: nb - 1 - n, V_SEGMENT)
    v_prev = bv.spec(w, lambda r, n: jnp.maximum(nb - 2 - n, 0), V_SEGMENT)
    in_specs = [cur, cur, prev, v_cur, v_prev, cur, stat, stat, whole]
    operands = [bv.view(qn), bv.view(kn), bv.view(kn), bv.view(proj), bv.view(proj), bv.view(dyb), bv.view(lse),
                bv.view(delta), bias]
    aliases = {}
    if running is not None:
        aliases = {len(operands) + i: i for i in range(3)}
        in_specs += [cur] * 3
        operands += [bv.view(t) for t in running]
    if dep is not None:
        body = _drop_arg(body, len(operands))
        in_specs.append(DEP_SPEC)
        operands.append(dep)
    dq, dk, dv, ds = pl.pallas_call(
        body, name=name, grid=(dil, nb),
        out_shape=[big, big, big, jax.ShapeDtypeStruct((n_heads, HEAD, 2 * HEAD), F32)],
        in_specs=in_specs, out_specs=[cur, cur, cur, whole], input_output_aliases=aliases,
        scratch_shapes=[pltpu.VMEM((HEAD, w), F32), pltpu.VMEM((HEAD, w), F32),
                        pltpu.VMEM((n_heads, HEAD, 2 * HEAD), F32), pltpu.VMEM((n_heads, HEAD, 2 * HEAD), F32),
                        pltpu.VMEM((n_heads, HEAD, 2 * HEAD), BF16), pltpu.VMEM((n_heads, HEAD, 2 * HEAD), BF16)],
        compiler_params=_params(("arbitrary", "arbitrary"), 56),
    )(*operands)
    return dq.reshape(s, w), dk.reshape(s, w), dv.reshape(s, w), ds


def _qkv_bwd(dproj, proj, dq, dk, dv, g_q, g_k, w):
    s = proj.shape[0]
    n_heads = w // HEAD
    tm = HEAD

    def body(dproj_hbm, q_ref, k_ref, gq_ref, gk_ref, dq_ref, dk_ref, dv_ref, out_ref, dgq_ref, dgk_ref):
        i = pl.program_id(0)
        gq = gq_ref[...] * (HEAD ** -0.5)
        gk = gk_ref[...]
        acc_q = jnp.zeros((1, HEAD), F32)
        acc_k = jnp.zeros((1, HEAD), F32)
        for h in range(n_heads):
            sl = slice(h * HEAD, (h + 1) * HEAD)
            q, k = q_ref[:, sl], k_ref[:, sl]
            dqn, dkn = dq_ref[:, sl], dk_ref[:, sl]
            rq, rk = _rms(q), _rms(k)
            out_ref[:, h * HEAD:(h + 1) * HEAD] = _rms_bwd(dqn, q, rq, gq).astype(BF16)
            out_ref[:, w + h * HEAD:w + (h + 1) * HEAD] = _rms_bwd(dkn, k, rk, gk).astype(BF16)
            acc_q += jnp.sum(dqn * q * rq, axis=0, keepdims=True)
            acc_k += jnp.sum(dkn * k * rk, axis=0, keepdims=True)
        out_ref[:, 2 * w:] = dv_ref[...].astype(BF16)

        @pl.when(i == 0)
        def _():
            dgq_ref[...] = jnp.zeros_like(dgq_ref)
            dgk_ref[...] = jnp.zeros_like(dgk_ref)

        dgq_ref[...] += acc_q * (HEAD ** -0.5)
        dgk_ref[...] += acc_k

    seg = lambda j: pl.BlockSpec((tm, w), lambda i, j=j: (i, j))
    vec = pl.BlockSpec((1, HEAD), lambda i: (0, 0))
    row = pl.BlockSpec((tm, w), lambda i: (i, 0))
    return pl.pallas_call(
        body, name="qkv_bwd", grid=(s // tm,),
        out_shape=[jax.ShapeDtypeStruct(dproj.shape, BF16),
                   jax.ShapeDtypeStruct((1, HEAD), F32), jax.ShapeDtypeStruct((1, HEAD), F32)],
        in_specs=[pl.BlockSpec(memory_space=pl.ANY), seg(3), seg(4), vec, vec] + [row] * 3,
        out_specs=[pl.BlockSpec((tm, 3 * w), lambda i: (i, 1)), vec, vec],
        input_output_aliases={0: 0},
        compiler_params=_params(("arbitrary",), 48),
    )(dproj, proj, proj, g_q, g_k, dq, dk, dv)


def _mixer_a(u, gv, ws_ref, bst_ref, lng, lnb, z_scr, ln_scr):
    n_groups = u.shape[1] // HEAD
    mu = jnp.mean(gv, axis=-1, keepdims=True)
    xc = gv - mu
    rs = lax.rsqrt(jnp.mean(xc * xc, axis=-1, keepdims=True) + EPS)
    xhat = xc * rs
    ln_scr[...] = (xhat * lng + lnb).astype(BF16)
    causal = _causal_mask()
    for g in range(n_groups):
        sl = slice(g * HEAD, (g + 1) * HEAD)
        wm = jnp.where(causal, ws_ref[g], 0.0).astype(BF16)
        z_scr[:, sl] = _dot(wm, ln_scr[:, sl]) + bst_ref[:, g:g + 1]
    return u, xhat, rs


def _causal_mask():
    token = lambda r: 16 * (r % SUB) + r // SUB
    row = lax.broadcasted_iota(jnp.int32, (HEAD, HEAD), 0)
    col = lax.broadcasted_iota(jnp.int32, (HEAD, HEAD), 1)
    return token(col) <= token(row)


def _merge_b(o_refs, lse_refs, yb_scr):
    n_heads = yb_scr.shape[1] // HEAD
    lses = [t[...] for t in lse_refs]
    m = jnp.maximum(jnp.maximum(lses[0], lses[1]), lses[2])
    tot = m + jnp.log(sum(jnp.exp(t - m) for t in lses))
    alphas = [jnp.exp(t - tot) for t in lses]
    for h in range(n_heads):
        sl = slice(h * HEAD, (h + 1) * HEAD)
        yb_scr[:, sl] = sum(a[:, h:h + 1] * o[:, sl].astype(F32) for a, o in zip(alphas, o_refs))
    return tot


def _mix_fwd(proj, outs, lses, w_s, bst, ln_g, ln_b, g_a, g_b, w, dep=None):
    s = proj.shape[0]
    n_groups = w // HEAD

    def body(au_ref, av_ref, az_ref, bz_ref, o1, o2, o3, l1, l2, l3, ws_ref, bst_ref,
             lng_ref, lnb_ref, ga_ref, gb_ref, p_ref, y_ref, z_scr, ln_scr, yb_scr):
        u, _, _ = _mixer_a(_gelu(au_ref[...]), _gelu(av_ref[...]), ws_ref, bst_ref, lng_ref[...], lnb_ref[...],
                           z_scr, ln_scr)
        ya = u * z_scr[...]
        silu_a, _ = _silu_and_grad(az_ref[...])
        perm = p_ref[...]
        y_ref[:, :w] = _dot(perm, (ya * _rms(ya) * ga_ref[...] * silu_a).astype(BF16), TN_DIMS).astype(BF16)
        _merge_b((o1, o2, o3), (l1, l2, l3), yb_scr)
        yb = yb_scr[...]
        silu_b, _ = _silu_and_grad(bz_ref[...])
        y_ref[:, w:] = _dot(perm, (yb * _rms(yb) * gb_ref[...] * silu_b).astype(BF16), TN_DIMS).astype(BF16)

    seg = lambda j: pl.BlockSpec((HEAD, w), lambda i, j=j: (i, j))
    row = pl.BlockSpec((HEAD, w), lambda i: (i, 0))
    stat = pl.BlockSpec((HEAD, HEAD), lambda i: (i, 0))
    vec = pl.BlockSpec((1, w), lambda i: (0, 0))
    in_specs = [seg(0), seg(1), seg(2), seg(6), row, row, row, stat, stat, stat,
                pl.BlockSpec((n_groups, HEAD, HEAD), lambda i: (0, 0, 0)),
                pl.BlockSpec((HEAD, n_groups), lambda i: (0, 0)), vec, vec, vec, vec,
                pl.BlockSpec((HEAD, HEAD), lambda i: (0, 0))]
    operands = [proj, proj, proj, proj, *outs, *lses, w_s, bst, ln_g, ln_b, g_a, g_b, _chunk_perm()]
    if dep is not None:
        body = _drop_arg(body, len(operands))
        in_specs.append(DEP_SPEC)
        operands.append(dep)
    return pl.pallas_call(
        body, name="mix_fwd", grid=(s // HEAD,),
        out_shape=jax.ShapeDtypeStruct((s, 2 * w), BF16), in_specs=in_specs,
        out_specs=pl.BlockSpec((HEAD, 2 * w), lambda i: (i, 0)),
        scratch_shapes=[pltpu.VMEM((HEAD, w), F32), pltpu.VMEM((HEAD, w), BF16), pltpu.VMEM((HEAD, w), F32)],
        compiler_params=_params(("arbitrary",), 48),
    )(*operands)


def _mix_bwd(proj, dy, outs, lses, w_s, bst, ln_g, ln_b, g_a, g_b, w):
    s = proj.shape[0]
    n_groups = w // HEAD

    def body(au_ref, av_ref, az_ref, bz_ref, dy_ref, o1, o2, o3, l1, l2, l3, ws_ref, bst_ref,
             lng_ref, lnb_ref, ga_ref, gb_ref,
             dproj_ref, dyb_ref, tot_ref, dl_ref, dws_ref, dbst_ref, dlng_ref, dlnb_ref, dga_ref, dgb_ref,
             z_scr, ln_scr, yb_scr, dz_scr, dln_scr):
        i = pl.program_id(0)

        @pl.when(i == 0)
        def _():
            for t in (dws_ref, dbst_ref, dlng_ref, dlnb_ref, dga_ref, dgb_ref):
                t[...] = jnp.zeros_like(t)

        az = az_ref[...]
        lng = lng_ref[...]
        u, du_dau = _gelu_and_grad(au_ref[...])
        gv, dgv_dav = _gelu_and_grad(av_ref[...])
        u, xhat, rs = _mixer_a(u, gv, ws_ref, bst_ref, lng, lnb_ref[...], z_scr, ln_scr)
        z = z_scr[...]
        ya = u * z
        ra = _rms(ya)
        silu_a, dsilu_a = _silu_and_grad(az)
        dya_all = dy_ref[:, :w]
        na = ya * ra * ga_ref[...]
        dna = dya_all * silu_a
        dproj_ref[:, 2 * w:3 * w] = (dya_all * na * dsilu_a).astype(BF16)
        dga_ref[...] += jnp.sum(dna * ya * ra, axis=0, keepdims=True)
        dya = _rms_bwd(dna, ya, ra, ga_ref[...])
        dproj_ref[:, :w] = (dya * z * du_dau).astype(BF16)
        dz_scr[...] = (dya * u).astype(BF16)

        causal = _causal_mask()
        for g in range(n_groups):
            sl = slice(g * HEAD, (g + 1) * HEAD)
            wm = jnp.where(causal, ws_ref[g], 0.0).astype(BF16)
            dz = dz_scr[:, sl]
            dln_scr[:, sl] = _dot(wm, dz, TN_DIMS)
            dws_ref[g] += jnp.where(causal, _dot(dz, ln_scr[:, sl], NT_DIMS), 0.0)
            dbst_ref[:, g:g + 1] += jnp.sum(dz.astype(F32), axis=-1, keepdims=True)
        dln = dln_scr[...]
        dlng_ref[...] += jnp.sum(dln * xhat, axis=0, keepdims=True)
        dlnb_ref[...] += jnp.sum(dln, axis=0, keepdims=True)
        gy = dln * lng
        dgv = rs * (gy - jnp.mean(gy, axis=-1, keepdims=True)
                    - xhat * jnp.mean(gy * xhat, axis=-1, keepdims=True))
        dproj_ref[:, w:2 * w] = (dgv * dgv_dav).astype(BF16)
        dproj_ref[:, 3 * w:6 * w] = jnp.zeros((HEAD, 3 * w), BF16)

        tot_ref[...] = _merge_b((o1, o2, o3), (l1, l2, l3), yb_scr)
        yb = yb_scr[...]
        rb = _rms(yb)
        bz = bz_ref[...]
        silu_b, dsilu_b = _silu_and_grad(bz)
        dyb_all = dy_ref[:, w:]
        dnb = dyb_all * silu_b
        dproj_ref[:, 6 * w:] = (dyb_all * yb * rb * gb_ref[...] * dsilu_b).astype(BF16)
        dgb_ref[...] += jnp.sum(dnb * yb * rb, axis=0, keepdims=True)
        dyb = _rms_bwd(dnb, yb, rb, gb_ref[...])
        dyb_ref[...] = dyb
        prod = dyb * yb
        dl_ref[...] = _lane_pick(
            [jnp.sum(prod[:, h * HEAD:(h + 1) * HEAD], axis=-1, keepdims=True) for h in range(n_groups)], HEAD)

    seg = lambda j: pl.BlockSpec((HEAD, w), lambda i, j=j: (i, j))
    row_w = pl.BlockSpec((HEAD, w), lambda i: (i, 0))
    stat = pl.BlockSpec((HEAD, HEAD), lambda i: (i, 0))
    vec = pl.BlockSpec((1, w), lambda i: (0, 0))
    ws_spec = pl.BlockSpec((n_groups, HEAD, HEAD), lambda i: (0, 0, 0))
    bst_spec = pl.BlockSpec((HEAD, n_groups), lambda i: (0, 0))
    vec_shape = jax.ShapeDtypeStruct((1, w), F32)
    return pl.pallas_call(
        body, name="mix_bwd", grid=(s // HEAD,),
        out_shape=[jax.ShapeDtypeStruct((s, N_SEG * w), BF16), jax.ShapeDtypeStruct((s, w), F32),
                   jax.ShapeDtypeStruct((s, HEAD), F32), jax.ShapeDtypeStruct((s, HEAD), F32),
                   jax.ShapeDtypeStruct((n_groups, HEAD, HEAD), F32), jax.ShapeDtypeStruct((HEAD, n_groups), F32),
                   vec_shape, vec_shape, vec_shape, vec_shape],
        in_specs=[seg(0), seg(1), seg(2), seg(6), pl.BlockSpec((HEAD, 2 * w), lambda i: (i, 0)),
                  row_w, row_w, row_w, stat, stat, stat, ws_spec, bst_spec, vec, vec, vec, vec],
        out_specs=[pl.BlockSpec((HEAD, N_SEG * w), lambda i: (i, 0)), row_w, stat, stat,
                   ws_spec, bst_spec, vec, vec, vec, vec],
        scratch_shapes=[pltpu.VMEM((HEAD, w), F32), pltpu.VMEM((HEAD, w), BF16), pltpu.VMEM((HEAD, w), F32),
                        pltpu.VMEM((HEAD, w), BF16), pltpu.VMEM((HEAD, w), F32)],
        compiler_params=_params(("arbitrary",), 56),
    )(proj, proj, proj, proj, dy, *outs, *lses, w_s, bst, ln_g, ln_b, g_a, g_b)


def _local_step(x, p, tgt, small, wg, ex, while_last_travels=None):
    s, d = x.shape
    n, _, c_in = wg.buffers["w_in"].shape
    assert n == N_DEV
    d_in = n * c_in
    w = d_in // N_SEG
    n_heads = w // HEAD
    p_dim, c_up = wg.buffers["w_ple_up"].shape[1:]
    assert s % (HEAD * DILATIONS[-1]) == 0 and w % HEAD == 0 and d == n * c_up == 2 * w

    near, far = (2, 4), (6,)
    wg.start("gather_in_pair", ["w_in"], (1,))
    wg.start("gather_in_near", ["w_in"], near)
    token = wg.start("gather_in_far", ["w_in"], far)
    rest = ["w_out", "w_ple_gate", "w_ple_up"]

    hn = _rmsnorm_fwd(x, small["g_pre"], "pre_norm", True, dep=token)
    tm, tk = _tile(s, 1024), _tile(d, 1024)

    def in_proj(shards, name, carry, dep=None):
        return _matmul(
            hn, wg.buffers["w_in"], name=name, grid=(s // tm, len(shards), d // tk), dims=NN_DIMS,
            prefetch=jnp.stack(shards).astype(jnp.int32),
            a_spec=pl.BlockSpec((tm, tk), lambda i, j, k, sh: (i, k)),
            b_spec=pl.BlockSpec((None, tk, c_in), lambda i, j, k, sh: (sh[j], k, 0)),
            acc_shape=(tm, c_in), out_shapes=[jax.ShapeDtypeStruct((s, d_in), F32)],
            out_specs=[pl.BlockSpec((tm, c_in), lambda i, j, k, sh: (i, sh[j]))], carry=carry, dep=dep)[0]

    me = wg.me
    wg.arrived("gather_in_pair", [hn])
    proj = in_proj([me, me ^ 1], "in_proj_pair", None)
    for tag, chips in (("near", near), ("far", far)):
        token = wg.forward("gather_in_" + tag, [proj])
        if tag == "far":
            token = wg.start("gather_rest", rest, (1,) + near + far, after=[token])
        proj = in_proj([me ^ k for k in chips], "in_proj_" + tag, proj, dep=token)
        wg.forwarded("gather_in_" + tag, [proj])
        proj = in_proj([me ^ k ^ 1 for k in chips], "in_proj_%s_forwarded" % tag, proj)
    win_g = wg.buffers["w_in"]

    qn, kn = _qkv_prep(proj, small["g_q"], small["g_k"], w)
    buckets = jnp.stack([_rel_buckets(dil) for dil in DILATIONS])
    bias = _bias_build(small["rel_bias"], buckets, n_heads)
    outs, lses = [], []
    for c, dil in enumerate(DILATIONS):
        o, l = _attn_fwd(qn, kn, proj, bias[c], dil, "attn_fwd_d%d" % dil)
        outs.append(o)
        lses.append(l)
    token = wg.forward("gather_rest", [outs[-1]])

    ws_p = small["w_s"][:, CHUNK_ORDER][:, :, CHUNK_ORDER]
    bst = small["b_s"].T[CHUNK_ORDER]
    mix_args = (outs, lses, ws_p, bst, small["ln_v_g"], small["ln_v_b"], small["g_out_a"], small["g_out_b"], w)
    y = _mix_fwd(proj, *mix_args, dep=token)
    wg.forwarded("gather_rest", [y])
    wout_g, wgate_g, wup_g = (wg.buffers[k] for k in rest)
    wout_f = wout_g.reshape(2 * w, d)
    wgate_f = wgate_g.reshape(d, d)

    tn = _tile(d, 1024)
    tk2 = _tile(2 * w, 1024)

    def resid_epilogue(acc, ex, outs_):
        outs_[0][...] = ex[0][...] + acc[...]

    h = _matmul(
        y, wout_f, name="out_proj", grid=(s // tm, d // tn, (2 * w) // tk2), dims=NN_DIMS,
        a_spec=pl.BlockSpec((tm, tk2), lambda i, j, k: (i, k)),
        b_spec=pl.BlockSpec((tk2, tn), lambda i, j, k: (k, j)),
        acc_shape=(tm, tn), out_shapes=[jax.ShapeDtypeStruct((s, d), F32)],
        out_specs=[pl.BlockSpec((tm, tn), lambda i, j, k: (i, j))],
        extra=(x,), extra_specs=(pl.BlockSpec((tm, tn), lambda i, j, k: (i, j)),),
        epilogue=resid_epilogue)[0]

    hn2 = _rmsnorm_fwd(h, small["g_ple"], "ple_norm", False)

    tmg = _tile(s, 512)

    def ple_epilogue(acc, ex, outs_):
        h_ref, p_ref, wup_ref, tgt_ref = ex
        dout_ref, dpre_ref, dup_ref, loss_ref = outs_
        gate = jax.nn.sigmoid(acc[...])
        up = _dot(p_ref[...].astype(BF16), wup_ref[...])
        err = h_ref[...] + gate * up - tgt_ref[...]
        dout = err * (1.0 / d)
        dout_ref[...] = dout
        dpre_ref[...] = (dout * up * gate * (1.0 - gate)).astype(BF16)
        dup_ref[...] = (dout * gate).astype(BF16)
        part = 0.5 * jnp.sum(err * err) * (1.0 / d)
        rr = lax.broadcasted_iota(jnp.int32, (8, HEAD), 0)
        cc = lax.broadcasted_iota(jnp.int32, (8, HEAD), 1)
        loss_ref[...] = jnp.where((rr == 0) & (cc == 0), part, 0.0)

    tile_ij = pl.BlockSpec((tmg, c_up), lambda i, j, k: (i, j))
    dout, dpre, dup, loss_parts = _matmul(
        hn2, wgate_f, name="ple_gate", grid=(s // tmg, n, 1), dims=NN_DIMS,
        a_spec=pl.BlockSpec((tmg, d), lambda i, j, k: (i, 0)),
        b_spec=pl.BlockSpec((d, c_up), lambda i, j, k: (0, j)),
        acc_shape=(tmg, c_up),
        out_shapes=[jax.ShapeDtypeStruct((s, d), F32), jax.ShapeDtypeStruct((s, d), BF16),
                    jax.ShapeDtypeStruct((s, d), BF16), jax.ShapeDtypeStruct((s // tmg * 8, n * HEAD), F32)],
        out_specs=[tile_ij, tile_ij, tile_ij, pl.BlockSpec((8, HEAD), lambda i, j, k: (i, j))],
        extra=(h, p, wup_g, tgt),
        extra_specs=(tile_ij, pl.BlockSpec((tmg, p_dim), lambda i, j, k: (i, 0)),
                     pl.BlockSpec((None, p_dim, c_up), lambda i, j, k: (j, 0, 0)), tile_ij),
        epilogue=ple_epilogue)
    loss = jnp.sum(loss_parts)

    tks = _tile(s, 1024)
    g_wup = _matmul(
        p, dup, name="grad_w_up", grid=(1, n, s // tks), dims=TN_DIMS,
        a_spec=pl.BlockSpec((tks, p_dim), lambda i, j, k: (k, 0)),
        b_spec=pl.BlockSpec((tks, c_up), lambda i, j, k: (k, j)),
        acc_shape=(p_dim, c_up), out_shapes=[jax.ShapeDtypeStruct((n, p_dim, c_up), BF16)],
        out_specs=[pl.BlockSpec((None, p_dim, c_up), lambda i, j, k: (j, 0, 0))])[0]

    def tn_matmul(a, b, name):
        m_, n_ = a.shape[1], b.shape[1]
        bm, bn = _tile(m_, 1024), _tile(n_, 1024)
        return _matmul(
            a, b, name=name, grid=(m_ // bm, n_ // bn, s // tks), dims=TN_DIMS,
            a_spec=pl.BlockSpec((tks, bm), lambda i, j, k: (k, i)),
            b_spec=pl.BlockSpec((tks, bn), lambda i, j, k: (k, j)),
            acc_shape=(bm, bn), out_shapes=[jax.ShapeDtypeStruct((m_, n_), BF16)],
            out_specs=[pl.BlockSpec((bm, bn), lambda i, j, k: (i, j))])[0]

    def nt_matmul(a, b, name, out_dtype, dep=None):
        k_, n_ = a.shape[1], b.shape[0]
        bm, bn, bk = _tile(s, 1024), _tile(n_, 1024), _tile(k_, 1024)
        return _matmul(
            a, b, name=name, grid=(s // bm, n_ // bn, k_ // bk), dims=NT_DIMS,
            a_spec=pl.BlockSpec((bm, bk), lambda i, j, k: (i, k)),
            b_spec=pl.BlockSpec((bn, bk), lambda i, j, k: (j, k)),
            acc_shape=(bm, bn), out_shapes=[jax.ShapeDtypeStruct((s, n_), out_dtype)],
            out_specs=[pl.BlockSpec((bm, bn), lambda i, j, k: (i, j))], dep=dep)[0]

    by_core = lambda g: g.reshape((N_CHIP, 2) + g.shape[-2:])
    g_wgate = tn_matmul(hn2, dpre, "grad_w_gate").reshape(wgate_g.shape)
    dhn2 = nt_matmul(dpre, wgate_f, "ple_gate_bwd", F32)
    dh, dh_b, dh_bp, dg_ple = _rmsnorm_bwd(dhn2, h, small["g_ple"], dout, "ple_norm_bwd", False, True)
    g_wout = tn_matmul(y, dh_b, "grad_w_out").reshape(wout_g.shape)

    late = ("w_out", "w_ple_gate", "w_ple_up")
    late_parts = (g_wout, g_wgate, g_wup)
    token = ex.push_pairs("pair_late", [by_core(g) for g in late_parts])
    dy = nt_matmul(dh_bp, wout_f, "out_proj_bwd", F32, dep=token)
    (dproj, dyb, lse_tot, delta, dws, dbst, dlng, dlnb, dga, dgb) = _mix_bwd(proj, dy, *mix_args)
    both_columns, from_sibling = ex.pairs_done("pair_late", [dproj])
    pair_sums = [_pair_add(mine.reshape((N_DEV,) + mine.shape[-2:]), theirs, "pair_add_" + k, ex.core)
                 for k, mine, theirs in zip(late, both_columns, from_sibling)]
    token = ex.push_chips("chip_late", pair_sums)

    running, dss = None, []
    for c, dil in enumerate(DILATIONS):
        *running, ds = _attn_bwd(qn, kn, proj, dyb, lse_tot, delta, bias[c], dil, "attn_bwd_d%d" % dil,
                                 running=running, dep=token if c == 0 else None)
        dss.append(ds)
    d_rel = _bias_grad(jnp.stack(dss), buckets, n_heads)
    dproj, dgq, dgk = _qkv_bwd(dproj, proj, *running, small["g_q"], small["g_k"], w)
    pair_sums, landed, _ = ex.chips_done("chip_late", [dproj])
    delivered = {k: (mine, theirs) for k, mine, theirs in zip(late, pair_sums, landed)}

    token_row = np.argsort(CHUNK_ORDER)
    dws = dws[:, token_row][:, :, token_row]
    dbst = dbst[token_row]
    small_grads = {
        "w_s": dws, "b_s": dbst.T, "ln_v_g": dlng, "ln_v_b": dlnb, "g_q": dgq, "g_k": dgk,
        "rel_bias": d_rel, "g_out_a": dga, "g_out_b": dgb, "g_ple": dg_ple,
    }

    bm = _tile(d, 1024)

    def grad_w_in(core, name, dep=None):
        return _matmul(
            hn, dproj, name=name, grid=(d // bm, N_CHIP, s // tks), dims=TN_DIMS, prefetch=core.reshape(1),
            a_spec=pl.BlockSpec((tks, bm), lambda i, j, k, core_ref: (k, i)),
            b_spec=pl.BlockSpec((tks, c_in), lambda i, j, k, core_ref: (k, 2 * j + core_ref[0])),
            acc_shape=(bm, c_in), out_shapes=[jax.ShapeDtypeStruct((N_CHIP, d, c_in), BF16)],
            out_specs=[pl.BlockSpec((None, bm, c_in), lambda i, j, k, core_ref: (j, i, 0))], dep=dep)[0]

    for_sibling = grad_w_in(1 - ex.core, "grad_w_in_sibling")
    token = ex.push_pairs("pair_in", [for_sibling])
    mine = grad_w_in(ex.core, "grad_w_in_mine", dep=token)
    _, from_sibling = ex.pairs_done("pair_in", [mine])
    pair_sum = _pair_add(mine, from_sibling[0], "pair_add_w_in")
    token = ex.push_chips("chip_in", [pair_sum], _pack_small(small_grads, SMALL_EARLY))

    dhn = _matmul(
        dproj, win_g, name="in_proj_bwd", grid=(s // tm, d // tn, n), dims=NT_DIMS,
        a_spec=pl.BlockSpec((tm, c_in), lambda i, j, k: (i, k)),
        b_spec=pl.BlockSpec((None, tn, c_in), lambda i, j, k: (k, j, 0)),
        acc_shape=(tm, tn), out_shapes=[jax.ShapeDtypeStruct((s, d), F32)],
        out_specs=[pl.BlockSpec((tm, tn), lambda i, j, k: (i, j))], dep=token)[0]
    grad_x, dg_pre = _rmsnorm_bwd(dhn, x, small["g_pre"], dh, "pre_norm_bwd", True, False)
    extra = while_last_travels(token, delivered) if while_last_travels is not None else []
    pair_sums, landed, slabs = ex.chips_done("chip_in", [grad_x] + list(extra))
    delivered["w_in"] = (pair_sums[0], landed[0])
    small_grads["g_pre"] = dg_pre
    return loss, grad_x, small_grads, delivered, slabs, extra


SMALL_EARLY = ("w_s", "b_s", "ln_v_g", "ln_v_b", "g_q", "g_k", "rel_bias", "g_out_a", "g_out_b", "g_ple")
SMALL_LAST = ("g_pre",)
SMALL_NAMES = SMALL_LAST + SMALL_EARLY


def _pack_small(tree, names):
    parts = []
    for name in names:
        flat = tree[name].astype(F32).reshape(-1)
        pad = (-flat.shape[0]) % HEAD
        parts.append(jnp.pad(flat, (0, pad)) if pad else flat)
    slab = jnp.concatenate(parts).reshape(-1, HEAD)
    pad_rows = (-slab.shape[0]) % 8
    return jnp.pad(slab, ((0, pad_rows), (0, 0))) if pad_rows else slab


def _unpack_small(slab, like, names):
    flat = slab.reshape(-1)
    out, off = {}, 0
    for name in names:
        size = like[name].size
        out[name] = flat[off:off + size].reshape(like[name].shape)
        off += size + (-size) % HEAD
    return out


def _peer(k):
    x, y, c = (lax.axis_index(a) for a in AXES)
    bits = ((k >> 2) & 1, (k >> 1) & 1, k & 1)
    px, py, pc = (1 - v if b else v for v, b in zip((x, y, c), bits))
    return (px, py, pc), 4 * px + 2 * py + pc


def _my_index():
    x, y, c = (lax.axis_index(a) for a in AXES)
    return 4 * x + 2 * y + c


N_CHIP = 4
HBM_SPEC = pl.BlockSpec(memory_space=pl.ANY)


def _remote(src, dst, send_sem, recv_sem, peer):
    return pltpu.make_async_remote_copy(src_ref=src, dst_ref=dst, send_sem=send_sem, recv_sem=recv_sem,
                                        device_id=peer, device_id_type=pl.DeviceIdType.MESH)


SEM_SPEC = pl.BlockSpec(memory_space=pltpu.SEMAPHORE)
HBM_ONLY = pl.BlockSpec(memory_space=pltpu.HBM)
DATAFLOW = pltpu.SideEffectType.DATAFLOW_SIDE_EFFECTING


def _comm_call(name, arrays, *, wait=None, start=None, after=()):
    n, n_after = len(arrays), len(after)

    def body(*refs):
        ins = refs[:n]
        pos = n
        if wait is not None:
            for cp in wait[2](ins, refs[pos], refs[pos + 1]):
                cp.wait()
            pos += 2
        outs = refs[pos + n_after:]
        if start is not None:
            for cp in start[1](ins, outs[0], outs[1]):
                cp.start()
        outs[-1][...] = jnp.zeros_like(outs[-1])

    operands = [pltpu.with_memory_space_constraint(a, pltpu.HBM) for a in arrays]
    in_specs = [HBM_ONLY] * n
    if wait is not None:
        operands += [wait[0], wait[1]]
        in_specs += [SEM_SPEC, SEM_SPEC]
    operands += list(after)
    in_specs += [HBM_SPEC] * n_after
    out_shape, out_specs = [], []
    if start is not None:
        out_shape += [pltpu.SemaphoreType.DMA((start[0],))] * 2
        out_specs += [SEM_SPEC, SEM_SPEC]
    first = len(out_shape)
    out_shape += [pltpu.HBM(a.shape, a.dtype) for a in arrays] + [jax.ShapeDtypeStruct((SUB, HEAD), F32)]
    out_specs += [HBM_ONLY] * n + [pl.BlockSpec(memory_space=pltpu.VMEM)]
    res = pl.pallas_call(
        body, name=name, out_shape=tuple(out_shape), in_specs=tuple(in_specs), out_specs=tuple(out_specs),
        input_output_aliases={i: first + i for i in range(n)},
        compiler_params=pltpu.CompilerParams(has_side_effects=DATAFLOW),
    )(*operands)
    sems = (res[0], res[1]) if start is not None else None
    return list(res[first:first + n]), sems, res[-1]


class _GradExchange:
    def __init__(self):
        x, y, c = (lax.axis_index(a) for a in AXES)
        self.core = c.astype(jnp.int32)
        self.chip = (2 * x + y).astype(jnp.int32)
        self.pending = {}

    def _pair_copies(self, n_arr):
        def make(refs, send_sems, recv_sems):
            sibling, _ = _peer(1)
            other = 1 - lax.axis_index("c")
            srcs, lands = refs[:n_arr], refs[n_arr:]
            pick = lambda ref, ch: ref.at[ch, other] if len(ref.shape) == 4 else ref.at[ch]
            return [_remote(pick(srcs[a], ch), lands[a].at[ch], send_sems.at[a * N_CHIP + ch],
                            recv_sems.at[a * N_CHIP + ch], sibling)
                    for a in range(n_arr) for ch in range(N_CHIP)]
        return make

    def _chip_copies(self, n_arr, with_slab):
        def make(refs, send_sems, recv_sems):
            x, y = lax.axis_index("x"), lax.axis_index("y")
            my_chip = 2 * x + y
            srcs, lands = refs[:n_arr], refs[n_arr:2 * n_arr]
            copies = []
            for j, k in enumerate((2, 4, 6)):
                peer, peer_idx = _peer(k)
                for a in range(n_arr):
                    copies.append(_remote(srcs[a].at[peer_idx // 2], lands[a].at[my_chip],
                                          send_sems.at[3 * a + j], recv_sems.at[3 * a + j], peer))
            if with_slab:
                slab, slab_land = refs[2 * n_arr], refs[2 * n_arr + 1]
                for k in range(1, N_DEV):
                    peer, _ = _peer(k)
                    copies.append(_remote(slab, slab_land.at[_my_index()], send_sems.at[3 * n_arr + k - 1],
                                          recv_sems.at[3 * n_arr + k - 1], peer))
            return copies
        return make

    def push_pairs(self, tag, for_sibling):
        n_arr = len(for_sibling)
        lands = [lax.empty((N_CHIP,) + a.shape[-2:], a.dtype) for a in for_sibling]
        make = self._pair_copies(n_arr)
        arrays, sems, token = _comm_call(tag + "_start", list(for_sibling) + lands, start=(n_arr * N_CHIP, make))
        self.pending[tag] = (arrays, sems, make, n_arr)
        return token

    def pairs_done(self, tag, after):
        arrays, sems, make, n_arr = self.pending.pop(tag)
        arrays, _, _ = _comm_call(tag + "_wait", arrays, wait=(sems[0], sems[1], make), after=after)
        return arrays[:n_arr], arrays[n_arr:]

    def push_chips(self, tag, pair_sums, slab=None):
        n_arr = len(pair_sums)
        arrays = list(pair_sums) + [lax.empty(a.shape, a.dtype) for a in pair_sums]
        n_copies = 3 * n_arr
        if slab is not None:
            arrays += [slab, lax.empty((N_DEV,) + slab.shape, slab.dtype)]
            n_copies += N_DEV - 1
        make = self._chip_copies(n_arr, slab is not None)
        arrays, sems, token = _comm_call(tag + "_start", arrays, start=(n_copies, make))
        self.pending[tag] = (arrays, sems, make, n_arr)
        return token

    def chips_done(self, tag, after):
        arrays, sems, make, n_arr = self.pending.pop(tag)
        arrays, _, _ = _comm_call(tag + "_wait", arrays, wait=(sems[0], sems[1], make), after=after)
        return arrays[:n_arr], arrays[n_arr:2 * n_arr], arrays[2 * n_arr:]


def _cast_place(w, name):
    r, c = w.shape
    tr = r if r * c <= MIB else 1 << ((MIB // c).bit_length() - 1)
    assert r % tr == 0

    def body(me_ref, w_ref, o_ref):
        o_ref[...] = w_ref[...].astype(BF16)

    return pl.pallas_call(
        body, name=name, out_shape=jax.ShapeDtypeStruct((N_DEV, r, c), BF16),
        grid_spec=pltpu.PrefetchScalarGridSpec(
            num_scalar_prefetch=1, grid=(r // tr,),
            in_specs=[pl.BlockSpec((tr, c), lambda i, me_ref: (i, 0))],
            out_specs=pl.BlockSpec((None, tr, c), lambda i, me_ref: (me_ref[0], i, 0))),
        compiler_params=_params(("arbitrary",), 40),
    )(_my_index().astype(jnp.int32).reshape(1), w)


class _WeightGather:
    CHIPS = (2, 4, 6)

    def __init__(self, buffers):
        self.buffers = dict(buffers)
        self.pending = {}
        self.me = _my_index().astype(jnp.int32)

    def _own_slot_to(self, peers):
        def make(refs, send_sems, recv_sems):
            me = _my_index()
            return [_remote(ref.at[me], ref.at[me], send_sems.at[len(peers) * a + j],
                            recv_sems.at[len(peers) * a + j], _peer(k)[0])
                    for a, ref in enumerate(refs) for j, k in enumerate(peers)]
        return make

    def _forward_from(self, chips):
        def make(refs, send_sems, recv_sems):
            sibling, _ = _peer(1)
            copies = []
            for a, ref in enumerate(refs):
                for j, k in enumerate(chips):
                    slot = ref.at[_peer(k)[1]]
                    copies.append(_remote(slot, slot, send_sems.at[len(chips) * a + j],
                                          recv_sems.at[len(chips) * a + j], sibling))
            return copies
        return make

    def _run(self, call, names, **kw):
        arrays, sems, token = _comm_call(call, [self.buffers[k] for k in names], **kw)
        self.buffers.update(zip(names, arrays))
        return sems, token

    def start(self, tag, names, peers, after=()):
        make = self._own_slot_to(peers)
        sems, token = self._run(tag + "_start", names, start=(len(names) * len(peers), make), after=after)
        self.pending[tag] = (names, sems, make, peers)
        return token

    def arrived(self, tag, after):
        names, sems, make, _ = self.pending.pop(tag)
        self._run(tag + "_wait", names, wait=(sems[0], sems[1], make), after=after)

    def forward(self, tag, after):
        names, sems, make, peers = self.pending.pop(tag)
        chips = tuple(k for k in peers if k != 1)
        onward = self._forward_from(chips)
        new_sems, token = self._run(tag + "_forward", names, wait=(sems[0], sems[1], make),
                                    start=(len(chips) * len(names), onward), after=after)
        self.pending[tag + "/fwd"] = (names, new_sems, onward)
        return token

    def forwarded(self, tag, after):
        names, sems, make = self.pending.pop(tag + "/fwd")
        self._run(tag + "_done", names, wait=(sems[0], sems[1], make), after=after)


def _pair_add(mine, theirs, name, core=None):
    _, r, c_dim = theirs.shape
    tr = r if r * c_dim <= MIB else 1 << ((MIB // c_dim).bit_length() - 1)
    assert r % tr == 0
    stride = 1 if core is None else 2
    offset = jnp.zeros((1,), jnp.int32) if core is None else core.reshape(1)

    def body(off_ref, a_ref, b_ref, o_ref):
        o_ref[...] = (a_ref[...].astype(F32) + b_ref[...].astype(F32)).astype(BF16)

    blk = (None, tr, c_dim)
    return pl.pallas_call(
        body, name=name, out_shape=jax.ShapeDtypeStruct(theirs.shape, BF16),
        grid_spec=pltpu.PrefetchScalarGridSpec(
            num_scalar_prefetch=1, grid=(N_CHIP, r // tr),
            in_specs=[pl.BlockSpec(blk, lambda ch, i, off_ref: (stride * ch + off_ref[0], i, 0)),
                      pl.BlockSpec(blk, lambda ch, i, off_ref: (ch, i, 0))],
            out_specs=pl.BlockSpec(blk, lambda ch, i, off_ref: (ch, i, 0))),
        compiler_params=_params(("arbitrary", "arbitrary"), 40),
    )(offset, mine, theirs)


def _slab_exchange(slab, name):
    def body(slab_in, slab_out, send_sems, recv_sems, local_sem):
        me = _my_index()
        local = pltpu.make_async_copy(slab_in, slab_out.at[me], local_sem)
        local.start()
        sends = []
        for k in range(1, N_DEV):
            peer, _ = _peer(k)
            sends.append(_remote(slab_in, slab_out.at[me], send_sems.at[k - 1], recv_sems.at[k - 1], peer))
        for cp in sends:
            cp.start()
        for k in range(1, N_DEV):
            peer, peer_idx = _peer(k)
            slot = slab_out.at[peer_idx]
            _remote(slot, slot, send_sems.at[k - 1], recv_sems.at[k - 1], peer).wait_recv()
        for cp in sends:
            cp.wait_send()
        local.wait()

    return pl.pallas_call(
        body, name=name, out_shape=jax.ShapeDtypeStruct((N_DEV,) + slab.shape, slab.dtype),
        in_specs=[HBM_SPEC], out_specs=HBM_SPEC,
        scratch_shapes=[pltpu.SemaphoreType.DMA((N_DEV - 1,)), pltpu.SemaphoreType.DMA((N_DEV - 1,)),
                        pltpu.SemaphoreType.DMA],
        compiler_params=pltpu.CompilerParams(has_side_effects=True),
    )(slab)


def _adamw_math(w, g, m, v):
    m = ADAM_B1 * m + (1.0 - ADAM_B1) * g
    v = ADAM_B2 * v + (1.0 - ADAM_B2) * (g * g)
    m_hat = m / (1.0 - ADAM_B1 ** ADAM_STEP)
    v_hat = v / (1.0 - ADAM_B2 ** ADAM_STEP)
    delta = -ADAM_LR * (m_hat / (jnp.sqrt(v_hat) + ADAM_EPS) + ADAM_WD * w)
    return delta, m, v


def _adamw(parts, own, place, w, m, v, name, dep=None):
    n_parts = parts.shape[0]
    r, c = w.shape
    budget = 280 * 1024
    tr = r if r * c <= budget else 1 << ((budget // c).bit_length() - 1)
    assert r % tr == 0

    def body(place_ref, p_ref, own_ref, w_ref, m_ref, v_ref, g_ref, d_ref, nm_ref, nv_ref):
        mine = own_ref[...].astype(F32)
        g = None
        for i in range(n_parts):
            term = jnp.where(place_ref[0] == i, mine, p_ref[i].astype(F32))
            g = term if g is None else g + term
        delta, nm, nv = _adamw_math(w_ref[...], g, m_ref[...], v_ref[...])
        g_ref[...] = g
        d_ref[...] = delta
        nm_ref[...] = nm
        nv_ref[...] = nv

    blk = pl.BlockSpec((tr, c), lambda i, place_ref: (i, 0))
    shape = jax.ShapeDtypeStruct((r, c), F32)
    in_specs = [pl.BlockSpec((n_parts, tr, c), lambda i, place_ref: (0, i, 0)),
                pl.BlockSpec((None, tr, c), lambda i, place_ref: (place_ref[1], i, 0)), blk, blk, blk]
    operands = [parts, own, w, m, v]
    if dep is not None:
        body = _drop_arg(body, 1 + len(operands))
        in_specs.append(pl.BlockSpec((SUB, HEAD), lambda i, place_ref: (0, 0)))
        operands.append(dep)
    return pl.pallas_call(
        body, name=name, out_shape=[shape] * 4,
        grid_spec=pltpu.PrefetchScalarGridSpec(
            num_scalar_prefetch=1, grid=(r // tr,), in_specs=in_specs, out_specs=[blk] * 4),
        compiler_params=_params(("arbitrary",), 48),
    )(place, *operands)


def kernel(x, p, g_pre, w_in, w_s, b_s, ln_v_g, ln_v_b, g_q, g_k, rel_bias, g_out_a, g_out_b, w_out, g_ple, w_ple_gate, w_ple_up, loss_target, m_g_pre, m_w_in, m_w_s, m_b_s, m_ln_v_g, m_ln_v_b, m_g_q, m_g_k, m_rel_bias, m_g_out_a, m_g_out_b, m_w_out, m_g_ple, m_w_ple_gate, m_w_ple_up, v_g_pre, v_w_in, v_w_s, v_b_s, v_ln_v_g, v_ln_v_b, v_g_q, v_g_k, v_rel_bias, v_g_out_a, v_g_out_b, v_w_out, v_g_ple, v_w_ple_gate, v_w_ple_up):
    args = dict(locals())
    small = {"g_pre": g_pre, "w_s": w_s[0], "b_s": b_s[0], "ln_v_g": ln_v_g, "ln_v_b": ln_v_b, "g_q": g_q,
             "g_k": g_k, "rel_bias": rel_bias, "g_out_a": g_out_a, "g_out_b": g_out_b, "g_ple": g_ple}
    big_names = ("w_in", "w_out", "w_ple_gate", "w_ple_up")
    big = {k: args[k][0] for k in big_names}

    wg = _WeightGather({k: _cast_place(big[k], "place_" + k) for k in big_names})
    ex = _GradExchange()
    results = {}

    def big_adamw(k, delivered, dep=None):
        mine, theirs = delivered[k]
        place = jnp.stack([ex.chip, ex.chip])
        return _adamw(theirs, mine, place, big[k], args["m_" + k][0], args["v_" + k][0], "adamw_" + k, dep=dep)

    def while_last_travels(token, delivered):
        done = []
        for k in big_names[1:]:
            results[k] = big_adamw(k, delivered, dep=token)
            done.append(results[k][0])
        return done

    loss, grad_x, small_parts, delivered, slabs, _ = _local_step(
        x[0], p[0, 0], loss_target[0], small, wg, ex, while_last_travels)
    results["w_in"] = big_adamw("w_in", delivered)
    for k in big_names:
        results[k] = [t[None] for t in results[k]]

    squeeze = lambda t: {k: (t[k][0] if k in ("w_s", "b_s") else t[k]) for k in SMALL_NAMES}
    small_m = squeeze({k: args["m_" + k] for k in SMALL_NAMES})
    small_v = squeeze({k: args["v_" + k] for k in SMALL_NAMES})
    me = _my_index().astype(jnp.int32)
    place = jnp.stack([me, jnp.zeros((), jnp.int32)])
    last_slab = _pack_small(small_parts, SMALL_LAST)
    groups = ((SMALL_EARLY, slabs[1], slabs[0], "adamw_small"),
              (SMALL_LAST, _slab_exchange(last_slab, "last_exchange"), last_slab, "adamw_last"))
    for names_, parts, own, call_name in groups:
        packed = _adamw(parts, own[None], place, _pack_small(small, names_), _pack_small(small_m, names_),
                        _pack_small(small_v, names_), call_name)
        for idx in range(4):
            tree = _unpack_small(packed[idx], small, names_)
            for k in names_:
                results.setdefault(k, [None] * 4)[idx] = tree[k].reshape(args[k].shape)

    names = ("g_pre", "w_in", "w_s", "b_s", "ln_v_g", "ln_v_b", "g_q", "g_k", "rel_bias", "g_out_a", "g_out_b",
             "w_out", "g_ple", "w_ple_gate", "w_ple_up")
    total = lax.psum(loss, AXES)
    out = [total, grad_x[None]]
    for idx in range(4):
        out += [results[k][idx] for k in names]
    return tuple(out)
```

```python
import functools
import math

import numpy as np
import jax
import jax.numpy as jnp
from jax import lax
from jax.experimental import pallas as pl
from jax.experimental.pallas import tpu as pltpu

F32 = jnp.float32
BF16 = jnp.bfloat16
EPS = 1e-6
NEG_INF = -1e30
HEAD = 128
DILATIONS = (1, 4, 16)
NUM_BUCKETS = 32
MAX_DISTANCE = 2048
N_SEG = 7
ADAM_LR = 0.001
ADAM_B1 = 0.9
ADAM_B2 = 0.999
ADAM_EPS = 1e-08
ADAM_WD = 0.01
ADAM_STEP = 10
AXES = ("x", "y", "c")
N_DEV = 8
MIB = 1 << 20

SUB = 8

CHUNK_ORDER = np.array([16 * (r % SUB) + r // SUB for r in range(HEAD)])
BLOCK_ORDER = {
    1: CHUNK_ORDER,
    4: np.array([32 * (r // 32) + 4 * (r % SUB) + (r // SUB) % 4 for r in range(HEAD)]),
    16: np.arange(HEAD),
}

NT_DIMS = (((1,), (1,)), ((), ()))
TN_DIMS = (((0,), (0,)), ((), ()))
NN_DIMS = (((1,), (0,)), ((), ()))


def _params(semantics, vmem_mib):
    return pltpu.CompilerParams(dimension_semantics=semantics, vmem_limit_bytes=vmem_mib * MIB)


def _gelu(a):
    return 0.5 * a * (1.0 + lax.erf(a * (2.0 ** -0.5)))


def _gelu_and_grad(a):
    cdf = 0.5 * (1.0 + lax.erf(a * (2.0 ** -0.5)))
    return a * cdf, cdf + a * jnp.exp(-0.5 * a * a) * ((2.0 * math.pi) ** -0.5)


def _silu_and_grad(a):
    s = jax.nn.sigmoid(a)
    return a * s, s * (1.0 + a * (1.0 - s))


def _rms(v):
    return lax.rsqrt(jnp.mean(v * v, axis=-1, keepdims=True) + EPS)


def _rms_bwd(dy, v, r, g):
    gy = dy * g
    return r * gy - v * (r * r * r) * jnp.mean(gy * v, axis=-1, keepdims=True)


def _dot(a, b, dims=NN_DIMS):
    return lax.dot_general(a, b, dims, preferred_element_type=F32)


def _lane_pick(cols, width):
    rows = cols[0].shape[0]
    lane = lax.broadcasted_iota(jnp.int32, (rows, width), 1)
    out = jnp.zeros((rows, width), F32)
    for h, col in enumerate(cols):
        out = jnp.where(lane == h, col, out)
    return out


def _chunk_perm():
    return jnp.asarray(np.eye(HEAD, dtype=np.float32)[CHUNK_ORDER], BF16)


def _unpermute_f32(p, v):
    hi = v.astype(BF16)
    rest = v - hi.astype(F32)
    mid = rest.astype(BF16)
    lo = (rest - mid.astype(F32)).astype(BF16)
    return _dot(p, hi, TN_DIMS) + _dot(p, mid, TN_DIMS) + _dot(p, lo, TN_DIMS)


def _rmsnorm_fwd(x, g, name, permute, dep=None):
    s, d = x.shape
    tm = HEAD

    def body(x_ref, g_ref, p_ref, o_ref):
        v = x_ref[...]
        out = (v * _rms(v) * g_ref[...]).astype(BF16)
        if permute:
            out = _dot(p_ref[...], out).astype(BF16)
        o_ref[...] = out

    in_specs = [pl.BlockSpec((tm, d), lambda i: (i, 0)), pl.BlockSpec((1, d), lambda i: (0, 0)),
                pl.BlockSpec((HEAD, HEAD), lambda i: (0, 0))]
    operands = [x, g, _chunk_perm()]
    if dep is not None:
        body = _drop_arg(body, len(operands))
        in_specs.append(DEP_SPEC)
        operands.append(dep)
    return pl.pallas_call(
        body, name=name, grid=(s // tm,),
        out_shape=jax.ShapeDtypeStruct((s, d), BF16), in_specs=in_specs,
        out_specs=pl.BlockSpec((tm, d), lambda i: (i, 0)),
        compiler_params=_params(("arbitrary",), 40),
    )(*operands)


def _rmsnorm_bwd(dy, v, g, res, name, dy_permuted, with_bf16):
    s, d = v.shape
    tm = HEAD
    perm = _chunk_perm()

    def body(dy_ref, v_ref, g_ref, res_ref, p_ref, *outs):
        dx_ref, dg_ref = outs[0], outs[-1]
        i = pl.program_id(0)
        vv, dyv = v_ref[...], dy_ref[...]
        if dy_permuted:
            dyv = _unpermute_f32(p_ref[...], dyv)
        r = _rms(vv)
        dx = res_ref[...] + _rms_bwd(dyv, vv, r, g_ref[...])
        dx_ref[...] = dx
        if with_bf16:
            dxb = dx.astype(BF16)
            outs[1][...] = dxb
            outs[2][...] = _dot(p_ref[...], dxb).astype(BF16)

        @pl.when(i == 0)
        def _():
            dg_ref[...] = jnp.zeros_like(dg_ref)

        dg_ref[...] += jnp.sum(dyv * vv * r, axis=0, keepdims=True)

    row = pl.BlockSpec((tm, d), lambda i: (i, 0))
    vec = pl.BlockSpec((1, d), lambda i: (0, 0))
    shapes = [jax.ShapeDtypeStruct((s, d), F32)]
    specs = [row]
    if with_bf16:
        shapes += [jax.ShapeDtypeStruct((s, d), BF16)] * 2
        specs += [row, row]
    shapes.append(jax.ShapeDtypeStruct((1, d), F32))
    specs.append(vec)
    return pl.pallas_call(
        body, name=name, grid=(s // tm,), out_shape=shapes,
        in_specs=[row, row, vec, row, pl.BlockSpec((HEAD, HEAD), lambda i: (0, 0))], out_specs=specs,
        compiler_params=_params(("arbitrary",), 40),
    )(dy, v, g, res, perm)


DEP_SPEC = pl.BlockSpec((SUB, HEAD), lambda *_: (0, 0))


def _drop_arg(body, pos):
    return lambda *refs: body(*refs[:pos], *refs[pos + 1:])


def _matmul(a, b, *, name, grid, a_spec, b_spec, dims, acc_shape, out_shapes, out_specs,
            extra=(), extra_specs=(), epilogue=None, vmem_mib=48, dep=None, prefetch=None, carry=None):
    nk = grid[2]
    n_user = len(extra)
    for unread, spec in ((dep, DEP_SPEC), (carry, HBM_SPEC)):
        if unread is not None:
            extra, extra_specs = tuple(extra) + (unread,), tuple(extra_specs) + (spec,)
    n_extra, n_out = len(extra), len(out_shapes)
    n_pre = 0 if prefetch is None else 1
    aliases = {} if carry is None else {n_pre + 2 + n_extra - 1: 0}

    def body(*refs):
        refs = refs[n_pre:]
        a_ref, b_ref = refs[0], refs[1]
        ex = refs[2:2 + n_user]
        outs = refs[2 + n_extra:2 + n_extra + n_out]
        acc = refs[-1]
        k = pl.program_id(2)

        @pl.when(k == 0)
        def _():
            acc[...] = jnp.zeros_like(acc)

        av = a_ref[...]
        if av.dtype != BF16:
            av = av.astype(BF16)
        acc[...] += _dot(av, b_ref[...], dims)

        @pl.when(k == nk - 1)
        def _():
            if epilogue is None:
                outs[0][...] = acc[...].astype(outs[0].dtype)
            else:
                epilogue(acc, ex, outs)

    scratch = [pltpu.VMEM(acc_shape, F32)]
    params = _params(("parallel", "parallel", "arbitrary"), vmem_mib)
    if prefetch is None:
        return pl.pallas_call(
            body, name=name, grid=grid, out_shape=list(out_shapes),
            in_specs=[a_spec, b_spec, *extra_specs], out_specs=list(out_specs),
            scratch_shapes=scratch, compiler_params=params, input_output_aliases=aliases,
        )(a, b, *extra)
    return pl.pallas_call(
        body, name=name, out_shape=list(out_shapes),
        grid_spec=pltpu.PrefetchScalarGridSpec(
            num_scalar_prefetch=1, grid=grid, in_specs=[a_spec, b_spec, *extra_specs],
            out_specs=list(out_specs), scratch_shapes=scratch),
        compiler_params=params, input_output_aliases=aliases,
    )(prefetch, a, b, *extra)


def _tile(n, want):
    t = min(n, want)
    while n % t:
        t //= 2
    return t


def _rel_buckets(dil):
    order = BLOCK_ORDER[dil]
    qi = jnp.asarray(HEAD + order)
    kj = jnp.asarray(np.concatenate([order, HEAD + order]))
    delta = qi[:, None] - kj[None, :]
    band = (delta >= 0) & (delta <= HEAD)
    dist = jnp.clip(delta, 0, None) * dil
    max_exact = NUM_BUCKETS // 2
    dd = jnp.maximum(dist, 1).astype(F32)
    large = max_exact + (jnp.log(dd / max_exact) / math.log(MAX_DISTANCE / max_exact)
                         * (NUM_BUCKETS - max_exact)).astype(jnp.int32)
    large = jnp.minimum(large, NUM_BUCKETS - 1)
    bucket = jnp.where(dist < max_exact, dist, large)
    return jnp.where(band, bucket, -1).astype(jnp.int32)


def _bias_build(rel_bias, buckets, n_heads):
    nd = buckets.shape[0]

    def body(rb_ref, bk_ref, o_ref):
        for c in range(nd):
            def per_head(h, carry, c=c):
                bk = bk_ref[c]
                acc = jnp.where(bk < 0, NEG_INF, 0.0).astype(F32)
                for b in range(NUM_BUCKETS):
                    acc = jnp.where(bk == b, rb_ref[b, h], acc)
                o_ref[c, h] = acc
                return carry

            lax.fori_loop(0, n_heads, per_head, 0)

    return pl.pallas_call(
        body, name="bias_build",
        out_shape=jax.ShapeDtypeStruct((nd, n_heads, HEAD, 2 * HEAD), F32),
        in_specs=[pl.BlockSpec(memory_space=pltpu.SMEM), pl.BlockSpec(memory_space=pltpu.VMEM)],
        out_specs=pl.BlockSpec(memory_space=pltpu.VMEM),
    )(rel_bias, buckets)


def _bias_grad(ds_all, buckets, n_heads):
    nd = buckets.shape[0]
    pairs = HEAD * 2 * HEAD

    def body(ds_ref, bk_ref, o_ref):
        rows = lax.broadcasted_iota(jnp.int32, (NUM_BUCKETS, pairs), 0)
        tot = jnp.zeros((n_heads, NUM_BUCKETS), F32)
        for c in range(nd):
            onehot = (rows == bk_ref[c]).astype(BF16)
            ds = ds_ref[c]
            hi = ds.astype(BF16)
            lo = (ds - hi.astype(F32)).astype(BF16)
            tot = tot + _dot(hi, onehot, NT_DIMS) + _dot(lo, onehot, NT_DIMS)
        o_ref[...] = tot

    out = pl.pallas_call(
        body, name="bias_grad",
        out_shape=jax.ShapeDtypeStruct((n_heads, NUM_BUCKETS), F32),
        in_specs=[pl.BlockSpec(memory_space=pltpu.VMEM), pl.BlockSpec(memory_space=pltpu.VMEM)],
        out_specs=pl.BlockSpec(memory_space=pltpu.VMEM),
        compiler_params=pltpu.CompilerParams(vmem_limit_bytes=40 * MIB),
    )(ds_all.reshape(nd, n_heads, pairs), buckets.reshape(nd, 1, pairs))
    return out.T


def _qkv_prep(proj, g_q, g_k, w, dep=None):
    s = proj.shape[0]
    n_heads = w // HEAD
    tm = HEAD

    def body(q_ref, k_ref, gq_ref, gk_ref, qn_ref, kn_ref):
        gq = gq_ref[...] * (HEAD ** -0.5)
        gk = gk_ref[...]
        for h in range(n_heads):
            sl = slice(h * HEAD, (h + 1) * HEAD)
            q = q_ref[:, sl]
            k = k_ref[:, sl]
            qn_ref[:, sl] = q * _rms(q) * gq
            kn_ref[:, sl] = k * _rms(k) * gk

    seg = lambda j: pl.BlockSpec((tm, w), lambda i, j=j: (i, j))
    vec = pl.BlockSpec((1, HEAD), lambda i: (0, 0))
    out = pl.BlockSpec((tm, w), lambda i: (i, 0))
    in_specs = [seg(3), seg(4), vec, vec]
    operands = [proj, proj, g_q, g_k]
    if dep is not None:
        body = _drop_arg(body, len(operands))
        in_specs.append(DEP_SPEC)
        operands.append(dep)
    return pl.pallas_call(
        body, name="qkv_prep", grid=(s // tm,),
        out_shape=[jax.ShapeDtypeStruct((s, w), F32)] * 2,
        in_specs=in_specs, out_specs=[out, out],
        compiler_params=_params(("arbitrary",), 40),
    )(*operands)


class _BlockView:
    def __init__(self, s, dil):
        assert s % (HEAD * dil) == 0 and dil in BLOCK_ORDER
        self.nb = s // (HEAD * dil)
        if dil == 1:
            self.lead, self.block = (s,), (HEAD,)
            self.index = lambda r, n: (n,)
        elif dil == 4:
            self.lead, self.block = (s // 512, 4, 4, 4, SUB), (None, 4, 4, None, SUB)
            self.index = lambda r, n: (n, 0, 0, r, 0)
        else:
            self.lead, self.block = (s // 2048, 16, 16, SUB), (None, 16, None, SUB)
            self.index = lambda r, n: (n, 0, r, 0)

    def view(self, t):
        return t.reshape(self.lead + (t.shape[-1],))

    def spec(self, width, block_of, column=0):
        return pl.BlockSpec(self.block + (width,), lambda r, n: self.index(r, block_of(r, n)) + (column,))


def _rows(ref, lanes=slice(None)):
    v = ref[(slice(None),) * (len(ref.shape) - 1) + (lanes,)]
    return v.reshape(HEAD, v.shape[-1])


def _set_rows(ref, lanes, value):
    ref[(slice(None),) * (len(ref.shape) - 1) + (lanes,)] = value.reshape(ref.shape[:-1] + (value.shape[-1],))


V_SEGMENT = 5


def _attn_fwd(qn, kn, proj, bias, dil, name, dep=None):
    s, w = qn.shape
    n_heads = w // HEAD
    bv = _BlockView(s, dil)

    def body(q_ref, kc_ref, kp_ref, vc_ref, vp_ref, bias_ref, o_ref, lse_ref, s_scr, e_scr, lse_scr, inv_scr):
        n = pl.program_id(1)
        heads = [slice(h * HEAD, (h + 1) * HEAD) for h in range(n_heads)]
        lse_scr[...] = jnp.zeros_like(lse_scr)
        for h, sl in enumerate(heads):
            q = _rows(q_ref, sl).astype(BF16)
            s_p = _dot(q, _rows(kp_ref, sl).astype(BF16), NT_DIMS) + bias_ref[h, :, :HEAD]
            s_scr[h, :, :HEAD] = jnp.where(n > 0, s_p, NEG_INF)
            s_scr[h, :, HEAD:] = _dot(q, _rows(kc_ref, sl).astype(BF16), NT_DIMS) + bias_ref[h, :, HEAD:]
        for h in range(n_heads):
            sc = s_scr[h]
            m = jnp.max(sc, axis=-1, keepdims=True)
            e = jnp.exp(sc - m)
            den = jnp.sum(e, axis=-1, keepdims=True)
            e_scr[h] = e.astype(BF16)
            lse_scr[:, h:h + 1] = m + jnp.log(den)
            inv_scr[:, h:h + 1] = 1.0 / den
        for h, sl in enumerate(heads):
            o = (_dot(e_scr[h, :, :HEAD], _rows(vp_ref, sl).astype(BF16))
                 + _dot(e_scr[h, :, HEAD:], _rows(vc_ref, sl).astype(BF16)))
            _set_rows(o_ref, sl, o * inv_scr[:, h:h + 1])
        _set_rows(lse_ref, slice(None), lse_scr[...])

    cur = bv.spec(w, lambda r, n: n)
    prev = bv.spec(w, lambda r, n: jnp.maximum(n - 1, 0))
    v_cur = bv.spec(w, lambda r, n: n, V_SEGMENT)
    v_prev = bv.spec(w, lambda r, n: jnp.maximum(n - 1, 0), V_SEGMENT)
    in_specs = [cur, cur, prev, v_cur, v_prev, pl.BlockSpec((n_heads, HEAD, 2 * HEAD), lambda r, n: (0, 0, 0))]
    operands = [bv.view(qn), bv.view(kn), bv.view(kn), bv.view(proj), bv.view(proj), bias]
    if dep is not None:
        body = _drop_arg(body, len(operands))
        in_specs.append(DEP_SPEC)
        operands.append(dep)
    o, lse = pl.pallas_call(
        body, name=name, grid=(dil, bv.nb),
        out_shape=[jax.ShapeDtypeStruct(bv.lead + (w,), F32), jax.ShapeDtypeStruct(bv.lead + (HEAD,), F32)],
        in_specs=in_specs,
        out_specs=[cur, bv.spec(HEAD, lambda r, n: n)],
        scratch_shapes=[pltpu.VMEM((n_heads, HEAD, 2 * HEAD), F32), pltpu.VMEM((n_heads, HEAD, 2 * HEAD), BF16),
                        pltpu.VMEM((HEAD, HEAD), F32), pltpu.VMEM((HEAD, HEAD), F32)],
        compiler_params=_params(("arbitrary", "arbitrary"), 48),
    )(*operands)
    return o.reshape(s, w), lse.reshape(s, HEAD)


def _attn_bwd(qn, kn, proj, dyb, lse, delta, bias, dil, name, running=None, dep=None):
    s, w = qn.shape
    n_heads = w // HEAD
    bv = _BlockView(s, dil)
    nb = bv.nb

    n_run = 0 if running is None else 3

    def body(q_ref, kc_ref, kp_ref, vc_ref, vp_ref, dy_ref, lse_ref, dl_ref, bias_ref, *rest):
        so_far = rest[:n_run]
        dq_ref, dk_ref, dv_ref, ds_ref, carry_k, carry_v, s_scr, dp_scr, p_scr, dsb_scr = rest[n_run:]
        base = (lambda i, sl: _rows(so_far[i], sl)) if n_run else (lambda i, sl: 0.0)
        r = pl.program_id(0)
        step = pl.program_id(1)
        blk = nb - 1 - step

        @pl.when((r == 0) & (step == 0))
        def _():
            ds_ref[...] = jnp.zeros_like(ds_ref)

        @pl.when(step == 0)
        def _():
            carry_k[...] = jnp.zeros_like(carry_k)
            carry_v[...] = jnp.zeros_like(carry_v)

        heads = [slice(h * HEAD, (h + 1) * HEAD) for h in range(n_heads)]
        tots = _rows(lse_ref)
        dls = _rows(dl_ref)
        for h, sl in enumerate(heads):
            q, dy = _rows(q_ref, sl).astype(BF16), _rows(dy_ref, sl).astype(BF16)
            kp, kc = _rows(kp_ref, sl).astype(BF16), _rows(kc_ref, sl).astype(BF16)
            vp, vc = _rows(vp_ref, sl).astype(BF16), _rows(vc_ref, sl).astype(BF16)
            s_p = _dot(q, kp, NT_DIMS) + bias_ref[h, :, :HEAD]
            s_scr[h, :, :HEAD] = jnp.where(blk > 0, s_p, NEG_INF)
            s_scr[h, :, HEAD:] = _dot(q, kc, NT_DIMS) + bias_ref[h, :, HEAD:]
            dp_scr[h, :, :HEAD] = _dot(dy, vp, NT_DIMS)
            dp_scr[h, :, HEAD:] = _dot(dy, vc, NT_DIMS)
        for h in range(n_heads):
            prob = jnp.exp(s_scr[h] - tots[:, h:h + 1])
            ds = prob * (dp_scr[h] - dls[:, h:h + 1])
            ds_ref[h] += ds
            p_scr[h] = prob.astype(BF16)
            dsb_scr[h] = ds.astype(BF16)
        for h, sl in enumerate(heads):
            q, dy = _rows(q_ref, sl).astype(BF16), _rows(dy_ref, sl).astype(BF16)
            kp, kc = _rows(kp_ref, sl).astype(BF16), _rows(kc_ref, sl).astype(BF16)
            ds_pb, ds_cb = dsb_scr[h, :, :HEAD], dsb_scr[h, :, HEAD:]
            _set_rows(dq_ref, sl, _dot(ds_pb, kp) + _dot(ds_cb, kc) + base(0, sl))
            _set_rows(dk_ref, sl, _dot(ds_cb, q, TN_DIMS) + carry_k[:, sl] + base(1, sl))
            carry_k[:, sl] = _dot(ds_pb, q, TN_DIMS)
            _set_rows(dv_ref, sl, _dot(p_scr[h, :, HEAD:], dy, TN_DIMS) + carry_v[:, sl] + base(2, sl))
            carry_v[:, sl] = _dot(p_scr[h, :, :HEAD], dy, TN_DIMS)

    cur = bv.spec(w, lambda r, n: nb - 1 - n)
    prev = bv.spec(w, lambda r, n: jnp.maximum(nb - 2 - n, 0))
    stat = bv.spec(HEAD, lambda r, n: nb - 1 - n)
    whole = pl.BlockSpec((n_heads, HEAD, 2 * HEAD), lambda r, n: (0, 0, 0))
    big = jax.ShapeDtypeStruct(bv.lead + (w,), F32)
    v_cur = bv.spec(w, lambda r, n: nb - 1 - n, V_SEGMENT)
    v_prev = bv.spec(w, lambda r, n: jnp.maximum(nb - 2 - n, 0), V_SEGMENT)
    in_specs = [cur, cur, prev, v_cur, v_prev, cur, stat, stat, whole]
    operands = [bv.view(qn), bv.view(kn), bv.view(kn), bv.view(proj), bv.view(proj), bv.view(dyb), bv.view(lse),
                bv.view(delta), bias]
    aliases = {}
    if running is not None:
        aliases = {len(operands) + i: i for i in range(3)}
        in_specs += [cur] * 3
        operands += [bv.view(t) for t in running]
    if dep is not None:
        body = _drop_arg(body, len(operands))
        in_specs.append(DEP_SPEC)
        operands.append(dep)
    dq, dk, dv, ds = pl.pallas_call(
        body, name=name, grid=(dil, nb),
        out_shape=[big, big, big, jax.ShapeDtypeStruct((n_heads, HEAD, 2 * HEAD), F32)],
        in_specs=in_specs, out_specs=[cur, cur, cur, whole], input_output_aliases=aliases,
        scratch_shapes=[pltpu.VMEM((HEAD, w), F32), pltpu.VMEM((HEAD, w), F32),
                        pltpu.VMEM((n_heads, HEAD, 2 * HEAD), F32), pltpu.VMEM((n_heads, HEAD, 2 * HEAD), F32),
                        pltpu.VMEM((n_heads, HEAD, 2 * HEAD), BF16), pltpu.VMEM((n_heads, HEAD, 2 * HEAD), BF16)],
        compiler_params=_params(("arbitrary", "arbitrary"), 56),
    )(*operands)
    return dq.reshape(s, w), dk.reshape(s, w), dv.reshape(s, w), ds


def _qkv_bwd(dproj, proj, dq, dk, dv, g_q, g_k, w):
    s = proj.shape[0]
    n_heads = w // HEAD
    tm = HEAD

    def body(dproj_hbm, q_ref, k_ref, gq_ref, gk_ref, dq_ref, dk_ref, dv_ref, out_ref, dgq_ref, dgk_ref):
        i = pl.program_id(0)
        gq = gq_ref[...] * (HEAD ** -0.5)
        gk = gk_ref[...]
        acc_q = jnp.zeros((1, HEAD), F32)
        acc_k = jnp.zeros((1, HEAD), F32)
        for h in range(n_heads):
            sl = slice(h * HEAD, (h + 1) * HEAD)
            q, k = q_ref[:, sl], k_ref[:, sl]
            dqn, dkn = dq_ref[:, sl], dk_ref[:, sl]
            rq, rk = _rms(q), _rms(k)
            out_ref[:, h * HEAD:(h + 1) * HEAD] = _rms_bwd(dqn, q, rq, gq).astype(BF16)
            out_ref[:, w + h * HEAD:w + (h + 1) * HEAD] = _rms_bwd(dkn, k, rk, gk).astype(BF16)
            acc_q += jnp.sum(dqn * q * rq, axis=0, keepdims=True)
            acc_k += jnp.sum(dkn * k * rk, axis=0, keepdims=True)
        out_ref[:, 2 * w:] = dv_ref[...].astype(BF16)

        @pl.when(i == 0)
        def _():
            dgq_ref[...] = jnp.zeros_like(dgq_ref)
            dgk_ref[...] = jnp.zeros_like(dgk_ref)

        dgq_ref[...] += acc_q * (HEAD ** -0.5)
        dgk_ref[...] += acc_k

    seg = lambda j: pl.BlockSpec((tm, w), lambda i, j=j: (i, j))
    vec = pl.BlockSpec((1, HEAD), lambda i: (0, 0))
    row = pl.BlockSpec((tm, w), lambda i: (i, 0))
    return pl.pallas_call(
        body, name="qkv_bwd", grid=(s // tm,),
        out_shape=[jax.ShapeDtypeStruct(dproj.shape, BF16),
                   jax.ShapeDtypeStruct((1, HEAD), F32), jax.ShapeDtypeStruct((1, HEAD), F32)],
        in_specs=[pl.BlockSpec(memory_space=pl.ANY), seg(3), seg(4), vec, vec] + [row] * 3,
        out_specs=[pl.BlockSpec((tm, 3 * w), lambda i: (i, 1)), vec, vec],
        input_output_aliases={0: 0},
        compiler_params=_params(("arbitrary",), 48),
    )(dproj, proj, proj, g_q, g_k, dq, dk, dv)


def _mixer_a(u, gv, ws_ref, bst_ref, lng, lnb, z_scr, ln_scr):
    n_groups = u.shape[1] // HEAD
    mu = jnp.mean(gv, axis=-1, keepdims=True)
    xc = gv - mu
    rs = lax.rsqrt(jnp.mean(xc * xc, axis=-1, keepdims=True) + EPS)
    xhat = xc * rs
    ln_scr[...] = (xhat * lng + lnb).astype(BF16)
    causal = _causal_mask()
    for g in range(n_groups):
        sl = slice(g * HEAD, (g + 1) * HEAD)
        wm = jnp.where(causal, ws_ref[g], 0.0).astype(BF16)
        z_scr[:, sl] = _dot(wm, ln_scr[:, sl]) + bst_ref[:, g:g + 1]
    return u, xhat, rs


def _causal_mask():
    token = lambda r: 16 * (r % SUB) + r // SUB
    row = lax.broadcasted_iota(jnp.int32, (HEAD, HEAD), 0)
    col = lax.broadcasted_iota(jnp.int32, (HEAD, HEAD), 1)
    return token(col) <= token(row)


def _merge_b(o_refs, lse_refs, yb_scr):
    n_heads = yb_scr.shape[1] // HEAD
    lses = [t[...] for t in lse_refs]
    m = jnp.maximum(jnp.maximum(lses[0], lses[1]), lses[2])
    tot = m + jnp.log(sum(jnp.exp(t - m) for t in lses))
    alphas = [jnp.exp(t - tot) for t in lses]
    for h in range(n_heads):
        sl = slice(h * HEAD, (h + 1) * HEAD)
        yb_scr[:, sl] = sum(a[:, h:h + 1] * o[:, sl].astype(F32) for a, o in zip(alphas, o_refs))
    return tot


def _mix_fwd(proj, outs, lses, w_s, bst, ln_g, ln_b, g_a, g_b, w, dep=None):
    s = proj.shape[0]
    n_groups = w // HEAD

    def body(au_ref, av_ref, az_ref, bz_ref, o1, o2, o3, l1, l2, l3, ws_ref, bst_ref,
             lng_ref, lnb_ref, ga_ref, gb_ref, p_ref, y_ref, z_scr, ln_scr, yb_scr):
        u, _, _ = _mixer_a(_gelu(au_ref[...]), _gelu(av_ref[...]), ws_ref, bst_ref, lng_ref[...], lnb_ref[...],
                           z_scr, ln_scr)
        ya = u * z_scr[...]
        silu_a, _ = _silu_and_grad(az_ref[...])
        perm = p_ref[...]
        y_ref[:, :w] = _dot(perm, (ya * _rms(ya) * ga_ref[...] * silu_a).astype(BF16), TN_DIMS).astype(BF16)
        _merge_b((o1, o2, o3), (l1, l2, l3), yb_scr)
        yb = yb_scr[...]
        silu_b, _ = _silu_and_grad(bz_ref[...])
        y_ref[:, w:] = _dot(perm, (yb * _rms(yb) * gb_ref[...] * silu_b).astype(BF16), TN_DIMS).astype(BF16)

    seg = lambda j: pl.BlockSpec((HEAD, w), lambda i, j=j: (i, j))
    row = pl.BlockSpec((HEAD, w), lambda i: (i, 0))
    stat = pl.BlockSpec((HEAD, HEAD), lambda i: (i, 0))
    vec = pl.BlockSpec((1, w), lambda i: (0, 0))
    in_specs = [seg(0), seg(1), seg(2), seg(6), row, row, row, stat, stat, stat,
                pl.BlockSpec((n_groups, HEAD, HEAD), lambda i: (0, 0, 0)),
                pl.BlockSpec((HEAD, n_groups), lambda i: (0, 0)), vec, vec, vec, vec,
                pl.BlockSpec((HEAD, HEAD), lambda i: (0, 0))]
    operands = [proj, proj, proj, proj, *outs, *lses, w_s, bst, ln_g, ln_b, g_a, g_b, _chunk_perm()]
    if dep is not None:
        body = _drop_arg(body, len(operands))
        in_specs.append(DEP_SPEC)
        operands.append(dep)
    return pl.pallas_call(
        body, name="mix_fwd", grid=(s // HEAD,),
        out_shape=jax.ShapeDtypeStruct((s, 2 * w), BF16), in_specs=in_specs,
        out_specs=pl.BlockSpec((HEAD, 2 * w), lambda i: (i, 0)),
        scratch_shapes=[pltpu.VMEM((HEAD, w), F32), pltpu.VMEM((HEAD, w), BF16), pltpu.VMEM((HEAD, w), F32)],
        compiler_params=_params(("arbitrary",), 48),
    )(*operands)


def _mix_bwd(proj, dy, outs, lses, w_s, bst, ln_g, ln_b, g_a, g_b, w):
    s = proj.shape[0]
    n_groups = w // HEAD

    def body(au_ref, av_ref, az_ref, bz_ref, dy_ref, o1, o2, o3, l1, l2, l3, ws_ref, bst_ref,
             lng_ref, lnb_ref, ga_ref, gb_ref,
             dproj_ref, dyb_ref, tot_ref, dl_ref, dws_ref, dbst_ref, dlng_ref, dlnb_ref, dga_ref, dgb_ref,
             z_scr, ln_scr, yb_scr, dz_scr, dln_scr):
        i = pl.program_id(0)

        @pl.when(i == 0)
        def _():
            for t in (dws_ref, dbst_ref, dlng_ref, dlnb_ref, dga_ref, dgb_ref):
                t[...] = jnp.zeros_like(t)

        az = az_ref[...]
        lng = lng_ref[...]
        u, du_dau = _gelu_and_grad(au_ref[...])
        gv, dgv_dav = _gelu_and_grad(av_ref[...])
        u, xhat, rs = _mixer_a(u, gv, ws_ref, bst_ref, lng, lnb_ref[...], z_scr, ln_scr)
        z = z_scr[...]
        ya = u * z
        ra = _rms(ya)
        silu_a, dsilu_a = _silu_and_grad(az)
        dya_all = dy_ref[:, :w]
        na = ya * ra * ga_ref[...]
        dna = dya_all * silu_a
        dproj_ref[:, 2 * w:3 * w] = (dya_all * na * dsilu_a).astype(BF16)
        dga_ref[...] += jnp.sum(dna * ya * ra, axis=0, keepdims=True)
        dya = _rms_bwd(dna, ya, ra, ga_ref[...])
        dproj_ref[:, :w] = (dya * z * du_dau).astype(BF16)
        dz_scr[...] = (dya * u).astype(BF16)

        causal = _causal_mask()
        for g in range(n_groups):
            sl = slice(g * HEAD, (g + 1) * HEAD)
            wm = jnp.where(causal, ws_ref[g], 0.0).astype(BF16)
            dz = dz_scr[:, sl]
            dln_scr[:, sl] = _dot(wm, dz, TN_DIMS)
            dws_ref[g] += jnp.where(causal, _dot(dz, ln_scr[:, sl], NT_DIMS), 0.0)
            dbst_ref[:, g:g + 1] += jnp.sum(dz.astype(F32), axis=-1, keepdims=True)
        dln = dln_scr[...]
        dlng_ref[...] += jnp.sum(dln * xhat, axis=0, keepdims=True)
        dlnb_ref[...] += jnp.sum(dln, axis=0, keepdims=True)
        gy = dln * lng
        dgv = rs * (gy - jnp.mean(gy, axis=-1, keepdims=True)
                    - xhat * jnp.mean(gy * xhat, axis=-1, keepdims=True))
        dproj_ref[:, w:2 * w] = (dgv * dgv_dav).astype(BF16)
        dproj_ref[:, 3 * w:6 * w] = jnp.zeros((HEAD, 3 * w), BF16)

        tot_ref[...] = _merge_b((o1, o2, o3), (l1, l2, l3), yb_scr)
        yb = yb_scr[...]
        rb = _rms(yb)
        bz = bz_ref[...]
        silu_b, dsilu_b = _silu_and_grad(bz)
        dyb_all = dy_ref[:, w:]
        dnb = dyb_all * silu_b
        dproj_ref[:, 6 * w:] = (dyb_all * yb * rb * gb_ref[...] * dsilu_b).astype(BF16)
        dgb_ref[...] += jnp.sum(dnb * yb * rb, axis=0, keepdims=True)
        dyb = _rms_bwd(dnb, yb, rb, gb_ref[...])
        dyb_ref[...] = dyb
        prod = dyb * yb
        dl_ref[...] = _lane_pick(
            [jnp.sum(prod[:, h * HEAD:(h + 1) * HEAD], axis=-1, keepdims=True) for h in range(n_groups)], HEAD)

    seg = lambda j: pl.BlockSpec((HEAD, w), lambda i, j=j: (i, j))
    row_w = pl.BlockSpec((HEAD, w), lambda i: (i, 0))
    stat = pl.BlockSpec((HEAD, HEAD), lambda i: (i, 0))
    vec = pl.BlockSpec((1, w), lambda i: (0, 0))
    ws_spec = pl.BlockSpec((n_groups, HEAD, HEAD), lambda i: (0, 0, 0))
    bst_spec = pl.BlockSpec((HEAD, n_groups), lambda i: (0, 0))
    vec_shape = jax.ShapeDtypeStruct((1, w), F32)
    return pl.pallas_call(
        body, name="mix_bwd", grid=(s // HEAD,),
        out_shape=[jax.ShapeDtypeStruct((s, N_SEG * w), BF16), jax.ShapeDtypeStruct((s, w), F32),
                   jax.ShapeDtypeStruct((s, HEAD), F32), jax.ShapeDtypeStruct((s, HEAD), F32),
                   jax.ShapeDtypeStruct((n_groups, HEAD, HEAD), F32), jax.ShapeDtypeStruct((HEAD, n_groups), F32),
                   vec_shape, vec_shape, vec_shape, vec_shape],
        in_specs=[seg(0), seg(1), seg(2), seg(6), pl.BlockSpec((HEAD, 2 * w), lambda i: (i, 0)),
                  row_w, row_w, row_w, stat, stat, stat, ws_spec, bst_spec, vec, vec, vec, vec],
        out_specs=[pl.BlockSpec((HEAD, N_SEG * w), lambda i: (i, 0)), row_w, stat, stat,
                   ws_spec, bst_spec, vec, vec, vec, vec],
        scratch_shapes=[pltpu.VMEM((HEAD, w), F32), pltpu.VMEM((HEAD, w), BF16), pltpu.VMEM((HEAD, w), F32),
                        pltpu.VMEM((HEAD, w), BF16), pltpu.VMEM((HEAD, w), F32)],
        compiler_params=_params(("arbitrary",), 56),
    )(proj, proj, proj, proj, dy, *outs, *lses, w_s, bst, ln_g, ln_b, g_a, g_b)


def _local_step(x, p, tgt, small, wg, ex, while_last_travels=None):
    s, d = x.shape
    n, _, c_in = wg.buffers["w_in"].shape
    assert n == N_DEV
    d_in = n * c_in
    w = d_in // N_SEG
    n_heads = w // HEAD
    p_dim, c_up = wg.buffers["w_ple_up"].shape[1:]
    assert s % (HEAD * DILATIONS[-1]) == 0 and w % HEAD == 0 and d == n * c_up == 2 * w

    near, far = (2, 4), (6,)
    wg.start("gather_in_pair", ["w_in"], (1,))
    token = wg.start("gather_in_near", ["w_in"], near)
    rest = ["w_out", "w_ple_gate", "w_ple_up"]

    hn = _rmsnorm_fwd(x, small["g_pre"], "pre_norm", True, dep=token)
    tm, tk = _tile(s, 1024), _tile(d, 1024)

    def in_proj(shards, name, carry, dep=None):
        return _matmul(
            hn, wg.buffers["w_in"], name=name, grid=(s // tm, len(shards), d // tk), dims=NN_DIMS,
            prefetch=jnp.stack(shards).astype(jnp.int32),
            a_spec=pl.BlockSpec((tm, tk), lambda i, j, k, sh: (i, k)),
            b_spec=pl.BlockSpec((None, tk, c_in), lambda i, j, k, sh: (sh[j], k, 0)),
            acc_shape=(tm, c_in), out_shapes=[jax.ShapeDtypeStruct((s, d_in), F32)],
            out_specs=[pl.BlockSpec((tm, c_in), lambda i, j, k, sh: (i, sh[j]))], carry=carry, dep=dep)[0]

    me = wg.me
    wg.arrived("gather_in_pair", [hn])
    proj = in_proj([me, me ^ 1], "in_proj_pair", None)
    for tag, chips in (("near", near), ("far", far)):
        token = wg.forward("gather_in_" + tag, [proj])
        if tag == "near":
            token = wg.relay_start("gather_in_far", ["w_in"], after=[token])
        else:
            token = wg.start("gather_rest", rest, (1,) + near, after=[token])
        proj = in_proj([me ^ k for k in chips], "in_proj_" + tag, proj, dep=token)
        wg.forwarded("gather_in_" + tag, [proj])
        proj = in_proj([me ^ k ^ 1 for k in chips], "in_proj_%s_forwarded" % tag, proj)
    win_g = wg.buffers["w_in"]

    qn, kn = _qkv_prep(proj, small["g_q"], small["g_k"], w)
    token = wg.forward("gather_rest", [qn])
    token = wg.relay_start("gather_rest_far", rest, after=[token])
    buckets = jnp.stack([_rel_buckets(dil) for dil in DILATIONS])
    bias = _bias_build(small["rel_bias"], buckets, n_heads)
    outs, lses = [], []
    for c, dil in enumerate(DILATIONS):
        o, l = _attn_fwd(qn, kn, proj, bias[c], dil, "attn_fwd_d%d" % dil, dep=token if c == 0 else None)
        outs.append(o)
        lses.append(l)
    token = wg.forward("gather_rest_far", [outs[-1]])

    ws_p = small["w_s"][:, CHUNK_ORDER][:, :, CHUNK_ORDER]
    bst = small["b_s"].T[CHUNK_ORDER]
    mix_args = (outs, lses, ws_p, bst, small["ln_v_g"], small["ln_v_b"], small["g_out_a"], small["g_out_b"], w)
    y = _mix_fwd(proj, *mix_args, dep=token)
    wg.forwarded("gather_rest", [y])
    wg.forwarded("gather_rest_far", [y])
    wout_g, wgate_g, wup_g = (wg.buffers[k] for k in rest)
    wout_f = wout_g.reshape(2 * w, d)
    wgate_f = wgate_g.reshape(d, d)

    tn = _tile(d, 1024)
    tk2 = _tile(2 * w, 1024)

    def resid_epilogue(acc, ex, outs_):
        outs_[0][...] = ex[0][...] + acc[...]

    h = _matmul(
        y, wout_f, name="out_proj", grid=(s // tm, d // tn, (2 * w) // tk2), dims=NN_DIMS,
        a_spec=pl.BlockSpec((tm, tk2), lambda i, j, k: (i, k)),
        b_spec=pl.BlockSpec((tk2, tn), lambda i, j, k: (k, j)),
        acc_shape=(tm, tn), out_shapes=[jax.ShapeDtypeStruct((s, d), F32)],
        out_specs=[pl.BlockSpec((tm, tn), lambda i, j, k: (i, j))],
        extra=(x,), extra_specs=(pl.BlockSpec((tm, tn), lambda i, j, k: (i, j)),),
        epilogue=resid_epilogue)[0]

    hn2 = _rmsnorm_fwd(h, small["g_ple"], "ple_norm", False)

    tmg = _tile(s, 512)

    def ple_epilogue(acc, ex, outs_):
        h_ref, p_ref, wup_ref, tgt_ref = ex
        dout_ref, dpre_ref, dup_ref, loss_ref = outs_
        gate = jax.nn.sigmoid(acc[...])
        up = _dot(p_ref[...].astype(BF16), wup_ref[...])
        err = h_ref[...] + gate * up - tgt_ref[...]
        dout = err * (1.0 / d)
        dout_ref[...] = dout
        dpre_ref[...] = (dout * up * gate * (1.0 - gate)).astype(BF16)
        dup_ref[...] = (dout * gate).astype(BF16)
        part = 0.5 * jnp.sum(err * err) * (1.0 / d)
        rr = lax.broadcasted_iota(jnp.int32, (8, HEAD), 0)
        cc = lax.broadcasted_iota(jnp.int32, (8, HEAD), 1)
        loss_ref[...] = jnp.where((rr == 0) & (cc == 0), part, 0.0)

    tile_ij = pl.BlockSpec((tmg, c_up), lambda i, j, k: (i, j))
    dout, dpre, dup, loss_parts = _matmul(
        hn2, wgate_f, name="ple_gate", grid=(s // tmg, n, 1), dims=NN_DIMS,
        a_spec=pl.BlockSpec((tmg, d), lambda i, j, k: (i, 0)),
        b_spec=pl.BlockSpec((d, c_up), lambda i, j, k: (0, j)),
        acc_shape=(tmg, c_up),
        out_shapes=[jax.ShapeDtypeStruct((s, d), F32), jax.ShapeDtypeStruct((s, d), BF16),
                    jax.ShapeDtypeStruct((s, d), BF16), jax.ShapeDtypeStruct((s // tmg * 8, n * HEAD), F32)],
        out_specs=[tile_ij, tile_ij, tile_ij, pl.BlockSpec((8, HEAD), lambda i, j, k: (i, j))],
        extra=(h, p, wup_g, tgt),
        extra_specs=(tile_ij, pl.BlockSpec((tmg, p_dim), lambda i, j, k: (i, 0)),
                     pl.BlockSpec((None, p_dim, c_up), lambda i, j, k: (j, 0, 0)), tile_ij),
        epilogue=ple_epilogue)
    loss = jnp.sum(loss_parts)

    tks = _tile(s, 1024)
    g_wup = _matmul(
        p, dup, name="grad_w_up", grid=(1, n, s // tks), dims=TN_DIMS,
        a_spec=pl.BlockSpec((tks, p_dim), lambda i, j, k: (k, 0)),
        b_spec=pl.BlockSpec((tks, c_up), lambda i, j, k: (k, j)),
        acc_shape=(p_dim, c_up), out_shapes=[jax.ShapeDtypeStruct((n, p_dim, c_up), BF16)],
        out_specs=[pl.BlockSpec((None, p_dim, c_up), lambda i, j, k: (j, 0, 0))])[0]

    def tn_matmul(a, b, name):
        m_, n_ = a.shape[1], b.shape[1]
        bm, bn = _tile(m_, 1024), _tile(n_, 1024)
        return _matmul(
            a, b, name=name, grid=(m_ // bm, n_ // bn, s // tks), dims=TN_DIMS,
            a_spec=pl.BlockSpec((tks, bm), lambda i, j, k: (k, i)),
            b_spec=pl.BlockSpec((tks, bn), lambda i, j, k: (k, j)),
            acc_shape=(bm, bn), out_shapes=[jax.ShapeDtypeStruct((m_, n_), BF16)],
            out_specs=[pl.BlockSpec((bm, bn), lambda i, j, k: (i, j))])[0]

    def nt_matmul(a, b, name, out_dtype, dep=None):
        k_, n_ = a.shape[1], b.shape[0]
        bm, bn, bk = _tile(s, 1024), _tile(n_, 1024), _tile(k_, 1024)
        return _matmul(
            a, b, name=name, grid=(s // bm, n_ // bn, k_ // bk), dims=NT_DIMS,
            a_spec=pl.BlockSpec((bm, bk), lambda i, j, k: (i, k)),
            b_spec=pl.BlockSpec((bn, bk), lambda i, j, k: (j, k)),
            acc_shape=(bm, bn), out_shapes=[jax.ShapeDtypeStruct((s, n_), out_dtype)],
            out_specs=[pl.BlockSpec((bm, bn), lambda i, j, k: (i, j))], dep=dep)[0]

    by_core = lambda g: g.reshape((N_CHIP, 2) + g.shape[-2:])
    g_wgate = tn_matmul(hn2, dpre, "grad_w_gate").reshape(wgate_g.shape)
    dhn2 = nt_matmul(dpre, wgate_f, "ple_gate_bwd", F32)
    dh, dh_b, dh_bp, dg_ple = _rmsnorm_bwd(dhn2, h, small["g_ple"], dout, "ple_norm_bwd", False, True)
    g_wout = tn_matmul(y, dh_b, "grad_w_out").reshape(wout_g.shape)

    late = ("w_out", "w_ple_gate", "w_ple_up")
    late_parts = (g_wout, g_wgate, g_wup)
    token = ex.push_pairs("pair_late", [by_core(g) for g in late_parts])
    dy = nt_matmul(dh_bp, wout_f, "out_proj_bwd", F32, dep=token)
    (dproj, dyb, lse_tot, delta, dws, dbst, dlng, dlnb, dga, dgb) = _mix_bwd(proj, dy, *mix_args)
    both_columns, from_sibling = ex.pairs_done("pair_late", [dproj])
    pair_sums = [_pair_add(mine.reshape((N_DEV,) + mine.shape[-2:]), theirs, "pair_add_" + k, ex.core)
                 for k, mine, theirs in zip(late, both_columns, from_sibling)]
    token = ex.push_chips("chip_late", pair_sums)

    running, dss = None, []
    for c, dil in enumerate(DILATIONS):
        *running, ds = _attn_bwd(qn, kn, proj, dyb, lse_tot, delta, bias[c], dil, "attn_bwd_d%d" % dil,
                                 running=running, dep=token if c == 0 else None)
        dss.append(ds)
    d_rel = _bias_grad(jnp.stack(dss), buckets, n_heads)
    dproj, dgq, dgk = _qkv_bwd(dproj, proj, *running, small["g_q"], small["g_k"], w)
    pair_sums, landed, _ = ex.chips_done("chip_late", [dproj])
    delivered = {k: (mine, theirs) for k, mine, theirs in zip(late, pair_sums, landed)}

    token_row = np.argsort(CHUNK_ORDER)
    dws = dws[:, token_row][:, :, token_row]
    dbst = dbst[token_row]
    small_grads = {
        "w_s": dws, "b_s": dbst.T, "ln_v_g": dlng, "ln_v_b": dlnb, "g_q": dgq, "g_k": dgk,
        "rel_bias": d_rel, "g_out_a": dga, "g_out_b": dgb, "g_ple": dg_ple,
    }

    bm = _tile(d, 1024)

    def grad_w_in(core, name, dep=None):
        return _matmul(
            hn, dproj, name=name, grid=(d // bm, N_CHIP, s // tks), dims=TN_DIMS, prefetch=core.reshape(1),
            a_spec=pl.BlockSpec((tks, bm), lambda i, j, k, core_ref: (k, i)),
            b_spec=pl.BlockSpec((tks, c_in), lambda i, j, k, core_ref: (k, 2 * j + core_ref[0])),
            acc_shape=(bm, c_in), out_shapes=[jax.ShapeDtypeStruct((N_CHIP, d, c_in), BF16)],
            out_specs=[pl.BlockSpec((None, bm, c_in), lambda i, j, k, core_ref: (j, i, 0))], dep=dep)[0]

    for_sibling = grad_w_in(1 - ex.core, "grad_w_in_sibling")
    token = ex.push_pairs("pair_in", [for_sibling])
    mine = grad_w_in(ex.core, "grad_w_in_mine", dep=token)
    _, from_sibling = ex.pairs_done("pair_in", [mine])
    pair_sum = _pair_add(mine, from_sibling[0], "pair_add_w_in")
    token = ex.push_chips("chip_in", [pair_sum], _pack_small(small_grads, SMALL_EARLY))

    dhn = _matmul(
        dproj, win_g, name="in_proj_bwd", grid=(s // tm, d // tn, n), dims=NT_DIMS,
        a_spec=pl.BlockSpec((tm, c_in), lambda i, j, k: (i, k)),
        b_spec=pl.BlockSpec((None, tn, c_in), lambda i, j, k: (k, j, 0)),
        acc_shape=(tm, tn), out_shapes=[jax.ShapeDtypeStruct((s, d), F32)],
        out_specs=[pl.BlockSpec((tm, tn), lambda i, j, k: (i, j))], dep=token)[0]
    grad_x, dg_pre = _rmsnorm_bwd(dhn, x, small["g_pre"], dh, "pre_norm_bwd", True, False)
    extra = while_last_travels(token, delivered) if while_last_travels is not None else []
    pair_sums, landed, slabs = ex.chips_done("chip_in", [grad_x] + list(extra))
    delivered["w_in"] = (pair_sums[0], landed[0])
    small_grads["g_pre"] = dg_pre
    return loss, grad_x, small_grads, delivered, slabs, extra


SMALL_EARLY = ("w_s", "b_s", "ln_v_g", "ln_v_b", "g_q", "g_k", "rel_bias", "g_out_a", "g_out_b", "g_ple")
SMALL_LAST = ("g_pre",)
SMALL_NAMES = SMALL_LAST + SMALL_EARLY


def _pack_small(tree, names):
    parts = []
    for name in names:
        flat = tree[name].astype(F32).reshape(-1)
        pad = (-flat.shape[0]) % HEAD
        parts.append(jnp.pad(flat, (0, pad)) if pad else flat)
    slab = jnp.concatenate(parts).reshape(-1, HEAD)
    pad_rows = (-slab.shape[0]) % 8
    return jnp.pad(slab, ((0, pad_rows), (0, 0))) if pad_rows else slab


def _unpack_small(slab, like, names):
    flat = slab.reshape(-1)
    out, off = {}, 0
    for name in names:
        size = like[name].size
        out[name] = flat[off:off + size].reshape(like[name].shape)
        off += size + (-size) % HEAD
    return out


def _peer(k):
    x, y, c = (lax.axis_index(a) for a in AXES)
    bits = ((k >> 2) & 1, (k >> 1) & 1, k & 1)
    px, py, pc = (1 - v if b else v for v, b in zip((x, y, c), bits))
    return (px, py, pc), 4 * px + 2 * py + pc


def _my_index():
    x, y, c = (lax.axis_index(a) for a in AXES)
    return 4 * x + 2 * y + c


N_CHIP = 4
HBM_SPEC = pl.BlockSpec(memory_space=pl.ANY)


def _remote(src, dst, send_sem, recv_sem, peer):
    return pltpu.make_async_remote_copy(src_ref=src, dst_ref=dst, send_sem=send_sem, recv_sem=recv_sem,
                                        device_id=peer, device_id_type=pl.DeviceIdType.MESH)


SEM_SPEC = pl.BlockSpec(memory_space=pltpu.SEMAPHORE)
HBM_ONLY = pl.BlockSpec(memory_space=pltpu.HBM)
DATAFLOW = pltpu.SideEffectType.DATAFLOW_SIDE_EFFECTING


def _comm_call(name, arrays, *, wait=None, start=None, after=()):
    n, n_after = len(arrays), len(after)

    def body(*refs):
        ins = refs[:n]
        pos = n
        if wait is not None:
            for cp in wait[2](ins, refs[pos], refs[pos + 1]):
                cp.wait()
            pos += 2
        outs = refs[pos + n_after:]
        if start is not None:
            for cp in start[1](ins, outs[0], outs[1]):
                cp.start()
        outs[-1][...] = jnp.zeros_like(outs[-1])

    operands = [pltpu.with_memory_space_constraint(a, pltpu.HBM) for a in arrays]
    in_specs = [HBM_ONLY] * n
    if wait is not None:
        operands += [wait[0], wait[1]]
        in_specs += [SEM_SPEC, SEM_SPEC]
    operands += list(after)
    in_specs += [HBM_SPEC] * n_after
    out_shape, out_specs = [], []
    if start is not None:
        out_shape += [pltpu.SemaphoreType.DMA((start[0],))] * 2
        out_specs += [SEM_SPEC, SEM_SPEC]
    first = len(out_shape)
    out_shape += [pltpu.HBM(a.shape, a.dtype) for a in arrays] + [jax.ShapeDtypeStruct((SUB, HEAD), F32)]
    out_specs += [HBM_ONLY] * n + [pl.BlockSpec(memory_space=pltpu.VMEM)]
    res = pl.pallas_call(
        body, name=name, out_shape=tuple(out_shape), in_specs=tuple(in_specs), out_specs=tuple(out_specs),
        input_output_aliases={i: first + i for i in range(n)},
        compiler_params=pltpu.CompilerParams(has_side_effects=DATAFLOW),
    )(*operands)
    sems = (res[0], res[1]) if start is not None else None
    return list(res[first:first + n]), sems, res[-1]


class _GradExchange:
    def __init__(self):
        x, y, c = (lax.axis_index(a) for a in AXES)
        self.core = c.astype(jnp.int32)
        self.chip = (2 * x + y).astype(jnp.int32)
        self.pending = {}

    def _pair_copies(self, n_arr):
        def make(refs, send_sems, recv_sems):
            sibling, _ = _peer(1)
            other = 1 - lax.axis_index("c")
            srcs, lands = refs[:n_arr], refs[n_arr:]
            pick = lambda ref, ch: ref.at[ch, other] if len(ref.shape) == 4 else ref.at[ch]
            return [_remote(pick(srcs[a], ch), lands[a].at[ch], send_sems.at[a * N_CHIP + ch],
                            recv_sems.at[a * N_CHIP + ch], sibling)
                    for a in range(n_arr) for ch in range(N_CHIP)]
        return make

    def _chip_copies(self, n_arr, with_slab):
        def make(refs, send_sems, recv_sems):
            x, y = lax.axis_index("x"), lax.axis_index("y")
            my_chip = 2 * x + y
            srcs, lands = refs[:n_arr], refs[n_arr:2 * n_arr]
            copies = []
            for j, k in enumerate((2, 4, 6)):
                peer, peer_idx = _peer(k)
                for a in range(n_arr):
                    copies.append(_remote(srcs[a].at[peer_idx // 2], lands[a].at[my_chip],
                                          send_sems.at[3 * a + j], recv_sems.at[3 * a + j], peer))
            if with_slab:
                slab, slab_land = refs[2 * n_arr], refs[2 * n_arr + 1]
                for k in range(1, N_DEV):
                    peer, _ = _peer(k)
                    copies.append(_remote(slab, slab_land.at[_my_index()], send_sems.at[3 * n_arr + k - 1],
                                          recv_sems.at[3 * n_arr + k - 1], peer))
            return copies
        return make

    def push_pairs(self, tag, for_sibling):
        n_arr = len(for_sibling)
        lands = [lax.empty((N_CHIP,) + a.shape[-2:], a.dtype) for a in for_sibling]
        make = self._pair_copies(n_arr)
        arrays, sems, token = _comm_call(tag + "_start", list(for_sibling) + lands, start=(n_arr * N_CHIP, make))
        self.pending[tag] = (arrays, sems, make, n_arr)
        return token

    def pairs_done(self, tag, after):
        arrays, sems, make, n_arr = self.pending.pop(tag)
        arrays, _, _ = _comm_call(tag + "_wait", arrays, wait=(sems[0], sems[1], make), after=after)
        return arrays[:n_arr], arrays[n_arr:]

    def push_chips(self, tag, pair_sums, slab=None):
        n_arr = len(pair_sums)
        arrays = list(pair_sums) + [lax.empty(a.shape, a.dtype) for a in pair_sums]
        n_copies = 3 * n_arr
        if slab is not None:
            arrays += [slab, lax.empty((N_DEV,) + slab.shape, slab.dtype)]
            n_copies += N_DEV - 1
        make = self._chip_copies(n_arr, slab is not None)
        arrays, sems, token = _comm_call(tag + "_start", arrays, start=(n_copies, make))
        self.pending[tag] = (arrays, sems, make, n_arr)
        return token

    def chips_done(self, tag, after):
        arrays, sems, make, n_arr = self.pending.pop(tag)
        arrays, _, _ = _comm_call(tag + "_wait", arrays, wait=(sems[0], sems[1], make), after=after)
        return arrays[:n_arr], arrays[n_arr:2 * n_arr], arrays[2 * n_arr:]


def _cast_place(w, name):
    r, c = w.shape
    tr = r if r * c <= MIB else 1 << ((MIB // c).bit_length() - 1)
    assert r % tr == 0

    def body(me_ref, w_ref, o_ref):
        o_ref[...] = w_ref[...].astype(BF16)

    return pl.pallas_call(
        body, name=name, out_shape=jax.ShapeDtypeStruct((N_DEV, r, c), BF16),
        grid_spec=pltpu.PrefetchScalarGridSpec(
            num_scalar_prefetch=1, grid=(r // tr,),
            in_specs=[pl.BlockSpec((tr, c), lambda i, me_ref: (i, 0))],
            out_specs=pl.BlockSpec((None, tr, c), lambda i, me_ref: (me_ref[0], i, 0))),
        compiler_params=_params(("arbitrary",), 40),
    )(_my_index().astype(jnp.int32).reshape(1), w)


class _WeightGather:
    CHIPS = (2, 4, 6)

    def __init__(self, buffers):
        self.buffers = dict(buffers)
        self.pending = {}
        self.me = _my_index().astype(jnp.int32)

    def _own_slot_to(self, peers):
        def make(refs, send_sems, recv_sems):
            me = _my_index()
            return [_remote(ref.at[me], ref.at[me], send_sems.at[len(peers) * a + j],
                            recv_sems.at[len(peers) * a + j], _peer(k)[0])
                    for a, ref in enumerate(refs) for j, k in enumerate(peers)]
        return make

    def _forward_from(self, chips):
        def make(refs, send_sems, recv_sems):
            sibling, _ = _peer(1)
            copies = []
            for a, ref in enumerate(refs):
                for j, k in enumerate(chips):
                    slot = ref.at[_peer(k)[1]]
                    copies.append(_remote(slot, slot, send_sems.at[len(chips) * a + j],
                                          recv_sems.at[len(chips) * a + j], sibling))
            return copies
        return make

    def _run(self, call, names, **kw):
        arrays, sems, token = _comm_call(call, [self.buffers[k] for k in names], **kw)
        self.buffers.update(zip(names, arrays))
        return sems, token

    def start(self, tag, names, peers, after=()):
        make = self._own_slot_to(peers)
        sems, token = self._run(tag + "_start", names, start=(len(names) * len(peers), make), after=after)
        self.pending[tag] = (names, sems, make, peers)
        return token

    @staticmethod
    def _relay(refs, send_sems, recv_sems):
        x, y, c = (lax.axis_index(a) for a in AXES)
        peer = (x ^ (1 - c), y ^ c, c)
        slot = _my_index() ^ (2 + 2 * c)
        return [_remote(ref.at[slot], ref.at[slot], send_sems.at[a], recv_sems.at[a], peer)
                for a, ref in enumerate(refs)]

    def relay_start(self, tag, names, after=()):
        sems, token = self._run(tag + "_start", names, start=(len(names), self._relay), after=after)
        self.pending[tag] = (names, sems, self._relay, (6,))
        return token

    def arrived(self, tag, after):
        names, sems, make, _ = self.pending.pop(tag)
        self._run(tag + "_wait", names, wait=(sems[0], sems[1], make), after=after)

    def forward(self, tag, after):
        names, sems, make, peers = self.pending.pop(tag)
        chips = tuple(k for k in peers if k != 1)
        onward = self._forward_from(chips)
        new_sems, token = self._run(tag + "_forward", names, wait=(sems[0], sems[1], make),
                                    start=(len(chips) * len(names), onward), after=after)
        self.pending[tag + "/fwd"] = (names, new_sems, onward)
        return token

    def forwarded(self, tag, after):
        names, sems, make = self.pending.pop(tag + "/fwd")
        self._run(tag + "_done", names, wait=(sems[0], sems[1], make), after=after)


def _pair_add(mine, theirs, name, core=None):
    _, r, c_dim = theirs.shape
    tr = r if r * c_dim <= MIB else 1 << ((MIB // c_dim).bit_length() - 1)
    assert r % tr == 0
    stride = 1 if core is None else 2
    offset = jnp.zeros((1,), jnp.int32) if core is None else core.reshape(1)

    def body(off_ref, a_ref, b_ref, o_ref):
        o_ref[...] = (a_ref[...].astype(F32) + b_ref[...].astype(F32)).astype(BF16)

    blk = (None, tr, c_dim)
    return pl.pallas_call(
        body, name=name, out_shape=jax.ShapeDtypeStruct(theirs.shape, BF16),
        grid_spec=pltpu.PrefetchScalarGridSpec(
            num_scalar_prefetch=1, grid=(N_CHIP, r // tr),
            in_specs=[pl.BlockSpec(blk, lambda ch, i, off_ref: (stride * ch + off_ref[0], i, 0)),
                      pl.BlockSpec(blk, lambda ch, i, off_ref: (ch, i, 0))],
            out_specs=pl.BlockSpec(blk, lambda ch, i, off_ref: (ch, i, 0))),
        compiler_params=_params(("arbitrary", "arbitrary"), 40),
    )(offset, mine, theirs)


def _slab_exchange(slab, name):
    def body(slab_in, slab_out, send_sems, recv_sems, local_sem):
        me = _my_index()
        local = pltpu.make_async_copy(slab_in, slab_out.at[me], local_sem)
        local.start()
        sends = []
        for k in range(1, N_DEV):
            peer, _ = _peer(k)
            sends.append(_remote(slab_in, slab_out.at[me], send_sems.at[k - 1], recv_sems.at[k - 1], peer))
        for cp in sends:
            cp.start()
        for k in range(1, N_DEV):
            peer, peer_idx = _peer(k)
            slot = slab_out.at[peer_idx]
            _remote(slot, slot, send_sems.at[k - 1], recv_sems.at[k - 1], peer).wait_recv()
        for cp in sends:
            cp.wait_send()
        local.wait()

    return pl.pallas_call(
        body, name=name, out_shape=jax.ShapeDtypeStruct((N_DEV,) + slab.shape, slab.dtype),
        in_specs=[HBM_SPEC], out_specs=HBM_SPEC,
        scratch_shapes=[pltpu.SemaphoreType.DMA((N_DEV - 1,)), pltpu.SemaphoreType.DMA((N_DEV - 1,)),
                        pltpu.SemaphoreType.DMA],
        compiler_params=pltpu.CompilerParams(has_side_effects=True),
    )(slab)


def _adamw_math(w, g, m, v):
    m = ADAM_B1 * m + (1.0 - ADAM_B1) * g
    v = ADAM_B2 * v + (1.0 - ADAM_B2) * (g * g)
    m_hat = m / (1.0 - ADAM_B1 ** ADAM_STEP)
    v_hat = v / (1.0 - ADAM_B2 ** ADAM_STEP)
    delta = -ADAM_LR * (m_hat / (jnp.sqrt(v_hat) + ADAM_EPS) + ADAM_WD * w)
    return delta, m, v


def _adamw(parts, own, place, w, m, v, name, dep=None):
    n_parts = parts.shape[0]
    r, c = w.shape
    budget = 280 * 1024
    tr = r if r * c <= budget else 1 << ((budget // c).bit_length() - 1)
    assert r % tr == 0

    def body(place_ref, p_ref, own_ref, w_ref, m_ref, v_ref, g_ref, d_ref, nm_ref, nv_ref):
        mine = own_ref[...].astype(F32)
        g = None
        for i in range(n_parts):
            term = jnp.where(place_ref[0] == i, mine, p_ref[i].astype(F32))
            g = term if g is None else g + term
        delta, nm, nv = _adamw_math(w_ref[...], g, m_ref[...], v_ref[...])
        g_ref[...] = g
        d_ref[...] = delta
        nm_ref[...] = nm
        nv_ref[...] = nv

    blk = pl.BlockSpec((tr, c), lambda i, place_ref: (i, 0))
    shape = jax.ShapeDtypeStruct((r, c), F32)
    in_specs = [pl.BlockSpec((n_parts, tr, c), lambda i, place_ref: (0, i, 0)),
                pl.BlockSpec((None, tr, c), lambda i, place_ref: (place_ref[1], i, 0)), blk, blk, blk]
    operands = [parts, own, w, m, v]
    if dep is not None:
        body = _drop_arg(body, 1 + len(operands))
        in_specs.append(pl.BlockSpec((SUB, HEAD), lambda i, place_ref: (0, 0)))
        operands.append(dep)
    return pl.pallas_call(
        body, name=name, out_shape=[shape] * 4,
        grid_spec=pltpu.PrefetchScalarGridSpec(
            num_scalar_prefetch=1, grid=(r // tr,), in_specs=in_specs, out_specs=[blk] * 4),
        compiler_params=_params(("arbitrary",), 48),
    )(place, *operands)


def kernel(x, p, g_pre, w_in, w_s, b_s, ln_v_g, ln_v_b, g_q, g_k, rel_bias, g_out_a, g_out_b, w_out, g_ple, w_ple_gate, w_ple_up, loss_target, m_g_pre, m_w_in, m_w_s, m_b_s, m_ln_v_g, m_ln_v_b, m_g_q, m_g_k, m_rel_bias, m_g_out_a, m_g_out_b, m_w_out, m_g_ple, m_w_ple_gate, m_w_ple_up, v_g_pre, v_w_in, v_w_s, v_b_s, v_ln_v_g, v_ln_v_b, v_g_q, v_g_k, v_rel_bias, v_g_out_a, v_g_out_b, v_w_out, v_g_ple, v_w_ple_gate, v_w_ple_up):
    args = dict(locals())
    small = {"g_pre": g_pre, "w_s": w_s[0], "b_s": b_s[0], "ln_v_g": ln_v_g, "ln_v_b": ln_v_b, "g_q": g_q,
             "g_k": g_k, "rel_bias": rel_bias, "g_out_a": g_out_a, "g_out_b": g_out_b, "g_ple": g_ple}
    big_names = ("w_in", "w_out", "w_ple_gate", "w_ple_up")
    big = {k: args[k][0] for k in big_names}

    wg = _WeightGather({k: _cast_place(big[k], "place_" + k) for k in big_names})
    ex = _GradExchange()
    results = {}

    def big_adamw(k, delivered, dep=None):
        mine, theirs = delivered[k]
        place = jnp.stack([ex.chip, ex.chip])
        return _adamw(theirs, mine, place, big[k], args["m_" + k][0], args["v_" + k][0], "adamw_" + k, dep=dep)

    def while_last_travels(token, delivered):
        done = []
        for k in big_names[1:]:
            results[k] = big_adamw(k, delivered, dep=token)
            done.append(results[k][0])
        return done

    loss, grad_x, small_parts, delivered, slabs, _ = _local_step(
        x[0], p[0, 0], loss_target[0], small, wg, ex, while_last_travels)
    results["w_in"] = big_adamw("w_in", delivered)
    for k in big_names:
        results[k] = [t[None] for t in results[k]]

    squeeze = lambda t: {k: (t[k][0] if k in ("w_s", "b_s") else t[k]) for k in SMALL_NAMES}
    small_m = squeeze({k: args["m_" + k] for k in SMALL_NAMES})
    small_v = squeeze({k: args["v_" + k] for k in SMALL_NAMES})
    me = _my_index().astype(jnp.int32)
    place = jnp.stack([me, jnp.zeros((), jnp.int32)])
    last_slab = _pack_small(small_parts, SMALL_LAST)
    groups = ((SMALL_EARLY, slabs[1], slabs[0], "adamw_small"),
              (SMALL_LAST, _slab_exchange(last_slab, "last_exchange"), last_slab, "adamw_last"))
    for names_, parts, own, call_name in groups:
        packed = _adamw(parts, own[None], place, _pack_small(small, names_), _pack_small(small_m, names_),
                        _pack_small(small_v, names_), call_name)
        for idx in range(4):
            tree = _unpack_small(packed[idx], small, names_)
            for k in names_:
                results.setdefault(k, [None] * 4)[idx] = tree[k].reshape(args[k].shape)

    names = ("g_pre", "w_in", "w_s", "b_s", "ln_v_g", "ln_v_b", "g_q", "g_k", "rel_bias", "g_out_a", "g_out_b",
             "w_out", "g_ple", "w_ple_gate", "w_ple_up")
    total = lax.psum(loss, AXES)
    out = [total, grad_x[None]]
    for idx in range(4):
        out += [results[k][idx] for k in names]
    return tuple(out)
```

```python
import functools
import math

import numpy as np
import jax
import jax.numpy as jnp
from jax import lax
from jax.experimental import pallas as pl
from jax.experimental.pallas import tpu as pltpu

F32 = jnp.float32
BF16 = jnp.bfloat16
EPS = 1e-6
NEG_INF = -1e30
HEAD = 128
DILATIONS = (1, 4, 16)
NUM_BUCKETS = 32
MAX_DISTANCE = 2048
N_SEG = 7
ADAM_LR = 0.001
ADAM_B1 = 0.9
ADAM_B2 = 0.999
ADAM_EPS = 1e-08
ADAM_WD = 0.01
ADAM_STEP = 10
AXES = ("x", "y", "c")
N_DEV = 8
MIB = 1 << 20

SUB = 8

CHUNK_ORDER = np.array([16 * (r % SUB) + r // SUB for r in range(HEAD)])
BLOCK_ORDER = {
    1: CHUNK_ORDER,
    4: np.array([32 * (r // 32) + 4 * (r % SUB) + (r // SUB) % 4 for r in range(HEAD)]),
    16: np.arange(HEAD),
}

NT_DIMS = (((1,), (1,)), ((), ()))
TN_DIMS = (((0,), (0,)), ((), ()))
NN_DIMS = (((1,), (0,)), ((), ()))


def _params(semantics, vmem_mib):
    return pltpu.CompilerParams(dimension_semantics=semantics, vmem_limit_bytes=vmem_mib * MIB)


def _gelu(a):
    return 0.5 * a * (1.0 + lax.erf(a * (2.0 ** -0.5)))


def _gelu_and_grad(a):
    cdf = 0.5 * (1.0 + lax.erf(a * (2.0 ** -0.5)))
    return a * cdf, cdf + a * jnp.exp(-0.5 * a * a) * ((2.0 * math.pi) ** -0.5)


def _silu_and_grad(a):
    s = jax.nn.sigmoid(a)
    return a * s, s * (1.0 + a * (1.0 - s))


def _rms(v):
    return lax.rsqrt(jnp.mean(v * v, axis=-1, keepdims=True) + EPS)


def _rms_bwd(dy, v, r, g):
    gy = dy * g
    return r * gy - v * (r * r * r) * jnp.mean(gy * v, axis=-1, keepdims=True)


def _dot(a, b, dims=NN_DIMS):
    return lax.dot_general(a, b, dims, preferred_element_type=F32)


def _lane_pick(cols, width):
    rows = cols[0].shape[0]
    lane = lax.broadcasted_iota(jnp.int32, (rows, width), 1)
    out = jnp.zeros((rows, width), F32)
    for h, col in enumerate(cols):
        out = jnp.where(lane == h, col, out)
    return out


def _chunk_perm():
    return jnp.asarray(np.eye(HEAD, dtype=np.float32)[CHUNK_ORDER], BF16)


def _unpermute_f32(p, v):
    hi = v.astype(BF16)
    rest = v - hi.astype(F32)
    mid = rest.astype(BF16)
    lo = (rest - mid.astype(F32)).astype(BF16)
    return _dot(p, hi, TN_DIMS) + _dot(p, mid, TN_DIMS) + _dot(p, lo, TN_DIMS)


def _rmsnorm_fwd(x, g, name, permute, dep=None):
    s, d = x.shape
    tm = HEAD

    def body(x_ref, g_ref, p_ref, o_ref):
        v = x_ref[...]
        out = (v * _rms(v) * g_ref[...]).astype(BF16)
        if permute:
            out = _dot(p_ref[...], out).astype(BF16)
        o_ref[...] = out

    in_specs = [pl.BlockSpec((tm, d), lambda i: (i, 0)), pl.BlockSpec((1, d), lambda i: (0, 0)),
                pl.BlockSpec((HEAD, HEAD), lambda i: (0, 0))]
    operands = [x, g, _chunk_perm()]
    if dep is not None:
        body = _drop_arg(body, len(operands))
        in_specs.append(DEP_SPEC)
        operands.append(dep)
    return pl.pallas_call(
        body, name=name, grid=(s // tm,),
        out_shape=jax.ShapeDtypeStruct((s, d), BF16), in_specs=in_specs,
        out_specs=pl.BlockSpec((tm, d), lambda i: (i, 0)),
        compiler_params=_params(("arbitrary",), 40),
    )(*operands)


def _rmsnorm_bwd(dy, v, g, res, name, dy_permuted, with_bf16):
    s, d = v.shape
    tm = HEAD
    perm = _chunk_perm()

    def body(dy_ref, v_ref, g_ref, res_ref, p_ref, *outs):
        dx_ref, dg_ref = outs[0], outs[-1]
        i = pl.program_id(0)
        vv, dyv = v_ref[...], dy_ref[...]
        if dy_permuted:
            dyv = _unpermute_f32(p_ref[...], dyv)
        r = _rms(vv)
        dx = res_ref[...] + _rms_bwd(dyv, vv, r, g_ref[...])
        dx_ref[...] = dx
        if with_bf16:
            dxb = dx.astype(BF16)
            outs[1][...] = dxb
            outs[2][...] = _dot(p_ref[...], dxb).astype(BF16)

        @pl.when(i == 0)
        def _():
            dg_ref[...] = jnp.zeros_like(dg_ref)

        dg_ref[...] += jnp.sum(dyv * vv * r, axis=0, keepdims=True)

    row = pl.BlockSpec((tm, d), lambda i: (i, 0))
    vec = pl.BlockSpec((1, d), lambda i: (0, 0))
    shapes = [jax.ShapeDtypeStruct((s, d), F32)]
    specs = [row]
    if with_bf16:
        shapes += [jax.ShapeDtypeStruct((s, d), BF16)] * 2
        specs += [row, row]
    shapes.append(jax.ShapeDtypeStruct((1, d), F32))
    specs.append(vec)
    return pl.pallas_call(
        body, name=name, grid=(s // tm,), out_shape=shapes,
        in_specs=[row, row, vec, row, pl.BlockSpec((HEAD, HEAD), lambda i: (0, 0))], out_specs=specs,
        compiler_params=_params(("arbitrary",), 40),
    )(dy, v, g, res, perm)


DEP_SPEC = pl.BlockSpec((SUB, HEAD), lambda *_: (0, 0))


def _drop_arg(body, pos):
    return lambda *refs: body(*refs[:pos], *refs[pos + 1:])


def _matmul(a, b, *, name, grid, a_spec, b_spec, dims, acc_shape, out_shapes, out_specs,
            extra=(), extra_specs=(), epilogue=None, vmem_mib=48, dep=None, prefetch=None, carry=None):
    nk = grid[2]
    n_user = len(extra)
    for unread, spec in ((dep, DEP_SPEC), (carry, HBM_SPEC)):
        if unread is not None:
            extra, extra_specs = tuple(extra) + (unread,), tuple(extra_specs) + (spec,)
    n_extra, n_out = len(extra), len(out_shapes)
    n_pre = 0 if prefetch is None else 1
    aliases = {} if carry is None else {n_pre + 2 + n_extra - 1: 0}

    def body(*refs):
        refs = refs[n_pre:]
        a_ref, b_ref = refs[0], refs[1]
        ex = refs[2:2 + n_user]
        outs = refs[2 + n_extra:2 + n_extra + n_out]
        acc = refs[-1]
        k = pl.program_id(2)

        @pl.when(k == 0)
        def _():
            acc[...] = jnp.zeros_like(acc)

        av = a_ref[...]
        if av.dtype != BF16:
            av = av.astype(BF16)
        acc[...] += _dot(av, b_ref[...], dims)

        @pl.when(k == nk - 1)
        def _():
            if epilogue is None:
                outs[0][...] = acc[...].astype(outs[0].dtype)
            else:
                epilogue(acc, ex, outs)

    scratch = [pltpu.VMEM(acc_shape, F32)]
    params = _params(("parallel", "parallel", "arbitrary"), vmem_mib)
    if prefetch is None:
        return pl.pallas_call(
            body, name=name, grid=grid, out_shape=list(out_shapes),
            in_specs=[a_spec, b_spec, *extra_specs], out_specs=list(out_specs),
            scratch_shapes=scratch, compiler_params=params, input_output_aliases=aliases,
        )(a, b, *extra)
    return pl.pallas_call(
        body, name=name, out_shape=list(out_shapes),
        grid_spec=pltpu.PrefetchScalarGridSpec(
            num_scalar_prefetch=1, grid=grid, in_specs=[a_spec, b_spec, *extra_specs],
            out_specs=list(out_specs), scratch_shapes=scratch),
        compiler_params=params, input_output_aliases=aliases,
    )(prefetch, a, b, *extra)


def _tile(n, want):
    t = min(n, want)
    while n % t:
        t //= 2
    return t


def _rel_buckets(dil):
    order = BLOCK_ORDER[dil]
    qi = jnp.asarray(HEAD + order)
    kj = jnp.asarray(np.concatenate([order, HEAD + order]))
    delta = qi[:, None] - kj[None, :]
    band = (delta >= 0) & (delta <= HEAD)
    dist = jnp.clip(delta, 0, None) * dil
    max_exact = NUM_BUCKETS // 2
    dd = jnp.maximum(dist, 1).astype(F32)
    large = max_exact + (jnp.log(dd / max_exact) / math.log(MAX_DISTANCE / max_exact)
                         * (NUM_BUCKETS - max_exact)).astype(jnp.int32)
    large = jnp.minimum(large, NUM_BUCKETS - 1)
    bucket = jnp.where(dist < max_exact, dist, large)
    return jnp.where(band, bucket, -1).astype(jnp.int32)


def _bias_build(rel_bias, buckets, n_heads):
    nd = buckets.shape[0]

    def body(rb_ref, bk_ref, o_ref):
        for c in range(nd):
            def per_head(h, carry, c=c):
                bk = bk_ref[c]
                acc = jnp.where(bk < 0, NEG_INF, 0.0).astype(F32)
                for b in range(NUM_BUCKETS):
                    acc = jnp.where(bk == b, rb_ref[b, h], acc)
                o_ref[c, h] = acc
                return carry

            lax.fori_loop(0, n_heads, per_head, 0)

    return pl.pallas_call(
        body, name="bias_build",
        out_shape=jax.ShapeDtypeStruct((nd, n_heads, HEAD, 2 * HEAD), F32),
        in_specs=[pl.BlockSpec(memory_space=pltpu.SMEM), pl.BlockSpec(memory_space=pltpu.VMEM)],
        out_specs=pl.BlockSpec(memory_space=pltpu.VMEM),
    )(rel_bias, buckets)


def _bias_grad(ds_all, buckets, n_heads):
    nd = buckets.shape[0]
    pairs = HEAD * 2 * HEAD

    def body(ds_ref, bk_ref, o_ref):
        rows = lax.broadcasted_iota(jnp.int32, (NUM_BUCKETS, pairs), 0)
        tot = jnp.zeros((n_heads, NUM_BUCKETS), F32)
        for c in range(nd):
            onehot = (rows == bk_ref[c]).astype(BF16)
            ds = ds_ref[c]
            hi = ds.astype(BF16)
            lo = (ds - hi.astype(F32)).astype(BF16)
            tot = tot + _dot(hi, onehot, NT_DIMS) + _dot(lo, onehot, NT_DIMS)
        o_ref[...] = tot

    out = pl.pallas_call(
        body, name="bias_grad",
        out_shape=jax.ShapeDtypeStruct((n_heads, NUM_BUCKETS), F32),
        in_specs=[pl.BlockSpec(memory_space=pltpu.VMEM), pl.BlockSpec(memory_space=pltpu.VMEM)],
        out_specs=pl.BlockSpec(memory_space=pltpu.VMEM),
        compiler_params=pltpu.CompilerParams(vmem_limit_bytes=40 * MIB),
    )(ds_all.reshape(nd, n_heads, pairs), buckets.reshape(nd, 1, pairs))
    return out.T


def _qkv_prep(proj, g_q, g_k, w, dep=None):
    s = proj.shape[0]
    n_heads = w // HEAD
    tm = HEAD

    def body(q_ref, k_ref, gq_ref, gk_ref, qn_ref, kn_ref):
        gq = gq_ref[...] * (HEAD ** -0.5)
        gk = gk_ref[...]
        for h in range(n_heads):
            sl = slice(h * HEAD, (h + 1) * HEAD)
            q = q_ref[:, sl]
            k = k_ref[:, sl]
            qn_ref[:, sl] = q * _rms(q) * gq
            kn_ref[:, sl] = k * _rms(k) * gk

    seg = lambda j: pl.BlockSpec((tm, w), lambda i, j=j: (i, j))
    vec = pl.BlockSpec((1, HEAD), lambda i: (0, 0))
    out = pl.BlockSpec((tm, w), lambda i: (i, 0))
    in_specs = [seg(3), seg(4), vec, vec]
    operands = [proj, proj, g_q, g_k]
    if dep is not None:
        body = _drop_arg(body, len(operands))
        in_specs.append(DEP_SPEC)
        operands.append(dep)
    return pl.pallas_call(
        body, name="qkv_prep", grid=(s // tm,),
        out_shape=[jax.ShapeDtypeStruct((s, w), F32)] * 2,
        in_specs=in_specs, out_specs=[out, out],
        compiler_params=_params(("arbitrary",), 40),
    )(*operands)


class _BlockView:
    def __init__(self, s, dil):
        assert s % (HEAD * dil) == 0 and dil in BLOCK_ORDER
        self.nb = s // (HEAD * dil)
        if dil == 1:
            self.lead, self.block = (s,), (HEAD,)
            self.index = lambda r, n: (n,)
        elif dil == 4:
            self.lead, self.block = (s // 512, 4, 4, 4, SUB), (None, 4, 4, None, SUB)
            self.index = lambda r, n: (n, 0, 0, r, 0)
        else:
            self.lead, self.block = (s // 2048, 16, 16, SUB), (None, 16, None, SUB)
            self.index = lambda r, n: (n, 0, r, 0)

    def view(self, t):
        return t.reshape(self.lead + (t.shape[-1],))

    def spec(self, width, block_of, column=0):
        return pl.BlockSpec(self.block + (width,), lambda r, n: self.index(r, block_of(r, n)) + (column,))


def _rows(ref, lanes=slice(None)):
    v = ref[(slice(None),) * (len(ref.shape) - 1) + (lanes,)]
    return v.reshape(HEAD, v.shape[-1])


def _set_rows(ref, lanes, value):
    ref[(slice(None),) * (len(ref.shape) - 1) + (lanes,)] = value.reshape(ref.shape[:-1] + (value.shape[-1],))


V_SEGMENT = 5


def _attn_fwd(qn, kn, proj, bias, dil, name, dep=None):
    s, w = qn.shape
    n_heads = w // HEAD
    bv = _BlockView(s, dil)

    def body(q_ref, kc_ref, vc_ref, bias_ref, o_ref, lse_ref, s_scr, e_scr, lse_scr, inv_scr, k_prev, v_prev):
        n = pl.program_id(1)
        heads = [slice(h * HEAD, (h + 1) * HEAD) for h in range(n_heads)]
        lse_scr[...] = jnp.zeros_like(lse_scr)

        @pl.when(n == 0)
        def _():
            k_prev[...] = jnp.zeros_like(k_prev)
            v_prev[...] = jnp.zeros_like(v_prev)

        for h, sl in enumerate(heads):
            q = _rows(q_ref, sl).astype(BF16)
            s_p = _dot(q, k_prev[:, sl], NT_DIMS) + bias_ref[h, :, :HEAD]
            s_scr[h, :, :HEAD] = jnp.where(n > 0, s_p, NEG_INF)
            s_scr[h, :, HEAD:] = _dot(q, _rows(kc_ref, sl).astype(BF16), NT_DIMS) + bias_ref[h, :, HEAD:]
        for h in range(n_heads):
            sc = s_scr[h]
            m = jnp.max(sc, axis=-1, keepdims=True)
            e = jnp.exp(sc - m)
            den = jnp.sum(e, axis=-1, keepdims=True)
            e_scr[h] = e.astype(BF16)
            lse_scr[:, h:h + 1] = m + jnp.log(den)
            inv_scr[:, h:h + 1] = 1.0 / den
        for h, sl in enumerate(heads):
            v_cur = _rows(vc_ref, sl).astype(BF16)
            o = _dot(e_scr[h, :, :HEAD], v_prev[:, sl]) + _dot(e_scr[h, :, HEAD:], v_cur)
            _set_rows(o_ref, sl, o * inv_scr[:, h:h + 1])
            v_prev[:, sl] = v_cur
            k_prev[:, sl] = _rows(kc_ref, sl).astype(BF16)
        _set_rows(lse_ref, slice(None), lse_scr[...])

    cur = bv.spec(w, lambda r, n: n)
    in_specs = [cur, cur, bv.spec(w, lambda r, n: n, V_SEGMENT),
                pl.BlockSpec((n_heads, HEAD, 2 * HEAD), lambda r, n: (0, 0, 0))]
    operands = [bv.view(qn), bv.view(kn), bv.view(proj), bias]
    if dep is not None:
        body = _drop_arg(body, len(operands))
        in_specs.append(DEP_SPEC)
        operands.append(dep)
    o, lse = pl.pallas_call(
        body, name=name, grid=(dil, bv.nb),
        out_shape=[jax.ShapeDtypeStruct(bv.lead + (w,), F32), jax.ShapeDtypeStruct(bv.lead + (HEAD,), F32)],
        in_specs=in_specs,
        out_specs=[cur, bv.spec(HEAD, lambda r, n: n)],
        scratch_shapes=[pltpu.VMEM((n_heads, HEAD, 2 * HEAD), F32), pltpu.VMEM((n_heads, HEAD, 2 * HEAD), BF16),
                        pltpu.VMEM((HEAD, HEAD), F32), pltpu.VMEM((HEAD, HEAD), F32),
                        pltpu.VMEM((HEAD, w), BF16), pltpu.VMEM((HEAD, w), BF16)],
        compiler_params=_params(("arbitrary", "arbitrary"), 48),
    )(*operands)
    return o.reshape(s, w), lse.reshape(s, HEAD)


def _attn_bwd(qn, kn, proj, dyb, lse, delta, bias, dil, name, running=None, dep=None):
    s, w = qn.shape
    n_heads = w // HEAD
    bv = _BlockView(s, dil)
    nb = bv.nb

    n_run = 0 if running is None else 3

    def body(q_ref, kc_ref, kp_ref, vc_ref, vp_ref, dy_ref, lse_ref, dl_ref, bias_ref, *rest):
        so_far = rest[:n_run]
        dq_ref, dk_ref, dv_ref, ds_ref, carry_k, carry_v, s_scr, dp_scr, p_scr, dsb_scr, k_cur, v_cur = rest[n_run:]
        base = (lambda i, sl: _rows(so_far[i], sl)) if n_run else (lambda i, sl: 0.0)
        r = pl.program_id(0)
        step = pl.program_id(1)
        blk = nb - 1 - step

        @pl.when((r == 0) & (step == 0))
        def _():
            ds_ref[...] = jnp.zeros_like(ds_ref)

        @pl.when(step == 0)
        def _():
            carry_k[...] = jnp.zeros_like(carry_k)
            carry_v[...] = jnp.zeros_like(carry_v)
            k_cur[...] = _rows(kc_ref).astype(BF16)
            v_cur[...] = _rows(vc_ref).astype(BF16)

        heads = [slice(h * HEAD, (h + 1) * HEAD) for h in range(n_heads)]
        tots = _rows(lse_ref)
        dls = _rows(dl_ref)
        for h, sl in enumerate(heads):
            q, dy = _rows(q_ref, sl).astype(BF16), _rows(dy_ref, sl).astype(BF16)
            kp, kc = _rows(kp_ref, sl).astype(BF16), k_cur[:, sl]
            vp, vc = _rows(vp_ref, sl).astype(BF16), v_cur[:, sl]
            s_p = _dot(q, kp, NT_DIMS) + bias_ref[h, :, :HEAD]
            s_scr[h, :, :HEAD] = jnp.where(blk > 0, s_p, NEG_INF)
            s_scr[h, :, HEAD:] = _dot(q, kc, NT_DIMS) + bias_ref[h, :, HEAD:]
            dp_scr[h, :, :HEAD] = _dot(dy, vp, NT_DIMS)
            dp_scr[h, :, HEAD:] = _dot(dy, vc, NT_DIMS)
        for h in range(n_heads):
            prob = jnp.exp(s_scr[h] - tots[:, h:h + 1])
            ds = prob * (dp_scr[h] - dls[:, h:h + 1])
            ds_ref[h] += ds
            p_scr[h] = prob.astype(BF16)
            dsb_scr[h] = ds.astype(BF16)
        for h, sl in enumerate(heads):
            q, dy = _rows(q_ref, sl).astype(BF16), _rows(dy_ref, sl).astype(BF16)
            kp, kc = _rows(kp_ref, sl).astype(BF16), k_cur[:, sl]
            ds_pb, ds_cb = dsb_scr[h, :, :HEAD], dsb_scr[h, :, HEAD:]
            _set_rows(dq_ref, sl, _dot(ds_pb, kp) + _dot(ds_cb, kc) + base(0, sl))
            _set_rows(dk_ref, sl, _dot(ds_cb, q, TN_DIMS) + carry_k[:, sl] + base(1, sl))
            carry_k[:, sl] = _dot(ds_pb, q, TN_DIMS)
            _set_rows(dv_ref, sl, _dot(p_scr[h, :, HEAD:], dy, TN_DIMS) + carry_v[:, sl] + base(2, sl))
            carry_v[:, sl] = _dot(p_scr[h, :, :HEAD], dy, TN_DIMS)
            k_cur[:, sl] = kp
            v_cur[:, sl] = _rows(vp_ref, sl).astype(BF16)

    cur = bv.spec(w, lambda r, n: nb - 1 - n)
    last = bv.spec(w, lambda r, n: nb - 1)
    prev = bv.spec(w, lambda r, n: jnp.maximum(nb - 2 - n, 0))
    stat = bv.spec(HEAD, lambda r, n: nb - 1 - n)
    whole = pl.BlockSpec((n_heads, HEAD, 2 * HEAD), lambda r, n: (0, 0, 0))
    big = jax.ShapeDtypeStruct(bv.lead + (w,), F32)
    v_last = bv.spec(w, lambda r, n: nb - 1, V_SEGMENT)
    v_prev = bv.spec(w, lambda r, n: jnp.maximum(nb - 2 - n, 0), V_SEGMENT)
    in_specs = [cur, last, prev, v_last, v_prev, cur, stat, stat, whole]
    operands = [bv.view(qn), bv.view(kn), bv.view(kn), bv.view(proj), bv.view(proj), bv.view(dyb), bv.view(lse),
                bv.view(delta), bias]
    aliases = {}
    if running is not None:
        aliases = {len(operands) + i: i for i in range(3)}
        in_specs += [cur] * 3
        operands += [bv.view(t) for t in running]
    if dep is not None:
        body = _drop_arg(body, len(operands))
        in_specs.append(DEP_SPEC)
        operands.append(dep)
    dq, dk, dv, ds = pl.pallas_call(
        body, name=name, grid=(dil, nb),
        out_shape=[big, big, big, jax.ShapeDtypeStruct((n_heads, HEAD, 2 * HEAD), F32)],
        in_specs=in_specs, out_specs=[cur, cur, cur, whole], input_output_aliases=aliases,
        scratch_shapes=[pltpu.VMEM((HEAD, w), F32), pltpu.VMEM((HEAD, w), F32),
                        pltpu.VMEM((n_heads, HEAD, 2 * HEAD), F32), pltpu.VMEM((n_heads, HEAD, 2 * HEAD), F32),
                        pltpu.VMEM((n_heads, HEAD, 2 * HEAD), BF16), pltpu.VMEM((n_heads, HEAD, 2 * HEAD), BF16),
                        pltpu.VMEM((HEAD, w), BF16), pltpu.VMEM((HEAD, w), BF16)],
        compiler_params=_params(("arbitrary", "arbitrary"), 56),
    )(*operands)
    return dq.reshape(s, w), dk.reshape(s, w), dv.reshape(s, w), ds


def _qkv_bwd(dproj, proj, dq, dk, dv, g_q, g_k, w):
    s = proj.shape[0]
    n_heads = w // HEAD
    tm = HEAD

    def body(dproj_hbm, q_ref, k_ref, gq_ref, gk_ref, dq_ref, dk_ref, dv_ref, out_ref, dgq_ref, dgk_ref):
        i = pl.program_id(0)
        gq = gq_ref[...] * (HEAD ** -0.5)
        gk = gk_ref[...]
        acc_q = jnp.zeros((1, HEAD), F32)
        acc_k = jnp.zeros((1, HEAD), F32)
        for h in range(n_heads):
            sl = slice(h * HEAD, (h + 1) * HEAD)
            q, k = q_ref[:, sl], k_ref[:, sl]
            dqn, dkn = dq_ref[:, sl], dk_ref[:, sl]
            rq, rk = _rms(q), _rms(k)
            out_ref[:, h * HEAD:(h + 1) * HEAD] = _rms_bwd(dqn, q, rq, gq).astype(BF16)
            out_ref[:, w + h * HEAD:w + (h + 1) * HEAD] = _rms_bwd(dkn, k, rk, gk).astype(BF16)
            acc_q += jnp.sum(dqn * q * rq, axis=0, keepdims=True)
            acc_k += jnp.sum(dkn * k * rk, axis=0, keepdims=True)
        out_ref[:, 2 * w:] = dv_ref[...].astype(BF16)

        @pl.when(i == 0)
        def _():
            dgq_ref[...] = jnp.zeros_like(dgq_ref)
            dgk_ref[...] = jnp.zeros_like(dgk_ref)

        dgq_ref[...] += acc_q * (HEAD ** -0.5)
        dgk_ref[...] += acc_k

    seg = lambda j: pl.BlockSpec((tm, w), lambda i, j=j: (i, j))
    vec = pl.BlockSpec((1, HEAD), lambda i: (0, 0))
    row = pl.BlockSpec((tm, w), lambda i: (i, 0))
    return pl.pallas_call(
        body, name="qkv_bwd", grid=(s // tm,),
        out_shape=[jax.ShapeDtypeStruct(dproj.shape, BF16),
                   jax.ShapeDtypeStruct((1, HEAD), F32), jax.ShapeDtypeStruct((1, HEAD), F32)],
        in_specs=[pl.BlockSpec(memory_space=pl.ANY), seg(3), seg(4), vec, vec] + [row] * 3,
        out_specs=[pl.BlockSpec((tm, 3 * w), lambda i: (i, 1)), vec, vec],
        input_output_aliases={0: 0},
        compiler_params=_params(("arbitrary",), 48),
    )(dproj, proj, proj, g_q, g_k, dq, dk, dv)


def _mixer_a(u, gv, ws_ref, bst_ref, lng, lnb, z_scr, ln_scr):
    n_groups = u.shape[1] // HEAD
    mu = jnp.mean(gv, axis=-1, keepdims=True)
    xc = gv - mu
    rs = lax.rsqrt(jnp.mean(xc * xc, axis=-1, keepdims=True) + EPS)
    xhat = xc * rs
    ln_scr[...] = (xhat * lng + lnb).astype(BF16)
    causal = _causal_mask()
    for g in range(n_groups):
        sl = slice(g * HEAD, (g + 1) * HEAD)
        wm = jnp.where(causal, ws_ref[g], 0.0).astype(BF16)
        z_scr[:, sl] = _dot(wm, ln_scr[:, sl]) + bst_ref[:, g:g + 1]
    return u, xhat, rs


def _causal_mask():
    token = lambda r: 16 * (r % SUB) + r // SUB
    row = lax.broadcasted_iota(jnp.int32, (HEAD, HEAD), 0)
    col = lax.broadcasted_iota(jnp.int32, (HEAD, HEAD), 1)
    return token(col) <= token(row)


def _merge_b(o_refs, lse_refs, yb_scr):
    n_heads = yb_scr.shape[1] // HEAD
    lses = [t[...] for t in lse_refs]
    m = jnp.maximum(jnp.maximum(lses[0], lses[1]), lses[2])
    tot = m + jnp.log(sum(jnp.exp(t - m) for t in lses))
    alphas = [jnp.exp(t - tot) for t in lses]
    for h in range(n_heads):
        sl = slice(h * HEAD, (h + 1) * HEAD)
        yb_scr[:, sl] = sum(a[:, h:h + 1] * o[:, sl].astype(F32) for a, o in zip(alphas, o_refs))
    return tot


def _mix_fwd(proj, outs, lses, w_s, bst, ln_g, ln_b, g_a, g_b, w, dep=None):
    s = proj.shape[0]
    n_groups = w // HEAD

    def body(au_ref, av_ref, az_ref, bz_ref, o1, o2, o3, l1, l2, l3, ws_ref, bst_ref,
             lng_ref, lnb_ref, ga_ref, gb_ref, p_ref, y_ref, z_scr, ln_scr, yb_scr):
        u, _, _ = _mixer_a(_gelu(au_ref[...]), _gelu(av_ref[...]), ws_ref, bst_ref, lng_ref[...], lnb_ref[...],
                           z_scr, ln_scr)
        ya = u * z_scr[...]
        silu_a, _ = _silu_and_grad(az_ref[...])
        perm = p_ref[...]
        y_ref[:, :w] = _dot(perm, (ya * _rms(ya) * ga_ref[...] * silu_a).astype(BF16), TN_DIMS).astype(BF16)
        _merge_b((o1, o2, o3), (l1, l2, l3), yb_scr)
        yb = yb_scr[...]
        silu_b, _ = _silu_and_grad(bz_ref[...])
        y_ref[:, w:] = _dot(perm, (yb * _rms(yb) * gb_ref[...] * silu_b).astype(BF16), TN_DIMS).astype(BF16)

    seg = lambda j: pl.BlockSpec((HEAD, w), lambda i, j=j: (i, j))
    row = pl.BlockSpec((HEAD, w), lambda i: (i, 0))
    stat = pl.BlockSpec((HEAD, HEAD), lambda i: (i, 0))
    vec = pl.BlockSpec((1, w), lambda i: (0, 0))
    in_specs = [seg(0), seg(1), seg(2), seg(6), row, row, row, stat, stat, stat,
                pl.BlockSpec((n_groups, HEAD, HEAD), lambda i: (0, 0, 0)),
                pl.BlockSpec((HEAD, n_groups), lambda i: (0, 0)), vec, vec, vec, vec,
                pl.BlockSpec((HEAD, HEAD), lambda i: (0, 0))]
    operands = [proj, proj, proj, proj, *outs, *lses, w_s, bst, ln_g, ln_b, g_a, g_b, _chunk_perm()]
    if dep is not None:
        body = _drop_arg(body, len(operands))
        in_specs.append(DEP_SPEC)
        operands.append(dep)
    return pl.pallas_call(
        body, name="mix_fwd", grid=(s // HEAD,),
        out_shape=jax.ShapeDtypeStruct((s, 2 * w), BF16), in_specs=in_specs,
        out_specs=pl.BlockSpec((HEAD, 2 * w), lambda i: (i, 0)),
        scratch_shapes=[pltpu.VMEM((HEAD, w), F32), pltpu.VMEM((HEAD, w), BF16), pltpu.VMEM((HEAD, w), F32)],
        compiler_params=_params(("arbitrary",), 48),
    )(*operands)


def _mix_bwd(proj, dy, outs, lses, w_s, bst, ln_g, ln_b, g_a, g_b, w):
    s = proj.shape[0]
    n_groups = w // HEAD

    def body(au_ref, av_ref, az_ref, bz_ref, dy_ref, o1, o2, o3, l1, l2, l3, ws_ref, bst_ref,
             lng_ref, lnb_ref, ga_ref, gb_ref,
             dproj_ref, dyb_ref, tot_ref, dl_ref, dws_ref, dbst_ref, dlng_ref, dlnb_ref, dga_ref, dgb_ref,
             z_scr, ln_scr, yb_scr, dz_scr, dln_scr):
        i = pl.program_id(0)

        @pl.when(i == 0)
        def _():
            for t in (dws_ref, dbst_ref, dlng_ref, dlnb_ref, dga_ref, dgb_ref):
                t[...] = jnp.zeros_like(t)

        az = az_ref[...]
        lng = lng_ref[...]
        u, du_dau = _gelu_and_grad(au_ref[...])
        gv, dgv_dav = _gelu_and_grad(av_ref[...])
        u, xhat, rs = _mixer_a(u, gv, ws_ref, bst_ref, lng, lnb_ref[...], z_scr, ln_scr)
        z = z_scr[...]
        ya = u * z
        ra = _rms(ya)
        silu_a, dsilu_a = _silu_and_grad(az)
        dya_all = dy_ref[:, :w]
        na = ya * ra * ga_ref[...]
        dna = dya_all * silu_a
        dproj_ref[:, 2 * w:3 * w] = (dya_all * na * dsilu_a).astype(BF16)
        dga_ref[...] += jnp.sum(dna * ya * ra, axis=0, keepdims=True)
        dya = _rms_bwd(dna, ya, ra, ga_ref[...])
        dproj_ref[:, :w] = (dya * z * du_dau).astype(BF16)
        dz_scr[...] = (dya * u).astype(BF16)

        causal = _causal_mask()
        for g in range(n_groups):
            sl = slice(g * HEAD, (g + 1) * HEAD)
            wm = jnp.where(causal, ws_ref[g], 0.0).astype(BF16)
            dz = dz_scr[:, sl]
            dln_scr[:, sl] = _dot(wm, dz, TN_DIMS)
            dws_ref[g] += jnp.where(causal, _dot(dz, ln_scr[:, sl], NT_DIMS), 0.0)
            dbst_ref[:, g:g + 1] += jnp.sum(dz.astype(F32), axis=-1, keepdims=True)
        dln = dln_scr[...]
        dlng_ref[...] += jnp.sum(dln * xhat, axis=0, keepdims=True)
        dlnb_ref[...] += jnp.sum(dln, axis=0, keepdims=True)
        gy = dln * lng
        dgv = rs * (gy - jnp.mean(gy, axis=-1, keepdims=True)
                    - xhat * jnp.mean(gy * xhat, axis=-1, keepdims=True))
        dproj_ref[:, w:2 * w] = (dgv * dgv_dav).astype(BF16)
        dproj_ref[:, 3 * w:6 * w] = jnp.zeros((HEAD, 3 * w), BF16)

        tot_ref[...] = _merge_b((o1, o2, o3), (l1, l2, l3), yb_scr)
        yb = yb_scr[...]
        rb = _rms(yb)
        bz = bz_ref[...]
        silu_b, dsilu_b = _silu_and_grad(bz)
        dyb_all = dy_ref[:, w:]
        dnb = dyb_all * silu_b
        dproj_ref[:, 6 * w:] = (dyb_all * yb * rb * gb_ref[...] * dsilu_b).astype(BF16)
        dgb_ref[...] += jnp.sum(dnb * yb * rb, axis=0, keepdims=True)
        dyb = _rms_bwd(dnb, yb, rb, gb_ref[...])
        dyb_ref[...] = dyb
        prod = dyb * yb
        dl_ref[...] = _lane_pick(
            [jnp.sum(prod[:, h * HEAD:(h + 1) * HEAD], axis=-1, keepdims=True) for h in range(n_groups)], HEAD)

    seg = lambda j: pl.BlockSpec((HEAD, w), lambda i, j=j: (i, j))
    row_w = pl.BlockSpec((HEAD, w), lambda i: (i, 0))
    stat = pl.BlockSpec((HEAD, HEAD), lambda i: (i, 0))
    vec = pl.BlockSpec((1, w), lambda i: (0, 0))
    ws_spec = pl.BlockSpec((n_groups, HEAD, HEAD), lambda i: (0, 0, 0))
    bst_spec = pl.BlockSpec((HEAD, n_groups), lambda i: (0, 0))
    vec_shape = jax.ShapeDtypeStruct((1, w), F32)
    return pl.pallas_call(
        body, name="mix_bwd", grid=(s // HEAD,),
        out_shape=[jax.ShapeDtypeStruct((s, N_SEG * w), BF16), jax.ShapeDtypeStruct((s, w), F32),
                   jax.ShapeDtypeStruct((s, HEAD), F32), jax.ShapeDtypeStruct((s, HEAD), F32),
                   jax.ShapeDtypeStruct((n_groups, HEAD, HEAD), F32), jax.ShapeDtypeStruct((HEAD, n_groups), F32),
                   vec_shape, vec_shape, vec_shape, vec_shape],
        in_specs=[seg(0), seg(1), seg(2), seg(6), pl.BlockSpec((HEAD, 2 * w), lambda i: (i, 0)),
                  row_w, row_w, row_w, stat, stat, stat, ws_spec, bst_spec, vec, vec, vec, vec],
        out_specs=[pl.BlockSpec((HEAD, N_SEG * w), lambda i: (i, 0)), row_w, stat, stat,
                   ws_spec, bst_spec, vec, vec, vec, vec],
        scratch_shapes=[pltpu.VMEM((HEAD, w), F32), pltpu.VMEM((HEAD, w), BF16), pltpu.VMEM((HEAD, w), F32),
                        pltpu.VMEM((HEAD, w), BF16), pltpu.VMEM((HEAD, w), F32)],
        compiler_params=_params(("arbitrary",), 56),
    )(proj, proj, proj, proj, dy, *outs, *lses, w_s, bst, ln_g, ln_b, g_a, g_b)


def _local_step(x, p, tgt, small, wg, ex, while_last_travels=None):
    s, d = x.shape
    n, _, c_in = wg.buffers["w_in"].shape
    assert n == N_DEV
    d_in = n * c_in
    w = d_in // N_SEG
    n_heads = w // HEAD
    p_dim, c_up = wg.buffers["w_ple_up"].shape[1:]
    assert s % (HEAD * DILATIONS[-1]) == 0 and w % HEAD == 0 and d == n * c_up == 2 * w

    near, far = (2, 4), (6,)
    wg.start("gather_in_pair", ["w_in"], (1,))
    token = wg.start("gather_in_near", ["w_in"], near)
    rest = ["w_out", "w_ple_gate", "w_ple_up"]

    hn = _rmsnorm_fwd(x, small["g_pre"], "pre_norm", True, dep=token)
    tm, tk = _tile(s, 1024), _tile(d, 2048)

    def in_proj(shards, name, carry, dep=None):
        return _matmul(
            hn, wg.buffers["w_in"], name=name, grid=(s // tm, len(shards), d // tk), dims=NN_DIMS,
            prefetch=jnp.stack(shards).astype(jnp.int32),
            a_spec=pl.BlockSpec((tm, tk), lambda i, j, k, sh: (i, k)),
            b_spec=pl.BlockSpec((None, tk, c_in), lambda i, j, k, sh: (sh[j], k, 0)),
            acc_shape=(tm, c_in), out_shapes=[jax.ShapeDtypeStruct((s, d_in), F32)],
            out_specs=[pl.BlockSpec((tm, c_in), lambda i, j, k, sh: (i, sh[j]))], carry=carry, dep=dep,
            vmem_mib=56)[0]

    me = wg.me
    wg.arrived("gather_in_pair", [hn])
    proj = in_proj([me, me ^ 1], "in_proj_pair", None)
    buckets = jnp.stack([_rel_buckets(dil) for dil in DILATIONS])
    bias = _bias_build(small["rel_bias"], buckets, n_heads)
    ahead = {"near": [bias] + [wg.buffers[k] for k in rest], "far": []}
    for tag, chips in (("near", near), ("far", far)):
        token = wg.forward("gather_in_" + tag, [proj] + ahead[tag])
        if tag == "near":
            token = wg.relay_start("gather_in_far", ["w_in"], after=[token])
        else:
            token = wg.start("gather_rest", rest, (1,) + near, after=[token])
        proj = in_proj([me ^ k for k in chips], "in_proj_" + tag, proj, dep=token)
        wg.forwarded("gather_in_" + tag, [proj])
        proj = in_proj([me ^ k ^ 1 for k in chips], "in_proj_%s_forwarded" % tag, proj)
    win_g = wg.buffers["w_in"]

    qn, kn = _qkv_prep(proj, small["g_q"], small["g_k"], w)
    token = wg.forward("gather_rest", [qn])
    token = wg.relay_start("gather_rest_far", rest, after=[token])
    outs, lses = [], []
    for c, dil in enumerate(DILATIONS):
        o, l = _attn_fwd(qn, kn, proj, bias[c], dil, "attn_fwd_d%d" % dil, dep=token)
        outs.append(o)
        lses.append(l)
    token = wg.forward("gather_rest_far", outs)

    ws_p = small["w_s"][:, CHUNK_ORDER][:, :, CHUNK_ORDER]
    bst = small["b_s"].T[CHUNK_ORDER]
    mix_args = (outs, lses, ws_p, bst, small["ln_v_g"], small["ln_v_b"], small["g_out_a"], small["g_out_b"], w)
    y = _mix_fwd(proj, *mix_args, dep=token)
    wg.forwarded("gather_rest", [y])
    wg.forwarded("gather_rest_far", [y])
    wout_g, wgate_g, wup_g = (wg.buffers[k] for k in rest)
    wout_f = wout_g.reshape(2 * w, d)
    wgate_f = wgate_g.reshape(d, d)

    tn = _tile(d, 1024)
    tk2 = _tile(2 * w, 2048)

    def resid_epilogue(acc, ex, outs_):
        outs_[0][...] = ex[0][...] + acc[...]

    h = _matmul(
        y, wout_f, name="out_proj", grid=(s // tm, d // tn, (2 * w) // tk2), dims=NN_DIMS,
        a_spec=pl.BlockSpec((tm, tk2), lambda i, j, k: (i, k)),
        b_spec=pl.BlockSpec((tk2, tn), lambda i, j, k: (k, j)),
        acc_shape=(tm, tn), out_shapes=[jax.ShapeDtypeStruct((s, d), F32)],
        out_specs=[pl.BlockSpec((tm, tn), lambda i, j, k: (i, j))],
        extra=(x,), extra_specs=(pl.BlockSpec((tm, tn), lambda i, j, k: (i, j)),),
        epilogue=resid_epilogue)[0]

    hn2 = _rmsnorm_fwd(h, small["g_ple"], "ple_norm", False)

    tmg = _tile(s, 512)

    def ple_epilogue(acc, ex, outs_):
        h_ref, p_ref, wup_ref, tgt_ref = ex
        dout_ref, dpre_ref, dup_ref, loss_ref = outs_
        gate = jax.nn.sigmoid(acc[...])
        up = _dot(p_ref[...].astype(BF16), wup_ref[...])
        err = h_ref[...] + gate * up - tgt_ref[...]
        dout = err * (1.0 / d)
        dout_ref[...] = dout
        dpre_ref[...] = (dout * up * gate * (1.0 - gate)).astype(BF16)
        dup_ref[...] = (dout * gate).astype(BF16)
        part = 0.5 * jnp.sum(err * err) * (1.0 / d)
        rr = lax.broadcasted_iota(jnp.int32, (8, HEAD), 0)
        cc = lax.broadcasted_iota(jnp.int32, (8, HEAD), 1)
        loss_ref[...] = jnp.where((rr == 0) & (cc == 0), part, 0.0)

    tile_ij = pl.BlockSpec((tmg, c_up), lambda i, j, k: (i, j))
    dout, dpre, dup, loss_parts = _matmul(
        hn2, wgate_f, name="ple_gate", grid=(s // tmg, n, 1), dims=NN_DIMS,
        a_spec=pl.BlockSpec((tmg, d), lambda i, j, k: (i, 0)),
        b_spec=pl.BlockSpec((d, c_up), lambda i, j, k: (0, j)),
        acc_shape=(tmg, c_up),
        out_shapes=[jax.ShapeDtypeStruct((s, d), F32), jax.ShapeDtypeStruct((s, d), BF16),
                    jax.ShapeDtypeStruct((s, d), BF16), jax.ShapeDtypeStruct((s // tmg * 8, n * HEAD), F32)],
        out_specs=[tile_ij, tile_ij, tile_ij, pl.BlockSpec((8, HEAD), lambda i, j, k: (i, j))],
        extra=(h, p, wup_g, tgt),
        extra_specs=(tile_ij, pl.BlockSpec((tmg, p_dim), lambda i, j, k: (i, 0)),
                     pl.BlockSpec((None, p_dim, c_up), lambda i, j, k: (j, 0, 0)), tile_ij),
        epilogue=ple_epilogue)
    loss = jnp.sum(loss_parts)

    tks = _tile(s, 2048)
    g_wup = _matmul(
        p, dup, name="grad_w_up", grid=(1, n, s // tks), dims=TN_DIMS,
        a_spec=pl.BlockSpec((tks, p_dim), lambda i, j, k: (k, 0)),
        b_spec=pl.BlockSpec((tks, c_up), lambda i, j, k: (k, j)),
        acc_shape=(p_dim, c_up), out_shapes=[jax.ShapeDtypeStruct((n, p_dim, c_up), BF16)],
        out_specs=[pl.BlockSpec((None, p_dim, c_up), lambda i, j, k: (j, 0, 0))])[0]

    def tn_matmul(a, b, name):
        m_, n_ = a.shape[1], b.shape[1]
        bm, bn = _tile(m_, 1024), _tile(n_, 1024)
        return _matmul(
            a, b, name=name, grid=(m_ // bm, n_ // bn, s // tks), dims=TN_DIMS,
            a_spec=pl.BlockSpec((tks, bm), lambda i, j, k: (k, i)),
            b_spec=pl.BlockSpec((tks, bn), lambda i, j, k: (k, j)),
            acc_shape=(bm, bn), out_shapes=[jax.ShapeDtypeStruct((m_, n_), BF16)],
            out_specs=[pl.BlockSpec((bm, bn), lambda i, j, k: (i, j))])[0]

    def nt_matmul(a, b, name, out_dtype, dep=None):
        k_, n_ = a.shape[1], b.shape[0]
        bm, bn, bk = _tile(s, 1024), _tile(n_, 1024), _tile(k_, 2048)
        return _matmul(
            a, b, name=name, grid=(s // bm, n_ // bn, k_ // bk), dims=NT_DIMS,
            a_spec=pl.BlockSpec((bm, bk), lambda i, j, k: (i, k)),
            b_spec=pl.BlockSpec((bn, bk), lambda i, j, k: (j, k)),
            acc_shape=(bm, bn), out_shapes=[jax.ShapeDtypeStruct((s, n_), out_dtype)],
            out_specs=[pl.BlockSpec((bm, bn), lambda i, j, k: (i, j))], dep=dep)[0]

    by_core = lambda g: g.reshape((N_CHIP, 2) + g.shape[-2:])
    g_wgate = tn_matmul(hn2, dpre, "grad_w_gate").reshape(wgate_g.shape)
    dhn2 = nt_matmul(dpre, wgate_f, "ple_gate_bwd", F32)
    dh, dh_b, dh_bp, dg_ple = _rmsnorm_bwd(dhn2, h, small["g_ple"], dout, "ple_norm_bwd", False, True)
    g_wout = tn_matmul(y, dh_b, "grad_w_out").reshape(wout_g.shape)

    late = ("w_out", "w_ple_gate", "w_ple_up")
    late_parts = (g_wout, g_wgate, g_wup)
    token = ex.push_pairs("pair_late", [by_core(g) for g in late_parts])
    dy = nt_matmul(dh_bp, wout_f, "out_proj_bwd", F32, dep=token)
    (dproj, dyb, lse_tot, delta, dws, dbst, dlng, dlnb, dga, dgb) = _mix_bwd(proj, dy, *mix_args)
    both_columns, from_sibling = ex.pairs_done("pair_late", [dproj])
    pair_sums = [_pair_add(mine.reshape((N_DEV,) + mine.shape[-2:]), theirs, "pair_add_" + k, ex.core)
                 for k, mine, theirs in zip(late, both_columns, from_sibling)]
    token = ex.push_chips("chip_late", pair_sums)

    running, dss = None, []
    for c, dil in enumerate(DILATIONS):
        *running, ds = _attn_bwd(qn, kn, proj, dyb, lse_tot, delta, bias[c], dil, "attn_bwd_d%d" % dil,
                                 running=running, dep=token if c == 0 else None)
        dss.append(ds)
    d_rel = _bias_grad(jnp.stack(dss), buckets, n_heads)
    dproj, dgq, dgk = _qkv_bwd(dproj, proj, *running, small["g_q"], small["g_k"], w)
    pair_sums, landed, _ = ex.chips_done("chip_late", [dproj])
    delivered = {k: (mine, theirs) for k, mine, theirs in zip(late, pair_sums, landed)}

    token_row = np.argsort(CHUNK_ORDER)
    dws = dws[:, token_row][:, :, token_row]
    dbst = dbst[token_row]
    small_grads = {
        "w_s": dws, "b_s": dbst.T, "ln_v_g": dlng, "ln_v_b": dlnb, "g_q": dgq, "g_k": dgk,
        "rel_bias": d_rel, "g_out_a": dga, "g_out_b": dgb, "g_ple": dg_ple,
    }

    bm = _tile(d, 1024)

    def grad_w_in(core, name, dep=None):
        return _matmul(
            hn, dproj, name=name, grid=(d // bm, N_CHIP, s // tks), dims=TN_DIMS, prefetch=core.reshape(1),
            a_spec=pl.BlockSpec((tks, bm), lambda i, j, k, core_ref: (k, i)),
            b_spec=pl.BlockSpec((tks, c_in), lambda i, j, k, core_ref: (k, 2 * j + core_ref[0])),
            acc_shape=(bm, c_in), out_shapes=[jax.ShapeDtypeStruct((N_CHIP, d, c_in), BF16)],
            out_specs=[pl.BlockSpec((None, bm, c_in), lambda i, j, k, core_ref: (j, i, 0))], dep=dep)[0]

    for_sibling = grad_w_in(1 - ex.core, "grad_w_in_sibling")
    token = ex.push_pairs("pair_in", [for_sibling])
    mine = grad_w_in(ex.core, "grad_w_in_mine", dep=token)
    _, from_sibling = ex.pairs_done("pair_in", [mine])
    pair_sum = _pair_add(mine, from_sibling[0], "pair_add_w_in")
    token = ex.push_chips("chip_in", [pair_sum], _pack_small(small_grads, SMALL_EARLY))

    dhn = _matmul(
        dproj, win_g, name="in_proj_bwd", grid=(s // tm, d // tn, n), dims=NT_DIMS,
        a_spec=pl.BlockSpec((tm, c_in), lambda i, j, k: (i, k)),
        b_spec=pl.BlockSpec((None, tn, c_in), lambda i, j, k: (k, j, 0)),
        acc_shape=(tm, tn), out_shapes=[jax.ShapeDtypeStruct((s, d), F32)],
        out_specs=[pl.BlockSpec((tm, tn), lambda i, j, k: (i, j))], dep=token)[0]
    grad_x, dg_pre = _rmsnorm_bwd(dhn, x, small["g_pre"], dh, "pre_norm_bwd", True, False)
    extra = while_last_travels(token, delivered) if while_last_travels is not None else []
    pair_sums, landed, slabs = ex.chips_done("chip_in", [grad_x] + list(extra))
    delivered["w_in"] = (pair_sums[0], landed[0])
    small_grads["g_pre"] = dg_pre
    return loss, grad_x, small_grads, delivered, slabs, extra


SMALL_EARLY = ("w_s", "b_s", "ln_v_g", "ln_v_b", "g_q", "g_k", "rel_bias", "g_out_a", "g_out_b", "g_ple")
SMALL_LAST = ("g_pre",)
SMALL_NAMES = SMALL_LAST + SMALL_EARLY


def _pack_small(tree, names):
    parts = []
    for name in names:
        flat = tree[name].astype(F32).reshape(-1)
        pad = (-flat.shape[0]) % HEAD
        parts.append(jnp.pad(flat, (0, pad)) if pad else flat)
    slab = jnp.concatenate(parts).reshape(-1, HEAD)
    pad_rows = (-slab.shape[0]) % 8
    return jnp.pad(slab, ((0, pad_rows), (0, 0))) if pad_rows else slab


def _unpack_small(slab, like, names):
    flat = slab.reshape(-1)
    out, off = {}, 0
    for name in names:
        size = like[name].size
        out[name] = flat[off:off + size].reshape(like[name].shape)
        off += size + (-size) % HEAD
    return out


def _peer(k):
    x, y, c = (lax.axis_index(a) for a in AXES)
    bits = ((k >> 2) & 1, (k >> 1) & 1, k & 1)
    px, py, pc = (1 - v if b else v for v, b in zip((x, y, c), bits))
    return (px, py, pc), 4 * px + 2 * py + pc


def _my_index():
    x, y, c = (lax.axis_index(a) for a in AXES)
    return 4 * x + 2 * y + c


N_CHIP = 4
HBM_SPEC = pl.BlockSpec(memory_space=pl.ANY)


def _remote(src, dst, send_sem, recv_sem, peer):
    return pltpu.make_async_remote_copy(src_ref=src, dst_ref=dst, send_sem=send_sem, recv_sem=recv_sem,
                                        device_id=peer, device_id_type=pl.DeviceIdType.MESH)


SEM_SPEC = pl.BlockSpec(memory_space=pltpu.SEMAPHORE)
HBM_ONLY = pl.BlockSpec(memory_space=pltpu.HBM)
DATAFLOW = pltpu.SideEffectType.DATAFLOW_SIDE_EFFECTING


def _comm_call(name, arrays, *, wait=None, start=None, after=()):
    n, n_after = len(arrays), len(after)

    def body(*refs):
        ins = refs[:n]
        pos = n
        if wait is not None:
            for cp in wait[2](ins, refs[pos], refs[pos + 1]):
                cp.wait()
            pos += 2
        outs = refs[pos + n_after:]
        if start is not None:
            for cp in start[1](ins, outs[0], outs[1]):
                cp.start()
        outs[-1][...] = jnp.zeros_like(outs[-1])

    operands = [pltpu.with_memory_space_constraint(a, pltpu.HBM) for a in arrays]
    in_specs = [HBM_ONLY] * n
    if wait is not None:
        operands += [wait[0], wait[1]]
        in_specs += [SEM_SPEC, SEM_SPEC]
    operands += list(after)
    in_specs += [HBM_SPEC] * n_after
    out_shape, out_specs = [], []
    if start is not None:
        out_shape += [pltpu.SemaphoreType.DMA((start[0],))] * 2
        out_specs += [SEM_SPEC, SEM_SPEC]
    first = len(out_shape)
    out_shape += [pltpu.HBM(a.shape, a.dtype) for a in arrays] + [jax.ShapeDtypeStruct((SUB, HEAD), F32)]
    out_specs += [HBM_ONLY] * n + [pl.BlockSpec(memory_space=pltpu.VMEM)]
    res = pl.pallas_call(
        body, name=name, out_shape=tuple(out_shape), in_specs=tuple(in_specs), out_specs=tuple(out_specs),
        input_output_aliases={i: first + i for i in range(n)},
        compiler_params=pltpu.CompilerParams(has_side_effects=DATAFLOW),
    )(*operands)
    sems = (res[0], res[1]) if start is not None else None
    return list(res[first:first + n]), sems, res[-1]


class _GradExchange:
    def __init__(self):
        x, y, c = (lax.axis_index(a) for a in AXES)
        self.core = c.astype(jnp.int32)
        self.chip = (2 * x + y).astype(jnp.int32)
        self.pending = {}

    def _pair_copies(self, n_arr):
        def make(refs, send_sems, recv_sems):
            sibling, _ = _peer(1)
            other = 1 - lax.axis_index("c")
            srcs, lands = refs[:n_arr], refs[n_arr:]
            pick = lambda ref, ch: ref.at[ch, other] if len(ref.shape) == 4 else ref.at[ch]
            return [_remote(pick(srcs[a], ch), lands[a].at[ch], send_sems.at[a * N_CHIP + ch],
                            recv_sems.at[a * N_CHIP + ch], sibling)
                    for a in range(n_arr) for ch in range(N_CHIP)]
        return make

    def _chip_copies(self, n_arr, with_slab):
        def make(refs, send_sems, recv_sems):
            x, y = lax.axis_index("x"), lax.axis_index("y")
            my_chip = 2 * x + y
            srcs, lands = refs[:n_arr], refs[n_arr:2 * n_arr]
            copies = []
            for j, k in enumerate((2, 4, 6)):
                peer, peer_idx = _peer(k)
                for a in range(n_arr):
                    copies.append(_remote(srcs[a].at[peer_idx // 2], lands[a].at[my_chip],
                                          send_sems.at[3 * a + j], recv_sems.at[3 * a + j], peer))
            if with_slab:
                slab, slab_land = refs[2 * n_arr], refs[2 * n_arr + 1]
                for k in range(1, N_DEV):
                    peer, _ = _peer(k)
                    copies.append(_remote(slab, slab_land.at[_my_index()], send_sems.at[3 * n_arr + k - 1],
                                          recv_sems.at[3 * n_arr + k - 1], peer))
            return copies
        return make

    def push_pairs(self, tag, for_sibling):
        n_arr = len(for_sibling)
        lands = [lax.empty((N_CHIP,) + a.shape[-2:], a.dtype) for a in for_sibling]
        make = self._pair_copies(n_arr)
        arrays, sems, token = _comm_call(tag + "_start", list(for_sibling) + lands, start=(n_arr * N_CHIP, make))
        self.pending[tag] = (arrays, sems, make, n_arr)
        return token

    def pairs_done(self, tag, after):
        arrays, sems, make, n_arr = self.pending.pop(tag)
        arrays, _, _ = _comm_call(tag + "_wait", arrays, wait=(sems[0], sems[1], make), after=after)
        return arrays[:n_arr], arrays[n_arr:]

    def push_chips(self, tag, pair_sums, slab=None):
        n_arr = len(pair_sums)
        arrays = list(pair_sums) + [lax.empty(a.shape, a.dtype) for a in pair_sums]
        n_copies = 3 * n_arr
        if slab is not None:
            arrays += [slab, lax.empty((N_DEV,) + slab.shape, slab.dtype)]
            n_copies += N_DEV - 1
        make = self._chip_copies(n_arr, slab is not None)
        arrays, sems, token = _comm_call(tag + "_start", arrays, start=(n_copies, make))
        self.pending[tag] = (arrays, sems, make, n_arr)
        return token

    def chips_done(self, tag, after):
        arrays, sems, make, n_arr = self.pending.pop(tag)
        arrays, _, _ = _comm_call(tag + "_wait", arrays, wait=(sems[0], sems[1], make), after=after)
        return arrays[:n_arr], arrays[n_arr:2 * n_arr], arrays[2 * n_arr:]


def _cast_place(w, name):
    r, c = w.shape
    tr = r if r * c <= MIB else 1 << ((MIB // c).bit_length() - 1)
    assert r % tr == 0

    def body(me_ref, w_ref, o_ref):
        o_ref[...] = w_ref[...].astype(BF16)

    return pl.pallas_call(
        body, name=name, out_shape=jax.ShapeDtypeStruct((N_DEV, r, c), BF16),
        grid_spec=pltpu.PrefetchScalarGridSpec(
            num_scalar_prefetch=1, grid=(r // tr,),
            in_specs=[pl.BlockSpec((tr, c), lambda i, me_ref: (i, 0))],
            out_specs=pl.BlockSpec((None, tr, c), lambda i, me_ref: (me_ref[0], i, 0))),
        compiler_params=_params(("arbitrary",), 40),
    )(_my_index().astype(jnp.int32).reshape(1), w)


class _WeightGather:
    CHIPS = (2, 4, 6)

    def __init__(self, buffers):
        self.buffers = dict(buffers)
        self.pending = {}
        self.me = _my_index().astype(jnp.int32)

    def _own_slot_to(self, peers):
        def make(refs, send_sems, recv_sems):
            me = _my_index()
            return [_remote(ref.at[me], ref.at[me], send_sems.at[len(peers) * a + j],
                            recv_sems.at[len(peers) * a + j], _peer(k)[0])
                    for a, ref in enumerate(refs) for j, k in enumerate(peers)]
        return make

    def _forward_from(self, chips):
        def make(refs, send_sems, recv_sems):
            sibling, _ = _peer(1)
            copies = []
            for a, ref in enumerate(refs):
                for j, k in enumerate(chips):
                    slot = ref.at[_peer(k)[1]]
                    copies.append(_remote(slot, slot, send_sems.at[len(chips) * a + j],
                                          recv_sems.at[len(chips) * a + j], sibling))
            return copies
        return make

    def _run(self, call, names, **kw):
        arrays, sems, token = _comm_call(call, [self.buffers[k] for k in names], **kw)
        self.buffers.update(zip(names, arrays))
        return sems, token

    def start(self, tag, names, peers, after=()):
        make = self._own_slot_to(peers)
        sems, token = self._run(tag + "_start", names, start=(len(names) * len(peers), make), after=after)
        self.pending[tag] = (names, sems, make, peers)
        return token

    @staticmethod
    def _relay(refs, send_sems, recv_sems):
        x, y, c = (lax.axis_index(a) for a in AXES)
        peer = (x ^ (1 - c), y ^ c, c)
        slot = _my_index() ^ (2 + 2 * c)
        return [_remote(ref.at[slot], ref.at[slot], send_sems.at[a], recv_sems.at[a], peer)
                for a, ref in enumerate(refs)]

    def relay_start(self, tag, names, after=()):
        sems, token = self._run(tag + "_start", names, start=(len(names), self._relay), after=after)
        self.pending[tag] = (names, sems, self._relay, (6,))
        return token

    def arrived(self, tag, after):
        names, sems, make, _ = self.pending.pop(tag)
        self._run(tag + "_wait", names, wait=(sems[0], sems[1], make), after=after)

    def forward(self, tag, after):
        names, sems, make, peers = self.pending.pop(tag)
        chips = tuple(k for k in peers if k != 1)
        onward = self._forward_from(chips)
        new_sems, token = self._run(tag + "_forward", names, wait=(sems[0], sems[1], make),
                                    start=(len(chips) * len(names), onward), after=after)
        self.pending[tag + "/fwd"] = (names, new_sems, onward)
        return token

    def forwarded(self, tag, after):
        names, sems, make = self.pending.pop(tag + "/fwd")
        self._run(tag + "_done", names, wait=(sems[0], sems[1], make), after=after)


def _pair_add(mine, theirs, name, core=None):
    _, r, c_dim = theirs.shape
    tr = r if r * c_dim <= MIB else 1 << ((MIB // c_dim).bit_length() - 1)
    assert r % tr == 0
    stride = 1 if core is None else 2
    offset = jnp.zeros((1,), jnp.int32) if core is None else core.reshape(1)

    def body(off_ref, a_ref, b_ref, o_ref):
        o_ref[...] = (a_ref[...].astype(F32) + b_ref[...].astype(F32)).astype(BF16)

    blk = (None, tr, c_dim)
    return pl.pallas_call(
        body, name=name, out_shape=jax.ShapeDtypeStruct(theirs.shape, BF16),
        grid_spec=pltpu.PrefetchScalarGridSpec(
            num_scalar_prefetch=1, grid=(N_CHIP, r // tr),
            in_specs=[pl.BlockSpec(blk, lambda ch, i, off_ref: (stride * ch + off_ref[0], i, 0)),
                      pl.BlockSpec(blk, lambda ch, i, off_ref: (ch, i, 0))],
            out_specs=pl.BlockSpec(blk, lambda ch, i, off_ref: (ch, i, 0))),
        compiler_params=_params(("arbitrary", "arbitrary"), 40),
    )(offset, mine, theirs)


def _slab_exchange(slab, name):
    def body(slab_in, slab_out, send_sems, recv_sems, local_sem):
        me = _my_index()
        local = pltpu.make_async_copy(slab_in, slab_out.at[me], local_sem)
        local.start()
        sends = []
        for k in range(1, N_DEV):
            peer, _ = _peer(k)
            sends.append(_remote(slab_in, slab_out.at[me], send_sems.at[k - 1], recv_sems.at[k - 1], peer))
        for cp in sends:
            cp.start()
        for k in range(1, N_DEV):
            peer, peer_idx = _peer(k)
            slot = slab_out.at[peer_idx]
            _remote(slot, slot, send_sems.at[k - 1], recv_sems.at[k - 1], peer).wait_recv()
        for cp in sends:
            cp.wait_send()
        local.wait()

    return pl.pallas_call(
        body, name=name, out_shape=jax.ShapeDtypeStruct((N_DEV,) + slab.shape, slab.dtype),
        in_specs=[HBM_SPEC], out_specs=HBM_SPEC,
        scratch_shapes=[pltpu.SemaphoreType.DMA((N_DEV - 1,)), pltpu.SemaphoreType.DMA((N_DEV - 1,)),
                        pltpu.SemaphoreType.DMA],
        compiler_params=pltpu.CompilerParams(has_side_effects=True),
    )(slab)


def _adamw_math(w, g, m, v):
    m = ADAM_B1 * m + (1.0 - ADAM_B1) * g
    v = ADAM_B2 * v + (1.0 - ADAM_B2) * (g * g)
    m_hat = m / (1.0 - ADAM_B1 ** ADAM_STEP)
    v_hat = v / (1.0 - ADAM_B2 ** ADAM_STEP)
    delta = -ADAM_LR * (m_hat / (jnp.sqrt(v_hat) + ADAM_EPS) + ADAM_WD * w)
    return delta, m, v


def _adamw(parts, own, place, w, m, v, name, dep=None):
    n_parts = parts.shape[0]
    r, c = w.shape
    budget = 280 * 1024
    tr = r if r * c <= budget else 1 << ((budget // c).bit_length() - 1)
    assert r % tr == 0

    def body(place_ref, p_ref, own_ref, w_ref, m_ref, v_ref, g_ref, d_ref, nm_ref, nv_ref):
        mine = own_ref[...].astype(F32)
        g = None
        for i in range(n_parts):
            term = jnp.where(place_ref[0] == i, mine, p_ref[i].astype(F32))
            g = term if g is None else g + term
        delta, nm, nv = _adamw_math(w_ref[...], g, m_ref[...], v_ref[...])
        g_ref[...] = g
        d_ref[...] = delta
        nm_ref[...] = nm
        nv_ref[...] = nv

    blk = pl.BlockSpec((tr, c), lambda i, place_ref: (i, 0))
    shape = jax.ShapeDtypeStruct((r, c), F32)
    in_specs = [pl.BlockSpec((n_parts, tr, c), lambda i, place_ref: (0, i, 0)),
                pl.BlockSpec((None, tr, c), lambda i, place_ref: (place_ref[1], i, 0)), blk, blk, blk]
    operands = [parts, own, w, m, v]
    if dep is not None:
        body = _drop_arg(body, 1 + len(operands))
        in_specs.append(pl.BlockSpec((SUB, HEAD), lambda i, place_ref: (0, 0)))
        operands.append(dep)
    return pl.pallas_call(
        body, name=name, out_shape=[shape] * 4,
        grid_spec=pltpu.PrefetchScalarGridSpec(
            num_scalar_prefetch=1, grid=(r // tr,), in_specs=in_specs, out_specs=[blk] * 4),
        compiler_params=_params(("arbitrary",), 48),
    )(place, *operands)


def kernel(x, p, g_pre, w_in, w_s, b_s, ln_v_g, ln_v_b, g_q, g_k, rel_bias, g_out_a, g_out_b, w_out, g_ple, w_ple_gate, w_ple_up, loss_target, m_g_pre, m_w_in, m_w_s, m_b_s, m_ln_v_g, m_ln_v_b, m_g_q, m_g_k, m_rel_bias, m_g_out_a, m_g_out_b, m_w_out, m_g_ple, m_w_ple_gate, m_w_ple_up, v_g_pre, v_w_in, v_w_s, v_b_s, v_ln_v_g, v_ln_v_b, v_g_q, v_g_k, v_rel_bias, v_g_out_a, v_g_out_b, v_w_out, v_g_ple, v_w_ple_gate, v_w_ple_up):
    args = dict(locals())
    small = {"g_pre": g_pre, "w_s": w_s[0], "b_s": b_s[0], "ln_v_g": ln_v_g, "ln_v_b": ln_v_b, "g_q": g_q,
             "g_k": g_k, "rel_bias": rel_bias, "g_out_a": g_out_a, "g_out_b": g_out_b, "g_ple": g_ple}
    big_names = ("w_in", "w_out", "w_ple_gate", "w_ple_up")
    big = {k: args[k][0] for k in big_names}

    wg = _WeightGather({k: _cast_place(big[k], "place_" + k) for k in big_names})
    ex = _GradExchange()
    results = {}

    def big_adamw(k, delivered, dep=None):
        mine, theirs = delivered[k]
        place = jnp.stack([ex.chip, ex.chip])
        return _adamw(theirs, mine, place, big[k], args["m_" + k][0], args["v_" + k][0], "adamw_" + k, dep=dep)

    def while_last_travels(token, delivered):
        done = []
        for k in big_names[1:]:
            results[k] = big_adamw(k, delivered, dep=token)
            done.append(results[k][0])
        return done

    loss, grad_x, small_parts, delivered, slabs, _ = _local_step(
        x[0], p[0, 0], loss_target[0], small, wg, ex, while_last_travels)
    results["w_in"] = big_adamw("w_in", delivered)
    for k in big_names:
        results[k] = [t[None] for t in results[k]]

    squeeze = lambda t: {k: (t[k][0] if k in ("w_s", "b_s") else t[k]) for k in SMALL_NAMES}
    small_m = squeeze({k: args["m_" + k] for k in SMALL_NAMES})
    small_v = squeeze({k: args["v_" + k] for k in SMALL_NAMES})
    me = _my_index().astype(jnp.int32)
    place = jnp.stack([me, jnp.zeros((), jnp.int32)])
    last_slab = _pack_small(small_parts, SMALL_LAST)
    groups = ((SMALL_EARLY, slabs[1], slabs[0], "adamw_small"),
              (SMALL_LAST, _slab_exchange(last_slab, "last_exchange"), last_slab, "adamw_last"))
    for names_, parts, own, call_name in groups:
        packed = _adamw(parts, own[None], place, _pack_small(small, names_), _pack_small(small_m, names_),
                        _pack_small(small_v, names_), call_name)
        for idx in range(4):
            tree = _unpack_small(packed[idx], small, names_)
            for k in names_:
                results.setdefault(k, [None] * 4)[idx] = tree[k].reshape(args[k].shape)

    names = ("g_pre", "w_in", "w_s", "b_s", "ln_v_g", "ln_v_b", "g_q", "g_k", "rel_bias", "g_out_a", "g_out_b",
             "w_out", "g_ple", "w_ple_gate", "w_ple_up")
    total = lax.psum(loss, AXES)
    out = [total, grad_x[None]]
    for idx in range(4):
        out += [results[k][idx] for k in names]
    return tuple(out)
```

```python
import functools
import math

import numpy as np
import jax
import jax.numpy as jnp
from jax import lax
from jax.experimental import pallas as pl
from jax.experimental.pallas import tpu as pltpu

F32 = jnp.float32
BF16 = jnp.bfloat16
EPS = 1e-6
NEG_INF = -1e30
HEAD = 128
DILATIONS = (1, 4, 16)
NUM_BUCKETS = 32
MAX_DISTANCE = 2048
N_SEG = 7
ADAM_LR = 0.001
ADAM_B1 = 0.9
ADAM_B2 = 0.999
ADAM_EPS = 1e-08
ADAM_WD = 0.01
ADAM_STEP = 10
AXES = ("x", "y", "c")
N_DEV = 8
MIB = 1 << 20

SUB = 8

CHUNK_ORDER = np.array([16 * (r % SUB) + r // SUB for r in range(HEAD)])
BLOCK_ORDER = {
    1: CHUNK_ORDER,
    4: np.array([32 * (r // 32) + 4 * (r % SUB) + (r // SUB) % 4 for r in range(HEAD)]),
    16: np.arange(HEAD),
}

NT_DIMS = (((1,), (1,)), ((), ()))
TN_DIMS = (((0,), (0,)), ((), ()))
NN_DIMS = (((1,), (0,)), ((), ()))


def _params(semantics, vmem_mib):
    return pltpu.CompilerParams(dimension_semantics=semantics, vmem_limit_bytes=vmem_mib * MIB)


def _gelu(a):
    return 0.5 * a * (1.0 + lax.erf(a * (2.0 ** -0.5)))


def _gelu_and_grad(a):
    cdf = 0.5 * (1.0 + lax.erf(a * (2.0 ** -0.5)))
    return a * cdf, cdf + a * jnp.exp(-0.5 * a * a) * ((2.0 * math.pi) ** -0.5)


def _silu_and_grad(a):
    s = jax.nn.sigmoid(a)
    return a * s, s * (1.0 + a * (1.0 - s))


def _rms(v):
    return lax.rsqrt(jnp.mean(v * v, axis=-1, keepdims=True) + EPS)


def _rms_bwd(dy, v, r, g):
    gy = dy * g
    return r * gy - v * (r * r * r) * jnp.mean(gy * v, axis=-1, keepdims=True)


def _dot(a, b, dims=NN_DIMS):
    return lax.dot_general(a, b, dims, preferred_element_type=F32)


def _lane_pick(cols, width):
    rows = cols[0].shape[0]
    lane = lax.broadcasted_iota(jnp.int32, (rows, width), 1)
    out = jnp.zeros((rows, width), F32)
    for h, col in enumerate(cols):
        out = jnp.where(lane == h, col, out)
    return out


def _chunk_perm():
    return jnp.asarray(np.eye(HEAD, dtype=np.float32)[CHUNK_ORDER], BF16)


def _unpermute_f32(p, v):
    hi = v.astype(BF16)
    rest = v - hi.astype(F32)
    mid = rest.astype(BF16)
    lo = (rest - mid.astype(F32)).astype(BF16)
    return _dot(p, hi, TN_DIMS) + _dot(p, mid, TN_DIMS) + _dot(p, lo, TN_DIMS)


def _rmsnorm_fwd(x, g, name, permute, dep=None):
    s, d = x.shape
    tm = HEAD

    def body(x_ref, g_ref, p_ref, o_ref):
        v = x_ref[...]
        out = (v * _rms(v) * g_ref[...]).astype(BF16)
        if permute:
            out = _dot(p_ref[...], out).astype(BF16)
        o_ref[...] = out

    in_specs = [pl.BlockSpec((tm, d), lambda i: (i, 0)), pl.BlockSpec((1, d), lambda i: (0, 0)),
                pl.BlockSpec((HEAD, HEAD), lambda i: (0, 0))]
    operands = [x, g, _chunk_perm()]
    if dep is not None:
        body = _drop_arg(body, len(operands))
        in_specs.append(DEP_SPEC)
        operands.append(dep)
    return pl.pallas_call(
        body, name=name, grid=(s // tm,),
        out_shape=jax.ShapeDtypeStruct((s, d), BF16), in_specs=in_specs,
        out_specs=pl.BlockSpec((tm, d), lambda i: (i, 0)),
        compiler_params=_params(("arbitrary",), 40),
    )(*operands)


def _rmsnorm_bwd(dy, v, g, res, name, dy_permuted, with_bf16):
    s, d = v.shape
    tm = HEAD
    perm = _chunk_perm()

    def body(dy_ref, v_ref, g_ref, res_ref, p_ref, *outs):
        dx_ref, dg_ref = outs[0], outs[-1]
        i = pl.program_id(0)
        vv, dyv = v_ref[...], dy_ref[...]
        if dy_permuted:
            dyv = _unpermute_f32(p_ref[...], dyv) if dyv.dtype == F32 else _dot(p_ref[...], dyv, TN_DIMS)
        dyv = dyv.astype(F32)
        r = _rms(vv)
        dx = res_ref[...] + _rms_bwd(dyv, vv, r, g_ref[...])
        dx_ref[...] = dx
        if with_bf16:
            dxb = dx.astype(BF16)
            outs[1][...] = dxb
            outs[2][...] = _dot(p_ref[...], dxb).astype(BF16)

        @pl.when(i == 0)
        def _():
            dg_ref[...] = jnp.zeros_like(dg_ref)

        dg_ref[...] += jnp.sum(dyv * vv * r, axis=0, keepdims=True)

    row = pl.BlockSpec((tm, d), lambda i: (i, 0))
    vec = pl.BlockSpec((1, d), lambda i: (0, 0))
    shapes = [jax.ShapeDtypeStruct((s, d), F32)]
    specs = [row]
    if with_bf16:
        shapes += [jax.ShapeDtypeStruct((s, d), BF16)] * 2
        specs += [row, row]
    shapes.append(jax.ShapeDtypeStruct((1, d), F32))
    specs.append(vec)
    return pl.pallas_call(
        body, name=name, grid=(s // tm,), out_shape=shapes,
        in_specs=[row, row, vec, row, pl.BlockSpec((HEAD, HEAD), lambda i: (0, 0))], out_specs=specs,
        compiler_params=_params(("arbitrary",), 40),
    )(dy, v, g, res, perm)


DEP_SPEC = pl.BlockSpec((SUB, HEAD), lambda *_: (0, 0))


def _drop_arg(body, pos):
    return lambda *refs: body(*refs[:pos], *refs[pos + 1:])


def _matmul(a, b, *, name, grid, a_spec, b_spec, dims, acc_shape, out_shapes, out_specs,
            extra=(), extra_specs=(), epilogue=None, vmem_mib=48, dep=None, prefetch=None, carry=None):
    nk = grid[2]
    n_user = len(extra)
    for unread, spec in ((dep, DEP_SPEC), (carry, HBM_SPEC)):
        if unread is not None:
            extra, extra_specs = tuple(extra) + (unread,), tuple(extra_specs) + (spec,)
    n_extra, n_out = len(extra), len(out_shapes)
    n_pre = 0 if prefetch is None else 1
    aliases = {} if carry is None else {n_pre + 2 + n_extra - 1: 0}

    def body(*refs):
        refs = refs[n_pre:]
        a_ref, b_ref = refs[0], refs[1]
        ex = refs[2:2 + n_user]
        outs = refs[2 + n_extra:2 + n_extra + n_out]
        acc = refs[-1]
        k = pl.program_id(2)

        @pl.when(k == 0)
        def _():
            acc[...] = jnp.zeros_like(acc)

        av = a_ref[...]
        if av.dtype != BF16:
            av = av.astype(BF16)
        if len(b_ref.shape) == 3:
            span = b_ref.shape[2]
            acc[...] += sum(_dot(av[:, g * span:(g + 1) * span], b_ref[g], dims) for g in range(b_ref.shape[0]))
        else:
            acc[...] += _dot(av, b_ref[...], dims)

        @pl.when(k == nk - 1)
        def _():
            if epilogue is None:
                outs[0][...] = acc[...].astype(outs[0].dtype)
            else:
                epilogue(acc, ex, outs)

    scratch = [pltpu.VMEM(acc_shape, F32)]
    params = _params(("parallel", "parallel", "arbitrary"), vmem_mib)
    if prefetch is None:
        return pl.pallas_call(
            body, name=name, grid=grid, out_shape=list(out_shapes),
            in_specs=[a_spec, b_spec, *extra_specs], out_specs=list(out_specs),
            scratch_shapes=scratch, compiler_params=params, input_output_aliases=aliases,
        )(a, b, *extra)
    return pl.pallas_call(
        body, name=name, out_shape=list(out_shapes),
        grid_spec=pltpu.PrefetchScalarGridSpec(
            num_scalar_prefetch=1, grid=grid, in_specs=[a_spec, b_spec, *extra_specs],
            out_specs=list(out_specs), scratch_shapes=scratch),
        compiler_params=params, input_output_aliases=aliases,
    )(prefetch, a, b, *extra)


def _tile(n, want):
    t = min(n, want)
    while n % t:
        t //= 2
    return t


def _rel_buckets(dil):
    order = BLOCK_ORDER[dil]
    qi = jnp.asarray(HEAD + order)
    kj = jnp.asarray(np.concatenate([order, HEAD + order]))
    delta = qi[:, None] - kj[None, :]
    band = (delta >= 0) & (delta <= HEAD)
    dist = jnp.clip(delta, 0, None) * dil
    max_exact = NUM_BUCKETS // 2
    dd = jnp.maximum(dist, 1).astype(F32)
    large = max_exact + (jnp.log(dd / max_exact) / math.log(MAX_DISTANCE / max_exact)
                         * (NUM_BUCKETS - max_exact)).astype(jnp.int32)
    large = jnp.minimum(large, NUM_BUCKETS - 1)
    bucket = jnp.where(dist < max_exact, dist, large)
    return jnp.where(band, bucket, -1).astype(jnp.int32)


def _bias_build(rel_bias, buckets, n_heads):
    nd = buckets.shape[0]

    def body(rb_ref, bk_ref, o_ref):
        for c in range(nd):
            def per_head(h, carry, c=c):
                bk = bk_ref[c]
                acc = jnp.where(bk < 0, NEG_INF, 0.0).astype(F32)
                for b in range(NUM_BUCKETS):
                    acc = jnp.where(bk == b, rb_ref[b, h], acc)
                o_ref[c, h] = acc
                return carry

            lax.fori_loop(0, n_heads, per_head, 0)

    return pl.pallas_call(
        body, name="bias_build",
        out_shape=jax.ShapeDtypeStruct((nd, n_heads, HEAD, 2 * HEAD), F32),
        in_specs=[pl.BlockSpec(memory_space=pltpu.SMEM), pl.BlockSpec(memory_space=pltpu.VMEM)],
        out_specs=pl.BlockSpec(memory_space=pltpu.VMEM),
    )(rel_bias, buckets)


def _bias_grad(ds_all, buckets, n_heads):
    nd = buckets.shape[0]
    pairs = HEAD * 2 * HEAD

    def body(ds_ref, bk_ref, o_ref):
        rows = lax.broadcasted_iota(jnp.int32, (NUM_BUCKETS, pairs), 0)
        tot = jnp.zeros((n_heads, NUM_BUCKETS), F32)
        for c in range(nd):
            onehot = (rows == bk_ref[c]).astype(BF16)
            ds = ds_ref[c]
            hi = ds.astype(BF16)
            lo = (ds - hi.astype(F32)).astype(BF16)
            tot = tot + _dot(hi, onehot, NT_DIMS) + _dot(lo, onehot, NT_DIMS)
        o_ref[...] = tot

    out = pl.pallas_call(
        body, name="bias_grad",
        out_shape=jax.ShapeDtypeStruct((n_heads, NUM_BUCKETS), F32),
        in_specs=[pl.BlockSpec(memory_space=pltpu.VMEM), pl.BlockSpec(memory_space=pltpu.VMEM)],
        out_specs=pl.BlockSpec(memory_space=pltpu.VMEM),
        compiler_params=pltpu.CompilerParams(vmem_limit_bytes=40 * MIB),
    )(ds_all.reshape(nd, n_heads, pairs), buckets.reshape(nd, 1, pairs))
    return out.T


def _qkv_prep(proj, g_q, g_k, w, dep=None):
    s = proj.shape[0]
    n_heads = w // HEAD
    tm = HEAD

    def body(q_ref, k_ref, gq_ref, gk_ref, qn_ref, kn_ref):
        gq = gq_ref[...] * (HEAD ** -0.5)
        gk = gk_ref[...]
        for h in range(n_heads):
            sl = slice(h * HEAD, (h + 1) * HEAD)
            q = q_ref[:, sl]
            k = k_ref[:, sl]
            qn_ref[:, sl] = q * _rms(q) * gq
            kn_ref[:, sl] = k * _rms(k) * gk

    seg = lambda j: pl.BlockSpec((tm, w), lambda i, j=j: (i, j))
    vec = pl.BlockSpec((1, HEAD), lambda i: (0, 0))
    out = pl.BlockSpec((tm, w), lambda i: (i, 0))
    in_specs = [seg(3), seg(4), vec, vec]
    operands = [proj, proj, g_q, g_k]
    if dep is not None:
        body = _drop_arg(body, len(operands))
        in_specs.append(DEP_SPEC)
        operands.append(dep)
    return pl.pallas_call(
        body, name="qkv_prep", grid=(s // tm,),
        out_shape=[jax.ShapeDtypeStruct((s, w), F32)] * 2,
        in_specs=in_specs, out_specs=[out, out],
        compiler_params=_params(("arbitrary",), 40),
    )(*operands)


class _BlockView:
    def __init__(self, s, dil):
        assert s % (HEAD * dil) == 0 and dil in BLOCK_ORDER
        self.nb = s // (HEAD * dil)
        if dil == 1:
            self.lead, self.block = (s,), (HEAD,)
            self.index = lambda r, n: (n,)
        elif dil == 4:
            self.lead, self.block = (s // 512, 4, 4, 4, SUB), (None, 4, 4, None, SUB)
            self.index = lambda r, n: (n, 0, 0, r, 0)
        else:
            self.lead, self.block = (s // 2048, 16, 16, SUB), (None, 16, None, SUB)
            self.index = lambda r, n: (n, 0, r, 0)

    def view(self, t):
        return t.reshape(self.lead + (t.shape[-1],))

    def spec(self, width, block_of, column=0):
        return pl.BlockSpec(self.block + (width,), lambda r, n: self.index(r, block_of(r, n)) + (column,))


def _rows(ref, lanes=slice(None)):
    v = ref[(slice(None),) * (len(ref.shape) - 1) + (lanes,)]
    return v.reshape(HEAD, v.shape[-1])


def _set_rows(ref, lanes, value):
    ref[(slice(None),) * (len(ref.shape) - 1) + (lanes,)] = value.reshape(ref.shape[:-1] + (value.shape[-1],))


V_SEGMENT = 5


def _attn_fwd(qn, kn, proj, bias, dil, name, dep=None):
    s, w = qn.shape
    n_heads = w // HEAD
    bv = _BlockView(s, dil)

    def body(q_ref, kc_ref, vc_ref, bias_ref, o_ref, lse_ref, s_scr, e_scr, lse_scr, inv_scr, k_prev, v_prev):
        n = pl.program_id(1)
        heads = [slice(h * HEAD, (h + 1) * HEAD) for h in range(n_heads)]
        lse_scr[...] = jnp.zeros_like(lse_scr)

        @pl.when(n == 0)
        def _():
            k_prev[...] = jnp.zeros_like(k_prev)
            v_prev[...] = jnp.zeros_like(v_prev)

        for h, sl in enumerate(heads):
            q = _rows(q_ref, sl).astype(BF16)
            s_p = _dot(q, k_prev[:, sl], NT_DIMS) + bias_ref[h, :, :HEAD]
            s_scr[h, :, :HEAD] = jnp.where(n > 0, s_p, NEG_INF)
            s_scr[h, :, HEAD:] = _dot(q, _rows(kc_ref, sl).astype(BF16), NT_DIMS) + bias_ref[h, :, HEAD:]
        for h in range(n_heads):
            sc = s_scr[h]
            m = jnp.max(sc, axis=-1, keepdims=True)
            e = jnp.exp(sc - m)
            den = jnp.sum(e, axis=-1, keepdims=True)
            e_scr[h] = e.astype(BF16)
            lse_scr[:, h:h + 1] = m + jnp.log(den)
            inv_scr[:, h:h + 1] = 1.0 / den
        for h, sl in enumerate(heads):
            v_cur = _rows(vc_ref, sl).astype(BF16)
            o = _dot(e_scr[h, :, :HEAD], v_prev[:, sl]) + _dot(e_scr[h, :, HEAD:], v_cur)
            _set_rows(o_ref, sl, o * inv_scr[:, h:h + 1])
            v_prev[:, sl] = v_cur
            k_prev[:, sl] = _rows(kc_ref, sl).astype(BF16)
        _set_rows(lse_ref, slice(None), lse_scr[...])

    cur = bv.spec(w, lambda r, n: n)
    in_specs = [cur, cur, bv.spec(w, lambda r, n: n, V_SEGMENT),
                pl.BlockSpec((n_heads, HEAD, 2 * HEAD), lambda r, n: (0, 0, 0))]
    operands = [bv.view(qn), bv.view(kn), bv.view(proj), bias]
    if dep is not None:
        body = _drop_arg(body, len(operands))
        in_specs.append(DEP_SPEC)
        operands.append(dep)
    o, lse = pl.pallas_call(
        body, name=name, grid=(dil, bv.nb),
        out_shape=[jax.ShapeDtypeStruct(bv.lead + (w,), F32), jax.ShapeDtypeStruct(bv.lead + (HEAD,), F32)],
        in_specs=in_specs,
        out_specs=[cur, bv.spec(HEAD, lambda r, n: n)],
        scratch_shapes=[pltpu.VMEM((n_heads, HEAD, 2 * HEAD), F32), pltpu.VMEM((n_heads, HEAD, 2 * HEAD), BF16),
                        pltpu.VMEM((HEAD, HEAD), F32), pltpu.VMEM((HEAD, HEAD), F32),
                        pltpu.VMEM((HEAD, w), BF16), pltpu.VMEM((HEAD, w), BF16)],
        compiler_params=_params(("arbitrary", "arbitrary"), 48),
    )(*operands)
    return o.reshape(s, w), lse.reshape(s, HEAD)


def _attn_bwd(qn, kn, proj, dyb, lse, delta, bias, dil, name, running=None, dep=None):
    s, w = qn.shape
    n_heads = w // HEAD
    bv = _BlockView(s, dil)
    nb = bv.nb

    n_run = 0 if running is None else 3

    def body(q_ref, kc_ref, kp_ref, vc_ref, vp_ref, dy_ref, lse_ref, dl_ref, bias_ref, *rest):
        so_far = rest[:n_run]
        dq_ref, dk_ref, dv_ref, ds_ref, carry_k, carry_v, s_scr, dp_scr, p_scr, dsb_scr, k_cur, v_cur = rest[n_run:]
        base = (lambda i, sl: _rows(so_far[i], sl)) if n_run else (lambda i, sl: 0.0)
        r = pl.program_id(0)
        step = pl.program_id(1)
        blk = nb - 1 - step

        @pl.when((r == 0) & (step == 0))
        def _():
            ds_ref[...] = jnp.zeros_like(ds_ref)

        @pl.when(step == 0)
        def _():
            carry_k[...] = jnp.zeros_like(carry_k)
            carry_v[...] = jnp.zeros_like(carry_v)
            k_cur[...] = _rows(kc_ref).astype(BF16)
            v_cur[...] = _rows(vc_ref).astype(BF16)

        heads = [slice(h * HEAD, (h + 1) * HEAD) for h in range(n_heads)]
        tots = _rows(lse_ref)
        dls = _rows(dl_ref)
        for h, sl in enumerate(heads):
            q, dy = _rows(q_ref, sl).astype(BF16), _rows(dy_ref, sl).astype(BF16)
            kp, kc = _rows(kp_ref, sl).astype(BF16), k_cur[:, sl]
            vp, vc = _rows(vp_ref, sl).astype(BF16), v_cur[:, sl]
            s_p = _dot(q, kp, NT_DIMS) + bias_ref[h, :, :HEAD]
            s_scr[h, :, :HEAD] = jnp.where(blk > 0, s_p, NEG_INF)
            s_scr[h, :, HEAD:] = _dot(q, kc, NT_DIMS) + bias_ref[h, :, HEAD:]
            dp_scr[h, :, :HEAD] = _dot(dy, vp, NT_DIMS)
            dp_scr[h, :, HEAD:] = _dot(dy, vc, NT_DIMS)
        for h in range(n_heads):
            prob = jnp.exp(s_scr[h] - tots[:, h:h + 1])
            ds = prob * (dp_scr[h] - dls[:, h:h + 1])
            ds_ref[h] += ds
            p_scr[h] = prob.astype(BF16)
            dsb_scr[h] = ds.astype(BF16)
        for h, sl in enumerate(heads):
            q, dy = _rows(q_ref, sl).astype(BF16), _rows(dy_ref, sl).astype(BF16)
            kp, kc = _rows(kp_ref, sl).astype(BF16), k_cur[:, sl]
            ds_pb, ds_cb = dsb_scr[h, :, :HEAD], dsb_scr[h, :, HEAD:]
            _set_rows(dq_ref, sl, _dot(ds_pb, kp) + _dot(ds_cb, kc) + base(0, sl))
            _set_rows(dk_ref, sl, _dot(ds_cb, q, TN_DIMS) + carry_k[:, sl] + base(1, sl))
            carry_k[:, sl] = _dot(ds_pb, q, TN_DIMS)
            _set_rows(dv_ref, sl, _dot(p_scr[h, :, HEAD:], dy, TN_DIMS) + carry_v[:, sl] + base(2, sl))
            carry_v[:, sl] = _dot(p_scr[h, :, :HEAD], dy, TN_DIMS)
            k_cur[:, sl] = kp
            v_cur[:, sl] = _rows(vp_ref, sl).astype(BF16)

    cur = bv.spec(w, lambda r, n: nb - 1 - n)
    last = bv.spec(w, lambda r, n: nb - 1)
    prev = bv.spec(w, lambda r, n: jnp.maximum(nb - 2 - n, 0))
    stat = bv.spec(HEAD, lambda r, n: nb - 1 - n)
    whole = pl.BlockSpec((n_heads, HEAD, 2 * HEAD), lambda r, n: (0, 0, 0))
    big = jax.ShapeDtypeStruct(bv.lead + (w,), F32)
    v_last = bv.spec(w, lambda r, n: nb - 1, V_SEGMENT)
    v_prev = bv.spec(w, lambda r, n: jnp.maximum(nb - 2 - n, 0), V_SEGMENT)
    in_specs = [cur, last, prev, v_last, v_prev, cur, stat, stat, whole]
    operands = [bv.view(qn), bv.view(kn), bv.view(kn), bv.view(proj), bv.view(proj), bv.view(dyb), bv.view(lse),
                bv.view(delta), bias]
    aliases = {}
    if running is not None:
        aliases = {len(operands) + i: i for i in range(3)}
        in_specs += [cur] * 3
        operands += [bv.view(t) for t in running]
    if dep is not None:
        body = _drop_arg(body, len(operands))
        in_specs.append(DEP_SPEC)
        operands.append(dep)
    dq, dk, dv, ds = pl.pallas_call(
        body, name=name, grid=(dil, nb),
        out_shape=[big, big, big, jax.ShapeDtypeStruct((n_heads, HEAD, 2 * HEAD), F32)],
        in_specs=in_specs, out_specs=[cur, cur, cur, whole], input_output_aliases=aliases,
        scratch_shapes=[pltpu.VMEM((HEAD, w), F32), pltpu.VMEM((HEAD, w), F32),
                        pltpu.VMEM((n_heads, HEAD, 2 * HEAD), F32), pltpu.VMEM((n_heads, HEAD, 2 * HEAD), F32),
                        pltpu.VMEM((n_heads, HEAD, 2 * HEAD), BF16), pltpu.VMEM((n_heads, HEAD, 2 * HEAD), BF16),
                        pltpu.VMEM((HEAD, w), BF16), pltpu.VMEM((HEAD, w), BF16)],
        compiler_params=_params(("arbitrary", "arbitrary"), 56),
    )(*operands)
    return dq.reshape(s, w), dk.reshape(s, w), dv.reshape(s, w), ds


def _qkv_bwd(dproj, proj, dq, dk, dv, g_q, g_k, w):
    s = proj.shape[0]
    n_heads = w // HEAD
    tm = HEAD

    def body(dproj_hbm, q_ref, k_ref, gq_ref, gk_ref, dq_ref, dk_ref, dv_ref, out_ref, dgq_ref, dgk_ref):
        i = pl.program_id(0)
        gq = gq_ref[...] * (HEAD ** -0.5)
        gk = gk_ref[...]
        acc_q = jnp.zeros((1, HEAD), F32)
        acc_k = jnp.zeros((1, HEAD), F32)
        for h in range(n_heads):
            sl = slice(h * HEAD, (h + 1) * HEAD)
            q, k = q_ref[:, sl], k_ref[:, sl]
            dqn, dkn = dq_ref[:, sl], dk_ref[:, sl]
            rq, rk = _rms(q), _rms(k)
            out_ref[:, h * HEAD:(h + 1) * HEAD] = _rms_bwd(dqn, q, rq, gq).astype(BF16)
            out_ref[:, w + h * HEAD:w + (h + 1) * HEAD] = _rms_bwd(dkn, k, rk, gk).astype(BF16)
            acc_q += jnp.sum(dqn * q * rq, axis=0, keepdims=True)
            acc_k += jnp.sum(dkn * k * rk, axis=0, keepdims=True)
        out_ref[:, 2 * w:] = dv_ref[...].astype(BF16)

        @pl.when(i == 0)
        def _():
            dgq_ref[...] = jnp.zeros_like(dgq_ref)
            dgk_ref[...] = jnp.zeros_like(dgk_ref)

        dgq_ref[...] += acc_q * (HEAD ** -0.5)
        dgk_ref[...] += acc_k

    seg = lambda j: pl.BlockSpec((tm, w), lambda i, j=j: (i, j))
    vec = pl.BlockSpec((1, HEAD), lambda i: (0, 0))
    row = pl.BlockSpec((tm, w), lambda i: (i, 0))
    return pl.pallas_call(
        body, name="qkv_bwd", grid=(s // tm,),
        out_shape=[jax.ShapeDtypeStruct(dproj.shape, BF16),
                   jax.ShapeDtypeStruct((1, HEAD), F32), jax.ShapeDtypeStruct((1, HEAD), F32)],
        in_specs=[pl.BlockSpec(memory_space=pl.ANY), seg(3), seg(4), vec, vec] + [row] * 3,
        out_specs=[pl.BlockSpec((tm, 3 * w), lambda i: (i, 1)), vec, vec],
        input_output_aliases={0: 0},
        compiler_params=_params(("arbitrary",), 48),
    )(dproj, proj, proj, g_q, g_k, dq, dk, dv)


def _mixer_a(u, gv, ws_ref, bst_ref, lng, lnb, z_scr, ln_scr):
    n_groups = u.shape[1] // HEAD
    mu = jnp.mean(gv, axis=-1, keepdims=True)
    xc = gv - mu
    rs = lax.rsqrt(jnp.mean(xc * xc, axis=-1, keepdims=True) + EPS)
    xhat = xc * rs
    ln_scr[...] = (xhat * lng + lnb).astype(BF16)
    causal = _causal_mask()
    for g in range(n_groups):
        sl = slice(g * HEAD, (g + 1) * HEAD)
        wm = jnp.where(causal, ws_ref[g], 0.0).astype(BF16)
        z_scr[:, sl] = _dot(wm, ln_scr[:, sl]) + bst_ref[:, g:g + 1]
    return u, xhat, rs


def _causal_mask():
    token = lambda r: 16 * (r % SUB) + r // SUB
    row = lax.broadcasted_iota(jnp.int32, (HEAD, HEAD), 0)
    col = lax.broadcasted_iota(jnp.int32, (HEAD, HEAD), 1)
    return token(col) <= token(row)


def _merge_b(o_refs, lse_refs, yb_scr):
    n_heads = yb_scr.shape[1] // HEAD
    lses = [t[...] for t in lse_refs]
    m = jnp.maximum(jnp.maximum(lses[0], lses[1]), lses[2])
    tot = m + jnp.log(sum(jnp.exp(t - m) for t in lses))
    alphas = [jnp.exp(t - tot) for t in lses]
    for h in range(n_heads):
        sl = slice(h * HEAD, (h + 1) * HEAD)
        yb_scr[:, sl] = sum(a[:, h:h + 1] * o[:, sl].astype(F32) for a, o in zip(alphas, o_refs))
    return tot


def _mix_fwd(proj, outs, lses, w_s, bst, ln_g, ln_b, g_a, g_b, w, dep=None):
    s = proj.shape[0]
    n_groups = w // HEAD

    def body(au_ref, av_ref, az_ref, bz_ref, o1, o2, o3, l1, l2, l3, ws_ref, bst_ref,
             lng_ref, lnb_ref, ga_ref, gb_ref, p_ref, y_ref, z_scr, ln_scr, yb_scr):
        u, _, _ = _mixer_a(_gelu(au_ref[...]), _gelu(av_ref[...]), ws_ref, bst_ref, lng_ref[...], lnb_ref[...],
                           z_scr, ln_scr)
        ya = u * z_scr[...]
        silu_a, _ = _silu_and_grad(az_ref[...])
        perm = p_ref[...]
        y_ref[:, :w] = _dot(perm, (ya * _rms(ya) * ga_ref[...] * silu_a).astype(BF16), TN_DIMS).astype(BF16)
        _merge_b((o1, o2, o3), (l1, l2, l3), yb_scr)
        yb = yb_scr[...]
        silu_b, _ = _silu_and_grad(bz_ref[...])
        y_ref[:, w:] = _dot(perm, (yb * _rms(yb) * gb_ref[...] * silu_b).astype(BF16), TN_DIMS).astype(BF16)

    seg = lambda j: pl.BlockSpec((HEAD, w), lambda i, j=j: (i, j))
    row = pl.BlockSpec((HEAD, w), lambda i: (i, 0))
    stat = pl.BlockSpec((HEAD, HEAD), lambda i: (i, 0))
    vec = pl.BlockSpec((1, w), lambda i: (0, 0))
    in_specs = [seg(0), seg(1), seg(2), seg(6), row, row, row, stat, stat, stat,
                pl.BlockSpec((n_groups, HEAD, HEAD), lambda i: (0, 0, 0)),
                pl.BlockSpec((HEAD, n_groups), lambda i: (0, 0)), vec, vec, vec, vec,
                pl.BlockSpec((HEAD, HEAD), lambda i: (0, 0))]
    operands = [proj, proj, proj, proj, *outs, *lses, w_s, bst, ln_g, ln_b, g_a, g_b, _chunk_perm()]
    if dep is not None:
        body = _drop_arg(body, len(operands))
        in_specs.append(DEP_SPEC)
        operands.append(dep)
    return pl.pallas_call(
        body, name="mix_fwd", grid=(s // HEAD,),
        out_shape=jax.ShapeDtypeStruct((s, 2 * w), BF16), in_specs=in_specs,
        out_specs=pl.BlockSpec((HEAD, 2 * w), lambda i: (i, 0)),
        scratch_shapes=[pltpu.VMEM((HEAD, w), F32), pltpu.VMEM((HEAD, w), BF16), pltpu.VMEM((HEAD, w), F32)],
        compiler_params=_params(("arbitrary",), 48),
    )(*operands)


def _mix_bwd(proj, dy, outs, lses, w_s, bst, ln_g, ln_b, g_a, g_b, w):
    s = proj.shape[0]
    n_groups = w // HEAD

    def body(au_ref, av_ref, az_ref, bz_ref, dy_ref, o1, o2, o3, l1, l2, l3, ws_ref, bst_ref,
             lng_ref, lnb_ref, ga_ref, gb_ref,
             dproj_ref, dyb_ref, tot_ref, dl_ref, dws_ref, dbst_ref, dlng_ref, dlnb_ref, dga_ref, dgb_ref,
             z_scr, ln_scr, yb_scr, dz_scr, dln_scr):
        i = pl.program_id(0)

        @pl.when(i == 0)
        def _():
            for t in (dws_ref, dbst_ref, dlng_ref, dlnb_ref, dga_ref, dgb_ref):
                t[...] = jnp.zeros_like(t)

        az = az_ref[...]
        lng = lng_ref[...]
        u, du_dau = _gelu_and_grad(au_ref[...])
        gv, dgv_dav = _gelu_and_grad(av_ref[...])
        u, xhat, rs = _mixer_a(u, gv, ws_ref, bst_ref, lng, lnb_ref[...], z_scr, ln_scr)
        z = z_scr[...]
        ya = u * z
        ra = _rms(ya)
        silu_a, dsilu_a = _silu_and_grad(az)
        dya_all = dy_ref[:, :w].astype(F32)
        na = ya * ra * ga_ref[...]
        dna = dya_all * silu_a
        dproj_ref[:, 2 * w:3 * w] = (dya_all * na * dsilu_a).astype(BF16)
        dga_ref[...] += jnp.sum(dna * ya * ra, axis=0, keepdims=True)
        dya = _rms_bwd(dna, ya, ra, ga_ref[...])
        dproj_ref[:, :w] = (dya * z * du_dau).astype(BF16)
        dz_scr[...] = (dya * u).astype(BF16)

        causal = _causal_mask()
        for g in range(n_groups):
            sl = slice(g * HEAD, (g + 1) * HEAD)
            wm = jnp.where(causal, ws_ref[g], 0.0).astype(BF16)
            dz = dz_scr[:, sl]
            dln_scr[:, sl] = _dot(wm, dz, TN_DIMS)
            dws_ref[g] += jnp.where(causal, _dot(dz, ln_scr[:, sl], NT_DIMS), 0.0)
            dbst_ref[:, g:g + 1] += jnp.sum(dz.astype(F32), axis=-1, keepdims=True)
        dln = dln_scr[...]
        dlng_ref[...] += jnp.sum(dln * xhat, axis=0, keepdims=True)
        dlnb_ref[...] += jnp.sum(dln, axis=0, keepdims=True)
        gy = dln * lng
        dgv = rs * (gy - jnp.mean(gy, axis=-1, keepdims=True)
                    - xhat * jnp.mean(gy * xhat, axis=-1, keepdims=True))
        dproj_ref[:, w:2 * w] = (dgv * dgv_dav).astype(BF16)
        dproj_ref[:, 3 * w:6 * w] = jnp.zeros((HEAD, 3 * w), BF16)

        tot_ref[...] = _merge_b((o1, o2, o3), (l1, l2, l3), yb_scr)
        yb = yb_scr[...]
        rb = _rms(yb)
        bz = bz_ref[...]
        silu_b, dsilu_b = _silu_and_grad(bz)
        dyb_all = dy_ref[:, w:].astype(F32)
        dnb = dyb_all * silu_b
        dproj_ref[:, 6 * w:] = (dyb_all * yb * rb * gb_ref[...] * dsilu_b).astype(BF16)
        dgb_ref[...] += jnp.sum(dnb * yb * rb, axis=0, keepdims=True)
        dyb = _rms_bwd(dnb, yb, rb, gb_ref[...])
        dyb_ref[...] = dyb
        prod = dyb * yb
        dl_ref[...] = _lane_pick(
            [jnp.sum(prod[:, h * HEAD:(h + 1) * HEAD], axis=-1, keepdims=True) for h in range(n_groups)], HEAD)

    seg = lambda j: pl.BlockSpec((HEAD, w), lambda i, j=j: (i, j))
    row_w = pl.BlockSpec((HEAD, w), lambda i: (i, 0))
    stat = pl.BlockSpec((HEAD, HEAD), lambda i: (i, 0))
    vec = pl.BlockSpec((1, w), lambda i: (0, 0))
    ws_spec = pl.BlockSpec((n_groups, HEAD, HEAD), lambda i: (0, 0, 0))
    bst_spec = pl.BlockSpec((HEAD, n_groups), lambda i: (0, 0))
    vec_shape = jax.ShapeDtypeStruct((1, w), F32)
    return pl.pallas_call(
        body, name="mix_bwd", grid=(s // HEAD,),
        out_shape=[jax.ShapeDtypeStruct((s, N_SEG * w), BF16), jax.ShapeDtypeStruct((s, w), F32),
                   jax.ShapeDtypeStruct((s, HEAD), F32), jax.ShapeDtypeStruct((s, HEAD), F32),
                   jax.ShapeDtypeStruct((n_groups, HEAD, HEAD), F32), jax.ShapeDtypeStruct((HEAD, n_groups), F32),
                   vec_shape, vec_shape, vec_shape, vec_shape],
        in_specs=[seg(0), seg(1), seg(2), seg(6), pl.BlockSpec((HEAD, 2 * w), lambda i: (i, 0)),
                  row_w, row_w, row_w, stat, stat, stat, ws_spec, bst_spec, vec, vec, vec, vec],
        out_specs=[pl.BlockSpec((HEAD, N_SEG * w), lambda i: (i, 0)), row_w, stat, stat,
                   ws_spec, bst_spec, vec, vec, vec, vec],
        scratch_shapes=[pltpu.VMEM((HEAD, w), F32), pltpu.VMEM((HEAD, w), BF16), pltpu.VMEM((HEAD, w), F32),
                        pltpu.VMEM((HEAD, w), BF16), pltpu.VMEM((HEAD, w), F32)],
        compiler_params=_params(("arbitrary",), 56),
    )(proj, proj, proj, proj, dy, *outs, *lses, w_s, bst, ln_g, ln_b, g_a, g_b)


def _local_step(x, p, tgt, small, wg, ex, while_last_travels=None):
    s, d = x.shape
    n, _, c_in = wg.buffers["w_in"].shape
    assert n == N_DEV
    d_in = n * c_in
    w = d_in // N_SEG
    n_heads = w // HEAD
    p_dim, c_up = wg.buffers["w_ple_up"].shape[1:]
    assert s % (HEAD * DILATIONS[-1]) == 0 and w % HEAD == 0 and d == n * c_up == 2 * w

    near, far = (2, 4), (6,)
    wg.start("gather_in_pair", ["w_in"], (1,))
    token = wg.start("gather_in_near", ["w_in"], near)
    rest = ["w_out", "w_ple_gate", "w_ple_up"]

    hn = _rmsnorm_fwd(x, small["g_pre"], "pre_norm", True, dep=token)
    tm, tk = _tile(s, 1024), _tile(d, 2048)

    def in_proj(shards, name, carry, dep=None):
        return _matmul(
            hn, wg.buffers["w_in"], name=name, grid=(s // tm, len(shards), d // tk), dims=NN_DIMS,
            prefetch=jnp.stack(shards).astype(jnp.int32),
            a_spec=pl.BlockSpec((tm, tk), lambda i, j, k, sh: (i, k)),
            b_spec=pl.BlockSpec((None, tk, c_in), lambda i, j, k, sh: (sh[j], k, 0)),
            acc_shape=(tm, c_in), out_shapes=[jax.ShapeDtypeStruct((s, d_in), F32)],
            out_specs=[pl.BlockSpec((tm, c_in), lambda i, j, k, sh: (i, sh[j]))], carry=carry, dep=dep,
            vmem_mib=56)[0]

    me = wg.me
    wg.arrived("gather_in_pair", [hn])
    proj = in_proj([me, me ^ 1], "in_proj_pair", None)
    buckets = jnp.stack([_rel_buckets(dil) for dil in DILATIONS])
    bias = _bias_build(small["rel_bias"], buckets, n_heads)
    ahead = {"near": [bias] + [wg.buffers[k] for k in rest], "far": []}
    for tag, chips in (("near", near), ("far", far)):
        token = wg.forward("gather_in_" + tag, [proj] + ahead[tag])
        if tag == "near":
            token = wg.relay_start("gather_in_far", ["w_in"], after=[token])
        else:
            token = wg.start("gather_rest", rest, (1,) + near, after=[token])
        proj = in_proj([me ^ k for k in chips], "in_proj_" + tag, proj, dep=token)
        wg.forwarded("gather_in_" + tag, [proj])
        proj = in_proj([me ^ k ^ 1 for k in chips], "in_proj_%s_forwarded" % tag, proj)
    win_g = wg.buffers["w_in"]

    qn, kn = _qkv_prep(proj, small["g_q"], small["g_k"], w)
    token = wg.forward("gather_rest", [qn])
    token = wg.relay_start("gather_rest_far", rest, after=[token])
    outs, lses = [], []
    for c, dil in enumerate(DILATIONS):
        o, l = _attn_fwd(qn, kn, proj, bias[c], dil, "attn_fwd_d%d" % dil, dep=token)
        outs.append(o)
        lses.append(l)
    token = wg.forward("gather_rest_far", outs)

    ws_p = small["w_s"][:, CHUNK_ORDER][:, :, CHUNK_ORDER]
    bst = small["b_s"].T[CHUNK_ORDER]
    mix_args = (outs, lses, ws_p, bst, small["ln_v_g"], small["ln_v_b"], small["g_out_a"], small["g_out_b"], w)
    y = _mix_fwd(proj, *mix_args, dep=token)
    wg.forwarded("gather_rest", [y])
    wg.forwarded("gather_rest_far", [y])
    wout_g, wgate_g, wup_g = (wg.buffers[k] for k in rest)
    wout_f = wout_g.reshape(2 * w, d)
    wgate_f = wgate_g.reshape(d, d)

    tn = _tile(d, 1024)
    tk2 = _tile(2 * w, 2048)

    def resid_epilogue(acc, ex, outs_):
        outs_[0][...] = ex[0][...] + acc[...]

    h = _matmul(
        y, wout_f, name="out_proj", grid=(s // tm, d // tn, (2 * w) // tk2), dims=NN_DIMS,
        a_spec=pl.BlockSpec((tm, tk2), lambda i, j, k: (i, k)),
        b_spec=pl.BlockSpec((tk2, tn), lambda i, j, k: (k, j)),
        acc_shape=(tm, tn), out_shapes=[jax.ShapeDtypeStruct((s, d), F32)],
        out_specs=[pl.BlockSpec((tm, tn), lambda i, j, k: (i, j))],
        extra=(x,), extra_specs=(pl.BlockSpec((tm, tn), lambda i, j, k: (i, j)),),
        epilogue=resid_epilogue)[0]

    hn2 = _rmsnorm_fwd(h, small["g_ple"], "ple_norm", False)

    tmg = _tile(s, 512)

    def ple_epilogue(acc, ex, outs_):
        h_ref, p_ref, wup_ref, tgt_ref = ex
        dout_ref, dpre_ref, dup_ref, loss_ref = outs_
        gate = jax.nn.sigmoid(acc[...])
        up = _dot(p_ref[...].astype(BF16), wup_ref[...])
        err = h_ref[...] + gate * up - tgt_ref[...]
        dout = err * (1.0 / d)
        dout_ref[...] = dout
        dpre_ref[...] = (dout * up * gate * (1.0 - gate)).astype(BF16)
        dup_ref[...] = (dout * gate).astype(BF16)
        part = 0.5 * jnp.sum(err * err) * (1.0 / d)
        rr = lax.broadcasted_iota(jnp.int32, (8, HEAD), 0)
        cc = lax.broadcasted_iota(jnp.int32, (8, HEAD), 1)
        loss_ref[...] = jnp.where((rr == 0) & (cc == 0), part, 0.0)

    tile_ij = pl.BlockSpec((tmg, c_up), lambda i, j, k: (i, j))
    dout, dpre, dup, loss_parts = _matmul(
        hn2, wgate_f, name="ple_gate", grid=(s // tmg, n, 1), dims=NN_DIMS,
        a_spec=pl.BlockSpec((tmg, d), lambda i, j, k: (i, 0)),
        b_spec=pl.BlockSpec((d, c_up), lambda i, j, k: (0, j)),
        acc_shape=(tmg, c_up),
        out_shapes=[jax.ShapeDtypeStruct((s, d), F32), jax.ShapeDtypeStruct((s, d), BF16),
                    jax.ShapeDtypeStruct((s, d), BF16), jax.ShapeDtypeStruct((s // tmg * 8, n * HEAD), F32)],
        out_specs=[tile_ij, tile_ij, tile_ij, pl.BlockSpec((8, HEAD), lambda i, j, k: (i, j))],
        extra=(h, p, wup_g, tgt),
        extra_specs=(tile_ij, pl.BlockSpec((tmg, p_dim), lambda i, j, k: (i, 0)),
                     pl.BlockSpec((None, p_dim, c_up), lambda i, j, k: (j, 0, 0)), tile_ij),
        epilogue=ple_epilogue)
    loss = jnp.sum(loss_parts)

    tks = _tile(s, 2048)
    g_wup = _matmul(
        p, dup, name="grad_w_up", grid=(1, n, s // tks), dims=TN_DIMS,
        a_spec=pl.BlockSpec((tks, p_dim), lambda i, j, k: (k, 0)),
        b_spec=pl.BlockSpec((tks, c_up), lambda i, j, k: (k, j)),
        acc_shape=(p_dim, c_up), out_shapes=[jax.ShapeDtypeStruct((n, p_dim, c_up), BF16)],
        out_specs=[pl.BlockSpec((None, p_dim, c_up), lambda i, j, k: (j, 0, 0))])[0]

    def tn_matmul(a, b, name):
        m_, n_ = a.shape[1], b.shape[1]
        bm, bn = _tile(m_, 1024), _tile(n_, 1024)
        return _matmul(
            a, b, name=name, grid=(m_ // bm, n_ // bn, s // tks), dims=TN_DIMS,
            a_spec=pl.BlockSpec((tks, bm), lambda i, j, k: (k, i)),
            b_spec=pl.BlockSpec((tks, bn), lambda i, j, k: (k, j)),
            acc_shape=(bm, bn), out_shapes=[jax.ShapeDtypeStruct((m_, n_), BF16)],
            out_specs=[pl.BlockSpec((bm, bn), lambda i, j, k: (i, j))])[0]

    def nt_matmul(a, b, name, out_dtype, dep=None):
        k_, n_ = a.shape[1], b.shape[0]
        bm, bn, bk = _tile(s, 1024), _tile(n_, 1024), _tile(k_, 2048)
        return _matmul(
            a, b, name=name, grid=(s // bm, n_ // bn, k_ // bk), dims=NT_DIMS,
            a_spec=pl.BlockSpec((bm, bk), lambda i, j, k: (i, k)),
            b_spec=pl.BlockSpec((bn, bk), lambda i, j, k: (j, k)),
            acc_shape=(bm, bn), out_shapes=[jax.ShapeDtypeStruct((s, n_), out_dtype)],
            out_specs=[pl.BlockSpec((bm, bn), lambda i, j, k: (i, j))], dep=dep)[0]

    by_core = lambda g: g.reshape((N_CHIP, 2) + g.shape[-2:])
    g_wgate = tn_matmul(hn2, dpre, "grad_w_gate").reshape(wgate_g.shape)
    dhn2 = nt_matmul(dpre, wgate_f, "ple_gate_bwd", BF16)
    dh, dh_b, dh_bp, dg_ple = _rmsnorm_bwd(dhn2, h, small["g_ple"], dout, "ple_norm_bwd", False, True)
    g_wout = tn_matmul(y, dh_b, "grad_w_out").reshape(wout_g.shape)

    late = ("w_out", "w_ple_gate", "w_ple_up")
    late_parts = (g_wout, g_wgate, g_wup)
    token = ex.push_pairs("pair_late", [by_core(g) for g in late_parts])
    dy = nt_matmul(dh_bp, wout_f, "out_proj_bwd", BF16, dep=token)
    (dproj, dyb, lse_tot, delta, dws, dbst, dlng, dlnb, dga, dgb) = _mix_bwd(proj, dy, *mix_args)
    both_columns, from_sibling = ex.pairs_done("pair_late", [dproj])
    pair_sums = [_pair_add(mine.reshape((N_DEV,) + mine.shape[-2:]), theirs, "pair_add_" + k, ex.core)
                 for k, mine, theirs in zip(late, both_columns, from_sibling)]
    token = ex.push_chips("chip_late", pair_sums)

    running, dss = None, []
    for c, dil in enumerate(DILATIONS):
        *running, ds = _attn_bwd(qn, kn, proj, dyb, lse_tot, delta, bias[c], dil, "attn_bwd_d%d" % dil,
                                 running=running, dep=token if c == 0 else None)
        dss.append(ds)
    d_rel = _bias_grad(jnp.stack(dss), buckets, n_heads)
    dproj, dgq, dgk = _qkv_bwd(dproj, proj, *running, small["g_q"], small["g_k"], w)
    pair_sums, landed, _ = ex.chips_done("chip_late", [dproj])
    delivered = {k: (mine, theirs) for k, mine, theirs in zip(late, pair_sums, landed)}

    token_row = np.argsort(CHUNK_ORDER)
    dws = dws[:, token_row][:, :, token_row]
    dbst = dbst[token_row]
    small_grads = {
        "w_s": dws, "b_s": dbst.T, "ln_v_g": dlng, "ln_v_b": dlnb, "g_q": dgq, "g_k": dgk,
        "rel_bias": d_rel, "g_out_a": dga, "g_out_b": dgb, "g_ple": dg_ple,
    }

    bm = _tile(d, 1024)

    def grad_w_in(core, name, dep=None):
        return _matmul(
            hn, dproj, name=name, grid=(d // bm, N_CHIP, s // tks), dims=TN_DIMS, prefetch=core.reshape(1),
            a_spec=pl.BlockSpec((tks, bm), lambda i, j, k, core_ref: (k, i)),
            b_spec=pl.BlockSpec((tks, c_in), lambda i, j, k, core_ref: (k, 2 * j + core_ref[0])),
            acc_shape=(bm, c_in), out_shapes=[jax.ShapeDtypeStruct((N_CHIP, d, c_in), BF16)],
            out_specs=[pl.BlockSpec((None, bm, c_in), lambda i, j, k, core_ref: (j, i, 0))], dep=dep)[0]

    for_sibling = grad_w_in(1 - ex.core, "grad_w_in_sibling")
    token = ex.push_pairs("pair_in", [for_sibling])
    mine = grad_w_in(ex.core, "grad_w_in_mine", dep=token)
    _, from_sibling = ex.pairs_done("pair_in", [mine])
    pair_sum = _pair_add(mine, from_sibling[0], "pair_add_w_in")
    token = ex.push_chips("chip_in", [pair_sum], _pack_small(small_grads, SMALL_EARLY))

    dhn = _matmul(
        dproj, win_g, name="in_proj_bwd", grid=(s // tm, d // tn, n // 2), dims=NT_DIMS,
        a_spec=pl.BlockSpec((tm, 2 * c_in), lambda i, j, k: (i, k)),
        b_spec=pl.BlockSpec((2, tn, c_in), lambda i, j, k: (k, j, 0)),
        acc_shape=(tm, tn), out_shapes=[jax.ShapeDtypeStruct((s, d), BF16)],
        out_specs=[pl.BlockSpec((tm, tn), lambda i, j, k: (i, j))], dep=token, vmem_mib=56)[0]
    grad_x, dg_pre = _rmsnorm_bwd(dhn, x, small["g_pre"], dh, "pre_norm_bwd", True, False)
    extra = while_last_travels(token, delivered) if while_last_travels is not None else []
    pair_sums, landed, slabs = ex.chips_done("chip_in", [grad_x] + list(extra))
    delivered["w_in"] = (pair_sums[0], landed[0])
    small_grads["g_pre"] = dg_pre
    return loss, grad_x, small_grads, delivered, slabs, extra


SMALL_EARLY = ("w_s", "b_s", "ln_v_g", "ln_v_b", "g_q", "g_k", "rel_bias", "g_out_a", "g_out_b", "g_ple")
SMALL_LAST = ("g_pre",)
SMALL_NAMES = SMALL_LAST + SMALL_EARLY


def _pack_small(tree, names):
    parts = []
    for name in names:
        flat = tree[name].astype(F32).reshape(-1)
        pad = (-flat.shape[0]) % HEAD
        parts.append(jnp.pad(flat, (0, pad)) if pad else flat)
    slab = jnp.concatenate(parts).reshape(-1, HEAD)
    pad_rows = (-slab.shape[0]) % 8
    return jnp.pad(slab, ((0, pad_rows), (0, 0))) if pad_rows else slab


def _unpack_small(slab, like, names):
    flat = slab.reshape(-1)
    out, off = {}, 0
    for name in names:
        size = like[name].size
        out[name] = flat[off:off + size].reshape(like[name].shape)
        off += size + (-size) % HEAD
    return out


def _peer(k):
    x, y, c = (lax.axis_index(a) for a in AXES)
    bits = ((k >> 2) & 1, (k >> 1) & 1, k & 1)
    px, py, pc = (1 - v if b else v for v, b in zip((x, y, c), bits))
    return (px, py, pc), 4 * px + 2 * py + pc


def _my_index():
    x, y, c = (lax.axis_index(a) for a in AXES)
    return 4 * x + 2 * y + c


N_CHIP = 4
HBM_SPEC = pl.BlockSpec(memory_space=pl.ANY)


def _remote(src, dst, send_sem, recv_sem, peer):
    return pltpu.make_async_remote_copy(src_ref=src, dst_ref=dst, send_sem=send_sem, recv_sem=recv_sem,
                                        device_id=peer, device_id_type=pl.DeviceIdType.MESH)


SEM_SPEC = pl.BlockSpec(memory_space=pltpu.SEMAPHORE)
HBM_ONLY = pl.BlockSpec(memory_space=pltpu.HBM)
DATAFLOW = pltpu.SideEffectType.DATAFLOW_SIDE_EFFECTING


def _comm_call(name, arrays, *, wait=None, start=None, after=()):
    n, n_after = len(arrays), len(after)

    def body(*refs):
        ins = refs[:n]
        pos = n
        if wait is not None:
            for cp in wait[2](ins, refs[pos], refs[pos + 1]):
                cp.wait()
            pos += 2
        outs = refs[pos + n_after:]
        if start is not None:
            for cp in start[1](ins, outs[0], outs[1]):
                cp.start()
        outs[-1][...] = jnp.zeros_like(outs[-1])

    operands = [pltpu.with_memory_space_constraint(a, pltpu.HBM) for a in arrays]
    in_specs = [HBM_ONLY] * n
    if wait is not None:
        operands += [wait[0], wait[1]]
        in_specs += [SEM_SPEC, SEM_SPEC]
    operands += list(after)
    in_specs += [HBM_SPEC] * n_after
    out_shape, out_specs = [], []
    if start is not None:
        out_shape += [pltpu.SemaphoreType.DMA((start[0],))] * 2
        out_specs += [SEM_SPEC, SEM_SPEC]
    first = len(out_shape)
    out_shape += [pltpu.HBM(a.shape, a.dtype) for a in arrays] + [jax.ShapeDtypeStruct((SUB, HEAD), F32)]
    out_specs += [HBM_ONLY] * n + [pl.BlockSpec(memory_space=pltpu.VMEM)]
    res = pl.pallas_call(
        body, name=name, out_shape=tuple(out_shape), in_specs=tuple(in_specs), out_specs=tuple(out_specs),
        input_output_aliases={i: first + i for i in range(n)},
        compiler_params=pltpu.CompilerParams(has_side_effects=DATAFLOW),
    )(*operands)
    sems = (res[0], res[1]) if start is not None else None
    return list(res[first:first + n]), sems, res[-1]


class _GradExchange:
    def __init__(self):
        x, y, c = (lax.axis_index(a) for a in AXES)
        self.core = c.astype(jnp.int32)
        self.chip = (2 * x + y).astype(jnp.int32)
        self.pending = {}

    def _pair_copies(self, n_arr):
        def make(refs, send_sems, recv_sems):
            sibling, _ = _peer(1)
            other = 1 - lax.axis_index("c")
            srcs, lands = refs[:n_arr], refs[n_arr:]
            pick = lambda ref, ch: ref.at[ch, other] if len(ref.shape) == 4 else ref.at[ch]
            return [_remote(pick(srcs[a], ch), lands[a].at[ch], send_sems.at[a * N_CHIP + ch],
                            recv_sems.at[a * N_CHIP + ch], sibling)
                    for a in range(n_arr) for ch in range(N_CHIP)]
        return make

    def _chip_copies(self, n_arr, with_slab):
        def make(refs, send_sems, recv_sems):
            x, y = lax.axis_index("x"), lax.axis_index("y")
            my_chip = 2 * x + y
            srcs, lands = refs[:n_arr], refs[n_arr:2 * n_arr]
            copies = []
            for j, k in enumerate((2, 4, 6)):
                peer, peer_idx = _peer(k)
                for a in range(n_arr):
                    copies.append(_remote(srcs[a].at[peer_idx // 2], lands[a].at[my_chip],
                                          send_sems.at[3 * a + j], recv_sems.at[3 * a + j], peer))
            if with_slab:
                slab, slab_land = refs[2 * n_arr], refs[2 * n_arr + 1]
                for k in range(1, N_DEV):
                    peer, _ = _peer(k)
                    copies.append(_remote(slab, slab_land.at[_my_index()], send_sems.at[3 * n_arr + k - 1],
                                          recv_sems.at[3 * n_arr + k - 1], peer))
            return copies
        return make

    def push_pairs(self, tag, for_sibling):
        n_arr = len(for_sibling)
        lands = [lax.empty((N_CHIP,) + a.shape[-2:], a.dtype) for a in for_sibling]
        make = self._pair_copies(n_arr)
        arrays, sems, token = _comm_call(tag + "_start", list(for_sibling) + lands, start=(n_arr * N_CHIP, make))
        self.pending[tag] = (arrays, sems, make, n_arr)
        return token

    def pairs_done(self, tag, after):
        arrays, sems, make, n_arr = self.pending.pop(tag)
        arrays, _, _ = _comm_call(tag + "_wait", arrays, wait=(sems[0], sems[1], make), after=after)
        return arrays[:n_arr], arrays[n_arr:]

    def push_chips(self, tag, pair_sums, slab=None):
        n_arr = len(pair_sums)
        arrays = list(pair_sums) + [lax.empty(a.shape, a.dtype) for a in pair_sums]
        n_copies = 3 * n_arr
        if slab is not None:
            arrays += [slab, lax.empty((N_DEV,) + slab.shape, slab.dtype)]
            n_copies += N_DEV - 1
        make = self._chip_copies(n_arr, slab is not None)
        arrays, sems, token = _comm_call(tag + "_start", arrays, start=(n_copies, make))
        self.pending[tag] = (arrays, sems, make, n_arr)
        return token

    def chips_done(self, tag, after):
        arrays, sems, make, n_arr = self.pending.pop(tag)
        arrays, _, _ = _comm_call(tag + "_wait", arrays, wait=(sems[0], sems[1], make), after=after)
        return arrays[:n_arr], arrays[n_arr:2 * n_arr], arrays[2 * n_arr:]


def _cast_place(w, name):
    r, c = w.shape
    tr = r if r * c <= MIB else 1 << ((MIB // c).bit_length() - 1)
    assert r % tr == 0

    def body(me_ref, w_ref, o_ref):
        o_ref[...] = w_ref[...].astype(BF16)

    return pl.pallas_call(
        body, name=name, out_shape=jax.ShapeDtypeStruct((N_DEV, r, c), BF16),
        grid_spec=pltpu.PrefetchScalarGridSpec(
            num_scalar_prefetch=1, grid=(r // tr,),
            in_specs=[pl.BlockSpec((tr, c), lambda i, me_ref: (i, 0))],
            out_specs=pl.BlockSpec((None, tr, c), lambda i, me_ref: (me_ref[0], i, 0))),
        compiler_params=_params(("arbitrary",), 40),
    )(_my_index().astype(jnp.int32).reshape(1), w)


class _WeightGather:
    CHIPS = (2, 4, 6)

    def __init__(self, buffers):
        self.buffers = dict(buffers)
        self.pending = {}
        self.me = _my_index().astype(jnp.int32)

    def _own_slot_to(self, peers):
        def make(refs, send_sems, recv_sems):
            me = _my_index()
            return [_remote(ref.at[me], ref.at[me], send_sems.at[len(peers) * a + j],
                            recv_sems.at[len(peers) * a + j], _peer(k)[0])
                    for a, ref in enumerate(refs) for j, k in enumerate(peers)]
        return make

    def _forward_from(self, chips):
        def make(refs, send_sems, recv_sems):
            sibling, _ = _peer(1)
            copies = []
            for a, ref in enumerate(refs):
                for j, k in enumerate(chips):
                    slot = ref.at[_peer(k)[1]]
                    copies.append(_remote(slot, slot, send_sems.at[len(chips) * a + j],
                                          recv_sems.at[len(chips) * a + j], sibling))
            return copies
        return make

    def _run(self, call, names, **kw):
        arrays, sems, token = _comm_call(call, [self.buffers[k] for k in names], **kw)
        self.buffers.update(zip(names, arrays))
        return sems, token

    def start(self, tag, names, peers, after=()):
        make = self._own_slot_to(peers)
        sems, token = self._run(tag + "_start", names, start=(len(names) * len(peers), make), after=after)
        self.pending[tag] = (names, sems, make, peers)
        return token

    @staticmethod
    def _relay(refs, send_sems, recv_sems):
        x, y, c = (lax.axis_index(a) for a in AXES)
        peer = (x ^ (1 - c), y ^ c, c)
        slot = _my_index() ^ (2 + 2 * c)
        return [_remote(ref.at[slot], ref.at[slot], send_sems.at[a], recv_sems.at[a], peer)
                for a, ref in enumerate(refs)]

    def relay_start(self, tag, names, after=()):
        sems, token = self._run(tag + "_start", names, start=(len(names), self._relay), after=after)
        self.pending[tag] = (names, sems, self._relay, (6,))
        return token

    def arrived(self, tag, after):
        names, sems, make, _ = self.pending.pop(tag)
        self._run(tag + "_wait", names, wait=(sems[0], sems[1], make), after=after)

    def forward(self, tag, after):
        names, sems, make, peers = self.pending.pop(tag)
        chips = tuple(k for k in peers if k != 1)
        onward = self._forward_from(chips)
        new_sems, token = self._run(tag + "_forward", names, wait=(sems[0], sems[1], make),
                                    start=(len(chips) * len(names), onward), after=after)
        self.pending[tag + "/fwd"] = (names, new_sems, onward)
        return token

    def forwarded(self, tag, after):
        names, sems, make = self.pending.pop(tag + "/fwd")
        self._run(tag + "_done", names, wait=(sems[0], sems[1], make), after=after)


def _pair_add(mine, theirs, name, core=None):
    _, r, c_dim = theirs.shape
    tr = r if r * c_dim <= MIB else 1 << ((MIB // c_dim).bit_length() - 1)
    assert r % tr == 0
    stride = 1 if core is None else 2
    offset = jnp.zeros((1,), jnp.int32) if core is None else core.reshape(1)

    def body(off_ref, a_ref, b_ref, o_ref):
        o_ref[...] = (a_ref[...].astype(F32) + b_ref[...].astype(F32)).astype(BF16)

    blk = (None, tr, c_dim)
    return pl.pallas_call(
        body, name=name, out_shape=jax.ShapeDtypeStruct(theirs.shape, BF16),
        grid_spec=pltpu.PrefetchScalarGridSpec(
            num_scalar_prefetch=1, grid=(N_CHIP, r // tr),
            in_specs=[pl.BlockSpec(blk, lambda ch, i, off_ref: (stride * ch + off_ref[0], i, 0)),
                      pl.BlockSpec(blk, lambda ch, i, off_ref: (ch, i, 0))],
            out_specs=pl.BlockSpec(blk, lambda ch, i, off_ref: (ch, i, 0))),
        compiler_params=_params(("arbitrary", "arbitrary"), 40),
    )(offset, mine, theirs)


def _slab_exchange(slab, name):
    def body(slab_in, slab_out, send_sems, recv_sems, local_sem):
        me = _my_index()
        local = pltpu.make_async_copy(slab_in, slab_out.at[me], local_sem)
        local.start()
        sends = []
        for k in range(1, N_DEV):
            peer, _ = _peer(k)
            sends.append(_remote(slab_in, slab_out.at[me], send_sems.at[k - 1], recv_sems.at[k - 1], peer))
        for cp in sends:
            cp.start()
        for k in range(1, N_DEV):
            peer, peer_idx = _peer(k)
            slot = slab_out.at[peer_idx]
            _remote(slot, slot, send_sems.at[k - 1], recv_sems.at[k - 1], peer).wait_recv()
        for cp in sends:
            cp.wait_send()
        local.wait()

    return pl.pallas_call(
        body, name=name, out_shape=jax.ShapeDtypeStruct((N_DEV,) + slab.shape, slab.dtype),
        in_specs=[HBM_SPEC], out_specs=HBM_SPEC,
        scratch_shapes=[pltpu.SemaphoreType.DMA((N_DEV - 1,)), pltpu.SemaphoreType.DMA((N_DEV - 1,)),
                        pltpu.SemaphoreType.DMA],
        compiler_params=pltpu.CompilerParams(has_side_effects=True),
    )(slab)


def _adamw_math(w, g, m, v):
    m = ADAM_B1 * m + (1.0 - ADAM_B1) * g
    v = ADAM_B2 * v + (1.0 - ADAM_B2) * (g * g)
    m_hat = m / (1.0 - ADAM_B1 ** ADAM_STEP)
    v_hat = v / (1.0 - ADAM_B2 ** ADAM_STEP)
    delta = -ADAM_LR * (m_hat / (jnp.sqrt(v_hat) + ADAM_EPS) + ADAM_WD * w)
    return delta, m, v


def _adamw(parts, own, place, w, m, v, name, dep=None):
    n_parts = parts.shape[0]
    r, c = w.shape
    budget = 280 * 1024
    tr = r if r * c <= budget else 1 << ((budget // c).bit_length() - 1)
    assert r % tr == 0

    def body(place_ref, p_ref, own_ref, w_ref, m_ref, v_ref, g_ref, d_ref, nm_ref, nv_ref):
        mine = own_ref[...].astype(F32)
        g = None
        for i in range(n_parts):
            term = jnp.where(place_ref[0] == i, mine, p_ref[i].astype(F32))
            g = term if g is None else g + term
        delta, nm, nv = _adamw_math(w_ref[...], g, m_ref[...], v_ref[...])
        g_ref[...] = g
        d_ref[...] = delta
        nm_ref[...] = nm
        nv_ref[...] = nv

    blk = pl.BlockSpec((tr, c), lambda i, place_ref: (i, 0))
    shape = jax.ShapeDtypeStruct((r, c), F32)
    in_specs = [pl.BlockSpec((n_parts, tr, c), lambda i, place_ref: (0, i, 0)),
                pl.BlockSpec((None, tr, c), lambda i, place_ref: (place_ref[1], i, 0)), blk, blk, blk]
    operands = [parts, own, w, m, v]
    if dep is not None:
        body = _drop_arg(body, 1 + len(operands))
        in_specs.append(pl.BlockSpec((SUB, HEAD), lambda i, place_ref: (0, 0)))
        operands.append(dep)
    return pl.pallas_call(
        body, name=name, out_shape=[shape] * 4,
        grid_spec=pltpu.PrefetchScalarGridSpec(
            num_scalar_prefetch=1, grid=(r // tr,), in_specs=in_specs, out_specs=[blk] * 4),
        compiler_params=_params(("arbitrary",), 48),
    )(place, *operands)


def kernel(x, p, g_pre, w_in, w_s, b_s, ln_v_g, ln_v_b, g_q, g_k, rel_bias, g_out_a, g_out_b, w_out, g_ple, w_ple_gate, w_ple_up, loss_target, m_g_pre, m_w_in, m_w_s, m_b_s, m_ln_v_g, m_ln_v_b, m_g_q, m_g_k, m_rel_bias, m_g_out_a, m_g_out_b, m_w_out, m_g_ple, m_w_ple_gate, m_w_ple_up, v_g_pre, v_w_in, v_w_s, v_b_s, v_ln_v_g, v_ln_v_b, v_g_q, v_g_k, v_rel_bias, v_g_out_a, v_g_out_b, v_w_out, v_g_ple, v_w_ple_gate, v_w_ple_up):
    args = dict(locals())
    small = {"g_pre": g_pre, "w_s": w_s[0], "b_s": b_s[0], "ln_v_g": ln_v_g, "ln_v_b": ln_v_b, "g_q": g_q,
             "g_k": g_k, "rel_bias": rel_bias, "g_out_a": g_out_a, "g_out_b": g_out_b, "g_ple": g_ple}
    big_names = ("w_in", "w_out", "w_ple_gate", "w_ple_up")
    big = {k: args[k][0] for k in big_names}

    wg = _WeightGather({k: _cast_place(big[k], "place_" + k) for k in big_names})
    ex = _GradExchange()
    results = {}

    def big_adamw(k, delivered, dep=None):
        mine, theirs = delivered[k]
        place = jnp.stack([ex.chip, ex.chip])
        return _adamw(theirs, mine, place, big[k], args["m_" + k][0], args["v_" + k][0], "adamw_" + k, dep=dep)

    def while_last_travels(token, delivered):
        done = []
        for k in big_names[1:]:
            results[k] = big_adamw(k, delivered, dep=token)
            done.append(results[k][0])
        return done

    loss, grad_x, small_parts, delivered, slabs, _ = _local_step(
        x[0], p[0, 0], loss_target[0], small, wg, ex, while_last_travels)
    results["w_in"] = big_adamw("w_in", delivered)
    for k in big_names:
        results[k] = [t[None] for t in results[k]]

    squeeze = lambda t: {k: (t[k][0] if k in ("w_s", "b_s") else t[k]) for k in SMALL_NAMES}
    small_m = squeeze({k: args["m_" + k] for k in SMALL_NAMES})
    small_v = squeeze({k: args["v_" + k] for k in SMALL_NAMES})
    me = _my_index().astype(jnp.int32)
    place = jnp.stack([me, jnp.zeros((), jnp.int32)])
    last_slab = _pack_small(small_parts, SMALL_LAST)
    groups = ((SMALL_EARLY, slabs[1], slabs[0], "adamw_small"),
              (SMALL_LAST, _slab_exchange(last_slab, "last_exchange"), last_slab, "adamw_last"))
    for names_, parts, own, call_name in groups:
        packed = _adamw(parts, own[None], place, _pack_small(small, names_), _pack_small(small_m, names_),
                        _pack_small(small_v, names_), call_name)
        for idx in range(4):
            tree = _unpack_small(packed[idx], small, names_)
            for k in names_:
                results.setdefault(k, [None] * 4)[idx] = tree[k].reshape(args[k].shape)

    names = ("g_pre", "w_in", "w_s", "b_s", "ln_v_g", "ln_v_b", "g_q", "g_k", "rel_bias", "g_out_a", "g_out_b",
             "w_out", "g_ple", "w_ple_gate", "w_ple_up")
    total = lax.psum(loss, AXES)
    out = [total, grad_x[None]]
    for idx in range(4):
        out += [results[k][idx] for k in names]
    return tuple(out)
```

```python
import math

import numpy as np
import jax
import jax.numpy as jnp
from jax import lax
from jax.experimental import pallas as pl
from jax.experimental.pallas import tpu as pltpu

F32 = jnp.float32
BF16 = jnp.bfloat16
EPS = 1e-6
NEG_INF = -1e30
HEAD = 128
DILATIONS = (1, 4, 16)
NUM_BUCKETS = 32
MAX_DISTANCE = 2048
N_SEG = 7
ADAM_LR = 0.001
ADAM_B1 = 0.9
ADAM_B2 = 0.999
ADAM_EPS = 1e-08
ADAM_WD = 0.01
ADAM_STEP = 10
AXES = ("x", "y", "c")
N_DEV = 8
MIB = 1 << 20

SUB = 8

CHUNK_ORDER = np.array([16 * (r % SUB) + r // SUB for r in range(HEAD)])
BLOCK_ORDER = {
    1: CHUNK_ORDER,
    4: np.array([32 * (r // 32) + 4 * (r % SUB) + (r // SUB) % 4 for r in range(HEAD)]),
    16: np.arange(HEAD),
}

NT_DIMS = (((1,), (1,)), ((), ()))
TN_DIMS = (((0,), (0,)), ((), ()))
NN_DIMS = (((1,), (0,)), ((), ()))


def _params(semantics, vmem_mib):
    return pltpu.CompilerParams(dimension_semantics=semantics, vmem_limit_bytes=vmem_mib * MIB)


def _gelu(a):
    return 0.5 * a * (1.0 + lax.erf(a * (2.0 ** -0.5)))


def _gelu_and_grad(a):
    cdf = 0.5 * (1.0 + lax.erf(a * (2.0 ** -0.5)))
    return a * cdf, cdf + a * jnp.exp(-0.5 * a * a) * ((2.0 * math.pi) ** -0.5)


def _silu_and_grad(a):
    s = jax.nn.sigmoid(a)
    return a * s, s * (1.0 + a * (1.0 - s))


def _rms(v):
    return lax.rsqrt(jnp.mean(v * v, axis=-1, keepdims=True) + EPS)


def _rms_bwd(dy, v, r, g):
    gy = dy * g
    return r * gy - v * (r * r * r) * jnp.mean(gy * v, axis=-1, keepdims=True)


def _dot(a, b, dims=NN_DIMS):
    return lax.dot_general(a, b, dims, preferred_element_type=F32)


def _lane_pick(cols, width):
    rows = cols[0].shape[0]
    lane = lax.broadcasted_iota(jnp.int32, (rows, width), 1)
    out = jnp.zeros((rows, width), F32)
    for h, col in enumerate(cols):
        out = jnp.where(lane == h, col, out)
    return out


def _chunk_perm():
    return jnp.asarray(np.eye(HEAD, dtype=np.float32)[CHUNK_ORDER], BF16)


def _unpermute_f32(p, v):
    hi = v.astype(BF16)
    rest = v - hi.astype(F32)
    mid = rest.astype(BF16)
    lo = (rest - mid.astype(F32)).astype(BF16)
    return _dot(p, hi, TN_DIMS) + _dot(p, mid, TN_DIMS) + _dot(p, lo, TN_DIMS)


def _rmsnorm_fwd(x, g, name, permute, dep=None):
    s, d = x.shape
    tm = HEAD

    def body(x_ref, g_ref, p_ref, o_ref):
        v = x_ref[...]
        out = (v * _rms(v) * g_ref[...]).astype(BF16)
        if permute:
            out = _dot(p_ref[...], out).astype(BF16)
        o_ref[...] = out

    in_specs = [pl.BlockSpec((tm, d), lambda i: (i, 0)), pl.BlockSpec((1, d), lambda i: (0, 0)),
                pl.BlockSpec((HEAD, HEAD), lambda i: (0, 0))]
    operands = [x, g, _chunk_perm()]
    if dep is not None:
        body = _drop_arg(body, len(operands))
        in_specs.append(DEP_SPEC)
        operands.append(dep)
    return pl.pallas_call(
        body, name=name, grid=(s // tm,),
        out_shape=jax.ShapeDtypeStruct((s, d), BF16), in_specs=in_specs,
        out_specs=pl.BlockSpec((tm, d), lambda i: (i, 0)),
        compiler_params=_params(("arbitrary",), 40),
    )(*operands)


def _rmsnorm_bwd(dy, v, g, res, name, dy_permuted, with_bf16):
    s, d = v.shape
    tm = HEAD
    perm = _chunk_perm()

    def body(dy_ref, v_ref, g_ref, res_ref, p_ref, *outs):
        dx_ref, dg_ref = outs[0], outs[-1]
        i = pl.program_id(0)
        vv, dyv = v_ref[...], dy_ref[...]
        if dy_permuted:
            dyv = _unpermute_f32(p_ref[...], dyv) if dyv.dtype == F32 else _dot(p_ref[...], dyv, TN_DIMS)
        dyv = dyv.astype(F32)
        r = _rms(vv)
        dx = res_ref[...] + _rms_bwd(dyv, vv, r, g_ref[...])
        dx_ref[...] = dx
        if with_bf16:
            dxb = dx.astype(BF16)
            outs[1][...] = dxb
            outs[2][...] = _dot(p_ref[...], dxb).astype(BF16)

        @pl.when(i == 0)
        def _():
            dg_ref[...] = jnp.zeros_like(dg_ref)

        dg_ref[...] += jnp.sum(dyv * vv * r, axis=0, keepdims=True)

    row = pl.BlockSpec((tm, d), lambda i: (i, 0))
    vec = pl.BlockSpec((1, d), lambda i: (0, 0))
    shapes = [jax.ShapeDtypeStruct((s, d), F32)]
    specs = [row]
    if with_bf16:
        shapes += [jax.ShapeDtypeStruct((s, d), BF16)] * 2
        specs += [row, row]
    shapes.append(jax.ShapeDtypeStruct((1, d), F32))
    specs.append(vec)
    return pl.pallas_call(
        body, name=name, grid=(s // tm,), out_shape=shapes,
        in_specs=[row, row, vec, row, pl.BlockSpec((HEAD, HEAD), lambda i: (0, 0))], out_specs=specs,
        compiler_params=_params(("arbitrary",), 40),
    )(dy, v, g, res, perm)


DEP_SPEC = pl.BlockSpec((SUB, HEAD), lambda *_: (0, 0))


def _drop_arg(body, pos):
    return lambda *refs: body(*refs[:pos], *refs[pos + 1:])


def _matmul(a, b, *, name, grid, a_spec, b_spec, dims, acc_shape, out_shapes, out_specs,
            extra=(), extra_specs=(), epilogue=None, vmem_mib=48, dep=None, prefetch=None, carry=None):
    nk = grid[2]
    n_user = len(extra)
    for unread, spec in ((dep, DEP_SPEC), (carry, HBM_SPEC)):
        if unread is not None:
            extra, extra_specs = tuple(extra) + (unread,), tuple(extra_specs) + (spec,)
    n_extra, n_out = len(extra), len(out_shapes)
    n_pre = 0 if prefetch is None else 1
    aliases = {} if carry is None else {n_pre + 2 + n_extra - 1: 0}

    def body(*refs):
        refs = refs[n_pre:]
        a_ref, b_ref = refs[0], refs[1]
        ex = refs[2:2 + n_user]
        outs = refs[2 + n_extra:2 + n_extra + n_out]
        acc = refs[-1]
        k = pl.program_id(2)

        @pl.when(k == 0)
        def _():
            acc[...] = jnp.zeros_like(acc)

        av = a_ref[...]
        if av.dtype != BF16:
            av = av.astype(BF16)
        if len(b_ref.shape) == 3:
            span = b_ref.shape[2]
            acc[...] += sum(_dot(av[:, g * span:(g + 1) * span], b_ref[g], dims) for g in range(b_ref.shape[0]))
        else:
            acc[...] += _dot(av, b_ref[...], dims)

        @pl.when(k == nk - 1)
        def _():
            if epilogue is None:
                outs[0][...] = acc[...].astype(outs[0].dtype)
            else:
                epilogue(acc, ex, outs)

    scratch = [pltpu.VMEM(acc_shape, F32)]
    params = _params(("parallel", "parallel", "arbitrary"), vmem_mib)
    if prefetch is None:
        return pl.pallas_call(
            body, name=name, grid=grid, out_shape=list(out_shapes),
            in_specs=[a_spec, b_spec, *extra_specs], out_specs=list(out_specs),
            scratch_shapes=scratch, compiler_params=params, input_output_aliases=aliases,
        )(a, b, *extra)
    return pl.pallas_call(
        body, name=name, out_shape=list(out_shapes),
        grid_spec=pltpu.PrefetchScalarGridSpec(
            num_scalar_prefetch=1, grid=grid, in_specs=[a_spec, b_spec, *extra_specs],
            out_specs=list(out_specs), scratch_shapes=scratch),
        compiler_params=params, input_output_aliases=aliases,
    )(prefetch, a, b, *extra)


def _tile(n, want):
    t = min(n, want)
    while n % t:
        t //= 2
    return t


def _rel_buckets(dil):
    order = BLOCK_ORDER[dil]
    qi = jnp.asarray(HEAD + order)
    kj = jnp.asarray(np.concatenate([order, HEAD + order]))
    delta = qi[:, None] - kj[None, :]
    band = (delta >= 0) & (delta <= HEAD)
    dist = jnp.clip(delta, 0, None) * dil
    max_exact = NUM_BUCKETS // 2
    dd = jnp.maximum(dist, 1).astype(F32)
    large = max_exact + (jnp.log(dd / max_exact) / math.log(MAX_DISTANCE / max_exact)
                         * (NUM_BUCKETS - max_exact)).astype(jnp.int32)
    large = jnp.minimum(large, NUM_BUCKETS - 1)
    bucket = jnp.where(dist < max_exact, dist, large)
    return jnp.where(band, bucket, -1).astype(jnp.int32)


def _bias_build(rel_bias, buckets, n_heads):
    nd = buckets.shape[0]

    def body(rb_ref, bk_ref, o_ref):
        for c in range(nd):
            def per_head(h, carry, c=c):
                bk = bk_ref[c]
                acc = jnp.where(bk < 0, NEG_INF, 0.0).astype(F32)
                for b in range(NUM_BUCKETS):
                    acc = jnp.where(bk == b, rb_ref[b, h], acc)
                o_ref[c, h] = acc
                return carry

            lax.fori_loop(0, n_heads, per_head, 0)

    return pl.pallas_call(
        body, name="bias_build",
        out_shape=jax.ShapeDtypeStruct((nd, n_heads, HEAD, 2 * HEAD), F32),
        in_specs=[pl.BlockSpec(memory_space=pltpu.SMEM), pl.BlockSpec(memory_space=pltpu.VMEM)],
        out_specs=pl.BlockSpec(memory_space=pltpu.VMEM),
    )(rel_bias, buckets)


def _bias_grad(ds_all, buckets, n_heads):
    nd = buckets.shape[0]
    pairs = HEAD * 2 * HEAD

    def body(ds_ref, bk_ref, o_ref):
        rows = lax.broadcasted_iota(jnp.int32, (NUM_BUCKETS, pairs), 0)
        tot = jnp.zeros((n_heads, NUM_BUCKETS), F32)
        for c in range(nd):
            onehot = (rows == bk_ref[c]).astype(BF16)
            ds = ds_ref[c]
            hi = ds.astype(BF16)
            lo = (ds - hi.astype(F32)).astype(BF16)
            tot = tot + _dot(hi, onehot, NT_DIMS) + _dot(lo, onehot, NT_DIMS)
        o_ref[...] = tot

    out = pl.pallas_call(
        body, name="bias_grad",
        out_shape=jax.ShapeDtypeStruct((n_heads, NUM_BUCKETS), F32),
        in_specs=[pl.BlockSpec(memory_space=pltpu.VMEM), pl.BlockSpec(memory_space=pltpu.VMEM)],
        out_specs=pl.BlockSpec(memory_space=pltpu.VMEM),
        compiler_params=pltpu.CompilerParams(vmem_limit_bytes=40 * MIB),
    )(ds_all.reshape(nd, n_heads, pairs), buckets.reshape(nd, 1, pairs))
    return out.T


def _qkv_prep(proj, g_q, g_k, w, dep=None):
    s = proj.shape[0]
    n_heads = w // HEAD
    tm = HEAD

    def body(q_ref, k_ref, gq_ref, gk_ref, qn_ref, kn_ref):
        gq = gq_ref[...] * (HEAD ** -0.5)
        gk = gk_ref[...]
        for h in range(n_heads):
            sl = slice(h * HEAD, (h + 1) * HEAD)
            q = q_ref[:, sl]
            k = k_ref[:, sl]
            qn_ref[:, sl] = q * _rms(q) * gq
            kn_ref[:, sl] = k * _rms(k) * gk

    seg = lambda j: pl.BlockSpec((tm, w), lambda i, j=j: (i, j))
    vec = pl.BlockSpec((1, HEAD), lambda i: (0, 0))
    out = pl.BlockSpec((tm, w), lambda i: (i, 0))
    in_specs = [seg(3), seg(4), vec, vec]
    operands = [proj, proj, g_q, g_k]
    if dep is not None:
        body = _drop_arg(body, len(operands))
        in_specs.append(DEP_SPEC)
        operands.append(dep)
    return pl.pallas_call(
        body, name="qkv_prep", grid=(s // tm,),
        out_shape=[jax.ShapeDtypeStruct((s, w), F32)] * 2,
        in_specs=in_specs, out_specs=[out, out],
        compiler_params=_params(("arbitrary",), 40),
    )(*operands)


class _BlockView:
    def __init__(self, s, dil):
        assert s % (HEAD * dil) == 0 and dil in BLOCK_ORDER
        self.nb = s // (HEAD * dil)
        if dil == 1:
            self.lead, self.block = (s,), (HEAD,)
            self.index = lambda r, n: (n,)
        elif dil == 4:
            self.lead, self.block = (s // 512, 4, 4, 4, SUB), (None, 4, 4, None, SUB)
            self.index = lambda r, n: (n, 0, 0, r, 0)
        else:
            self.lead, self.block = (s // 2048, 16, 16, SUB), (None, 16, None, SUB)
            self.index = lambda r, n: (n, 0, r, 0)

    def view(self, t):
        return t.reshape(self.lead + (t.shape[-1],))

    def spec(self, width, block_of, column=0):
        return pl.BlockSpec(self.block + (width,), lambda r, n: self.index(r, block_of(r, n)) + (column,))


def _rows(ref, lanes=slice(None)):
    v = ref[(slice(None),) * (len(ref.shape) - 1) + (lanes,)]
    return v.reshape(HEAD, v.shape[-1])


def _set_rows(ref, lanes, value):
    ref[(slice(None),) * (len(ref.shape) - 1) + (lanes,)] = value.reshape(ref.shape[:-1] + (value.shape[-1],))


V_SEGMENT = 5


def _attn_fwd(qn, kn, proj, bias, dil, name, dep=None):
    s, w = qn.shape
    n_heads = w // HEAD
    bv = _BlockView(s, dil)

    def body(q_ref, kc_ref, vc_ref, bias_ref, o_ref, lse_ref, s_scr, e_scr, lse_scr, inv_scr, k_prev, v_prev):
        n = pl.program_id(1)
        heads = [slice(h * HEAD, (h + 1) * HEAD) for h in range(n_heads)]
        lse_scr[...] = jnp.zeros_like(lse_scr)

        @pl.when(n == 0)
        def _():
            k_prev[...] = jnp.zeros_like(k_prev)
            v_prev[...] = jnp.zeros_like(v_prev)

        for h, sl in enumerate(heads):
            q = _rows(q_ref, sl).astype(BF16)
            s_p = _dot(q, k_prev[:, sl], NT_DIMS) + bias_ref[h, :, :HEAD]
            s_scr[h, :, :HEAD] = jnp.where(n > 0, s_p, NEG_INF)
            s_scr[h, :, HEAD:] = _dot(q, _rows(kc_ref, sl).astype(BF16), NT_DIMS) + bias_ref[h, :, HEAD:]
        for h in range(n_heads):
            sc = s_scr[h]
            m = jnp.max(sc, axis=-1, keepdims=True)
            e = jnp.exp(sc - m)
            den = jnp.sum(e, axis=-1, keepdims=True)
            e_scr[h] = e.astype(BF16)
            lse_scr[:, h:h + 1] = m + jnp.log(den)
            inv_scr[:, h:h + 1] = 1.0 / den
        for h, sl in enumerate(heads):
            v_cur = _rows(vc_ref, sl).astype(BF16)
            o = _dot(e_scr[h, :, :HEAD], v_prev[:, sl]) + _dot(e_scr[h, :, HEAD:], v_cur)
            _set_rows(o_ref, sl, o * inv_scr[:, h:h + 1])
            v_prev[:, sl] = v_cur
            k_prev[:, sl] = _rows(kc_ref, sl).astype(BF16)
        _set_rows(lse_ref, slice(None), lse_scr[...])

    cur = bv.spec(w, lambda r, n: n)
    in_specs = [cur, cur, bv.spec(w, lambda r, n: n, V_SEGMENT),
                pl.BlockSpec((n_heads, HEAD, 2 * HEAD), lambda r, n: (0, 0, 0))]
    operands = [bv.view(qn), bv.view(kn), bv.view(proj), bias]
    if dep is not None:
        body = _drop_arg(body, len(operands))
        in_specs.append(DEP_SPEC)
        operands.append(dep)
    o, lse = pl.pallas_call(
        body, name=name, grid=(dil, bv.nb),
        out_shape=[jax.ShapeDtypeStruct(bv.lead + (w,), F32), jax.ShapeDtypeStruct(bv.lead + (HEAD,), F32)],
        in_specs=in_specs,
        out_specs=[cur, bv.spec(HEAD, lambda r, n: n)],
        scratch_shapes=[pltpu.VMEM((n_heads, HEAD, 2 * HEAD), F32), pltpu.VMEM((n_heads, HEAD, 2 * HEAD), BF16),
                        pltpu.VMEM((HEAD, HEAD), F32), pltpu.VMEM((HEAD, HEAD), F32),
                        pltpu.VMEM((HEAD, w), BF16), pltpu.VMEM((HEAD, w), BF16)],
        compiler_params=_params(("arbitrary", "arbitrary"), 48),
    )(*operands)
    return o.reshape(s, w), lse.reshape(s, HEAD)


def _attn_bwd(qn, kn, proj, dyb, lse, delta, bias, dil, name, running=None, dep=None):
    s, w = qn.shape
    n_heads = w // HEAD
    bv = _BlockView(s, dil)
    nb = bv.nb

    n_run = 0 if running is None else 3

    def body(q_ref, kc_ref, kp_ref, vc_ref, vp_ref, dy_ref, lse_ref, dl_ref, bias_ref, *rest):
        so_far = rest[:n_run]
        dq_ref, dk_ref, dv_ref, ds_ref, carry_k, carry_v, s_scr, dp_scr, p_scr, dsb_scr, k_cur, v_cur = rest[n_run:]
        base = (lambda i, sl: _rows(so_far[i], sl)) if n_run else (lambda i, sl: 0.0)
        r = pl.program_id(0)
        step = pl.program_id(1)
        blk = nb - 1 - step

        @pl.when((r == 0) & (step == 0))
        def _():
            ds_ref[...] = jnp.zeros_like(ds_ref)

        @pl.when(step == 0)
        def _():
            carry_k[...] = jnp.zeros_like(carry_k)
            carry_v[...] = jnp.zeros_like(carry_v)
            k_cur[...] = _rows(kc_ref).astype(BF16)
            v_cur[...] = _rows(vc_ref).astype(BF16)

        heads = [slice(h * HEAD, (h + 1) * HEAD) for h in range(n_heads)]
        tots = _rows(lse_ref)
        dls = _rows(dl_ref)
        for h, sl in enumerate(heads):
            q, dy = _rows(q_ref, sl).astype(BF16), _rows(dy_ref, sl).astype(BF16)
            kp, kc = _rows(kp_ref, sl).astype(BF16), k_cur[:, sl]
            vp, vc = _rows(vp_ref, sl).astype(BF16), v_cur[:, sl]
            s_p = _dot(q, kp, NT_DIMS) + bias_ref[h, :, :HEAD]
            s_scr[h, :, :HEAD] = jnp.where(blk > 0, s_p, NEG_INF)
            s_scr[h, :, HEAD:] = _dot(q, kc, NT_DIMS) + bias_ref[h, :, HEAD:]
            dp_scr[h, :, :HEAD] = _dot(dy, vp, NT_DIMS)
            dp_scr[h, :, HEAD:] = _dot(dy, vc, NT_DIMS)
        for h in range(n_heads):
            prob = jnp.exp(s_scr[h] - tots[:, h:h + 1])
            ds = prob * (dp_scr[h] - dls[:, h:h + 1])
            ds_ref[h] += ds
            p_scr[h] = prob.astype(BF16)
            dsb_scr[h] = ds.astype(BF16)
        for h, sl in enumerate(heads):
            q, dy = _rows(q_ref, sl).astype(BF16), _rows(dy_ref, sl).astype(BF16)
            kp, kc = _rows(kp_ref, sl).astype(BF16), k_cur[:, sl]
            ds_pb, ds_cb = dsb_scr[h, :, :HEAD], dsb_scr[h, :, HEAD:]
            _set_rows(dq_ref, sl, _dot(ds_pb, kp) + _dot(ds_cb, kc) + base(0, sl))
            _set_rows(dk_ref, sl, _dot(ds_cb, q, TN_DIMS) + carry_k[:, sl] + base(1, sl))
            carry_k[:, sl] = _dot(ds_pb, q, TN_DIMS)
            _set_rows(dv_ref, sl, _dot(p_scr[h, :, HEAD:], dy, TN_DIMS) + carry_v[:, sl] + base(2, sl))
            carry_v[:, sl] = _dot(p_scr[h, :, :HEAD], dy, TN_DIMS)
            k_cur[:, sl] = kp
            v_cur[:, sl] = _rows(vp_ref, sl).astype(BF16)

    cur = bv.spec(w, lambda r, n: nb - 1 - n)
    last = bv.spec(w, lambda r, n: nb - 1)
    prev = bv.spec(w, lambda r, n: jnp.maximum(nb - 2 - n, 0))
    stat = bv.spec(HEAD, lambda r, n: nb - 1 - n)
    whole = pl.BlockSpec((n_heads, HEAD, 2 * HEAD), lambda r, n: (0, 0, 0))
    big = jax.ShapeDtypeStruct(bv.lead + (w,), F32)
    v_last = bv.spec(w, lambda r, n: nb - 1, V_SEGMENT)
    v_prev = bv.spec(w, lambda r, n: jnp.maximum(nb - 2 - n, 0), V_SEGMENT)
    in_specs = [cur, last, prev, v_last, v_prev, cur, stat, stat, whole]
    operands = [bv.view(qn), bv.view(kn), bv.view(kn), bv.view(proj), bv.view(proj), bv.view(dyb), bv.view(lse),
                bv.view(delta), bias]
    aliases = {}
    if running is not None:
        aliases = {len(operands) + i: i for i in range(3)}
        in_specs += [cur] * 3
        operands += [bv.view(t) for t in running]
    if dep is not None:
        body = _drop_arg(body, len(operands))
        in_specs.append(DEP_SPEC)
        operands.append(dep)
    dq, dk, dv, ds = pl.pallas_call(
        body, name=name, grid=(dil, nb),
        out_shape=[big, big, big, jax.ShapeDtypeStruct((n_heads, HEAD, 2 * HEAD), F32)],
        in_specs=in_specs, out_specs=[cur, cur, cur, whole], input_output_aliases=aliases,
        scratch_shapes=[pltpu.VMEM((HEAD, w), F32), pltpu.VMEM((HEAD, w), F32),
                        pltpu.VMEM((n_heads, HEAD, 2 * HEAD), F32), pltpu.VMEM((n_heads, HEAD, 2 * HEAD), F32),
                        pltpu.VMEM((n_heads, HEAD, 2 * HEAD), BF16), pltpu.VMEM((n_heads, HEAD, 2 * HEAD), BF16),
                        pltpu.VMEM((HEAD, w), BF16), pltpu.VMEM((HEAD, w), BF16)],
        compiler_params=_params(("arbitrary", "arbitrary"), 56),
    )(*operands)
    return dq.reshape(s, w), dk.reshape(s, w), dv.reshape(s, w), ds


def _qkv_bwd(dproj, proj, dq, dk, dv, g_q, g_k, w):
    s = proj.shape[0]
    n_heads = w // HEAD
    tm = HEAD

    def body(dproj_hbm, q_ref, k_ref, gq_ref, gk_ref, dq_ref, dk_ref, dv_ref, out_ref, dgq_ref, dgk_ref):
        i = pl.program_id(0)
        gq = gq_ref[...] * (HEAD ** -0.5)
        gk = gk_ref[...]
        acc_q = jnp.zeros((1, HEAD), F32)
        acc_k = jnp.zeros((1, HEAD), F32)
        for h in range(n_heads):
            sl = slice(h * HEAD, (h + 1) * HEAD)
            q, k = q_ref[:, sl], k_ref[:, sl]
            dqn, dkn = dq_ref[:, sl], dk_ref[:, sl]
            rq, rk = _rms(q), _rms(k)
            out_ref[:, h * HEAD:(h + 1) * HEAD] = _rms_bwd(dqn, q, rq, gq).astype(BF16)
            out_ref[:, w + h * HEAD:w + (h + 1) * HEAD] = _rms_bwd(dkn, k, rk, gk).astype(BF16)
            acc_q += jnp.sum(dqn * q * rq, axis=0, keepdims=True)
            acc_k += jnp.sum(dkn * k * rk, axis=0, keepdims=True)
        out_ref[:, 2 * w:] = dv_ref[...].astype(BF16)

        @pl.when(i == 0)
        def _():
            dgq_ref[...] = jnp.zeros_like(dgq_ref)
            dgk_ref[...] = jnp.zeros_like(dgk_ref)

        dgq_ref[...] += acc_q * (HEAD ** -0.5)
        dgk_ref[...] += acc_k

    seg = lambda j: pl.BlockSpec((tm, w), lambda i, j=j: (i, j))
    vec = pl.BlockSpec((1, HEAD), lambda i: (0, 0))
    row = pl.BlockSpec((tm, w), lambda i: (i, 0))
    return pl.pallas_call(
        body, name="qkv_bwd", grid=(s // tm,),
        out_shape=[jax.ShapeDtypeStruct(dproj.shape, BF16),
                   jax.ShapeDtypeStruct((1, HEAD), F32), jax.ShapeDtypeStruct((1, HEAD), F32)],
        in_specs=[pl.BlockSpec(memory_space=pl.ANY), seg(3), seg(4), vec, vec] + [row] * 3,
        out_specs=[pl.BlockSpec((tm, 3 * w), lambda i: (i, 1)), vec, vec],
        input_output_aliases={0: 0},
        compiler_params=_params(("arbitrary",), 48),
    )(dproj, proj, proj, g_q, g_k, dq, dk, dv)


def _mixer_a(u, gv, ws_ref, bst_ref, lng, lnb, z_scr, ln_scr):
    n_groups = u.shape[1] // HEAD
    mu = jnp.mean(gv, axis=-1, keepdims=True)
    xc = gv - mu
    rs = lax.rsqrt(jnp.mean(xc * xc, axis=-1, keepdims=True) + EPS)
    xhat = xc * rs
    ln_scr[...] = (xhat * lng + lnb).astype(BF16)
    causal = _causal_mask()
    for g in range(n_groups):
        sl = slice(g * HEAD, (g + 1) * HEAD)
        wm = jnp.where(causal, ws_ref[g], 0.0).astype(BF16)
        z_scr[:, sl] = _dot(wm, ln_scr[:, sl]) + bst_ref[:, g:g + 1]
    return u, xhat, rs


def _causal_mask():
    token = lambda r: 16 * (r % SUB) + r // SUB
    row = lax.broadcasted_iota(jnp.int32, (HEAD, HEAD), 0)
    col = lax.broadcasted_iota(jnp.int32, (HEAD, HEAD), 1)
    return token(col) <= token(row)


def _merge_b(o_refs, lse_refs, yb_scr):
    n_heads = yb_scr.shape[1] // HEAD
    lses = [t[...] for t in lse_refs]
    m = jnp.maximum(jnp.maximum(lses[0], lses[1]), lses[2])
    tot = m + jnp.log(sum(jnp.exp(t - m) for t in lses))
    alphas = [jnp.exp(t - tot) for t in lses]
    for h in range(n_heads):
        sl = slice(h * HEAD, (h + 1) * HEAD)
        yb_scr[:, sl] = sum(a[:, h:h + 1] * o[:, sl].astype(F32) for a, o in zip(alphas, o_refs))
    return tot


def _mix_fwd(proj, outs, lses, w_s, bst, ln_g, ln_b, g_a, g_b, w, dep=None):
    s = proj.shape[0]
    n_groups = w // HEAD

    def body(au_ref, av_ref, az_ref, bz_ref, o1, o2, o3, l1, l2, l3, ws_ref, bst_ref,
             lng_ref, lnb_ref, ga_ref, gb_ref, p_ref, y_ref, z_scr, ln_scr, yb_scr):
        u, _, _ = _mixer_a(_gelu(au_ref[...]), _gelu(av_ref[...]), ws_ref, bst_ref, lng_ref[...], lnb_ref[...],
                           z_scr, ln_scr)
        ya = u * z_scr[...]
        silu_a, _ = _silu_and_grad(az_ref[...])
        perm = p_ref[...]
        y_ref[:, :w] = _dot(perm, (ya * _rms(ya) * ga_ref[...] * silu_a).astype(BF16), TN_DIMS).astype(BF16)
        _merge_b((o1, o2, o3), (l1, l2, l3), yb_scr)
        yb = yb_scr[...]
        silu_b, _ = _silu_and_grad(bz_ref[...])
        y_ref[:, w:] = _dot(perm, (yb * _rms(yb) * gb_ref[...] * silu_b).astype(BF16), TN_DIMS).astype(BF16)

    seg = lambda j: pl.BlockSpec((HEAD, w), lambda i, j=j: (i, j))
    row = pl.BlockSpec((HEAD, w), lambda i: (i, 0))
    stat = pl.BlockSpec((HEAD, HEAD), lambda i: (i, 0))
    vec = pl.BlockSpec((1, w), lambda i: (0, 0))
    in_specs = [seg(0), seg(1), seg(2), seg(6), row, row, row, stat, stat, stat,
                pl.BlockSpec((n_groups, HEAD, HEAD), lambda i: (0, 0, 0)),
                pl.BlockSpec((HEAD, n_groups), lambda i: (0, 0)), vec, vec, vec, vec,
                pl.BlockSpec((HEAD, HEAD), lambda i: (0, 0))]
    operands = [proj, proj, proj, proj, *outs, *lses, w_s, bst, ln_g, ln_b, g_a, g_b, _chunk_perm()]
    if dep is not None:
        body = _drop_arg(body, len(operands))
        in_specs.append(DEP_SPEC)
        operands.append(dep)
    return pl.pallas_call(
        body, name="mix_fwd", grid=(s // HEAD,),
        out_shape=jax.ShapeDtypeStruct((s, 2 * w), BF16), in_specs=in_specs,
        out_specs=pl.BlockSpec((HEAD, 2 * w), lambda i: (i, 0)),
        scratch_shapes=[pltpu.VMEM((HEAD, w), F32), pltpu.VMEM((HEAD, w), BF16), pltpu.VMEM((HEAD, w), F32)],
        compiler_params=_params(("arbitrary",), 48),
    )(*operands)


def _mix_bwd(proj, dy, outs, lses, w_s, bst, ln_g, ln_b, g_a, g_b, w):
    s = proj.shape[0]
    n_groups = w // HEAD

    def body(au_ref, av_ref, az_ref, bz_ref, dy_ref, o1, o2, o3, l1, l2, l3, ws_ref, bst_ref,
             lng_ref, lnb_ref, ga_ref, gb_ref,
             dproj_ref, dyb_ref, tot_ref, dl_ref, dws_ref, dbst_ref, dlng_ref, dlnb_ref, dga_ref, dgb_ref,
             z_scr, ln_scr, yb_scr, dz_scr, dln_scr):
        i = pl.program_id(0)

        @pl.when(i == 0)
        def _():
            for t in (dws_ref, dbst_ref, dlng_ref, dlnb_ref, dga_ref, dgb_ref):
                t[...] = jnp.zeros_like(t)

        az = az_ref[...]
        lng = lng_ref[...]
        u, du_dau = _gelu_and_grad(au_ref[...])
        gv, dgv_dav = _gelu_and_grad(av_ref[...])
        u, xhat, rs = _mixer_a(u, gv, ws_ref, bst_ref, lng, lnb_ref[...], z_scr, ln_scr)
        z = z_scr[...]
        ya = u * z
        ra = _rms(ya)
        silu_a, dsilu_a = _silu_and_grad(az)
        dya_all = dy_ref[:, :w].astype(F32)
        na = ya * ra * ga_ref[...]
        dna = dya_all * silu_a
        dproj_ref[:, 2 * w:3 * w] = (dya_all * na * dsilu_a).astype(BF16)
        dga_ref[...] += jnp.sum(dna * ya * ra, axis=0, keepdims=True)
        dya = _rms_bwd(dna, ya, ra, ga_ref[...])
        dproj_ref[:, :w] = (dya * z * du_dau).astype(BF16)
        dz_scr[...] = (dya * u).astype(BF16)

        causal = _causal_mask()
        for g in range(n_groups):
            sl = slice(g * HEAD, (g + 1) * HEAD)
            wm = jnp.where(causal, ws_ref[g], 0.0).astype(BF16)
            dz = dz_scr[:, sl]
            dln_scr[:, sl] = _dot(wm, dz, TN_DIMS)
            dws_ref[g] += jnp.where(causal, _dot(dz, ln_scr[:, sl], NT_DIMS), 0.0)
            dbst_ref[:, g:g + 1] += jnp.sum(dz.astype(F32), axis=-1, keepdims=True)
        dln = dln_scr[...]
        dlng_ref[...] += jnp.sum(dln * xhat, axis=0, keepdims=True)
        dlnb_ref[...] += jnp.sum(dln, axis=0, keepdims=True)
        gy = dln * lng
        dgv = rs * (gy - jnp.mean(gy, axis=-1, keepdims=True)
                    - xhat * jnp.mean(gy * xhat, axis=-1, keepdims=True))
        dproj_ref[:, w:2 * w] = (dgv * dgv_dav).astype(BF16)
        dproj_ref[:, 3 * w:6 * w] = jnp.zeros((HEAD, 3 * w), BF16)

        tot_ref[...] = _merge_b((o1, o2, o3), (l1, l2, l3), yb_scr)
        yb = yb_scr[...]
        rb = _rms(yb)
        bz = bz_ref[...]
        silu_b, dsilu_b = _silu_and_grad(bz)
        dyb_all = dy_ref[:, w:].astype(F32)
        dnb = dyb_all * silu_b
        dproj_ref[:, 6 * w:] = (dyb_all * yb * rb * gb_ref[...] * dsilu_b).astype(BF16)
        dgb_ref[...] += jnp.sum(dnb * yb * rb, axis=0, keepdims=True)
        dyb = _rms_bwd(dnb, yb, rb, gb_ref[...])
        dyb_ref[...] = dyb
        prod = dyb * yb
        dl_ref[...] = _lane_pick(
            [jnp.sum(prod[:, h * HEAD:(h + 1) * HEAD], axis=-1, keepdims=True) for h in range(n_groups)], HEAD)

    seg = lambda j: pl.BlockSpec((HEAD, w), lambda i, j=j: (i, j))
    row_w = pl.BlockSpec((HEAD, w), lambda i: (i, 0))
    stat = pl.BlockSpec((HEAD, HEAD), lambda i: (i, 0))
    vec = pl.BlockSpec((1, w), lambda i: (0, 0))
    ws_spec = pl.BlockSpec((n_groups, HEAD, HEAD), lambda i: (0, 0, 0))
    bst_spec = pl.BlockSpec((HEAD, n_groups), lambda i: (0, 0))
    vec_shape = jax.ShapeDtypeStruct((1, w), F32)
    return pl.pallas_call(
        body, name="mix_bwd", grid=(s // HEAD,),
        out_shape=[jax.ShapeDtypeStruct((s, N_SEG * w), BF16), jax.ShapeDtypeStruct((s, w), F32),
                   jax.ShapeDtypeStruct((s, HEAD), F32), jax.ShapeDtypeStruct((s, HEAD), F32),
                   jax.ShapeDtypeStruct((n_groups, HEAD, HEAD), F32), jax.ShapeDtypeStruct((HEAD, n_groups), F32),
                   vec_shape, vec_shape, vec_shape, vec_shape],
        in_specs=[seg(0), seg(1), seg(2), seg(6), pl.BlockSpec((HEAD, 2 * w), lambda i: (i, 0)),
                  row_w, row_w, row_w, stat, stat, stat, ws_spec, bst_spec, vec, vec, vec, vec],
        out_specs=[pl.BlockSpec((HEAD, N_SEG * w), lambda i: (i, 0)), row_w, stat, stat,
                   ws_spec, bst_spec, vec, vec, vec, vec],
        scratch_shapes=[pltpu.VMEM((HEAD, w), F32), pltpu.VMEM((HEAD, w), BF16), pltpu.VMEM((HEAD, w), F32),
                        pltpu.VMEM((HEAD, w), BF16), pltpu.VMEM((HEAD, w), F32)],
        compiler_params=_params(("arbitrary",), 56),
    )(proj, proj, proj, proj, dy, *outs, *lses, w_s, bst, ln_g, ln_b, g_a, g_b)


def _local_step(x, p, tgt, small, wg, ex, while_last_travels=None):
    s, d = x.shape
    n, _, c_in = wg.buffers["w_in"].shape
    assert n == N_DEV
    d_in = n * c_in
    w = d_in // N_SEG
    n_heads = w // HEAD
    p_dim, c_up = wg.buffers["w_ple_up"].shape[1:]
    assert s % (HEAD * DILATIONS[-1]) == 0 and w % HEAD == 0 and d == n * c_up == 2 * w

    near = (2, 4)
    wg.start("gather_in_pair", ["w_in"], (1,))
    token = wg.start("gather_in_first", ["w_in"], ("first",))
    rest = ["w_out", "w_ple_gate", "w_ple_up"]

    hn = _rmsnorm_fwd(x, small["g_pre"], "pre_norm", True, dep=token)
    tm, tk = _tile(s, 1024), _tile(d, 2048)

    def in_proj(shards, name, carry, dep=None):
        return _matmul(
            hn, wg.buffers["w_in"], name=name, grid=(s // tm, len(shards), d // tk), dims=NN_DIMS,
            prefetch=jnp.stack(shards).astype(jnp.int32),
            a_spec=pl.BlockSpec((tm, tk), lambda i, j, k, sh: (i, k)),
            b_spec=pl.BlockSpec((None, tk, c_in), lambda i, j, k, sh: (sh[j], k, 0)),
            acc_shape=(tm, c_in), out_shapes=[jax.ShapeDtypeStruct((s, d_in), F32)],
            out_specs=[pl.BlockSpec((tm, c_in), lambda i, j, k, sh: (i, sh[j]))], carry=carry, dep=dep,
            vmem_mib=56)[0]

    me = wg.me
    core = me & 1
    first, second, far = me ^ (4 - 2 * core), me ^ (2 + 2 * core), me ^ 6
    proj = in_proj([me], "in_proj_own", None)
    token = wg.start("gather_in_second", ["w_in"], ("second",), after=[proj])
    wg.arrived("gather_in_pair", [token])
    proj = in_proj([me ^ 1], "in_proj_sibling", proj)
    buckets = jnp.stack([_rel_buckets(dil) for dil in DILATIONS])
    bias = _bias_build(small["rel_bias"], buckets, n_heads)
    ahead = [bias] + [wg.buffers[k] for k in rest]
    for tag, mine, from_sibling in (("first", first, second ^ 1), ("second", second, first ^ 1), ("far", far, far ^ 1)):
        token = wg.forward("gather_in_" + tag, [proj] + ahead)
        ahead = []
        if tag == "second":
            token = wg.relay_start("gather_in_far", ["w_in"], after=[token])
        elif tag == "far":
            token = wg.start("gather_rest", rest, (1,) + near, after=[token])
        proj = in_proj([mine], "in_proj_" + tag, proj, dep=token)
        wg.forwarded("gather_in_" + tag, [proj])
        proj = in_proj([from_sibling], "in_proj_%s_forwarded" % tag, proj)
    win_g = wg.buffers["w_in"]

    qn, kn = _qkv_prep(proj, small["g_q"], small["g_k"], w)
    token = wg.forward("gather_rest", [qn])
    token = wg.relay_start("gather_rest_far", rest, after=[token])
    outs, lses = [], []
    for c, dil in enumerate(DILATIONS):
        o, l = _attn_fwd(qn, kn, proj, bias[c], dil, "attn_fwd_d%d" % dil, dep=token)
        outs.append(o)
        lses.append(l)
    token = wg.forward("gather_rest_far", outs)

    ws_p = small["w_s"][:, CHUNK_ORDER][:, :, CHUNK_ORDER]
    bst = small["b_s"].T[CHUNK_ORDER]
    mix_args = (outs, lses, ws_p, bst, small["ln_v_g"], small["ln_v_b"], small["g_out_a"], small["g_out_b"], w)
    y = _mix_fwd(proj, *mix_args, dep=token)
    wg.forwarded("gather_rest", [y])
    wg.forwarded("gather_rest_far", [y])
    wout_g, wgate_g, wup_g = (wg.buffers[k] for k in rest)
    wout_f = wout_g.reshape(2 * w, d)
    wgate_f = wgate_g.reshape(d, d)

    tn = _tile(d, 1024)
    tk2 = _tile(2 * w, 2048)

    def resid_epilogue(acc, ex, outs_):
        outs_[0][...] = ex[0][...] + acc[...]

    h = _matmul(
        y, wout_f, name="out_proj", grid=(s // tm, d // tn, (2 * w) // tk2), dims=NN_DIMS,
        a_spec=pl.BlockSpec((tm, tk2), lambda i, j, k: (i, k)),
        b_spec=pl.BlockSpec((tk2, tn), lambda i, j, k: (k, j)),
        acc_shape=(tm, tn), out_shapes=[jax.ShapeDtypeStruct((s, d), F32)],
        out_specs=[pl.BlockSpec((tm, tn), lambda i, j, k: (i, j))],
        extra=(x,), extra_specs=(pl.BlockSpec((tm, tn), lambda i, j, k: (i, j)),),
        epilogue=resid_epilogue)[0]

    hn2 = _rmsnorm_fwd(h, small["g_ple"], "ple_norm", False)

    tmg = _tile(s, 512)

    def ple_epilogue(acc, ex, outs_):
        h_ref, p_ref, wup_ref, tgt_ref = ex
        dout_ref, dpre_ref, dup_ref, loss_ref = outs_
        gate = jax.nn.sigmoid(acc[...])
        up = _dot(p_ref[...].astype(BF16), wup_ref[...])
        err = h_ref[...] + gate * up - tgt_ref[...]
        dout = err * (1.0 / d)
        dout_ref[...] = dout
        dpre_ref[...] = (dout * up * gate * (1.0 - gate)).astype(BF16)
        dup_ref[...] = (dout * gate).astype(BF16)
        part = 0.5 * jnp.sum(err * err) * (1.0 / d)
        rr = lax.broadcasted_iota(jnp.int32, (8, HEAD), 0)
        cc = lax.broadcasted_iota(jnp.int32, (8, HEAD), 1)
        loss_ref[...] = jnp.where((rr == 0) & (cc == 0), part, 0.0)

    tile_ij = pl.BlockSpec((tmg, c_up), lambda i, j, k: (i, j))
    dout, dpre, dup, loss_parts = _matmul(
        hn2, wgate_f, name="ple_gate", grid=(s // tmg, n, 1), dims=NN_DIMS,
        a_spec=pl.BlockSpec((tmg, d), lambda i, j, k: (i, 0)),
        b_spec=pl.BlockSpec((d, c_up), lambda i, j, k: (0, j)),
        acc_shape=(tmg, c_up),
        out_shapes=[jax.ShapeDtypeStruct((s, d), F32), jax.ShapeDtypeStruct((s, d), BF16),
                    jax.ShapeDtypeStruct((s, d), BF16), jax.ShapeDtypeStruct((s // tmg * 8, n * HEAD), F32)],
        out_specs=[tile_ij, tile_ij, tile_ij, pl.BlockSpec((8, HEAD), lambda i, j, k: (i, j))],
        extra=(h, p, wup_g, tgt),
        extra_specs=(tile_ij, pl.BlockSpec((tmg, p_dim), lambda i, j, k: (i, 0)),
                     pl.BlockSpec((None, p_dim, c_up), lambda i, j, k: (j, 0, 0)), tile_ij),
        epilogue=ple_epilogue)
    loss = jnp.sum(loss_parts)

    tks = _tile(s, 2048)
    g_wup = _matmul(
        p, dup, name="grad_w_up", grid=(1, n, s // tks), dims=TN_DIMS,
        a_spec=pl.BlockSpec((tks, p_dim), lambda i, j, k: (k, 0)),
        b_spec=pl.BlockSpec((tks, c_up), lambda i, j, k: (k, j)),
        acc_shape=(p_dim, c_up), out_shapes=[jax.ShapeDtypeStruct((n, p_dim, c_up), BF16)],
        out_specs=[pl.BlockSpec((None, p_dim, c_up), lambda i, j, k: (j, 0, 0))])[0]

    def tn_matmul(a, b, name):
        m_, n_ = a.shape[1], b.shape[1]
        bm, bn = _tile(m_, 1024), _tile(n_, 1024)
        return _matmul(
            a, b, name=name, grid=(m_ // bm, n_ // bn, s // tks), dims=TN_DIMS,
            a_spec=pl.BlockSpec((tks, bm), lambda i, j, k: (k, i)),
            b_spec=pl.BlockSpec((tks, bn), lambda i, j, k: (k, j)),
            acc_shape=(bm, bn), out_shapes=[jax.ShapeDtypeStruct((m_, n_), BF16)],
            out_specs=[pl.BlockSpec((bm, bn), lambda i, j, k: (i, j))])[0]

    def nt_matmul(a, b, name, out_dtype, dep=None):
        k_, n_ = a.shape[1], b.shape[0]
        bm, bn, bk = _tile(s, 1024), _tile(n_, 1024), _tile(k_, 2048)
        return _matmul(
            a, b, name=name, grid=(s // bm, n_ // bn, k_ // bk), dims=NT_DIMS,
            a_spec=pl.BlockSpec((bm, bk), lambda i, j, k: (i, k)),
            b_spec=pl.BlockSpec((bn, bk), lambda i, j, k: (j, k)),
            acc_shape=(bm, bn), out_shapes=[jax.ShapeDtypeStruct((s, n_), out_dtype)],
            out_specs=[pl.BlockSpec((bm, bn), lambda i, j, k: (i, j))], dep=dep)[0]

    by_core = lambda g: g.reshape((N_CHIP, 2) + g.shape[-2:])
    g_wgate = tn_matmul(hn2, dpre, "grad_w_gate").reshape(wgate_g.shape)
    dhn2 = nt_matmul(dpre, wgate_f, "ple_gate_bwd", BF16)
    dh, dh_b, dh_bp, dg_ple = _rmsnorm_bwd(dhn2, h, small["g_ple"], dout, "ple_norm_bwd", False, True)
    g_wout = tn_matmul(y, dh_b, "grad_w_out").reshape(wout_g.shape)

    late = ("w_out", "w_ple_gate", "w_ple_up")
    late_parts = (g_wout, g_wgate, g_wup)
    token = ex.push_pairs("pair_late", [by_core(g) for g in late_parts])
    dy = nt_matmul(dh_bp, wout_f, "out_proj_bwd", BF16, dep=token)
    (dproj, dyb, lse_tot, delta, dws, dbst, dlng, dlnb, dga, dgb) = _mix_bwd(proj, dy, *mix_args)
    both_columns, from_sibling = ex.pairs_done("pair_late", [dproj])
    pair_sums = [_pair_add(mine.reshape((N_DEV,) + mine.shape[-2:]), theirs, "pair_add_" + k, ex.core)
                 for k, mine, theirs in zip(late, both_columns, from_sibling)]
    token = ex.push_chips("chip_late", pair_sums)

    running, dss = None, []
    for c, dil in enumerate(DILATIONS):
        *running, ds = _attn_bwd(qn, kn, proj, dyb, lse_tot, delta, bias[c], dil, "attn_bwd_d%d" % dil,
                                 running=running, dep=token if c == 0 else None)
        dss.append(ds)
    d_rel = _bias_grad(jnp.stack(dss), buckets, n_heads)
    dproj, dgq, dgk = _qkv_bwd(dproj, proj, *running, small["g_q"], small["g_k"], w)
    pair_sums, landed, _ = ex.chips_done("chip_late", [dproj])
    delivered = {k: (mine, theirs) for k, mine, theirs in zip(late, pair_sums, landed)}

    token_row = np.argsort(CHUNK_ORDER)
    dws = dws[:, token_row][:, :, token_row]
    dbst = dbst[token_row]
    small_grads = {
        "w_s": dws, "b_s": dbst.T, "ln_v_g": dlng, "ln_v_b": dlnb, "g_q": dgq, "g_k": dgk,
        "rel_bias": d_rel, "g_out_a": dga, "g_out_b": dgb, "g_ple": dg_ple,
    }

    bm = _tile(d, 1024)

    def grad_w_in(core, name, dep=None):
        return _matmul(
            hn, dproj, name=name, grid=(d // bm, N_CHIP, s // tks), dims=TN_DIMS, prefetch=core.reshape(1),
            a_spec=pl.BlockSpec((tks, bm), lambda i, j, k, core_ref: (k, i)),
            b_spec=pl.BlockSpec((tks, c_in), lambda i, j, k, core_ref: (k, 2 * j + core_ref[0])),
            acc_shape=(bm, c_in), out_shapes=[jax.ShapeDtypeStruct((N_CHIP, d, c_in), BF16)],
            out_specs=[pl.BlockSpec((None, bm, c_in), lambda i, j, k, core_ref: (j, i, 0))], dep=dep)[0]

    for_sibling = grad_w_in(1 - ex.core, "grad_w_in_sibling")
    token = ex.push_pairs("pair_in", [for_sibling])
    mine = grad_w_in(ex.core, "grad_w_in_mine", dep=token)
    _, from_sibling = ex.pairs_done("pair_in", [mine])
    pair_sum = _pair_add(mine, from_sibling[0], "pair_add_w_in")
    token = ex.push_chips("chip_in", [pair_sum], _pack_small(small_grads, SMALL_EARLY))

    dhn = _matmul(
        dproj, win_g, name="in_proj_bwd", grid=(s // tm, d // tn, n // 2), dims=NT_DIMS,
        a_spec=pl.BlockSpec((tm, 2 * c_in), lambda i, j, k: (i, k)),
        b_spec=pl.BlockSpec((2, tn, c_in), lambda i, j, k: (k, j, 0)),
        acc_shape=(tm, tn), out_shapes=[jax.ShapeDtypeStruct((s, d), BF16)],
        out_specs=[pl.BlockSpec((tm, tn), lambda i, j, k: (i, j))], dep=token, vmem_mib=56)[0]
    grad_x, dg_pre = _rmsnorm_bwd(dhn, x, small["g_pre"], dh, "pre_norm_bwd", True, False)
    extra = while_last_travels(token, delivered, dg_pre) if while_last_travels is not None else []
    pair_sums, landed, slabs = ex.chips_done("chip_in", [grad_x] + list(extra))
    delivered["w_in"] = (pair_sums[0], landed[0])
    small_grads["g_pre"] = dg_pre
    return loss, grad_x, small_grads, delivered, slabs, extra


SMALL_EARLY = ("w_s", "b_s", "ln_v_g", "ln_v_b", "g_q", "g_k", "rel_bias", "g_out_a", "g_out_b", "g_ple")
SMALL_LAST = ("g_pre",)
SMALL_NAMES = SMALL_LAST + SMALL_EARLY


def _pack_small(tree, names):
    parts = []
    for name in names:
        flat = tree[name].astype(F32).reshape(-1)
        pad = (-flat.shape[0]) % HEAD
        parts.append(jnp.pad(flat, (0, pad)) if pad else flat)
    slab = jnp.concatenate(parts).reshape(-1, HEAD)
    pad_rows = (-slab.shape[0]) % 8
    return jnp.pad(slab, ((0, pad_rows), (0, 0))) if pad_rows else slab


def _unpack_small(slab, like, names):
    flat = slab.reshape(-1)
    out, off = {}, 0
    for name in names:
        size = like[name].size
        out[name] = flat[off:off + size].reshape(like[name].shape)
        off += size + (-size) % HEAD
    return out


def _peer(k):
    x, y, c = (lax.axis_index(a) for a in AXES)
    if k == "first":
        px, py, pc = x ^ (1 - c), y ^ c, c
    elif k == "second":
        px, py, pc = x ^ c, y ^ (1 - c), c
    else:
        bits = ((k >> 2) & 1, (k >> 1) & 1, k & 1)
        px, py, pc = (1 - v if b else v for v, b in zip((x, y, c), bits))
    return (px, py, pc), 4 * px + 2 * py + pc


def _my_index():
    x, y, c = (lax.axis_index(a) for a in AXES)
    return 4 * x + 2 * y + c


N_CHIP = 4
HBM_SPEC = pl.BlockSpec(memory_space=pl.ANY)


def _remote(src, dst, send_sem, recv_sem, peer):
    return pltpu.make_async_remote_copy(src_ref=src, dst_ref=dst, send_sem=send_sem, recv_sem=recv_sem,
                                        device_id=peer, device_id_type=pl.DeviceIdType.MESH)


SEM_SPEC = pl.BlockSpec(memory_space=pltpu.SEMAPHORE)
HBM_ONLY = pl.BlockSpec(memory_space=pltpu.HBM)
DATAFLOW = pltpu.SideEffectType.DATAFLOW_SIDE_EFFECTING


def _comm_call(name, arrays, *, wait=None, start=None, after=()):
    n, n_after = len(arrays), len(after)

    def body(*refs):
        ins = refs[:n]
        pos = n
        if wait is not None:
            for cp in wait[2](ins, refs[pos], refs[pos + 1]):
                cp.wait()
            pos += 2
        outs = refs[pos + n_after:]
        if start is not None:
            for cp in start[1](ins, outs[0], outs[1]):
                cp.start()
        outs[-1][...] = jnp.zeros_like(outs[-1])

    operands = [pltpu.with_memory_space_constraint(a, pltpu.HBM) for a in arrays]
    in_specs = [HBM_ONLY] * n
    if wait is not None:
        operands += [wait[0], wait[1]]
        in_specs += [SEM_SPEC, SEM_SPEC]
    operands += list(after)
    in_specs += [HBM_SPEC] * n_after
    out_shape, out_specs = [], []
    if start is not None:
        out_shape += [pltpu.SemaphoreType.DMA((start[0],))] * 2
        out_specs += [SEM_SPEC, SEM_SPEC]
    first = len(out_shape)
    out_shape += [pltpu.HBM(a.shape, a.dtype) for a in arrays] + [jax.ShapeDtypeStruct((SUB, HEAD), F32)]
    out_specs += [HBM_ONLY] * n + [pl.BlockSpec(memory_space=pltpu.VMEM)]
    res = pl.pallas_call(
        body, name=name, out_shape=tuple(out_shape), in_specs=tuple(in_specs), out_specs=tuple(out_specs),
        input_output_aliases={i: first + i for i in range(n)},
        compiler_params=pltpu.CompilerParams(has_side_effects=DATAFLOW),
    )(*operands)
    sems = (res[0], res[1]) if start is not None else None
    return list(res[first:first + n]), sems, res[-1]


class _GradExchange:
    def __init__(self):
        x, y, c = (lax.axis_index(a) for a in AXES)
        self.core = c.astype(jnp.int32)
        self.chip = (2 * x + y).astype(jnp.int32)
        self.pending = {}

    def _pair_copies(self, n_arr):
        def make(refs, send_sems, recv_sems):
            sibling, _ = _peer(1)
            other = 1 - lax.axis_index("c")
            srcs, lands = refs[:n_arr], refs[n_arr:]
            pick = lambda ref, ch: ref.at[ch, other] if len(ref.shape) == 4 else ref.at[ch]
            return [_remote(pick(srcs[a], ch), lands[a].at[ch], send_sems.at[a * N_CHIP + ch],
                            recv_sems.at[a * N_CHIP + ch], sibling)
                    for a in range(n_arr) for ch in range(N_CHIP)]
        return make

    def _chip_copies(self, n_arr, with_slab):
        def make(refs, send_sems, recv_sems):
            x, y = lax.axis_index("x"), lax.axis_index("y")
            my_chip = 2 * x + y
            srcs, lands = refs[:n_arr], refs[n_arr:2 * n_arr]
            copies = []
            for j, k in enumerate((2, 4, 6)):
                peer, peer_idx = _peer(k)
                for a in range(n_arr):
                    copies.append(_remote(srcs[a].at[peer_idx // 2], lands[a].at[my_chip],
                                          send_sems.at[3 * a + j], recv_sems.at[3 * a + j], peer))
            if with_slab:
                slab, slab_land = refs[2 * n_arr], refs[2 * n_arr + 1]
                for k in range(1, N_DEV):
                    peer, _ = _peer(k)
                    copies.append(_remote(slab, slab_land.at[_my_index()], send_sems.at[3 * n_arr + k - 1],
                                          recv_sems.at[3 * n_arr + k - 1], peer))
            return copies
        return make

    def push_pairs(self, tag, for_sibling):
        n_arr = len(for_sibling)
        lands = [lax.empty((N_CHIP,) + a.shape[-2:], a.dtype) for a in for_sibling]
        make = self._pair_copies(n_arr)
        arrays, sems, token = _comm_call(tag + "_start", list(for_sibling) + lands, start=(n_arr * N_CHIP, make))
        self.pending[tag] = (arrays, sems, make, n_arr)
        return token

    def pairs_done(self, tag, after):
        arrays, sems, make, n_arr = self.pending.pop(tag)
        arrays, _, _ = _comm_call(tag + "_wait", arrays, wait=(sems[0], sems[1], make), after=after)
        return arrays[:n_arr], arrays[n_arr:]

    def push_chips(self, tag, pair_sums, slab=None):
        n_arr = len(pair_sums)
        arrays = list(pair_sums) + [lax.empty(a.shape, a.dtype) for a in pair_sums]
        n_copies = 3 * n_arr
        if slab is not None:
            arrays += [slab, lax.empty((N_DEV,) + slab.shape, slab.dtype)]
            n_copies += N_DEV - 1
        make = self._chip_copies(n_arr, slab is not None)
        arrays, sems, token = _comm_call(tag + "_start", arrays, start=(n_copies, make))
        self.pending[tag] = (arrays, sems, make, n_arr)
        return token

    def chips_done(self, tag, after):
        arrays, sems, make, n_arr = self.pending.pop(tag)
        arrays, _, _ = _comm_call(tag + "_wait", arrays, wait=(sems[0], sems[1], make), after=after)
        return arrays[:n_arr], arrays[n_arr:2 * n_arr], arrays[2 * n_arr:]


def _cast_place(w, name):
    r, c = w.shape
    tr = r if r * c <= MIB else 1 << ((MIB // c).bit_length() - 1)
    assert r % tr == 0

    def body(me_ref, w_ref, o_ref):
        o_ref[...] = w_ref[...].astype(BF16)

    return pl.pallas_call(
        body, name=name, out_shape=jax.ShapeDtypeStruct((N_DEV, r, c), BF16),
        grid_spec=pltpu.PrefetchScalarGridSpec(
            num_scalar_prefetch=1, grid=(r // tr,),
            in_specs=[pl.BlockSpec((tr, c), lambda i, me_ref: (i, 0))],
            out_specs=pl.BlockSpec((None, tr, c), lambda i, me_ref: (me_ref[0], i, 0))),
        compiler_params=_params(("arbitrary",), 40),
    )(_my_index().astype(jnp.int32).reshape(1), w)


class _WeightGather:
    CHIPS = (2, 4, 6)

    def __init__(self, buffers):
        self.buffers = dict(buffers)
        self.pending = {}
        self.me = _my_index().astype(jnp.int32)

    def _own_slot_to(self, peers):
        def make(refs, send_sems, recv_sems):
            me = _my_index()
            return [_remote(ref.at[me], ref.at[me], send_sems.at[len(peers) * a + j],
                            recv_sems.at[len(peers) * a + j], _peer(k)[0])
                    for a, ref in enumerate(refs) for j, k in enumerate(peers)]
        return make

    def _forward_from(self, chips):
        def make(refs, send_sems, recv_sems):
            sibling, _ = _peer(1)
            copies = []
            for a, ref in enumerate(refs):
                for j, k in enumerate(chips):
                    slot = ref.at[_peer(k)[1]]
                    copies.append(_remote(slot, slot, send_sems.at[len(chips) * a + j],
                                          recv_sems.at[len(chips) * a + j], sibling))
            return copies
        return make

    def _run(self, call, names, **kw):
        arrays, sems, token = _comm_call(call, [self.buffers[k] for k in names], **kw)
        self.buffers.update(zip(names, arrays))
        return sems, token

    def start(self, tag, names, peers, after=()):
        make = self._own_slot_to(peers)
        sems, token = self._run(tag + "_start", names, start=(len(names) * len(peers), make), after=after)
        self.pending[tag] = (names, sems, make, peers)
        return token

    @staticmethod
    def _relay(refs, send_sems, recv_sems):
        x, y, c = (lax.axis_index(a) for a in AXES)
        peer = (x ^ (1 - c), y ^ c, c)
        slot = _my_index() ^ (2 + 2 * c)
        return [_remote(ref.at[slot], ref.at[slot], send_sems.at[a], recv_sems.at[a], peer)
                for a, ref in enumerate(refs)]

    def relay_start(self, tag, names, after=()):
        sems, token = self._run(tag + "_start", names, start=(len(names), self._relay), after=after)
        self.pending[tag] = (names, sems, self._relay, (6,))
        return token

    def arrived(self, tag, after):
        names, sems, make, _ = self.pending.pop(tag)
        self._run(tag + "_wait", names, wait=(sems[0], sems[1], make), after=after)

    def forward(self, tag, after):
        names, sems, make, peers = self.pending.pop(tag)
        chips = tuple(k for k in peers if k != 1)
        onward = self._forward_from(chips)
        new_sems, token = self._run(tag + "_forward", names, wait=(sems[0], sems[1], make),
                                    start=(len(chips) * len(names), onward), after=after)
        self.pending[tag + "/fwd"] = (names, new_sems, onward)
        return token

    def forwarded(self, tag, after):
        names, sems, make = self.pending.pop(tag + "/fwd")
        self._run(tag + "_done", names, wait=(sems[0], sems[1], make), after=after)


def _pair_add(mine, theirs, name, core=None):
    _, r, c_dim = theirs.shape
    tr = r if r * c_dim <= MIB else 1 << ((MIB // c_dim).bit_length() - 1)
    assert r % tr == 0
    stride = 1 if core is None else 2
    offset = jnp.zeros((1,), jnp.int32) if core is None else core.reshape(1)

    def body(off_ref, a_ref, b_ref, o_ref):
        o_ref[...] = (a_ref[...].astype(F32) + b_ref[...].astype(F32)).astype(BF16)

    blk = (None, tr, c_dim)
    return pl.pallas_call(
        body, name=name, out_shape=jax.ShapeDtypeStruct(theirs.shape, BF16),
        grid_spec=pltpu.PrefetchScalarGridSpec(
            num_scalar_prefetch=1, grid=(N_CHIP, r // tr),
            in_specs=[pl.BlockSpec(blk, lambda ch, i, off_ref: (stride * ch + off_ref[0], i, 0)),
                      pl.BlockSpec(blk, lambda ch, i, off_ref: (ch, i, 0))],
            out_specs=pl.BlockSpec(blk, lambda ch, i, off_ref: (ch, i, 0))),
        compiler_params=_params(("arbitrary", "arbitrary"), 40),
    )(offset, mine, theirs)


def _slab_exchange(slab, name):
    def body(slab_in, slab_out, send_sems, recv_sems, local_sem):
        me = _my_index()
        local = pltpu.make_async_copy(slab_in, slab_out.at[me], local_sem)
        local.start()
        sends = []
        for k in range(1, N_DEV):
            peer, _ = _peer(k)
            sends.append(_remote(slab_in, slab_out.at[me], send_sems.at[k - 1], recv_sems.at[k - 1], peer))
        for cp in sends:
            cp.start()
        for k in range(1, N_DEV):
            peer, peer_idx = _peer(k)
            slot = slab_out.at[peer_idx]
            _remote(slot, slot, send_sems.at[k - 1], recv_sems.at[k - 1], peer).wait_recv()
        for cp in sends:
            cp.wait_send()
        local.wait()

    return pl.pallas_call(
        body, name=name, out_shape=jax.ShapeDtypeStruct((N_DEV,) + slab.shape, slab.dtype),
        in_specs=[HBM_SPEC], out_specs=HBM_SPEC,
        scratch_shapes=[pltpu.SemaphoreType.DMA((N_DEV - 1,)), pltpu.SemaphoreType.DMA((N_DEV - 1,)),
                        pltpu.SemaphoreType.DMA],
        compiler_params=pltpu.CompilerParams(has_side_effects=True),
    )(slab)


def _adamw_math(w, g, m, v):
    m = ADAM_B1 * m + (1.0 - ADAM_B1) * g
    v = ADAM_B2 * v + (1.0 - ADAM_B2) * (g * g)
    m_hat = m / (1.0 - ADAM_B1 ** ADAM_STEP)
    v_hat = v / (1.0 - ADAM_B2 ** ADAM_STEP)
    delta = -ADAM_LR * (m_hat / (jnp.sqrt(v_hat) + ADAM_EPS) + ADAM_WD * w)
    return delta, m, v


def _adamw(parts, own, place, w, m, v, name, dep=None):
    n_parts = parts.shape[0]
    r, c = w.shape
    budget = 280 * 1024
    tr = r if r * c <= budget else 1 << ((budget // c).bit_length() - 1)
    assert r % tr == 0

    def body(place_ref, p_ref, own_ref, w_ref, m_ref, v_ref, g_ref, d_ref, nm_ref, nv_ref):
        mine = own_ref[...].astype(F32)
        g = None
        for i in range(n_parts):
            term = jnp.where(place_ref[0] == i, mine, p_ref[i].astype(F32))
            g = term if g is None else g + term
        delta, nm, nv = _adamw_math(w_ref[...], g, m_ref[...], v_ref[...])
        g_ref[...] = g
        d_ref[...] = delta
        nm_ref[...] = nm
        nv_ref[...] = nv

    blk = pl.BlockSpec((tr, c), lambda i, place_ref: (i, 0))
    shape = jax.ShapeDtypeStruct((r, c), F32)
    in_specs = [pl.BlockSpec((n_parts, tr, c), lambda i, place_ref: (0, i, 0)),
                pl.BlockSpec((None, tr, c), lambda i, place_ref: (place_ref[1], i, 0)), blk, blk, blk]
    operands = [parts, own, w, m, v]
    if dep is not None:
        body = _drop_arg(body, 1 + len(operands))
        in_specs.append(pl.BlockSpec((SUB, HEAD), lambda i, place_ref: (0, 0)))
        operands.append(dep)
    return pl.pallas_call(
        body, name=name, out_shape=[shape] * 4,
        grid_spec=pltpu.PrefetchScalarGridSpec(
            num_scalar_prefetch=1, grid=(r // tr,), in_specs=in_specs, out_specs=[blk] * 4),
        compiler_params=_params(("arbitrary",), 48),
    )(place, *operands)


def kernel(x, p, g_pre, w_in, w_s, b_s, ln_v_g, ln_v_b, g_q, g_k, rel_bias, g_out_a, g_out_b, w_out, g_ple, w_ple_gate, w_ple_up, loss_target, m_g_pre, m_w_in, m_w_s, m_b_s, m_ln_v_g, m_ln_v_b, m_g_q, m_g_k, m_rel_bias, m_g_out_a, m_g_out_b, m_w_out, m_g_ple, m_w_ple_gate, m_w_ple_up, v_g_pre, v_w_in, v_w_s, v_b_s, v_ln_v_g, v_ln_v_b, v_g_q, v_g_k, v_rel_bias, v_g_out_a, v_g_out_b, v_w_out, v_g_ple, v_w_ple_gate, v_w_ple_up):
    args = dict(locals())
    small = {"g_pre": g_pre, "w_s": w_s[0], "b_s": b_s[0], "ln_v_g": ln_v_g, "ln_v_b": ln_v_b, "g_q": g_q,
             "g_k": g_k, "rel_bias": rel_bias, "g_out_a": g_out_a, "g_out_b": g_out_b, "g_ple": g_ple}
    big_names = ("w_in", "w_out", "w_ple_gate", "w_ple_up")
    big = {k: args[k][0] for k in big_names}

    wg = _WeightGather({k: _cast_place(big[k], "place_" + k) for k in big_names})
    ex = _GradExchange()
    results = {}

    def big_adamw(k, delivered, dep=None):
        mine, theirs = delivered[k]
        place = jnp.stack([ex.chip, ex.chip])
        return _adamw(theirs, mine, place, big[k], args["m_" + k][0], args["v_" + k][0], "adamw_" + k, dep=dep)

    squeeze = lambda t: {k: (t[k][0] if k in ("w_s", "b_s") else t[k]) for k in SMALL_NAMES}
    small_m = squeeze({k: args["m_" + k] for k in SMALL_NAMES})
    small_v = squeeze({k: args["v_" + k] for k in SMALL_NAMES})
    slab_place = jnp.stack([_my_index().astype(jnp.int32), jnp.zeros((), jnp.int32)])

    def small_adamw(names_, parts, own, call_name):
        packed = _adamw(parts, own[None], slab_place, _pack_small(small, names_), _pack_small(small_m, names_),
                        _pack_small(small_v, names_), call_name)
        for idx in range(4):
            tree = _unpack_small(packed[idx], small, names_)
            for k in names_:
                results.setdefault(k, [None] * 4)[idx] = tree[k].reshape(args[k].shape)
        return packed[0]

    def while_last_travels(token, delivered, dg_pre):
        done = []
        for k in big_names[1:]:
            results[k] = big_adamw(k, delivered, dep=token)
            done.append(results[k][0])
        last_slab = _pack_small({"g_pre": dg_pre}, SMALL_LAST)
        done.append(small_adamw(SMALL_LAST, _slab_exchange(last_slab, "last_exchange"), last_slab, "adamw_last"))
        return done

    loss, grad_x, small_parts, delivered, slabs, _ = _local_step(
        x[0], p[0, 0], loss_target[0], small, wg, ex, while_last_travels)
    results["w_in"] = big_adamw("w_in", delivered)
    for k in big_names:
        results[k] = [t[None] for t in results[k]]
    small_adamw(SMALL_EARLY, slabs[1], slabs[0], "adamw_small")

    names = ("g_pre", "w_in", "w_s", "b_s", "ln_v_g", "ln_v_b", "g_q", "g_k", "rel_bias", "g_out_a", "g_out_b",
             "w_out", "g_ple", "w_ple_gate", "w_ple_up")
    total = lax.psum(loss, AXES)
    out = [total, grad_x[None]]
    for idx in range(4):
        out += [results[k][idx] for k in names]
    return tuple(out)
```

```python
import math

import numpy as np
import jax
import jax.numpy as jnp
from jax import lax
from jax.experimental import pallas as pl
from jax.experimental.pallas import tpu as pltpu

F32 = jnp.float32
BF16 = jnp.bfloat16
EPS = 1e-6
NEG_INF = -1e30
HEAD = 128
DILATIONS = (1, 4, 16)
NUM_BUCKETS = 32
MAX_DISTANCE = 2048
N_SEG = 7
ADAM_LR = 0.001
ADAM_B1 = 0.9
ADAM_B2 = 0.999
ADAM_EPS = 1e-08
ADAM_WD = 0.01
ADAM_STEP = 10
AXES = ("x", "y", "c")
N_DEV = 8
MIB = 1 << 20

SUB = 8

CHUNK_ORDER = np.array([16 * (r % SUB) + r // SUB for r in range(HEAD)])
BLOCK_ORDER = {
    1: CHUNK_ORDER,
    4: np.array([32 * (r // 32) + 4 * (r % SUB) + (r // SUB) % 4 for r in range(HEAD)]),
    16: np.arange(HEAD),
}

NT_DIMS = (((1,), (1,)), ((), ()))
TN_DIMS = (((0,), (0,)), ((), ()))
NN_DIMS = (((1,), (0,)), ((), ()))


def _params(semantics, vmem_mib):
    return pltpu.CompilerParams(dimension_semantics=semantics, vmem_limit_bytes=vmem_mib * MIB)


def _gelu(a):
    return 0.5 * a * (1.0 + lax.erf(a * (2.0 ** -0.5)))


def _gelu_and_grad(a):
    cdf = 0.5 * (1.0 + lax.erf(a * (2.0 ** -0.5)))
    return a * cdf, cdf + a * jnp.exp(-0.5 * a * a) * ((2.0 * math.pi) ** -0.5)


def _silu_and_grad(a):
    s = jax.nn.sigmoid(a)
    return a * s, s * (1.0 + a * (1.0 - s))


def _rms(v):
    return lax.rsqrt(jnp.mean(v * v, axis=-1, keepdims=True) + EPS)


def _rms_bwd(dy, v, r, g):
    gy = dy * g
    return r * gy - v * (r * r * r) * jnp.mean(gy * v, axis=-1, keepdims=True)


def _dot(a, b, dims=NN_DIMS):
    return lax.dot_general(a, b, dims, preferred_element_type=F32)


def _lane_pick(cols, width):
    rows = cols[0].shape[0]
    lane = lax.broadcasted_iota(jnp.int32, (rows, width), 1)
    out = jnp.zeros((rows, width), F32)
    for h, col in enumerate(cols):
        out = jnp.where(lane == h, col, out)
    return out


def _chunk_perm():
    return jnp.asarray(np.eye(HEAD, dtype=np.float32)[CHUNK_ORDER], BF16)


def _unpermute_f32(p, v):
    hi = v.astype(BF16)
    rest = v - hi.astype(F32)
    mid = rest.astype(BF16)
    lo = (rest - mid.astype(F32)).astype(BF16)
    return _dot(p, hi, TN_DIMS) + _dot(p, mid, TN_DIMS) + _dot(p, lo, TN_DIMS)


def _rmsnorm_fwd(x, g, name, permute, dep=None):
    s, d = x.shape
    tm = HEAD

    def body(x_ref, g_ref, p_ref, o_ref):
        v = x_ref[...]
        out = (v * _rms(v) * g_ref[...]).astype(BF16)
        if permute:
            out = _dot(p_ref[...], out).astype(BF16)
        o_ref[...] = out

    in_specs = [pl.BlockSpec((tm, d), lambda i: (i, 0)), pl.BlockSpec((1, d), lambda i: (0, 0)),
                pl.BlockSpec((HEAD, HEAD), lambda i: (0, 0))]
    operands = [x, g, _chunk_perm()]
    if dep is not None:
        body = _drop_arg(body, len(operands))
        in_specs.append(DEP_SPEC)
        operands.append(dep)
    return pl.pallas_call(
        body, name=name, grid=(s // tm,),
        out_shape=jax.ShapeDtypeStruct((s, d), BF16), in_specs=in_specs,
        out_specs=pl.BlockSpec((tm, d), lambda i: (i, 0)),
        compiler_params=_params(("arbitrary",), 40),
    )(*operands)


def _rmsnorm_bwd(dy, v, g, res, name, dy_permuted, with_bf16):
    s, d = v.shape
    tm = HEAD
    perm = _chunk_perm()

    def body(dy_ref, v_ref, g_ref, res_ref, p_ref, *outs):
        dx_ref, dg_ref = outs[0], outs[-1]
        i = pl.program_id(0)
        vv, dyv = v_ref[...], dy_ref[...]
        if dy_permuted:
            dyv = _unpermute_f32(p_ref[...], dyv) if dyv.dtype == F32 else _dot(p_ref[...], dyv, TN_DIMS)
        dyv = dyv.astype(F32)
        r = _rms(vv)
        dx = res_ref[...] + _rms_bwd(dyv, vv, r, g_ref[...])
        dx_ref[...] = dx
        if with_bf16:
            dxb = dx.astype(BF16)
            outs[1][...] = dxb
            outs[2][...] = _dot(p_ref[...], dxb).astype(BF16)

        @pl.when(i == 0)
        def _():
            dg_ref[...] = jnp.zeros_like(dg_ref)

        dg_ref[...] += jnp.sum(dyv * vv * r, axis=0, keepdims=True)

    row = pl.BlockSpec((tm, d), lambda i: (i, 0))
    vec = pl.BlockSpec((1, d), lambda i: (0, 0))
    shapes = [jax.ShapeDtypeStruct((s, d), F32)]
    specs = [row]
    if with_bf16:
        shapes += [jax.ShapeDtypeStruct((s, d), BF16)] * 2
        specs += [row, row]
    shapes.append(jax.ShapeDtypeStruct((1, d), F32))
    specs.append(vec)
    return pl.pallas_call(
        body, name=name, grid=(s // tm,), out_shape=shapes,
        in_specs=[row, row, vec, row, pl.BlockSpec((HEAD, HEAD), lambda i: (0, 0))], out_specs=specs,
        compiler_params=_params(("arbitrary",), 40),
    )(dy, v, g, res, perm)


DEP_SPEC = pl.BlockSpec((SUB, HEAD), lambda *_: (0, 0))


def _drop_arg(body, pos):
    return lambda *refs: body(*refs[:pos], *refs[pos + 1:])


def _matmul(a, b, *, name, grid, a_spec, b_spec, dims, acc_shape, out_shapes, out_specs,
            extra=(), extra_specs=(), epilogue=None, vmem_mib=48, dep=None, prefetch=None, carry=None):
    nk = grid[2]
    n_user = len(extra)
    for unread, spec in ((dep, DEP_SPEC), (carry, HBM_SPEC)):
        if unread is not None:
            extra, extra_specs = tuple(extra) + (unread,), tuple(extra_specs) + (spec,)
    n_extra, n_out = len(extra), len(out_shapes)
    n_pre = 0 if prefetch is None else 1
    aliases = {} if carry is None else {n_pre + 2 + n_extra - 1: 0}

    def body(*refs):
        refs = refs[n_pre:]
        a_ref, b_ref = refs[0], refs[1]
        ex = refs[2:2 + n_user]
        outs = refs[2 + n_extra:2 + n_extra + n_out]
        acc = refs[-1]
        k = pl.program_id(2)
        av = a_ref[...]
        if av.dtype != BF16:
            av = av.astype(BF16)
        if nk == 1 and epilogue is None:
            outs[0][...] = _dot(av, b_ref[...], dims).astype(outs[0].dtype)
            return

        @pl.when(k == 0)
        def _():
            acc[...] = jnp.zeros_like(acc)

        if len(b_ref.shape) == 3:
            span = b_ref.shape[2]
            acc[...] += sum(_dot(av[:, g * span:(g + 1) * span], b_ref[g], dims) for g in range(b_ref.shape[0]))
        else:
            acc[...] += _dot(av, b_ref[...], dims)

        @pl.when(k == nk - 1)
        def _():
            if epilogue is None:
                outs[0][...] = acc[...].astype(outs[0].dtype)
            else:
                epilogue(acc, ex, outs)

    scratch = [pltpu.VMEM(acc_shape, F32)]
    params = _params(("parallel", "parallel", "arbitrary"), vmem_mib)
    if prefetch is None:
        return pl.pallas_call(
            body, name=name, grid=grid, out_shape=list(out_shapes),
            in_specs=[a_spec, b_spec, *extra_specs], out_specs=list(out_specs),
            scratch_shapes=scratch, compiler_params=params, input_output_aliases=aliases,
        )(a, b, *extra)
    return pl.pallas_call(
        body, name=name, out_shape=list(out_shapes),
        grid_spec=pltpu.PrefetchScalarGridSpec(
            num_scalar_prefetch=1, grid=grid, in_specs=[a_spec, b_spec, *extra_specs],
            out_specs=list(out_specs), scratch_shapes=scratch),
        compiler_params=params, input_output_aliases=aliases,
    )(prefetch, a, b, *extra)


def _tile(n, want):
    t = min(n, want)
    while n % t:
        t //= 2
    return t


def _rel_buckets(dil):
    order = BLOCK_ORDER[dil]
    qi = jnp.asarray(HEAD + order)
    kj = jnp.asarray(np.concatenate([order, HEAD + order]))
    delta = qi[:, None] - kj[None, :]
    band = (delta >= 0) & (delta <= HEAD)
    dist = jnp.clip(delta, 0, None) * dil
    max_exact = NUM_BUCKETS // 2
    dd = jnp.maximum(dist, 1).astype(F32)
    large = max_exact + (jnp.log(dd / max_exact) / math.log(MAX_DISTANCE / max_exact)
                         * (NUM_BUCKETS - max_exact)).astype(jnp.int32)
    large = jnp.minimum(large, NUM_BUCKETS - 1)
    bucket = jnp.where(dist < max_exact, dist, large)
    return jnp.where(band, bucket, -1).astype(jnp.int32)


def _bias_build(rel_bias, buckets, n_heads):
    nd = buckets.shape[0]

    def body(rb_ref, bk_ref, o_ref):
        for c in range(nd):
            def per_head(h, carry, c=c):
                bk = bk_ref[c]
                acc = jnp.where(bk < 0, NEG_INF, 0.0).astype(F32)
                for b in range(NUM_BUCKETS):
                    acc = jnp.where(bk == b, rb_ref[b, h], acc)
                o_ref[c, h] = acc
                return carry

            lax.fori_loop(0, n_heads, per_head, 0)

    return pl.pallas_call(
        body, name="bias_build",
        out_shape=jax.ShapeDtypeStruct((nd, n_heads, HEAD, 2 * HEAD), F32),
        in_specs=[pl.BlockSpec(memory_space=pltpu.SMEM), pl.BlockSpec(memory_space=pltpu.VMEM)],
        out_specs=pl.BlockSpec(memory_space=pltpu.VMEM),
    )(rel_bias, buckets)


def _bias_grad(ds_all, buckets, n_heads):
    nd = buckets.shape[0]
    pairs = HEAD * 2 * HEAD

    def body(ds_ref, bk_ref, o_ref):
        rows = lax.broadcasted_iota(jnp.int32, (NUM_BUCKETS, pairs), 0)
        tot = jnp.zeros((n_heads, NUM_BUCKETS), F32)
        for c in range(nd):
            onehot = (rows == bk_ref[c]).astype(BF16)
            ds = ds_ref[c]
            hi = ds.astype(BF16)
            lo = (ds - hi.astype(F32)).astype(BF16)
            tot = tot + _dot(hi, onehot, NT_DIMS) + _dot(lo, onehot, NT_DIMS)
        o_ref[...] = tot

    out = pl.pallas_call(
        body, name="bias_grad",
        out_shape=jax.ShapeDtypeStruct((n_heads, NUM_BUCKETS), F32),
        in_specs=[pl.BlockSpec(memory_space=pltpu.VMEM), pl.BlockSpec(memory_space=pltpu.VMEM)],
        out_specs=pl.BlockSpec(memory_space=pltpu.VMEM),
        compiler_params=pltpu.CompilerParams(vmem_limit_bytes=40 * MIB),
    )(ds_all.reshape(nd, n_heads, pairs), buckets.reshape(nd, 1, pairs))
    return out.T


def _qkv_prep(proj, g_q, g_k, w, dep=None):
    s = proj.shape[0]
    n_heads = w // HEAD
    tm = HEAD

    def body(q_ref, k_ref, gq_ref, gk_ref, qn_ref, kn_ref):
        gq = gq_ref[...] * (HEAD ** -0.5)
        gk = gk_ref[...]
        for h in range(n_heads):
            sl = slice(h * HEAD, (h + 1) * HEAD)
            q = q_ref[:, sl]
            k = k_ref[:, sl]
            qn_ref[:, sl] = q * _rms(q) * gq
            kn_ref[:, sl] = k * _rms(k) * gk

    seg = lambda j: pl.BlockSpec((tm, w), lambda i, j=j: (i, j))
    vec = pl.BlockSpec((1, HEAD), lambda i: (0, 0))
    out = pl.BlockSpec((tm, w), lambda i: (i, 0))
    in_specs = [seg(3), seg(4), vec, vec]
    operands = [proj, proj, g_q, g_k]
    if dep is not None:
        body = _drop_arg(body, len(operands))
        in_specs.append(DEP_SPEC)
        operands.append(dep)
    return pl.pallas_call(
        body, name="qkv_prep", grid=(s // tm,),
        out_shape=[jax.ShapeDtypeStruct((s, w), F32)] * 2,
        in_specs=in_specs, out_specs=[out, out],
        compiler_params=_params(("arbitrary",), 40),
    )(*operands)


class _BlockView:
    def __init__(self, s, dil):
        assert s % (HEAD * dil) == 0 and dil in BLOCK_ORDER
        self.nb = s // (HEAD * dil)
        if dil == 1:
            self.lead, self.block = (s,), (HEAD,)
            self.index = lambda r, n: (n,)
        elif dil == 4:
            self.lead, self.block = (s // 512, 4, 4, 4, SUB), (None, 4, 4, None, SUB)
            self.index = lambda r, n: (n, 0, 0, r, 0)
        else:
            self.lead, self.block = (s // 2048, 16, 16, SUB), (None, 16, None, SUB)
            self.index = lambda r, n: (n, 0, r, 0)

    def view(self, t):
        return t.reshape(self.lead + (t.shape[-1],))

    def spec(self, width, block_of, column=0):
        return pl.BlockSpec(self.block + (width,), lambda r, n: self.index(r, block_of(r, n)) + (column,))


def _rows(ref, lanes=slice(None)):
    v = ref[(slice(None),) * (len(ref.shape) - 1) + (lanes,)]
    return v.reshape(HEAD, v.shape[-1])


def _set_rows(ref, lanes, value):
    ref[(slice(None),) * (len(ref.shape) - 1) + (lanes,)] = value.reshape(ref.shape[:-1] + (value.shape[-1],))


V_SEGMENT = 5


def _attn_fwd(qn, kn, proj, bias, dil, name, dep=None):
    s, w = qn.shape
    n_heads = w // HEAD
    bv = _BlockView(s, dil)

    def body(q_ref, kc_ref, vc_ref, bias_ref, o_ref, lse_ref, s_scr, e_scr, lse_scr, inv_scr, k_prev, v_prev):
        n = pl.program_id(1)
        heads = [slice(h * HEAD, (h + 1) * HEAD) for h in range(n_heads)]
        lse_scr[...] = jnp.zeros_like(lse_scr)

        @pl.when(n == 0)
        def _():
            k_prev[...] = jnp.zeros_like(k_prev)
            v_prev[...] = jnp.zeros_like(v_prev)

        for h, sl in enumerate(heads):
            q = _rows(q_ref, sl).astype(BF16)
            s_p = _dot(q, k_prev[:, sl], NT_DIMS) + bias_ref[h, :, :HEAD]
            s_scr[h, :, :HEAD] = jnp.where(n > 0, s_p, NEG_INF)
            s_scr[h, :, HEAD:] = _dot(q, _rows(kc_ref, sl).astype(BF16), NT_DIMS) + bias_ref[h, :, HEAD:]
        for h in range(n_heads):
            sc = s_scr[h]
            m = jnp.max(sc, axis=-1, keepdims=True)
            e = jnp.exp(sc - m)
            den = jnp.sum(e, axis=-1, keepdims=True)
            e_scr[h] = e.astype(BF16)
            lse_scr[:, h:h + 1] = m + jnp.log(den)
            inv_scr[:, h:h + 1] = 1.0 / den
        for h, sl in enumerate(heads):
            v_cur = _rows(vc_ref, sl).astype(BF16)
            o = _dot(e_scr[h, :, :HEAD], v_prev[:, sl]) + _dot(e_scr[h, :, HEAD:], v_cur)
            _set_rows(o_ref, sl, o * inv_scr[:, h:h + 1])
            v_prev[:, sl] = v_cur
            k_prev[:, sl] = _rows(kc_ref, sl).astype(BF16)
        _set_rows(lse_ref, slice(None), lse_scr[...])

    cur = bv.spec(w, lambda r, n: n)
    in_specs = [cur, cur, bv.spec(w, lambda r, n: n, V_SEGMENT),
                pl.BlockSpec((n_heads, HEAD, 2 * HEAD), lambda r, n: (0, 0, 0))]
    operands = [bv.view(qn), bv.view(kn), bv.view(proj), bias]
    if dep is not None:
        body = _drop_arg(body, len(operands))
        in_specs.append(DEP_SPEC)
        operands.append(dep)
    o, lse = pl.pallas_call(
        body, name=name, grid=(dil, bv.nb),
        out_shape=[jax.ShapeDtypeStruct(bv.lead + (w,), F32), jax.ShapeDtypeStruct(bv.lead + (HEAD,), F32)],
        in_specs=in_specs,
        out_specs=[cur, bv.spec(HEAD, lambda r, n: n)],
        scratch_shapes=[pltpu.VMEM((n_heads, HEAD, 2 * HEAD), F32), pltpu.VMEM((n_heads, HEAD, 2 * HEAD), BF16),
                        pltpu.VMEM((HEAD, HEAD), F32), pltpu.VMEM((HEAD, HEAD), F32),
                        pltpu.VMEM((HEAD, w), BF16), pltpu.VMEM((HEAD, w), BF16)],
        compiler_params=_params(("arbitrary", "arbitrary"), 48),
    )(*operands)
    return o.reshape(s, w), lse.reshape(s, HEAD)


def _attn_bwd(qn, kn, proj, dyb, lse, delta, bias, dil, name, running=None, dep=None):
    s, w = qn.shape
    n_heads = w // HEAD
    bv = _BlockView(s, dil)
    nb = bv.nb

    n_run = 0 if running is None else 3

    def body(q_ref, kc_ref, kp_ref, vc_ref, vp_ref, dy_ref, lse_ref, dl_ref, bias_ref, *rest):
        so_far = rest[:n_run]
        dq_ref, dk_ref, dv_ref, ds_ref, carry_k, carry_v, s_scr, dp_scr, p_scr, dsb_scr, k_cur, v_cur = rest[n_run:]
        base = (lambda i, sl: _rows(so_far[i], sl)) if n_run else (lambda i, sl: 0.0)
        r = pl.program_id(0)
        step = pl.program_id(1)
        blk = nb - 1 - step

        @pl.when((r == 0) & (step == 0))
        def _():
            ds_ref[...] = jnp.zeros_like(ds_ref)

        @pl.when(step == 0)
        def _():
            carry_k[...] = jnp.zeros_like(carry_k)
            carry_v[...] = jnp.zeros_like(carry_v)
            k_cur[...] = _rows(kc_ref).astype(BF16)
            v_cur[...] = _rows(vc_ref).astype(BF16)

        heads = [slice(h * HEAD, (h + 1) * HEAD) for h in range(n_heads)]
        tots = _rows(lse_ref)
        dls = _rows(dl_ref)
        for h, sl in enumerate(heads):
            q, dy = _rows(q_ref, sl).astype(BF16), _rows(dy_ref, sl).astype(BF16)
            kp, kc = _rows(kp_ref, sl).astype(BF16), k_cur[:, sl]
            vp, vc = _rows(vp_ref, sl).astype(BF16), v_cur[:, sl]
            s_p = _dot(q, kp, NT_DIMS) + bias_ref[h, :, :HEAD]
            s_scr[h, :, :HEAD] = jnp.where(blk > 0, s_p, NEG_INF)
            s_scr[h, :, HEAD:] = _dot(q, kc, NT_DIMS) + bias_ref[h, :, HEAD:]
            dp_scr[h, :, :HEAD] = _dot(dy, vp, NT_DIMS)
            dp_scr[h, :, HEAD:] = _dot(dy, vc, NT_DIMS)
        for h in range(n_heads):
            prob = jnp.exp(s_scr[h] - tots[:, h:h + 1])
            ds = prob * (dp_scr[h] - dls[:, h:h + 1])
            ds_ref[h] += ds
            p_scr[h] = prob.astype(BF16)
            dsb_scr[h] = ds.astype(BF16)
        for h, sl in enumerate(heads):
            q, dy = _rows(q_ref, sl).astype(BF16), _rows(dy_ref, sl).astype(BF16)
            kp, kc = _rows(kp_ref, sl).astype(BF16), k_cur[:, sl]
            ds_pb, ds_cb = dsb_scr[h, :, :HEAD], dsb_scr[h, :, HEAD:]
            _set_rows(dq_ref, sl, _dot(ds_pb, kp) + _dot(ds_cb, kc) + base(0, sl))
            _set_rows(dk_ref, sl, _dot(ds_cb, q, TN_DIMS) + carry_k[:, sl] + base(1, sl))
            carry_k[:, sl] = _dot(ds_pb, q, TN_DIMS)
            _set_rows(dv_ref, sl, _dot(p_scr[h, :, HEAD:], dy, TN_DIMS) + carry_v[:, sl] + base(2, sl))
            carry_v[:, sl] = _dot(p_scr[h, :, :HEAD], dy, TN_DIMS)
            k_cur[:, sl] = kp
            v_cur[:, sl] = _rows(vp_ref, sl).astype(BF16)

    cur = bv.spec(w, lambda r, n: nb - 1 - n)
    last = bv.spec(w, lambda r, n: nb - 1)
    prev = bv.spec(w, lambda r, n: jnp.maximum(nb - 2 - n, 0))
    stat = bv.spec(HEAD, lambda r, n: nb - 1 - n)
    whole = pl.BlockSpec((n_heads, HEAD, 2 * HEAD), lambda r, n: (0, 0, 0))
    big = jax.ShapeDtypeStruct(bv.lead + (w,), F32)
    v_last = bv.spec(w, lambda r, n: nb - 1, V_SEGMENT)
    v_prev = bv.spec(w, lambda r, n: jnp.maximum(nb - 2 - n, 0), V_SEGMENT)
    in_specs = [cur, last, prev, v_last, v_prev, cur, stat, stat, whole]
    operands = [bv.view(qn), bv.view(kn), bv.view(kn), bv.view(proj), bv.view(proj), bv.view(dyb), bv.view(lse),
                bv.view(delta), bias]
    aliases = {}
    if running is not None:
        aliases = {len(operands) + i: i for i in range(3)}
        in_specs += [cur] * 3
        operands += [bv.view(t) for t in running]
    if dep is not None:
        body = _drop_arg(body, len(operands))
        in_specs.append(DEP_SPEC)
        operands.append(dep)
    dq, dk, dv, ds = pl.pallas_call(
        body, name=name, grid=(dil, nb),
        out_shape=[big, big, big, jax.ShapeDtypeStruct((n_heads, HEAD, 2 * HEAD), F32)],
        in_specs=in_specs, out_specs=[cur, cur, cur, whole], input_output_aliases=aliases,
        scratch_shapes=[pltpu.VMEM((HEAD, w), F32), pltpu.VMEM((HEAD, w), F32),
                        pltpu.VMEM((n_heads, HEAD, 2 * HEAD), F32), pltpu.VMEM((n_heads, HEAD, 2 * HEAD), F32),
                        pltpu.VMEM((n_heads, HEAD, 2 * HEAD), BF16), pltpu.VMEM((n_heads, HEAD, 2 * HEAD), BF16),
                        pltpu.VMEM((HEAD, w), BF16), pltpu.VMEM((HEAD, w), BF16)],
        compiler_params=_params(("arbitrary", "arbitrary"), 56),
    )(*operands)
    return dq.reshape(s, w), dk.reshape(s, w), dv.reshape(s, w), ds


def _qkv_bwd(dproj, proj, dq, dk, dv, g_q, g_k, w):
    s = proj.shape[0]
    n_heads = w // HEAD
    tm = HEAD

    def body(dproj_hbm, q_ref, k_ref, gq_ref, gk_ref, dq_ref, dk_ref, dv_ref, out_ref, dgq_ref, dgk_ref):
        i = pl.program_id(0)
        gq = gq_ref[...] * (HEAD ** -0.5)
        gk = gk_ref[...]
        acc_q = jnp.zeros((1, HEAD), F32)
        acc_k = jnp.zeros((1, HEAD), F32)
        for h in range(n_heads):
            sl = slice(h * HEAD, (h + 1) * HEAD)
            q, k = q_ref[:, sl], k_ref[:, sl]
            dqn, dkn = dq_ref[:, sl], dk_ref[:, sl]
            rq, rk = _rms(q), _rms(k)
            out_ref[:, h * HEAD:(h + 1) * HEAD] = _rms_bwd(dqn, q, rq, gq).astype(BF16)
            out_ref[:, w + h * HEAD:w + (h + 1) * HEAD] = _rms_bwd(dkn, k, rk, gk).astype(BF16)
            acc_q += jnp.sum(dqn * q * rq, axis=0, keepdims=True)
            acc_k += jnp.sum(dkn * k * rk, axis=0, keepdims=True)
        out_ref[:, 2 * w:] = dv_ref[...].astype(BF16)

        @pl.when(i == 0)
        def _():
            dgq_ref[...] = jnp.zeros_like(dgq_ref)
            dgk_ref[...] = jnp.zeros_like(dgk_ref)

        dgq_ref[...] += acc_q * (HEAD ** -0.5)
        dgk_ref[...] += acc_k

    seg = lambda j: pl.BlockSpec((tm, w), lambda i, j=j: (i, j))
    vec = pl.BlockSpec((1, HEAD), lambda i: (0, 0))
    row = pl.BlockSpec((tm, w), lambda i: (i, 0))
    return pl.pallas_call(
        body, name="qkv_bwd", grid=(s // tm,),
        out_shape=[jax.ShapeDtypeStruct(dproj.shape, BF16),
                   jax.ShapeDtypeStruct((1, HEAD), F32), jax.ShapeDtypeStruct((1, HEAD), F32)],
        in_specs=[pl.BlockSpec(memory_space=pl.ANY), seg(3), seg(4), vec, vec] + [row] * 3,
        out_specs=[pl.BlockSpec((tm, 3 * w), lambda i: (i, 1)), vec, vec],
        input_output_aliases={0: 0},
        compiler_params=_params(("arbitrary",), 48),
    )(dproj, proj, proj, g_q, g_k, dq, dk, dv)


def _mixer_a(u, gv, ws_ref, bst_ref, lng, lnb, z_scr, ln_scr):
    n_groups = u.shape[1] // HEAD
    mu = jnp.mean(gv, axis=-1, keepdims=True)
    xc = gv - mu
    rs = lax.rsqrt(jnp.mean(xc * xc, axis=-1, keepdims=True) + EPS)
    xhat = xc * rs
    ln_scr[...] = (xhat * lng + lnb).astype(BF16)
    causal = _causal_mask()
    for g in range(n_groups):
        sl = slice(g * HEAD, (g + 1) * HEAD)
        wm = jnp.where(causal, ws_ref[g], 0.0).astype(BF16)
        z_scr[:, sl] = _dot(wm, ln_scr[:, sl]) + bst_ref[:, g:g + 1]
    return u, xhat, rs


def _causal_mask():
    token = lambda r: 16 * (r % SUB) + r // SUB
    row = lax.broadcasted_iota(jnp.int32, (HEAD, HEAD), 0)
    col = lax.broadcasted_iota(jnp.int32, (HEAD, HEAD), 1)
    return token(col) <= token(row)


def _merge_b(o_refs, lse_refs, yb_scr):
    n_heads = yb_scr.shape[1] // HEAD
    lses = [t[...] for t in lse_refs]
    m = jnp.maximum(jnp.maximum(lses[0], lses[1]), lses[2])
    tot = m + jnp.log(sum(jnp.exp(t - m) for t in lses))
    alphas = [jnp.exp(t - tot) for t in lses]
    for h in range(n_heads):
        sl = slice(h * HEAD, (h + 1) * HEAD)
        yb_scr[:, sl] = sum(a[:, h:h + 1] * o[:, sl].astype(F32) for a, o in zip(alphas, o_refs))
    return tot


def _mix_fwd(proj, outs, lses, w_s, bst, ln_g, ln_b, g_a, g_b, w, dep=None):
    s = proj.shape[0]
    n_groups = w // HEAD

    def body(au_ref, av_ref, az_ref, bz_ref, o1, o2, o3, l1, l2, l3, ws_ref, bst_ref,
             lng_ref, lnb_ref, ga_ref, gb_ref, p_ref, y_ref, z_scr, ln_scr, yb_scr):
        u, _, _ = _mixer_a(_gelu(au_ref[...]), _gelu(av_ref[...]), ws_ref, bst_ref, lng_ref[...], lnb_ref[...],
                           z_scr, ln_scr)
        ya = u * z_scr[...]
        silu_a, _ = _silu_and_grad(az_ref[...])
        perm = p_ref[...]
        y_ref[:, :w] = _dot(perm, (ya * _rms(ya) * ga_ref[...] * silu_a).astype(BF16), TN_DIMS).astype(BF16)
        _merge_b((o1, o2, o3), (l1, l2, l3), yb_scr)
        yb = yb_scr[...]
        silu_b, _ = _silu_and_grad(bz_ref[...])
        y_ref[:, w:] = _dot(perm, (yb * _rms(yb) * gb_ref[...] * silu_b).astype(BF16), TN_DIMS).astype(BF16)

    seg = lambda j: pl.BlockSpec((HEAD, w), lambda i, j=j: (i, j))
    row = pl.BlockSpec((HEAD, w), lambda i: (i, 0))
    stat = pl.BlockSpec((HEAD, HEAD), lambda i: (i, 0))
    vec = pl.BlockSpec((1, w), lambda i: (0, 0))
    in_specs = [seg(0), seg(1), seg(2), seg(6), row, row, row, stat, stat, stat,
                pl.BlockSpec((n_groups, HEAD, HEAD), lambda i: (0, 0, 0)),
                pl.BlockSpec((HEAD, n_groups), lambda i: (0, 0)), vec, vec, vec, vec,
                pl.BlockSpec((HEAD, HEAD), lambda i: (0, 0))]
    operands = [proj, proj, proj, proj, *outs, *lses, w_s, bst, ln_g, ln_b, g_a, g_b, _chunk_perm()]
    if dep is not None:
        body = _drop_arg(body, len(operands))
        in_specs.append(DEP_SPEC)
        operands.append(dep)
    return pl.pallas_call(
        body, name="mix_fwd", grid=(s // HEAD,),
        out_shape=jax.ShapeDtypeStruct((s, 2 * w), BF16), in_specs=in_specs,
        out_specs=pl.BlockSpec((HEAD, 2 * w), lambda i: (i, 0)),
        scratch_shapes=[pltpu.VMEM((HEAD, w), F32), pltpu.VMEM((HEAD, w), BF16), pltpu.VMEM((HEAD, w), F32)],
        compiler_params=_params(("arbitrary",), 48),
    )(*operands)


def _mix_bwd(proj, dy, outs, lses, w_s, bst, ln_g, ln_b, g_a, g_b, w):
    s = proj.shape[0]
    n_groups = w // HEAD

    def body(au_ref, av_ref, az_ref, bz_ref, dy_ref, o1, o2, o3, l1, l2, l3, ws_ref, bst_ref,
             lng_ref, lnb_ref, ga_ref, gb_ref,
             dproj_ref, dyb_ref, tot_ref, dl_ref, dws_ref, dbst_ref, dlng_ref, dlnb_ref, dga_ref, dgb_ref,
             z_scr, ln_scr, yb_scr, dz_scr, dln_scr):
        i = pl.program_id(0)

        @pl.when(i == 0)
        def _():
            for t in (dws_ref, dbst_ref, dlng_ref, dlnb_ref, dga_ref, dgb_ref):
                t[...] = jnp.zeros_like(t)

        az = az_ref[...]
        lng = lng_ref[...]
        u, du_dau = _gelu_and_grad(au_ref[...])
        gv, dgv_dav = _gelu_and_grad(av_ref[...])
        u, xhat, rs = _mixer_a(u, gv, ws_ref, bst_ref, lng, lnb_ref[...], z_scr, ln_scr)
        z = z_scr[...]
        ya = u * z
        ra = _rms(ya)
        silu_a, dsilu_a = _silu_and_grad(az)
        dya_all = dy_ref[:, :w].astype(F32)
        na = ya * ra * ga_ref[...]
        dna = dya_all * silu_a
        dproj_ref[:, 2 * w:3 * w] = (dya_all * na * dsilu_a).astype(BF16)
        dga_ref[...] += jnp.sum(dna * ya * ra, axis=0, keepdims=True)
        dya = _rms_bwd(dna, ya, ra, ga_ref[...])
        dproj_ref[:, :w] = (dya * z * du_dau).astype(BF16)
        dz_scr[...] = (dya * u).astype(BF16)

        causal = _causal_mask()
        for g in range(n_groups):
            sl = slice(g * HEAD, (g + 1) * HEAD)
            wm = jnp.where(causal, ws_ref[g], 0.0).astype(BF16)
            dz = dz_scr[:, sl]
            dln_scr[:, sl] = _dot(wm, dz, TN_DIMS)
            dws_ref[g] += jnp.where(causal, _dot(dz, ln_scr[:, sl], NT_DIMS), 0.0)
            dbst_ref[:, g:g + 1] += jnp.sum(dz.astype(F32), axis=-1, keepdims=True)
        dln = dln_scr[...]
        dlng_ref[...] += jnp.sum(dln * xhat, axis=0, keepdims=True)
        dlnb_ref[...] += jnp.sum(dln, axis=0, keepdims=True)
        gy = dln * lng
        dgv = rs * (gy - jnp.mean(gy, axis=-1, keepdims=True)
                    - xhat * jnp.mean(gy * xhat, axis=-1, keepdims=True))
        dproj_ref[:, w:2 * w] = (dgv * dgv_dav).astype(BF16)
        dproj_ref[:, 3 * w:6 * w] = jnp.zeros((HEAD, 3 * w), BF16)

        tot_ref[...] = _merge_b((o1, o2, o3), (l1, l2, l3), yb_scr)
        yb = yb_scr[...]
        rb = _rms(yb)
        bz = bz_ref[...]
        silu_b, dsilu_b = _silu_and_grad(bz)
        dyb_all = dy_ref[:, w:].astype(F32)
        dnb = dyb_all * silu_b
        dproj_ref[:, 6 * w:] = (dyb_all * yb * rb * gb_ref[...] * dsilu_b).astype(BF16)
        dgb_ref[...] += jnp.sum(dnb * yb * rb, axis=0, keepdims=True)
        dyb = _rms_bwd(dnb, yb, rb, gb_ref[...])
        dyb_ref[...] = dyb
        prod = dyb * yb
        dl_ref[...] = _lane_pick(
            [jnp.sum(prod[:, h * HEAD:(h + 1) * HEAD], axis=-1, keepdims=True) for h in range(n_groups)], HEAD)

    seg = lambda j: pl.BlockSpec((HEAD, w), lambda i, j=j: (i, j))
    row_w = pl.BlockSpec((HEAD, w), lambda i: (i, 0))
    stat = pl.BlockSpec((HEAD, HEAD), lambda i: (i, 0))
    vec = pl.BlockSpec((1, w), lambda i: (0, 0))
    ws_spec = pl.BlockSpec((n_groups, HEAD, HEAD), lambda i: (0, 0, 0))
    bst_spec = pl.BlockSpec((HEAD, n_groups), lambda i: (0, 0))
    vec_shape = jax.ShapeDtypeStruct((1, w), F32)
    return pl.pallas_call(
        body, name="mix_bwd", grid=(s // HEAD,),
        out_shape=[jax.ShapeDtypeStruct((s, N_SEG * w), BF16), jax.ShapeDtypeStruct((s, w), F32),
                   jax.ShapeDtypeStruct((s, HEAD), F32), jax.ShapeDtypeStruct((s, HEAD), F32),
                   jax.ShapeDtypeStruct((n_groups, HEAD, HEAD), F32), jax.ShapeDtypeStruct((HEAD, n_groups), F32),
                   vec_shape, vec_shape, vec_shape, vec_shape],
        in_specs=[seg(0), seg(1), seg(2), seg(6), pl.BlockSpec((HEAD, 2 * w), lambda i: (i, 0)),
                  row_w, row_w, row_w, stat, stat, stat, ws_spec, bst_spec, vec, vec, vec, vec],
        out_specs=[pl.BlockSpec((HEAD, N_SEG * w), lambda i: (i, 0)), row_w, stat, stat,
                   ws_spec, bst_spec, vec, vec, vec, vec],
        scratch_shapes=[pltpu.VMEM((HEAD, w), F32), pltpu.VMEM((HEAD, w), BF16), pltpu.VMEM((HEAD, w), F32),
                        pltpu.VMEM((HEAD, w), BF16), pltpu.VMEM((HEAD, w), F32)],
        compiler_params=_params(("arbitrary",), 56),
    )(proj, proj, proj, proj, dy, *outs, *lses, w_s, bst, ln_g, ln_b, g_a, g_b)


def _local_step(x, p, tgt, small, wg, ex, while_last_travels=None):
    s, d = x.shape
    n, _, c_in = wg.buffers["w_in"].shape
    assert n == N_DEV
    d_in = n * c_in
    w = d_in // N_SEG
    n_heads = w // HEAD
    p_dim, c_up = wg.buffers["w_ple_up"].shape[1:]
    assert s % (HEAD * DILATIONS[-1]) == 0 and w % HEAD == 0 and d == n * c_up == 2 * w

    near = (2, 4)
    wg.start("gather_in_pair", ["w_in"], (1,))
    token = wg.start("gather_in_first", ["w_in"], ("first",))
    rest = ["w_out", "w_ple_gate", "w_ple_up"]

    hn = _rmsnorm_fwd(x, small["g_pre"], "pre_norm", True, dep=token)
    tm, tk = _tile(s, 1024), _tile(d, 2048)

    def in_proj(shards, name, carry, dep=None):
        return _matmul(
            hn, wg.buffers["w_in"], name=name, grid=(s // tm, len(shards), d // tk), dims=NN_DIMS,
            prefetch=jnp.stack(shards).astype(jnp.int32),
            a_spec=pl.BlockSpec((tm, tk), lambda i, j, k, sh: (i, k)),
            b_spec=pl.BlockSpec((None, tk, c_in), lambda i, j, k, sh: (sh[j], k, 0)),
            acc_shape=(tm, c_in), out_shapes=[jax.ShapeDtypeStruct((s, d_in), F32)],
            out_specs=[pl.BlockSpec((tm, c_in), lambda i, j, k, sh: (i, sh[j]))], carry=carry, dep=dep,
            vmem_mib=56)[0]

    me = wg.me
    core = me & 1
    first, second, far = me ^ (4 - 2 * core), me ^ (2 + 2 * core), me ^ 6
    proj = in_proj([me], "in_proj_own", None)
    token = wg.start("gather_in_second", ["w_in"], ("second",), after=[proj])
    wg.arrived("gather_in_pair", [token])
    proj = in_proj([me ^ 1], "in_proj_sibling", proj)
    buckets = jnp.stack([_rel_buckets(dil) for dil in DILATIONS])
    bias = _bias_build(small["rel_bias"], buckets, n_heads)
    ahead = [bias] + [wg.buffers[k] for k in rest]
    for tag, mine, from_sibling in (("first", first, second ^ 1), ("second", second, first ^ 1), ("far", far, far ^ 1)):
        token = wg.forward("gather_in_" + tag, [proj] + ahead)
        ahead = []
        if tag == "second":
            token = wg.relay_start("gather_in_far", ["w_in"], after=[token])
        elif tag == "far":
            token = wg.start("gather_rest", rest, (1,) + near, after=[token])
        proj = in_proj([mine], "in_proj_" + tag, proj, dep=token)
        wg.forwarded("gather_in_" + tag, [proj])
        proj = in_proj([from_sibling], "in_proj_%s_forwarded" % tag, proj)
    win_g = wg.buffers["w_in"]

    qn, kn = _qkv_prep(proj, small["g_q"], small["g_k"], w)
    token = wg.forward("gather_rest", [qn])
    token = wg.relay_start("gather_rest_far", rest, after=[token])
    outs, lses = [], []
    for c, dil in enumerate(DILATIONS):
        o, l = _attn_fwd(qn, kn, proj, bias[c], dil, "attn_fwd_d%d" % dil, dep=token)
        outs.append(o)
        lses.append(l)
    token = wg.forward("gather_rest_far", outs)

    ws_p = small["w_s"][:, CHUNK_ORDER][:, :, CHUNK_ORDER]
    bst = small["b_s"].T[CHUNK_ORDER]
    mix_args = (outs, lses, ws_p, bst, small["ln_v_g"], small["ln_v_b"], small["g_out_a"], small["g_out_b"], w)
    y = _mix_fwd(proj, *mix_args, dep=token)
    wg.forwarded("gather_rest", [y])
    wg.forwarded("gather_rest_far", [y])
    wout_g, wgate_g, wup_g = (wg.buffers[k] for k in rest)
    wout_f = wout_g.reshape(2 * w, d)
    wgate_f = wgate_g.reshape(d, d)

    tn = _tile(d, 1024)
    tk2 = 2 * w

    def resid_epilogue(acc, ex, outs_):
        outs_[0][...] = ex[0][...] + acc[...]

    h = _matmul(
        y, wout_f, name="out_proj", grid=(s // tm, d // tn, (2 * w) // tk2), dims=NN_DIMS,
        a_spec=pl.BlockSpec((tm, tk2), lambda i, j, k: (i, k)),
        b_spec=pl.BlockSpec((tk2, tn), lambda i, j, k: (k, j)),
        acc_shape=(tm, tn), out_shapes=[jax.ShapeDtypeStruct((s, d), F32)],
        out_specs=[pl.BlockSpec((tm, tn), lambda i, j, k: (i, j))],
        extra=(x,), extra_specs=(pl.BlockSpec((tm, tn), lambda i, j, k: (i, j)),),
        epilogue=resid_epilogue, vmem_mib=56)[0]

    hn2 = _rmsnorm_fwd(h, small["g_ple"], "ple_norm", False)

    tmg = _tile(s, 512)

    def ple_epilogue(acc, ex, outs_):
        h_ref, p_ref, wup_ref, tgt_ref = ex
        dout_ref, dpre_ref, dup_ref, loss_ref = outs_
        gate = jax.nn.sigmoid(acc[...])
        up = _dot(p_ref[...].astype(BF16), wup_ref[...])
        err = h_ref[...] + gate * up - tgt_ref[...]
        dout = err * (1.0 / d)
        dout_ref[...] = dout
        dpre_ref[...] = (dout * up * gate * (1.0 - gate)).astype(BF16)
        dup_ref[...] = (dout * gate).astype(BF16)
        part = 0.5 * jnp.sum(err * err) * (1.0 / d)
        rr = lax.broadcasted_iota(jnp.int32, (8, HEAD), 0)
        cc = lax.broadcasted_iota(jnp.int32, (8, HEAD), 1)
        loss_ref[...] = jnp.where((rr == 0) & (cc == 0), part, 0.0)

    tile_ij = pl.BlockSpec((tmg, c_up), lambda i, j, k: (i, j))
    dout, dpre, dup, loss_parts = _matmul(
        hn2, wgate_f, name="ple_gate", grid=(s // tmg, n, 1), dims=NN_DIMS,
        a_spec=pl.BlockSpec((tmg, d), lambda i, j, k: (i, 0)),
        b_spec=pl.BlockSpec((d, c_up), lambda i, j, k: (0, j)),
        acc_shape=(tmg, c_up),
        out_shapes=[jax.ShapeDtypeStruct((s, d), F32), jax.ShapeDtypeStruct((s, d), BF16),
                    jax.ShapeDtypeStruct((s, d), BF16), jax.ShapeDtypeStruct((s // tmg * 8, n * HEAD), F32)],
        out_specs=[tile_ij, tile_ij, tile_ij, pl.BlockSpec((8, HEAD), lambda i, j, k: (i, j))],
        extra=(h, p, wup_g, tgt),
        extra_specs=(tile_ij, pl.BlockSpec((tmg, p_dim), lambda i, j, k: (i, 0)),
                     pl.BlockSpec((None, p_dim, c_up), lambda i, j, k: (j, 0, 0)), tile_ij),
        epilogue=ple_epilogue)
    loss = jnp.sum(loss_parts)

    tks = _tile(s, 2048)
    g_wup = _matmul(
        p, dup, name="grad_w_up", grid=(1, n, s // tks), dims=TN_DIMS,
        a_spec=pl.BlockSpec((tks, p_dim), lambda i, j, k: (k, 0)),
        b_spec=pl.BlockSpec((tks, c_up), lambda i, j, k: (k, j)),
        acc_shape=(p_dim, c_up), out_shapes=[jax.ShapeDtypeStruct((n, p_dim, c_up), BF16)],
        out_specs=[pl.BlockSpec((None, p_dim, c_up), lambda i, j, k: (j, 0, 0))])[0]

    def tn_matmul(a, b, name):
        m_, n_ = a.shape[1], b.shape[1]
        bm, bn = _tile(m_, 1024), _tile(n_, 1024)
        return _matmul(
            a, b, name=name, grid=(m_ // bm, n_ // bn, 1), dims=TN_DIMS,
            a_spec=pl.BlockSpec((s, bm), lambda i, j, k: (0, i)),
            b_spec=pl.BlockSpec((s, bn), lambda i, j, k: (0, j)),
            acc_shape=(bm, bn), out_shapes=[jax.ShapeDtypeStruct((m_, n_), BF16)],
            out_specs=[pl.BlockSpec((bm, bn), lambda i, j, k: (i, j))], vmem_mib=56)[0]

    def nt_matmul(a, b, name, out_dtype, dep=None):
        k_, n_ = a.shape[1], b.shape[0]
        bm, bn, bk = _tile(s, 1024), _tile(n_, 1024), k_
        return _matmul(
            a, b, name=name, grid=(s // bm, n_ // bn, k_ // bk), dims=NT_DIMS,
            a_spec=pl.BlockSpec((bm, bk), lambda i, j, k: (i, k)),
            b_spec=pl.BlockSpec((bn, bk), lambda i, j, k: (j, k)),
            acc_shape=(bm, bn), out_shapes=[jax.ShapeDtypeStruct((s, n_), out_dtype)],
            out_specs=[pl.BlockSpec((bm, bn), lambda i, j, k: (i, j))], dep=dep, vmem_mib=56)[0]

    by_core = lambda g: g.reshape((N_CHIP, 2) + g.shape[-2:])
    g_wgate = tn_matmul(hn2, dpre, "grad_w_gate").reshape(wgate_g.shape)
    dhn2 = nt_matmul(dpre, wgate_f, "ple_gate_bwd", BF16)
    dh, dh_b, dh_bp, dg_ple = _rmsnorm_bwd(dhn2, h, small["g_ple"], dout, "ple_norm_bwd", False, True)
    g_wout = tn_matmul(y, dh_b, "grad_w_out").reshape(wout_g.shape)

    late = ("w_out", "w_ple_gate", "w_ple_up")
    late_parts = (g_wout, g_wgate, g_wup)
    token = ex.push_pairs("pair_late", [by_core(g) for g in late_parts])
    dy = nt_matmul(dh_bp, wout_f, "out_proj_bwd", BF16, dep=token)
    (dproj, dyb, lse_tot, delta, dws, dbst, dlng, dlnb, dga, dgb) = _mix_bwd(proj, dy, *mix_args)
    both_columns, from_sibling = ex.pairs_done("pair_late", [dproj])
    pair_sums = [_pair_add(mine.reshape((N_DEV,) + mine.shape[-2:]), theirs, "pair_add_" + k, ex.core)
                 for k, mine, theirs in zip(late, both_columns, from_sibling)]
    token = ex.push_chips("chip_late", pair_sums)

    running, dss = None, []
    for c, dil in enumerate(DILATIONS):
        *running, ds = _attn_bwd(qn, kn, proj, dyb, lse_tot, delta, bias[c], dil, "attn_bwd_d%d" % dil,
                                 running=running, dep=token if c == 0 else None)
        dss.append(ds)
    d_rel = _bias_grad(jnp.stack(dss), buckets, n_heads)
    dproj, dgq, dgk = _qkv_bwd(dproj, proj, *running, small["g_q"], small["g_k"], w)
    pair_sums, landed, _ = ex.chips_done("chip_late", [dproj])
    delivered = {k: (mine, theirs) for k, mine, theirs in zip(late, pair_sums, landed)}

    token_row = np.argsort(CHUNK_ORDER)
    dws = dws[:, token_row][:, :, token_row]
    dbst = dbst[token_row]
    small_grads = {
        "w_s": dws, "b_s": dbst.T, "ln_v_g": dlng, "ln_v_b": dlnb, "g_q": dgq, "g_k": dgk,
        "rel_bias": d_rel, "g_out_a": dga, "g_out_b": dgb, "g_ple": dg_ple,
    }

    bm = _tile(d, 1024)

    def grad_w_in(core, name, dep=None):
        return _matmul(
            hn, dproj, name=name, grid=(d // bm, N_CHIP, s // tks), dims=TN_DIMS, prefetch=core.reshape(1),
            a_spec=pl.BlockSpec((tks, bm), lambda i, j, k, core_ref: (k, i)),
            b_spec=pl.BlockSpec((tks, c_in), lambda i, j, k, core_ref: (k, 2 * j + core_ref[0])),
            acc_shape=(bm, c_in), out_shapes=[jax.ShapeDtypeStruct((N_CHIP, d, c_in), BF16)],
            out_specs=[pl.BlockSpec((None, bm, c_in), lambda i, j, k, core_ref: (j, i, 0))], dep=dep)[0]

    for_sibling = grad_w_in(1 - ex.core, "grad_w_in_sibling")
    token = ex.push_pairs("pair_in", [for_sibling])
    mine = grad_w_in(ex.core, "grad_w_in_mine", dep=token)
    _, from_sibling = ex.pairs_done("pair_in", [mine])
    pair_sum = _pair_add(mine, from_sibling[0], "pair_add_w_in")
    token = ex.push_chips("chip_in", [pair_sum], _pack_small(small_grads, SMALL_EARLY))

    dhn = _matmul(
        dproj, win_g, name="in_proj_bwd", grid=(s // tm, d // tn, n // 2), dims=NT_DIMS,
        a_spec=pl.BlockSpec((tm, 2 * c_in), lambda i, j, k: (i, k)),
        b_spec=pl.BlockSpec((2, tn, c_in), lambda i, j, k: (k, j, 0)),
        acc_shape=(tm, tn), out_shapes=[jax.ShapeDtypeStruct((s, d), BF16)],
        out_specs=[pl.BlockSpec((tm, tn), lambda i, j, k: (i, j))], dep=token, vmem_mib=56)[0]
    grad_x, dg_pre = _rmsnorm_bwd(dhn, x, small["g_pre"], dh, "pre_norm_bwd", True, False)
    extra = while_last_travels(token, delivered, dg_pre) if while_last_travels is not None else []
    pair_sums, landed, slabs = ex.chips_done("chip_in", [grad_x] + list(extra))
    delivered["w_in"] = (pair_sums[0], landed[0])
    small_grads["g_pre"] = dg_pre
    return loss, grad_x, small_grads, delivered, slabs, extra


SMALL_EARLY = ("w_s", "b_s", "ln_v_g", "ln_v_b", "g_q", "g_k", "rel_bias", "g_out_a", "g_out_b", "g_ple")
SMALL_LAST = ("g_pre",)
SMALL_NAMES = SMALL_LAST + SMALL_EARLY


def _pack_small(tree, names):
    parts = []
    for name in names:
        flat = tree[name].astype(F32).reshape(-1)
        pad = (-flat.shape[0]) % HEAD
        parts.append(jnp.pad(flat, (0, pad)) if pad else flat)
    slab = jnp.concatenate(parts).reshape(-1, HEAD)
    pad_rows = (-slab.shape[0]) % 8
    return jnp.pad(slab, ((0, pad_rows), (0, 0))) if pad_rows else slab


def _unpack_small(slab, like, names):
    flat = slab.reshape(-1)
    out, off = {}, 0
    for name in names:
        size = like[name].size
        out[name] = flat[off:off + size].reshape(like[name].shape)
        off += size + (-size) % HEAD
    return out


def _peer(k):
    x, y, c = (lax.axis_index(a) for a in AXES)
    if k == "first":
        px, py, pc = x ^ (1 - c), y ^ c, c
    elif k == "second":
        px, py, pc = x ^ c, y ^ (1 - c), c
    else:
        bits = ((k >> 2) & 1, (k >> 1) & 1, k & 1)
        px, py, pc = (1 - v if b else v for v, b in zip((x, y, c), bits))
    return (px, py, pc), 4 * px + 2 * py + pc


def _my_index():
    x, y, c = (lax.axis_index(a) for a in AXES)
    return 4 * x + 2 * y + c


N_CHIP = 4
HBM_SPEC = pl.BlockSpec(memory_space=pl.ANY)


def _remote(src, dst, send_sem, recv_sem, peer):
    return pltpu.make_async_remote_copy(src_ref=src, dst_ref=dst, send_sem=send_sem, recv_sem=recv_sem,
                                        device_id=peer, device_id_type=pl.DeviceIdType.MESH)


SEM_SPEC = pl.BlockSpec(memory_space=pltpu.SEMAPHORE)
HBM_ONLY = pl.BlockSpec(memory_space=pltpu.HBM)
DATAFLOW = pltpu.SideEffectType.DATAFLOW_SIDE_EFFECTING


def _comm_call(name, arrays, *, wait=None, start=None, after=()):
    n, n_after = len(arrays), len(after)

    def body(*refs):
        ins = refs[:n]
        pos = n
        if wait is not None:
            for cp in wait[2](ins, refs[pos], refs[pos + 1]):
                cp.wait()
            pos += 2
        outs = refs[pos + n_after:]
        if start is not None:
            for cp in start[1](ins, outs[0], outs[1]):
                cp.start()
        outs[-1][...] = jnp.zeros_like(outs[-1])

    operands = [pltpu.with_memory_space_constraint(a, pltpu.HBM) for a in arrays]
    in_specs = [HBM_ONLY] * n
    if wait is not None:
        operands += [wait[0], wait[1]]
        in_specs += [SEM_SPEC, SEM_SPEC]
    operands += list(after)
    in_specs += [HBM_SPEC] * n_after
    out_shape, out_specs = [], []
    if start is not None:
        out_shape += [pltpu.SemaphoreType.DMA((start[0],))] * 2
        out_specs += [SEM_SPEC, SEM_SPEC]
    first = len(out_shape)
    out_shape += [pltpu.HBM(a.shape, a.dtype) for a in arrays] + [jax.ShapeDtypeStruct((SUB, HEAD), F32)]
    out_specs += [HBM_ONLY] * n + [pl.BlockSpec(memory_space=pltpu.VMEM)]
    res = pl.pallas_call(
        body, name=name, out_shape=tuple(out_shape), in_specs=tuple(in_specs), out_specs=tuple(out_specs),
        input_output_aliases={i: first + i for i in range(n)},
        compiler_params=pltpu.CompilerParams(has_side_effects=DATAFLOW),
    )(*operands)
    sems = (res[0], res[1]) if start is not None else None
    return list(res[first:first + n]), sems, res[-1]


class _GradExchange:
    def __init__(self):
        x, y, c = (lax.axis_index(a) for a in AXES)
        self.core = c.astype(jnp.int32)
        self.chip = (2 * x + y).astype(jnp.int32)
        self.pending = {}

    def _pair_copies(self, n_arr):
        def make(refs, send_sems, recv_sems):
            sibling, _ = _peer(1)
            other = 1 - lax.axis_index("c")
            srcs, lands = refs[:n_arr], refs[n_arr:]
            pick = lambda ref, ch: ref.at[ch, other] if len(ref.shape) == 4 else ref.at[ch]
            return [_remote(pick(srcs[a], ch), lands[a].at[ch], send_sems.at[a * N_CHIP + ch],
                            recv_sems.at[a * N_CHIP + ch], sibling)
                    for a in range(n_arr) for ch in range(N_CHIP)]
        return make

    def _chip_copies(self, n_arr, with_slab):
        def make(refs, send_sems, recv_sems):
            x, y = lax.axis_index("x"), lax.axis_index("y")
            my_chip = 2 * x + y
            srcs, lands = refs[:n_arr], refs[n_arr:2 * n_arr]
            copies = []
            for j, k in enumerate((2, 4, 6)):
                peer, peer_idx = _peer(k)
                for a in range(n_arr):
                    copies.append(_remote(srcs[a].at[peer_idx // 2], lands[a].at[my_chip],
                                          send_sems.at[3 * a + j], recv_sems.at[3 * a + j], peer))
            if with_slab:
                slab, slab_land = refs[2 * n_arr], refs[2 * n_arr + 1]
                for k in range(1, N_DEV):
                    peer, _ = _peer(k)
                    copies.append(_remote(slab, slab_land.at[_my_index()], send_sems.at[3 * n_arr + k - 1],
                                          recv_sems.at[3 * n_arr + k - 1], peer))
            return copies
        return make

    def push_pairs(self, tag, for_sibling):
        n_arr = len(for_sibling)
        lands = [lax.empty((N_CHIP,) + a.shape[-2:], a.dtype) for a in for_sibling]
        make = self._pair_copies(n_arr)
        arrays, sems, token = _comm_call(tag + "_start", list(for_sibling) + lands, start=(n_arr * N_CHIP, make))
        self.pending[tag] = (arrays, sems, make, n_arr)
        return token

    def pairs_done(self, tag, after):
        arrays, sems, make, n_arr = self.pending.pop(tag)
        arrays, _, _ = _comm_call(tag + "_wait", arrays, wait=(sems[0], sems[1], make), after=after)
        return arrays[:n_arr], arrays[n_arr:]

    def push_chips(self, tag, pair_sums, slab=None):
        n_arr = len(pair_sums)
        arrays = list(pair_sums) + [lax.empty(a.shape, a.dtype) for a in pair_sums]
        n_copies = 3 * n_arr
        if slab is not None:
            arrays += [slab, lax.empty((N_DEV,) + slab.shape, slab.dtype)]
            n_copies += N_DEV - 1
        make = self._chip_copies(n_arr, slab is not None)
        arrays, sems, token = _comm_call(tag + "_start", arrays, start=(n_copies, make))
        self.pending[tag] = (arrays, sems, make, n_arr)
        return token

    def chips_done(self, tag, after):
        arrays, sems, make, n_arr = self.pending.pop(tag)
        arrays, _, _ = _comm_call(tag + "_wait", arrays, wait=(sems[0], sems[1], make), after=after)
        return arrays[:n_arr], arrays[n_arr:2 * n_arr], arrays[2 * n_arr:]


def _cast_place(w, name):
    r, c = w.shape
    tr = r if r * c <= MIB else 1 << ((MIB // c).bit_length() - 1)
    assert r % tr == 0

    def body(me_ref, w_ref, o_ref):
        o_ref[...] = w_ref[...].astype(BF16)

    return pl.pallas_call(
        body, name=name, out_shape=jax.ShapeDtypeStruct((N_DEV, r, c), BF16),
        grid_spec=pltpu.PrefetchScalarGridSpec(
            num_scalar_prefetch=1, grid=(r // tr,),
            in_specs=[pl.BlockSpec((tr, c), lambda i, me_ref: (i, 0))],
            out_specs=pl.BlockSpec((None, tr, c), lambda i, me_ref: (me_ref[0], i, 0))),
        compiler_params=_params(("arbitrary",), 40),
    )(_my_index().astype(jnp.int32).reshape(1), w)


class _WeightGather:
    CHIPS = (2, 4, 6)

    def __init__(self, buffers):
        self.buffers = dict(buffers)
        self.pending = {}
        self.me = _my_index().astype(jnp.int32)

    def _own_slot_to(self, peers):
        def make(refs, send_sems, recv_sems):
            me = _my_index()
            return [_remote(ref.at[me], ref.at[me], send_sems.at[len(peers) * a + j],
                            recv_sems.at[len(peers) * a + j], _peer(k)[0])
                    for a, ref in enumerate(refs) for j, k in enumerate(peers)]
        return make

    def _forward_from(self, chips):
        def make(refs, send_sems, recv_sems):
            sibling, _ = _peer(1)
            copies = []
            for a, ref in enumerate(refs):
                for j, k in enumerate(chips):
                    slot = ref.at[_peer(k)[1]]
                    copies.append(_remote(slot, slot, send_sems.at[len(chips) * a + j],
                                          recv_sems.at[len(chips) * a + j], sibling))
            return copies
        return make

    def _run(self, call, names, **kw):
        arrays, sems, token = _comm_call(call, [self.buffers[k] for k in names], **kw)
        self.buffers.update(zip(names, arrays))
        return sems, token

    def start(self, tag, names, peers, after=()):
        make = self._own_slot_to(peers)
        sems, token = self._run(tag + "_start", names, start=(len(names) * len(peers), make), after=after)
        self.pending[tag] = (names, sems, make, peers)
        return token

    @staticmethod
    def _relay(refs, send_sems, recv_sems):
        x, y, c = (lax.axis_index(a) for a in AXES)
        peer = (x ^ (1 - c), y ^ c, c)
        slot = _my_index() ^ (2 + 2 * c)
        return [_remote(ref.at[slot], ref.at[slot], send_sems.at[a], recv_sems.at[a], peer)
                for a, ref in enumerate(refs)]

    def relay_start(self, tag, names, after=()):
        sems, token = self._run(tag + "_start", names, start=(len(names), self._relay), after=after)
        self.pending[tag] = (names, sems, self._relay, (6,))
        return token

    def arrived(self, tag, after):
        names, sems, make, _ = self.pending.pop(tag)
        self._run(tag + "_wait", names, wait=(sems[0], sems[1], make), after=after)

    def forward(self, tag, after):
        names, sems, make, peers = self.pending.pop(tag)
        chips = tuple(k for k in peers if k != 1)
        onward = self._forward_from(chips)
        new_sems, token = self._run(tag + "_forward", names, wait=(sems[0], sems[1], make),
                                    start=(len(chips) * len(names), onward), after=after)
        self.pending[tag + "/fwd"] = (names, new_sems, onward)
        return token

    def forwarded(self, tag, after):
        names, sems, make = self.pending.pop(tag + "/fwd")
        self._run(tag + "_done", names, wait=(sems[0], sems[1], make), after=after)


def _pair_add(mine, theirs, name, core=None):
    _, r, c_dim = theirs.shape
    tr = r if r * c_dim <= MIB else 1 << ((MIB // c_dim).bit_length() - 1)
    assert r % tr == 0
    stride = 1 if core is None else 2
    offset = jnp.zeros((1,), jnp.int32) if core is None else core.reshape(1)

    def body(off_ref, a_ref, b_ref, o_ref):
        o_ref[...] = (a_ref[...].astype(F32) + b_ref[...].astype(F32)).astype(BF16)

    blk = (None, tr, c_dim)
    return pl.pallas_call(
        body, name=name, out_shape=jax.ShapeDtypeStruct(theirs.shape, BF16),
        grid_spec=pltpu.PrefetchScalarGridSpec(
            num_scalar_prefetch=1, grid=(N_CHIP, r // tr),
            in_specs=[pl.BlockSpec(blk, lambda ch, i, off_ref: (stride * ch + off_ref[0], i, 0)),
                      pl.BlockSpec(blk, lambda ch, i, off_ref: (ch, i, 0))],
            out_specs=pl.BlockSpec(blk, lambda ch, i, off_ref: (ch, i, 0))),
        compiler_params=_params(("arbitrary", "arbitrary"), 40),
    )(offset, mine, theirs)


def _slab_exchange(slab, name):
    def body(slab_in, slab_out, send_sems, recv_sems, local_sem):
        me = _my_index()
        local = pltpu.make_async_copy(slab_in, slab_out.at[me], local_sem)
        local.start()
        sends = []
        for k in range(1, N_DEV):
            peer, _ = _peer(k)
            sends.append(_remote(slab_in, slab_out.at[me], send_sems.at[k - 1], recv_sems.at[k - 1], peer))
        for cp in sends:
            cp.start()
        for k in range(1, N_DEV):
            peer, peer_idx = _peer(k)
            slot = slab_out.at[peer_idx]
            _remote(slot, slot, send_sems.at[k - 1], recv_sems.at[k - 1], peer).wait_recv()
        for cp in sends:
            cp.wait_send()
        local.wait()

    return pl.pallas_call(
        body, name=name, out_shape=jax.ShapeDtypeStruct((N_DEV,) + slab.shape, slab.dtype),
        in_specs=[HBM_SPEC], out_specs=HBM_SPEC,
        scratch_shapes=[pltpu.SemaphoreType.DMA((N_DEV - 1,)), pltpu.SemaphoreType.DMA((N_DEV - 1,)),
                        pltpu.SemaphoreType.DMA],
        compiler_params=pltpu.CompilerParams(has_side_effects=True),
    )(slab)


def _adamw_math(w, g, m, v):
    m = ADAM_B1 * m + (1.0 - ADAM_B1) * g
    v = ADAM_B2 * v + (1.0 - ADAM_B2) * (g * g)
    m_hat = m / (1.0 - ADAM_B1 ** ADAM_STEP)
    v_hat = v / (1.0 - ADAM_B2 ** ADAM_STEP)
    delta = -ADAM_LR * (m_hat / (jnp.sqrt(v_hat) + ADAM_EPS) + ADAM_WD * w)
    return delta, m, v


def _adamw(parts, own, place, w, m, v, name, dep=None):
    n_parts = parts.shape[0]
    r, c = w.shape
    budget = 280 * 1024
    tr = r if r * c <= budget else 1 << ((budget // c).bit_length() - 1)
    assert r % tr == 0

    def body(place_ref, p_ref, own_ref, w_ref, m_ref, v_ref, g_ref, d_ref, nm_ref, nv_ref):
        mine = own_ref[...].astype(F32)
        g = None
        for i in range(n_parts):
            term = jnp.where(place_ref[0] == i, mine, p_ref[i].astype(F32))
            g = term if g is None else g + term
        delta, nm, nv = _adamw_math(w_ref[...], g, m_ref[...], v_ref[...])
        g_ref[...] = g
        d_ref[...] = delta
        nm_ref[...] = nm
        nv_ref[...] = nv

    blk = pl.BlockSpec((tr, c), lambda i, place_ref: (i, 0))
    shape = jax.ShapeDtypeStruct((r, c), F32)
    in_specs = [pl.BlockSpec((n_parts, tr, c), lambda i, place_ref: (0, i, 0)),
                pl.BlockSpec((None, tr, c), lambda i, place_ref: (place_ref[1], i, 0)), blk, blk, blk]
    operands = [parts, own, w, m, v]
    if dep is not None:
        body = _drop_arg(body, 1 + len(operands))
        in_specs.append(pl.BlockSpec((SUB, HEAD), lambda i, place_ref: (0, 0)))
        operands.append(dep)
    return pl.pallas_call(
        body, name=name, out_shape=[shape] * 4,
        grid_spec=pltpu.PrefetchScalarGridSpec(
            num_scalar_prefetch=1, grid=(r // tr,), in_specs=in_specs, out_specs=[blk] * 4),
        compiler_params=_params(("arbitrary",), 48),
    )(place, *operands)


def kernel(x, p, g_pre, w_in, w_s, b_s, ln_v_g, ln_v_b, g_q, g_k, rel_bias, g_out_a, g_out_b, w_out, g_ple, w_ple_gate, w_ple_up, loss_target, m_g_pre, m_w_in, m_w_s, m_b_s, m_ln_v_g, m_ln_v_b, m_g_q, m_g_k, m_rel_bias, m_g_out_a, m_g_out_b, m_w_out, m_g_ple, m_w_ple_gate, m_w_ple_up, v_g_pre, v_w_in, v_w_s, v_b_s, v_ln_v_g, v_ln_v_b, v_g_q, v_g_k, v_rel_bias, v_g_out_a, v_g_out_b, v_w_out, v_g_ple, v_w_ple_gate, v_w_ple_up):
    args = dict(locals())
    small = {"g_pre": g_pre, "w_s": w_s[0], "b_s": b_s[0], "ln_v_g": ln_v_g, "ln_v_b": ln_v_b, "g_q": g_q,
             "g_k": g_k, "rel_bias": rel_bias, "g_out_a": g_out_a, "g_out_b": g_out_b, "g_ple": g_ple}
    big_names = ("w_in", "w_out", "w_ple_gate", "w_ple_up")
    big = {k: args[k][0] for k in big_names}

    wg = _WeightGather({k: _cast_place(big[k], "place_" + k) for k in big_names})
    ex = _GradExchange()
    results = {}

    def big_adamw(k, delivered, dep=None):
        mine, theirs = delivered[k]
        place = jnp.stack([ex.chip, ex.chip])
        return _adamw(theirs, mine, place, big[k], args["m_" + k][0], args["v_" + k][0], "adamw_" + k, dep=dep)

    squeeze = lambda t: {k: (t[k][0] if k in ("w_s", "b_s") else t[k]) for k in SMALL_NAMES}
    small_m = squeeze({k: args["m_" + k] for k in SMALL_NAMES})
    small_v = squeeze({k: args["v_" + k] for k in SMALL_NAMES})
    slab_place = jnp.stack([_my_index().astype(jnp.int32), jnp.zeros((), jnp.int32)])

    def small_adamw(names_, parts, own, call_name):
        packed = _adamw(parts, own[None], slab_place, _pack_small(small, names_), _pack_small(small_m, names_),
                        _pack_small(small_v, names_), call_name)
        for idx in range(4):
            tree = _unpack_small(packed[idx], small, names_)
            for k in names_:
                results.setdefault(k, [None] * 4)[idx] = tree[k].reshape(args[k].shape)
        return packed[0]

    def while_last_travels(token, delivered, dg_pre):
        done = []
        for k in big_names[1:]:
            results[k] = big_adamw(k, delivered, dep=token)
            done.append(results[k][0])
        last_slab = _pack_small({"g_pre": dg_pre}, SMALL_LAST)
        done.append(small_adamw(SMALL_LAST, _slab_exchange(last_slab, "last_exchange"), last_slab, "adamw_last"))
        return done

    loss, grad_x, small_parts, delivered, slabs, _ = _local_step(
        x[0], p[0, 0], loss_target[0], small, wg, ex, while_last_travels)
    results["w_in"] = big_adamw("w_in", delivered)
    for k in big_names:
        results[k] = [t[None] for t in results[k]]
    small_adamw(SMALL_EARLY, slabs[1], slabs[0], "adamw_small")

    names = ("g_pre", "w_in", "w_s", "b_s", "ln_v_g", "ln_v_b", "g_q", "g_k", "rel_bias", "g_out_a", "g_out_b",
             "w_out", "g_ple", "w_ple_gate", "w_ple_up")
    total = lax.psum(loss, AXES)
    out = [total, grad_x[None]]
    for idx in range(4):
        out += [results[k][idx] for k in names]
    return tuple(out)
```

```python
import math

import numpy as np
import jax
import jax.numpy as jnp
from jax import lax
from jax.experimental import pallas as pl
from jax.experimental.pallas import tpu as pltpu

F32 = jnp.float32
BF16 = jnp.bfloat16
EPS = 1e-6
NEG_INF = -1e30
HEAD = 128
DILATIONS = (1, 4, 16)
NUM_BUCKETS = 32
MAX_DISTANCE = 2048
N_SEG = 7
ADAM_LR = 0.001
ADAM_B1 = 0.9
ADAM_B2 = 0.999
ADAM_EPS = 1e-08
ADAM_WD = 0.01
ADAM_STEP = 10
AXES = ("x", "y", "c")
N_DEV = 8
MIB = 1 << 20

SUB = 8

CHUNK_ORDER = np.array([16 * (r % SUB) + r // SUB for r in range(HEAD)])
BLOCK_ORDER = {
    1: CHUNK_ORDER,
    4: np.array([32 * (r // 32) + 4 * (r % SUB) + (r // SUB) % 4 for r in range(HEAD)]),
    16: np.arange(HEAD),
}

NT_DIMS = (((1,), (1,)), ((), ()))
TN_DIMS = (((0,), (0,)), ((), ()))
NN_DIMS = (((1,), (0,)), ((), ()))


def _params(semantics, vmem_mib):
    return pltpu.CompilerParams(dimension_semantics=semantics, vmem_limit_bytes=vmem_mib * MIB)


def _gelu(a):
    return 0.5 * a * (1.0 + lax.erf(a * (2.0 ** -0.5)))


def _gelu_and_grad(a):
    cdf = 0.5 * (1.0 + lax.erf(a * (2.0 ** -0.5)))
    return a * cdf, cdf + a * jnp.exp(-0.5 * a * a) * ((2.0 * math.pi) ** -0.5)


def _silu_and_grad(a):
    s = jax.nn.sigmoid(a)
    return a * s, s * (1.0 + a * (1.0 - s))


def _rms(v):
    return lax.rsqrt(jnp.mean(v * v, axis=-1, keepdims=True) + EPS)


def _rms_bwd(dy, v, r, g):
    gy = dy * g
    return r * gy - v * (r * r * r) * jnp.mean(gy * v, axis=-1, keepdims=True)


def _dot(a, b, dims=NN_DIMS):
    return lax.dot_general(a, b, dims, preferred_element_type=F32)


def _lane_pick(cols, width):
    rows = cols[0].shape[0]
    lane = lax.broadcasted_iota(jnp.int32, (rows, width), 1)
    out = jnp.zeros((rows, width), F32)
    for h, col in enumerate(cols):
        out = jnp.where(lane == h, col, out)
    return out


def _chunk_perm():
    return jnp.asarray(np.eye(HEAD, dtype=np.float32)[CHUNK_ORDER], BF16)


def _unpermute_f32(p, v):
    hi = v.astype(BF16)
    rest = v - hi.astype(F32)
    mid = rest.astype(BF16)
    lo = (rest - mid.astype(F32)).astype(BF16)
    return _dot(p, hi, TN_DIMS) + _dot(p, mid, TN_DIMS) + _dot(p, lo, TN_DIMS)


def _rmsnorm_fwd(x, g, name, permute, dep=None):
    s, d = x.shape
    tm = HEAD

    def body(x_ref, g_ref, p_ref, o_ref):
        v = x_ref[...]
        out = (v * _rms(v) * g_ref[...]).astype(BF16)
        if permute:
            out = _dot(p_ref[...], out).astype(BF16)
        o_ref[...] = out

    in_specs = [pl.BlockSpec((tm, d), lambda i: (i, 0)), pl.BlockSpec((1, d), lambda i: (0, 0)),
                pl.BlockSpec((HEAD, HEAD), lambda i: (0, 0))]
    operands = [x, g, _chunk_perm()]
    if dep is not None:
        body = _drop_arg(body, len(operands))
        in_specs.append(DEP_SPEC)
        operands.append(dep)
    return pl.pallas_call(
        body, name=name, grid=(s // tm,),
        out_shape=jax.ShapeDtypeStruct((s, d), BF16), in_specs=in_specs,
        out_specs=pl.BlockSpec((tm, d), lambda i: (i, 0)),
        compiler_params=_params(("arbitrary",), 40),
    )(*operands)


def _rmsnorm_bwd(dy, v, g, res, name, dy_permuted, with_bf16):
    s, d = v.shape
    tm = HEAD
    perm = _chunk_perm()

    def body(dy_ref, v_ref, g_ref, res_ref, p_ref, *outs):
        dx_ref, dg_ref = outs[0], outs[-1]
        i = pl.program_id(0)
        vv, dyv = v_ref[...], dy_ref[...]
        if dy_permuted:
            dyv = _unpermute_f32(p_ref[...], dyv) if dyv.dtype == F32 else _dot(p_ref[...], dyv, TN_DIMS)
        dyv = dyv.astype(F32)
        r = _rms(vv)
        dx = res_ref[...] + _rms_bwd(dyv, vv, r, g_ref[...])
        dx_ref[...] = dx
        if with_bf16:
            dxb = dx.astype(BF16)
            outs[1][...] = dxb
            outs[2][...] = _dot(p_ref[...], dxb).astype(BF16)

        @pl.when(i == 0)
        def _():
            dg_ref[...] = jnp.zeros_like(dg_ref)

        dg_ref[...] += jnp.sum(dyv * vv * r, axis=0, keepdims=True)

    row = pl.BlockSpec((tm, d), lambda i: (i, 0))
    vec = pl.BlockSpec((1, d), lambda i: (0, 0))
    shapes = [jax.ShapeDtypeStruct((s, d), F32)]
    specs = [row]
    if with_bf16:
        shapes += [jax.ShapeDtypeStruct((s, d), BF16)] * 2
        specs += [row, row]
    shapes.append(jax.ShapeDtypeStruct((1, d), F32))
    specs.append(vec)
    return pl.pallas_call(
        body, name=name, grid=(s // tm,), out_shape=shapes,
        in_specs=[row, row, vec, row, pl.BlockSpec((HEAD, HEAD), lambda i: (0, 0))], out_specs=specs,
        compiler_params=_params(("arbitrary",), 40),
    )(dy, v, g, res, perm)


DEP_SPEC = pl.BlockSpec((SUB, HEAD), lambda *_: (0, 0))


def _drop_arg(body, pos):
    return lambda *refs: body(*refs[:pos], *refs[pos + 1:])


def _matmul(a, b, *, name, grid, a_spec, b_spec, dims, acc_shape, out_shapes, out_specs,
            extra=(), extra_specs=(), epilogue=None, vmem_mib=48, dep=None, prefetch=None, carry=None):
    nk = grid[2]
    n_user = len(extra)
    for unread, spec in ((dep, DEP_SPEC), (carry, HBM_SPEC)):
        if unread is not None:
            extra, extra_specs = tuple(extra) + (unread,), tuple(extra_specs) + (spec,)
    n_extra, n_out = len(extra), len(out_shapes)
    n_pre = 0 if prefetch is None else 1
    aliases = {} if carry is None else {n_pre + 2 + n_extra - 1: 0}

    def body(*refs):
        refs = refs[n_pre:]
        a_ref, b_ref = refs[0], refs[1]
        ex = refs[2:2 + n_user]
        outs = refs[2 + n_extra:2 + n_extra + n_out]
        av = a_ref[...]
        if av.dtype != BF16:
            av = av.astype(BF16)
        if nk == 1 and epilogue is None:
            outs[0][...] = _dot(av, b_ref[...], dims).astype(outs[0].dtype)
            return
        acc = refs[-1]
        k = pl.program_id(2)

        @pl.when(k == 0)
        def _():
            acc[...] = jnp.zeros_like(acc)

        if len(b_ref.shape) == 3:
            span = b_ref.shape[2]
            acc[...] += sum(_dot(av[:, g * span:(g + 1) * span], b_ref[g], dims) for g in range(b_ref.shape[0]))
        else:
            acc[...] += _dot(av, b_ref[...], dims)

        @pl.when(k == nk - 1)
        def _():
            if epilogue is None:
                outs[0][...] = acc[...].astype(outs[0].dtype)
            else:
                epilogue(acc, ex, outs)

    direct = nk == 1 and epilogue is None
    scratch = [] if direct else [pltpu.VMEM(acc_shape, F32)]
    params = _params(("parallel", "parallel", "arbitrary"), vmem_mib)
    if prefetch is None:
        return pl.pallas_call(
            body, name=name, grid=grid, out_shape=list(out_shapes),
            in_specs=[a_spec, b_spec, *extra_specs], out_specs=list(out_specs),
            scratch_shapes=scratch, compiler_params=params, input_output_aliases=aliases,
        )(a, b, *extra)
    return pl.pallas_call(
        body, name=name, out_shape=list(out_shapes),
        grid_spec=pltpu.PrefetchScalarGridSpec(
            num_scalar_prefetch=1, grid=grid, in_specs=[a_spec, b_spec, *extra_specs],
            out_specs=list(out_specs), scratch_shapes=scratch),
        compiler_params=params, input_output_aliases=aliases,
    )(prefetch, a, b, *extra)


def _tile(n, want):
    t = min(n, want)
    while n % t:
        t //= 2
    return t


def _rel_buckets(dil):
    order = BLOCK_ORDER[dil]
    qi = jnp.asarray(HEAD + order)
    kj = jnp.asarray(np.concatenate([order, HEAD + order]))
    delta = qi[:, None] - kj[None, :]
    band = (delta >= 0) & (delta <= HEAD)
    dist = jnp.clip(delta, 0, None) * dil
    max_exact = NUM_BUCKETS // 2
    dd = jnp.maximum(dist, 1).astype(F32)
    large = max_exact + (jnp.log(dd / max_exact) / math.log(MAX_DISTANCE / max_exact)
                         * (NUM_BUCKETS - max_exact)).astype(jnp.int32)
    large = jnp.minimum(large, NUM_BUCKETS - 1)
    bucket = jnp.where(dist < max_exact, dist, large)
    return jnp.where(band, bucket, -1).astype(jnp.int32)


def _bias_build(rel_bias, buckets, n_heads):
    nd = buckets.shape[0]

    def body(rb_ref, bk_ref, o_ref):
        for c in range(nd):
            def per_head(h, carry, c=c):
                bk = bk_ref[c]
                acc = jnp.where(bk < 0, NEG_INF, 0.0).astype(F32)
                for b in range(NUM_BUCKETS):
                    acc = jnp.where(bk == b, rb_ref[b, h], acc)
                o_ref[c, h] = acc
                return carry

            lax.fori_loop(0, n_heads, per_head, 0)

    return pl.pallas_call(
        body, name="bias_build",
        out_shape=jax.ShapeDtypeStruct((nd, n_heads, HEAD, 2 * HEAD), F32),
        in_specs=[pl.BlockSpec(memory_space=pltpu.SMEM), pl.BlockSpec(memory_space=pltpu.VMEM)],
        out_specs=pl.BlockSpec(memory_space=pltpu.VMEM),
    )(rel_bias, buckets)


def _bias_grad(ds_all, buckets, n_heads):
    nd = buckets.shape[0]
    pairs = HEAD * 2 * HEAD

    def body(ds_ref, bk_ref, o_ref):
        rows = lax.broadcasted_iota(jnp.int32, (NUM_BUCKETS, pairs), 0)
        tot = jnp.zeros((n_heads, NUM_BUCKETS), F32)
        for c in range(nd):
            onehot = (rows == bk_ref[c]).astype(BF16)
            ds = ds_ref[c]
            hi = ds.astype(BF16)
            lo = (ds - hi.astype(F32)).astype(BF16)
            tot = tot + _dot(hi, onehot, NT_DIMS) + _dot(lo, onehot, NT_DIMS)
        o_ref[...] = tot

    out = pl.pallas_call(
        body, name="bias_grad",
        out_shape=jax.ShapeDtypeStruct((n_heads, NUM_BUCKETS), F32),
        in_specs=[pl.BlockSpec(memory_space=pltpu.VMEM), pl.BlockSpec(memory_space=pltpu.VMEM)],
        out_specs=pl.BlockSpec(memory_space=pltpu.VMEM),
        compiler_params=pltpu.CompilerParams(vmem_limit_bytes=40 * MIB),
    )(ds_all.reshape(nd, n_heads, pairs), buckets.reshape(nd, 1, pairs))
    return out.T


def _qkv_prep(proj, g_q, g_k, w, dep=None):
    s = proj.shape[0]
    n_heads = w // HEAD
    tm = HEAD

    def body(q_ref, k_ref, gq_ref, gk_ref, qn_ref, kn_ref):
        gq = gq_ref[...] * (HEAD ** -0.5)
        gk = gk_ref[...]
        for h in range(n_heads):
            sl = slice(h * HEAD, (h + 1) * HEAD)
            q = q_ref[:, sl]
            k = k_ref[:, sl]
            qn_ref[:, sl] = q * _rms(q) * gq
            kn_ref[:, sl] = k * _rms(k) * gk

    seg = lambda j: pl.BlockSpec((tm, w), lambda i, j=j: (i, j))
    vec = pl.BlockSpec((1, HEAD), lambda i: (0, 0))
    out = pl.BlockSpec((tm, w), lambda i: (i, 0))
    in_specs = [seg(3), seg(4), vec, vec]
    operands = [proj, proj, g_q, g_k]
    if dep is not None:
        body = _drop_arg(body, len(operands))
        in_specs.append(DEP_SPEC)
        operands.append(dep)
    return pl.pallas_call(
        body, name="qkv_prep", grid=(s // tm,),
        out_shape=[jax.ShapeDtypeStruct((s, w), F32)] * 2,
        in_specs=in_specs, out_specs=[out, out],
        compiler_params=_params(("arbitrary",), 40),
    )(*operands)


class _BlockView:
    def __init__(self, s, dil):
        assert s % (HEAD * dil) == 0 and dil in BLOCK_ORDER
        self.nb = s // (HEAD * dil)
        if dil == 1:
            self.lead, self.block = (s,), (HEAD,)
            self.index = lambda r, n: (n,)
        elif dil == 4:
            self.lead, self.block = (s // 512, 4, 4, 4, SUB), (None, 4, 4, None, SUB)
            self.index = lambda r, n: (n, 0, 0, r, 0)
        else:
            self.lead, self.block = (s // 2048, 16, 16, SUB), (None, 16, None, SUB)
            self.index = lambda r, n: (n, 0, r, 0)

    def view(self, t):
        return t.reshape(self.lead + (t.shape[-1],))

    def spec(self, width, block_of, column=0):
        return pl.BlockSpec(self.block + (width,), lambda r, n: self.index(r, block_of(r, n)) + (column,))


def _rows(ref, lanes=slice(None)):
    v = ref[(slice(None),) * (len(ref.shape) - 1) + (lanes,)]
    return v.reshape(HEAD, v.shape[-1])


def _set_rows(ref, lanes, value):
    ref[(slice(None),) * (len(ref.shape) - 1) + (lanes,)] = value.reshape(ref.shape[:-1] + (value.shape[-1],))


V_SEGMENT = 5


def _attn_fwd(qn, kn, proj, bias, dil, name, dep=None):
    s, w = qn.shape
    n_heads = w // HEAD
    bv = _BlockView(s, dil)

    def body(q_ref, kc_ref, vc_ref, bias_ref, o_ref, lse_ref, s_scr, e_scr, lse_scr, inv_scr, k_prev, v_prev):
        n = pl.program_id(1)
        heads = [slice(h * HEAD, (h + 1) * HEAD) for h in range(n_heads)]
        lse_scr[...] = jnp.zeros_like(lse_scr)

        @pl.when(n == 0)
        def _():
            k_prev[...] = jnp.zeros_like(k_prev)
            v_prev[...] = jnp.zeros_like(v_prev)

        for h, sl in enumerate(heads):
            q = _rows(q_ref, sl).astype(BF16)
            s_p = _dot(q, k_prev[:, sl], NT_DIMS) + bias_ref[h, :, :HEAD]
            s_scr[h, :, :HEAD] = jnp.where(n > 0, s_p, NEG_INF)
            s_scr[h, :, HEAD:] = _dot(q, _rows(kc_ref, sl).astype(BF16), NT_DIMS) + bias_ref[h, :, HEAD:]
        for h in range(n_heads):
            sc = s_scr[h]
            m = jnp.max(sc, axis=-1, keepdims=True)
            e = jnp.exp(sc - m)
            den = jnp.sum(e, axis=-1, keepdims=True)
            e_scr[h] = e.astype(BF16)
            lse_scr[:, h:h + 1] = m + jnp.log(den)
            inv_scr[:, h:h + 1] = 1.0 / den
        for h, sl in enumerate(heads):
            v_cur = _rows(vc_ref, sl).astype(BF16)
            o = _dot(e_scr[h, :, :HEAD], v_prev[:, sl]) + _dot(e_scr[h, :, HEAD:], v_cur)
            _set_rows(o_ref, sl, o * inv_scr[:, h:h + 1])
            v_prev[:, sl] = v_cur
            k_prev[:, sl] = _rows(kc_ref, sl).astype(BF16)
        _set_rows(lse_ref, slice(None), lse_scr[...])

    cur = bv.spec(w, lambda r, n: n)
    in_specs = [cur, cur, bv.spec(w, lambda r, n: n, V_SEGMENT),
                pl.BlockSpec((n_heads, HEAD, 2 * HEAD), lambda r, n: (0, 0, 0))]
    operands = [bv.view(qn), bv.view(kn), bv.view(proj), bias]
    if dep is not None:
        body = _drop_arg(body, len(operands))
        in_specs.append(DEP_SPEC)
        operands.append(dep)
    o, lse = pl.pallas_call(
        body, name=name, grid=(dil, bv.nb),
        out_shape=[jax.ShapeDtypeStruct(bv.lead + (w,), F32), jax.ShapeDtypeStruct(bv.lead + (HEAD,), F32)],
        in_specs=in_specs,
        out_specs=[cur, bv.spec(HEAD, lambda r, n: n)],
        scratch_shapes=[pltpu.VMEM((n_heads, HEAD, 2 * HEAD), F32), pltpu.VMEM((n_heads, HEAD, 2 * HEAD), BF16),
                        pltpu.VMEM((HEAD, HEAD), F32), pltpu.VMEM((HEAD, HEAD), F32),
                        pltpu.VMEM((HEAD, w), BF16), pltpu.VMEM((HEAD, w), BF16)],
        compiler_params=_params(("arbitrary", "arbitrary"), 48),
    )(*operands)
    return o.reshape(s, w), lse.reshape(s, HEAD)


def _attn_bwd(qn, kn, proj, dyb, lse, delta, bias, dil, name, running=None, dep=None):
    s, w = qn.shape
    n_heads = w // HEAD
    bv = _BlockView(s, dil)
    nb = bv.nb

    n_run = 0 if running is None else 3

    def body(q_ref, kc_ref, kp_ref, vc_ref, vp_ref, dy_ref, lse_ref, dl_ref, bias_ref, *rest):
        so_far = rest[:n_run]
        dq_ref, dk_ref, dv_ref, ds_ref, carry_k, carry_v, s_scr, dp_scr, p_scr, dsb_scr, k_cur, v_cur = rest[n_run:]
        base = (lambda i, sl: _rows(so_far[i], sl)) if n_run else (lambda i, sl: 0.0)
        r = pl.program_id(0)
        step = pl.program_id(1)
        blk = nb - 1 - step

        @pl.when((r == 0) & (step == 0))
        def _():
            ds_ref[...] = jnp.zeros_like(ds_ref)

        @pl.when(step == 0)
        def _():
            carry_k[...] = jnp.zeros_like(carry_k)
            carry_v[...] = jnp.zeros_like(carry_v)
            k_cur[...] = _rows(kc_ref).astype(BF16)
            v_cur[...] = _rows(vc_ref).astype(BF16)

        heads = [slice(h * HEAD, (h + 1) * HEAD) for h in range(n_heads)]
        tots = _rows(lse_ref)
        dls = _rows(dl_ref)
        for h, sl in enumerate(heads):
            q, dy = _rows(q_ref, sl).astype(BF16), _rows(dy_ref, sl).astype(BF16)
            kp, kc = _rows(kp_ref, sl).astype(BF16), k_cur[:, sl]
            vp, vc = _rows(vp_ref, sl).astype(BF16), v_cur[:, sl]
            s_p = _dot(q, kp, NT_DIMS) + bias_ref[h, :, :HEAD]
            s_scr[h, :, :HEAD] = jnp.where(blk > 0, s_p, NEG_INF)
            s_scr[h, :, HEAD:] = _dot(q, kc, NT_DIMS) + bias_ref[h, :, HEAD:]
            dp_scr[h, :, :HEAD] = _dot(dy, vp, NT_DIMS)
            dp_scr[h, :, HEAD:] = _dot(dy, vc, NT_DIMS)
        for h in range(n_heads):
            prob = jnp.exp(s_scr[h] - tots[:, h:h + 1])
            ds = prob * (dp_scr[h] - dls[:, h:h + 1])
            ds_ref[h] += ds
            p_scr[h] = prob.astype(BF16)
            dsb_scr[h] = ds.astype(BF16)
        for h, sl in enumerate(heads):
            q, dy = _rows(q_ref, sl).astype(BF16), _rows(dy_ref, sl).astype(BF16)
            kp, kc = _rows(kp_ref, sl).astype(BF16), k_cur[:, sl]
            ds_pb, ds_cb = dsb_scr[h, :, :HEAD], dsb_scr[h, :, HEAD:]
            _set_rows(dq_ref, sl, _dot(ds_pb, kp) + _dot(ds_cb, kc) + base(0, sl))
            _set_rows(dk_ref, sl, _dot(ds_cb, q, TN_DIMS) + carry_k[:, sl] + base(1, sl))
            carry_k[:, sl] = _dot(ds_pb, q, TN_DIMS)
            _set_rows(dv_ref, sl, _dot(p_scr[h, :, HEAD:], dy, TN_DIMS) + carry_v[:, sl] + base(2, sl))
            carry_v[:, sl] = _dot(p_scr[h, :, :HEAD], dy, TN_DIMS)
            k_cur[:, sl] = kp
            v_cur[:, sl] = _rows(vp_ref, sl).astype(BF16)

    cur = bv.spec(w, lambda r, n: nb - 1 - n)
    last = bv.spec(w, lambda r, n: nb - 1)
    prev = bv.spec(w, lambda r, n: jnp.maximum(nb - 2 - n, 0))
    stat = bv.spec(HEAD, lambda r, n: nb - 1 - n)
    whole = pl.BlockSpec((n_heads, HEAD, 2 * HEAD), lambda r, n: (0, 0, 0))
    big = jax.ShapeDtypeStruct(bv.lead + (w,), F32)
    v_last = bv.spec(w, lambda r, n: nb - 1, V_SEGMENT)
    v_prev = bv.spec(w, lambda r, n: jnp.maximum(nb - 2 - n, 0), V_SEGMENT)
    in_specs = [cur, last, prev, v_last, v_prev, cur, stat, stat, whole]
    operands = [bv.view(qn), bv.view(kn), bv.view(kn), bv.view(proj), bv.view(proj), bv.view(dyb), bv.view(lse),
                bv.view(delta), bias]
    aliases = {}
    if running is not None:
        aliases = {len(operands) + i: i for i in range(3)}
        in_specs += [cur] * 3
        operands += [bv.view(t) for t in running]
    if dep is not None:
        body = _drop_arg(body, len(operands))
        in_specs.append(DEP_SPEC)
        operands.append(dep)
    dq, dk, dv, ds = pl.pallas_call(
        body, name=name, grid=(dil, nb),
        out_shape=[big, big, big, jax.ShapeDtypeStruct((n_heads, HEAD, 2 * HEAD), F32)],
        in_specs=in_specs, out_specs=[cur, cur, cur, whole], input_output_aliases=aliases,
        scratch_shapes=[pltpu.VMEM((HEAD, w), F32), pltpu.VMEM((HEAD, w), F32),
                        pltpu.VMEM((n_heads, HEAD, 2 * HEAD), F32), pltpu.VMEM((n_heads, HEAD, 2 * HEAD), F32),
                        pltpu.VMEM((n_heads, HEAD, 2 * HEAD), BF16), pltpu.VMEM((n_heads, HEAD, 2 * HEAD), BF16),
                        pltpu.VMEM((HEAD, w), BF16), pltpu.VMEM((HEAD, w), BF16)],
        compiler_params=_params(("arbitrary", "arbitrary"), 56),
    )(*operands)
    return dq.reshape(s, w), dk.reshape(s, w), dv.reshape(s, w), ds


def _qkv_bwd(dproj, proj, dq, dk, dv, g_q, g_k, w):
    s = proj.shape[0]
    n_heads = w // HEAD
    tm = HEAD

    def body(dproj_hbm, q_ref, k_ref, gq_ref, gk_ref, dq_ref, dk_ref, dv_ref, out_ref, dgq_ref, dgk_ref):
        i = pl.program_id(0)
        gq = gq_ref[...] * (HEAD ** -0.5)
        gk = gk_ref[...]
        acc_q = jnp.zeros((1, HEAD), F32)
        acc_k = jnp.zeros((1, HEAD), F32)
        for h in range(n_heads):
            sl = slice(h * HEAD, (h + 1) * HEAD)
            q, k = q_ref[:, sl], k_ref[:, sl]
            dqn, dkn = dq_ref[:, sl], dk_ref[:, sl]
            rq, rk = _rms(q), _rms(k)
            out_ref[:, h * HEAD:(h + 1) * HEAD] = _rms_bwd(dqn, q, rq, gq).astype(BF16)
            out_ref[:, w + h * HEAD:w + (h + 1) * HEAD] = _rms_bwd(dkn, k, rk, gk).astype(BF16)
            acc_q += jnp.sum(dqn * q * rq, axis=0, keepdims=True)
            acc_k += jnp.sum(dkn * k * rk, axis=0, keepdims=True)
        out_ref[:, 2 * w:] = dv_ref[...].astype(BF16)

        @pl.when(i == 0)
        def _():
            dgq_ref[...] = jnp.zeros_like(dgq_ref)
            dgk_ref[...] = jnp.zeros_like(dgk_ref)

        dgq_ref[...] += acc_q * (HEAD ** -0.5)
        dgk_ref[...] += acc_k

    seg = lambda j: pl.BlockSpec((tm, w), lambda i, j=j: (i, j))
    vec = pl.BlockSpec((1, HEAD), lambda i: (0, 0))
    row = pl.BlockSpec((tm, w), lambda i: (i, 0))
    return pl.pallas_call(
        body, name="qkv_bwd", grid=(s // tm,),
        out_shape=[jax.ShapeDtypeStruct(dproj.shape, BF16),
                   jax.ShapeDtypeStruct((1, HEAD), F32), jax.ShapeDtypeStruct((1, HEAD), F32)],
        in_specs=[pl.BlockSpec(memory_space=pl.ANY), seg(3), seg(4), vec, vec] + [row] * 3,
        out_specs=[pl.BlockSpec((tm, 3 * w), lambda i: (i, 1)), vec, vec],
        input_output_aliases={0: 0},
        compiler_params=_params(("arbitrary",), 48),
    )(dproj, proj, proj, g_q, g_k, dq, dk, dv)


def _mixer_a(u, gv, ws_ref, bst_ref, lng, lnb, z_scr, ln_scr):
    n_groups = u.shape[1] // HEAD
    mu = jnp.mean(gv, axis=-1, keepdims=True)
    xc = gv - mu
    rs = lax.rsqrt(jnp.mean(xc * xc, axis=-1, keepdims=True) + EPS)
    xhat = xc * rs
    ln_scr[...] = (xhat * lng + lnb).astype(BF16)
    causal = _causal_mask()
    for g in range(n_groups):
        sl = slice(g * HEAD, (g + 1) * HEAD)
        wm = jnp.where(causal, ws_ref[g], 0.0).astype(BF16)
        z_scr[:, sl] = _dot(wm, ln_scr[:, sl]) + bst_ref[:, g:g + 1]
    return u, xhat, rs


def _causal_mask():
    token = lambda r: 16 * (r % SUB) + r // SUB
    row = lax.broadcasted_iota(jnp.int32, (HEAD, HEAD), 0)
    col = lax.broadcasted_iota(jnp.int32, (HEAD, HEAD), 1)
    return token(col) <= token(row)


def _merge_b(o_refs, lse_refs, yb_scr):
    n_heads = yb_scr.shape[1] // HEAD
    lses = [t[...] for t in lse_refs]
    m = jnp.maximum(jnp.maximum(lses[0], lses[1]), lses[2])
    tot = m + jnp.log(sum(jnp.exp(t - m) for t in lses))
    alphas = [jnp.exp(t - tot) for t in lses]
    for h in range(n_heads):
        sl = slice(h * HEAD, (h + 1) * HEAD)
        yb_scr[:, sl] = sum(a[:, h:h + 1] * o[:, sl].astype(F32) for a, o in zip(alphas, o_refs))
    return tot


def _mix_fwd(proj, outs, lses, w_s, bst, ln_g, ln_b, g_a, g_b, w, dep=None):
    s = proj.shape[0]
    n_groups = w // HEAD

    def body(au_ref, av_ref, az_ref, bz_ref, o1, o2, o3, l1, l2, l3, ws_ref, bst_ref,
             lng_ref, lnb_ref, ga_ref, gb_ref, p_ref, y_ref, z_scr, ln_scr, yb_scr):
        u, _, _ = _mixer_a(_gelu(au_ref[...]), _gelu(av_ref[...]), ws_ref, bst_ref, lng_ref[...], lnb_ref[...],
                           z_scr, ln_scr)
        ya = u * z_scr[...]
        silu_a, _ = _silu_and_grad(az_ref[...])
        perm = p_ref[...]
        y_ref[:, :w] = _dot(perm, (ya * _rms(ya) * ga_ref[...] * silu_a).astype(BF16), TN_DIMS).astype(BF16)
        _merge_b((o1, o2, o3), (l1, l2, l3), yb_scr)
        yb = yb_scr[...]
        silu_b, _ = _silu_and_grad(bz_ref[...])
        y_ref[:, w:] = _dot(perm, (yb * _rms(yb) * gb_ref[...] * silu_b).astype(BF16), TN_DIMS).astype(BF16)

    seg = lambda j: pl.BlockSpec((HEAD, w), lambda i, j=j: (i, j))
    row = pl.BlockSpec((HEAD, w), lambda i: (i, 0))
    stat = pl.BlockSpec((HEAD, HEAD), lambda i: (i, 0))
    vec = pl.BlockSpec((1, w), lambda i: (0, 0))
    in_specs = [seg(0), seg(1), seg(2), seg(6), row, row, row, stat, stat, stat,
                pl.BlockSpec((n_groups, HEAD, HEAD), lambda i: (0, 0, 0)),
                pl.BlockSpec((HEAD, n_groups), lambda i: (0, 0)), vec, vec, vec, vec,
                pl.BlockSpec((HEAD, HEAD), lambda i: (0, 0))]
    operands = [proj, proj, proj, proj, *outs, *lses, w_s, bst, ln_g, ln_b, g_a, g_b, _chunk_perm()]
    if dep is not None:
        body = _drop_arg(body, len(operands))
        in_specs.append(DEP_SPEC)
        operands.append(dep)
    return pl.pallas_call(
        body, name="mix_fwd", grid=(s // HEAD,),
        out_shape=jax.ShapeDtypeStruct((s, 2 * w), BF16), in_specs=in_specs,
        out_specs=pl.BlockSpec((HEAD, 2 * w), lambda i: (i, 0)),
        scratch_shapes=[pltpu.VMEM((HEAD, w), F32), pltpu.VMEM((HEAD, w), BF16), pltpu.VMEM((HEAD, w), F32)],
        compiler_params=_params(("arbitrary",), 48),
    )(*operands)


def _mix_bwd(proj, dy, outs, lses, w_s, bst, ln_g, ln_b, g_a, g_b, w):
    s = proj.shape[0]
    n_groups = w // HEAD

    def body(au_ref, av_ref, az_ref, bz_ref, dy_ref, o1, o2, o3, l1, l2, l3, ws_ref, bst_ref,
             lng_ref, lnb_ref, ga_ref, gb_ref,
             dproj_ref, dyb_ref, tot_ref, dl_ref, dws_ref, dbst_ref, dlng_ref, dlnb_ref, dga_ref, dgb_ref,
             z_scr, ln_scr, yb_scr, dz_scr, dln_scr):
        i = pl.program_id(0)

        @pl.when(i == 0)
        def _():
            for t in (dws_ref, dbst_ref, dlng_ref, dlnb_ref, dga_ref, dgb_ref):
                t[...] = jnp.zeros_like(t)

        az = az_ref[...]
        lng = lng_ref[...]
        u, du_dau = _gelu_and_grad(au_ref[...])
        gv, dgv_dav = _gelu_and_grad(av_ref[...])
        u, xhat, rs = _mixer_a(u, gv, ws_ref, bst_ref, lng, lnb_ref[...], z_scr, ln_scr)
        z = z_scr[...]
        ya = u * z
        ra = _rms(ya)
        silu_a, dsilu_a = _silu_and_grad(az)
        dya_all = dy_ref[:, :w].astype(F32)
        na = ya * ra * ga_ref[...]
        dna = dya_all * silu_a
        dproj_ref[:, 2 * w:3 * w] = (dya_all * na * dsilu_a).astype(BF16)
        dga_ref[...] += jnp.sum(dna * ya * ra, axis=0, keepdims=True)
        dya = _rms_bwd(dna, ya, ra, ga_ref[...])
        dproj_ref[:, :w] = (dya * z * du_dau).astype(BF16)
        dz_scr[...] = (dya * u).astype(BF16)

        causal = _causal_mask()
        for g in range(n_groups):
            sl = slice(g * HEAD, (g + 1) * HEAD)
            wm = jnp.where(causal, ws_ref[g], 0.0).astype(BF16)
            dz = dz_scr[:, sl]
            dln_scr[:, sl] = _dot(wm, dz, TN_DIMS)
            dws_ref[g] += jnp.where(causal, _dot(dz, ln_scr[:, sl], NT_DIMS), 0.0)
            dbst_ref[:, g:g + 1] += jnp.sum(dz.astype(F32), axis=-1, keepdims=True)
        dln = dln_scr[...]
        dlng_ref[...] += jnp.sum(dln * xhat, axis=0, keepdims=True)
        dlnb_ref[...] += jnp.sum(dln, axis=0, keepdims=True)
        gy = dln * lng
        dgv = rs * (gy - jnp.mean(gy, axis=-1, keepdims=True)
                    - xhat * jnp.mean(gy * xhat, axis=-1, keepdims=True))
        dproj_ref[:, w:2 * w] = (dgv * dgv_dav).astype(BF16)
        dproj_ref[:, 3 * w:6 * w] = jnp.zeros((HEAD, 3 * w), BF16)

        tot_ref[...] = _merge_b((o1, o2, o3), (l1, l2, l3), yb_scr)
        yb = yb_scr[...]
        rb = _rms(yb)
        bz = bz_ref[...]
        silu_b, dsilu_b = _silu_and_grad(bz)
        dyb_all = dy_ref[:, w:].astype(F32)
        dnb = dyb_all * silu_b
        dproj_ref[:, 6 * w:] = (dyb_all * yb * rb * gb_ref[...] * dsilu_b).astype(BF16)
        dgb_ref[...] += jnp.sum(dnb * yb * rb, axis=0, keepdims=True)
        dyb = _rms_bwd(dnb, yb, rb, gb_ref[...])
        dyb_ref[...] = dyb
        prod = dyb * yb
        dl_ref[...] = _lane_pick(
            [jnp.sum(prod[:, h * HEAD:(h + 1) * HEAD], axis=-1, keepdims=True) for h in range(n_groups)], HEAD)

    seg = lambda j: pl.BlockSpec((HEAD, w), lambda i, j=j: (i, j))
    row_w = pl.BlockSpec((HEAD, w), lambda i: (i, 0))
    stat = pl.BlockSpec((HEAD, HEAD), lambda i: (i, 0))
    vec = pl.BlockSpec((1, w), lambda i: (0, 0))
    ws_spec = pl.BlockSpec((n_groups, HEAD, HEAD), lambda i: (0, 0, 0))
    bst_spec = pl.BlockSpec((HEAD, n_groups), lambda i: (0, 0))
    vec_shape = jax.ShapeDtypeStruct((1, w), F32)
    return pl.pallas_call(
        body, name="mix_bwd", grid=(s // HEAD,),
        out_shape=[jax.ShapeDtypeStruct((s, N_SEG * w), BF16), jax.ShapeDtypeStruct((s, w), F32),
                   jax.ShapeDtypeStruct((s, HEAD), F32), jax.ShapeDtypeStruct((s, HEAD), F32),
                   jax.ShapeDtypeStruct((n_groups, HEAD, HEAD), F32), jax.ShapeDtypeStruct((HEAD, n_groups), F32),
                   vec_shape, vec_shape, vec_shape, vec_shape],
        in_specs=[seg(0), seg(1), seg(2), seg(6), pl.BlockSpec((HEAD, 2 * w), lambda i: (i, 0)),
                  row_w, row_w, row_w, stat, stat, stat, ws_spec, bst_spec, vec, vec, vec, vec],
        out_specs=[pl.BlockSpec((HEAD, N_SEG * w), lambda i: (i, 0)), row_w, stat, stat,
                   ws_spec, bst_spec, vec, vec, vec, vec],
        scratch_shapes=[pltpu.VMEM((HEAD, w), F32), pltpu.VMEM((HEAD, w), BF16), pltpu.VMEM((HEAD, w), F32),
                        pltpu.VMEM((HEAD, w), BF16), pltpu.VMEM((HEAD, w), F32)],
        compiler_params=_params(("arbitrary",), 56),
    )(proj, proj, proj, proj, dy, *outs, *lses, w_s, bst, ln_g, ln_b, g_a, g_b)


def _local_step(x, p, tgt, small, wg, ex, while_last_travels=None):
    s, d = x.shape
    n, _, c_in = wg.buffers["w_in"].shape
    assert n == N_DEV
    d_in = n * c_in
    w = d_in // N_SEG
    n_heads = w // HEAD
    p_dim, c_up = wg.buffers["w_ple_up"].shape[1:]
    assert s % (HEAD * DILATIONS[-1]) == 0 and w % HEAD == 0 and d == n * c_up == 2 * w

    near = (2, 4)
    wg.start("gather_in_pair", ["w_in"], (1,))
    token = wg.start("gather_in_first", ["w_in"], ("first",))
    rest = ["w_out", "w_ple_gate", "w_ple_up"]

    hn = _rmsnorm_fwd(x, small["g_pre"], "pre_norm", True, dep=token)
    tm = _tile(s, 1024)

    def in_proj(shards, name, carry, dep=None):
        return _matmul(
            hn, wg.buffers["w_in"], name=name, grid=(s // tm, len(shards), 1), dims=NN_DIMS,
            prefetch=jnp.stack(shards).astype(jnp.int32),
            a_spec=pl.BlockSpec((tm, d), lambda i, j, k, sh: (i, 0)),
            b_spec=pl.BlockSpec((None, d, c_in), lambda i, j, k, sh: (sh[j], 0, 0), pipeline_mode=pl.Buffered(1)),
            acc_shape=(tm, c_in), out_shapes=[jax.ShapeDtypeStruct((s, d_in), F32)],
            out_specs=[pl.BlockSpec((tm, c_in), lambda i, j, k, sh: (i, sh[j]))], carry=carry, dep=dep,
            vmem_mib=56)[0]

    me = wg.me
    core = me & 1
    first, second, far = me ^ (4 - 2 * core), me ^ (2 + 2 * core), me ^ 6
    proj = in_proj([me], "in_proj_own", None)
    token = wg.start("gather_in_second", ["w_in"], ("second",), after=[proj])
    wg.arrived("gather_in_pair", [token])
    proj = in_proj([me ^ 1], "in_proj_sibling", proj)
    buckets = jnp.stack([_rel_buckets(dil) for dil in DILATIONS])
    bias = _bias_build(small["rel_bias"], buckets, n_heads)
    ahead = [bias] + [wg.buffers[k] for k in rest]
    for tag, mine, from_sibling in (("first", first, second ^ 1), ("second", second, first ^ 1), ("far", far, far ^ 1)):
        token = wg.forward("gather_in_" + tag, [proj] + ahead)
        ahead = []
        if tag == "second":
            token = wg.relay_start("gather_in_far", ["w_in"], after=[token])
        elif tag == "far":
            token = wg.start("gather_rest", rest, (1,) + near, after=[token])
        proj = in_proj([mine], "in_proj_" + tag, proj, dep=token)
        wg.forwarded("gather_in_" + tag, [proj])
        proj = in_proj([from_sibling], "in_proj_%s_forwarded" % tag, proj)
    win_g = wg.buffers["w_in"]

    qn, kn = _qkv_prep(proj, small["g_q"], small["g_k"], w)
    token = wg.forward("gather_rest", [qn])
    token = wg.relay_start("gather_rest_far", rest, after=[token])
    outs, lses = [], []
    for c, dil in enumerate(DILATIONS):
        o, l = _attn_fwd(qn, kn, proj, bias[c], dil, "attn_fwd_d%d" % dil, dep=token)
        outs.append(o)
        lses.append(l)
    token = wg.forward("gather_rest_far", outs)

    ws_p = small["w_s"][:, CHUNK_ORDER][:, :, CHUNK_ORDER]
    bst = small["b_s"].T[CHUNK_ORDER]
    mix_args = (outs, lses, ws_p, bst, small["ln_v_g"], small["ln_v_b"], small["g_out_a"], small["g_out_b"], w)
    y = _mix_fwd(proj, *mix_args, dep=token)
    wg.forwarded("gather_rest", [y])
    wg.forwarded("gather_rest_far", [y])
    wout_g, wgate_g, wup_g = (wg.buffers[k] for k in rest)
    wout_f = wout_g.reshape(2 * w, d)
    wgate_f = wgate_g.reshape(d, d)

    tn = _tile(d, 1024)
    tk2 = 2 * w

    def resid_epilogue(acc, ex, outs_):
        outs_[0][...] = ex[0][...] + acc[...]

    h = _matmul(
        y, wout_f, name="out_proj", grid=(s // tm, d // tn, (2 * w) // tk2), dims=NN_DIMS,
        a_spec=pl.BlockSpec((tm, tk2), lambda i, j, k: (i, k)),
        b_spec=pl.BlockSpec((tk2, tn), lambda i, j, k: (k, j)),
        acc_shape=(tm, tn), out_shapes=[jax.ShapeDtypeStruct((s, d), F32)],
        out_specs=[pl.BlockSpec((tm, tn), lambda i, j, k: (i, j))],
        extra=(x,), extra_specs=(pl.BlockSpec((tm, tn), lambda i, j, k: (i, j)),),
        epilogue=resid_epilogue, vmem_mib=56)[0]

    hn2 = _rmsnorm_fwd(h, small["g_ple"], "ple_norm", False)

    tmg = _tile(s, 512)

    def ple_epilogue(acc, ex, outs_):
        h_ref, p_ref, wup_ref, tgt_ref = ex
        dout_ref, dpre_ref, dup_ref, loss_ref = outs_
        gate = jax.nn.sigmoid(acc[...])
        up = _dot(p_ref[...].astype(BF16), wup_ref[...])
        err = h_ref[...] + gate * up - tgt_ref[...]
        dout = err * (1.0 / d)
        dout_ref[...] = dout
        dpre_ref[...] = (dout * up * gate * (1.0 - gate)).astype(BF16)
        dup_ref[...] = (dout * gate).astype(BF16)
        part = 0.5 * jnp.sum(err * err) * (1.0 / d)
        rr = lax.broadcasted_iota(jnp.int32, (8, HEAD), 0)
        cc = lax.broadcasted_iota(jnp.int32, (8, HEAD), 1)
        loss_ref[...] = jnp.where((rr == 0) & (cc == 0), part, 0.0)

    tile_ij = pl.BlockSpec((tmg, c_up), lambda i, j, k: (i, j))
    dout, dpre, dup, loss_parts = _matmul(
        hn2, wgate_f, name="ple_gate", grid=(s // tmg, n, 1), dims=NN_DIMS,
        a_spec=pl.BlockSpec((tmg, d), lambda i, j, k: (i, 0)),
        b_spec=pl.BlockSpec((d, c_up), lambda i, j, k: (0, j)),
        acc_shape=(tmg, c_up),
        out_shapes=[jax.ShapeDtypeStruct((s, d), F32), jax.ShapeDtypeStruct((s, d), BF16),
                    jax.ShapeDtypeStruct((s, d), BF16), jax.ShapeDtypeStruct((s // tmg * 8, n * HEAD), F32)],
        out_specs=[tile_ij, tile_ij, tile_ij, pl.BlockSpec((8, HEAD), lambda i, j, k: (i, j))],
        extra=(h, p, wup_g, tgt),
        extra_specs=(tile_ij, pl.BlockSpec((tmg, p_dim), lambda i, j, k: (i, 0)),
                     pl.BlockSpec((None, p_dim, c_up), lambda i, j, k: (j, 0, 0)), tile_ij),
        epilogue=ple_epilogue)
    loss = jnp.sum(loss_parts)

    tks = _tile(s, 2048)
    g_wup = _matmul(
        p, dup, name="grad_w_up", grid=(1, n, s // tks), dims=TN_DIMS,
        a_spec=pl.BlockSpec((tks, p_dim), lambda i, j, k: (k, 0)),
        b_spec=pl.BlockSpec((tks, c_up), lambda i, j, k: (k, j)),
        acc_shape=(p_dim, c_up), out_shapes=[jax.ShapeDtypeStruct((n, p_dim, c_up), BF16)],
        out_specs=[pl.BlockSpec((None, p_dim, c_up), lambda i, j, k: (j, 0, 0))])[0]

    def tn_matmul(a, b, name):
        m_, n_ = a.shape[1], b.shape[1]
        bm, bn = _tile(m_, 1024), _tile(n_, 1024)
        return _matmul(
            a, b, name=name, grid=(m_ // bm, n_ // bn, 1), dims=TN_DIMS,
            a_spec=pl.BlockSpec((s, bm), lambda i, j, k: (0, i)),
            b_spec=pl.BlockSpec((s, bn), lambda i, j, k: (0, j)),
            acc_shape=(bm, bn), out_shapes=[jax.ShapeDtypeStruct((m_, n_), BF16)],
            out_specs=[pl.BlockSpec((bm, bn), lambda i, j, k: (i, j))], vmem_mib=56)[0]

    def nt_matmul(a, b, name, out_dtype, dep=None):
        k_, n_ = a.shape[1], b.shape[0]
        bm, bn, bk = _tile(s, 1024), _tile(n_, 1024), k_
        return _matmul(
            a, b, name=name, grid=(s // bm, n_ // bn, k_ // bk), dims=NT_DIMS,
            a_spec=pl.BlockSpec((bm, bk), lambda i, j, k: (i, k)),
            b_spec=pl.BlockSpec((bn, bk), lambda i, j, k: (j, k)),
            acc_shape=(bm, bn), out_shapes=[jax.ShapeDtypeStruct((s, n_), out_dtype)],
            out_specs=[pl.BlockSpec((bm, bn), lambda i, j, k: (i, j))], dep=dep, vmem_mib=56)[0]

    by_core = lambda g: g.reshape((N_CHIP, 2) + g.shape[-2:])
    g_wgate = tn_matmul(hn2, dpre, "grad_w_gate").reshape(wgate_g.shape)
    dhn2 = nt_matmul(dpre, wgate_f, "ple_gate_bwd", BF16)
    dh, dh_b, dh_bp, dg_ple = _rmsnorm_bwd(dhn2, h, small["g_ple"], dout, "ple_norm_bwd", False, True)
    g_wout = tn_matmul(y, dh_b, "grad_w_out").reshape(wout_g.shape)

    late = ("w_out", "w_ple_gate", "w_ple_up")
    late_parts = (g_wout, g_wgate, g_wup)
    token = ex.push_pairs("pair_late", [by_core(g) for g in late_parts])
    dy = nt_matmul(dh_bp, wout_f, "out_proj_bwd", BF16, dep=token)
    (dproj, dyb, lse_tot, delta, dws, dbst, dlng, dlnb, dga, dgb) = _mix_bwd(proj, dy, *mix_args)
    both_columns, from_sibling = ex.pairs_done("pair_late", [dproj])
    pair_sums = [_pair_add(mine.reshape((N_DEV,) + mine.shape[-2:]), theirs, "pair_add_" + k, ex.core)
                 for k, mine, theirs in zip(late, both_columns, from_sibling)]
    token = ex.push_chips("chip_late", pair_sums)

    running, dss = None, []
    for c, dil in enumerate(DILATIONS):
        *running, ds = _attn_bwd(qn, kn, proj, dyb, lse_tot, delta, bias[c], dil, "attn_bwd_d%d" % dil,
                                 running=running, dep=token if c == 0 else None)
        dss.append(ds)
    d_rel = _bias_grad(jnp.stack(dss), buckets, n_heads)
    dproj, dgq, dgk = _qkv_bwd(dproj, proj, *running, small["g_q"], small["g_k"], w)
    pair_sums, landed, _ = ex.chips_done("chip_late", [dproj])
    delivered = {k: (mine, theirs) for k, mine, theirs in zip(late, pair_sums, landed)}

    token_row = np.argsort(CHUNK_ORDER)
    dws = dws[:, token_row][:, :, token_row]
    dbst = dbst[token_row]
    small_grads = {
        "w_s": dws, "b_s": dbst.T, "ln_v_g": dlng, "ln_v_b": dlnb, "g_q": dgq, "g_k": dgk,
        "rel_bias": d_rel, "g_out_a": dga, "g_out_b": dgb, "g_ple": dg_ple,
    }

    bm = _tile(d, 1024)

    def grad_w_in(core, name, dep=None):
        return _matmul(
            hn, dproj, name=name, grid=(d // bm, N_CHIP, 1), dims=TN_DIMS, prefetch=core.reshape(1),
            a_spec=pl.BlockSpec((s, bm), lambda i, j, k, core_ref: (0, i)),
            b_spec=pl.BlockSpec((s, c_in), lambda i, j, k, core_ref: (0, 2 * j + core_ref[0])),
            acc_shape=(bm, c_in), out_shapes=[jax.ShapeDtypeStruct((N_CHIP, d, c_in), BF16)],
            out_specs=[pl.BlockSpec((None, bm, c_in), lambda i, j, k, core_ref: (j, i, 0))], dep=dep,
            vmem_mib=60)[0]

    for_sibling = grad_w_in(1 - ex.core, "grad_w_in_sibling")
    token = ex.push_pairs("pair_in", [for_sibling])
    mine = grad_w_in(ex.core, "grad_w_in_mine", dep=token)
    _, from_sibling = ex.pairs_done("pair_in", [mine])
    pair_sum = _pair_add(mine, from_sibling[0], "pair_add_w_in")
    token = ex.push_chips("chip_in", [pair_sum], _pack_small(small_grads, SMALL_EARLY))

    dhn = _matmul(
        dproj, win_g, name="in_proj_bwd", grid=(s // tm, d // tn, n // 2), dims=NT_DIMS,
        a_spec=pl.BlockSpec((tm, 2 * c_in), lambda i, j, k: (i, k)),
        b_spec=pl.BlockSpec((2, tn, c_in), lambda i, j, k: (k, j, 0)),
        acc_shape=(tm, tn), out_shapes=[jax.ShapeDtypeStruct((s, d), BF16)],
        out_specs=[pl.BlockSpec((tm, tn), lambda i, j, k: (i, j))], dep=token, vmem_mib=56)[0]
    grad_x, dg_pre = _rmsnorm_bwd(dhn, x, small["g_pre"], dh, "pre_norm_bwd", True, False)
    extra = while_last_travels(token, delivered, dg_pre) if while_last_travels is not None else []
    pair_sums, landed, slabs = ex.chips_done("chip_in", [grad_x] + list(extra))
    delivered["w_in"] = (pair_sums[0], landed[0])
    small_grads["g_pre"] = dg_pre
    return loss, grad_x, small_grads, delivered, slabs, extra


SMALL_EARLY = ("w_s", "b_s", "ln_v_g", "ln_v_b", "g_q", "g_k", "rel_bias", "g_out_a", "g_out_b", "g_ple")
SMALL_LAST = ("g_pre",)
SMALL_NAMES = SMALL_LAST + SMALL_EARLY


def _pack_small(tree, names):
    parts = []
    for name in names:
        flat = tree[name].astype(F32).reshape(-1)
        pad = (-flat.shape[0]) % HEAD
        parts.append(jnp.pad(flat, (0, pad)) if pad else flat)
    slab = jnp.concatenate(parts).reshape(-1, HEAD)
    pad_rows = (-slab.shape[0]) % 8
    return jnp.pad(slab, ((0, pad_rows), (0, 0))) if pad_rows else slab


def _unpack_small(slab, like, names):
    flat = slab.reshape(-1)
    out, off = {}, 0
    for name in names:
        size = like[name].size
        out[name] = flat[off:off + size].reshape(like[name].shape)
        off += size + (-size) % HEAD
    return out


def _peer(k):
    x, y, c = (lax.axis_index(a) for a in AXES)
    if k == "first":
        px, py, pc = x ^ (1 - c), y ^ c, c
    elif k == "second":
        px, py, pc = x ^ c, y ^ (1 - c), c
    else:
        bits = ((k >> 2) & 1, (k >> 1) & 1, k & 1)
        px, py, pc = (1 - v if b else v for v, b in zip((x, y, c), bits))
    return (px, py, pc), 4 * px + 2 * py + pc


def _my_index():
    x, y, c = (lax.axis_index(a) for a in AXES)
    return 4 * x + 2 * y + c


N_CHIP = 4
HBM_SPEC = pl.BlockSpec(memory_space=pl.ANY)


def _remote(src, dst, send_sem, recv_sem, peer):
    return pltpu.make_async_remote_copy(src_ref=src, dst_ref=dst, send_sem=send_sem, recv_sem=recv_sem,
                                        device_id=peer, device_id_type=pl.DeviceIdType.MESH)


SEM_SPEC = pl.BlockSpec(memory_space=pltpu.SEMAPHORE)
HBM_ONLY = pl.BlockSpec(memory_space=pltpu.HBM)
DATAFLOW = pltpu.SideEffectType.DATAFLOW_SIDE_EFFECTING


def _comm_call(name, arrays, *, wait=None, start=None, after=()):
    n, n_after = len(arrays), len(after)

    def body(*refs):
        ins = refs[:n]
        pos = n
        if wait is not None:
            for cp in wait[2](ins, refs[pos], refs[pos + 1]):
                cp.wait()
            pos += 2
        outs = refs[pos + n_after:]
        if start is not None:
            for cp in start[1](ins, outs[0], outs[1]):
                cp.start()
        outs[-1][...] = jnp.zeros_like(outs[-1])

    operands = [pltpu.with_memory_space_constraint(a, pltpu.HBM) for a in arrays]
    in_specs = [HBM_ONLY] * n
    if wait is not None:
        operands += [wait[0], wait[1]]
        in_specs += [SEM_SPEC, SEM_SPEC]
    operands += list(after)
    in_specs += [HBM_SPEC] * n_after
    out_shape, out_specs = [], []
    if start is not None:
        out_shape += [pltpu.SemaphoreType.DMA((start[0],))] * 2
        out_specs += [SEM_SPEC, SEM_SPEC]
    first = len(out_shape)
    out_shape += [pltpu.HBM(a.shape, a.dtype) for a in arrays] + [jax.ShapeDtypeStruct((SUB, HEAD), F32)]
    out_specs += [HBM_ONLY] * n + [pl.BlockSpec(memory_space=pltpu.VMEM)]
    res = pl.pallas_call(
        body, name=name, out_shape=tuple(out_shape), in_specs=tuple(in_specs), out_specs=tuple(out_specs),
        input_output_aliases={i: first + i for i in range(n)},
        compiler_params=pltpu.CompilerParams(has_side_effects=DATAFLOW),
    )(*operands)
    sems = (res[0], res[1]) if start is not None else None
    return list(res[first:first + n]), sems, res[-1]


class _GradExchange:
    def __init__(self):
        x, y, c = (lax.axis_index(a) for a in AXES)
        self.core = c.astype(jnp.int32)
        self.chip = (2 * x + y).astype(jnp.int32)
        self.pending = {}

    def _pair_copies(self, n_arr):
        def make(refs, send_sems, recv_sems):
            sibling, _ = _peer(1)
            other = 1 - lax.axis_index("c")
            srcs, lands = refs[:n_arr], refs[n_arr:]
            pick = lambda ref, ch: ref.at[ch, other] if len(ref.shape) == 4 else ref.at[ch]
            return [_remote(pick(srcs[a], ch), lands[a].at[ch], send_sems.at[a * N_CHIP + ch],
                            recv_sems.at[a * N_CHIP + ch], sibling)
                    for a in range(n_arr) for ch in range(N_CHIP)]
        return make

    def _chip_copies(self, n_arr, with_slab):
        def make(refs, send_sems, recv_sems):
            x, y = lax.axis_index("x"), lax.axis_index("y")
            my_chip = 2 * x + y
            srcs, lands = refs[:n_arr], refs[n_arr:2 * n_arr]
            copies = []
            for j, k in enumerate((2, 4, 6)):
                peer, peer_idx = _peer(k)
                for a in range(n_arr):
                    copies.append(_remote(srcs[a].at[peer_idx // 2], lands[a].at[my_chip],
                                          send_sems.at[3 * a + j], recv_sems.at[3 * a + j], peer))
            if with_slab:
                slab, slab_land = refs[2 * n_arr], refs[2 * n_arr + 1]
                for k in range(1, N_DEV):
                    peer, _ = _peer(k)
                    copies.append(_remote(slab, slab_land.at[_my_index()], send_sems.at[3 * n_arr + k - 1],
                                          recv_sems.at[3 * n_arr + k - 1], peer))
            return copies
        return make

    def push_pairs(self, tag, for_sibling):
        n_arr = len(for_sibling)
        lands = [lax.empty((N_CHIP,) + a.shape[-2:], a.dtype) for a in for_sibling]
        make = self._pair_copies(n_arr)
        arrays, sems, token = _comm_call(tag + "_start", list(for_sibling) + lands, start=(n_arr * N_CHIP, make))
        self.pending[tag] = (arrays, sems, make, n_arr)
        return token

    def pairs_done(self, tag, after):
        arrays, sems, make, n_arr = self.pending.pop(tag)
        arrays, _, _ = _comm_call(tag + "_wait", arrays, wait=(sems[0], sems[1], make), after=after)
        return arrays[:n_arr], arrays[n_arr:]

    def push_chips(self, tag, pair_sums, slab=None):
        n_arr = len(pair_sums)
        arrays = list(pair_sums) + [lax.empty(a.shape, a.dtype) for a in pair_sums]
        n_copies = 3 * n_arr
        if slab is not None:
            arrays += [slab, lax.empty((N_DEV,) + slab.shape, slab.dtype)]
            n_copies += N_DEV - 1
        make = self._chip_copies(n_arr, slab is not None)
        arrays, sems, token = _comm_call(tag + "_start", arrays, start=(n_copies, make))
        self.pending[tag] = (arrays, sems, make, n_arr)
        return token

    def chips_done(self, tag, after):
        arrays, sems, make, n_arr = self.pending.pop(tag)
        arrays, _, _ = _comm_call(tag + "_wait", arrays, wait=(sems[0], sems[1], make), after=after)
        return arrays[:n_arr], arrays[n_arr:2 * n_arr], arrays[2 * n_arr:]


def _cast_place(w, name):
    r, c = w.shape
    tr = r if r * c <= MIB else 1 << ((MIB // c).bit_length() - 1)
    assert r % tr == 0

    def body(me_ref, w_ref, o_ref):
        o_ref[...] = w_ref[...].astype(BF16)

    return pl.pallas_call(
        body, name=name, out_shape=jax.ShapeDtypeStruct((N_DEV, r, c), BF16),
        grid_spec=pltpu.PrefetchScalarGridSpec(
            num_scalar_prefetch=1, grid=(r // tr,),
            in_specs=[pl.BlockSpec((tr, c), lambda i, me_ref: (i, 0))],
            out_specs=pl.BlockSpec((None, tr, c), lambda i, me_ref: (me_ref[0], i, 0))),
        compiler_params=_params(("arbitrary",), 40),
    )(_my_index().astype(jnp.int32).reshape(1), w)


class _WeightGather:
    CHIPS = (2, 4, 6)

    def __init__(self, buffers):
        self.buffers = dict(buffers)
        self.pending = {}
        self.me = _my_index().astype(jnp.int32)

    def _own_slot_to(self, peers):
        def make(refs, send_sems, recv_sems):
            me = _my_index()
            return [_remote(ref.at[me], ref.at[me], send_sems.at[len(peers) * a + j],
                            recv_sems.at[len(peers) * a + j], _peer(k)[0])
                    for a, ref in enumerate(refs) for j, k in enumerate(peers)]
        return make

    def _forward_from(self, chips):
        def make(refs, send_sems, recv_sems):
            sibling, _ = _peer(1)
            copies = []
            for a, ref in enumerate(refs):
                for j, k in enumerate(chips):
                    slot = ref.at[_peer(k)[1]]
                    copies.append(_remote(slot, slot, send_sems.at[len(chips) * a + j],
                                          recv_sems.at[len(chips) * a + j], sibling))
            return copies
        return make

    def _run(self, call, names, **kw):
        arrays, sems, token = _comm_call(call, [self.buffers[k] for k in names], **kw)
        self.buffers.update(zip(names, arrays))
        return sems, token

    def start(self, tag, names, peers, after=()):
        make = self._own_slot_to(peers)
        sems, token = self._run(tag + "_start", names, start=(len(names) * len(peers), make), after=after)
        self.pending[tag] = (names, sems, make, peers)
        return token

    @staticmethod
    def _relay(refs, send_sems, recv_sems):
        x, y, c = (lax.axis_index(a) for a in AXES)
        peer = (x ^ (1 - c), y ^ c, c)
        slot = _my_index() ^ (2 + 2 * c)
        return [_remote(ref.at[slot], ref.at[slot], send_sems.at[a], recv_sems.at[a], peer)
                for a, ref in enumerate(refs)]

    def relay_start(self, tag, names, after=()):
        sems, token = self._run(tag + "_start", names, start=(len(names), self._relay), after=after)
        self.pending[tag] = (names, sems, self._relay, (6,))
        return token

    def arrived(self, tag, after):
        names, sems, make, _ = self.pending.pop(tag)
        self._run(tag + "_wait", names, wait=(sems[0], sems[1], make), after=after)

    def forward(self, tag, after):
        names, sems, make, peers = self.pending.pop(tag)
        chips = tuple(k for k in peers if k != 1)
        onward = self._forward_from(chips)
        new_sems, token = self._run(tag + "_forward", names, wait=(sems[0], sems[1], make),
                                    start=(len(chips) * len(names), onward), after=after)
        self.pending[tag + "/fwd"] = (names, new_sems, onward)
        return token

    def forwarded(self, tag, after):
        names, sems, make = self.pending.pop(tag + "/fwd")
        self._run(tag + "_done", names, wait=(sems[0], sems[1], make), after=after)


def _pair_add(mine, theirs, name, core=None):
    _, r, c_dim = theirs.shape
    tr = r if r * c_dim <= MIB else 1 << ((MIB // c_dim).bit_length() - 1)
    assert r % tr == 0
    stride = 1 if core is None else 2
    offset = jnp.zeros((1,), jnp.int32) if core is None else core.reshape(1)

    def body(off_ref, a_ref, b_ref, o_ref):
        o_ref[...] = (a_ref[...].astype(F32) + b_ref[...].astype(F32)).astype(BF16)

    blk = (None, tr, c_dim)
    return pl.pallas_call(
        body, name=name, out_shape=jax.ShapeDtypeStruct(theirs.shape, BF16),
        grid_spec=pltpu.PrefetchScalarGridSpec(
            num_scalar_prefetch=1, grid=(N_CHIP, r // tr),
            in_specs=[pl.BlockSpec(blk, lambda ch, i, off_ref: (stride * ch + off_ref[0], i, 0)),
                      pl.BlockSpec(blk, lambda ch, i, off_ref: (ch, i, 0))],
            out_specs=pl.BlockSpec(blk, lambda ch, i, off_ref: (ch, i, 0))),
        compiler_params=_params(("arbitrary", "arbitrary"), 40),
    )(offset, mine, theirs)


def _slab_exchange(slab, name):
    def body(slab_in, slab_out, send_sems, recv_sems, local_sem):
        me = _my_index()
        local = pltpu.make_async_copy(slab_in, slab_out.at[me], local_sem)
        local.start()
        sends = []
        for k in range(1, N_DEV):
            peer, _ = _peer(k)
            sends.append(_remote(slab_in, slab_out.at[me], send_sems.at[k - 1], recv_sems.at[k - 1], peer))
        for cp in sends:
            cp.start()
        for k in range(1, N_DEV):
            peer, peer_idx = _peer(k)
            slot = slab_out.at[peer_idx]
            _remote(slot, slot, send_sems.at[k - 1], recv_sems.at[k - 1], peer).wait_recv()
        for cp in sends:
            cp.wait_send()
        local.wait()

    return pl.pallas_call(
        body, name=name, out_shape=jax.ShapeDtypeStruct((N_DEV,) + slab.shape, slab.dtype),
        in_specs=[HBM_SPEC], out_specs=HBM_SPEC,
        scratch_shapes=[pltpu.SemaphoreType.DMA((N_DEV - 1,)), pltpu.SemaphoreType.DMA((N_DEV - 1,)),
                        pltpu.SemaphoreType.DMA],
        compiler_params=pltpu.CompilerParams(has_side_effects=True),
    )(slab)


def _adamw_math(w, g, m, v):
    m = ADAM_B1 * m + (1.0 - ADAM_B1) * g
    v = ADAM_B2 * v + (1.0 - ADAM_B2) * (g * g)
    m_hat = m / (1.0 - ADAM_B1 ** ADAM_STEP)
    v_hat = v / (1.0 - ADAM_B2 ** ADAM_STEP)
    delta = -ADAM_LR * (m_hat / (jnp.sqrt(v_hat) + ADAM_EPS) + ADAM_WD * w)
    return delta, m, v


def _adamw(parts, own, place, w, m, v, name, dep=None):
    n_parts = parts.shape[0]
    r, c = w.shape
    budget = 280 * 1024
    tr = r if r * c <= budget else 1 << ((budget // c).bit_length() - 1)
    assert r % tr == 0

    def body(place_ref, p_ref, own_ref, w_ref, m_ref, v_ref, g_ref, d_ref, nm_ref, nv_ref):
        mine = own_ref[...].astype(F32)
        g = None
        for i in range(n_parts):
            term = jnp.where(place_ref[0] == i, mine, p_ref[i].astype(F32))
            g = term if g is None else g + term
        delta, nm, nv = _adamw_math(w_ref[...], g, m_ref[...], v_ref[...])
        g_ref[...] = g
        d_ref[...] = delta
        nm_ref[...] = nm
        nv_ref[...] = nv

    blk = pl.BlockSpec((tr, c), lambda i, place_ref: (i, 0))
    shape = jax.ShapeDtypeStruct((r, c), F32)
    in_specs = [pl.BlockSpec((n_parts, tr, c), lambda i, place_ref: (0, i, 0)),
                pl.BlockSpec((None, tr, c), lambda i, place_ref: (place_ref[1], i, 0)), blk, blk, blk]
    operands = [parts, own, w, m, v]
    if dep is not None:
        body = _drop_arg(body, 1 + len(operands))
        in_specs.append(pl.BlockSpec((SUB, HEAD), lambda i, place_ref: (0, 0)))
        operands.append(dep)
    return pl.pallas_call(
        body, name=name, out_shape=[shape] * 4,
        grid_spec=pltpu.PrefetchScalarGridSpec(
            num_scalar_prefetch=1, grid=(r // tr,), in_specs=in_specs, out_specs=[blk] * 4),
        compiler_params=_params(("arbitrary",), 48),
    )(place, *operands)


def kernel(x, p, g_pre, w_in, w_s, b_s, ln_v_g, ln_v_b, g_q, g_k, rel_bias, g_out_a, g_out_b, w_out, g_ple, w_ple_gate, w_ple_up, loss_target, m_g_pre, m_w_in, m_w_s, m_b_s, m_ln_v_g, m_ln_v_b, m_g_q, m_g_k, m_rel_bias, m_g_out_a, m_g_out_b, m_w_out, m_g_ple, m_w_ple_gate, m_w_ple_up, v_g_pre, v_w_in, v_w_s, v_b_s, v_ln_v_g, v_ln_v_b, v_g_q, v_g_k, v_rel_bias, v_g_out_a, v_g_out_b, v_w_out, v_g_ple, v_w_ple_gate, v_w_ple_up):
    args = dict(locals())
    small = {"g_pre": g_pre, "w_s": w_s[0], "b_s": b_s[0], "ln_v_g": ln_v_g, "ln_v_b": ln_v_b, "g_q": g_q,
             "g_k": g_k, "rel_bias": rel_bias, "g_out_a": g_out_a, "g_out_b": g_out_b, "g_ple": g_ple}
    big_names = ("w_in", "w_out", "w_ple_gate", "w_ple_up")
    big = {k: args[k][0] for k in big_names}

    wg = _WeightGather({k: _cast_place(big[k], "place_" + k) for k in big_names})
    ex = _GradExchange()
    results = {}

    def big_adamw(k, delivered, dep=None):
        mine, theirs = delivered[k]
        place = jnp.stack([ex.chip, ex.chip])
        return _adamw(theirs, mine, place, big[k], args["m_" + k][0], args["v_" + k][0], "adamw_" + k, dep=dep)

    squeeze = lambda t: {k: (t[k][0] if k in ("w_s", "b_s") else t[k]) for k in SMALL_NAMES}
    small_m = squeeze({k: args["m_" + k] for k in SMALL_NAMES})
    small_v = squeeze({k: args["v_" + k] for k in SMALL_NAMES})
    slab_place = jnp.stack([_my_index().astype(jnp.int32), jnp.zeros((), jnp.int32)])

    def small_adamw(names_, parts, own, call_name):
        packed = _adamw(parts, own[None], slab_place, _pack_small(small, names_), _pack_small(small_m, names_),
                        _pack_small(small_v, names_), call_name)
        for idx in range(4):
            tree = _unpack_small(packed[idx], small, names_)
            for k in names_:
                results.setdefault(k, [None] * 4)[idx] = tree[k].reshape(args[k].shape)
        return packed[0]

    def while_last_travels(token, delivered, dg_pre):
        done = []
        for k in big_names[1:]:
            results[k] = big_adamw(k, delivered, dep=token)
            done.append(results[k][0])
        last_slab = _pack_small({"g_pre": dg_pre}, SMALL_LAST)
        done.append(small_adamw(SMALL_LAST, _slab_exchange(last_slab, "last_exchange"), last_slab, "adamw_last"))
        return done

    loss, grad_x, small_parts, delivered, slabs, _ = _local_step(
        x[0], p[0, 0], loss_target[0], small, wg, ex, while_last_travels)
    results["w_in"] = big_adamw("w_in", delivered)
    for k in big_names:
        results[k] = [t[None] for t in results[k]]
    small_adamw(SMALL_EARLY, slabs[1], slabs[0], "adamw_small")

    names = ("g_pre", "w_in", "w_s", "b_s", "ln_v_g", "ln_v_b", "g_q", "g_k", "rel_bias", "g_out_a", "g_out_b",
             "w_out", "g_ple", "w_ple_gate", "w_ple_up")
    total = lax.psum(loss, AXES)
    out = [total, grad_x[None]]
    for idx in range(4):
        out += [results[k][idx] for k in names]
    return tuple(out)
```

```python
import math

import numpy as np
import jax
import jax.numpy as jnp
from jax import lax
from jax.experimental import pallas as pl
from jax.experimental.pallas import tpu as pltpu

F32 = jnp.float32
BF16 = jnp.bfloat16
EPS = 1e-6
NEG_INF = -1e30
HEAD = 128
DILATIONS = (1, 4, 16)
NUM_BUCKETS = 32
MAX_DISTANCE = 2048
N_SEG = 7
ADAM_LR = 0.001
ADAM_B1 = 0.9
ADAM_B2 = 0.999
ADAM_EPS = 1e-08
ADAM_WD = 0.01
ADAM_STEP = 10
AXES = ("x", "y", "c")
N_DEV = 8
MIB = 1 << 20

SUB = 8

CHUNK_ORDER = np.array([16 * (r % SUB) + r // SUB for r in range(HEAD)])
BLOCK_ORDER = {
    1: CHUNK_ORDER,
    4: np.array([32 * (r // 32) + 4 * (r % SUB) + (r // SUB) % 4 for r in range(HEAD)]),
    16: np.arange(HEAD),
}

NT_DIMS = (((1,), (1,)), ((), ()))
TN_DIMS = (((0,), (0,)), ((), ()))
NN_DIMS = (((1,), (0,)), ((), ()))


def _params(semantics, vmem_mib):
    return pltpu.CompilerParams(dimension_semantics=semantics, vmem_limit_bytes=vmem_mib * MIB)


def _gelu(a):
    return 0.5 * a * (1.0 + lax.erf(a * (2.0 ** -0.5)))


def _gelu_and_grad(a):
    cdf = 0.5 * (1.0 + lax.erf(a * (2.0 ** -0.5)))
    return a * cdf, cdf + a * jnp.exp(-0.5 * a * a) * ((2.0 * math.pi) ** -0.5)


def _silu_and_grad(a):
    s = jax.nn.sigmoid(a)
    return a * s, s * (1.0 + a * (1.0 - s))


def _rms(v):
    return lax.rsqrt(jnp.mean(v * v, axis=-1, keepdims=True) + EPS)


def _rms_bwd(dy, v, r, g):
    gy = dy * g
    return r * gy - v * (r * r * r) * jnp.mean(gy * v, axis=-1, keepdims=True)


def _dot(a, b, dims=NN_DIMS):
    return lax.dot_general(a, b, dims, preferred_element_type=F32)


def _lane_pick(cols, width):
    rows = cols[0].shape[0]
    lane = lax.broadcasted_iota(jnp.int32, (rows, width), 1)
    out = jnp.zeros((rows, width), F32)
    for h, col in enumerate(cols):
        out = jnp.where(lane == h, col, out)
    return out


def _chunk_perm():
    return jnp.asarray(np.eye(HEAD, dtype=np.float32)[CHUNK_ORDER], BF16)


def _unpermute_f32(p, v):
    hi = v.astype(BF16)
    rest = v - hi.astype(F32)
    mid = rest.astype(BF16)
    lo = (rest - mid.astype(F32)).astype(BF16)
    return _dot(p, hi, TN_DIMS) + _dot(p, mid, TN_DIMS) + _dot(p, lo, TN_DIMS)


def _rmsnorm_fwd(x, g, name, permute, dep=None):
    s, d = x.shape
    tm = HEAD

    def body(x_ref, g_ref, p_ref, o_ref):
        v = x_ref[...]
        out = (v * _rms(v) * g_ref[...]).astype(BF16)
        if permute:
            out = _dot(p_ref[...], out).astype(BF16)
        o_ref[...] = out

    in_specs = [pl.BlockSpec((tm, d), lambda i: (i, 0)), pl.BlockSpec((1, d), lambda i: (0, 0)),
                pl.BlockSpec((HEAD, HEAD), lambda i: (0, 0))]
    operands = [x, g, _chunk_perm()]
    if dep is not None:
        body = _drop_arg(body, len(operands))
        in_specs.append(DEP_SPEC)
        operands.append(dep)
    return pl.pallas_call(
        body, name=name, grid=(s // tm,),
        out_shape=jax.ShapeDtypeStruct((s, d), BF16), in_specs=in_specs,
        out_specs=pl.BlockSpec((tm, d), lambda i: (i, 0)),
        compiler_params=_params(("arbitrary",), 40),
    )(*operands)


def _rmsnorm_bwd(dy, v, g, res, name, dy_permuted, with_bf16):
    s, d = v.shape
    tm = HEAD
    perm = _chunk_perm()

    def body(dy_ref, v_ref, g_ref, res_ref, p_ref, *outs):
        dg_ref = outs[-1]
        i = pl.program_id(0)
        vv, dyv = v_ref[...], dy_ref[...]
        if dy_permuted:
            dyv = _unpermute_f32(p_ref[...], dyv) if dyv.dtype == F32 else _dot(p_ref[...], dyv, TN_DIMS)
        dyv = dyv.astype(F32)
        r = _rms(vv)
        dx = res_ref[...].astype(F32) + _rms_bwd(dyv, vv, r, g_ref[...])
        if with_bf16:
            dxb = dx.astype(BF16)
            outs[0][...] = dxb
            outs[1][...] = _dot(p_ref[...], dxb).astype(BF16)
        else:
            outs[0][...] = dx

        @pl.when(i == 0)
        def _():
            dg_ref[...] = jnp.zeros_like(dg_ref)

        dg_ref[...] += jnp.sum(dyv * vv * r, axis=0, keepdims=True)

    row = pl.BlockSpec((tm, d), lambda i: (i, 0))
    vec = pl.BlockSpec((1, d), lambda i: (0, 0))
    if with_bf16:
        shapes = [jax.ShapeDtypeStruct((s, d), BF16)] * 2
        specs = [row, row]
    else:
        shapes = [jax.ShapeDtypeStruct((s, d), F32)]
        specs = [row]
    shapes.append(jax.ShapeDtypeStruct((1, d), F32))
    specs.append(vec)
    return pl.pallas_call(
        body, name=name, grid=(s // tm,), out_shape=shapes,
        in_specs=[row, row, vec, row, pl.BlockSpec((HEAD, HEAD), lambda i: (0, 0))], out_specs=specs,
        compiler_params=_params(("arbitrary",), 40),
    )(dy, v, g, res, perm)


DEP_SPEC = pl.BlockSpec((SUB, HEAD), lambda *_: (0, 0))


def _drop_arg(body, pos):
    return lambda *refs: body(*refs[:pos], *refs[pos + 1:])


def _matmul(a, b, *, name, grid, a_spec, b_spec, dims, acc_shape, out_shapes, out_specs,
            extra=(), extra_specs=(), epilogue=None, vmem_mib=48, dep=None, prefetch=None, carry=None):
    nk = grid[2]
    n_user = len(extra)
    for unread, spec in ((dep, DEP_SPEC), (carry, HBM_SPEC)):
        if unread is not None:
            extra, extra_specs = tuple(extra) + (unread,), tuple(extra_specs) + (spec,)
    n_extra, n_out = len(extra), len(out_shapes)
    n_pre = 0 if prefetch is None else 1
    aliases = {} if carry is None else {n_pre + 2 + n_extra - 1: 0}

    def body(*refs):
        refs = refs[n_pre:]
        a_ref, b_ref = refs[0], refs[1]
        ex = refs[2:2 + n_user]
        outs = refs[2 + n_extra:2 + n_extra + n_out]
        av = a_ref[...]
        if av.dtype != BF16:
            av = av.astype(BF16)
        if nk == 1 and epilogue is None:
            outs[0][...] = _dot(av, b_ref[...], dims).astype(outs[0].dtype)
            return
        acc = refs[-1]
        k = pl.program_id(2)

        @pl.when(k == 0)
        def _():
            acc[...] = jnp.zeros_like(acc)

        if len(b_ref.shape) == 3:
            span = b_ref.shape[2]
            acc[...] += sum(_dot(av[:, g * span:(g + 1) * span], b_ref[g], dims) for g in range(b_ref.shape[0]))
        else:
            acc[...] += _dot(av, b_ref[...], dims)

        @pl.when(k == nk - 1)
        def _():
            if epilogue is None:
                outs[0][...] = acc[...].astype(outs[0].dtype)
            else:
                epilogue(acc, ex, outs)

    direct = nk == 1 and epilogue is None
    scratch = [] if direct else [pltpu.VMEM(acc_shape, F32)]
    params = _params(("parallel", "parallel", "arbitrary"), vmem_mib)
    if prefetch is None:
        return pl.pallas_call(
            body, name=name, grid=grid, out_shape=list(out_shapes),
            in_specs=[a_spec, b_spec, *extra_specs], out_specs=list(out_specs),
            scratch_shapes=scratch, compiler_params=params, input_output_aliases=aliases,
        )(a, b, *extra)
    return pl.pallas_call(
        body, name=name, out_shape=list(out_shapes),
        grid_spec=pltpu.PrefetchScalarGridSpec(
            num_scalar_prefetch=1, grid=grid, in_specs=[a_spec, b_spec, *extra_specs],
            out_specs=list(out_specs), scratch_shapes=scratch),
        compiler_params=params, input_output_aliases=aliases,
    )(prefetch, a, b, *extra)


def _tile(n, want):
    t = min(n, want)
    while n % t:
        t //= 2
    return t


def _rel_buckets(dil):
    order = BLOCK_ORDER[dil]
    qi = jnp.asarray(HEAD + order)
    kj = jnp.asarray(np.concatenate([order, HEAD + order]))
    delta = qi[:, None] - kj[None, :]
    band = (delta >= 0) & (delta <= HEAD)
    dist = jnp.clip(delta, 0, None) * dil
    max_exact = NUM_BUCKETS // 2
    dd = jnp.maximum(dist, 1).astype(F32)
    large = max_exact + (jnp.log(dd / max_exact) / math.log(MAX_DISTANCE / max_exact)
                         * (NUM_BUCKETS - max_exact)).astype(jnp.int32)
    large = jnp.minimum(large, NUM_BUCKETS - 1)
    bucket = jnp.where(dist < max_exact, dist, large)
    return jnp.where(band, bucket, -1).astype(jnp.int32)


def _bias_build(rel_bias, buckets, n_heads):
    nd = buckets.shape[0]

    def body(rb_ref, bk_ref, o_ref):
        for c in range(nd):
            def per_head(h, carry, c=c):
                bk = bk_ref[c]
                acc = jnp.where(bk < 0, NEG_INF, 0.0).astype(F32)
                for b in range(NUM_BUCKETS):
                    acc = jnp.where(bk == b, rb_ref[b, h], acc)
                o_ref[c, h] = acc
                return carry

            lax.fori_loop(0, n_heads, per_head, 0)

    return pl.pallas_call(
        body, name="bias_build",
        out_shape=jax.ShapeDtypeStruct((nd, n_heads, HEAD, 2 * HEAD), F32),
        in_specs=[pl.BlockSpec(memory_space=pltpu.SMEM), pl.BlockSpec(memory_space=pltpu.VMEM)],
        out_specs=pl.BlockSpec(memory_space=pltpu.VMEM),
    )(rel_bias, buckets)


def _bias_grad(ds_all, buckets, n_heads):
    nd = buckets.shape[0]
    pairs = HEAD * 2 * HEAD

    def body(ds_ref, bk_ref, o_ref):
        rows = lax.broadcasted_iota(jnp.int32, (NUM_BUCKETS, pairs), 0)
        tot = jnp.zeros((n_heads, NUM_BUCKETS), F32)
        for c in range(nd):
            onehot = (rows == bk_ref[c]).astype(BF16)
            ds = ds_ref[c]
            hi = ds.astype(BF16)
            lo = (ds - hi.astype(F32)).astype(BF16)
            tot = tot + _dot(hi, onehot, NT_DIMS) + _dot(lo, onehot, NT_DIMS)
        o_ref[...] = tot

    out = pl.pallas_call(
        body, name="bias_grad",
        out_shape=jax.ShapeDtypeStruct((n_heads, NUM_BUCKETS), F32),
        in_specs=[pl.BlockSpec(memory_space=pltpu.VMEM), pl.BlockSpec(memory_space=pltpu.VMEM)],
        out_specs=pl.BlockSpec(memory_space=pltpu.VMEM),
        compiler_params=pltpu.CompilerParams(vmem_limit_bytes=40 * MIB),
    )(ds_all.reshape(nd, n_heads, pairs), buckets.reshape(nd, 1, pairs))
    return out.T


def _qkv_prep(proj, g_q, g_k, w, dep=None):
    s = proj.shape[0]
    n_heads = w // HEAD
    tm = HEAD

    def body(q_ref, k_ref, gq_ref, gk_ref, qn_ref, kn_ref):
        gq = gq_ref[...] * (HEAD ** -0.5)
        gk = gk_ref[...]
        for h in range(n_heads):
            sl = slice(h * HEAD, (h + 1) * HEAD)
            q = q_ref[:, sl]
            k = k_ref[:, sl]
            qn_ref[:, sl] = q * _rms(q) * gq
            kn_ref[:, sl] = k * _rms(k) * gk

    seg = lambda j: pl.BlockSpec((tm, w), lambda i, j=j: (i, j))
    vec = pl.BlockSpec((1, HEAD), lambda i: (0, 0))
    out = pl.BlockSpec((tm, w), lambda i: (i, 0))
    in_specs = [seg(3), seg(4), vec, vec]
    operands = [proj, proj, g_q, g_k]
    if dep is not None:
        body = _drop_arg(body, len(operands))
        in_specs.append(DEP_SPEC)
        operands.append(dep)
    return pl.pallas_call(
        body, name="qkv_prep", grid=(s // tm,),
        out_shape=[jax.ShapeDtypeStruct((s, w), F32)] * 2,
        in_specs=in_specs, out_specs=[out, out],
        compiler_params=_params(("arbitrary",), 40),
    )(*operands)


class _BlockView:
    def __init__(self, s, dil):
        assert s % (HEAD * dil) == 0 and dil in BLOCK_ORDER
        self.nb = s // (HEAD * dil)
        if dil == 1:
            self.lead, self.block = (s,), (HEAD,)
            self.index = lambda r, n: (n,)
        elif dil == 4:
            self.lead, self.block = (s // 512, 4, 4, 4, SUB), (None, 4, 4, None, SUB)
            self.index = lambda r, n: (n, 0, 0, r, 0)
        else:
            self.lead, self.block = (s // 2048, 16, 16, SUB), (None, 16, None, SUB)
            self.index = lambda r, n: (n, 0, r, 0)

    def view(self, t):
        return t.reshape(self.lead + (t.shape[-1],))

    def spec(self, width, block_of, column=0):
        return pl.BlockSpec(self.block + (width,), lambda r, n: self.index(r, block_of(r, n)) + (column,))


def _rows(ref, lanes=slice(None)):
    v = ref[(slice(None),) * (len(ref.shape) - 1) + (lanes,)]
    return v.reshape(HEAD, v.shape[-1])


def _set_rows(ref, lanes, value):
    ref[(slice(None),) * (len(ref.shape) - 1) + (lanes,)] = value.reshape(ref.shape[:-1] + (value.shape[-1],))


V_SEGMENT = 5


def _attn_fwd(qn, kn, proj, bias, dil, name, dep=None):
    s, w = qn.shape
    n_heads = w // HEAD
    bv = _BlockView(s, dil)

    def body(q_ref, kc_ref, vc_ref, bias_ref, o_ref, lse_ref, s_scr, e_scr, lse_scr, inv_scr, k_prev, v_prev):
        n = pl.program_id(1)
        heads = [slice(h * HEAD, (h + 1) * HEAD) for h in range(n_heads)]
        lse_scr[...] = jnp.zeros_like(lse_scr)

        @pl.when(n == 0)
        def _():
            k_prev[...] = jnp.zeros_like(k_prev)
            v_prev[...] = jnp.zeros_like(v_prev)

        for h, sl in enumerate(heads):
            q = _rows(q_ref, sl).astype(BF16)
            s_p = _dot(q, k_prev[:, sl], NT_DIMS) + bias_ref[h, :, :HEAD]
            s_scr[h, :, :HEAD] = jnp.where(n > 0, s_p, NEG_INF)
            s_scr[h, :, HEAD:] = _dot(q, _rows(kc_ref, sl).astype(BF16), NT_DIMS) + bias_ref[h, :, HEAD:]
        for h in range(n_heads):
            sc = s_scr[h]
            m = jnp.max(sc, axis=-1, keepdims=True)
            e = jnp.exp(sc - m)
            den = jnp.sum(e, axis=-1, keepdims=True)
            e_scr[h] = e.astype(BF16)
            lse_scr[:, h:h + 1] = m + jnp.log(den)
            inv_scr[:, h:h + 1] = 1.0 / den
        for h, sl in enumerate(heads):
            v_cur = _rows(vc_ref, sl).astype(BF16)
            o = _dot(e_scr[h, :, :HEAD], v_prev[:, sl]) + _dot(e_scr[h, :, HEAD:], v_cur)
            _set_rows(o_ref, sl, o * inv_scr[:, h:h + 1])
            v_prev[:, sl] = v_cur
            k_prev[:, sl] = _rows(kc_ref, sl).astype(BF16)
        _set_rows(lse_ref, slice(None), lse_scr[...])

    cur = bv.spec(w, lambda r, n: n)
    in_specs = [cur, cur, bv.spec(w, lambda r, n: n, V_SEGMENT),
                pl.BlockSpec((n_heads, HEAD, 2 * HEAD), lambda r, n: (0, 0, 0))]
    operands = [bv.view(qn), bv.view(kn), bv.view(proj), bias]
    if dep is not None:
        body = _drop_arg(body, len(operands))
        in_specs.append(DEP_SPEC)
        operands.append(dep)
    o, lse = pl.pallas_call(
        body, name=name, grid=(dil, bv.nb),
        out_shape=[jax.ShapeDtypeStruct(bv.lead + (w,), F32), jax.ShapeDtypeStruct(bv.lead + (HEAD,), F32)],
        in_specs=in_specs,
        out_specs=[cur, bv.spec(HEAD, lambda r, n: n)],
        scratch_shapes=[pltpu.VMEM((n_heads, HEAD, 2 * HEAD), F32), pltpu.VMEM((n_heads, HEAD, 2 * HEAD), BF16),
                        pltpu.VMEM((HEAD, HEAD), F32), pltpu.VMEM((HEAD, HEAD), F32),
                        pltpu.VMEM((HEAD, w), BF16), pltpu.VMEM((HEAD, w), BF16)],
        compiler_params=_params(("arbitrary", "arbitrary"), 48),
    )(*operands)
    return o.reshape(s, w), lse.reshape(s, HEAD)


def _attn_bwd(qn, kn, proj, dyb, lse, delta, bias, dil, name, running=None, dep=None):
    s, w = qn.shape
    n_heads = w // HEAD
    bv = _BlockView(s, dil)
    nb = bv.nb

    n_run = 0 if running is None else 3

    def body(q_ref, kc_ref, kp_ref, vc_ref, vp_ref, dy_ref, lse_ref, dl_ref, bias_ref, *rest):
        so_far = rest[:n_run]
        dq_ref, dk_ref, dv_ref, ds_ref, carry_k, carry_v, s_scr, dp_scr, p_scr, dsb_scr, k_cur, v_cur = rest[n_run:]
        base = (lambda i, sl: _rows(so_far[i], sl)) if n_run else (lambda i, sl: 0.0)
        r = pl.program_id(0)
        step = pl.program_id(1)
        blk = nb - 1 - step

        @pl.when((r == 0) & (step == 0))
        def _():
            ds_ref[...] = jnp.zeros_like(ds_ref)

        @pl.when(step == 0)
        def _():
            carry_k[...] = jnp.zeros_like(carry_k)
            carry_v[...] = jnp.zeros_like(carry_v)
            k_cur[...] = _rows(kc_ref).astype(BF16)
            v_cur[...] = _rows(vc_ref).astype(BF16)

        heads = [slice(h * HEAD, (h + 1) * HEAD) for h in range(n_heads)]
        tots = _rows(lse_ref)
        dls = _rows(dl_ref)
        for h, sl in enumerate(heads):
            q, dy = _rows(q_ref, sl).astype(BF16), _rows(dy_ref, sl).astype(BF16)
            kp, kc = _rows(kp_ref, sl).astype(BF16), k_cur[:, sl]
            vp, vc = _rows(vp_ref, sl).astype(BF16), v_cur[:, sl]
            s_p = _dot(q, kp, NT_DIMS) + bias_ref[h, :, :HEAD]
            s_scr[h, :, :HEAD] = jnp.where(blk > 0, s_p, NEG_INF)
            s_scr[h, :, HEAD:] = _dot(q, kc, NT_DIMS) + bias_ref[h, :, HEAD:]
            dp_scr[h, :, :HEAD] = _dot(dy, vp, NT_DIMS)
            dp_scr[h, :, HEAD:] = _dot(dy, vc, NT_DIMS)
        for h in range(n_heads):
            prob = jnp.exp(s_scr[h] - tots[:, h:h + 1])
            ds = prob * (dp_scr[h] - dls[:, h:h + 1])
            ds_ref[h] += ds
            p_scr[h] = prob.astype(BF16)
            dsb_scr[h] = ds.astype(BF16)
        for h, sl in enumerate(heads):
            q, dy = _rows(q_ref, sl).astype(BF16), _rows(dy_ref, sl).astype(BF16)
            kp, kc = _rows(kp_ref, sl).astype(BF16), k_cur[:, sl]
            ds_pb, ds_cb = dsb_scr[h, :, :HEAD], dsb_scr[h, :, HEAD:]
            _set_rows(dq_ref, sl, _dot(ds_pb, kp) + _dot(ds_cb, kc) + base(0, sl))
            _set_rows(dk_ref, sl, _dot(ds_cb, q, TN_DIMS) + carry_k[:, sl] + base(1, sl))
            carry_k[:, sl] = _dot(ds_pb, q, TN_DIMS)
            _set_rows(dv_ref, sl, _dot(p_scr[h, :, HEAD:], dy, TN_DIMS) + carry_v[:, sl] + base(2, sl))
            carry_v[:, sl] = _dot(p_scr[h, :, :HEAD], dy, TN_DIMS)
            k_cur[:, sl] = kp
            v_cur[:, sl] = _rows(vp_ref, sl).astype(BF16)

    cur = bv.spec(w, lambda r, n: nb - 1 - n)
    last = bv.spec(w, lambda r, n: nb - 1)
    prev = bv.spec(w, lambda r, n: jnp.maximum(nb - 2 - n, 0))
    stat = bv.spec(HEAD, lambda r, n: nb - 1 - n)
    whole = pl.BlockSpec((n_heads, HEAD, 2 * HEAD), lambda r, n: (0, 0, 0))
    big = jax.ShapeDtypeStruct(bv.lead + (w,), F32)
    v_last = bv.spec(w, lambda r, n: nb - 1, V_SEGMENT)
    v_prev = bv.spec(w, lambda r, n: jnp.maximum(nb - 2 - n, 0), V_SEGMENT)
    in_specs = [cur, last, prev, v_last, v_prev, cur, stat, stat, whole]
    operands = [bv.view(qn), bv.view(kn), bv.view(kn), bv.view(proj), bv.view(proj), bv.view(dyb), bv.view(lse),
                bv.view(delta), bias]
    aliases = {}
    if running is not None:
        aliases = {len(operands) + i: i for i in range(3)}
        in_specs += [cur] * 3
        operands += [bv.view(t) for t in running]
    if dep is not None:
        body = _drop_arg(body, len(operands))
        in_specs.append(DEP_SPEC)
        operands.append(dep)
    dq, dk, dv, ds = pl.pallas_call(
        body, name=name, grid=(dil, nb),
        out_shape=[big, big, big, jax.ShapeDtypeStruct((n_heads, HEAD, 2 * HEAD), F32)],
        in_specs=in_specs, out_specs=[cur, cur, cur, whole], input_output_aliases=aliases,
        scratch_shapes=[pltpu.VMEM((HEAD, w), F32), pltpu.VMEM((HEAD, w), F32),
                        pltpu.VMEM((n_heads, HEAD, 2 * HEAD), F32), pltpu.VMEM((n_heads, HEAD, 2 * HEAD), F32),
                        pltpu.VMEM((n_heads, HEAD, 2 * HEAD), BF16), pltpu.VMEM((n_heads, HEAD, 2 * HEAD), BF16),
                        pltpu.VMEM((HEAD, w), BF16), pltpu.VMEM((HEAD, w), BF16)],
        compiler_params=_params(("arbitrary", "arbitrary"), 56),
    )(*operands)
    return dq.reshape(s, w), dk.reshape(s, w), dv.reshape(s, w), ds


def _qkv_bwd(dproj, proj, dq, dk, dv, g_q, g_k, w):
    s = proj.shape[0]
    n_heads = w // HEAD
    tm = HEAD

    def body(dproj_hbm, q_ref, k_ref, gq_ref, gk_ref, dq_ref, dk_ref, dv_ref, out_ref, dgq_ref, dgk_ref):
        i = pl.program_id(0)
        gq = gq_ref[...] * (HEAD ** -0.5)
        gk = gk_ref[...]
        acc_q = jnp.zeros((1, HEAD), F32)
        acc_k = jnp.zeros((1, HEAD), F32)
        for h in range(n_heads):
            sl = slice(h * HEAD, (h + 1) * HEAD)
            q, k = q_ref[:, sl], k_ref[:, sl]
            dqn, dkn = dq_ref[:, sl], dk_ref[:, sl]
            rq, rk = _rms(q), _rms(k)
            out_ref[:, h * HEAD:(h + 1) * HEAD] = _rms_bwd(dqn, q, rq, gq).astype(BF16)
            out_ref[:, w + h * HEAD:w + (h + 1) * HEAD] = _rms_bwd(dkn, k, rk, gk).astype(BF16)
            acc_q += jnp.sum(dqn * q * rq, axis=0, keepdims=True)
            acc_k += jnp.sum(dkn * k * rk, axis=0, keepdims=True)
        out_ref[:, 2 * w:] = dv_ref[...].astype(BF16)

        @pl.when(i == 0)
        def _():
            dgq_ref[...] = jnp.zeros_like(dgq_ref)
            dgk_ref[...] = jnp.zeros_like(dgk_ref)

        dgq_ref[...] += acc_q * (HEAD ** -0.5)
        dgk_ref[...] += acc_k

    seg = lambda j: pl.BlockSpec((tm, w), lambda i, j=j: (i, j))
    vec = pl.BlockSpec((1, HEAD), lambda i: (0, 0))
    row = pl.BlockSpec((tm, w), lambda i: (i, 0))
    return pl.pallas_call(
        body, name="qkv_bwd", grid=(s // tm,),
        out_shape=[jax.ShapeDtypeStruct(dproj.shape, BF16),
                   jax.ShapeDtypeStruct((1, HEAD), F32), jax.ShapeDtypeStruct((1, HEAD), F32)],
        in_specs=[pl.BlockSpec(memory_space=pl.ANY), seg(3), seg(4), vec, vec] + [row] * 3,
        out_specs=[pl.BlockSpec((tm, 3 * w), lambda i: (i, 1)), vec, vec],
        input_output_aliases={0: 0},
        compiler_params=_params(("arbitrary",), 48),
    )(dproj, proj, proj, g_q, g_k, dq, dk, dv)


def _mixer_a(u, gv, ws_ref, bst_ref, lng, lnb, z_scr, ln_scr):
    n_groups = u.shape[1] // HEAD
    mu = jnp.mean(gv, axis=-1, keepdims=True)
    xc = gv - mu
    rs = lax.rsqrt(jnp.mean(xc * xc, axis=-1, keepdims=True) + EPS)
    xhat = xc * rs
    ln_scr[...] = (xhat * lng + lnb).astype(BF16)
    causal = _causal_mask()
    for g in range(n_groups):
        sl = slice(g * HEAD, (g + 1) * HEAD)
        wm = jnp.where(causal, ws_ref[g], 0.0).astype(BF16)
        z_scr[:, sl] = _dot(wm, ln_scr[:, sl]) + bst_ref[:, g:g + 1]
    return u, xhat, rs


def _causal_mask():
    token = lambda r: 16 * (r % SUB) + r // SUB
    row = lax.broadcasted_iota(jnp.int32, (HEAD, HEAD), 0)
    col = lax.broadcasted_iota(jnp.int32, (HEAD, HEAD), 1)
    return token(col) <= token(row)


def _merge_b(o_refs, lse_refs, yb_scr):
    n_heads = yb_scr.shape[1] // HEAD
    lses = [t[...] for t in lse_refs]
    m = jnp.maximum(jnp.maximum(lses[0], lses[1]), lses[2])
    tot = m + jnp.log(sum(jnp.exp(t - m) for t in lses))
    alphas = [jnp.exp(t - tot) for t in lses]
    for h in range(n_heads):
        sl = slice(h * HEAD, (h + 1) * HEAD)
        yb_scr[:, sl] = sum(a[:, h:h + 1] * o[:, sl].astype(F32) for a, o in zip(alphas, o_refs))
    return tot


def _mix_fwd(proj, outs, lses, w_s, bst, ln_g, ln_b, g_a, g_b, w, dep=None):
    s = proj.shape[0]
    n_groups = w // HEAD

    def body(au_ref, av_ref, az_ref, bz_ref, o1, o2, o3, l1, l2, l3, ws_ref, bst_ref,
             lng_ref, lnb_ref, ga_ref, gb_ref, p_ref, y_ref, z_scr, ln_scr, yb_scr):
        u, _, _ = _mixer_a(_gelu(au_ref[...]), _gelu(av_ref[...]), ws_ref, bst_ref, lng_ref[...], lnb_ref[...],
                           z_scr, ln_scr)
        ya = u * z_scr[...]
        silu_a, _ = _silu_and_grad(az_ref[...])
        perm = p_ref[...]
        y_ref[:, :w] = _dot(perm, (ya * _rms(ya) * ga_ref[...] * silu_a).astype(BF16), TN_DIMS).astype(BF16)
        _merge_b((o1, o2, o3), (l1, l2, l3), yb_scr)
        yb = yb_scr[...]
        silu_b, _ = _silu_and_grad(bz_ref[...])
        y_ref[:, w:] = _dot(perm, (yb * _rms(yb) * gb_ref[...] * silu_b).astype(BF16), TN_DIMS).astype(BF16)

    seg = lambda j: pl.BlockSpec((HEAD, w), lambda i, j=j: (i, j))
    row = pl.BlockSpec((HEAD, w), lambda i: (i, 0))
    stat = pl.BlockSpec((HEAD, HEAD), lambda i: (i, 0))
    vec = pl.BlockSpec((1, w), lambda i: (0, 0))
    in_specs = [seg(0), seg(1), seg(2), seg(6), row, row, row, stat, stat, stat,
                pl.BlockSpec((n_groups, HEAD, HEAD), lambda i: (0, 0, 0)),
                pl.BlockSpec((HEAD, n_groups), lambda i: (0, 0)), vec, vec, vec, vec,
                pl.BlockSpec((HEAD, HEAD), lambda i: (0, 0))]
    operands = [proj, proj, proj, proj, *outs, *lses, w_s, bst, ln_g, ln_b, g_a, g_b, _chunk_perm()]
    if dep is not None:
        body = _drop_arg(body, len(operands))
        in_specs.append(DEP_SPEC)
        operands.append(dep)
    return pl.pallas_call(
        body, name="mix_fwd", grid=(s // HEAD,),
        out_shape=jax.ShapeDtypeStruct((s, 2 * w), BF16), in_specs=in_specs,
        out_specs=pl.BlockSpec((HEAD, 2 * w), lambda i: (i, 0)),
        scratch_shapes=[pltpu.VMEM((HEAD, w), F32), pltpu.VMEM((HEAD, w), BF16), pltpu.VMEM((HEAD, w), F32)],
        compiler_params=_params(("arbitrary",), 48),
    )(*operands)


def _mix_bwd(proj, dy, outs, lses, w_s, bst, ln_g, ln_b, g_a, g_b, w):
    s = proj.shape[0]
    n_groups = w // HEAD

    def body(au_ref, av_ref, az_ref, bz_ref, dy_ref, o1, o2, o3, l1, l2, l3, ws_ref, bst_ref,
             lng_ref, lnb_ref, ga_ref, gb_ref,
             dproj_ref, dyb_ref, tot_ref, dl_ref, dws_ref, dbst_ref, dlng_ref, dlnb_ref, dga_ref, dgb_ref,
             z_scr, ln_scr, yb_scr, dz_scr, dln_scr):
        i = pl.program_id(0)

        @pl.when(i == 0)
        def _():
            for t in (dws_ref, dbst_ref, dlng_ref, dlnb_ref, dga_ref, dgb_ref):
                t[...] = jnp.zeros_like(t)

        az = az_ref[...]
        lng = lng_ref[...]
        u, du_dau = _gelu_and_grad(au_ref[...])
        gv, dgv_dav = _gelu_and_grad(av_ref[...])
        u, xhat, rs = _mixer_a(u, gv, ws_ref, bst_ref, lng, lnb_ref[...], z_scr, ln_scr)
        z = z_scr[...]
        ya = u * z
        ra = _rms(ya)
        silu_a, dsilu_a = _silu_and_grad(az)
        dya_all = dy_ref[:, :w].astype(F32)
        na = ya * ra * ga_ref[...]
        dna = dya_all * silu_a
        dproj_ref[:, 2 * w:3 * w] = (dya_all * na * dsilu_a).astype(BF16)
        dga_ref[...] += jnp.sum(dna * ya * ra, axis=0, keepdims=True)
        dya = _rms_bwd(dna, ya, ra, ga_ref[...])
        dproj_ref[:, :w] = (dya * z * du_dau).astype(BF16)
        dz_scr[...] = (dya * u).astype(BF16)

        causal = _causal_mask()
        for g in range(n_groups):
            sl = slice(g * HEAD, (g + 1) * HEAD)
            wm = jnp.where(causal, ws_ref[g], 0.0).astype(BF16)
            dz = dz_scr[:, sl]
            dln_scr[:, sl] = _dot(wm, dz, TN_DIMS)
            dws_ref[g] += jnp.where(causal, _dot(dz, ln_scr[:, sl], NT_DIMS), 0.0)
            dbst_ref[:, g:g + 1] += jnp.sum(dz.astype(F32), axis=-1, keepdims=True)
        dln = dln_scr[...]
        dlng_ref[...] += jnp.sum(dln * xhat, axis=0, keepdims=True)
        dlnb_ref[...] += jnp.sum(dln, axis=0, keepdims=True)
        gy = dln * lng
        dgv = rs * (gy - jnp.mean(gy, axis=-1, keepdims=True)
                    - xhat * jnp.mean(gy * xhat, axis=-1, keepdims=True))
        dproj_ref[:, w:2 * w] = (dgv * dgv_dav).astype(BF16)
        dproj_ref[:, 3 * w:6 * w] = jnp.zeros((HEAD, 3 * w), BF16)

        tot_ref[...] = _merge_b((o1, o2, o3), (l1, l2, l3), yb_scr)
        yb = yb_scr[...]
        rb = _rms(yb)
        bz = bz_ref[...]
        silu_b, dsilu_b = _silu_and_grad(bz)
        dyb_all = dy_ref[:, w:].astype(F32)
        dnb = dyb_all * silu_b
        dproj_ref[:, 6 * w:] = (dyb_all * yb * rb * gb_ref[...] * dsilu_b).astype(BF16)
        dgb_ref[...] += jnp.sum(dnb * yb * rb, axis=0, keepdims=True)
        dyb = _rms_bwd(dnb, yb, rb, gb_ref[...])
        dyb_ref[...] = dyb
        prod = dyb * yb
        dl_ref[...] = _lane_pick(
            [jnp.sum(prod[:, h * HEAD:(h + 1) * HEAD], axis=-1, keepdims=True) for h in range(n_groups)], HEAD)

    seg = lambda j: pl.BlockSpec((HEAD, w), lambda i, j=j: (i, j))
    row_w = pl.BlockSpec((HEAD, w), lambda i: (i, 0))
    stat = pl.BlockSpec((HEAD, HEAD), lambda i: (i, 0))
    vec = pl.BlockSpec((1, w), lambda i: (0, 0))
    ws_spec = pl.BlockSpec((n_groups, HEAD, HEAD), lambda i: (0, 0, 0))
    bst_spec = pl.BlockSpec((HEAD, n_groups), lambda i: (0, 0))
    vec_shape = jax.ShapeDtypeStruct((1, w), F32)
    return pl.pallas_call(
        body, name="mix_bwd", grid=(s // HEAD,),
        out_shape=[jax.ShapeDtypeStruct((s, N_SEG * w), BF16), jax.ShapeDtypeStruct((s, w), F32),
                   jax.ShapeDtypeStruct((s, HEAD), F32), jax.ShapeDtypeStruct((s, HEAD), F32),
                   jax.ShapeDtypeStruct((n_groups, HEAD, HEAD), F32), jax.ShapeDtypeStruct((HEAD, n_groups), F32),
                   vec_shape, vec_shape, vec_shape, vec_shape],
        in_specs=[seg(0), seg(1), seg(2), seg(6), pl.BlockSpec((HEAD, 2 * w), lambda i: (i, 0)),
                  row_w, row_w, row_w, stat, stat, stat, ws_spec, bst_spec, vec, vec, vec, vec],
        out_specs=[pl.BlockSpec((HEAD, N_SEG * w), lambda i: (i, 0)), row_w, stat, stat,
                   ws_spec, bst_spec, vec, vec, vec, vec],
        scratch_shapes=[pltpu.VMEM((HEAD, w), F32), pltpu.VMEM((HEAD, w), BF16), pltpu.VMEM((HEAD, w), F32),
                        pltpu.VMEM((HEAD, w), BF16), pltpu.VMEM((HEAD, w), F32)],
        compiler_params=_params(("arbitrary",), 56),
    )(proj, proj, proj, proj, dy, *outs, *lses, w_s, bst, ln_g, ln_b, g_a, g_b)


def _ple_gate(hn2, wgate, h, p, wup_g, tgt):
    s, d = h.shape
    n, p_dim, c_up = wup_g.shape
    tm, pair = _tile(s, 512), 2
    tn = pair * c_up
    cols = n // pair
    tiles = (s // tm) * cols
    cur = lambda t: jnp.minimum(t, tiles - 1)
    prv = lambda t: jnp.maximum(t - 1, 0)

    def body(a_ref, b_ref, h_ref, p_ref, wup_ref, tgt_ref, dout_ref, dpre_ref, dup_ref, loss_ref, acc):
        @pl.when(pl.program_id(0) == 0)
        def _():
            acc[...] = jnp.zeros_like(acc)

        gate = jax.nn.sigmoid(acc[...])
        pb = p_ref[...].astype(BF16)
        up = jnp.concatenate([_dot(pb, wup_ref[g]) for g in range(pair)], axis=1)
        err = h_ref[...] + gate * up - tgt_ref[...]
        dout = err * (1.0 / d)
        dout_ref[...] = dout.astype(BF16)
        dpre_ref[...] = (dout * up * gate * (1.0 - gate)).astype(BF16)
        dup_ref[...] = (dout * gate).astype(BF16)
        part = 0.5 * jnp.sum(err * err) * (1.0 / d)
        rr = lax.broadcasted_iota(jnp.int32, (SUB, HEAD), 0)
        cc = lax.broadcasted_iota(jnp.int32, (SUB, HEAD), 1)
        loss_ref[...] = jnp.where((rr == 0) & (cc == 0), part, 0.0)
        acc[...] = _dot(a_ref[...], b_ref[...])

    tail = pl.BlockSpec((tm, tn), lambda t: (prv(t) // cols, prv(t) % cols))
    big = jax.ShapeDtypeStruct((s, d), BF16)
    return pl.pallas_call(
        body, name="ple_gate", grid=(tiles + 1,),
        out_shape=[big, big, big, jax.ShapeDtypeStruct((s // tm * SUB, cols * HEAD), F32)],
        in_specs=[pl.BlockSpec((tm, d), lambda t: (cur(t) // cols, 0)),
                  pl.BlockSpec((d, tn), lambda t: (0, cur(t) % cols)),
                  tail, pl.BlockSpec((tm, p_dim), lambda t: (prv(t) // cols, 0)),
                  pl.BlockSpec((pair, p_dim, c_up), lambda t: (prv(t) % cols, 0, 0)), tail],
        out_specs=[tail, tail, tail, pl.BlockSpec((SUB, HEAD), lambda t: (prv(t) // cols, prv(t) % cols))],
        scratch_shapes=[pltpu.VMEM((tm, tn), F32)],
        compiler_params=_params(("arbitrary",), 60),
    )(hn2, wgate, h, p, wup_g, tgt)


def _local_step(x, p, tgt, small, wg, ex, while_last_travels=None):
    s, d = x.shape
    n, _, c_in = wg.buffers["w_in"].shape
    assert n == N_DEV
    d_in = n * c_in
    w = d_in // N_SEG
    n_heads = w // HEAD
    p_dim, c_up = wg.buffers["w_ple_up"].shape[1:]
    assert s % (HEAD * DILATIONS[-1]) == 0 and w % HEAD == 0 and d == n * c_up == 2 * w

    near = (2, 4)
    wg.start("gather_in_pair", ["w_in"], (1,))
    token = wg.start("gather_in_first", ["w_in"], ("first",))
    rest = ["w_out", "w_ple_gate", "w_ple_up"]

    hn = _rmsnorm_fwd(x, small["g_pre"], "pre_norm", True, dep=token)
    tm = _tile(s, 1024)

    def in_proj(shards, name, carry, dep=None):
        return _matmul(
            hn, wg.buffers["w_in"], name=name, grid=(s // tm, len(shards), 1), dims=NN_DIMS,
            prefetch=jnp.stack(shards).astype(jnp.int32),
            a_spec=pl.BlockSpec((tm, d), lambda i, j, k, sh: (i, 0)),
            b_spec=pl.BlockSpec((None, d, c_in), lambda i, j, k, sh: (sh[j], 0, 0), pipeline_mode=pl.Buffered(1)),
            acc_shape=(tm, c_in), out_shapes=[jax.ShapeDtypeStruct((s, d_in), F32)],
            out_specs=[pl.BlockSpec((tm, c_in), lambda i, j, k, sh: (i, sh[j]))], carry=carry, dep=dep,
            vmem_mib=56)[0]

    me = wg.me
    core = me & 1
    first, second, far = me ^ (4 - 2 * core), me ^ (2 + 2 * core), me ^ 6
    proj = in_proj([me], "in_proj_own", None)
    token = wg.start("gather_in_second", ["w_in"], ("second",), after=[proj])
    wg.arrived("gather_in_pair", [token])
    proj = in_proj([me ^ 1], "in_proj_sibling", proj)
    buckets = jnp.stack([_rel_buckets(dil) for dil in DILATIONS])
    bias = _bias_build(small["rel_bias"], buckets, n_heads)
    ahead = [bias] + [wg.buffers[k] for k in rest]
    for tag, mine, from_sibling in (("first", first, second ^ 1), ("second", second, first ^ 1), ("far", far, far ^ 1)):
        token = wg.forward("gather_in_" + tag, [proj] + ahead)
        ahead = []
        if tag == "second":
            token = wg.relay_start("gather_in_far", ["w_in"], after=[token])
        elif tag == "far":
            token = wg.start("gather_rest", rest, (1,) + near, after=[token])
        proj = in_proj([mine], "in_proj_" + tag, proj, dep=token)
        wg.forwarded("gather_in_" + tag, [proj])
        proj = in_proj([from_sibling], "in_proj_%s_forwarded" % tag, proj)
    win_g = wg.buffers["w_in"]

    qn, kn = _qkv_prep(proj, small["g_q"], small["g_k"], w)
    token = wg.forward("gather_rest", [qn])
    token = wg.relay_start("gather_rest_far", rest, after=[token])
    outs, lses = [], []
    for c, dil in enumerate(DILATIONS):
        o, l = _attn_fwd(qn, kn, proj, bias[c], dil, "attn_fwd_d%d" % dil, dep=token)
        outs.append(o)
        lses.append(l)
    token = wg.forward("gather_rest_far", outs)

    ws_p = small["w_s"][:, CHUNK_ORDER][:, :, CHUNK_ORDER]
    bst = small["b_s"].T[CHUNK_ORDER]
    mix_args = (outs, lses, ws_p, bst, small["ln_v_g"], small["ln_v_b"], small["g_out_a"], small["g_out_b"], w)
    y = _mix_fwd(proj, *mix_args, dep=token)
    wg.forwarded("gather_rest", [y])
    wg.forwarded("gather_rest_far", [y])
    wout_g, wgate_g, wup_g = (wg.buffers[k] for k in rest)
    wout_f = wout_g.reshape(2 * w, d)
    wgate_f = wgate_g.reshape(d, d)

    tn = _tile(d, 1024)
    tk2 = 2 * w

    def resid_epilogue(acc, ex, outs_):
        outs_[0][...] = ex[0][...] + acc[...]

    h = _matmul(
        y, wout_f, name="out_proj", grid=(s // tm, d // tn, (2 * w) // tk2), dims=NN_DIMS,
        a_spec=pl.BlockSpec((tm, tk2), lambda i, j, k: (i, k)),
        b_spec=pl.BlockSpec((tk2, tn), lambda i, j, k: (k, j)),
        acc_shape=(tm, tn), out_shapes=[jax.ShapeDtypeStruct((s, d), F32)],
        out_specs=[pl.BlockSpec((tm, tn), lambda i, j, k: (i, j))],
        extra=(x,), extra_specs=(pl.BlockSpec((tm, tn), lambda i, j, k: (i, j)),),
        epilogue=resid_epilogue, vmem_mib=56)[0]

    hn2 = _rmsnorm_fwd(h, small["g_ple"], "ple_norm", False)

    dout, dpre, dup, loss_parts = _ple_gate(hn2, wgate_f, h, p, wup_g, tgt)
    loss = jnp.sum(loss_parts)

    tks = _tile(s, 2048)
    g_wup = _matmul(
        p, dup, name="grad_w_up", grid=(1, n, s // tks), dims=TN_DIMS,
        a_spec=pl.BlockSpec((tks, p_dim), lambda i, j, k: (k, 0)),
        b_spec=pl.BlockSpec((tks, c_up), lambda i, j, k: (k, j)),
        acc_shape=(p_dim, c_up), out_shapes=[jax.ShapeDtypeStruct((n, p_dim, c_up), BF16)],
        out_specs=[pl.BlockSpec((None, p_dim, c_up), lambda i, j, k: (j, 0, 0))])[0]

    def tn_matmul(a, b, name):
        m_, n_ = a.shape[1], b.shape[1]
        bm, bn = _tile(m_, 1024), _tile(n_, 1024)
        return _matmul(
            a, b, name=name, grid=(m_ // bm, n_ // bn, 1), dims=TN_DIMS,
            a_spec=pl.BlockSpec((s, bm), lambda i, j, k: (0, i)),
            b_spec=pl.BlockSpec((s, bn), lambda i, j, k: (0, j)),
            acc_shape=(bm, bn), out_shapes=[jax.ShapeDtypeStruct((m_, n_), BF16)],
            out_specs=[pl.BlockSpec((bm, bn), lambda i, j, k: (i, j))], vmem_mib=56)[0]

    def nt_matmul(a, b, name, out_dtype, dep=None):
        k_, n_ = a.shape[1], b.shape[0]
        bm, bn, bk = _tile(s, 1024), _tile(n_, 1024), k_
        return _matmul(
            a, b, name=name, grid=(s // bm, n_ // bn, k_ // bk), dims=NT_DIMS,
            a_spec=pl.BlockSpec((bm, bk), lambda i, j, k: (i, k)),
            b_spec=pl.BlockSpec((bn, bk), lambda i, j, k: (j, k)),
            acc_shape=(bm, bn), out_shapes=[jax.ShapeDtypeStruct((s, n_), out_dtype)],
            out_specs=[pl.BlockSpec((bm, bn), lambda i, j, k: (i, j))], dep=dep, vmem_mib=56)[0]

    by_core = lambda g: g.reshape((N_CHIP, 2) + g.shape[-2:])
    g_wgate = tn_matmul(hn2, dpre, "grad_w_gate").reshape(wgate_g.shape)
    dhn2 = nt_matmul(dpre, wgate_f, "ple_gate_bwd", BF16)
    dh_b, dh_bp, dg_ple = _rmsnorm_bwd(dhn2, h, small["g_ple"], dout, "ple_norm_bwd", False, True)
    g_wout = tn_matmul(y, dh_b, "grad_w_out").reshape(wout_g.shape)

    late = ("w_out", "w_ple_gate", "w_ple_up")
    late_parts = (g_wout, g_wgate, g_wup)
    token = ex.push_pairs("pair_late", [by_core(g) for g in late_parts])
    dy = nt_matmul(dh_bp, wout_f, "out_proj_bwd", BF16, dep=token)
    (dproj, dyb, lse_tot, delta, dws, dbst, dlng, dlnb, dga, dgb) = _mix_bwd(proj, dy, *mix_args)
    both_columns, from_sibling = ex.pairs_done("pair_late", [dproj])
    pair_sums = [_pair_add(mine.reshape((N_DEV,) + mine.shape[-2:]), theirs, "pair_add_" + k, ex.core)
                 for k, mine, theirs in zip(late, both_columns, from_sibling)]
    token = ex.push_chips("chip_late", pair_sums)

    running, dss = None, []
    for c, dil in enumerate(DILATIONS):
        *running, ds = _attn_bwd(qn, kn, proj, dyb, lse_tot, delta, bias[c], dil, "attn_bwd_d%d" % dil,
                                 running=running, dep=token if c == 0 else None)
        dss.append(ds)
    d_rel = _bias_grad(jnp.stack(dss), buckets, n_heads)
    dproj, dgq, dgk = _qkv_bwd(dproj, proj, *running, small["g_q"], small["g_k"], w)
    pair_sums, landed, _ = ex.chips_done("chip_late", [dproj])
    delivered = {k: (mine, theirs) for k, mine, theirs in zip(late, pair_sums, landed)}

    token_row = np.argsort(CHUNK_ORDER)
    dws = dws[:, token_row][:, :, token_row]
    dbst = dbst[token_row]
    small_grads = {
        "w_s": dws, "b_s": dbst.T, "ln_v_g": dlng, "ln_v_b": dlnb, "g_q": dgq, "g_k": dgk,
        "rel_bias": d_rel, "g_out_a": dga, "g_out_b": dgb, "g_ple": dg_ple,
    }

    bm = _tile(d, 1024)

    def grad_w_in(core, name, dep=None):
        return _matmul(
            hn, dproj, name=name, grid=(d // bm, N_CHIP, 1), dims=TN_DIMS, prefetch=core.reshape(1),
            a_spec=pl.BlockSpec((s, bm), lambda i, j, k, core_ref: (0, i)),
            b_spec=pl.BlockSpec((s, c_in), lambda i, j, k, core_ref: (0, 2 * j + core_ref[0])),
            acc_shape=(bm, c_in), out_shapes=[jax.ShapeDtypeStruct((N_CHIP, d, c_in), BF16)],
            out_specs=[pl.BlockSpec((None, bm, c_in), lambda i, j, k, core_ref: (j, i, 0))], dep=dep,
            vmem_mib=60)[0]

    for_sibling = grad_w_in(1 - ex.core, "grad_w_in_sibling")
    token = ex.push_pairs("pair_in", [for_sibling])
    mine = grad_w_in(ex.core, "grad_w_in_mine", dep=token)
    _, from_sibling = ex.pairs_done("pair_in", [mine])
    pair_sum = _pair_add(mine, from_sibling[0], "pair_add_w_in")
    token = ex.push_chips("chip_in", [pair_sum], _pack_small(small_grads, SMALL_EARLY))

    dhn = _matmul(
        dproj, win_g, name="in_proj_bwd", grid=(s // tm, d // tn, n // 2), dims=NT_DIMS,
        a_spec=pl.BlockSpec((tm, 2 * c_in), lambda i, j, k: (i, k)),
        b_spec=pl.BlockSpec((2, tn, c_in), lambda i, j, k: (k, j, 0)),
        acc_shape=(tm, tn), out_shapes=[jax.ShapeDtypeStruct((s, d), BF16)],
        out_specs=[pl.BlockSpec((tm, tn), lambda i, j, k: (i, j))], dep=token, vmem_mib=56)[0]
    grad_x, dg_pre = _rmsnorm_bwd(dhn, x, small["g_pre"], dh_b, "pre_norm_bwd", True, False)
    extra = while_last_travels(token, delivered, dg_pre) if while_last_travels is not None else []
    pair_sums, landed, slabs = ex.chips_done("chip_in", [grad_x] + list(extra))
    delivered["w_in"] = (pair_sums[0], landed[0])
    small_grads["g_pre"] = dg_pre
    return loss, grad_x, small_grads, delivered, slabs, extra


SMALL_EARLY = ("w_s", "b_s", "ln_v_g", "ln_v_b", "g_q", "g_k", "rel_bias", "g_out_a", "g_out_b", "g_ple")
SMALL_LAST = ("g_pre",)
SMALL_NAMES = SMALL_LAST + SMALL_EARLY


def _pack_small(tree, names):
    parts = []
    for name in names:
        flat = tree[name].astype(F32).reshape(-1)
        pad = (-flat.shape[0]) % HEAD
        parts.append(jnp.pad(flat, (0, pad)) if pad else flat)
    slab = jnp.concatenate(parts).reshape(-1, HEAD)
    pad_rows = (-slab.shape[0]) % 8
    return jnp.pad(slab, ((0, pad_rows), (0, 0))) if pad_rows else slab


def _unpack_small(slab, like, names):
    flat = slab.reshape(-1)
    out, off = {}, 0
    for name in names:
        size = like[name].size
        out[name] = flat[off:off + size].reshape(like[name].shape)
        off += size + (-size) % HEAD
    return out


def _peer(k):
    x, y, c = (lax.axis_index(a) for a in AXES)
    if k == "first":
        px, py, pc = x ^ (1 - c), y ^ c, c
    elif k == "second":
        px, py, pc = x ^ c, y ^ (1 - c), c
    else:
        bits = ((k >> 2) & 1, (k >> 1) & 1, k & 1)
        px, py, pc = (1 - v if b else v for v, b in zip((x, y, c), bits))
    return (px, py, pc), 4 * px + 2 * py + pc


def _my_index():
    x, y, c = (lax.axis_index(a) for a in AXES)
    return 4 * x + 2 * y + c


N_CHIP = 4
HBM_SPEC = pl.BlockSpec(memory_space=pl.ANY)


def _remote(src, dst, send_sem, recv_sem, peer):
    return pltpu.make_async_remote_copy(src_ref=src, dst_ref=dst, send_sem=send_sem, recv_sem=recv_sem,
                                        device_id=peer, device_id_type=pl.DeviceIdType.MESH)


SEM_SPEC = pl.BlockSpec(memory_space=pltpu.SEMAPHORE)
HBM_ONLY = pl.BlockSpec(memory_space=pltpu.HBM)
DATAFLOW = pltpu.SideEffectType.DATAFLOW_SIDE_EFFECTING


def _comm_call(name, arrays, *, wait=None, start=None, after=()):
    n, n_after = len(arrays), len(after)

    def body(*refs):
        ins = refs[:n]
        pos = n
        if wait is not None:
            for cp in wait[2](ins, refs[pos], refs[pos + 1]):
                cp.wait()
            pos += 2
        outs = refs[pos + n_after:]
        if start is not None:
            for cp in start[1](ins, outs[0], outs[1]):
                cp.start()
        outs[-1][...] = jnp.zeros_like(outs[-1])

    operands = [pltpu.with_memory_space_constraint(a, pltpu.HBM) for a in arrays]
    in_specs = [HBM_ONLY] * n
    if wait is not None:
        operands += [wait[0], wait[1]]
        in_specs += [SEM_SPEC, SEM_SPEC]
    operands += list(after)
    in_specs += [HBM_SPEC] * n_after
    out_shape, out_specs = [], []
    if start is not None:
        out_shape += [pltpu.SemaphoreType.DMA((start[0],))] * 2
        out_specs += [SEM_SPEC, SEM_SPEC]
    first = len(out_shape)
    out_shape += [pltpu.HBM(a.shape, a.dtype) for a in arrays] + [jax.ShapeDtypeStruct((SUB, HEAD), F32)]
    out_specs += [HBM_ONLY] * n + [pl.BlockSpec(memory_space=pltpu.VMEM)]
    res = pl.pallas_call(
        body, name=name, out_shape=tuple(out_shape), in_specs=tuple(in_specs), out_specs=tuple(out_specs),
        input_output_aliases={i: first + i for i in range(n)},
        compiler_params=pltpu.CompilerParams(has_side_effects=DATAFLOW),
    )(*operands)
    sems = (res[0], res[1]) if start is not None else None
    return list(res[first:first + n]), sems, res[-1]


class _GradExchange:
    def __init__(self):
        x, y, c = (lax.axis_index(a) for a in AXES)
        self.core = c.astype(jnp.int32)
        self.chip = (2 * x + y).astype(jnp.int32)
        self.pending = {}

    def _pair_copies(self, n_arr):
        def make(refs, send_sems, recv_sems):
            sibling, _ = _peer(1)
            other = 1 - lax.axis_index("c")
            srcs, lands = refs[:n_arr], refs[n_arr:]
            pick = lambda ref, ch: ref.at[ch, other] if len(ref.shape) == 4 else ref.at[ch]
            return [_remote(pick(srcs[a], ch), lands[a].at[ch], send_sems.at[a * N_CHIP + ch],
                            recv_sems.at[a * N_CHIP + ch], sibling)
                    for a in range(n_arr) for ch in range(N_CHIP)]
        return make

    def _chip_copies(self, n_arr, with_slab):
        def make(refs, send_sems, recv_sems):
            x, y = lax.axis_index("x"), lax.axis_index("y")
            my_chip = 2 * x + y
            srcs, lands = refs[:n_arr], refs[n_arr:2 * n_arr]
            copies = []
            for j, k in enumerate((2, 4, 6)):
                peer, peer_idx = _peer(k)
                for a in range(n_arr):
                    copies.append(_remote(srcs[a].at[peer_idx // 2], lands[a].at[my_chip],
                                          send_sems.at[3 * a + j], recv_sems.at[3 * a + j], peer))
            if with_slab:
                slab, slab_land = refs[2 * n_arr], refs[2 * n_arr + 1]
                for k in range(1, N_DEV):
                    peer, _ = _peer(k)
                    copies.append(_remote(slab, slab_land.at[_my_index()], send_sems.at[3 * n_arr + k - 1],
                                          recv_sems.at[3 * n_arr + k - 1], peer))
            return copies
        return make

    def push_pairs(self, tag, for_sibling):
        n_arr = len(for_sibling)
        lands = [lax.empty((N_CHIP,) + a.shape[-2:], a.dtype) for a in for_sibling]
        make = self._pair_copies(n_arr)
        arrays, sems, token = _comm_call(tag + "_start", list(for_sibling) + lands, start=(n_arr * N_CHIP, make))
        self.pending[tag] = (arrays, sems, make, n_arr)
        return token

    def pairs_done(self, tag, after):
        arrays, sems, make, n_arr = self.pending.pop(tag)
        arrays, _, _ = _comm_call(tag + "_wait", arrays, wait=(sems[0], sems[1], make), after=after)
        return arrays[:n_arr], arrays[n_arr:]

    def push_chips(self, tag, pair_sums, slab=None):
        n_arr = len(pair_sums)
        arrays = list(pair_sums) + [lax.empty(a.shape, a.dtype) for a in pair_sums]
        n_copies = 3 * n_arr
        if slab is not None:
            arrays += [slab, lax.empty((N_DEV,) + slab.shape, slab.dtype)]
            n_copies += N_DEV - 1
        make = self._chip_copies(n_arr, slab is not None)
        arrays, sems, token = _comm_call(tag + "_start", arrays, start=(n_copies, make))
        self.pending[tag] = (arrays, sems, make, n_arr)
        return token

    def chips_done(self, tag, after):
        arrays, sems, make, n_arr = self.pending.pop(tag)
        arrays, _, _ = _comm_call(tag + "_wait", arrays, wait=(sems[0], sems[1], make), after=after)
        return arrays[:n_arr], arrays[n_arr:2 * n_arr], arrays[2 * n_arr:]


def _cast_place(w, name):
    r, c = w.shape
    tr = r if r * c <= MIB else 1 << ((MIB // c).bit_length() - 1)
    assert r % tr == 0

    def body(me_ref, w_ref, o_ref):
        o_ref[...] = w_ref[...].astype(BF16)

    return pl.pallas_call(
        body, name=name, out_shape=jax.ShapeDtypeStruct((N_DEV, r, c), BF16),
        grid_spec=pltpu.PrefetchScalarGridSpec(
            num_scalar_prefetch=1, grid=(r // tr,),
            in_specs=[pl.BlockSpec((tr, c), lambda i, me_ref: (i, 0))],
            out_specs=pl.BlockSpec((None, tr, c), lambda i, me_ref: (me_ref[0], i, 0))),
        compiler_params=_params(("arbitrary",), 40),
    )(_my_index().astype(jnp.int32).reshape(1), w)


class _WeightGather:
    CHIPS = (2, 4, 6)

    def __init__(self, buffers):
        self.buffers = dict(buffers)
        self.pending = {}
        self.me = _my_index().astype(jnp.int32)

    def _own_slot_to(self, peers):
        def make(refs, send_sems, recv_sems):
            me = _my_index()
            return [_remote(ref.at[me], ref.at[me], send_sems.at[len(peers) * a + j],
                            recv_sems.at[len(peers) * a + j], _peer(k)[0])
                    for a, ref in enumerate(refs) for j, k in enumerate(peers)]
        return make

    def _forward_from(self, chips):
        def make(refs, send_sems, recv_sems):
            sibling, _ = _peer(1)
            copies = []
            for a, ref in enumerate(refs):
                for j, k in enumerate(chips):
                    slot = ref.at[_peer(k)[1]]
                    copies.append(_remote(slot, slot, send_sems.at[len(chips) * a + j],
                                          recv_sems.at[len(chips) * a + j], sibling))
            return copies
        return make

    def _run(self, call, names, **kw):
        arrays, sems, token = _comm_call(call, [self.buffers[k] for k in names], **kw)
        self.buffers.update(zip(names, arrays))
        return sems, token

    def start(self, tag, names, peers, after=()):
        make = self._own_slot_to(peers)
        sems, token = self._run(tag + "_start", names, start=(len(names) * len(peers), make), after=after)
        self.pending[tag] = (names, sems, make, peers)
        return token

    @staticmethod
    def _relay(refs, send_sems, recv_sems):
        x, y, c = (lax.axis_index(a) for a in AXES)
        peer = (x ^ (1 - c), y ^ c, c)
        slot = _my_index() ^ (2 + 2 * c)
        return [_remote(ref.at[slot], ref.at[slot], send_sems.at[a], recv_sems.at[a], peer)
                for a, ref in enumerate(refs)]

    def relay_start(self, tag, names, after=()):
        sems, token = self._run(tag + "_start", names, start=(len(names), self._relay), after=after)
        self.pending[tag] = (names, sems, self._relay, (6,))
        return token

    def arrived(self, tag, after):
        names, sems, make, _ = self.pending.pop(tag)
        self._run(tag + "_wait", names, wait=(sems[0], sems[1], make), after=after)

    def forward(self, tag, after):
        names, sems, make, peers = self.pending.pop(tag)
        chips = tuple(k for k in peers if k != 1)
        onward = self._forward_from(chips)
        new_sems, token = self._run(tag + "_forward", names, wait=(sems[0], sems[1], make),
                                    start=(len(chips) * len(names), onward), after=after)
        self.pending[tag + "/fwd"] = (names, new_sems, onward)
        return token

    def forwarded(self, tag, after):
        names, sems, make = self.pending.pop(tag + "/fwd")
        self._run(tag + "_done", names, wait=(sems[0], sems[1], make), after=after)


def _pair_add(mine, theirs, name, core=None):
    _, r, c_dim = theirs.shape
    tr = r if r * c_dim <= MIB else 1 << ((MIB // c_dim).bit_length() - 1)
    assert r % tr == 0
    stride = 1 if core is None else 2
    offset = jnp.zeros((1,), jnp.int32) if core is None else core.reshape(1)

    def body(off_ref, a_ref, b_ref, o_ref):
        o_ref[...] = (a_ref[...].astype(F32) + b_ref[...].astype(F32)).astype(BF16)

    blk = (None, tr, c_dim)
    return pl.pallas_call(
        body, name=name, out_shape=jax.ShapeDtypeStruct(theirs.shape, BF16),
        grid_spec=pltpu.PrefetchScalarGridSpec(
            num_scalar_prefetch=1, grid=(N_CHIP, r // tr),
            in_specs=[pl.BlockSpec(blk, lambda ch, i, off_ref: (stride * ch + off_ref[0], i, 0)),
                      pl.BlockSpec(blk, lambda ch, i, off_ref: (ch, i, 0))],
            out_specs=pl.BlockSpec(blk, lambda ch, i, off_ref: (ch, i, 0))),
        compiler_params=_params(("arbitrary", "arbitrary"), 40),
    )(offset, mine, theirs)


def _slab_exchange(slab, name):
    def body(slab_in, slab_out, send_sems, recv_sems, local_sem):
        me = _my_index()
        local = pltpu.make_async_copy(slab_in, slab_out.at[me], local_sem)
        local.start()
        sends = []
        for k in range(1, N_DEV):
            peer, _ = _peer(k)
            sends.append(_remote(slab_in, slab_out.at[me], send_sems.at[k - 1], recv_sems.at[k - 1], peer))
        for cp in sends:
            cp.start()
        for k in range(1, N_DEV):
            peer, peer_idx = _peer(k)
            slot = slab_out.at[peer_idx]
            _remote(slot, slot, send_sems.at[k - 1], recv_sems.at[k - 1], peer).wait_recv()
        for cp in sends:
            cp.wait_send()
        local.wait()

    return pl.pallas_call(
        body, name=name, out_shape=jax.ShapeDtypeStruct((N_DEV,) + slab.shape, slab.dtype),
        in_specs=[HBM_SPEC], out_specs=HBM_SPEC,
        scratch_shapes=[pltpu.SemaphoreType.DMA((N_DEV - 1,)), pltpu.SemaphoreType.DMA((N_DEV - 1,)),
                        pltpu.SemaphoreType.DMA],
        compiler_params=pltpu.CompilerParams(has_side_effects=True),
    )(slab)


def _adamw_math(w, g, m, v):
    m = ADAM_B1 * m + (1.0 - ADAM_B1) * g
    v = ADAM_B2 * v + (1.0 - ADAM_B2) * (g * g)
    m_hat = m / (1.0 - ADAM_B1 ** ADAM_STEP)
    v_hat = v / (1.0 - ADAM_B2 ** ADAM_STEP)
    delta = -ADAM_LR * (m_hat / (jnp.sqrt(v_hat) + ADAM_EPS) + ADAM_WD * w)
    return delta, m, v


def _adamw(parts, own, place, w, m, v, name, dep=None):
    n_parts = parts.shape[0]
    r, c = w.shape
    budget = 280 * 1024
    tr = r if r * c <= budget else 1 << ((budget // c).bit_length() - 1)
    assert r % tr == 0

    def body(place_ref, p_ref, own_ref, w_ref, m_ref, v_ref, g_ref, d_ref, nm_ref, nv_ref):
        mine = own_ref[...].astype(F32)
        g = None
        for i in range(n_parts):
            term = jnp.where(place_ref[0] == i, mine, p_ref[i].astype(F32))
            g = term if g is None else g + term
        delta, nm, nv = _adamw_math(w_ref[...], g, m_ref[...], v_ref[...])
        g_ref[...] = g
        d_ref[...] = delta
        nm_ref[...] = nm
        nv_ref[...] = nv

    blk = pl.BlockSpec((tr, c), lambda i, place_ref: (i, 0))
    shape = jax.ShapeDtypeStruct((r, c), F32)
    in_specs = [pl.BlockSpec((n_parts, tr, c), lambda i, place_ref: (0, i, 0)),
                pl.BlockSpec((None, tr, c), lambda i, place_ref: (place_ref[1], i, 0)), blk, blk, blk]
    operands = [parts, own, w, m, v]
    if dep is not None:
        body = _drop_arg(body, 1 + len(operands))
        in_specs.append(pl.BlockSpec((SUB, HEAD), lambda i, place_ref: (0, 0)))
        operands.append(dep)
    return pl.pallas_call(
        body, name=name, out_shape=[shape] * 4,
        grid_spec=pltpu.PrefetchScalarGridSpec(
            num_scalar_prefetch=1, grid=(r // tr,), in_specs=in_specs, out_specs=[blk] * 4),
        compiler_params=_params(("arbitrary",), 48),
    )(place, *operands)


def kernel(x, p, g_pre, w_in, w_s, b_s, ln_v_g, ln_v_b, g_q, g_k, rel_bias, g_out_a, g_out_b, w_out, g_ple, w_ple_gate, w_ple_up, loss_target, m_g_pre, m_w_in, m_w_s, m_b_s, m_ln_v_g, m_ln_v_b, m_g_q, m_g_k, m_rel_bias, m_g_out_a, m_g_out_b, m_w_out, m_g_ple, m_w_ple_gate, m_w_ple_up, v_g_pre, v_w_in, v_w_s, v_b_s, v_ln_v_g, v_ln_v_b, v_g_q, v_g_k, v_rel_bias, v_g_out_a, v_g_out_b, v_w_out, v_g_ple, v_w_ple_gate, v_w_ple_up):
    args = dict(locals())
    small = {"g_pre": g_pre, "w_s": w_s[0], "b_s": b_s[0], "ln_v_g": ln_v_g, "ln_v_b": ln_v_b, "g_q": g_q,
             "g_k": g_k, "rel_bias": rel_bias, "g_out_a": g_out_a, "g_out_b": g_out_b, "g_ple": g_ple}
    big_names = ("w_in", "w_out", "w_ple_gate", "w_ple_up")
    big = {k: args[k][0] for k in big_names}

    wg = _WeightGather({k: _cast_place(big[k], "place_" + k) for k in big_names})
    ex = _GradExchange()
    results = {}

    def big_adamw(k, delivered, dep=None):
        mine, theirs = delivered[k]
        place = jnp.stack([ex.chip, ex.chip])
        return _adamw(theirs, mine, place, big[k], args["m_" + k][0], args["v_" + k][0], "adamw_" + k, dep=dep)

    squeeze = lambda t: {k: (t[k][0] if k in ("w_s", "b_s") else t[k]) for k in SMALL_NAMES}
    small_m = squeeze({k: args["m_" + k] for k in SMALL_NAMES})
    small_v = squeeze({k: args["v_" + k] for k in SMALL_NAMES})
    slab_place = jnp.stack([_my_index().astype(jnp.int32), jnp.zeros((), jnp.int32)])

    def small_adamw(names_, parts, own, call_name):
        packed = _adamw(parts, own[None], slab_place, _pack_small(small, names_), _pack_small(small_m, names_),
                        _pack_small(small_v, names_), call_name)
        for idx in range(4):
            tree = _unpack_small(packed[idx], small, names_)
            for k in names_:
                results.setdefault(k, [None] * 4)[idx] = tree[k].reshape(args[k].shape)
        return packed[0]

    def while_last_travels(token, delivered, dg_pre):
        done = []
        for k in big_names[1:]:
            results[k] = big_adamw(k, delivered, dep=token)
            done.append(results[k][0])
        last_slab = _pack_small({"g_pre": dg_pre}, SMALL_LAST)
        done.append(small_adamw(SMALL_LAST, _slab_exchange(last_slab, "last_exchange"), last_slab, "adamw_last"))
        return done

    loss, grad_x, small_parts, delivered, slabs, _ = _local_step(
        x[0], p[0, 0], loss_target[0], small, wg, ex, while_last_travels)
    results["w_in"] = big_adamw("w_in", delivered)
    for k in big_names:
        results[k] = [t[None] for t in results[k]]
    small_adamw(SMALL_EARLY, slabs[1], slabs[0], "adamw_small")

    names = ("g_pre", "w_in", "w_s", "b_s", "ln_v_g", "ln_v_b", "g_q", "g_k", "rel_bias", "g_out_a", "g_out_b",
             "w_out", "g_ple", "w_ple_gate", "w_ple_up")
    total = lax.psum(loss, AXES)
    out = [total, grad_x[None]]
    for idx in range(4):
        out += [results[k][idx] for k in names]
    return tuple(out)
```

```python
import math

import numpy as np
import jax
import jax.numpy as jnp
from jax import lax
from jax.experimental import pallas as pl
from jax.experimental.pallas import tpu as pltpu

F32 = jnp.float32
BF16 = jnp.bfloat16
EPS = 1e-6
NEG_INF = -1e30
HEAD = 128
DILATIONS = (1, 4, 16)
NUM_BUCKETS = 32
MAX_DISTANCE = 2048
N_SEG = 7
ADAM_LR = 0.001
ADAM_B1 = 0.9
ADAM_B2 = 0.999
ADAM_EPS = 1e-08
ADAM_WD = 0.01
ADAM_STEP = 10
AXES = ("x", "y", "c")
N_DEV = 8
MIB = 1 << 20

SUB = 8

CHUNK_ORDER = np.array([16 * (r % SUB) + r // SUB for r in range(HEAD)])
BLOCK_ORDER = {
    1: CHUNK_ORDER,
    4: np.array([32 * (r // 32) + 4 * (r % SUB) + (r // SUB) % 4 for r in range(HEAD)]),
    16: np.arange(HEAD),
}

NT_DIMS = (((1,), (1,)), ((), ()))
TN_DIMS = (((0,), (0,)), ((), ()))
NN_DIMS = (((1,), (0,)), ((), ()))


def _params(semantics, vmem_mib):
    return pltpu.CompilerParams(dimension_semantics=semantics, vmem_limit_bytes=vmem_mib * MIB)


def _gelu(a):
    return 0.5 * a * (1.0 + lax.erf(a * (2.0 ** -0.5)))


def _gelu_and_grad(a):
    cdf = 0.5 * (1.0 + lax.erf(a * (2.0 ** -0.5)))
    return a * cdf, cdf + a * jnp.exp(-0.5 * a * a) * ((2.0 * math.pi) ** -0.5)


def _silu_and_grad(a):
    s = jax.nn.sigmoid(a)
    return a * s, s * (1.0 + a * (1.0 - s))


def _rms(v):
    return lax.rsqrt(jnp.mean(v * v, axis=-1, keepdims=True) + EPS)


def _rms_bwd(dy, v, r, g):
    gy = dy * g
    return r * gy - v * (r * r * r) * jnp.mean(gy * v, axis=-1, keepdims=True)


def _dot(a, b, dims=NN_DIMS):
    return lax.dot_general(a, b, dims, preferred_element_type=F32)


def _lane_pick(cols, width):
    rows = cols[0].shape[0]
    lane = lax.broadcasted_iota(jnp.int32, (rows, width), 1)
    out = jnp.zeros((rows, width), F32)
    for h, col in enumerate(cols):
        out = jnp.where(lane == h, col, out)
    return out


def _chunk_perm():
    return jnp.asarray(np.eye(HEAD, dtype=np.float32)[CHUNK_ORDER], BF16)


def _unpermute_f32(p, v):
    hi = v.astype(BF16)
    rest = v - hi.astype(F32)
    mid = rest.astype(BF16)
    lo = (rest - mid.astype(F32)).astype(BF16)
    return _dot(p, hi, TN_DIMS) + _dot(p, mid, TN_DIMS) + _dot(p, lo, TN_DIMS)


def _rmsnorm_fwd(x, g, name, permute, dep=None):
    s, d = x.shape
    tm = HEAD

    def body(x_ref, g_ref, p_ref, o_ref):
        v = x_ref[...]
        out = (v * _rms(v) * g_ref[...]).astype(BF16)
        if permute:
            out = _dot(p_ref[...], out).astype(BF16)
        o_ref[...] = out

    in_specs = [pl.BlockSpec((tm, d), lambda i: (i, 0)), pl.BlockSpec((1, d), lambda i: (0, 0)),
                pl.BlockSpec((HEAD, HEAD), lambda i: (0, 0))]
    operands = [x, g, _chunk_perm()]
    if dep is not None:
        body = _drop_arg(body, len(operands))
        in_specs.append(DEP_SPEC)
        operands.append(dep)
    return pl.pallas_call(
        body, name=name, grid=(s // tm,),
        out_shape=jax.ShapeDtypeStruct((s, d), BF16), in_specs=in_specs,
        out_specs=pl.BlockSpec((tm, d), lambda i: (i, 0)),
        compiler_params=_params(("arbitrary",), 40),
    )(*operands)


def _rmsnorm_bwd(dy, v, g, res, name, dy_permuted, with_bf16):
    s, d = v.shape
    tm = HEAD
    perm = _chunk_perm()

    def body(dy_ref, v_ref, g_ref, res_ref, p_ref, *outs):
        dg_ref = outs[-1]
        i = pl.program_id(0)
        vv, dyv = v_ref[...], dy_ref[...]
        if dy_permuted:
            dyv = _unpermute_f32(p_ref[...], dyv) if dyv.dtype == F32 else _dot(p_ref[...], dyv, TN_DIMS)
        dyv = dyv.astype(F32)
        r = _rms(vv)
        dx = res_ref[...].astype(F32) + _rms_bwd(dyv, vv, r, g_ref[...])
        if with_bf16:
            dxb = dx.astype(BF16)
            outs[0][...] = dxb
            outs[1][...] = _dot(p_ref[...], dxb).astype(BF16)
        else:
            outs[0][...] = dx

        @pl.when(i == 0)
        def _():
            dg_ref[...] = jnp.zeros_like(dg_ref)

        dg_ref[...] += jnp.sum(dyv * vv * r, axis=0, keepdims=True)

    row = pl.BlockSpec((tm, d), lambda i: (i, 0))
    vec = pl.BlockSpec((1, d), lambda i: (0, 0))
    if with_bf16:
        shapes = [jax.ShapeDtypeStruct((s, d), BF16)] * 2
        specs = [row, row]
    else:
        shapes = [jax.ShapeDtypeStruct((s, d), F32)]
        specs = [row]
    shapes.append(jax.ShapeDtypeStruct((1, d), F32))
    specs.append(vec)
    return pl.pallas_call(
        body, name=name, grid=(s // tm,), out_shape=shapes,
        in_specs=[row, row, vec, row, pl.BlockSpec((HEAD, HEAD), lambda i: (0, 0))], out_specs=specs,
        compiler_params=_params(("arbitrary",), 40),
    )(dy, v, g, res, perm)


DEP_SPEC = pl.BlockSpec((SUB, HEAD), lambda *_: (0, 0))


def _drop_arg(body, pos):
    return lambda *refs: body(*refs[:pos], *refs[pos + 1:])


def _matmul(a, b, *, name, grid, a_spec, b_spec, dims, acc_shape, out_shapes, out_specs,
            extra=(), extra_specs=(), epilogue=None, vmem_mib=48, dep=None, prefetch=None, carry=None):
    nk = grid[2]
    n_user = len(extra)
    for unread, spec in ((dep, DEP_SPEC), (carry, HBM_SPEC)):
        if unread is not None:
            extra, extra_specs = tuple(extra) + (unread,), tuple(extra_specs) + (spec,)
    n_extra, n_out = len(extra), len(out_shapes)
    n_pre = 0 if prefetch is None else 1
    aliases = {} if carry is None else {n_pre + 2 + n_extra - 1: 0}

    def body(*refs):
        refs = refs[n_pre:]
        a_ref, b_ref = refs[0], refs[1]
        ex = refs[2:2 + n_user]
        outs = refs[2 + n_extra:2 + n_extra + n_out]
        av = a_ref[...]
        if av.dtype != BF16:
            av = av.astype(BF16)
        if nk == 1 and epilogue is None:
            outs[0][...] = _dot(av, b_ref[...], dims).astype(outs[0].dtype)
            return
        acc = refs[-1]
        k = pl.program_id(2)

        @pl.when(k == 0)
        def _():
            acc[...] = jnp.zeros_like(acc)

        if len(b_ref.shape) == 3:
            span = b_ref.shape[2]
            acc[...] += sum(_dot(av[:, g * span:(g + 1) * span], b_ref[g], dims) for g in range(b_ref.shape[0]))
        else:
            acc[...] += _dot(av, b_ref[...], dims)

        @pl.when(k == nk - 1)
        def _():
            if epilogue is None:
                outs[0][...] = acc[...].astype(outs[0].dtype)
            else:
                epilogue(acc, ex, outs)

    direct = nk == 1 and epilogue is None
    scratch = [] if direct else [pltpu.VMEM(acc_shape, F32)]
    params = _params(("parallel", "parallel", "arbitrary"), vmem_mib)
    if prefetch is None:
        return pl.pallas_call(
            body, name=name, grid=grid, out_shape=list(out_shapes),
            in_specs=[a_spec, b_spec, *extra_specs], out_specs=list(out_specs),
            scratch_shapes=scratch, compiler_params=params, input_output_aliases=aliases,
        )(a, b, *extra)
    return pl.pallas_call(
        body, name=name, out_shape=list(out_shapes),
        grid_spec=pltpu.PrefetchScalarGridSpec(
            num_scalar_prefetch=1, grid=grid, in_specs=[a_spec, b_spec, *extra_specs],
            out_specs=list(out_specs), scratch_shapes=scratch),
        compiler_params=params, input_output_aliases=aliases,
    )(prefetch, a, b, *extra)


def _tile(n, want):
    t = min(n, want)
    while n % t:
        t //= 2
    return t


def _rel_buckets(dil):
    order = BLOCK_ORDER[dil]
    qi = jnp.asarray(HEAD + order)
    kj = jnp.asarray(np.concatenate([order, HEAD + order]))
    delta = qi[:, None] - kj[None, :]
    band = (delta >= 0) & (delta <= HEAD)
    dist = jnp.clip(delta, 0, None) * dil
    max_exact = NUM_BUCKETS // 2
    dd = jnp.maximum(dist, 1).astype(F32)
    large = max_exact + (jnp.log(dd / max_exact) / math.log(MAX_DISTANCE / max_exact)
                         * (NUM_BUCKETS - max_exact)).astype(jnp.int32)
    large = jnp.minimum(large, NUM_BUCKETS - 1)
    bucket = jnp.where(dist < max_exact, dist, large)
    return jnp.where(band, bucket, -1).astype(jnp.int32)


def _bias_build(rel_bias, buckets, n_heads):
    nd = buckets.shape[0]

    def body(rb_ref, bk_ref, o_ref):
        for c in range(nd):
            def per_head(h, carry, c=c):
                bk = bk_ref[c]
                acc = jnp.where(bk < 0, NEG_INF, 0.0).astype(F32)
                for b in range(NUM_BUCKETS):
                    acc = jnp.where(bk == b, rb_ref[b, h], acc)
                o_ref[c, h] = acc
                return carry

            lax.fori_loop(0, n_heads, per_head, 0)

    return pl.pallas_call(
        body, name="bias_build",
        out_shape=jax.ShapeDtypeStruct((nd, n_heads, HEAD, 2 * HEAD), F32),
        in_specs=[pl.BlockSpec(memory_space=pltpu.SMEM), pl.BlockSpec(memory_space=pltpu.VMEM)],
        out_specs=pl.BlockSpec(memory_space=pltpu.VMEM),
    )(rel_bias, buckets)


def _bias_grad(ds_all, buckets, n_heads):
    nd = buckets.shape[0]
    pairs = HEAD * 2 * HEAD

    def body(ds_ref, bk_ref, o_ref):
        rows = lax.broadcasted_iota(jnp.int32, (NUM_BUCKETS, pairs), 0)
        tot = jnp.zeros((n_heads, NUM_BUCKETS), F32)
        for c in range(nd):
            onehot = (rows == bk_ref[c]).astype(BF16)
            ds = ds_ref[c]
            hi = ds.astype(BF16)
            lo = (ds - hi.astype(F32)).astype(BF16)
            tot = tot + _dot(hi, onehot, NT_DIMS) + _dot(lo, onehot, NT_DIMS)
        o_ref[...] = tot

    out = pl.pallas_call(
        body, name="bias_grad",
        out_shape=jax.ShapeDtypeStruct((n_heads, NUM_BUCKETS), F32),
        in_specs=[pl.BlockSpec(memory_space=pltpu.VMEM), pl.BlockSpec(memory_space=pltpu.VMEM)],
        out_specs=pl.BlockSpec(memory_space=pltpu.VMEM),
        compiler_params=pltpu.CompilerParams(vmem_limit_bytes=40 * MIB),
    )(ds_all.reshape(nd, n_heads, pairs), buckets.reshape(nd, 1, pairs))
    return out.T


def _qkv_prep(proj, g_q, g_k, w, dep=None):
    s = proj.shape[0]
    n_heads = w // HEAD
    tm = HEAD

    def body(q_ref, k_ref, gq_ref, gk_ref, qn_ref, kn_ref):
        gq = gq_ref[...] * (HEAD ** -0.5)
        gk = gk_ref[...]
        for h in range(n_heads):
            sl = slice(h * HEAD, (h + 1) * HEAD)
            q = q_ref[:, sl]
            k = k_ref[:, sl]
            qn_ref[:, sl] = q * _rms(q) * gq
            kn_ref[:, sl] = k * _rms(k) * gk

    seg = lambda j: pl.BlockSpec((tm, w), lambda i, j=j: (i, j))
    vec = pl.BlockSpec((1, HEAD), lambda i: (0, 0))
    out = pl.BlockSpec((tm, w), lambda i: (i, 0))
    in_specs = [seg(3), seg(4), vec, vec]
    operands = [proj, proj, g_q, g_k]
    if dep is not None:
        body = _drop_arg(body, len(operands))
        in_specs.append(DEP_SPEC)
        operands.append(dep)
    return pl.pallas_call(
        body, name="qkv_prep", grid=(s // tm,),
        out_shape=[jax.ShapeDtypeStruct((s, w), F32)] * 2,
        in_specs=in_specs, out_specs=[out, out],
        compiler_params=_params(("arbitrary",), 40),
    )(*operands)


class _BlockView:
    def __init__(self, s, dil):
        assert s % (HEAD * dil) == 0 and dil in BLOCK_ORDER
        self.nb = s // (HEAD * dil)
        if dil == 1:
            self.lead, self.block = (s,), (HEAD,)
            self.index = lambda r, n: (n,)
        elif dil == 4:
            self.lead, self.block = (s // 512, 4, 4, 4, SUB), (None, 4, 4, None, SUB)
            self.index = lambda r, n: (n, 0, 0, r, 0)
        else:
            self.lead, self.block = (s // 2048, 16, 16, SUB), (None, 16, None, SUB)
            self.index = lambda r, n: (n, 0, r, 0)

    def view(self, t):
        return t.reshape(self.lead + (t.shape[-1],))

    def spec(self, width, block_of, column=0):
        return pl.BlockSpec(self.block + (width,), lambda r, n: self.index(r, block_of(r, n)) + (column,))


def _rows(ref, lanes=slice(None)):
    v = ref[(slice(None),) * (len(ref.shape) - 1) + (lanes,)]
    return v.reshape(HEAD, v.shape[-1])


def _set_rows(ref, lanes, value):
    ref[(slice(None),) * (len(ref.shape) - 1) + (lanes,)] = value.reshape(ref.shape[:-1] + (value.shape[-1],))


V_SEGMENT = 5


def _attn_fwd(qn, kn, proj, bias, dil, name, dep=None):
    s, w = qn.shape
    n_heads = w // HEAD
    bv = _BlockView(s, dil)

    def body(q_ref, kc_ref, vc_ref, bias_ref, o_ref, lse_ref, s_scr, e_scr, lse_scr, inv_scr, k_prev, v_prev):
        n = pl.program_id(1)
        heads = [slice(h * HEAD, (h + 1) * HEAD) for h in range(n_heads)]
        lse_scr[...] = jnp.zeros_like(lse_scr)

        @pl.when(n == 0)
        def _():
            k_prev[...] = jnp.zeros_like(k_prev)
            v_prev[...] = jnp.zeros_like(v_prev)

        for h, sl in enumerate(heads):
            q = _rows(q_ref, sl).astype(BF16)
            s_p = _dot(q, k_prev[:, sl], NT_DIMS) + bias_ref[h, :, :HEAD]
            s_scr[h, :, :HEAD] = jnp.where(n > 0, s_p, NEG_INF)
            s_scr[h, :, HEAD:] = _dot(q, _rows(kc_ref, sl).astype(BF16), NT_DIMS) + bias_ref[h, :, HEAD:]
        for h in range(n_heads):
            sc = s_scr[h]
            m = jnp.max(sc, axis=-1, keepdims=True)
            e = jnp.exp(sc - m)
            den = jnp.sum(e, axis=-1, keepdims=True)
            e_scr[h] = e.astype(BF16)
            lse_scr[:, h:h + 1] = m + jnp.log(den)
            inv_scr[:, h:h + 1] = 1.0 / den
        for h, sl in enumerate(heads):
            v_cur = _rows(vc_ref, sl).astype(BF16)
            o = _dot(e_scr[h, :, :HEAD], v_prev[:, sl]) + _dot(e_scr[h, :, HEAD:], v_cur)
            _set_rows(o_ref, sl, o * inv_scr[:, h:h + 1])
            v_prev[:, sl] = v_cur
            k_prev[:, sl] = _rows(kc_ref, sl).astype(BF16)
        _set_rows(lse_ref, slice(None), lse_scr[...])

    cur = bv.spec(w, lambda r, n: n)
    in_specs = [cur, cur, bv.spec(w, lambda r, n: n, V_SEGMENT),
                pl.BlockSpec((n_heads, HEAD, 2 * HEAD), lambda r, n: (0, 0, 0))]
    operands = [bv.view(qn), bv.view(kn), bv.view(proj), bias]
    if dep is not None:
        body = _drop_arg(body, len(operands))
        in_specs.append(DEP_SPEC)
        operands.append(dep)
    o, lse = pl.pallas_call(
        body, name=name, grid=(dil, bv.nb),
        out_shape=[jax.ShapeDtypeStruct(bv.lead + (w,), F32), jax.ShapeDtypeStruct(bv.lead + (HEAD,), F32)],
        in_specs=in_specs,
        out_specs=[cur, bv.spec(HEAD, lambda r, n: n)],
        scratch_shapes=[pltpu.VMEM((n_heads, HEAD, 2 * HEAD), F32), pltpu.VMEM((n_heads, HEAD, 2 * HEAD), BF16),
                        pltpu.VMEM((HEAD, HEAD), F32), pltpu.VMEM((HEAD, HEAD), F32),
                        pltpu.VMEM((HEAD, w), BF16), pltpu.VMEM((HEAD, w), BF16)],
        compiler_params=_params(("arbitrary", "arbitrary"), 48),
    )(*operands)
    return o.reshape(s, w), lse.reshape(s, HEAD)


def _attn_bwd(qn, kn, proj, dyb, lse, delta, bias, dil, name, running=None, dep=None):
    s, w = qn.shape
    n_heads = w // HEAD
    bv = _BlockView(s, dil)
    nb = bv.nb

    n_run = 0 if running is None else 3

    def body(q_ref, kc_ref, kp_ref, vc_ref, vp_ref, dy_ref, lse_ref, dl_ref, bias_ref, *rest):
        so_far = rest[:n_run]
        dq_ref, dk_ref, dv_ref, ds_ref, carry_k, carry_v, s_scr, dp_scr, p_scr, dsb_scr, k_cur, v_cur = rest[n_run:]
        base = (lambda i, sl: _rows(so_far[i], sl)) if n_run else (lambda i, sl: 0.0)
        r = pl.program_id(0)
        step = pl.program_id(1)
        blk = nb - 1 - step

        @pl.when((r == 0) & (step == 0))
        def _():
            ds_ref[...] = jnp.zeros_like(ds_ref)

        @pl.when(step == 0)
        def _():
            carry_k[...] = jnp.zeros_like(carry_k)
            carry_v[...] = jnp.zeros_like(carry_v)
            k_cur[...] = _rows(kc_ref).astype(BF16)
            v_cur[...] = _rows(vc_ref).astype(BF16)

        heads = [slice(h * HEAD, (h + 1) * HEAD) for h in range(n_heads)]
        tots = _rows(lse_ref)
        dls = _rows(dl_ref)
        for h, sl in enumerate(heads):
            q, dy = _rows(q_ref, sl).astype(BF16), _rows(dy_ref, sl).astype(BF16)
            kp, kc = _rows(kp_ref, sl).astype(BF16), k_cur[:, sl]
            vp, vc = _rows(vp_ref, sl).astype(BF16), v_cur[:, sl]
            s_p = _dot(q, kp, NT_DIMS) + bias_ref[h, :, :HEAD]
            s_scr[h, :, :HEAD] = jnp.where(blk > 0, s_p, NEG_INF)
            s_scr[h, :, HEAD:] = _dot(q, kc, NT_DIMS) + bias_ref[h, :, HEAD:]
            dp_scr[h, :, :HEAD] = _dot(dy, vp, NT_DIMS)
            dp_scr[h, :, HEAD:] = _dot(dy, vc, NT_DIMS)
        for h in range(n_heads):
            prob = jnp.exp(s_scr[h] - tots[:, h:h + 1])
            ds = prob * (dp_scr[h] - dls[:, h:h + 1])
            ds_ref[h] += ds
            p_scr[h] = prob.astype(BF16)
            dsb_scr[h] = ds.astype(BF16)
        for h, sl in enumerate(heads):
            q, dy = _rows(q_ref, sl).astype(BF16), _rows(dy_ref, sl).astype(BF16)
            kp, kc = _rows(kp_ref, sl).astype(BF16), k_cur[:, sl]
            ds_pb, ds_cb = dsb_scr[h, :, :HEAD], dsb_scr[h, :, HEAD:]
            _set_rows(dq_ref, sl, _dot(ds_pb, kp) + _dot(ds_cb, kc) + base(0, sl))
            _set_rows(dk_ref, sl, _dot(ds_cb, q, TN_DIMS) + carry_k[:, sl] + base(1, sl))
            carry_k[:, sl] = _dot(ds_pb, q, TN_DIMS)
            _set_rows(dv_ref, sl, _dot(p_scr[h, :, HEAD:], dy, TN_DIMS) + carry_v[:, sl] + base(2, sl))
            carry_v[:, sl] = _dot(p_scr[h, :, :HEAD], dy, TN_DIMS)
            k_cur[:, sl] = kp
            v_cur[:, sl] = _rows(vp_ref, sl).astype(BF16)

    cur = bv.spec(w, lambda r, n: nb - 1 - n)
    last = bv.spec(w, lambda r, n: nb - 1)
    prev = bv.spec(w, lambda r, n: jnp.maximum(nb - 2 - n, 0))
    stat = bv.spec(HEAD, lambda r, n: nb - 1 - n)
    whole = pl.BlockSpec((n_heads, HEAD, 2 * HEAD), lambda r, n: (0, 0, 0))
    big = jax.ShapeDtypeStruct(bv.lead + (w,), F32)
    v_last = bv.spec(w, lambda r, n: nb - 1, V_SEGMENT)
    v_prev = bv.spec(w, lambda r, n: jnp.maximum(nb - 2 - n, 0), V_SEGMENT)
    in_specs = [cur, last, prev, v_last, v_prev, cur, stat, stat, whole]
    operands = [bv.view(qn), bv.view(kn), bv.view(kn), bv.view(proj), bv.view(proj), bv.view(dyb), bv.view(lse),
                bv.view(delta), bias]
    aliases = {}
    if running is not None:
        aliases = {len(operands) + i: i for i in range(3)}
        in_specs += [cur] * 3
        operands += [bv.view(t) for t in running]
    if dep is not None:
        body = _drop_arg(body, len(operands))
        in_specs.append(DEP_SPEC)
        operands.append(dep)
    dq, dk, dv, ds = pl.pallas_call(
        body, name=name, grid=(dil, nb),
        out_shape=[big, big, big, jax.ShapeDtypeStruct((n_heads, HEAD, 2 * HEAD), F32)],
        in_specs=in_specs, out_specs=[cur, cur, cur, whole], input_output_aliases=aliases,
        scratch_shapes=[pltpu.VMEM((HEAD, w), F32), pltpu.VMEM((HEAD, w), F32),
                        pltpu.VMEM((n_heads, HEAD, 2 * HEAD), F32), pltpu.VMEM((n_heads, HEAD, 2 * HEAD), F32),
                        pltpu.VMEM((n_heads, HEAD, 2 * HEAD), BF16), pltpu.VMEM((n_heads, HEAD, 2 * HEAD), BF16),
                        pltpu.VMEM((HEAD, w), BF16), pltpu.VMEM((HEAD, w), BF16)],
        compiler_params=_params(("arbitrary", "arbitrary"), 56),
    )(*operands)
    return dq.reshape(s, w), dk.reshape(s, w), dv.reshape(s, w), ds


def _qkv_bwd(dproj, proj, dq, dk, dv, g_q, g_k, w):
    s = proj.shape[0]
    n_heads = w // HEAD
    tm = HEAD

    def body(dproj_hbm, q_ref, k_ref, gq_ref, gk_ref, dq_ref, dk_ref, dv_ref, out_ref, dgq_ref, dgk_ref):
        i = pl.program_id(0)
        gq = gq_ref[...] * (HEAD ** -0.5)
        gk = gk_ref[...]
        acc_q = jnp.zeros((1, HEAD), F32)
        acc_k = jnp.zeros((1, HEAD), F32)
        for h in range(n_heads):
            sl = slice(h * HEAD, (h + 1) * HEAD)
            q, k = q_ref[:, sl], k_ref[:, sl]
            dqn, dkn = dq_ref[:, sl], dk_ref[:, sl]
            rq, rk = _rms(q), _rms(k)
            out_ref[:, h * HEAD:(h + 1) * HEAD] = _rms_bwd(dqn, q, rq, gq).astype(BF16)
            out_ref[:, w + h * HEAD:w + (h + 1) * HEAD] = _rms_bwd(dkn, k, rk, gk).astype(BF16)
            acc_q += jnp.sum(dqn * q * rq, axis=0, keepdims=True)
            acc_k += jnp.sum(dkn * k * rk, axis=0, keepdims=True)
        out_ref[:, 2 * w:] = dv_ref[...].astype(BF16)

        @pl.when(i == 0)
        def _():
            dgq_ref[...] = jnp.zeros_like(dgq_ref)
            dgk_ref[...] = jnp.zeros_like(dgk_ref)

        dgq_ref[...] += acc_q * (HEAD ** -0.5)
        dgk_ref[...] += acc_k

    seg = lambda j: pl.BlockSpec((tm, w), lambda i, j=j: (i, j))
    vec = pl.BlockSpec((1, HEAD), lambda i: (0, 0))
    row = pl.BlockSpec((tm, w), lambda i: (i, 0))
    return pl.pallas_call(
        body, name="qkv_bwd", grid=(s // tm,),
        out_shape=[jax.ShapeDtypeStruct(dproj.shape, BF16),
                   jax.ShapeDtypeStruct((1, HEAD), F32), jax.ShapeDtypeStruct((1, HEAD), F32)],
        in_specs=[pl.BlockSpec(memory_space=pl.ANY), seg(3), seg(4), vec, vec] + [row] * 3,
        out_specs=[pl.BlockSpec((tm, 3 * w), lambda i: (i, 1)), vec, vec],
        input_output_aliases={0: 0},
        compiler_params=_params(("arbitrary",), 48),
    )(dproj, proj, proj, g_q, g_k, dq, dk, dv)


def _mixer_a(u, gv, ws_ref, bst_ref, lng, lnb, z_scr, ln_scr):
    n_groups = u.shape[1] // HEAD
    mu = jnp.mean(gv, axis=-1, keepdims=True)
    xc = gv - mu
    rs = lax.rsqrt(jnp.mean(xc * xc, axis=-1, keepdims=True) + EPS)
    xhat = xc * rs
    ln_scr[...] = (xhat * lng + lnb).astype(BF16)
    causal = _causal_mask()
    for g in range(n_groups):
        sl = slice(g * HEAD, (g + 1) * HEAD)
        wm = jnp.where(causal, ws_ref[g], 0.0).astype(BF16)
        z_scr[:, sl] = _dot(wm, ln_scr[:, sl]) + bst_ref[:, g:g + 1]
    return u, xhat, rs


def _causal_mask():
    token = lambda r: 16 * (r % SUB) + r // SUB
    row = lax.broadcasted_iota(jnp.int32, (HEAD, HEAD), 0)
    col = lax.broadcasted_iota(jnp.int32, (HEAD, HEAD), 1)
    return token(col) <= token(row)


def _merge_b(o_refs, lse_refs, yb_scr):
    n_heads = yb_scr.shape[1] // HEAD
    lses = [t[...] for t in lse_refs]
    m = jnp.maximum(jnp.maximum(lses[0], lses[1]), lses[2])
    tot = m + jnp.log(sum(jnp.exp(t - m) for t in lses))
    alphas = [jnp.exp(t - tot) for t in lses]
    for h in range(n_heads):
        sl = slice(h * HEAD, (h + 1) * HEAD)
        yb_scr[:, sl] = sum(a[:, h:h + 1] * o[:, sl].astype(F32) for a, o in zip(alphas, o_refs))
    return tot


def _mix_fwd(proj, outs, lses, w_s, bst, ln_g, ln_b, g_a, g_b, w, dep=None):
    s = proj.shape[0]
    n_groups = w // HEAD

    def body(au_ref, av_ref, az_ref, bz_ref, o1, o2, o3, l1, l2, l3, ws_ref, bst_ref,
             lng_ref, lnb_ref, ga_ref, gb_ref, p_ref, y_ref, z_scr, ln_scr, yb_scr):
        u, _, _ = _mixer_a(_gelu(au_ref[...]), _gelu(av_ref[...]), ws_ref, bst_ref, lng_ref[...], lnb_ref[...],
                           z_scr, ln_scr)
        ya = u * z_scr[...]
        silu_a, _ = _silu_and_grad(az_ref[...])
        perm = p_ref[...]
        y_ref[:, :w] = _dot(perm, (ya * _rms(ya) * ga_ref[...] * silu_a).astype(BF16), TN_DIMS).astype(BF16)
        _merge_b((o1, o2, o3), (l1, l2, l3), yb_scr)
        yb = yb_scr[...]
        silu_b, _ = _silu_and_grad(bz_ref[...])
        y_ref[:, w:] = _dot(perm, (yb * _rms(yb) * gb_ref[...] * silu_b).astype(BF16), TN_DIMS).astype(BF16)

    seg = lambda j: pl.BlockSpec((HEAD, w), lambda i, j=j: (i, j))
    row = pl.BlockSpec((HEAD, w), lambda i: (i, 0))
    stat = pl.BlockSpec((HEAD, HEAD), lambda i: (i, 0))
    vec = pl.BlockSpec((1, w), lambda i: (0, 0))
    in_specs = [seg(0), seg(1), seg(2), seg(6), row, row, row, stat, stat, stat,
                pl.BlockSpec((n_groups, HEAD, HEAD), lambda i: (0, 0, 0)),
                pl.BlockSpec((HEAD, n_groups), lambda i: (0, 0)), vec, vec, vec, vec,
                pl.BlockSpec((HEAD, HEAD), lambda i: (0, 0))]
    operands = [proj, proj, proj, proj, *outs, *lses, w_s, bst, ln_g, ln_b, g_a, g_b, _chunk_perm()]
    if dep is not None:
        body = _drop_arg(body, len(operands))
        in_specs.append(DEP_SPEC)
        operands.append(dep)
    return pl.pallas_call(
        body, name="mix_fwd", grid=(s // HEAD,),
        out_shape=jax.ShapeDtypeStruct((s, 2 * w), BF16), in_specs=in_specs,
        out_specs=pl.BlockSpec((HEAD, 2 * w), lambda i: (i, 0)),
        scratch_shapes=[pltpu.VMEM((HEAD, w), F32), pltpu.VMEM((HEAD, w), BF16), pltpu.VMEM((HEAD, w), F32)],
        compiler_params=_params(("arbitrary",), 48),
    )(*operands)


def _mix_bwd(proj, dy, outs, lses, w_s, bst, ln_g, ln_b, g_a, g_b, w):
    s = proj.shape[0]
    n_groups = w // HEAD

    def body(au_ref, av_ref, az_ref, bz_ref, dy_ref, o1, o2, o3, l1, l2, l3, ws_ref, bst_ref,
             lng_ref, lnb_ref, ga_ref, gb_ref,
             dproj_ref, dyb_ref, tot_ref, dl_ref, dws_ref, dbst_ref, dlng_ref, dlnb_ref, dga_ref, dgb_ref,
             z_scr, ln_scr, yb_scr, dz_scr, dln_scr):
        i = pl.program_id(0)

        @pl.when(i == 0)
        def _():
            for t in (dws_ref, dbst_ref, dlng_ref, dlnb_ref, dga_ref, dgb_ref):
                t[...] = jnp.zeros_like(t)

        az = az_ref[...]
        lng = lng_ref[...]
        u, du_dau = _gelu_and_grad(au_ref[...])
        gv, dgv_dav = _gelu_and_grad(av_ref[...])
        u, xhat, rs = _mixer_a(u, gv, ws_ref, bst_ref, lng, lnb_ref[...], z_scr, ln_scr)
        z = z_scr[...]
        ya = u * z
        ra = _rms(ya)
        silu_a, dsilu_a = _silu_and_grad(az)
        dya_all = dy_ref[:, :w].astype(F32)
        na = ya * ra * ga_ref[...]
        dna = dya_all * silu_a
        dproj_ref[:, 2 * w:3 * w] = (dya_all * na * dsilu_a).astype(BF16)
        dga_ref[...] += jnp.sum(dna * ya * ra, axis=0, keepdims=True)
        dya = _rms_bwd(dna, ya, ra, ga_ref[...])
        dproj_ref[:, :w] = (dya * z * du_dau).astype(BF16)
        dz_scr[...] = (dya * u).astype(BF16)

        causal = _causal_mask()
        for g in range(n_groups):
            sl = slice(g * HEAD, (g + 1) * HEAD)
            wm = jnp.where(causal, ws_ref[g], 0.0).astype(BF16)
            dz = dz_scr[:, sl]
            dln_scr[:, sl] = _dot(wm, dz, TN_DIMS)
            dws_ref[g] += jnp.where(causal, _dot(dz, ln_scr[:, sl], NT_DIMS), 0.0)
            dbst_ref[:, g:g + 1] += jnp.sum(dz.astype(F32), axis=-1, keepdims=True)
        dln = dln_scr[...]
        dlng_ref[...] += jnp.sum(dln * xhat, axis=0, keepdims=True)
        dlnb_ref[...] += jnp.sum(dln, axis=0, keepdims=True)
        gy = dln * lng
        dgv = rs * (gy - jnp.mean(gy, axis=-1, keepdims=True)
                    - xhat * jnp.mean(gy * xhat, axis=-1, keepdims=True))
        dproj_ref[:, w:2 * w] = (dgv * dgv_dav).astype(BF16)
        dproj_ref[:, 3 * w:6 * w] = jnp.zeros((HEAD, 3 * w), BF16)

        tot_ref[...] = _merge_b((o1, o2, o3), (l1, l2, l3), yb_scr)
        yb = yb_scr[...]
        rb = _rms(yb)
        bz = bz_ref[...]
        silu_b, dsilu_b = _silu_and_grad(bz)
        dyb_all = dy_ref[:, w:].astype(F32)
        dnb = dyb_all * silu_b
        dproj_ref[:, 6 * w:] = (dyb_all * yb * rb * gb_ref[...] * dsilu_b).astype(BF16)
        dgb_ref[...] += jnp.sum(dnb * yb * rb, axis=0, keepdims=True)
        dyb = _rms_bwd(dnb, yb, rb, gb_ref[...])
        dyb_ref[...] = dyb
        prod = dyb * yb
        dl_ref[...] = _lane_pick(
            [jnp.sum(prod[:, h * HEAD:(h + 1) * HEAD], axis=-1, keepdims=True) for h in range(n_groups)], HEAD)

    seg = lambda j: pl.BlockSpec((HEAD, w), lambda i, j=j: (i, j))
    row_w = pl.BlockSpec((HEAD, w), lambda i: (i, 0))
    stat = pl.BlockSpec((HEAD, HEAD), lambda i: (i, 0))
    vec = pl.BlockSpec((1, w), lambda i: (0, 0))
    ws_spec = pl.BlockSpec((n_groups, HEAD, HEAD), lambda i: (0, 0, 0))
    bst_spec = pl.BlockSpec((HEAD, n_groups), lambda i: (0, 0))
    vec_shape = jax.ShapeDtypeStruct((1, w), F32)
    return pl.pallas_call(
        body, name="mix_bwd", grid=(s // HEAD,),
        out_shape=[jax.ShapeDtypeStruct((s, N_SEG * w), BF16), jax.ShapeDtypeStruct((s, w), F32),
                   jax.ShapeDtypeStruct((s, HEAD), F32), jax.ShapeDtypeStruct((s, HEAD), F32),
                   jax.ShapeDtypeStruct((n_groups, HEAD, HEAD), F32), jax.ShapeDtypeStruct((HEAD, n_groups), F32),
                   vec_shape, vec_shape, vec_shape, vec_shape],
        in_specs=[seg(0), seg(1), seg(2), seg(6), pl.BlockSpec((HEAD, 2 * w), lambda i: (i, 0)),
                  row_w, row_w, row_w, stat, stat, stat, ws_spec, bst_spec, vec, vec, vec, vec],
        out_specs=[pl.BlockSpec((HEAD, N_SEG * w), lambda i: (i, 0)), row_w, stat, stat,
                   ws_spec, bst_spec, vec, vec, vec, vec],
        scratch_shapes=[pltpu.VMEM((HEAD, w), F32), pltpu.VMEM((HEAD, w), BF16), pltpu.VMEM((HEAD, w), F32),
                        pltpu.VMEM((HEAD, w), BF16), pltpu.VMEM((HEAD, w), F32)],
        compiler_params=_params(("arbitrary",), 56),
    )(proj, proj, proj, proj, dy, *outs, *lses, w_s, bst, ln_g, ln_b, g_a, g_b)


def _ple_gate(hn2, wgate, h, p, wup_g, tgt):
    s, d = h.shape
    n, p_dim, c_up = wup_g.shape
    tm, pair = _tile(s, 512), 2
    tn = pair * c_up
    cols = n // pair
    tiles = (s // tm) * cols
    cur = lambda t: jnp.minimum(t, tiles - 1)
    prv = lambda t: jnp.maximum(t - 1, 0)

    def body(a_ref, b_ref, h_ref, p_ref, wup_ref, tgt_ref, dout_ref, dpre_ref, dup_ref, loss_ref, acc):
        @pl.when(pl.program_id(0) == 0)
        def _():
            acc[...] = jnp.zeros_like(acc)

        gate = jax.nn.sigmoid(acc[...])
        pb = p_ref[...].astype(BF16)
        up = jnp.concatenate([_dot(pb, wup_ref[g]) for g in range(pair)], axis=1)
        err = h_ref[...] + gate * up - tgt_ref[...]
        dout = err * (1.0 / d)
        dout_ref[...] = dout.astype(BF16)
        dpre_ref[...] = (dout * up * gate * (1.0 - gate)).astype(BF16)
        dup_ref[...] = (dout * gate).astype(BF16)
        part = 0.5 * jnp.sum(err * err) * (1.0 / d)
        rr = lax.broadcasted_iota(jnp.int32, (SUB, HEAD), 0)
        cc = lax.broadcasted_iota(jnp.int32, (SUB, HEAD), 1)
        loss_ref[...] = jnp.where((rr == 0) & (cc == 0), part, 0.0)
        acc[...] = _dot(a_ref[...], b_ref[...])

    rows = s // tm
    tail = pl.BlockSpec((tm, tn), lambda t: (prv(t) % rows, prv(t) // rows))
    big = jax.ShapeDtypeStruct((s, d), BF16)
    return pl.pallas_call(
        body, name="ple_gate", grid=(tiles + 1,),
        out_shape=[big, big, big, jax.ShapeDtypeStruct((rows * SUB, cols * HEAD), F32)],
        in_specs=[pl.BlockSpec((tm, d), lambda t: (cur(t) % rows, 0)),
                  pl.BlockSpec((d, tn), lambda t: (0, cur(t) // rows)),
                  tail, pl.BlockSpec((tm, p_dim), lambda t: (prv(t) % rows, 0)),
                  pl.BlockSpec((pair, p_dim, c_up), lambda t: (prv(t) // rows, 0, 0)), tail],
        out_specs=[tail, tail, tail, pl.BlockSpec((SUB, HEAD), lambda t: (prv(t) % rows, prv(t) // rows))],
        scratch_shapes=[pltpu.VMEM((tm, tn), F32)],
        compiler_params=_params(("arbitrary",), 60),
    )(hn2, wgate, h, p, wup_g, tgt)


def _local_step(x, p, tgt, small, wg, ex, while_last_travels=None):
    s, d = x.shape
    n, _, c_in = wg.buffers["w_in"].shape
    assert n == N_DEV
    d_in = n * c_in
    w = d_in // N_SEG
    n_heads = w // HEAD
    p_dim, c_up = wg.buffers["w_ple_up"].shape[1:]
    assert s % (HEAD * DILATIONS[-1]) == 0 and w % HEAD == 0 and d == n * c_up == 2 * w

    near = (2, 4)
    wg.start("gather_in_pair", ["w_in"], (1,))
    token = wg.start("gather_in_first", ["w_in"], ("first",))
    rest = ["w_out", "w_ple_gate", "w_ple_up"]

    hn = _rmsnorm_fwd(x, small["g_pre"], "pre_norm", True, dep=token)
    tm = _tile(s, 1024)

    def in_proj(shards, name, carry, dep=None):
        return _matmul(
            hn, wg.buffers["w_in"], name=name, grid=(s // tm, len(shards), 1), dims=NN_DIMS,
            prefetch=jnp.stack(shards).astype(jnp.int32),
            a_spec=pl.BlockSpec((tm, d), lambda i, j, k, sh: (i, 0)),
            b_spec=pl.BlockSpec((None, d, c_in), lambda i, j, k, sh: (sh[j], 0, 0), pipeline_mode=pl.Buffered(1)),
            acc_shape=(tm, c_in), out_shapes=[jax.ShapeDtypeStruct((s, d_in), F32)],
            out_specs=[pl.BlockSpec((tm, c_in), lambda i, j, k, sh: (i, sh[j]))], carry=carry, dep=dep,
            vmem_mib=56)[0]

    me = wg.me
    core = me & 1
    first, second, far = me ^ (4 - 2 * core), me ^ (2 + 2 * core), me ^ 6
    proj = in_proj([me], "in_proj_own", None)
    token = wg.start("gather_in_second", ["w_in"], ("second",), after=[proj])
    wg.arrived("gather_in_pair", [token])
    proj = in_proj([me ^ 1], "in_proj_sibling", proj)
    buckets = jnp.stack([_rel_buckets(dil) for dil in DILATIONS])
    bias = _bias_build(small["rel_bias"], buckets, n_heads)
    ahead = [bias] + [wg.buffers[k] for k in rest]
    for tag, mine, from_sibling in (("first", first, second ^ 1), ("second", second, first ^ 1), ("far", far, far ^ 1)):
        token = wg.forward("gather_in_" + tag, [proj] + ahead)
        ahead = []
        if tag == "second":
            token = wg.relay_start("gather_in_far", ["w_in"], after=[token])
        elif tag == "far":
            token = wg.start("gather_rest", rest, (1,) + near, after=[token])
        proj = in_proj([mine], "in_proj_" + tag, proj, dep=token)
        wg.forwarded("gather_in_" + tag, [proj])
        proj = in_proj([from_sibling], "in_proj_%s_forwarded" % tag, proj)
    win_g = wg.buffers["w_in"]

    qn, kn = _qkv_prep(proj, small["g_q"], small["g_k"], w)
    token = wg.forward("gather_rest", [qn])
    token = wg.relay_start("gather_rest_far", rest, after=[token])
    outs, lses = [], []
    for c, dil in enumerate(DILATIONS):
        o, l = _attn_fwd(qn, kn, proj, bias[c], dil, "attn_fwd_d%d" % dil, dep=token)
        outs.append(o)
        lses.append(l)
    token = wg.forward("gather_rest_far", outs)

    ws_p = small["w_s"][:, CHUNK_ORDER][:, :, CHUNK_ORDER]
    bst = small["b_s"].T[CHUNK_ORDER]
    mix_args = (outs, lses, ws_p, bst, small["ln_v_g"], small["ln_v_b"], small["g_out_a"], small["g_out_b"], w)
    y = _mix_fwd(proj, *mix_args, dep=token)
    wg.forwarded("gather_rest", [y])
    wg.forwarded("gather_rest_far", [y])
    wout_g, wgate_g, wup_g = (wg.buffers[k] for k in rest)
    wout_f = wout_g.reshape(2 * w, d)
    wgate_f = wgate_g.reshape(d, d)

    tn = _tile(d, 1024)
    tk2 = 2 * w

    def resid_epilogue(acc, ex, outs_):
        outs_[0][...] = ex[0][...] + acc[...]

    h = _matmul(
        y, wout_f, name="out_proj", grid=(s // tm, d // tn, (2 * w) // tk2), dims=NN_DIMS,
        a_spec=pl.BlockSpec((tm, tk2), lambda i, j, k: (i, k)),
        b_spec=pl.BlockSpec((tk2, tn), lambda i, j, k: (k, j)),
        acc_shape=(tm, tn), out_shapes=[jax.ShapeDtypeStruct((s, d), F32)],
        out_specs=[pl.BlockSpec((tm, tn), lambda i, j, k: (i, j))],
        extra=(x,), extra_specs=(pl.BlockSpec((tm, tn), lambda i, j, k: (i, j)),),
        epilogue=resid_epilogue, vmem_mib=56)[0]

    hn2 = _rmsnorm_fwd(h, small["g_ple"], "ple_norm", False)

    dout, dpre, dup, loss_parts = _ple_gate(hn2, wgate_f, h, p, wup_g, tgt)
    loss = jnp.sum(loss_parts)

    tks = _tile(s, 2048)
    g_wup = _matmul(
        p, dup, name="grad_w_up", grid=(1, n, s // tks), dims=TN_DIMS,
        a_spec=pl.BlockSpec((tks, p_dim), lambda i, j, k: (k, 0)),
        b_spec=pl.BlockSpec((tks, c_up), lambda i, j, k: (k, j)),
        acc_shape=(p_dim, c_up), out_shapes=[jax.ShapeDtypeStruct((n, p_dim, c_up), BF16)],
        out_specs=[pl.BlockSpec((None, p_dim, c_up), lambda i, j, k: (j, 0, 0))])[0]

    def tn_matmul(a, b, name):
        m_, n_ = a.shape[1], b.shape[1]
        bm, bn = _tile(m_, 1024), _tile(n_, 1024)
        return _matmul(
            a, b, name=name, grid=(m_ // bm, n_ // bn, 1), dims=TN_DIMS,
            a_spec=pl.BlockSpec((s, bm), lambda i, j, k: (0, i)),
            b_spec=pl.BlockSpec((s, bn), lambda i, j, k: (0, j)),
            acc_shape=(bm, bn), out_shapes=[jax.ShapeDtypeStruct((m_, n_), BF16)],
            out_specs=[pl.BlockSpec((bm, bn), lambda i, j, k: (i, j))], vmem_mib=56)[0]

    def nt_matmul(a, b, name, out_dtype, dep=None):
        k_, n_ = a.shape[1], b.shape[0]
        bm, bn, bk = _tile(s, 1024), _tile(n_, 1024), k_
        return _matmul(
            a, b, name=name, grid=(s // bm, n_ // bn, k_ // bk), dims=NT_DIMS,
            a_spec=pl.BlockSpec((bm, bk), lambda i, j, k: (i, k)),
            b_spec=pl.BlockSpec((bn, bk), lambda i, j, k: (j, k)),
            acc_shape=(bm, bn), out_shapes=[jax.ShapeDtypeStruct((s, n_), out_dtype)],
            out_specs=[pl.BlockSpec((bm, bn), lambda i, j, k: (i, j))], dep=dep, vmem_mib=56)[0]

    by_core = lambda g: g.reshape((N_CHIP, 2) + g.shape[-2:])
    g_wgate = tn_matmul(hn2, dpre, "grad_w_gate").reshape(wgate_g.shape)
    dhn2 = nt_matmul(dpre, wgate_f, "ple_gate_bwd", BF16)
    dh_b, dh_bp, dg_ple = _rmsnorm_bwd(dhn2, h, small["g_ple"], dout, "ple_norm_bwd", False, True)
    g_wout = tn_matmul(y, dh_b, "grad_w_out").reshape(wout_g.shape)

    late = ("w_out", "w_ple_gate", "w_ple_up")
    late_parts = (g_wout, g_wgate, g_wup)
    token = ex.push_pairs("pair_late", [by_core(g) for g in late_parts])
    dy = nt_matmul(dh_bp, wout_f, "out_proj_bwd", BF16, dep=token)
    (dproj, dyb, lse_tot, delta, dws, dbst, dlng, dlnb, dga, dgb) = _mix_bwd(proj, dy, *mix_args)
    both_columns, from_sibling = ex.pairs_done("pair_late", [dproj])
    pair_sums = [_pair_add(mine.reshape((N_DEV,) + mine.shape[-2:]), theirs, "pair_add_" + k, ex.core)
                 for k, mine, theirs in zip(late, both_columns, from_sibling)]
    token = ex.push_chips("chip_late", pair_sums)

    running, dss = None, []
    for c, dil in enumerate(DILATIONS):
        *running, ds = _attn_bwd(qn, kn, proj, dyb, lse_tot, delta, bias[c], dil, "attn_bwd_d%d" % dil,
                                 running=running, dep=token if c == 0 else None)
        dss.append(ds)
    d_rel = _bias_grad(jnp.stack(dss), buckets, n_heads)
    dproj, dgq, dgk = _qkv_bwd(dproj, proj, *running, small["g_q"], small["g_k"], w)
    pair_sums, landed, _ = ex.chips_done("chip_late", [dproj])
    delivered = {k: (mine, theirs) for k, mine, theirs in zip(late, pair_sums, landed)}

    token_row = np.argsort(CHUNK_ORDER)
    dws = dws[:, token_row][:, :, token_row]
    dbst = dbst[token_row]
    small_grads = {
        "w_s": dws, "b_s": dbst.T, "ln_v_g": dlng, "ln_v_b": dlnb, "g_q": dgq, "g_k": dgk,
        "rel_bias": d_rel, "g_out_a": dga, "g_out_b": dgb, "g_ple": dg_ple,
    }

    bm = _tile(d, 1024)

    def grad_w_in(core, name, dep=None):
        return _matmul(
            hn, dproj, name=name, grid=(d // bm, N_CHIP, 1), dims=TN_DIMS, prefetch=core.reshape(1),
            a_spec=pl.BlockSpec((s, bm), lambda i, j, k, core_ref: (0, i)),
            b_spec=pl.BlockSpec((s, c_in), lambda i, j, k, core_ref: (0, 2 * j + core_ref[0])),
            acc_shape=(bm, c_in), out_shapes=[jax.ShapeDtypeStruct((N_CHIP, d, c_in), BF16)],
            out_specs=[pl.BlockSpec((None, bm, c_in), lambda i, j, k, core_ref: (j, i, 0))], dep=dep,
            vmem_mib=60)[0]

    for_sibling = grad_w_in(1 - ex.core, "grad_w_in_sibling")
    token = ex.push_pairs("pair_in", [for_sibling])
    mine = grad_w_in(ex.core, "grad_w_in_mine", dep=token)
    _, from_sibling = ex.pairs_done("pair_in", [mine])
    pair_sum = _pair_add(mine, from_sibling[0], "pair_add_w_in")
    token = ex.push_chips("chip_in", [pair_sum], _pack_small(small_grads, SMALL_EARLY))

    dhn = _matmul(
        dproj, win_g, name="in_proj_bwd", grid=(s // tm, d // tn, n // 2), dims=NT_DIMS,
        a_spec=pl.BlockSpec((tm, 2 * c_in), lambda i, j, k: (i, k)),
        b_spec=pl.BlockSpec((2, tn, c_in), lambda i, j, k: (k, j, 0)),
        acc_shape=(tm, tn), out_shapes=[jax.ShapeDtypeStruct((s, d), BF16)],
        out_specs=[pl.BlockSpec((tm, tn), lambda i, j, k: (i, j))], dep=token, vmem_mib=56)[0]
    grad_x, dg_pre = _rmsnorm_bwd(dhn, x, small["g_pre"], dh_b, "pre_norm_bwd", True, False)
    extra = while_last_travels(token, delivered, dg_pre) if while_last_travels is not None else []
    pair_sums, landed, slabs = ex.chips_done("chip_in", [grad_x] + list(extra))
    delivered["w_in"] = (pair_sums[0], landed[0])
    small_grads["g_pre"] = dg_pre
    return loss, grad_x, small_grads, delivered, slabs, extra


SMALL_EARLY = ("w_s", "b_s", "ln_v_g", "ln_v_b", "g_q", "g_k", "rel_bias", "g_out_a", "g_out_b", "g_ple")
SMALL_LAST = ("g_pre",)
SMALL_NAMES = SMALL_LAST + SMALL_EARLY


def _pack_small(tree, names):
    parts = []
    for name in names:
        flat = tree[name].astype(F32).reshape(-1)
        pad = (-flat.shape[0]) % HEAD
        parts.append(jnp.pad(flat, (0, pad)) if pad else flat)
    slab = jnp.concatenate(parts).reshape(-1, HEAD)
    pad_rows = (-slab.shape[0]) % 8
    return jnp.pad(slab, ((0, pad_rows), (0, 0))) if pad_rows else slab


def _unpack_small(slab, like, names):
    flat = slab.reshape(-1)
    out, off = {}, 0
    for name in names:
        size = like[name].size
        out[name] = flat[off:off + size].reshape(like[name].shape)
        off += size + (-size) % HEAD
    return out


def _peer(k):
    x, y, c = (lax.axis_index(a) for a in AXES)
    if k == "first":
        px, py, pc = x ^ (1 - c), y ^ c, c
    elif k == "second":
        px, py, pc = x ^ c, y ^ (1 - c), c
    else:
        bits = ((k >> 2) & 1, (k >> 1) & 1, k & 1)
        px, py, pc = (1 - v if b else v for v, b in zip((x, y, c), bits))
    return (px, py, pc), 4 * px + 2 * py + pc


def _my_index():
    x, y, c = (lax.axis_index(a) for a in AXES)
    return 4 * x + 2 * y + c


N_CHIP = 4
HBM_SPEC = pl.BlockSpec(memory_space=pl.ANY)


def _remote(src, dst, send_sem, recv_sem, peer):
    return pltpu.make_async_remote_copy(src_ref=src, dst_ref=dst, send_sem=send_sem, recv_sem=recv_sem,
                                        device_id=peer, device_id_type=pl.DeviceIdType.MESH)


SEM_SPEC = pl.BlockSpec(memory_space=pltpu.SEMAPHORE)
HBM_ONLY = pl.BlockSpec(memory_space=pltpu.HBM)
DATAFLOW = pltpu.SideEffectType.DATAFLOW_SIDE_EFFECTING


def _comm_call(name, arrays, *, wait=None, start=None, after=()):
    n, n_after = len(arrays), len(after)

    def body(*refs):
        ins = refs[:n]
        pos = n
        if wait is not None:
            for cp in wait[2](ins, refs[pos], refs[pos + 1]):
                cp.wait()
            pos += 2
        outs = refs[pos + n_after:]
        if start is not None:
            for cp in start[1](ins, outs[0], outs[1]):
                cp.start()
        outs[-1][...] = jnp.zeros_like(outs[-1])

    operands = [pltpu.with_memory_space_constraint(a, pltpu.HBM) for a in arrays]
    in_specs = [HBM_ONLY] * n
    if wait is not None:
        operands += [wait[0], wait[1]]
        in_specs += [SEM_SPEC, SEM_SPEC]
    operands += list(after)
    in_specs += [HBM_SPEC] * n_after
    out_shape, out_specs = [], []
    if start is not None:
        out_shape += [pltpu.SemaphoreType.DMA((start[0],))] * 2
        out_specs += [SEM_SPEC, SEM_SPEC]
    first = len(out_shape)
    out_shape += [pltpu.HBM(a.shape, a.dtype) for a in arrays] + [jax.ShapeDtypeStruct((SUB, HEAD), F32)]
    out_specs += [HBM_ONLY] * n + [pl.BlockSpec(memory_space=pltpu.VMEM)]
    res = pl.pallas_call(
        body, name=name, out_shape=tuple(out_shape), in_specs=tuple(in_specs), out_specs=tuple(out_specs),
        input_output_aliases={i: first + i for i in range(n)},
        compiler_params=pltpu.CompilerParams(has_side_effects=DATAFLOW),
    )(*operands)
    sems = (res[0], res[1]) if start is not None else None
    return list(res[first:first + n]), sems, res[-1]


class _GradExchange:
    def __init__(self):
        x, y, c = (lax.axis_index(a) for a in AXES)
        self.core = c.astype(jnp.int32)
        self.chip = (2 * x + y).astype(jnp.int32)
        self.pending = {}

    def _pair_copies(self, n_arr):
        def make(refs, send_sems, recv_sems):
            sibling, _ = _peer(1)
            other = 1 - lax.axis_index("c")
            srcs, lands = refs[:n_arr], refs[n_arr:]
            pick = lambda ref, ch: ref.at[ch, other] if len(ref.shape) == 4 else ref.at[ch]
            return [_remote(pick(srcs[a], ch), lands[a].at[ch], send_sems.at[a * N_CHIP + ch],
                            recv_sems.at[a * N_CHIP + ch], sibling)
                    for a in range(n_arr) for ch in range(N_CHIP)]
        return make

    def _chip_copies(self, n_arr, with_slab):
        def make(refs, send_sems, recv_sems):
            x, y = lax.axis_index("x"), lax.axis_index("y")
            my_chip = 2 * x + y
            srcs, lands = refs[:n_arr], refs[n_arr:2 * n_arr]
            copies = []
            for j, k in enumerate((2, 4, 6)):
                peer, peer_idx = _peer(k)
                for a in range(n_arr):
                    copies.append(_remote(srcs[a].at[peer_idx // 2], lands[a].at[my_chip],
                                          send_sems.at[3 * a + j], recv_sems.at[3 * a + j], peer))
            if with_slab:
                slab, slab_land = refs[2 * n_arr], refs[2 * n_arr + 1]
                for k in range(1, N_DEV):
                    peer, _ = _peer(k)
                    copies.append(_remote(slab, slab_land.at[_my_index()], send_sems.at[3 * n_arr + k - 1],
                                          recv_sems.at[3 * n_arr + k - 1], peer))
            return copies
        return make

    def push_pairs(self, tag, for_sibling):
        n_arr = len(for_sibling)
        lands = [lax.empty((N_CHIP,) + a.shape[-2:], a.dtype) for a in for_sibling]
        make = self._pair_copies(n_arr)
        arrays, sems, token = _comm_call(tag + "_start", list(for_sibling) + lands, start=(n_arr * N_CHIP, make))
        self.pending[tag] = (arrays, sems, make, n_arr)
        return token

    def pairs_done(self, tag, after):
        arrays, sems, make, n_arr = self.pending.pop(tag)
        arrays, _, _ = _comm_call(tag + "_wait", arrays, wait=(sems[0], sems[1], make), after=after)
        return arrays[:n_arr], arrays[n_arr:]

    def push_chips(self, tag, pair_sums, slab=None):
        n_arr = len(pair_sums)
        arrays = list(pair_sums) + [lax.empty(a.shape, a.dtype) for a in pair_sums]
        n_copies = 3 * n_arr
        if slab is not None:
            arrays += [slab, lax.empty((N_DEV,) + slab.shape, slab.dtype)]
            n_copies += N_DEV - 1
        make = self._chip_copies(n_arr, slab is not None)
        arrays, sems, token = _comm_call(tag + "_start", arrays, start=(n_copies, make))
        self.pending[tag] = (arrays, sems, make, n_arr)
        return token

    def chips_done(self, tag, after):
        arrays, sems, make, n_arr = self.pending.pop(tag)
        arrays, _, _ = _comm_call(tag + "_wait", arrays, wait=(sems[0], sems[1], make), after=after)
        return arrays[:n_arr], arrays[n_arr:2 * n_arr], arrays[2 * n_arr:]


def _cast_place(w, name):
    r, c = w.shape
    tr = r if r * c <= MIB else 1 << ((MIB // c).bit_length() - 1)
    assert r % tr == 0

    def body(me_ref, w_ref, o_ref):
        o_ref[...] = w_ref[...].astype(BF16)

    return pl.pallas_call(
        body, name=name, out_shape=jax.ShapeDtypeStruct((N_DEV, r, c), BF16),
        grid_spec=pltpu.PrefetchScalarGridSpec(
            num_scalar_prefetch=1, grid=(r // tr,),
            in_specs=[pl.BlockSpec((tr, c), lambda i, me_ref: (i, 0))],
            out_specs=pl.BlockSpec((None, tr, c), lambda i, me_ref: (me_ref[0], i, 0))),
        compiler_params=_params(("arbitrary",), 40),
    )(_my_index().astype(jnp.int32).reshape(1), w)


class _WeightGather:
    CHIPS = (2, 4, 6)

    def __init__(self, buffers):
        self.buffers = dict(buffers)
        self.pending = {}
        self.me = _my_index().astype(jnp.int32)

    def _own_slot_to(self, peers):
        def make(refs, send_sems, recv_sems):
            me = _my_index()
            return [_remote(ref.at[me], ref.at[me], send_sems.at[len(peers) * a + j],
                            recv_sems.at[len(peers) * a + j], _peer(k)[0])
                    for a, ref in enumerate(refs) for j, k in enumerate(peers)]
        return make

    def _forward_from(self, chips):
        def make(refs, send_sems, recv_sems):
            sibling, _ = _peer(1)
            copies = []
            for a, ref in enumerate(refs):
                for j, k in enumerate(chips):
                    slot = ref.at[_peer(k)[1]]
                    copies.append(_remote(slot, slot, send_sems.at[len(chips) * a + j],
                                          recv_sems.at[len(chips) * a + j], sibling))
            return copies
        return make

    def _run(self, call, names, **kw):
        arrays, sems, token = _comm_call(call, [self.buffers[k] for k in names], **kw)
        self.buffers.update(zip(names, arrays))
        return sems, token

    def start(self, tag, names, peers, after=()):
        make = self._own_slot_to(peers)
        sems, token = self._run(tag + "_start", names, start=(len(names) * len(peers), make), after=after)
        self.pending[tag] = (names, sems, make, peers)
        return token

    @staticmethod
    def _relay(refs, send_sems, recv_sems):
        x, y, c = (lax.axis_index(a) for a in AXES)
        peer = (x ^ (1 - c), y ^ c, c)
        slot = _my_index() ^ (2 + 2 * c)
        return [_remote(ref.at[slot], ref.at[slot], send_sems.at[a], recv_sems.at[a], peer)
                for a, ref in enumerate(refs)]

    def relay_start(self, tag, names, after=()):
        sems, token = self._run(tag + "_start", names, start=(len(names), self._relay), after=after)
        self.pending[tag] = (names, sems, self._relay, (6,))
        return token

    def arrived(self, tag, after):
        names, sems, make, _ = self.pending.pop(tag)
        self._run(tag + "_wait", names, wait=(sems[0], sems[1], make), after=after)

    def forward(self, tag, after):
        names, sems, make, peers = self.pending.pop(tag)
        chips = tuple(k for k in peers if k != 1)
        onward = self._forward_from(chips)
        new_sems, token = self._run(tag + "_forward", names, wait=(sems[0], sems[1], make),
                                    start=(len(chips) * len(names), onward), after=after)
        self.pending[tag + "/fwd"] = (names, new_sems, onward)
        return token

    def forwarded(self, tag, after):
        names, sems, make = self.pending.pop(tag + "/fwd")
        self._run(tag + "_done", names, wait=(sems[0], sems[1], make), after=after)


def _pair_add(mine, theirs, name, core=None):
    _, r, c_dim = theirs.shape
    tr = r if r * c_dim <= MIB else 1 << ((MIB // c_dim).bit_length() - 1)
    assert r % tr == 0
    stride = 1 if core is None else 2
    offset = jnp.zeros((1,), jnp.int32) if core is None else core.reshape(1)

    def body(off_ref, a_ref, b_ref, o_ref):
        o_ref[...] = (a_ref[...].astype(F32) + b_ref[...].astype(F32)).astype(BF16)

    blk = (None, tr, c_dim)
    return pl.pallas_call(
        body, name=name, out_shape=jax.ShapeDtypeStruct(theirs.shape, BF16),
        grid_spec=pltpu.PrefetchScalarGridSpec(
            num_scalar_prefetch=1, grid=(N_CHIP, r // tr),
            in_specs=[pl.BlockSpec(blk, lambda ch, i, off_ref: (stride * ch + off_ref[0], i, 0)),
                      pl.BlockSpec(blk, lambda ch, i, off_ref: (ch, i, 0))],
            out_specs=pl.BlockSpec(blk, lambda ch, i, off_ref: (ch, i, 0))),
        compiler_params=_params(("arbitrary", "arbitrary"), 40),
    )(offset, mine, theirs)


def _slab_exchange(slab, name):
    def body(slab_in, slab_out, send_sems, recv_sems, local_sem):
        me = _my_index()
        local = pltpu.make_async_copy(slab_in, slab_out.at[me], local_sem)
        local.start()
        sends = []
        for k in range(1, N_DEV):
            peer, _ = _peer(k)
            sends.append(_remote(slab_in, slab_out.at[me], send_sems.at[k - 1], recv_sems.at[k - 1], peer))
        for cp in sends:
            cp.start()
        for k in range(1, N_DEV):
            peer, peer_idx = _peer(k)
            slot = slab_out.at[peer_idx]
            _remote(slot, slot, send_sems.at[k - 1], recv_sems.at[k - 1], peer).wait_recv()
        for cp in sends:
            cp.wait_send()
        local.wait()

    return pl.pallas_call(
        body, name=name, out_shape=jax.ShapeDtypeStruct((N_DEV,) + slab.shape, slab.dtype),
        in_specs=[HBM_SPEC], out_specs=HBM_SPEC,
        scratch_shapes=[pltpu.SemaphoreType.DMA((N_DEV - 1,)), pltpu.SemaphoreType.DMA((N_DEV - 1,)),
                        pltpu.SemaphoreType.DMA],
        compiler_params=pltpu.CompilerParams(has_side_effects=True),
    )(slab)


def _adamw_math(w, g, m, v):
    m = ADAM_B1 * m + (1.0 - ADAM_B1) * g
    v = ADAM_B2 * v + (1.0 - ADAM_B2) * (g * g)
    m_hat = m / (1.0 - ADAM_B1 ** ADAM_STEP)
    v_hat = v / (1.0 - ADAM_B2 ** ADAM_STEP)
    delta = -ADAM_LR * (m_hat / (jnp.sqrt(v_hat) + ADAM_EPS) + ADAM_WD * w)
    return delta, m, v


def _adamw(parts, own, place, w, m, v, name, dep=None):
    n_parts = parts.shape[0]
    r, c = w.shape
    budget = 280 * 1024
    tr = r if r * c <= budget else 1 << ((budget // c).bit_length() - 1)
    assert r % tr == 0

    def body(place_ref, p_ref, own_ref, w_ref, m_ref, v_ref, g_ref, d_ref, nm_ref, nv_ref):
        mine = own_ref[...].astype(F32)
        g = None
        for i in range(n_parts):
            term = jnp.where(place_ref[0] == i, mine, p_ref[i].astype(F32))
            g = term if g is None else g + term
        delta, nm, nv = _adamw_math(w_ref[...], g, m_ref[...], v_ref[...])
        g_ref[...] = g
        d_ref[...] = delta
        nm_ref[...] = nm
        nv_ref[...] = nv

    blk = pl.BlockSpec((tr, c), lambda i, place_ref: (i, 0))
    shape = jax.ShapeDtypeStruct((r, c), F32)
    in_specs = [pl.BlockSpec((n_parts, tr, c), lambda i, place_ref: (0, i, 0)),
                pl.BlockSpec((None, tr, c), lambda i, place_ref: (place_ref[1], i, 0)), blk, blk, blk]
    operands = [parts, own, w, m, v]
    if dep is not None:
        body = _drop_arg(body, 1 + len(operands))
        in_specs.append(pl.BlockSpec((SUB, HEAD), lambda i, place_ref: (0, 0)))
        operands.append(dep)
    return pl.pallas_call(
        body, name=name, out_shape=[shape] * 4,
        grid_spec=pltpu.PrefetchScalarGridSpec(
            num_scalar_prefetch=1, grid=(r // tr,), in_specs=in_specs, out_specs=[blk] * 4),
        compiler_params=_params(("arbitrary",), 48),
    )(place, *operands)


def kernel(x, p, g_pre, w_in, w_s, b_s, ln_v_g, ln_v_b, g_q, g_k, rel_bias, g_out_a, g_out_b, w_out, g_ple, w_ple_gate, w_ple_up, loss_target, m_g_pre, m_w_in, m_w_s, m_b_s, m_ln_v_g, m_ln_v_b, m_g_q, m_g_k, m_rel_bias, m_g_out_a, m_g_out_b, m_w_out, m_g_ple, m_w_ple_gate, m_w_ple_up, v_g_pre, v_w_in, v_w_s, v_b_s, v_ln_v_g, v_ln_v_b, v_g_q, v_g_k, v_rel_bias, v_g_out_a, v_g_out_b, v_w_out, v_g_ple, v_w_ple_gate, v_w_ple_up):
    args = dict(locals())
    small = {"g_pre": g_pre, "w_s": w_s[0], "b_s": b_s[0], "ln_v_g": ln_v_g, "ln_v_b": ln_v_b, "g_q": g_q,
             "g_k": g_k, "rel_bias": rel_bias, "g_out_a": g_out_a, "g_out_b": g_out_b, "g_ple": g_ple}
    big_names = ("w_in", "w_out", "w_ple_gate", "w_ple_up")
    big = {k: args[k][0] for k in big_names}

    wg = _WeightGather({k: _cast_place(big[k], "place_" + k) for k in big_names})
    ex = _GradExchange()
    results = {}

    def big_adamw(k, delivered, dep=None):
        mine, theirs = delivered[k]
        place = jnp.stack([ex.chip, ex.chip])
        return _adamw(theirs, mine, place, big[k], args["m_" + k][0], args["v_" + k][0], "adamw_" + k, dep=dep)

    squeeze = lambda t: {k: (t[k][0] if k in ("w_s", "b_s") else t[k]) for k in SMALL_NAMES}
    small_m = squeeze({k: args["m_" + k] for k in SMALL_NAMES})
    small_v = squeeze({k: args["v_" + k] for k in SMALL_NAMES})
    slab_place = jnp.stack([_my_index().astype(jnp.int32), jnp.zeros((), jnp.int32)])

    def small_adamw(names_, parts, own, call_name):
        packed = _adamw(parts, own[None], slab_place, _pack_small(small, names_), _pack_small(small_m, names_),
                        _pack_small(small_v, names_), call_name)
        for idx in range(4):
            tree = _unpack_small(packed[idx], small, names_)
            for k in names_:
                results.setdefault(k, [None] * 4)[idx] = tree[k].reshape(args[k].shape)
        return packed[0]

    def while_last_travels(token, delivered, dg_pre):
        done = []
        for k in big_names[1:]:
            results[k] = big_adamw(k, delivered, dep=token)
            done.append(results[k][0])
        last_slab = _pack_small({"g_pre": dg_pre}, SMALL_LAST)
        done.append(small_adamw(SMALL_LAST, _slab_exchange(last_slab, "last_exchange"), last_slab, "adamw_last"))
        return done

    loss, grad_x, small_parts, delivered, slabs, _ = _local_step(
        x[0], p[0, 0], loss_target[0], small, wg, ex, while_last_travels)
    results["w_in"] = big_adamw("w_in", delivered)
    for k in big_names:
        results[k] = [t[None] for t in results[k]]
    small_adamw(SMALL_EARLY, slabs[1], slabs[0], "adamw_small")

    names = ("g_pre", "w_in", "w_s", "b_s", "ln_v_g", "ln_v_b", "g_q", "g_k", "rel_bias", "g_out_a", "g_out_b",
             "w_out", "g_ple", "w_ple_gate", "w_ple_up")
    total = lax.psum(loss, AXES)
    out = [total, grad_x[None]]
    for idx in range(4):
        out += [results[k][idx] for k in names]
    return tuple(out)
```

```python
import math

import numpy as np
import jax
import jax.numpy as jnp
from jax import lax
from jax.experimental import pallas as pl
from jax.experimental.pallas import tpu as pltpu

F32 = jnp.float32
BF16 = jnp.bfloat16
EPS = 1e-6
NEG_INF = -1e30
HEAD = 128
DILATIONS = (1, 4, 16)
NUM_BUCKETS = 32
MAX_DISTANCE = 2048
N_SEG = 7
ADAM_LR = 0.001
ADAM_B1 = 0.9
ADAM_B2 = 0.999
ADAM_EPS = 1e-08
ADAM_WD = 0.01
ADAM_STEP = 10
AXES = ("x", "y", "c")
N_DEV = 8
MIB = 1 << 20

SUB = 8

CHUNK_ORDER = np.array([16 * (r % SUB) + r // SUB for r in range(HEAD)])
BLOCK_ORDER = {
    1: CHUNK_ORDER,
    4: np.array([32 * (r // 32) + 4 * (r % SUB) + (r // SUB) % 4 for r in range(HEAD)]),
    16: np.arange(HEAD),
}

NT_DIMS = (((1,), (1,)), ((), ()))
TN_DIMS = (((0,), (0,)), ((), ()))
NN_DIMS = (((1,), (0,)), ((), ()))


def _params(semantics, vmem_mib):
    return pltpu.CompilerParams(dimension_semantics=semantics, vmem_limit_bytes=vmem_mib * MIB)


def _gelu(a):
    return 0.5 * a * (1.0 + lax.erf(a * (2.0 ** -0.5)))


def _gelu_and_grad(a):
    cdf = 0.5 * (1.0 + lax.erf(a * (2.0 ** -0.5)))
    return a * cdf, cdf + a * jnp.exp(-0.5 * a * a) * ((2.0 * math.pi) ** -0.5)


def _silu_and_grad(a):
    s = jax.nn.sigmoid(a)
    return a * s, s * (1.0 + a * (1.0 - s))


def _rms(v):
    return lax.rsqrt(jnp.mean(v * v, axis=-1, keepdims=True) + EPS)


def _rms_bwd(dy, v, r, g):
    gy = dy * g
    return r * gy - v * (r * r * r) * jnp.mean(gy * v, axis=-1, keepdims=True)


def _dot(a, b, dims=NN_DIMS):
    return lax.dot_general(a, b, dims, preferred_element_type=F32)


def _lane_pick(cols, width):
    rows = cols[0].shape[0]
    lane = lax.broadcasted_iota(jnp.int32, (rows, width), 1)
    out = jnp.zeros((rows, width), F32)
    for h, col in enumerate(cols):
        out = jnp.where(lane == h, col, out)
    return out


def _chunk_perm():
    return jnp.asarray(np.eye(HEAD, dtype=np.float32)[CHUNK_ORDER], BF16)


def _unpermute_f32(p, v):
    hi = v.astype(BF16)
    rest = v - hi.astype(F32)
    mid = rest.astype(BF16)
    lo = (rest - mid.astype(F32)).astype(BF16)
    return _dot(p, hi, TN_DIMS) + _dot(p, mid, TN_DIMS) + _dot(p, lo, TN_DIMS)


def _rmsnorm_fwd(x, g, name, permute, dep=None):
    s, d = x.shape
    tm = HEAD

    def body(x_ref, g_ref, p_ref, o_ref):
        v = x_ref[...]
        out = (v * _rms(v) * g_ref[...]).astype(BF16)
        if permute:
            out = _dot(p_ref[...], out).astype(BF16)
        o_ref[...] = out

    in_specs = [pl.BlockSpec((tm, d), lambda i: (i, 0)), pl.BlockSpec((1, d), lambda i: (0, 0)),
                pl.BlockSpec((HEAD, HEAD), lambda i: (0, 0))]
    operands = [x, g, _chunk_perm()]
    if dep is not None:
        body = _drop_arg(body, len(operands))
        in_specs.append(DEP_SPEC)
        operands.append(dep)
    return pl.pallas_call(
        body, name=name, grid=(s // tm,),
        out_shape=jax.ShapeDtypeStruct((s, d), BF16), in_specs=in_specs,
        out_specs=pl.BlockSpec((tm, d), lambda i: (i, 0)),
        compiler_params=_params(("arbitrary",), 40),
    )(*operands)


def _rmsnorm_bwd(dy, v, g, res, name, dy_permuted, with_bf16):
    s, d = v.shape
    tm = HEAD
    perm = _chunk_perm()

    def body(dy_ref, v_ref, g_ref, res_ref, p_ref, *outs):
        dg_ref = outs[-1]

        @pl.when(pl.program_id(0) == 0)
        def _():
            dg_ref[...] = jnp.zeros_like(dg_ref)

        vv, dyv = v_ref[...], dy_ref[...]
        if dy_permuted:
            dyv = _unpermute_f32(p_ref[...], dyv) if dyv.dtype == F32 else _dot(p_ref[...], dyv, TN_DIMS)
        dyv = dyv.astype(F32)
        r = _rms(vv)
        dx = res_ref[...].astype(F32) + _rms_bwd(dyv, vv, r, g_ref[...])
        if with_bf16:
            dxb = dx.astype(BF16)
            outs[0][...] = dxb
            outs[1][...] = _dot(p_ref[...], dxb).astype(BF16)
        else:
            outs[0][...] = dx
        dg_ref[...] += jnp.sum(dyv * vv * r, axis=0, keepdims=True)

    row = pl.BlockSpec((tm, d), lambda i: (i, 0))
    vec = pl.BlockSpec((1, d), lambda i: (0, 0))
    if with_bf16:
        shapes = [jax.ShapeDtypeStruct((s, d), BF16)] * 2
        specs = [row, row]
    else:
        shapes = [jax.ShapeDtypeStruct((s, d), F32)]
        specs = [row]
    shapes.append(jax.ShapeDtypeStruct((1, d), F32))
    specs.append(vec)
    return pl.pallas_call(
        body, name=name, grid=(s // tm,), out_shape=shapes,
        in_specs=[row, row, vec, row, pl.BlockSpec((HEAD, HEAD), lambda i: (0, 0))], out_specs=specs,
        compiler_params=_params(("arbitrary",), 40),
    )(dy, v, g, res, perm)


DEP_SPEC = pl.BlockSpec((SUB, HEAD), lambda *_: (0, 0))


def _drop_arg(body, pos):
    return lambda *refs: body(*refs[:pos], *refs[pos + 1:])


def _matmul(a, b, *, name, grid, a_spec, b_spec, dims, acc_shape, out_shapes, out_specs,
            extra=(), extra_specs=(), epilogue=None, vmem_mib=48, dep=None, prefetch=None, carry=None):
    nk = grid[2]
    n_user = len(extra)
    for unread, spec in ((dep, DEP_SPEC), (carry, HBM_SPEC)):
        if unread is not None:
            extra, extra_specs = tuple(extra) + (unread,), tuple(extra_specs) + (spec,)
    n_extra, n_out = len(extra), len(out_shapes)
    n_pre = 0 if prefetch is None else 1
    aliases = {} if carry is None else {n_pre + 2 + n_extra - 1: 0}

    def body(*refs):
        refs = refs[n_pre:]
        a_ref, b_ref = refs[0], refs[1]
        ex = refs[2:2 + n_user]
        outs = refs[2 + n_extra:2 + n_extra + n_out]
        def lhs():
            av = a_ref[...]
            return av if av.dtype == BF16 else av.astype(BF16)

        if nk == 1 and epilogue is None:
            outs[0][...] = _dot(lhs(), b_ref[...], dims).astype(outs[0].dtype)
            return
        acc = refs[-1]
        k = pl.program_id(2)

        @pl.when(k == 0)
        def _():
            acc[...] = jnp.zeros_like(acc)

        av = lhs()
        if len(b_ref.shape) == 3:
            span = b_ref.shape[2]
            acc[...] += sum(_dot(av[:, g * span:(g + 1) * span], b_ref[g], dims) for g in range(b_ref.shape[0]))
        else:
            acc[...] += _dot(av, b_ref[...], dims)

        @pl.when(k == nk - 1)
        def _():
            if epilogue is None:
                outs[0][...] = acc[...].astype(outs[0].dtype)
            else:
                epilogue(acc, ex, outs)

    direct = nk == 1 and epilogue is None
    scratch = [] if direct else [pltpu.VMEM(acc_shape, F32)]
    params = _params(("parallel", "parallel", "arbitrary"), vmem_mib)
    if prefetch is None:
        return pl.pallas_call(
            body, name=name, grid=grid, out_shape=list(out_shapes),
            in_specs=[a_spec, b_spec, *extra_specs], out_specs=list(out_specs),
            scratch_shapes=scratch, compiler_params=params, input_output_aliases=aliases,
        )(a, b, *extra)
    return pl.pallas_call(
        body, name=name, out_shape=list(out_shapes),
        grid_spec=pltpu.PrefetchScalarGridSpec(
            num_scalar_prefetch=1, grid=grid, in_specs=[a_spec, b_spec, *extra_specs],
            out_specs=list(out_specs), scratch_shapes=scratch),
        compiler_params=params, input_output_aliases=aliases,
    )(prefetch, a, b, *extra)


def _tile(n, want):
    t = min(n, want)
    while n % t:
        t //= 2
    return t


def _rel_buckets(dil):
    order = BLOCK_ORDER[dil]
    qi = jnp.asarray(HEAD + order)
    kj = jnp.asarray(np.concatenate([order, HEAD + order]))
    delta = qi[:, None] - kj[None, :]
    band = (delta >= 0) & (delta <= HEAD)
    dist = jnp.clip(delta, 0, None) * dil
    max_exact = NUM_BUCKETS // 2
    dd = jnp.maximum(dist, 1).astype(F32)
    large = max_exact + (jnp.log(dd / max_exact) / math.log(MAX_DISTANCE / max_exact)
                         * (NUM_BUCKETS - max_exact)).astype(jnp.int32)
    large = jnp.minimum(large, NUM_BUCKETS - 1)
    bucket = jnp.where(dist < max_exact, dist, large)
    return jnp.where(band, bucket, -1).astype(jnp.int32)


def _bias_build(rel_bias, buckets, n_heads):
    nd = buckets.shape[0]

    def body(rb_ref, bk_ref, o_ref):
        for c in range(nd):
            def per_head(h, carry, c=c):
                bk = bk_ref[c]
                acc = jnp.where(bk < 0, NEG_INF, 0.0).astype(F32)
                for b in range(NUM_BUCKETS):
                    acc = jnp.where(bk == b, rb_ref[b, h], acc)
                o_ref[c, h] = acc
                return carry

            lax.fori_loop(0, n_heads, per_head, 0)

    return pl.pallas_call(
        body, name="bias_build",
        out_shape=jax.ShapeDtypeStruct((nd, n_heads, HEAD, 2 * HEAD), F32),
        in_specs=[pl.BlockSpec(memory_space=pltpu.SMEM), pl.BlockSpec(memory_space=pltpu.VMEM)],
        out_specs=pl.BlockSpec(memory_space=pltpu.VMEM),
    )(rel_bias, buckets)


def _bias_grad(ds_all, buckets, n_heads):
    nd = buckets.shape[0]
    pairs = HEAD * 2 * HEAD

    def body(ds_ref, bk_ref, o_ref):
        rows = lax.broadcasted_iota(jnp.int32, (NUM_BUCKETS, pairs), 0)
        tot = jnp.zeros((n_heads, NUM_BUCKETS), F32)
        for c in range(nd):
            onehot = (rows == bk_ref[c]).astype(BF16)
            ds = ds_ref[c]
            hi = ds.astype(BF16)
            lo = (ds - hi.astype(F32)).astype(BF16)
            tot = tot + _dot(hi, onehot, NT_DIMS) + _dot(lo, onehot, NT_DIMS)
        o_ref[...] = tot

    out = pl.pallas_call(
        body, name="bias_grad",
        out_shape=jax.ShapeDtypeStruct((n_heads, NUM_BUCKETS), F32),
        in_specs=[pl.BlockSpec(memory_space=pltpu.VMEM), pl.BlockSpec(memory_space=pltpu.VMEM)],
        out_specs=pl.BlockSpec(memory_space=pltpu.VMEM),
        compiler_params=pltpu.CompilerParams(vmem_limit_bytes=40 * MIB),
    )(ds_all.reshape(nd, n_heads, pairs), buckets.reshape(nd, 1, pairs))
    return out.T


def _qkv_prep(proj, g_q, g_k, w, dep=None):
    s = proj.shape[0]
    n_heads = w // HEAD
    tm = HEAD

    def body(q_ref, k_ref, gq_ref, gk_ref, qn_ref, kn_ref):
        gq = gq_ref[...] * (HEAD ** -0.5)
        gk = gk_ref[...]
        for h in range(n_heads):
            sl = slice(h * HEAD, (h + 1) * HEAD)
            q = q_ref[:, sl]
            k = k_ref[:, sl]
            qn_ref[:, sl] = q * _rms(q) * gq
            kn_ref[:, sl] = k * _rms(k) * gk

    seg = lambda j: pl.BlockSpec((tm, w), lambda i, j=j: (i, j))
    vec = pl.BlockSpec((1, HEAD), lambda i: (0, 0))
    out = pl.BlockSpec((tm, w), lambda i: (i, 0))
    in_specs = [seg(3), seg(4), vec, vec]
    operands = [proj, proj, g_q, g_k]
    if dep is not None:
        body = _drop_arg(body, len(operands))
        in_specs.append(DEP_SPEC)
        operands.append(dep)
    return pl.pallas_call(
        body, name="qkv_prep", grid=(s // tm,),
        out_shape=[jax.ShapeDtypeStruct((s, w), F32)] * 2,
        in_specs=in_specs, out_specs=[out, out],
        compiler_params=_params(("arbitrary",), 40),
    )(*operands)


class _BlockView:
    def __init__(self, s, dil):
        assert s % (HEAD * dil) == 0 and dil in BLOCK_ORDER
        self.nb = s // (HEAD * dil)
        if dil == 1:
            self.lead, self.block = (s,), (HEAD,)
            self.index = lambda r, n: (n,)
        elif dil == 4:
            self.lead, self.block = (s // 512, 4, 4, 4, SUB), (None, 4, 4, None, SUB)
            self.index = lambda r, n: (n, 0, 0, r, 0)
        else:
            self.lead, self.block = (s // 2048, 16, 16, SUB), (None, 16, None, SUB)
            self.index = lambda r, n: (n, 0, r, 0)

    def view(self, t):
        return t.reshape(self.lead + (t.shape[-1],))

    def spec(self, width, block_of, column=0):
        return pl.BlockSpec(self.block + (width,), lambda r, n: self.index(r, block_of(r, n)) + (column,))


def _rows(ref, lanes=slice(None)):
    v = ref[(slice(None),) * (len(ref.shape) - 1) + (lanes,)]
    return v.reshape(HEAD, v.shape[-1])


def _set_rows(ref, lanes, value):
    ref[(slice(None),) * (len(ref.shape) - 1) + (lanes,)] = value.reshape(ref.shape[:-1] + (value.shape[-1],))


V_SEGMENT = 5


def _attn_fwd(qn, kn, proj, bias, dil, name, dep=None):
    s, w = qn.shape
    n_heads = w // HEAD
    bv = _BlockView(s, dil)

    def body(q_ref, kc_ref, vc_ref, bias_ref, o_ref, lse_ref, s_scr, e_scr, lse_scr, inv_scr, k_prev, v_prev):
        n = pl.program_id(1)
        heads = [slice(h * HEAD, (h + 1) * HEAD) for h in range(n_heads)]
        lse_scr[...] = jnp.zeros_like(lse_scr)

        @pl.when(n == 0)
        def _():
            k_prev[...] = jnp.zeros_like(k_prev)
            v_prev[...] = jnp.zeros_like(v_prev)

        for h, sl in enumerate(heads):
            q = _rows(q_ref, sl).astype(BF16)
            s_p = _dot(q, k_prev[:, sl], NT_DIMS) + bias_ref[h, :, :HEAD]
            s_scr[h, :, :HEAD] = jnp.where(n > 0, s_p, NEG_INF)
            s_scr[h, :, HEAD:] = _dot(q, _rows(kc_ref, sl).astype(BF16), NT_DIMS) + bias_ref[h, :, HEAD:]
        for h in range(n_heads):
            sc = s_scr[h]
            m = jnp.max(sc, axis=-1, keepdims=True)
            e = jnp.exp(sc - m)
            den = jnp.sum(e, axis=-1, keepdims=True)
            e_scr[h] = e.astype(BF16)
            lse_scr[:, h:h + 1] = m + jnp.log(den)
            inv_scr[:, h:h + 1] = 1.0 / den
        for h, sl in enumerate(heads):
            v_cur = _rows(vc_ref, sl).astype(BF16)
            o = _dot(e_scr[h, :, :HEAD], v_prev[:, sl]) + _dot(e_scr[h, :, HEAD:], v_cur)
            _set_rows(o_ref, sl, o * inv_scr[:, h:h + 1])
            v_prev[:, sl] = v_cur
            k_prev[:, sl] = _rows(kc_ref, sl).astype(BF16)
        _set_rows(lse_ref, slice(None), lse_scr[...])

    cur = bv.spec(w, lambda r, n: n)
    in_specs = [cur, cur, bv.spec(w, lambda r, n: n, V_SEGMENT),
                pl.BlockSpec((n_heads, HEAD, 2 * HEAD), lambda r, n: (0, 0, 0))]
    operands = [bv.view(qn), bv.view(kn), bv.view(proj), bias]
    if dep is not None:
        body = _drop_arg(body, len(operands))
        in_specs.append(DEP_SPEC)
        operands.append(dep)
    o, lse = pl.pallas_call(
        body, name=name, grid=(dil, bv.nb),
        out_shape=[jax.ShapeDtypeStruct(bv.lead + (w,), F32), jax.ShapeDtypeStruct(bv.lead + (HEAD,), F32)],
        in_specs=in_specs,
        out_specs=[cur, bv.spec(HEAD, lambda r, n: n)],
        scratch_shapes=[pltpu.VMEM((n_heads, HEAD, 2 * HEAD), F32), pltpu.VMEM((n_heads, HEAD, 2 * HEAD), BF16),
                        pltpu.VMEM((HEAD, HEAD), F32), pltpu.VMEM((HEAD, HEAD), F32),
                        pltpu.VMEM((HEAD, w), BF16), pltpu.VMEM((HEAD, w), BF16)],
        compiler_params=_params(("arbitrary", "arbitrary"), 48),
    )(*operands)
    return o.reshape(s, w), lse.reshape(s, HEAD)


def _attn_bwd(qn, kn, proj, dyb, lse, delta, bias, dil, name, running=None, dep=None):
    s, w = qn.shape
    n_heads = w // HEAD
    bv = _BlockView(s, dil)
    nb = bv.nb

    n_run = 0 if running is None else 3

    def body(q_ref, kc_ref, kp_ref, vc_ref, vp_ref, dy_ref, lse_ref, dl_ref, bias_ref, *rest):
        so_far = rest[:n_run]
        dq_ref, dk_ref, dv_ref, ds_ref, carry_k, carry_v, s_scr, dp_scr, p_scr, dsb_scr, k_cur, v_cur = rest[n_run:]
        base = (lambda i, sl: _rows(so_far[i], sl)) if n_run else (lambda i, sl: 0.0)
        r = pl.program_id(0)
        step = pl.program_id(1)
        blk = nb - 1 - step

        @pl.when((r == 0) & (step == 0))
        def _():
            ds_ref[...] = jnp.zeros_like(ds_ref)

        @pl.when(step == 0)
        def _():
            carry_k[...] = jnp.zeros_like(carry_k)
            carry_v[...] = jnp.zeros_like(carry_v)
            k_cur[...] = _rows(kc_ref).astype(BF16)
            v_cur[...] = _rows(vc_ref).astype(BF16)

        heads = [slice(h * HEAD, (h + 1) * HEAD) for h in range(n_heads)]
        tots = _rows(lse_ref)
        dls = _rows(dl_ref)
        for h, sl in enumerate(heads):
            q, dy = _rows(q_ref, sl).astype(BF16), _rows(dy_ref, sl).astype(BF16)
            kp, kc = _rows(kp_ref, sl).astype(BF16), k_cur[:, sl]
            vp, vc = _rows(vp_ref, sl).astype(BF16), v_cur[:, sl]
            s_p = _dot(q, kp, NT_DIMS) + bias_ref[h, :, :HEAD]
            s_scr[h, :, :HEAD] = jnp.where(blk > 0, s_p, NEG_INF)
            s_scr[h, :, HEAD:] = _dot(q, kc, NT_DIMS) + bias_ref[h, :, HEAD:]
            dp_scr[h, :, :HEAD] = _dot(dy, vp, NT_DIMS)
            dp_scr[h, :, HEAD:] = _dot(dy, vc, NT_DIMS)
        for h in range(n_heads):
            prob = jnp.exp(s_scr[h] - tots[:, h:h + 1])
            ds = prob * (dp_scr[h] - dls[:, h:h + 1])
            ds_ref[h] += ds
            p_scr[h] = prob.astype(BF16)
            dsb_scr[h] = ds.astype(BF16)
        for h, sl in enumerate(heads):
            q, dy = _rows(q_ref, sl).astype(BF16), _rows(dy_ref, sl).astype(BF16)
            kp, kc = _rows(kp_ref, sl).astype(BF16), k_cur[:, sl]
            ds_pb, ds_cb = dsb_scr[h, :, :HEAD], dsb_scr[h, :, HEAD:]
            _set_rows(dq_ref, sl, _dot(ds_pb, kp) + _dot(ds_cb, kc) + base(0, sl))
            _set_rows(dk_ref, sl, _dot(ds_cb, q, TN_DIMS) + carry_k[:, sl] + base(1, sl))
            carry_k[:, sl] = _dot(ds_pb, q, TN_DIMS)
            _set_rows(dv_ref, sl, _dot(p_scr[h, :, HEAD:], dy, TN_DIMS) + carry_v[:, sl] + base(2, sl))
            carry_v[:, sl] = _dot(p_scr[h, :, :HEAD], dy, TN_DIMS)
            k_cur[:, sl] = kp
            v_cur[:, sl] = _rows(vp_ref, sl).astype(BF16)

    cur = bv.spec(w, lambda r, n: nb - 1 - n)
    last = bv.spec(w, lambda r, n: nb - 1)
    prev = bv.spec(w, lambda r, n: jnp.maximum(nb - 2 - n, 0))
    stat = bv.spec(HEAD, lambda r, n: nb - 1 - n)
    whole = pl.BlockSpec((n_heads, HEAD, 2 * HEAD), lambda r, n: (0, 0, 0))
    big = jax.ShapeDtypeStruct(bv.lead + (w,), F32)
    v_last = bv.spec(w, lambda r, n: nb - 1, V_SEGMENT)
    v_prev = bv.spec(w, lambda r, n: jnp.maximum(nb - 2 - n, 0), V_SEGMENT)
    in_specs = [cur, last, prev, v_last, v_prev, cur, stat, stat, whole]
    operands = [bv.view(qn), bv.view(kn), bv.view(kn), bv.view(proj), bv.view(proj), bv.view(dyb), bv.view(lse),
                bv.view(delta), bias]
    aliases = {}
    if running is not None:
        aliases = {len(operands) + i: i for i in range(3)}
        in_specs += [cur] * 3
        operands += [bv.view(t) for t in running]
    if dep is not None:
        body = _drop_arg(body, len(operands))
        in_specs.append(DEP_SPEC)
        operands.append(dep)
    dq, dk, dv, ds = pl.pallas_call(
        body, name=name, grid=(dil, nb),
        out_shape=[big, big, big, jax.ShapeDtypeStruct((n_heads, HEAD, 2 * HEAD), F32)],
        in_specs=in_specs, out_specs=[cur, cur, cur, whole], input_output_aliases=aliases,
        scratch_shapes=[pltpu.VMEM((HEAD, w), F32), pltpu.VMEM((HEAD, w), F32),
                        pltpu.VMEM((n_heads, HEAD, 2 * HEAD), F32), pltpu.VMEM((n_heads, HEAD, 2 * HEAD), F32),
                        pltpu.VMEM((n_heads, HEAD, 2 * HEAD), BF16), pltpu.VMEM((n_heads, HEAD, 2 * HEAD), BF16),
                        pltpu.VMEM((HEAD, w), BF16), pltpu.VMEM((HEAD, w), BF16)],
        compiler_params=_params(("arbitrary", "arbitrary"), 56),
    )(*operands)
    return dq.reshape(s, w), dk.reshape(s, w), dv.reshape(s, w), ds


def _qkv_bwd(dproj, proj, dq, dk, dv, g_q, g_k, w):
    s = proj.shape[0]
    n_heads = w // HEAD
    tm = HEAD

    def body(dproj_hbm, q_ref, k_ref, gq_ref, gk_ref, dq_ref, dk_ref, dv_ref, out_ref, dgq_ref, dgk_ref):
        i = pl.program_id(0)
        gq = gq_ref[...] * (HEAD ** -0.5)
        gk = gk_ref[...]
        acc_q = jnp.zeros((1, HEAD), F32)
        acc_k = jnp.zeros((1, HEAD), F32)
        for h in range(n_heads):
            sl = slice(h * HEAD, (h + 1) * HEAD)
            q, k = q_ref[:, sl], k_ref[:, sl]
            dqn, dkn = dq_ref[:, sl], dk_ref[:, sl]
            rq, rk = _rms(q), _rms(k)
            out_ref[:, h * HEAD:(h + 1) * HEAD] = _rms_bwd(dqn, q, rq, gq).astype(BF16)
            out_ref[:, w + h * HEAD:w + (h + 1) * HEAD] = _rms_bwd(dkn, k, rk, gk).astype(BF16)
            acc_q += jnp.sum(dqn * q * rq, axis=0, keepdims=True)
            acc_k += jnp.sum(dkn * k * rk, axis=0, keepdims=True)
        out_ref[:, 2 * w:] = dv_ref[...].astype(BF16)

        @pl.when(i == 0)
        def _():
            dgq_ref[...] = jnp.zeros_like(dgq_ref)
            dgk_ref[...] = jnp.zeros_like(dgk_ref)

        dgq_ref[...] += acc_q * (HEAD ** -0.5)
        dgk_ref[...] += acc_k

    seg = lambda j: pl.BlockSpec((tm, w), lambda i, j=j: (i, j))
    vec = pl.BlockSpec((1, HEAD), lambda i: (0, 0))
    row = pl.BlockSpec((tm, w), lambda i: (i, 0))
    return pl.pallas_call(
        body, name="qkv_bwd", grid=(s // tm,),
        out_shape=[jax.ShapeDtypeStruct(dproj.shape, BF16),
                   jax.ShapeDtypeStruct((1, HEAD), F32), jax.ShapeDtypeStruct((1, HEAD), F32)],
        in_specs=[pl.BlockSpec(memory_space=pl.ANY), seg(3), seg(4), vec, vec] + [row] * 3,
        out_specs=[pl.BlockSpec((tm, 3 * w), lambda i: (i, 1)), vec, vec],
        input_output_aliases={0: 0},
        compiler_params=_params(("arbitrary",), 48),
    )(dproj, proj, proj, g_q, g_k, dq, dk, dv)


def _mixer_a(u, gv, ws_ref, bst_ref, lng, lnb, z_scr, ln_scr):
    n_groups = u.shape[1] // HEAD
    mu = jnp.mean(gv, axis=-1, keepdims=True)
    xc = gv - mu
    rs = lax.rsqrt(jnp.mean(xc * xc, axis=-1, keepdims=True) + EPS)
    xhat = xc * rs
    ln_scr[...] = (xhat * lng + lnb).astype(BF16)
    causal = _causal_mask()
    for g in range(n_groups):
        sl = slice(g * HEAD, (g + 1) * HEAD)
        wm = jnp.where(causal, ws_ref[g], 0.0).astype(BF16)
        z_scr[:, sl] = _dot(wm, ln_scr[:, sl]) + bst_ref[:, g:g + 1]
    return u, xhat, rs


def _causal_mask():
    token = lambda r: 16 * (r % SUB) + r // SUB
    row = lax.broadcasted_iota(jnp.int32, (HEAD, HEAD), 0)
    col = lax.broadcasted_iota(jnp.int32, (HEAD, HEAD), 1)
    return token(col) <= token(row)


def _merge_b(o_refs, lse_refs, yb_scr):
    n_heads = yb_scr.shape[1] // HEAD
    lses = [t[...] for t in lse_refs]
    m = jnp.maximum(jnp.maximum(lses[0], lses[1]), lses[2])
    tot = m + jnp.log(sum(jnp.exp(t - m) for t in lses))
    alphas = [jnp.exp(t - tot) for t in lses]
    for h in range(n_heads):
        sl = slice(h * HEAD, (h + 1) * HEAD)
        yb_scr[:, sl] = sum(a[:, h:h + 1] * o[:, sl].astype(F32) for a, o in zip(alphas, o_refs))
    return tot


def _mix_fwd(proj, outs, lses, w_s, bst, ln_g, ln_b, g_a, g_b, w, dep=None):
    s = proj.shape[0]
    n_groups = w // HEAD

    def body(au_ref, av_ref, az_ref, bz_ref, o1, o2, o3, l1, l2, l3, ws_ref, bst_ref,
             lng_ref, lnb_ref, ga_ref, gb_ref, p_ref, y_ref, z_scr, ln_scr, yb_scr):
        u, _, _ = _mixer_a(_gelu(au_ref[...]), _gelu(av_ref[...]), ws_ref, bst_ref, lng_ref[...], lnb_ref[...],
                           z_scr, ln_scr)
        ya = u * z_scr[...]
        silu_a, _ = _silu_and_grad(az_ref[...])
        perm = p_ref[...]
        y_ref[:, :w] = _dot(perm, (ya * _rms(ya) * ga_ref[...] * silu_a).astype(BF16), TN_DIMS).astype(BF16)
        _merge_b((o1, o2, o3), (l1, l2, l3), yb_scr)
        yb = yb_scr[...]
        silu_b, _ = _silu_and_grad(bz_ref[...])
        y_ref[:, w:] = _dot(perm, (yb * _rms(yb) * gb_ref[...] * silu_b).astype(BF16), TN_DIMS).astype(BF16)

    seg = lambda j: pl.BlockSpec((HEAD, w), lambda i, j=j: (i, j))
    row = pl.BlockSpec((HEAD, w), lambda i: (i, 0))
    stat = pl.BlockSpec((HEAD, HEAD), lambda i: (i, 0))
    vec = pl.BlockSpec((1, w), lambda i: (0, 0))
    in_specs = [seg(0), seg(1), seg(2), seg(6), row, row, row, stat, stat, stat,
                pl.BlockSpec((n_groups, HEAD, HEAD), lambda i: (0, 0, 0)),
                pl.BlockSpec((HEAD, n_groups), lambda i: (0, 0)), vec, vec, vec, vec,
                pl.BlockSpec((HEAD, HEAD), lambda i: (0, 0))]
    operands = [proj, proj, proj, proj, *outs, *lses, w_s, bst, ln_g, ln_b, g_a, g_b, _chunk_perm()]
    if dep is not None:
        body = _drop_arg(body, len(operands))
        in_specs.append(DEP_SPEC)
        operands.append(dep)
    return pl.pallas_call(
        body, name="mix_fwd", grid=(s // HEAD,),
        out_shape=jax.ShapeDtypeStruct((s, 2 * w), BF16), in_specs=in_specs,
        out_specs=pl.BlockSpec((HEAD, 2 * w), lambda i: (i, 0)),
        scratch_shapes=[pltpu.VMEM((HEAD, w), F32), pltpu.VMEM((HEAD, w), BF16), pltpu.VMEM((HEAD, w), F32)],
        compiler_params=_params(("arbitrary",), 48),
    )(*operands)


def _mix_bwd(proj, dy, outs, lses, w_s, bst, ln_g, ln_b, g_a, g_b, w):
    s = proj.shape[0]
    n_groups = w // HEAD

    def body(au_ref, av_ref, az_ref, bz_ref, dy_ref, o1, o2, o3, l1, l2, l3, ws_ref, bst_ref,
             lng_ref, lnb_ref, ga_ref, gb_ref,
             dproj_ref, dyb_ref, tot_ref, dl_ref, dws_ref, dbst_ref, dlng_ref, dlnb_ref, dga_ref, dgb_ref,
             z_scr, ln_scr, yb_scr, dz_scr, dln_scr):
        i = pl.program_id(0)

        @pl.when(i == 0)
        def _():
            for t in (dws_ref, dbst_ref, dlng_ref, dlnb_ref, dga_ref, dgb_ref):
                t[...] = jnp.zeros_like(t)

        az = az_ref[...]
        lng = lng_ref[...]
        u, du_dau = _gelu_and_grad(au_ref[...])
        gv, dgv_dav = _gelu_and_grad(av_ref[...])
        u, xhat, rs = _mixer_a(u, gv, ws_ref, bst_ref, lng, lnb_ref[...], z_scr, ln_scr)
        z = z_scr[...]
        ya = u * z
        ra = _rms(ya)
        silu_a, dsilu_a = _silu_and_grad(az)
        dya_all = dy_ref[:, :w].astype(F32)
        na = ya * ra * ga_ref[...]
        dna = dya_all * silu_a
        dproj_ref[:, 2 * w:3 * w] = (dya_all * na * dsilu_a).astype(BF16)
        dga_ref[...] += jnp.sum(dna * ya * ra, axis=0, keepdims=True)
        dya = _rms_bwd(dna, ya, ra, ga_ref[...])
        dproj_ref[:, :w] = (dya * z * du_dau).astype(BF16)
        dz_scr[...] = (dya * u).astype(BF16)

        causal = _causal_mask()
        for g in range(n_groups):
            sl = slice(g * HEAD, (g + 1) * HEAD)
            wm = jnp.where(causal, ws_ref[g], 0.0).astype(BF16)
            dz = dz_scr[:, sl]
            dln_scr[:, sl] = _dot(wm, dz, TN_DIMS)
            dws_ref[g] += jnp.where(causal, _dot(dz, ln_scr[:, sl], NT_DIMS), 0.0)
            dbst_ref[:, g:g + 1] += jnp.sum(dz.astype(F32), axis=-1, keepdims=True)
        dln = dln_scr[...]
        dlng_ref[...] += jnp.sum(dln * xhat, axis=0, keepdims=True)
        dlnb_ref[...] += jnp.sum(dln, axis=0, keepdims=True)
        gy = dln * lng
        dgv = rs * (gy - jnp.mean(gy, axis=-1, keepdims=True)
                    - xhat * jnp.mean(gy * xhat, axis=-1, keepdims=True))
        dproj_ref[:, w:2 * w] = (dgv * dgv_dav).astype(BF16)
        dproj_ref[:, 3 * w:6 * w] = jnp.zeros((HEAD, 3 * w), BF16)

        tot_ref[...] = _merge_b((o1, o2, o3), (l1, l2, l3), yb_scr)
        yb = yb_scr[...]
        rb = _rms(yb)
        bz = bz_ref[...]
        silu_b, dsilu_b = _silu_and_grad(bz)
        dyb_all = dy_ref[:, w:].astype(F32)
        dnb = dyb_all * silu_b
        dproj_ref[:, 6 * w:] = (dyb_all * yb * rb * gb_ref[...] * dsilu_b).astype(BF16)
        dgb_ref[...] += jnp.sum(dnb * yb * rb, axis=0, keepdims=True)
        dyb = _rms_bwd(dnb, yb, rb, gb_ref[...])
        dyb_ref[...] = dyb
        prod = dyb * yb
        dl_ref[...] = _lane_pick(
            [jnp.sum(prod[:, h * HEAD:(h + 1) * HEAD], axis=-1, keepdims=True) for h in range(n_groups)], HEAD)

    seg = lambda j: pl.BlockSpec((HEAD, w), lambda i, j=j: (i, j))
    row_w = pl.BlockSpec((HEAD, w), lambda i: (i, 0))
    stat = pl.BlockSpec((HEAD, HEAD), lambda i: (i, 0))
    vec = pl.BlockSpec((1, w), lambda i: (0, 0))
    ws_spec = pl.BlockSpec((n_groups, HEAD, HEAD), lambda i: (0, 0, 0))
    bst_spec = pl.BlockSpec((HEAD, n_groups), lambda i: (0, 0))
    vec_shape = jax.ShapeDtypeStruct((1, w), F32)
    return pl.pallas_call(
        body, name="mix_bwd", grid=(s // HEAD,),
        out_shape=[jax.ShapeDtypeStruct((s, N_SEG * w), BF16), jax.ShapeDtypeStruct((s, w), F32),
                   jax.ShapeDtypeStruct((s, HEAD), F32), jax.ShapeDtypeStruct((s, HEAD), F32),
                   jax.ShapeDtypeStruct((n_groups, HEAD, HEAD), F32), jax.ShapeDtypeStruct((HEAD, n_groups), F32),
                   vec_shape, vec_shape, vec_shape, vec_shape],
        in_specs=[seg(0), seg(1), seg(2), seg(6), pl.BlockSpec((HEAD, 2 * w), lambda i: (i, 0)),
                  row_w, row_w, row_w, stat, stat, stat, ws_spec, bst_spec, vec, vec, vec, vec],
        out_specs=[pl.BlockSpec((HEAD, N_SEG * w), lambda i: (i, 0)), row_w, stat, stat,
                   ws_spec, bst_spec, vec, vec, vec, vec],
        scratch_shapes=[pltpu.VMEM((HEAD, w), F32), pltpu.VMEM((HEAD, w), BF16), pltpu.VMEM((HEAD, w), F32),
                        pltpu.VMEM((HEAD, w), BF16), pltpu.VMEM((HEAD, w), F32)],
        compiler_params=_params(("arbitrary",), 56),
    )(proj, proj, proj, proj, dy, *outs, *lses, w_s, bst, ln_g, ln_b, g_a, g_b)


def _ple_gate(hn2, wgate, h, p, wup_g, tgt):
    s, d = h.shape
    n, p_dim, c_up = wup_g.shape
    tm, pair = _tile(s, 512), 2
    tn = pair * c_up
    cols = n // pair
    tiles = (s // tm) * cols
    cur = lambda t: jnp.minimum(t, tiles - 1)
    prv = lambda t: jnp.maximum(t - 1, 0)

    def body(a_ref, b_ref, h_ref, p_ref, wup_ref, tgt_ref, dout_ref, dpre_ref, dup_ref, loss_ref, acc):
        @pl.when(pl.program_id(0) == 0)
        def _():
            acc[...] = jnp.zeros_like(acc)

        gate = jax.nn.sigmoid(acc[...])
        pb = p_ref[...].astype(BF16)
        up = jnp.concatenate([_dot(pb, wup_ref[g]) for g in range(pair)], axis=1)
        err = h_ref[...] + gate * up - tgt_ref[...]
        dout = err * (1.0 / d)
        dout_ref[...] = dout.astype(BF16)
        dpre_ref[...] = (dout * up * gate * (1.0 - gate)).astype(BF16)
        dup_ref[...] = (dout * gate).astype(BF16)
        part = 0.5 * jnp.sum(err * err) * (1.0 / d)
        rr = lax.broadcasted_iota(jnp.int32, (SUB, HEAD), 0)
        cc = lax.broadcasted_iota(jnp.int32, (SUB, HEAD), 1)
        loss_ref[...] = jnp.where((rr == 0) & (cc == 0), part, 0.0)
        acc[...] = _dot(a_ref[...], b_ref[...])

    rows = s // tm
    tail = pl.BlockSpec((tm, tn), lambda t: (prv(t) % rows, prv(t) // rows))
    big = jax.ShapeDtypeStruct((s, d), BF16)
    return pl.pallas_call(
        body, name="ple_gate", grid=(tiles + 1,),
        out_shape=[big, big, big, jax.ShapeDtypeStruct((rows * SUB, cols * HEAD), F32)],
        in_specs=[pl.BlockSpec((tm, d), lambda t: (cur(t) % rows, 0)),
                  pl.BlockSpec((d, tn), lambda t: (0, cur(t) // rows)),
                  tail, pl.BlockSpec((tm, p_dim), lambda t: (prv(t) % rows, 0)),
                  pl.BlockSpec((pair, p_dim, c_up), lambda t: (prv(t) // rows, 0, 0)), tail],
        out_specs=[tail, tail, tail, pl.BlockSpec((SUB, HEAD), lambda t: (prv(t) % rows, prv(t) // rows))],
        scratch_shapes=[pltpu.VMEM((tm, tn), F32)],
        compiler_params=_params(("arbitrary",), 60),
    )(hn2, wgate, h, p, wup_g, tgt)


def _local_step(x, p, tgt, small, wg, ex, while_last_travels=None):
    s, d = x.shape
    n, _, c_in = wg.buffers["w_in"].shape
    assert n == N_DEV
    d_in = n * c_in
    w = d_in // N_SEG
    n_heads = w // HEAD
    p_dim, c_up = wg.buffers["w_ple_up"].shape[1:]
    assert s % (HEAD * DILATIONS[-1]) == 0 and w % HEAD == 0 and d == n * c_up == 2 * w

    near = (2, 4)
    wg.start("gather_in_pair", ["w_in"], (1,))
    token = wg.start("gather_in_first", ["w_in"], ("first",))
    rest = ["w_out", "w_ple_gate", "w_ple_up"]

    hn = _rmsnorm_fwd(x, small["g_pre"], "pre_norm", True, dep=token)
    tm = _tile(s, 1024)

    def in_proj(shards, name, carry, dep=None):
        return _matmul(
            hn, wg.buffers["w_in"], name=name, grid=(s // tm, len(shards), 1), dims=NN_DIMS,
            prefetch=jnp.stack(shards).astype(jnp.int32),
            a_spec=pl.BlockSpec((tm, d), lambda i, j, k, sh: (i, 0)),
            b_spec=pl.BlockSpec((None, d, c_in), lambda i, j, k, sh: (sh[j], 0, 0), pipeline_mode=pl.Buffered(1)),
            acc_shape=(tm, c_in), out_shapes=[jax.ShapeDtypeStruct((s, d_in), F32)],
            out_specs=[pl.BlockSpec((tm, c_in), lambda i, j, k, sh: (i, sh[j]))], carry=carry, dep=dep,
            vmem_mib=56)[0]

    me = wg.me
    core = me & 1
    first, second, far = me ^ (4 - 2 * core), me ^ (2 + 2 * core), me ^ 6
    proj = in_proj([me], "in_proj_own", None)
    token = wg.start("gather_in_second", ["w_in"], ("second",), after=[proj])
    wg.arrived("gather_in_pair", [token])
    proj = in_proj([me ^ 1], "in_proj_sibling", proj)
    buckets = jnp.stack([_rel_buckets(dil) for dil in DILATIONS])
    bias = _bias_build(small["rel_bias"], buckets, n_heads)
    ahead = [bias] + [wg.buffers[k] for k in rest]
    for tag, mine, from_sibling in (("first", first, second ^ 1), ("second", second, first ^ 1), ("far", far, far ^ 1)):
        token = wg.forward("gather_in_" + tag, [proj] + ahead)
        ahead = []
        if tag == "second":
            token = wg.relay_start("gather_in_far", ["w_in"], after=[token])
        elif tag == "far":
            token = wg.start("gather_rest", rest, (1,) + near, after=[token])
        proj = in_proj([mine], "in_proj_" + tag, proj, dep=token)
        wg.forwarded("gather_in_" + tag, [proj])
        proj = in_proj([from_sibling], "in_proj_%s_forwarded" % tag, proj)
    win_g = wg.buffers["w_in"]

    qn, kn = _qkv_prep(proj, small["g_q"], small["g_k"], w)
    token = wg.forward("gather_rest", [qn])
    token = wg.relay_start("gather_rest_far", rest, after=[token])
    outs, lses = [], []
    for c, dil in enumerate(DILATIONS):
        o, l = _attn_fwd(qn, kn, proj, bias[c], dil, "attn_fwd_d%d" % dil, dep=token)
        outs.append(o)
        lses.append(l)
    token = wg.forward("gather_rest_far", outs)

    ws_p = small["w_s"][:, CHUNK_ORDER][:, :, CHUNK_ORDER]
    bst = small["b_s"].T[CHUNK_ORDER]
    mix_args = (outs, lses, ws_p, bst, small["ln_v_g"], small["ln_v_b"], small["g_out_a"], small["g_out_b"], w)
    y = _mix_fwd(proj, *mix_args, dep=token)
    wg.forwarded("gather_rest", [y])
    wg.forwarded("gather_rest_far", [y])
    wout_g, wgate_g, wup_g = (wg.buffers[k] for k in rest)
    wout_f = wout_g.reshape(2 * w, d)
    wgate_f = wgate_g.reshape(d, d)

    tn = _tile(d, 1024)
    tk2 = 2 * w

    def resid_epilogue(acc, ex, outs_):
        outs_[0][...] = ex[0][...] + acc[...]

    h = _matmul(
        y, wout_f, name="out_proj", grid=(s // tm, d // tn, (2 * w) // tk2), dims=NN_DIMS,
        a_spec=pl.BlockSpec((tm, tk2), lambda i, j, k: (i, k)),
        b_spec=pl.BlockSpec((tk2, tn), lambda i, j, k: (k, j)),
        acc_shape=(tm, tn), out_shapes=[jax.ShapeDtypeStruct((s, d), F32)],
        out_specs=[pl.BlockSpec((tm, tn), lambda i, j, k: (i, j))],
        extra=(x,), extra_specs=(pl.BlockSpec((tm, tn), lambda i, j, k: (i, j)),),
        epilogue=resid_epilogue, vmem_mib=56)[0]

    hn2 = _rmsnorm_fwd(h, small["g_ple"], "ple_norm", False)

    dout, dpre, dup, loss_parts = _ple_gate(hn2, wgate_f, h, p, wup_g, tgt)
    loss = jnp.sum(loss_parts)

    tks = _tile(s, 2048)
    g_wup = _matmul(
        p, dup, name="grad_w_up", grid=(1, n, s // tks), dims=TN_DIMS,
        a_spec=pl.BlockSpec((tks, p_dim), lambda i, j, k: (k, 0)),
        b_spec=pl.BlockSpec((tks, c_up), lambda i, j, k: (k, j)),
        acc_shape=(p_dim, c_up), out_shapes=[jax.ShapeDtypeStruct((n, p_dim, c_up), BF16)],
        out_specs=[pl.BlockSpec((None, p_dim, c_up), lambda i, j, k: (j, 0, 0))])[0]

    def tn_matmul(a, b, name):
        m_, n_ = a.shape[1], b.shape[1]
        bm, bn = _tile(m_, 1024), _tile(n_, 1024)
        return _matmul(
            a, b, name=name, grid=(m_ // bm, n_ // bn, 1), dims=TN_DIMS,
            a_spec=pl.BlockSpec((s, bm), lambda i, j, k: (0, i)),
            b_spec=pl.BlockSpec((s, bn), lambda i, j, k: (0, j)),
            acc_shape=(bm, bn), out_shapes=[jax.ShapeDtypeStruct((m_, n_), BF16)],
            out_specs=[pl.BlockSpec((bm, bn), lambda i, j, k: (i, j))], vmem_mib=56)[0]

    def nt_matmul(a, b, name, out_dtype, dep=None):
        k_, n_ = a.shape[1], b.shape[0]
        bm, bn, bk = _tile(s, 1024), _tile(n_, 1024), k_
        return _matmul(
            a, b, name=name, grid=(s // bm, n_ // bn, k_ // bk), dims=NT_DIMS,
            a_spec=pl.BlockSpec((bm, bk), lambda i, j, k: (i, k)),
            b_spec=pl.BlockSpec((bn, bk), lambda i, j, k: (j, k)),
            acc_shape=(bm, bn), out_shapes=[jax.ShapeDtypeStruct((s, n_), out_dtype)],
            out_specs=[pl.BlockSpec((bm, bn), lambda i, j, k: (i, j))], dep=dep, vmem_mib=56)[0]

    by_core = lambda g: g.reshape((N_CHIP, 2) + g.shape[-2:])
    g_wgate = tn_matmul(hn2, dpre, "grad_w_gate").reshape(wgate_g.shape)
    dhn2 = nt_matmul(dpre, wgate_f, "ple_gate_bwd", BF16)
    dh_b, dh_bp, dg_ple = _rmsnorm_bwd(dhn2, h, small["g_ple"], dout, "ple_norm_bwd", False, True)
    g_wout = tn_matmul(y, dh_b, "grad_w_out").reshape(wout_g.shape)

    late = ("w_out", "w_ple_gate", "w_ple_up")
    late_parts = (g_wout, g_wgate, g_wup)
    token = ex.push_pairs("pair_late", [by_core(g) for g in late_parts])
    dy = nt_matmul(dh_bp, wout_f, "out_proj_bwd", BF16, dep=token)
    (dproj, dyb, lse_tot, delta, dws, dbst, dlng, dlnb, dga, dgb) = _mix_bwd(proj, dy, *mix_args)
    both_columns, from_sibling = ex.pairs_done("pair_late", [dproj])
    pair_sums = [_pair_add(mine.reshape((N_DEV,) + mine.shape[-2:]), theirs, "pair_add_" + k, ex.core)
                 for k, mine, theirs in zip(late, both_columns, from_sibling)]
    token = ex.push_chips("chip_late", pair_sums)

    running, dss = None, []
    for c, dil in enumerate(DILATIONS):
        *running, ds = _attn_bwd(qn, kn, proj, dyb, lse_tot, delta, bias[c], dil, "attn_bwd_d%d" % dil,
                                 running=running, dep=token if c == 0 else None)
        dss.append(ds)
    d_rel = _bias_grad(jnp.stack(dss), buckets, n_heads)
    dproj, dgq, dgk = _qkv_bwd(dproj, proj, *running, small["g_q"], small["g_k"], w)
    pair_sums, landed, _ = ex.chips_done("chip_late", [dproj])
    delivered = {k: (mine, theirs) for k, mine, theirs in zip(late, pair_sums, landed)}

    token_row = np.argsort(CHUNK_ORDER)
    dws = dws[:, token_row][:, :, token_row]
    dbst = dbst[token_row]
    small_grads = {
        "w_s": dws, "b_s": dbst.T, "ln_v_g": dlng, "ln_v_b": dlnb, "g_q": dgq, "g_k": dgk,
        "rel_bias": d_rel, "g_out_a": dga, "g_out_b": dgb, "g_ple": dg_ple,
    }

    bm = _tile(d, 1024)

    def grad_w_in(core, name, dep=None):
        return _matmul(
            hn, dproj, name=name, grid=(d // bm, N_CHIP, 1), dims=TN_DIMS, prefetch=core.reshape(1),
            a_spec=pl.BlockSpec((s, bm), lambda i, j, k, core_ref: (0, i)),
            b_spec=pl.BlockSpec((s, c_in), lambda i, j, k, core_ref: (0, 2 * j + core_ref[0])),
            acc_shape=(bm, c_in), out_shapes=[jax.ShapeDtypeStruct((N_CHIP, d, c_in), BF16)],
            out_specs=[pl.BlockSpec((None, bm, c_in), lambda i, j, k, core_ref: (j, i, 0))], dep=dep,
            vmem_mib=60)[0]

    for_sibling = grad_w_in(1 - ex.core, "grad_w_in_sibling")
    token = ex.push_pairs("pair_in", [for_sibling])
    mine = grad_w_in(ex.core, "grad_w_in_mine", dep=token)
    _, from_sibling = ex.pairs_done("pair_in", [mine])
    pair_sum = _pair_add(mine, from_sibling[0], "pair_add_w_in")
    token = ex.push_chips("chip_in", [pair_sum], _pack_small(small_grads, SMALL_EARLY))

    dhn = _matmul(
        dproj, win_g, name="in_proj_bwd", grid=(s // tm, d // tn, n // 2), dims=NT_DIMS,
        a_spec=pl.BlockSpec((tm, 2 * c_in), lambda i, j, k: (i, k)),
        b_spec=pl.BlockSpec((2, tn, c_in), lambda i, j, k: (k, j, 0)),
        acc_shape=(tm, tn), out_shapes=[jax.ShapeDtypeStruct((s, d), BF16)],
        out_specs=[pl.BlockSpec((tm, tn), lambda i, j, k: (i, j))], dep=token, vmem_mib=56)[0]
    grad_x, dg_pre = _rmsnorm_bwd(dhn, x, small["g_pre"], dh_b, "pre_norm_bwd", True, False)
    extra = while_last_travels(token, delivered, dg_pre) if while_last_travels is not None else []
    pair_sums, landed, slabs = ex.chips_done("chip_in", [grad_x] + list(extra))
    delivered["w_in"] = (pair_sums[0], landed[0])
    small_grads["g_pre"] = dg_pre
    return loss, grad_x, small_grads, delivered, slabs, extra


SMALL_EARLY = ("w_s", "b_s", "ln_v_g", "ln_v_b", "g_q", "g_k", "rel_bias", "g_out_a", "g_out_b", "g_ple")
SMALL_LAST = ("g_pre",)
SMALL_NAMES = SMALL_LAST + SMALL_EARLY


def _pack_small(tree, names):
    parts = []
    for name in names:
        flat = tree[name].astype(F32).reshape(-1)
        pad = (-flat.shape[0]) % HEAD
        parts.append(jnp.pad(flat, (0, pad)) if pad else flat)
    slab = jnp.concatenate(parts).reshape(-1, HEAD)
    pad_rows = (-slab.shape[0]) % 8
    return jnp.pad(slab, ((0, pad_rows), (0, 0))) if pad_rows else slab


def _unpack_small(slab, like, names):
    flat = slab.reshape(-1)
    out, off = {}, 0
    for name in names:
        size = like[name].size
        out[name] = flat[off:off + size].reshape(like[name].shape)
        off += size + (-size) % HEAD
    return out


def _peer(k):
    x, y, c = (lax.axis_index(a) for a in AXES)
    if k == "first":
        px, py, pc = x ^ (1 - c), y ^ c, c
    elif k == "second":
        px, py, pc = x ^ c, y ^ (1 - c), c
    else:
        bits = ((k >> 2) & 1, (k >> 1) & 1, k & 1)
        px, py, pc = (1 - v if b else v for v, b in zip((x, y, c), bits))
    return (px, py, pc), 4 * px + 2 * py + pc


def _my_index():
    x, y, c = (lax.axis_index(a) for a in AXES)
    return 4 * x + 2 * y + c


N_CHIP = 4
HBM_SPEC = pl.BlockSpec(memory_space=pl.ANY)


def _remote(src, dst, send_sem, recv_sem, peer):
    return pltpu.make_async_remote_copy(src_ref=src, dst_ref=dst, send_sem=send_sem, recv_sem=recv_sem,
                                        device_id=peer, device_id_type=pl.DeviceIdType.MESH)


SEM_SPEC = pl.BlockSpec(memory_space=pltpu.SEMAPHORE)
HBM_ONLY = pl.BlockSpec(memory_space=pltpu.HBM)
DATAFLOW = pltpu.SideEffectType.DATAFLOW_SIDE_EFFECTING


def _comm_call(name, arrays, *, wait=None, start=None, after=()):
    n, n_after = len(arrays), len(after)

    def body(*refs):
        ins = refs[:n]
        pos = n
        if wait is not None:
            for cp in wait[2](ins, refs[pos], refs[pos + 1]):
                cp.wait()
            pos += 2
        outs = refs[pos + n_after:]
        if start is not None:
            for cp in start[1](ins, outs[0], outs[1]):
                cp.start()
        outs[-1][...] = jnp.zeros_like(outs[-1])

    operands = [pltpu.with_memory_space_constraint(a, pltpu.HBM) for a in arrays]
    in_specs = [HBM_ONLY] * n
    if wait is not None:
        operands += [wait[0], wait[1]]
        in_specs += [SEM_SPEC, SEM_SPEC]
    operands += list(after)
    in_specs += [HBM_SPEC] * n_after
    out_shape, out_specs = [], []
    if start is not None:
        out_shape += [pltpu.SemaphoreType.DMA((start[0],))] * 2
        out_specs += [SEM_SPEC, SEM_SPEC]
    first = len(out_shape)
    out_shape += [pltpu.HBM(a.shape, a.dtype) for a in arrays] + [jax.ShapeDtypeStruct((SUB, HEAD), F32)]
    out_specs += [HBM_ONLY] * n + [pl.BlockSpec(memory_space=pltpu.VMEM)]
    res = pl.pallas_call(
        body, name=name, out_shape=tuple(out_shape), in_specs=tuple(in_specs), out_specs=tuple(out_specs),
        input_output_aliases={i: first + i for i in range(n)},
        compiler_params=pltpu.CompilerParams(has_side_effects=DATAFLOW),
    )(*operands)
    sems = (res[0], res[1]) if start is not None else None
    return list(res[first:first + n]), sems, res[-1]


class _GradExchange:
    def __init__(self):
        x, y, c = (lax.axis_index(a) for a in AXES)
        self.core = c.astype(jnp.int32)
        self.chip = (2 * x + y).astype(jnp.int32)
        self.pending = {}

    def _pair_copies(self, n_arr):
        def make(refs, send_sems, recv_sems):
            sibling, _ = _peer(1)
            other = 1 - lax.axis_index("c")
            srcs, lands = refs[:n_arr], refs[n_arr:]
            pick = lambda ref, ch: ref.at[ch, other] if len(ref.shape) == 4 else ref.at[ch]
            return [_remote(pick(srcs[a], ch), lands[a].at[ch], send_sems.at[a * N_CHIP + ch],
                            recv_sems.at[a * N_CHIP + ch], sibling)
                    for a in range(n_arr) for ch in range(N_CHIP)]
        return make

    def _chip_copies(self, n_arr, with_slab):
        def make(refs, send_sems, recv_sems):
            x, y = lax.axis_index("x"), lax.axis_index("y")
            my_chip = 2 * x + y
            srcs, lands = refs[:n_arr], refs[n_arr:2 * n_arr]
            copies = []
            for j, k in enumerate((2, 4, 6)):
                peer, peer_idx = _peer(k)
                for a in range(n_arr):
                    copies.append(_remote(srcs[a].at[peer_idx // 2], lands[a].at[my_chip],
                                          send_sems.at[3 * a + j], recv_sems.at[3 * a + j], peer))
            if with_slab:
                slab, slab_land = refs[2 * n_arr], refs[2 * n_arr + 1]
                for k in range(1, N_DEV):
                    peer, _ = _peer(k)
                    copies.append(_remote(slab, slab_land.at[_my_index()], send_sems.at[3 * n_arr + k - 1],
                                          recv_sems.at[3 * n_arr + k - 1], peer))
            return copies
        return make

    def push_pairs(self, tag, for_sibling):
        n_arr = len(for_sibling)
        lands = [lax.empty((N_CHIP,) + a.shape[-2:], a.dtype) for a in for_sibling]
        make = self._pair_copies(n_arr)
        arrays, sems, token = _comm_call(tag + "_start", list(for_sibling) + lands, start=(n_arr * N_CHIP, make))
        self.pending[tag] = (arrays, sems, make, n_arr)
        return token

    def pairs_done(self, tag, after):
        arrays, sems, make, n_arr = self.pending.pop(tag)
        arrays, _, _ = _comm_call(tag + "_wait", arrays, wait=(sems[0], sems[1], make), after=after)
        return arrays[:n_arr], arrays[n_arr:]

    def push_chips(self, tag, pair_sums, slab=None):
        n_arr = len(pair_sums)
        arrays = list(pair_sums) + [lax.empty(a.shape, a.dtype) for a in pair_sums]
        n_copies = 3 * n_arr
        if slab is not None:
            arrays += [slab, lax.empty((N_DEV,) + slab.shape, slab.dtype)]
            n_copies += N_DEV - 1
        make = self._chip_copies(n_arr, slab is not None)
        arrays, sems, token = _comm_call(tag + "_start", arrays, start=(n_copies, make))
        self.pending[tag] = (arrays, sems, make, n_arr)
        return token

    def chips_done(self, tag, after):
        arrays, sems, make, n_arr = self.pending.pop(tag)
        arrays, _, _ = _comm_call(tag + "_wait", arrays, wait=(sems[0], sems[1], make), after=after)
        return arrays[:n_arr], arrays[n_arr:2 * n_arr], arrays[2 * n_arr:]


def _cast_place(w, name):
    r, c = w.shape
    tr = r if r * c <= MIB else 1 << ((MIB // c).bit_length() - 1)
    assert r % tr == 0

    def body(me_ref, w_ref, o_ref):
        o_ref[...] = w_ref[...].astype(BF16)

    return pl.pallas_call(
        body, name=name, out_shape=jax.ShapeDtypeStruct((N_DEV, r, c), BF16),
        grid_spec=pltpu.PrefetchScalarGridSpec(
            num_scalar_prefetch=1, grid=(r // tr,),
            in_specs=[pl.BlockSpec((tr, c), lambda i, me_ref: (i, 0))],
            out_specs=pl.BlockSpec((None, tr, c), lambda i, me_ref: (me_ref[0], i, 0))),
        compiler_params=_params(("arbitrary",), 40),
    )(_my_index().astype(jnp.int32).reshape(1), w)


class _WeightGather:
    CHIPS = (2, 4, 6)

    def __init__(self, buffers):
        self.buffers = dict(buffers)
        self.pending = {}
        self.me = _my_index().astype(jnp.int32)

    def _own_slot_to(self, peers):
        def make(refs, send_sems, recv_sems):
            me = _my_index()
            return [_remote(ref.at[me], ref.at[me], send_sems.at[len(peers) * a + j],
                            recv_sems.at[len(peers) * a + j], _peer(k)[0])
                    for a, ref in enumerate(refs) for j, k in enumerate(peers)]
        return make

    def _forward_from(self, chips):
        def make(refs, send_sems, recv_sems):
            sibling, _ = _peer(1)
            copies = []
            for a, ref in enumerate(refs):
                for j, k in enumerate(chips):
                    slot = ref.at[_peer(k)[1]]
                    copies.append(_remote(slot, slot, send_sems.at[len(chips) * a + j],
                                          recv_sems.at[len(chips) * a + j], sibling))
            return copies
        return make

    def _run(self, call, names, **kw):
        arrays, sems, token = _comm_call(call, [self.buffers[k] for k in names], **kw)
        self.buffers.update(zip(names, arrays))
        return sems, token

    def start(self, tag, names, peers, after=()):
        make = self._own_slot_to(peers)
        sems, token = self._run(tag + "_start", names, start=(len(names) * len(peers), make), after=after)
        self.pending[tag] = (names, sems, make, peers)
        return token

    @staticmethod
    def _relay(refs, send_sems, recv_sems):
        x, y, c = (lax.axis_index(a) for a in AXES)
        peer = (x ^ (1 - c), y ^ c, c)
        slot = _my_index() ^ (2 + 2 * c)
        return [_remote(ref.at[slot], ref.at[slot], send_sems.at[a], recv_sems.at[a], peer)
                for a, ref in enumerate(refs)]

    def relay_start(self, tag, names, after=()):
        sems, token = self._run(tag + "_start", names, start=(len(names), self._relay), after=after)
        self.pending[tag] = (names, sems, self._relay, (6,))
        return token

    def arrived(self, tag, after):
        names, sems, make, _ = self.pending.pop(tag)
        self._run(tag + "_wait", names, wait=(sems[0], sems[1], make), after=after)

    def forward(self, tag, after):
        names, sems, make, peers = self.pending.pop(tag)
        chips = tuple(k for k in peers if k != 1)
        onward = self._forward_from(chips)
        new_sems, token = self._run(tag + "_forward", names, wait=(sems[0], sems[1], make),
                                    start=(len(chips) * len(names), onward), after=after)
        self.pending[tag + "/fwd"] = (names, new_sems, onward)
        return token

    def forwarded(self, tag, after):
        names, sems, make = self.pending.pop(tag + "/fwd")
        self._run(tag + "_done", names, wait=(sems[0], sems[1], make), after=after)


def _pair_add(mine, theirs, name, core=None):
    _, r, c_dim = theirs.shape
    tr = r if r * c_dim <= MIB else 1 << ((MIB // c_dim).bit_length() - 1)
    assert r % tr == 0
    stride = 1 if core is None else 2
    offset = jnp.zeros((1,), jnp.int32) if core is None else core.reshape(1)

    def body(off_ref, a_ref, b_ref, o_ref):
        o_ref[...] = (a_ref[...].astype(F32) + b_ref[...].astype(F32)).astype(BF16)

    blk = (None, tr, c_dim)
    return pl.pallas_call(
        body, name=name, out_shape=jax.ShapeDtypeStruct(theirs.shape, BF16),
        grid_spec=pltpu.PrefetchScalarGridSpec(
            num_scalar_prefetch=1, grid=(N_CHIP, r // tr),
            in_specs=[pl.BlockSpec(blk, lambda ch, i, off_ref: (stride * ch + off_ref[0], i, 0)),
                      pl.BlockSpec(blk, lambda ch, i, off_ref: (ch, i, 0))],
            out_specs=pl.BlockSpec(blk, lambda ch, i, off_ref: (ch, i, 0))),
        compiler_params=_params(("arbitrary", "arbitrary"), 40),
    )(offset, mine, theirs)


def _slab_exchange(slab, name):
    def body(slab_in, slab_out, send_sems, recv_sems, local_sem):
        me = _my_index()
        local = pltpu.make_async_copy(slab_in, slab_out.at[me], local_sem)
        local.start()
        sends = []
        for k in range(1, N_DEV):
            peer, _ = _peer(k)
            sends.append(_remote(slab_in, slab_out.at[me], send_sems.at[k - 1], recv_sems.at[k - 1], peer))
        for cp in sends:
            cp.start()
        for k in range(1, N_DEV):
            peer, peer_idx = _peer(k)
            slot = slab_out.at[peer_idx]
            _remote(slot, slot, send_sems.at[k - 1], recv_sems.at[k - 1], peer).wait_recv()
        for cp in sends:
            cp.wait_send()
        local.wait()

    return pl.pallas_call(
        body, name=name, out_shape=jax.ShapeDtypeStruct((N_DEV,) + slab.shape, slab.dtype),
        in_specs=[HBM_SPEC], out_specs=HBM_SPEC,
        scratch_shapes=[pltpu.SemaphoreType.DMA((N_DEV - 1,)), pltpu.SemaphoreType.DMA((N_DEV - 1,)),
                        pltpu.SemaphoreType.DMA],
        compiler_params=pltpu.CompilerParams(has_side_effects=True),
    )(slab)


def _adamw_math(w, g, m, v):
    m = ADAM_B1 * m + (1.0 - ADAM_B1) * g
    v = ADAM_B2 * v + (1.0 - ADAM_B2) * (g * g)
    m_hat = m / (1.0 - ADAM_B1 ** ADAM_STEP)
    v_hat = v / (1.0 - ADAM_B2 ** ADAM_STEP)
    delta = -ADAM_LR * (m_hat / (jnp.sqrt(v_hat) + ADAM_EPS) + ADAM_WD * w)
    return delta, m, v


def _adamw(parts, own, place, w, m, v, name, dep=None):
    n_parts = parts.shape[0]
    r, c = w.shape
    budget = 280 * 1024
    tr = r if r * c <= budget else 1 << ((budget // c).bit_length() - 1)
    assert r % tr == 0

    def body(place_ref, p_ref, own_ref, w_ref, m_ref, v_ref, g_ref, d_ref, nm_ref, nv_ref):
        mine = own_ref[...].astype(F32)
        g = None
        for i in range(n_parts):
            term = jnp.where(place_ref[0] == i, mine, p_ref[i].astype(F32))
            g = term if g is None else g + term
        delta, nm, nv = _adamw_math(w_ref[...], g, m_ref[...], v_ref[...])
        g_ref[...] = g
        d_ref[...] = delta
        nm_ref[...] = nm
        nv_ref[...] = nv

    blk = pl.BlockSpec((tr, c), lambda i, place_ref: (i, 0))
    shape = jax.ShapeDtypeStruct((r, c), F32)
    in_specs = [pl.BlockSpec((n_parts, tr, c), lambda i, place_ref: (0, i, 0)),
                pl.BlockSpec((None, tr, c), lambda i, place_ref: (place_ref[1], i, 0)), blk, blk, blk]
    operands = [parts, own, w, m, v]
    if dep is not None:
        body = _drop_arg(body, 1 + len(operands))
        in_specs.append(pl.BlockSpec((SUB, HEAD), lambda i, place_ref: (0, 0)))
        operands.append(dep)
    return pl.pallas_call(
        body, name=name, out_shape=[shape] * 4,
        grid_spec=pltpu.PrefetchScalarGridSpec(
            num_scalar_prefetch=1, grid=(r // tr,), in_specs=in_specs, out_specs=[blk] * 4),
        compiler_params=_params(("arbitrary",), 48),
    )(place, *operands)


def kernel(x, p, g_pre, w_in, w_s, b_s, ln_v_g, ln_v_b, g_q, g_k, rel_bias, g_out_a, g_out_b, w_out, g_ple, w_ple_gate, w_ple_up, loss_target, m_g_pre, m_w_in, m_w_s, m_b_s, m_ln_v_g, m_ln_v_b, m_g_q, m_g_k, m_rel_bias, m_g_out_a, m_g_out_b, m_w_out, m_g_ple, m_w_ple_gate, m_w_ple_up, v_g_pre, v_w_in, v_w_s, v_b_s, v_ln_v_g, v_ln_v_b, v_g_q, v_g_k, v_rel_bias, v_g_out_a, v_g_out_b, v_w_out, v_g_ple, v_w_ple_gate, v_w_ple_up):
    args = dict(locals())
    small = {"g_pre": g_pre, "w_s": w_s[0], "b_s": b_s[0], "ln_v_g": ln_v_g, "ln_v_b": ln_v_b, "g_q": g_q,
             "g_k": g_k, "rel_bias": rel_bias, "g_out_a": g_out_a, "g_out_b": g_out_b, "g_ple": g_ple}
    big_names = ("w_in", "w_out", "w_ple_gate", "w_ple_up")
    big = {k: args[k][0] for k in big_names}

    wg = _WeightGather({k: _cast_place(big[k], "place_" + k) for k in big_names})
    ex = _GradExchange()
    results = {}

    def big_adamw(k, delivered, dep=None):
        mine, theirs = delivered[k]
        place = jnp.stack([ex.chip, ex.chip])
        return _adamw(theirs, mine, place, big[k], args["m_" + k][0], args["v_" + k][0], "adamw_" + k, dep=dep)

    squeeze = lambda t: {k: (t[k][0] if k in ("w_s", "b_s") else t[k]) for k in SMALL_NAMES}
    small_m = squeeze({k: args["m_" + k] for k in SMALL_NAMES})
    small_v = squeeze({k: args["v_" + k] for k in SMALL_NAMES})
    slab_place = jnp.stack([_my_index().astype(jnp.int32), jnp.zeros((), jnp.int32)])

    def small_adamw(names_, parts, own, call_name):
        packed = _adamw(parts, own[None], slab_place, _pack_small(small, names_), _pack_small(small_m, names_),
                        _pack_small(small_v, names_), call_name)
        for idx in range(4):
            tree = _unpack_small(packed[idx], small, names_)
            for k in names_:
                results.setdefault(k, [None] * 4)[idx] = tree[k].reshape(args[k].shape)
        return packed[0]

    def while_last_travels(token, delivered, dg_pre):
        done = []
        for k in big_names[1:]:
            results[k] = big_adamw(k, delivered, dep=token)
            done.append(results[k][0])
        last_slab = _pack_small({"g_pre": dg_pre}, SMALL_LAST)
        done.append(small_adamw(SMALL_LAST, _slab_exchange(last_slab, "last_exchange"), last_slab, "adamw_last"))
        return done

    loss, grad_x, small_parts, delivered, slabs, _ = _local_step(
        x[0], p[0, 0], loss_target[0], small, wg, ex, while_last_travels)
    results["w_in"] = big_adamw("w_in", delivered)
    for k in big_names:
        results[k] = [t[None] for t in results[k]]
    small_adamw(SMALL_EARLY, slabs[1], slabs[0], "adamw_small")

    names = ("g_pre", "w_in", "w_s", "b_s", "ln_v_g", "ln_v_b", "g_q", "g_k", "rel_bias", "g_out_a", "g_out_b",
             "w_out", "g_ple", "w_ple_gate", "w_ple_up")
    total = lax.psum(loss, AXES)
    out = [total, grad_x[None]]
    for idx in range(4):
        out += [results[k][idx] for k in names]
    return tuple(out)
```

```python
import math

import numpy as np
import jax
import jax.numpy as jnp
from jax import lax
from jax.experimental import pallas as pl
from jax.experimental.pallas import tpu as pltpu

F32 = jnp.float32
BF16 = jnp.bfloat16
EPS = 1e-6
NEG_INF = -1e30
HEAD = 128
DILATIONS = (1, 4, 16)
NUM_BUCKETS = 32
MAX_DISTANCE = 2048
N_SEG = 7
ADAM_LR = 0.001
ADAM_B1 = 0.9
ADAM_B2 = 0.999
ADAM_EPS = 1e-08
ADAM_WD = 0.01
ADAM_STEP = 10
AXES = ("x", "y", "c")
N_DEV = 8
MIB = 1 << 20

SUB = 8

CHUNK_ORDER = np.array([16 * (r % SUB) + r // SUB for r in range(HEAD)])
BLOCK_ORDER = {
    1: CHUNK_ORDER,
    4: np.array([32 * (r // 32) + 4 * (r % SUB) + (r // SUB) % 4 for r in range(HEAD)]),
    16: np.arange(HEAD),
}

NT_DIMS = (((1,), (1,)), ((), ()))
TN_DIMS = (((0,), (0,)), ((), ()))
NN_DIMS = (((1,), (0,)), ((), ()))


def _params(semantics, vmem_mib):
    return pltpu.CompilerParams(dimension_semantics=semantics, vmem_limit_bytes=vmem_mib * MIB)


def _gelu(a):
    return 0.5 * a * (1.0 + lax.erf(a * (2.0 ** -0.5)))


def _gelu_and_grad(a):
    cdf = 0.5 * (1.0 + lax.erf(a * (2.0 ** -0.5)))
    return a * cdf, cdf + a * jnp.exp(-0.5 * a * a) * ((2.0 * math.pi) ** -0.5)


def _silu_and_grad(a):
    s = jax.nn.sigmoid(a)
    return a * s, s * (1.0 + a * (1.0 - s))


def _rms(v):
    return lax.rsqrt(jnp.mean(v * v, axis=-1, keepdims=True) + EPS)


def _rms_bwd(dy, v, r, g):
    gy = dy * g
    return r * gy - v * (r * r * r) * jnp.mean(gy * v, axis=-1, keepdims=True)


def _dot(a, b, dims=NN_DIMS):
    return lax.dot_general(a, b, dims, preferred_element_type=F32)


def _lane_pick(cols, width):
    rows = cols[0].shape[0]
    lane = lax.broadcasted_iota(jnp.int32, (rows, width), 1)
    out = jnp.zeros((rows, width), F32)
    for h, col in enumerate(cols):
        out = jnp.where(lane == h, col, out)
    return out


def _chunk_perm():
    return jnp.asarray(np.eye(HEAD, dtype=np.float32)[CHUNK_ORDER], BF16)


def _unpermute_f32(p, v):
    hi = v.astype(BF16)
    rest = v - hi.astype(F32)
    mid = rest.astype(BF16)
    lo = (rest - mid.astype(F32)).astype(BF16)
    return _dot(p, hi, TN_DIMS) + _dot(p, mid, TN_DIMS) + _dot(p, lo, TN_DIMS)


def _rmsnorm_fwd(x, g, name, permute, dep=None):
    s, d = x.shape
    tm = HEAD

    def body(x_ref, g_ref, p_ref, o_ref):
        v = x_ref[...]
        out = (v * _rms(v) * g_ref[...]).astype(BF16)
        if permute:
            out = _dot(p_ref[...], out).astype(BF16)
        o_ref[...] = out

    in_specs = [pl.BlockSpec((tm, d), lambda i: (i, 0)), pl.BlockSpec((1, d), lambda i: (0, 0)),
                pl.BlockSpec((HEAD, HEAD), lambda i: (0, 0))]
    operands = [x, g, _chunk_perm()]
    if dep is not None:
        body = _drop_arg(body, len(operands))
        in_specs.append(DEP_SPEC)
        operands.append(dep)
    return pl.pallas_call(
        body, name=name, grid=(s // tm,),
        out_shape=jax.ShapeDtypeStruct((s, d), BF16), in_specs=in_specs,
        out_specs=pl.BlockSpec((tm, d), lambda i: (i, 0)),
        compiler_params=_params(("arbitrary",), 40),
    )(*operands)


def _rmsnorm_bwd(dy, v, g, res, name, dy_permuted, with_bf16):
    s, d = v.shape
    tm = HEAD
    perm = _chunk_perm()

    def body(dy_ref, v_ref, g_ref, res_ref, p_ref, *outs):
        dg_ref = outs[-1]

        @pl.when(pl.program_id(0) == 0)
        def _():
            dg_ref[...] = jnp.zeros_like(dg_ref)

        vv, dyv = v_ref[...], dy_ref[...]
        if dy_permuted:
            dyv = _unpermute_f32(p_ref[...], dyv) if dyv.dtype == F32 else _dot(p_ref[...], dyv, TN_DIMS)
        dyv = dyv.astype(F32)
        r = _rms(vv)
        dx = res_ref[...].astype(F32) + _rms_bwd(dyv, vv, r, g_ref[...])
        if with_bf16:
            dxb = dx.astype(BF16)
            outs[0][...] = dxb
            outs[1][...] = _dot(p_ref[...], dxb).astype(BF16)
        else:
            outs[0][...] = dx
        dg_ref[...] += jnp.sum(dyv * vv * r, axis=0, keepdims=True)

    row = pl.BlockSpec((tm, d), lambda i: (i, 0))
    vec = pl.BlockSpec((1, d), lambda i: (0, 0))
    if with_bf16:
        shapes = [jax.ShapeDtypeStruct((s, d), BF16)] * 2
        specs = [row, row]
    else:
        shapes = [jax.ShapeDtypeStruct((s, d), F32)]
        specs = [row]
    shapes.append(jax.ShapeDtypeStruct((1, d), F32))
    specs.append(vec)
    return pl.pallas_call(
        body, name=name, grid=(s // tm,), out_shape=shapes,
        in_specs=[row, row, vec, row, pl.BlockSpec((HEAD, HEAD), lambda i: (0, 0))], out_specs=specs,
        compiler_params=_params(("arbitrary",), 40),
    )(dy, v, g, res, perm)


DEP_SPEC = pl.BlockSpec((SUB, HEAD), lambda *_: (0, 0))


def _drop_arg(body, pos):
    return lambda *refs: body(*refs[:pos], *refs[pos + 1:])


def _matmul(a, b, *, name, grid, a_spec, b_spec, dims, acc_shape, out_shapes, out_specs,
            extra=(), extra_specs=(), epilogue=None, vmem_mib=48, dep=None, prefetch=None, carry=None):
    nk = grid[2]
    n_user = len(extra)
    for unread, spec in ((dep, DEP_SPEC), (carry, HBM_SPEC)):
        if unread is not None:
            extra, extra_specs = tuple(extra) + (unread,), tuple(extra_specs) + (spec,)
    n_extra, n_out = len(extra), len(out_shapes)
    n_pre = 0 if prefetch is None else 1
    aliases = {} if carry is None else {n_pre + 2 + n_extra - 1: 0}

    def body(*refs):
        refs = refs[n_pre:]
        a_ref, b_ref = refs[0], refs[1]
        ex = refs[2:2 + n_user]
        outs = refs[2 + n_extra:2 + n_extra + n_out]
        def lhs():
            av = a_ref[...]
            return av if av.dtype == BF16 else av.astype(BF16)

        if nk == 1 and epilogue is None:
            outs[0][...] = _dot(lhs(), b_ref[...], dims).astype(outs[0].dtype)
            return
        acc = refs[-1]
        k = pl.program_id(2)

        @pl.when(k == 0)
        def _():
            acc[...] = jnp.zeros_like(acc)

        av = lhs()
        if len(b_ref.shape) == 3:
            span = b_ref.shape[2]
            acc[...] += sum(_dot(av[:, g * span:(g + 1) * span], b_ref[g], dims) for g in range(b_ref.shape[0]))
        else:
            acc[...] += _dot(av, b_ref[...], dims)

        @pl.when(k == nk - 1)
        def _():
            if epilogue is None:
                outs[0][...] = acc[...].astype(outs[0].dtype)
            else:
                epilogue(acc, ex, outs)

    direct = nk == 1 and epilogue is None
    scratch = [] if direct else [pltpu.VMEM(acc_shape, F32)]
    params = _params(("parallel", "parallel", "arbitrary"), vmem_mib)
    if prefetch is None:
        return pl.pallas_call(
            body, name=name, grid=grid, out_shape=list(out_shapes),
            in_specs=[a_spec, b_spec, *extra_specs], out_specs=list(out_specs),
            scratch_shapes=scratch, compiler_params=params, input_output_aliases=aliases,
        )(a, b, *extra)
    return pl.pallas_call(
        body, name=name, out_shape=list(out_shapes),
        grid_spec=pltpu.PrefetchScalarGridSpec(
            num_scalar_prefetch=1, grid=grid, in_specs=[a_spec, b_spec, *extra_specs],
            out_specs=list(out_specs), scratch_shapes=scratch),
        compiler_params=params, input_output_aliases=aliases,
    )(prefetch, a, b, *extra)


def _tile(n, want):
    t = min(n, want)
    while n % t:
        t //= 2
    return t


def _rel_buckets(dil):
    order = BLOCK_ORDER[dil]
    qi = jnp.asarray(HEAD + order)
    kj = jnp.asarray(np.concatenate([order, HEAD + order]))
    delta = qi[:, None] - kj[None, :]
    band = (delta >= 0) & (delta <= HEAD)
    dist = jnp.clip(delta, 0, None) * dil
    max_exact = NUM_BUCKETS // 2
    dd = jnp.maximum(dist, 1).astype(F32)
    large = max_exact + (jnp.log(dd / max_exact) / math.log(MAX_DISTANCE / max_exact)
                         * (NUM_BUCKETS - max_exact)).astype(jnp.int32)
    large = jnp.minimum(large, NUM_BUCKETS - 1)
    bucket = jnp.where(dist < max_exact, dist, large)
    return jnp.where(band, bucket, -1).astype(jnp.int32)


def _bias_build(rel_bias, buckets, n_heads):
    nd = buckets.shape[0]

    def body(rb_ref, bk_ref, o_ref):
        for c in range(nd):
            def per_head(h, carry, c=c):
                bk = bk_ref[c]
                acc = jnp.where(bk < 0, NEG_INF, 0.0).astype(F32)
                for b in range(NUM_BUCKETS):
                    acc = jnp.where(bk == b, rb_ref[b, h], acc)
                o_ref[c, h] = acc
                return carry

            lax.fori_loop(0, n_heads, per_head, 0)

    return pl.pallas_call(
        body, name="bias_build",
        out_shape=jax.ShapeDtypeStruct((nd, n_heads, HEAD, 2 * HEAD), F32),
        in_specs=[pl.BlockSpec(memory_space=pltpu.SMEM), pl.BlockSpec(memory_space=pltpu.VMEM)],
        out_specs=pl.BlockSpec(memory_space=pltpu.VMEM),
    )(rel_bias, buckets)


def _bias_grad(ds_all, buckets, n_heads):
    nd = buckets.shape[0]
    pairs = HEAD * 2 * HEAD

    def body(ds_ref, bk_ref, o_ref):
        rows = lax.broadcasted_iota(jnp.int32, (NUM_BUCKETS, pairs), 0)
        tot = jnp.zeros((n_heads, NUM_BUCKETS), F32)
        for c in range(nd):
            onehot = (rows == bk_ref[c]).astype(BF16)
            ds = ds_ref[c]
            hi = ds.astype(BF16)
            lo = (ds - hi.astype(F32)).astype(BF16)
            tot = tot + _dot(hi, onehot, NT_DIMS) + _dot(lo, onehot, NT_DIMS)
        o_ref[...] = tot

    out = pl.pallas_call(
        body, name="bias_grad",
        out_shape=jax.ShapeDtypeStruct((n_heads, NUM_BUCKETS), F32),
        in_specs=[pl.BlockSpec(memory_space=pltpu.VMEM), pl.BlockSpec(memory_space=pltpu.VMEM)],
        out_specs=pl.BlockSpec(memory_space=pltpu.VMEM),
        compiler_params=pltpu.CompilerParams(vmem_limit_bytes=40 * MIB),
    )(ds_all.reshape(nd, n_heads, pairs), buckets.reshape(nd, 1, pairs))
    return out.T


def _qkv_prep(proj, g_q, g_k, w, dep=None):
    s = proj.shape[0]
    n_heads = w // HEAD
    tm = HEAD

    def body(q_ref, k_ref, gq_ref, gk_ref, qn_ref, kn_ref):
        gq = gq_ref[...] * (HEAD ** -0.5)
        gk = gk_ref[...]
        for h in range(n_heads):
            sl = slice(h * HEAD, (h + 1) * HEAD)
            q = q_ref[:, sl]
            k = k_ref[:, sl]
            qn_ref[:, sl] = q * _rms(q) * gq
            kn_ref[:, sl] = k * _rms(k) * gk

    seg = lambda j: pl.BlockSpec((tm, w), lambda i, j=j: (i, j))
    vec = pl.BlockSpec((1, HEAD), lambda i: (0, 0))
    out = pl.BlockSpec((tm, w), lambda i: (i, 0))
    in_specs = [seg(3), seg(4), vec, vec]
    operands = [proj, proj, g_q, g_k]
    if dep is not None:
        body = _drop_arg(body, len(operands))
        in_specs.append(DEP_SPEC)
        operands.append(dep)
    return pl.pallas_call(
        body, name="qkv_prep", grid=(s // tm,),
        out_shape=[jax.ShapeDtypeStruct((s, w), F32)] * 2,
        in_specs=in_specs, out_specs=[out, out],
        compiler_params=_params(("arbitrary",), 40),
    )(*operands)


class _BlockView:
    def __init__(self, s, dil):
        assert s % (HEAD * dil) == 0 and dil in BLOCK_ORDER
        self.nb = s // (HEAD * dil)
        if dil == 1:
            self.lead, self.block = (s,), (HEAD,)
            self.index = lambda r, n: (n,)
        elif dil == 4:
            self.lead, self.block = (s // 512, 4, 4, 4, SUB), (None, 4, 4, None, SUB)
            self.index = lambda r, n: (n, 0, 0, r, 0)
        else:
            self.lead, self.block = (s // 2048, 16, 16, SUB), (None, 16, None, SUB)
            self.index = lambda r, n: (n, 0, r, 0)

    def view(self, t):
        return t.reshape(self.lead + (t.shape[-1],))

    def spec(self, width, block_of, column=0):
        return pl.BlockSpec(self.block + (width,), lambda r, n: self.index(r, block_of(r, n)) + (column,))


def _rows(ref, lanes=slice(None)):
    v = ref[(slice(None),) * (len(ref.shape) - 1) + (lanes,)]
    return v.reshape(HEAD, v.shape[-1])


def _set_rows(ref, lanes, value):
    ref[(slice(None),) * (len(ref.shape) - 1) + (lanes,)] = value.reshape(ref.shape[:-1] + (value.shape[-1],))


V_SEGMENT = 5


def _attn_fwd(qn, kn, proj, bias, dil, name, dep=None):
    s, w = qn.shape
    n_heads = w // HEAD
    bv = _BlockView(s, dil)

    def body(q_ref, kc_ref, vc_ref, bias_ref, o_ref, lse_ref, s_scr, e_scr, lse_scr, inv_scr, k_prev, v_prev):
        n = pl.program_id(1)
        heads = [slice(h * HEAD, (h + 1) * HEAD) for h in range(n_heads)]
        lse_scr[...] = jnp.zeros_like(lse_scr)

        @pl.when(n == 0)
        def _():
            k_prev[...] = jnp.zeros_like(k_prev)
            v_prev[...] = jnp.zeros_like(v_prev)

        for h, sl in enumerate(heads):
            q = _rows(q_ref, sl).astype(BF16)
            s_p = _dot(q, k_prev[:, sl], NT_DIMS) + bias_ref[h, :, :HEAD]
            s_scr[h, :, :HEAD] = jnp.where(n > 0, s_p, NEG_INF)
            s_scr[h, :, HEAD:] = _dot(q, _rows(kc_ref, sl).astype(BF16), NT_DIMS) + bias_ref[h, :, HEAD:]
        for h in range(n_heads):
            sc = s_scr[h]
            m = jnp.max(sc, axis=-1, keepdims=True)
            e = jnp.exp(sc - m)
            den = jnp.sum(e, axis=-1, keepdims=True)
            e_scr[h] = e.astype(BF16)
            lse_scr[:, h:h + 1] = m + jnp.log(den)
            inv_scr[:, h:h + 1] = 1.0 / den
        for h, sl in enumerate(heads):
            v_cur = _rows(vc_ref, sl).astype(BF16)
            o = _dot(e_scr[h, :, :HEAD], v_prev[:, sl]) + _dot(e_scr[h, :, HEAD:], v_cur)
            _set_rows(o_ref, sl, o * inv_scr[:, h:h + 1])
            v_prev[:, sl] = v_cur
            k_prev[:, sl] = _rows(kc_ref, sl).astype(BF16)
        _set_rows(lse_ref, slice(None), lse_scr[...])

    cur = bv.spec(w, lambda r, n: n)
    in_specs = [cur, cur, bv.spec(w, lambda r, n: n, V_SEGMENT),
                pl.BlockSpec((n_heads, HEAD, 2 * HEAD), lambda r, n: (0, 0, 0))]
    operands = [bv.view(qn), bv.view(kn), bv.view(proj), bias]
    if dep is not None:
        body = _drop_arg(body, len(operands))
        in_specs.append(DEP_SPEC)
        operands.append(dep)
    o, lse = pl.pallas_call(
        body, name=name, grid=(dil, bv.nb),
        out_shape=[jax.ShapeDtypeStruct(bv.lead + (w,), F32), jax.ShapeDtypeStruct(bv.lead + (HEAD,), F32)],
        in_specs=in_specs,
        out_specs=[cur, bv.spec(HEAD, lambda r, n: n)],
        scratch_shapes=[pltpu.VMEM((n_heads, HEAD, 2 * HEAD), F32), pltpu.VMEM((n_heads, HEAD, 2 * HEAD), BF16),
                        pltpu.VMEM((HEAD, HEAD), F32), pltpu.VMEM((HEAD, HEAD), F32),
                        pltpu.VMEM((HEAD, w), BF16), pltpu.VMEM((HEAD, w), BF16)],
        compiler_params=_params(("arbitrary", "arbitrary"), 48),
    )(*operands)
    return o.reshape(s, w), lse.reshape(s, HEAD)


def _attn_bwd(qn, kn, proj, dyb, lse, delta, bias, dil, name, running=None, dep=None):
    s, w = qn.shape
    n_heads = w // HEAD
    bv = _BlockView(s, dil)
    nb = bv.nb

    n_run = 0 if running is None else 3

    def body(q_ref, kc_ref, kp_ref, vc_ref, vp_ref, dy_ref, lse_ref, dl_ref, bias_ref, *rest):
        so_far = rest[:n_run]
        dq_ref, dk_ref, dv_ref, ds_ref, carry_k, carry_v, s_scr, dp_scr, p_scr, dsb_scr, k_cur, v_cur = rest[n_run:]
        base = (lambda i, sl: _rows(so_far[i], sl)) if n_run else (lambda i, sl: 0.0)
        r = pl.program_id(0)
        step = pl.program_id(1)
        blk = nb - 1 - step

        @pl.when((r == 0) & (step == 0))
        def _():
            ds_ref[...] = jnp.zeros_like(ds_ref)

        @pl.when(step == 0)
        def _():
            carry_k[...] = jnp.zeros_like(carry_k)
            carry_v[...] = jnp.zeros_like(carry_v)
            k_cur[...] = _rows(kc_ref).astype(BF16)
            v_cur[...] = _rows(vc_ref).astype(BF16)

        heads = [slice(h * HEAD, (h + 1) * HEAD) for h in range(n_heads)]
        tots = _rows(lse_ref)
        dls = _rows(dl_ref)
        for h, sl in enumerate(heads):
            q, dy = _rows(q_ref, sl).astype(BF16), _rows(dy_ref, sl).astype(BF16)
            kp, kc = _rows(kp_ref, sl).astype(BF16), k_cur[:, sl]
            vp, vc = _rows(vp_ref, sl).astype(BF16), v_cur[:, sl]
            s_p = _dot(q, kp, NT_DIMS) + bias_ref[h, :, :HEAD]
            s_scr[h, :, :HEAD] = jnp.where(blk > 0, s_p, NEG_INF)
            s_scr[h, :, HEAD:] = _dot(q, kc, NT_DIMS) + bias_ref[h, :, HEAD:]
            dp_scr[h, :, :HEAD] = _dot(dy, vp, NT_DIMS)
            dp_scr[h, :, HEAD:] = _dot(dy, vc, NT_DIMS)
        for h in range(n_heads):
            prob = jnp.exp(s_scr[h] - tots[:, h:h + 1])
            ds = prob * (dp_scr[h] - dls[:, h:h + 1])
            ds_ref[h] += ds
            p_scr[h] = prob.astype(BF16)
            dsb_scr[h] = ds.astype(BF16)
        for h, sl in enumerate(heads):
            q, dy = _rows(q_ref, sl).astype(BF16), _rows(dy_ref, sl).astype(BF16)
            kp, kc = _rows(kp_ref, sl).astype(BF16), k_cur[:, sl]
            ds_pb, ds_cb = dsb_scr[h, :, :HEAD], dsb_scr[h, :, HEAD:]
            _set_rows(dq_ref, sl, _dot(ds_pb, kp) + _dot(ds_cb, kc) + base(0, sl))
            _set_rows(dk_ref, sl, _dot(ds_cb, q, TN_DIMS) + carry_k[:, sl] + base(1, sl))
            carry_k[:, sl] = _dot(ds_pb, q, TN_DIMS)
            _set_rows(dv_ref, sl, _dot(p_scr[h, :, HEAD:], dy, TN_DIMS) + carry_v[:, sl] + base(2, sl))
            carry_v[:, sl] = _dot(p_scr[h, :, :HEAD], dy, TN_DIMS)
            k_cur[:, sl] = kp
            v_cur[:, sl] = _rows(vp_ref, sl).astype(BF16)

    cur = bv.spec(w, lambda r, n: nb - 1 - n)
    last = bv.spec(w, lambda r, n: nb - 1)
    prev = bv.spec(w, lambda r, n: jnp.maximum(nb - 2 - n, 0))
    stat = bv.spec(HEAD, lambda r, n: nb - 1 - n)
    whole = pl.BlockSpec((n_heads, HEAD, 2 * HEAD), lambda r, n: (0, 0, 0))
    big = jax.ShapeDtypeStruct(bv.lead + (w,), F32)
    v_last = bv.spec(w, lambda r, n: nb - 1, V_SEGMENT)
    v_prev = bv.spec(w, lambda r, n: jnp.maximum(nb - 2 - n, 0), V_SEGMENT)
    in_specs = [cur, last, prev, v_last, v_prev, cur, stat, stat, whole]
    operands = [bv.view(qn), bv.view(kn), bv.view(kn), bv.view(proj), bv.view(proj), bv.view(dyb), bv.view(lse),
                bv.view(delta), bias]
    aliases = {}
    if running is not None:
        aliases = {len(operands) + i: i for i in range(3)}
        in_specs += [cur] * 3
        operands += [bv.view(t) for t in running]
    if dep is not None:
        body = _drop_arg(body, len(operands))
        in_specs.append(DEP_SPEC)
        operands.append(dep)
    dq, dk, dv, ds = pl.pallas_call(
        body, name=name, grid=(dil, nb),
        out_shape=[big, big, big, jax.ShapeDtypeStruct((n_heads, HEAD, 2 * HEAD), F32)],
        in_specs=in_specs, out_specs=[cur, cur, cur, whole], input_output_aliases=aliases,
        scratch_shapes=[pltpu.VMEM((HEAD, w), F32), pltpu.VMEM((HEAD, w), F32),
                        pltpu.VMEM((n_heads, HEAD, 2 * HEAD), F32), pltpu.VMEM((n_heads, HEAD, 2 * HEAD), F32),
                        pltpu.VMEM((n_heads, HEAD, 2 * HEAD), BF16), pltpu.VMEM((n_heads, HEAD, 2 * HEAD), BF16),
                        pltpu.VMEM((HEAD, w), BF16), pltpu.VMEM((HEAD, w), BF16)],
        compiler_params=_params(("arbitrary", "arbitrary"), 56),
    )(*operands)
    return dq.reshape(s, w), dk.reshape(s, w), dv.reshape(s, w), ds


def _qkv_bwd(dproj, proj, dq, dk, dv, g_q, g_k, w):
    s = proj.shape[0]
    n_heads = w // HEAD
    tm = HEAD

    def body(dproj_hbm, q_ref, k_ref, gq_ref, gk_ref, dq_ref, dk_ref, dv_ref, out_ref, dgq_ref, dgk_ref):
        i = pl.program_id(0)
        gq = gq_ref[...] * (HEAD ** -0.5)
        gk = gk_ref[...]
        acc_q = jnp.zeros((1, HEAD), F32)
        acc_k = jnp.zeros((1, HEAD), F32)
        for h in range(n_heads):
            sl = slice(h * HEAD, (h + 1) * HEAD)
            q, k = q_ref[:, sl], k_ref[:, sl]
            dqn, dkn = dq_ref[:, sl], dk_ref[:, sl]
            rq, rk = _rms(q), _rms(k)
            out_ref[:, h * HEAD:(h + 1) * HEAD] = _rms_bwd(dqn, q, rq, gq).astype(BF16)
            out_ref[:, w + h * HEAD:w + (h + 1) * HEAD] = _rms_bwd(dkn, k, rk, gk).astype(BF16)
            acc_q += jnp.sum(dqn * q * rq, axis=0, keepdims=True)
            acc_k += jnp.sum(dkn * k * rk, axis=0, keepdims=True)
        out_ref[:, 2 * w:] = dv_ref[...].astype(BF16)

        @pl.when(i == 0)
        def _():
            dgq_ref[...] = jnp.zeros_like(dgq_ref)
            dgk_ref[...] = jnp.zeros_like(dgk_ref)

        dgq_ref[...] += acc_q * (HEAD ** -0.5)
        dgk_ref[...] += acc_k

    seg = lambda j: pl.BlockSpec((tm, w), lambda i, j=j: (i, j))
    vec = pl.BlockSpec((1, HEAD), lambda i: (0, 0))
    row = pl.BlockSpec((tm, w), lambda i: (i, 0))
    return pl.pallas_call(
        body, name="qkv_bwd", grid=(s // tm,),
        out_shape=[jax.ShapeDtypeStruct(dproj.shape, BF16),
                   jax.ShapeDtypeStruct((1, HEAD), F32), jax.ShapeDtypeStruct((1, HEAD), F32)],
        in_specs=[pl.BlockSpec(memory_space=pl.ANY), seg(3), seg(4), vec, vec] + [row] * 3,
        out_specs=[pl.BlockSpec((tm, 3 * w), lambda i: (i, 1)), vec, vec],
        input_output_aliases={0: 0},
        compiler_params=_params(("arbitrary",), 48),
    )(dproj, proj, proj, g_q, g_k, dq, dk, dv)


def _mixer_a(u, gv, ws_ref, bst_ref, lng, lnb, z_scr, ln_scr):
    n_groups = u.shape[1] // HEAD
    mu = jnp.mean(gv, axis=-1, keepdims=True)
    xc = gv - mu
    rs = lax.rsqrt(jnp.mean(xc * xc, axis=-1, keepdims=True) + EPS)
    xhat = xc * rs
    ln_scr[...] = (xhat * lng + lnb).astype(BF16)
    causal = _causal_mask()
    for g in range(n_groups):
        sl = slice(g * HEAD, (g + 1) * HEAD)
        wm = jnp.where(causal, ws_ref[g], 0.0).astype(BF16)
        z_scr[:, sl] = _dot(wm, ln_scr[:, sl]) + bst_ref[:, g:g + 1]
    return u, xhat, rs


def _causal_mask():
    token = lambda r: 16 * (r % SUB) + r // SUB
    row = lax.broadcasted_iota(jnp.int32, (HEAD, HEAD), 0)
    col = lax.broadcasted_iota(jnp.int32, (HEAD, HEAD), 1)
    return token(col) <= token(row)


def _merge_b(o_refs, lse_refs, yb_scr):
    n_heads = yb_scr.shape[1] // HEAD
    lses = [t[...] for t in lse_refs]
    m = jnp.maximum(jnp.maximum(lses[0], lses[1]), lses[2])
    tot = m + jnp.log(sum(jnp.exp(t - m) for t in lses))
    alphas = [jnp.exp(t - tot) for t in lses]
    for h in range(n_heads):
        sl = slice(h * HEAD, (h + 1) * HEAD)
        yb_scr[:, sl] = sum(a[:, h:h + 1] * o[:, sl].astype(F32) for a, o in zip(alphas, o_refs))
    return tot


def _mix_fwd(proj, outs, lses, w_s, bst, ln_g, ln_b, g_a, g_b, w, dep=None):
    s = proj.shape[0]
    n_groups = w // HEAD

    def body(au_ref, av_ref, az_ref, bz_ref, o1, o2, o3, l1, l2, l3, ws_ref, bst_ref,
             lng_ref, lnb_ref, ga_ref, gb_ref, p_ref, y_ref, z_scr, ln_scr, yb_scr):
        u, _, _ = _mixer_a(_gelu(au_ref[...]), _gelu(av_ref[...]), ws_ref, bst_ref, lng_ref[...], lnb_ref[...],
                           z_scr, ln_scr)
        ya = u * z_scr[...]
        silu_a, _ = _silu_and_grad(az_ref[...])
        perm = p_ref[...]
        y_ref[:, :w] = _dot(perm, (ya * _rms(ya) * ga_ref[...] * silu_a).astype(BF16), TN_DIMS).astype(BF16)
        _merge_b((o1, o2, o3), (l1, l2, l3), yb_scr)
        yb = yb_scr[...]
        silu_b, _ = _silu_and_grad(bz_ref[...])
        y_ref[:, w:] = _dot(perm, (yb * _rms(yb) * gb_ref[...] * silu_b).astype(BF16), TN_DIMS).astype(BF16)

    seg = lambda j: pl.BlockSpec((HEAD, w), lambda i, j=j: (i, j))
    row = pl.BlockSpec((HEAD, w), lambda i: (i, 0))
    stat = pl.BlockSpec((HEAD, HEAD), lambda i: (i, 0))
    vec = pl.BlockSpec((1, w), lambda i: (0, 0))
    in_specs = [seg(0), seg(1), seg(2), seg(6), row, row, row, stat, stat, stat,
                pl.BlockSpec((n_groups, HEAD, HEAD), lambda i: (0, 0, 0)),
                pl.BlockSpec((HEAD, n_groups), lambda i: (0, 0)), vec, vec, vec, vec,
                pl.BlockSpec((HEAD, HEAD), lambda i: (0, 0))]
    operands = [proj, proj, proj, proj, *outs, *lses, w_s, bst, ln_g, ln_b, g_a, g_b, _chunk_perm()]
    if dep is not None:
        body = _drop_arg(body, len(operands))
        in_specs.append(DEP_SPEC)
        operands.append(dep)
    return pl.pallas_call(
        body, name="mix_fwd", grid=(s // HEAD,),
        out_shape=jax.ShapeDtypeStruct((s, 2 * w), BF16), in_specs=in_specs,
        out_specs=pl.BlockSpec((HEAD, 2 * w), lambda i: (i, 0)),
        scratch_shapes=[pltpu.VMEM((HEAD, w), F32), pltpu.VMEM((HEAD, w), BF16), pltpu.VMEM((HEAD, w), F32)],
        compiler_params=_params(("arbitrary",), 48),
    )(*operands)


def _mix_bwd(proj, dy, outs, lses, w_s, bst, ln_g, ln_b, g_a, g_b, w):
    s = proj.shape[0]
    n_groups = w // HEAD

    def body(au_ref, av_ref, az_ref, bz_ref, dy_ref, o1, o2, o3, l1, l2, l3, ws_ref, bst_ref,
             lng_ref, lnb_ref, ga_ref, gb_ref,
             dproj_ref, dyb_ref, tot_ref, dl_ref, dws_ref, dbst_ref, dlng_ref, dlnb_ref, dga_ref, dgb_ref,
             z_scr, ln_scr, yb_scr, dz_scr, dln_scr):
        i = pl.program_id(0)

        @pl.when(i == 0)
        def _():
            for t in (dws_ref, dbst_ref, dlng_ref, dlnb_ref, dga_ref, dgb_ref):
                t[...] = jnp.zeros_like(t)

        az = az_ref[...]
        lng = lng_ref[...]
        u, du_dau = _gelu_and_grad(au_ref[...])
        gv, dgv_dav = _gelu_and_grad(av_ref[...])
        u, xhat, rs = _mixer_a(u, gv, ws_ref, bst_ref, lng, lnb_ref[...], z_scr, ln_scr)
        z = z_scr[...]
        ya = u * z
        ra = _rms(ya)
        silu_a, dsilu_a = _silu_and_grad(az)
        dya_all = dy_ref[:, :w].astype(F32)
        na = ya * ra * ga_ref[...]
        dna = dya_all * silu_a
        dproj_ref[:, 2 * w:3 * w] = (dya_all * na * dsilu_a).astype(BF16)
        dga_ref[...] += jnp.sum(dna * ya * ra, axis=0, keepdims=True)
        dya = _rms_bwd(dna, ya, ra, ga_ref[...])
        dproj_ref[:, :w] = (dya * z * du_dau).astype(BF16)
        dz_scr[...] = (dya * u).astype(BF16)

        causal = _causal_mask()
        for g in range(n_groups):
            sl = slice(g * HEAD, (g + 1) * HEAD)
            wm = jnp.where(causal, ws_ref[g], 0.0).astype(BF16)
            dz = dz_scr[:, sl]
            dln_scr[:, sl] = _dot(wm, dz, TN_DIMS)
            dws_ref[g] += jnp.where(causal, _dot(dz, ln_scr[:, sl], NT_DIMS), 0.0)
            dbst_ref[:, g:g + 1] += jnp.sum(dz.astype(F32), axis=-1, keepdims=True)
        dln = dln_scr[...]
        dlng_ref[...] += jnp.sum(dln * xhat, axis=0, keepdims=True)
        dlnb_ref[...] += jnp.sum(dln, axis=0, keepdims=True)
        gy = dln * lng
        dgv = rs * (gy - jnp.mean(gy, axis=-1, keepdims=True)
                    - xhat * jnp.mean(gy * xhat, axis=-1, keepdims=True))
        dproj_ref[:, w:2 * w] = (dgv * dgv_dav).astype(BF16)
        dproj_ref[:, 3 * w:6 * w] = jnp.zeros((HEAD, 3 * w), BF16)

        tot_ref[...] = _merge_b((o1, o2, o3), (l1, l2, l3), yb_scr)
        yb = yb_scr[...]
        rb = _rms(yb)
        bz = bz_ref[...]
        silu_b, dsilu_b = _silu_and_grad(bz)
        dyb_all = dy_ref[:, w:].astype(F32)
        dnb = dyb_all * silu_b
        dproj_ref[:, 6 * w:] = (dyb_all * yb * rb * gb_ref[...] * dsilu_b).astype(BF16)
        dgb_ref[...] += jnp.sum(dnb * yb * rb, axis=0, keepdims=True)
        dyb = _rms_bwd(dnb, yb, rb, gb_ref[...])
        dyb_ref[...] = dyb
        prod = dyb * yb
        dl_ref[...] = _lane_pick(
            [jnp.sum(prod[:, h * HEAD:(h + 1) * HEAD], axis=-1, keepdims=True) for h in range(n_groups)], HEAD)

    seg = lambda j: pl.BlockSpec((HEAD, w), lambda i, j=j: (i, j))
    row_w = pl.BlockSpec((HEAD, w), lambda i: (i, 0))
    stat = pl.BlockSpec((HEAD, HEAD), lambda i: (i, 0))
    vec = pl.BlockSpec((1, w), lambda i: (0, 0))
    ws_spec = pl.BlockSpec((n_groups, HEAD, HEAD), lambda i: (0, 0, 0))
    bst_spec = pl.BlockSpec((HEAD, n_groups), lambda i: (0, 0))
    vec_shape = jax.ShapeDtypeStruct((1, w), F32)
    return pl.pallas_call(
        body, name="mix_bwd", grid=(s // HEAD,),
        out_shape=[jax.ShapeDtypeStruct((s, N_SEG * w), BF16), jax.ShapeDtypeStruct((s, w), F32),
                   jax.ShapeDtypeStruct((s, HEAD), F32), jax.ShapeDtypeStruct((s, HEAD), F32),
                   jax.ShapeDtypeStruct((n_groups, HEAD, HEAD), F32), jax.ShapeDtypeStruct((HEAD, n_groups), F32),
                   vec_shape, vec_shape, vec_shape, vec_shape],
        in_specs=[seg(0), seg(1), seg(2), seg(6), pl.BlockSpec((HEAD, 2 * w), lambda i: (i, 0)),
                  row_w, row_w, row_w, stat, stat, stat, ws_spec, bst_spec, vec, vec, vec, vec],
        out_specs=[pl.BlockSpec((HEAD, N_SEG * w), lambda i: (i, 0)), row_w, stat, stat,
                   ws_spec, bst_spec, vec, vec, vec, vec],
        scratch_shapes=[pltpu.VMEM((HEAD, w), F32), pltpu.VMEM((HEAD, w), BF16), pltpu.VMEM((HEAD, w), F32),
                        pltpu.VMEM((HEAD, w), BF16), pltpu.VMEM((HEAD, w), F32)],
        compiler_params=_params(("arbitrary",), 56),
    )(proj, proj, proj, proj, dy, *outs, *lses, w_s, bst, ln_g, ln_b, g_a, g_b)


def _ple_gate(hn2, wgate, h, p, wup_g, tgt):
    s, d = h.shape
    n, p_dim, c_up = wup_g.shape
    tm, pair = _tile(s, 512), 2
    tn = pair * c_up
    cols = n // pair
    tiles = (s // tm) * cols
    cur = lambda t: jnp.minimum(t, tiles - 1)
    prv = lambda t: jnp.maximum(t - 1, 0)

    def body(a_ref, b_ref, h_ref, p_ref, wup_ref, tgt_ref, dout_ref, dpre_ref, dup_ref, loss_ref, acc):
        @pl.when(pl.program_id(0) == 0)
        def _():
            acc[...] = jnp.zeros_like(acc)

        gate = jax.nn.sigmoid(acc[...])
        pb = p_ref[...].astype(BF16)
        up = jnp.concatenate([_dot(pb, wup_ref[g]) for g in range(pair)], axis=1)
        err = h_ref[...] + gate * up - tgt_ref[...]
        dout = err * (1.0 / d)
        dout_ref[...] = dout.astype(BF16)
        dpre_ref[...] = (dout * up * gate * (1.0 - gate)).astype(BF16)
        dup_ref[...] = (dout * gate).astype(BF16)
        part = 0.5 * jnp.sum(err * err) * (1.0 / d)
        rr = lax.broadcasted_iota(jnp.int32, (SUB, HEAD), 0)
        cc = lax.broadcasted_iota(jnp.int32, (SUB, HEAD), 1)
        loss_ref[...] = jnp.where((rr == 0) & (cc == 0), part, 0.0)
        acc[...] = _dot(a_ref[...], b_ref[...])

    rows = s // tm
    tail = pl.BlockSpec((tm, tn), lambda t: (prv(t) % rows, prv(t) // rows))
    big = jax.ShapeDtypeStruct((s, d), BF16)
    return pl.pallas_call(
        body, name="ple_gate", grid=(tiles + 1,),
        out_shape=[big, big, big, jax.ShapeDtypeStruct((rows * SUB, cols * HEAD), F32)],
        in_specs=[pl.BlockSpec((tm, d), lambda t: (cur(t) % rows, 0)),
                  pl.BlockSpec((d, tn), lambda t: (0, cur(t) // rows)),
                  tail, pl.BlockSpec((tm, p_dim), lambda t: (prv(t) % rows, 0)),
                  pl.BlockSpec((pair, p_dim, c_up), lambda t: (prv(t) // rows, 0, 0)), tail],
        out_specs=[tail, tail, tail, pl.BlockSpec((SUB, HEAD), lambda t: (prv(t) % rows, prv(t) // rows))],
        scratch_shapes=[pltpu.VMEM((tm, tn), F32)],
        compiler_params=_params(("arbitrary",), 60),
    )(hn2, wgate, h, p, wup_g, tgt)


def _local_step(x, p, tgt, small, wg, ex, while_last_travels=None):
    s, d = x.shape
    n, _, c_in = wg.buffers["w_in"].shape
    assert n == N_DEV
    d_in = n * c_in
    w = d_in // N_SEG
    n_heads = w // HEAD
    p_dim, c_up = wg.buffers["w_ple_up"].shape[1:]
    assert s % (HEAD * DILATIONS[-1]) == 0 and w % HEAD == 0 and d == n * c_up == 2 * w

    near = (2, 4)
    wg.start("gather_in_pair", ["w_in"], (1,))
    token = wg.start("gather_in_first", ["w_in"], ("first",))
    rest = ["w_out", "w_ple_gate", "w_ple_up"]

    hn = _rmsnorm_fwd(x, small["g_pre"], "pre_norm", True, dep=token)
    tm = _tile(s, 1024)

    def in_proj(shards, name, carry, dep=None):
        return _matmul(
            hn, wg.buffers["w_in"], name=name, grid=(s // tm, len(shards), 1), dims=NN_DIMS,
            prefetch=jnp.stack(shards).astype(jnp.int32),
            a_spec=pl.BlockSpec((tm, d), lambda i, j, k, sh: (i, 0)),
            b_spec=pl.BlockSpec((None, d, c_in), lambda i, j, k, sh: (sh[j], 0, 0), pipeline_mode=pl.Buffered(1)),
            acc_shape=(tm, c_in), out_shapes=[jax.ShapeDtypeStruct((s, d_in), F32)],
            out_specs=[pl.BlockSpec((tm, c_in), lambda i, j, k, sh: (i, sh[j]))], carry=carry, dep=dep,
            vmem_mib=56)[0]

    me = wg.me
    core = me & 1
    first, second, far = me ^ (4 - 2 * core), me ^ (2 + 2 * core), me ^ 6
    proj = in_proj([me], "in_proj_own", None)
    token = wg.start("gather_in_second", ["w_in"], ("second",), after=[proj])
    wg.arrived("gather_in_pair", [token])
    proj = in_proj([me ^ 1], "in_proj_sibling", proj)
    buckets = jnp.stack([_rel_buckets(dil) for dil in DILATIONS])
    bias = _bias_build(small["rel_bias"], buckets, n_heads)
    ahead = [bias] + [wg.buffers[k] for k in rest]
    for tag, mine, from_sibling in (("first", first, second ^ 1), ("second", second, first ^ 1), ("far", far, far ^ 1)):
        token = wg.forward("gather_in_" + tag, [proj] + ahead)
        ahead = []
        if tag == "second":
            token = wg.relay_start("gather_in_far", ["w_in"], after=[token])
        elif tag == "far":
            token = wg.start("gather_rest", rest, (1,) + near, after=[token])
        proj = in_proj([mine], "in_proj_" + tag, proj, dep=token)
        wg.forwarded("gather_in_" + tag, [proj])
        proj = in_proj([from_sibling], "in_proj_%s_forwarded" % tag, proj)
    win_g = wg.buffers["w_in"]

    qn, kn = _qkv_prep(proj, small["g_q"], small["g_k"], w)
    token = wg.forward("gather_rest", [qn])
    token = wg.relay_start("gather_rest_far", rest, after=[token])
    outs, lses = [], []
    for c, dil in enumerate(DILATIONS):
        o, l = _attn_fwd(qn, kn, proj, bias[c], dil, "attn_fwd_d%d" % dil, dep=token)
        outs.append(o)
        lses.append(l)
    token = wg.forward("gather_rest_far", outs)

    ws_p = small["w_s"][:, CHUNK_ORDER][:, :, CHUNK_ORDER]
    bst = small["b_s"].T[CHUNK_ORDER]
    mix_args = (outs, lses, ws_p, bst, small["ln_v_g"], small["ln_v_b"], small["g_out_a"], small["g_out_b"], w)
    y = _mix_fwd(proj, *mix_args, dep=token)
    wg.forwarded("gather_rest", [y])
    wg.forwarded("gather_rest_far", [y])
    wout_g, wgate_g, wup_g = (wg.buffers[k] for k in rest)
    wout_f = wout_g.reshape(2 * w, d)
    wgate_f = wgate_g.reshape(d, d)

    tn = _tile(d, 1024)
    tk2 = 2 * w

    def resid_epilogue(acc, ex, outs_):
        outs_[0][...] = ex[0][...] + acc[...]

    h = _matmul(
        y, wout_f, name="out_proj", grid=(s // tm, d // tn, (2 * w) // tk2), dims=NN_DIMS,
        a_spec=pl.BlockSpec((tm, tk2), lambda i, j, k: (i, k)),
        b_spec=pl.BlockSpec((tk2, tn), lambda i, j, k: (k, j)),
        acc_shape=(tm, tn), out_shapes=[jax.ShapeDtypeStruct((s, d), F32)],
        out_specs=[pl.BlockSpec((tm, tn), lambda i, j, k: (i, j))],
        extra=(x,), extra_specs=(pl.BlockSpec((tm, tn), lambda i, j, k: (i, j)),),
        epilogue=resid_epilogue, vmem_mib=56)[0]

    hn2 = _rmsnorm_fwd(h, small["g_ple"], "ple_norm", False)

    dout, dpre, dup, loss_parts = _ple_gate(hn2, wgate_f, h, p, wup_g, tgt)
    loss = jnp.sum(loss_parts)

    tks = _tile(s, 2048)
    g_wup = _matmul(
        p, dup, name="grad_w_up", grid=(1, n, s // tks), dims=TN_DIMS,
        a_spec=pl.BlockSpec((tks, p_dim), lambda i, j, k: (k, 0)),
        b_spec=pl.BlockSpec((tks, c_up), lambda i, j, k: (k, j)),
        acc_shape=(p_dim, c_up), out_shapes=[jax.ShapeDtypeStruct((n, p_dim, c_up), BF16)],
        out_specs=[pl.BlockSpec((None, p_dim, c_up), lambda i, j, k: (j, 0, 0))])[0]

    def tn_matmul(a, b, name):
        m_, n_ = a.shape[1], b.shape[1]
        bm, bn = _tile(m_, 1024), _tile(n_, 1024)
        return _matmul(
            a, b, name=name, grid=(m_ // bm, n_ // bn, 1), dims=TN_DIMS,
            a_spec=pl.BlockSpec((s, bm), lambda i, j, k: (0, i)),
            b_spec=pl.BlockSpec((s, bn), lambda i, j, k: (0, j)),
            acc_shape=(bm, bn), out_shapes=[jax.ShapeDtypeStruct((m_, n_), BF16)],
            out_specs=[pl.BlockSpec((bm, bn), lambda i, j, k: (i, j))], vmem_mib=56)[0]

    def nt_matmul(a, b, name, out_dtype, dep=None):
        k_, n_ = a.shape[1], b.shape[0]
        bm, bn, bk = _tile(s, 1024), _tile(n_, 1024), k_
        return _matmul(
            a, b, name=name, grid=(s // bm, n_ // bn, k_ // bk), dims=NT_DIMS,
            a_spec=pl.BlockSpec((bm, bk), lambda i, j, k: (i, k)),
            b_spec=pl.BlockSpec((bn, bk), lambda i, j, k: (j, k)),
            acc_shape=(bm, bn), out_shapes=[jax.ShapeDtypeStruct((s, n_), out_dtype)],
            out_specs=[pl.BlockSpec((bm, bn), lambda i, j, k: (i, j))], dep=dep, vmem_mib=56)[0]

    by_core = lambda g: g.reshape((N_CHIP, 2) + g.shape[-2:])
    g_wgate = tn_matmul(hn2, dpre, "grad_w_gate").reshape(wgate_g.shape)
    dhn2 = nt_matmul(dpre, wgate_f, "ple_gate_bwd", BF16)
    dh_b, dh_bp, dg_ple = _rmsnorm_bwd(dhn2, h, small["g_ple"], dout, "ple_norm_bwd", False, True)
    g_wout = tn_matmul(y, dh_b, "grad_w_out").reshape(wout_g.shape)

    late = ("w_out", "w_ple_gate", "w_ple_up")
    late_parts = (g_wout, g_wgate, g_wup)
    token = ex.push_pairs("pair_late", [by_core(g) for g in late_parts])
    dy = nt_matmul(dh_bp, wout_f, "out_proj_bwd", BF16, dep=token)
    (dproj, dyb, lse_tot, delta, dws, dbst, dlng, dlnb, dga, dgb) = _mix_bwd(proj, dy, *mix_args)
    both_columns, from_sibling = ex.pairs_done("pair_late", [dproj])
    pair_sums = [_pair_add(mine.reshape((N_DEV,) + mine.shape[-2:]), theirs, "pair_add_" + k, ex.core)
                 for k, mine, theirs in zip(late, both_columns, from_sibling)]
    token = ex.push_chips("chip_late", pair_sums)

    running, dss = None, []
    for c, dil in enumerate(DILATIONS):
        *running, ds = _attn_bwd(qn, kn, proj, dyb, lse_tot, delta, bias[c], dil, "attn_bwd_d%d" % dil,
                                 running=running, dep=token if c == 0 else None)
        dss.append(ds)
    d_rel = _bias_grad(jnp.stack(dss), buckets, n_heads)
    dproj, dgq, dgk = _qkv_bwd(dproj, proj, *running, small["g_q"], small["g_k"], w)
    pair_sums, landed, _ = ex.chips_done("chip_late", [dproj])
    delivered = {k: (mine, theirs) for k, mine, theirs in zip(late, pair_sums, landed)}

    token_row = np.argsort(CHUNK_ORDER)
    dws = dws[:, token_row][:, :, token_row]
    dbst = dbst[token_row]
    small_grads = {
        "w_s": dws, "b_s": dbst.T, "ln_v_g": dlng, "ln_v_b": dlnb, "g_q": dgq, "g_k": dgk,
        "rel_bias": d_rel, "g_out_a": dga, "g_out_b": dgb, "g_ple": dg_ple,
    }

    bm = _tile(d, 1024)

    def grad_w_in(core, name, dep=None):
        return _matmul(
            hn, dproj, name=name, grid=(d // bm, N_CHIP, 1), dims=TN_DIMS, prefetch=core.reshape(1),
            a_spec=pl.BlockSpec((s, bm), lambda i, j, k, core_ref: (0, i)),
            b_spec=pl.BlockSpec((s, c_in), lambda i, j, k, core_ref: (0, 2 * j + core_ref[0])),
            acc_shape=(bm, c_in), out_shapes=[jax.ShapeDtypeStruct((N_CHIP, d, c_in), BF16)],
            out_specs=[pl.BlockSpec((None, bm, c_in), lambda i, j, k, core_ref: (j, i, 0))], dep=dep,
            vmem_mib=60)[0]

    for_sibling = grad_w_in(1 - ex.core, "grad_w_in_sibling")
    token = ex.push_pairs("pair_in", [for_sibling])
    mine = grad_w_in(ex.core, "grad_w_in_mine", dep=token)
    _, from_sibling = ex.pairs_done("pair_in", [mine])
    pair_sum = _pair_add(mine, from_sibling[0], "pair_add_w_in")
    token = ex.push_chips("chip_in", [pair_sum], _pack_small(small_grads, SMALL_EARLY))

    dhn = _matmul(
        dproj, win_g, name="in_proj_bwd", grid=(s // tm, d // tn, n // 2), dims=NT_DIMS,
        a_spec=pl.BlockSpec((tm, 2 * c_in), lambda i, j, k: (i, k)),
        b_spec=pl.BlockSpec((2, tn, c_in), lambda i, j, k: (k, j, 0)),
        acc_shape=(tm, tn), out_shapes=[jax.ShapeDtypeStruct((s, d), BF16)],
        out_specs=[pl.BlockSpec((tm, tn), lambda i, j, k: (i, j))], dep=token, vmem_mib=56)[0]
    grad_x, dg_pre = _rmsnorm_bwd(dhn, x, small["g_pre"], dh_b, "pre_norm_bwd", True, False)
    extra = while_last_travels(token, delivered, dg_pre) if while_last_travels is not None else []
    pair_sums, landed, slabs = ex.chips_done("chip_in", [grad_x] + list(extra))
    delivered["w_in"] = (pair_sums[0], landed[0])
    small_grads["g_pre"] = dg_pre
    return loss, grad_x, small_grads, delivered, slabs, extra


SMALL_EARLY = ("w_s", "b_s", "ln_v_g", "ln_v_b", "g_q", "g_k", "rel_bias", "g_out_a", "g_out_b", "g_ple")
SMALL_LAST = ("g_pre",)
SMALL_NAMES = SMALL_LAST + SMALL_EARLY


def _pack_small(tree, names):
    parts = []
    for name in names:
        flat = tree[name].astype(F32).reshape(-1)
        pad = (-flat.shape[0]) % HEAD
        parts.append(jnp.pad(flat, (0, pad)) if pad else flat)
    slab = jnp.concatenate(parts).reshape(-1, HEAD)
    pad_rows = (-slab.shape[0]) % 8
    return jnp.pad(slab, ((0, pad_rows), (0, 0))) if pad_rows else slab


def _unpack_small(slab, like, names):
    flat = slab.reshape(-1)
    out, off = {}, 0
    for name in names:
        size = like[name].size
        out[name] = flat[off:off + size].reshape(like[name].shape)
        off += size + (-size) % HEAD
    return out


def _peer(k):
    x, y, c = (lax.axis_index(a) for a in AXES)
    if k == "first":
        px, py, pc = x ^ (1 - c), y ^ c, c
    elif k == "second":
        px, py, pc = x ^ c, y ^ (1 - c), c
    else:
        bits = ((k >> 2) & 1, (k >> 1) & 1, k & 1)
        px, py, pc = (1 - v if b else v for v, b in zip((x, y, c), bits))
    return (px, py, pc), 4 * px + 2 * py + pc


def _my_index():
    x, y, c = (lax.axis_index(a) for a in AXES)
    return 4 * x + 2 * y + c


N_CHIP = 4
HBM_SPEC = pl.BlockSpec(memory_space=pl.ANY)


def _remote(src, dst, send_sem, recv_sem, peer):
    return pltpu.make_async_remote_copy(src_ref=src, dst_ref=dst, send_sem=send_sem, recv_sem=recv_sem,
                                        device_id=peer, device_id_type=pl.DeviceIdType.MESH)


SEM_SPEC = pl.BlockSpec(memory_space=pltpu.SEMAPHORE)
HBM_ONLY = pl.BlockSpec(memory_space=pltpu.HBM)
DATAFLOW = pltpu.SideEffectType.DATAFLOW_SIDE_EFFECTING


def _comm_call(name, arrays, *, wait=None, start=None, after=()):
    n, n_after = len(arrays), len(after)

    def body(*refs):
        ins = refs[:n]
        pos = n
        if wait is not None:
            for cp in wait[2](ins, refs[pos], refs[pos + 1]):
                cp.wait()
            pos += 2
        outs = refs[pos + n_after:]
        if start is not None:
            for cp in start[1](ins, outs[0], outs[1]):
                cp.start()
        outs[-1][...] = jnp.zeros_like(outs[-1])

    operands = [pltpu.with_memory_space_constraint(a, pltpu.HBM) for a in arrays]
    in_specs = [HBM_ONLY] * n
    if wait is not None:
        operands += [wait[0], wait[1]]
        in_specs += [SEM_SPEC, SEM_SPEC]
    operands += list(after)
    in_specs += [HBM_SPEC] * n_after
    out_shape, out_specs = [], []
    if start is not None:
        out_shape += [pltpu.SemaphoreType.DMA((start[0],))] * 2
        out_specs += [SEM_SPEC, SEM_SPEC]
    first = len(out_shape)
    out_shape += [pltpu.HBM(a.shape, a.dtype) for a in arrays] + [jax.ShapeDtypeStruct((SUB, HEAD), F32)]
    out_specs += [HBM_ONLY] * n + [pl.BlockSpec(memory_space=pltpu.VMEM)]
    res = pl.pallas_call(
        body, name=name, out_shape=tuple(out_shape), in_specs=tuple(in_specs), out_specs=tuple(out_specs),
        input_output_aliases={i: first + i for i in range(n)},
        compiler_params=pltpu.CompilerParams(has_side_effects=DATAFLOW),
    )(*operands)
    sems = (res[0], res[1]) if start is not None else None
    return list(res[first:first + n]), sems, res[-1]


class _GradExchange:
    def __init__(self):
        x, y, c = (lax.axis_index(a) for a in AXES)
        self.core = c.astype(jnp.int32)
        self.chip = (2 * x + y).astype(jnp.int32)
        self.pending = {}

    def _pair_copies(self, n_arr):
        def make(refs, send_sems, recv_sems):
            sibling, _ = _peer(1)
            other = 1 - lax.axis_index("c")
            srcs, lands = refs[:n_arr], refs[n_arr:]
            pick = lambda ref, ch: ref.at[ch, other] if len(ref.shape) == 4 else ref.at[ch]
            return [_remote(pick(srcs[a], ch), lands[a].at[ch], send_sems.at[a * N_CHIP + ch],
                            recv_sems.at[a * N_CHIP + ch], sibling)
                    for a in range(n_arr) for ch in range(N_CHIP)]
        return make

    def _chip_copies(self, n_arr, with_slab):
        def make(refs, send_sems, recv_sems):
            x, y = lax.axis_index("x"), lax.axis_index("y")
            my_chip = 2 * x + y
            srcs, lands = refs[:n_arr], refs[n_arr:2 * n_arr]
            copies = []
            for j, k in enumerate((6, 2, 4)):
                peer, peer_idx = _peer(k)
                for a in range(n_arr):
                    copies.append(_remote(srcs[a].at[peer_idx // 2], lands[a].at[my_chip],
                                          send_sems.at[3 * a + j], recv_sems.at[3 * a + j], peer))
            if with_slab:
                slab, slab_land = refs[2 * n_arr], refs[2 * n_arr + 1]
                for k in range(1, N_DEV):
                    peer, _ = _peer(k)
                    copies.append(_remote(slab, slab_land.at[_my_index()], send_sems.at[3 * n_arr + k - 1],
                                          recv_sems.at[3 * n_arr + k - 1], peer))
            return copies
        return make

    def push_pairs(self, tag, for_sibling):
        n_arr = len(for_sibling)
        lands = [lax.empty((N_CHIP,) + a.shape[-2:], a.dtype) for a in for_sibling]
        make = self._pair_copies(n_arr)
        arrays, sems, token = _comm_call(tag + "_start", list(for_sibling) + lands, start=(n_arr * N_CHIP, make))
        self.pending[tag] = (arrays, sems, make, n_arr)
        return token

    def pairs_done(self, tag, after):
        arrays, sems, make, n_arr = self.pending.pop(tag)
        arrays, _, _ = _comm_call(tag + "_wait", arrays, wait=(sems[0], sems[1], make), after=after)
        return arrays[:n_arr], arrays[n_arr:]

    def push_chips(self, tag, pair_sums, slab=None):
        n_arr = len(pair_sums)
        arrays = list(pair_sums) + [lax.empty(a.shape, a.dtype) for a in pair_sums]
        n_copies = 3 * n_arr
        if slab is not None:
            arrays += [slab, lax.empty((N_DEV,) + slab.shape, slab.dtype)]
            n_copies += N_DEV - 1
        make = self._chip_copies(n_arr, slab is not None)
        arrays, sems, token = _comm_call(tag + "_start", arrays, start=(n_copies, make))
        self.pending[tag] = (arrays, sems, make, n_arr)
        return token

    def chips_done(self, tag, after):
        arrays, sems, make, n_arr = self.pending.pop(tag)
        arrays, _, _ = _comm_call(tag + "_wait", arrays, wait=(sems[0], sems[1], make), after=after)
        return arrays[:n_arr], arrays[n_arr:2 * n_arr], arrays[2 * n_arr:]


def _cast_place(w, name):
    r, c = w.shape
    tr = r if r * c <= MIB else 1 << ((MIB // c).bit_length() - 1)
    assert r % tr == 0

    def body(me_ref, w_ref, o_ref):
        o_ref[...] = w_ref[...].astype(BF16)

    return pl.pallas_call(
        body, name=name, out_shape=jax.ShapeDtypeStruct((N_DEV, r, c), BF16),
        grid_spec=pltpu.PrefetchScalarGridSpec(
            num_scalar_prefetch=1, grid=(r // tr,),
            in_specs=[pl.BlockSpec((tr, c), lambda i, me_ref: (i, 0))],
            out_specs=pl.BlockSpec((None, tr, c), lambda i, me_ref: (me_ref[0], i, 0))),
        compiler_params=_params(("arbitrary",), 40),
    )(_my_index().astype(jnp.int32).reshape(1), w)


class _WeightGather:
    CHIPS = (2, 4, 6)

    def __init__(self, buffers):
        self.buffers = dict(buffers)
        self.pending = {}
        self.me = _my_index().astype(jnp.int32)

    def _own_slot_to(self, peers):
        def make(refs, send_sems, recv_sems):
            me = _my_index()
            return [_remote(ref.at[me], ref.at[me], send_sems.at[len(peers) * a + j],
                            recv_sems.at[len(peers) * a + j], _peer(k)[0])
                    for a, ref in enumerate(refs) for j, k in enumerate(peers)]
        return make

    def _forward_from(self, chips):
        def make(refs, send_sems, recv_sems):
            sibling, _ = _peer(1)
            copies = []
            for a, ref in enumerate(refs):
                for j, k in enumerate(chips):
                    slot = ref.at[_peer(k)[1]]
                    copies.append(_remote(slot, slot, send_sems.at[len(chips) * a + j],
                                          recv_sems.at[len(chips) * a + j], sibling))
            return copies
        return make

    def _run(self, call, names, **kw):
        arrays, sems, token = _comm_call(call, [self.buffers[k] for k in names], **kw)
        self.buffers.update(zip(names, arrays))
        return sems, token

    def start(self, tag, names, peers, after=()):
        make = self._own_slot_to(peers)
        sems, token = self._run(tag + "_start", names, start=(len(names) * len(peers), make), after=after)
        self.pending[tag] = (names, sems, make, peers)
        return token

    @staticmethod
    def _relay(refs, send_sems, recv_sems):
        x, y, c = (lax.axis_index(a) for a in AXES)
        peer = (x ^ (1 - c), y ^ c, c)
        slot = _my_index() ^ (2 + 2 * c)
        return [_remote(ref.at[slot], ref.at[slot], send_sems.at[a], recv_sems.at[a], peer)
                for a, ref in enumerate(refs)]

    def relay_start(self, tag, names, after=()):
        sems, token = self._run(tag + "_start", names, start=(len(names), self._relay), after=after)
        self.pending[tag] = (names, sems, self._relay, (6,))
        return token

    def arrived(self, tag, after):
        names, sems, make, _ = self.pending.pop(tag)
        self._run(tag + "_wait", names, wait=(sems[0], sems[1], make), after=after)

    def forward(self, tag, after):
        names, sems, make, peers = self.pending.pop(tag)
        chips = tuple(k for k in peers if k != 1)
        onward = self._forward_from(chips)
        new_sems, token = self._run(tag + "_forward", names, wait=(sems[0], sems[1], make),
                                    start=(len(chips) * len(names), onward), after=after)
        self.pending[tag + "/fwd"] = (names, new_sems, onward)
        return token

    def forwarded(self, tag, after):
        names, sems, make = self.pending.pop(tag + "/fwd")
        self._run(tag + "_done", names, wait=(sems[0], sems[1], make), after=after)


def _pair_add(mine, theirs, name, core=None):
    _, r, c_dim = theirs.shape
    tr = r if r * c_dim <= MIB else 1 << ((MIB // c_dim).bit_length() - 1)
    assert r % tr == 0
    stride = 1 if core is None else 2
    offset = jnp.zeros((1,), jnp.int32) if core is None else core.reshape(1)

    def body(off_ref, a_ref, b_ref, o_ref):
        o_ref[...] = (a_ref[...].astype(F32) + b_ref[...].astype(F32)).astype(BF16)

    blk = (None, tr, c_dim)
    return pl.pallas_call(
        body, name=name, out_shape=jax.ShapeDtypeStruct(theirs.shape, BF16),
        grid_spec=pltpu.PrefetchScalarGridSpec(
            num_scalar_prefetch=1, grid=(N_CHIP, r // tr),
            in_specs=[pl.BlockSpec(blk, lambda ch, i, off_ref: (stride * ch + off_ref[0], i, 0)),
                      pl.BlockSpec(blk, lambda ch, i, off_ref: (ch, i, 0))],
            out_specs=pl.BlockSpec(blk, lambda ch, i, off_ref: (ch, i, 0))),
        compiler_params=_params(("arbitrary", "arbitrary"), 40),
    )(offset, mine, theirs)


def _slab_exchange(slab, name):
    def body(slab_in, slab_out, send_sems, recv_sems, local_sem):
        me = _my_index()
        local = pltpu.make_async_copy(slab_in, slab_out.at[me], local_sem)
        local.start()
        sends = []
        for k in range(1, N_DEV):
            peer, _ = _peer(k)
            sends.append(_remote(slab_in, slab_out.at[me], send_sems.at[k - 1], recv_sems.at[k - 1], peer))
        for cp in sends:
            cp.start()
        for k in range(1, N_DEV):
            peer, peer_idx = _peer(k)
            slot = slab_out.at[peer_idx]
            _remote(slot, slot, send_sems.at[k - 1], recv_sems.at[k - 1], peer).wait_recv()
        for cp in sends:
            cp.wait_send()
        local.wait()

    return pl.pallas_call(
        body, name=name, out_shape=jax.ShapeDtypeStruct((N_DEV,) + slab.shape, slab.dtype),
        in_specs=[HBM_SPEC], out_specs=HBM_SPEC,
        scratch_shapes=[pltpu.SemaphoreType.DMA((N_DEV - 1,)), pltpu.SemaphoreType.DMA((N_DEV - 1,)),
                        pltpu.SemaphoreType.DMA],
        compiler_params=pltpu.CompilerParams(has_side_effects=True),
    )(slab)


def _adamw_math(w, g, m, v):
    m = ADAM_B1 * m + (1.0 - ADAM_B1) * g
    v = ADAM_B2 * v + (1.0 - ADAM_B2) * (g * g)
    m_hat = m / (1.0 - ADAM_B1 ** ADAM_STEP)
    v_hat = v / (1.0 - ADAM_B2 ** ADAM_STEP)
    delta = -ADAM_LR * (m_hat / (jnp.sqrt(v_hat) + ADAM_EPS) + ADAM_WD * w)
    return delta, m, v


def _adamw(parts, own, place, w, m, v, name, dep=None):
    n_parts = parts.shape[0]
    r, c = w.shape
    budget = 280 * 1024
    tr = r if r * c <= budget else 1 << ((budget // c).bit_length() - 1)
    assert r % tr == 0

    def body(place_ref, p_ref, own_ref, w_ref, m_ref, v_ref, g_ref, d_ref, nm_ref, nv_ref):
        mine = own_ref[...].astype(F32)
        g = None
        for i in range(n_parts):
            term = jnp.where(place_ref[0] == i, mine, p_ref[i].astype(F32))
            g = term if g is None else g + term
        delta, nm, nv = _adamw_math(w_ref[...], g, m_ref[...], v_ref[...])
        g_ref[...] = g
        d_ref[...] = delta
        nm_ref[...] = nm
        nv_ref[...] = nv

    blk = pl.BlockSpec((tr, c), lambda i, place_ref: (i, 0))
    shape = jax.ShapeDtypeStruct((r, c), F32)
    in_specs = [pl.BlockSpec((n_parts, tr, c), lambda i, place_ref: (0, i, 0)),
                pl.BlockSpec((None, tr, c), lambda i, place_ref: (place_ref[1], i, 0)), blk, blk, blk]
    operands = [parts, own, w, m, v]
    if dep is not None:
        body = _drop_arg(body, 1 + len(operands))
        in_specs.append(pl.BlockSpec((SUB, HEAD), lambda i, place_ref: (0, 0)))
        operands.append(dep)
    return pl.pallas_call(
        body, name=name, out_shape=[shape] * 4,
        grid_spec=pltpu.PrefetchScalarGridSpec(
            num_scalar_prefetch=1, grid=(r // tr,), in_specs=in_specs, out_specs=[blk] * 4),
        compiler_params=_params(("arbitrary",), 48),
    )(place, *operands)


def kernel(x, p, g_pre, w_in, w_s, b_s, ln_v_g, ln_v_b, g_q, g_k, rel_bias, g_out_a, g_out_b, w_out, g_ple, w_ple_gate, w_ple_up, loss_target, m_g_pre, m_w_in, m_w_s, m_b_s, m_ln_v_g, m_ln_v_b, m_g_q, m_g_k, m_rel_bias, m_g_out_a, m_g_out_b, m_w_out, m_g_ple, m_w_ple_gate, m_w_ple_up, v_g_pre, v_w_in, v_w_s, v_b_s, v_ln_v_g, v_ln_v_b, v_g_q, v_g_k, v_rel_bias, v_g_out_a, v_g_out_b, v_w_out, v_g_ple, v_w_ple_gate, v_w_ple_up):
    args = dict(locals())
    small = {"g_pre": g_pre, "w_s": w_s[0], "b_s": b_s[0], "ln_v_g": ln_v_g, "ln_v_b": ln_v_b, "g_q": g_q,
             "g_k": g_k, "rel_bias": rel_bias, "g_out_a": g_out_a, "g_out_b": g_out_b, "g_ple": g_ple}
    big_names = ("w_in", "w_out", "w_ple_gate", "w_ple_up")
    big = {k: args[k][0] for k in big_names}

    wg = _WeightGather({k: _cast_place(big[k], "place_" + k) for k in big_names})
    ex = _GradExchange()
    results = {}

    def big_adamw(k, delivered, dep=None):
        mine, theirs = delivered[k]
        place = jnp.stack([ex.chip, ex.chip])
        return _adamw(theirs, mine, place, big[k], args["m_" + k][0], args["v_" + k][0], "adamw_" + k, dep=dep)

    squeeze = lambda t: {k: (t[k][0] if k in ("w_s", "b_s") else t[k]) for k in SMALL_NAMES}
    small_m = squeeze({k: args["m_" + k] for k in SMALL_NAMES})
    small_v = squeeze({k: args["v_" + k] for k in SMALL_NAMES})
    slab_place = jnp.stack([_my_index().astype(jnp.int32), jnp.zeros((), jnp.int32)])

    def small_adamw(names_, parts, own, call_name):
        packed = _adamw(parts, own[None], slab_place, _pack_small(small, names_), _pack_small(small_m, names_),
                        _pack_small(small_v, names_), call_name)
        for idx in range(4):
            tree = _unpack_small(packed[idx], small, names_)
            for k in names_:
                results.setdefault(k, [None] * 4)[idx] = tree[k].reshape(args[k].shape)
        return packed[0]

    def while_last_travels(token, delivered, dg_pre):
        done = []
        for k in big_names[1:]:
            results[k] = big_adamw(k, delivered, dep=token)
            done.append(results[k][0])
        last_slab = _pack_small({"g_pre": dg_pre}, SMALL_LAST)
        done.append(small_adamw(SMALL_LAST, _slab_exchange(last_slab, "last_exchange"), last_slab, "adamw_last"))
        return done

    loss, grad_x, small_parts, delivered, slabs, _ = _local_step(
        x[0], p[0, 0], loss_target[0], small, wg, ex, while_last_travels)
    results["w_in"] = big_adamw("w_in", delivered)
    for k in big_names:
        results[k] = [t[None] for t in results[k]]
    small_adamw(SMALL_EARLY, slabs[1], slabs[0], "adamw_small")

    names = ("g_pre", "w_in", "w_s", "b_s", "ln_v_g", "ln_v_b", "g_q", "g_k", "rel_bias", "g_out_a", "g_out_b",
             "w_out", "g_ple", "w_ple_gate", "w_ple_up")
    total = lax.psum(loss, AXES)
    out = [total, grad_x[None]]
    for idx in range(4):
        out += [results[k][idx] for k in names]
    return tuple(out)
```

```python
import math

import numpy as np
import jax
import jax.numpy as jnp
from jax import lax
from jax.experimental import pallas as pl
from jax.experimental.pallas import tpu as pltpu

F32 = jnp.float32
BF16 = jnp.bfloat16
EPS = 1e-6
NEG_INF = -1e30
HEAD = 128
DILATIONS = (1, 4, 16)
NUM_BUCKETS = 32
MAX_DISTANCE = 2048
N_SEG = 7
ADAM_LR = 0.001
ADAM_B1 = 0.9
ADAM_B2 = 0.999
ADAM_EPS = 1e-08
ADAM_WD = 0.01
ADAM_STEP = 10
AXES = ("x", "y", "c")
N_DEV = 8
MIB = 1 << 20

SUB = 8

CHUNK_ORDER = np.array([16 * (r % SUB) + r // SUB for r in range(HEAD)])
BLOCK_ORDER = {
    1: CHUNK_ORDER,
    4: np.array([32 * (r // 32) + 4 * (r % SUB) + (r // SUB) % 4 for r in range(HEAD)]),
    16: np.arange(HEAD),
}

NT_DIMS = (((1,), (1,)), ((), ()))
TN_DIMS = (((0,), (0,)), ((), ()))
NN_DIMS = (((1,), (0,)), ((), ()))


def _params(semantics, vmem_mib):
    return pltpu.CompilerParams(dimension_semantics=semantics, vmem_limit_bytes=vmem_mib * MIB)


def _gelu(a):
    return 0.5 * a * (1.0 + lax.erf(a * (2.0 ** -0.5)))


def _gelu_and_grad(a):
    cdf = 0.5 * (1.0 + lax.erf(a * (2.0 ** -0.5)))
    return a * cdf, cdf + a * jnp.exp(-0.5 * a * a) * ((2.0 * math.pi) ** -0.5)


def _silu_and_grad(a):
    s = jax.nn.sigmoid(a)
    return a * s, s * (1.0 + a * (1.0 - s))


def _rms(v):
    return lax.rsqrt(jnp.mean(v * v, axis=-1, keepdims=True) + EPS)


def _rms_bwd(dy, v, r, g):
    gy = dy * g
    return r * gy - v * (r * r * r) * jnp.mean(gy * v, axis=-1, keepdims=True)


def _dot(a, b, dims=NN_DIMS):
    return lax.dot_general(a, b, dims, preferred_element_type=F32)


def _lane_pick(cols, width):
    rows = cols[0].shape[0]
    lane = lax.broadcasted_iota(jnp.int32, (rows, width), 1)
    out = jnp.zeros((rows, width), F32)
    for h, col in enumerate(cols):
        out = jnp.where(lane == h, col, out)
    return out


def _chunk_perm():
    return jnp.asarray(np.eye(HEAD, dtype=np.float32)[CHUNK_ORDER], BF16)


def _unpermute_f32(p, v):
    hi = v.astype(BF16)
    rest = v - hi.astype(F32)
    mid = rest.astype(BF16)
    lo = (rest - mid.astype(F32)).astype(BF16)
    return _dot(p, hi, TN_DIMS) + _dot(p, mid, TN_DIMS) + _dot(p, lo, TN_DIMS)


def _by_chunk(fn, v):
    return jnp.concatenate([fn(v[c * HEAD:(c + 1) * HEAD]) for c in range(v.shape[0] // HEAD)], axis=0)


def _rmsnorm_fwd(x, g, name, permute, dep=None):
    s, d = x.shape
    tm = _tile(s, 2 * HEAD)

    def body(x_ref, g_ref, p_ref, o_ref):
        v = x_ref[...]
        out = (v * _rms(v) * g_ref[...]).astype(BF16)
        if permute:
            out = _by_chunk(lambda t: _dot(p_ref[...], t).astype(BF16), out)
        o_ref[...] = out

    in_specs = [pl.BlockSpec((tm, d), lambda i: (i, 0)), pl.BlockSpec((1, d), lambda i: (0, 0)),
                pl.BlockSpec((HEAD, HEAD), lambda i: (0, 0))]
    operands = [x, g, _chunk_perm()]
    if dep is not None:
        body = _drop_arg(body, len(operands))
        in_specs.append(DEP_SPEC)
        operands.append(dep)
    return pl.pallas_call(
        body, name=name, grid=(s // tm,),
        out_shape=jax.ShapeDtypeStruct((s, d), BF16), in_specs=in_specs,
        out_specs=pl.BlockSpec((tm, d), lambda i: (i, 0)),
        compiler_params=_params(("arbitrary",), 40),
    )(*operands)


def _rmsnorm_bwd(dy, v, g, res, name, dy_permuted, with_bf16):
    s, d = v.shape
    tm = _tile(s, 2 * HEAD)
    perm = _chunk_perm()

    def body(dy_ref, v_ref, g_ref, res_ref, p_ref, *outs):
        dg_ref = outs[-1]

        @pl.when(pl.program_id(0) == 0)
        def _():
            dg_ref[...] = jnp.zeros_like(dg_ref)

        vv, dyv = v_ref[...], dy_ref[...]
        if dy_permuted:
            undo = _unpermute_f32 if dyv.dtype == F32 else (lambda p_, t: _dot(p_, t, TN_DIMS))
            dyv = _by_chunk(lambda t: undo(p_ref[...], t), dyv)
        dyv = dyv.astype(F32)
        r = _rms(vv)
        dx = res_ref[...].astype(F32) + _rms_bwd(dyv, vv, r, g_ref[...])
        if with_bf16:
            dxb = dx.astype(BF16)
            outs[0][...] = dxb
            outs[1][...] = _by_chunk(lambda t: _dot(p_ref[...], t).astype(BF16), dxb)
        else:
            outs[0][...] = dx
        dg_ref[...] += jnp.sum(dyv * vv * r, axis=0, keepdims=True)

    row = pl.BlockSpec((tm, d), lambda i: (i, 0))
    vec = pl.BlockSpec((1, d), lambda i: (0, 0))
    if with_bf16:
        shapes = [jax.ShapeDtypeStruct((s, d), BF16)] * 2
        specs = [row, row]
    else:
        shapes = [jax.ShapeDtypeStruct((s, d), F32)]
        specs = [row]
    shapes.append(jax.ShapeDtypeStruct((1, d), F32))
    specs.append(vec)
    return pl.pallas_call(
        body, name=name, grid=(s // tm,), out_shape=shapes,
        in_specs=[row, row, vec, row, pl.BlockSpec((HEAD, HEAD), lambda i: (0, 0))], out_specs=specs,
        compiler_params=_params(("arbitrary",), 56),
    )(dy, v, g, res, perm)


DEP_SPEC = pl.BlockSpec((SUB, HEAD), lambda *_: (0, 0))


def _drop_arg(body, pos):
    return lambda *refs: body(*refs[:pos], *refs[pos + 1:])


def _matmul(a, b, *, name, grid, a_spec, b_spec, dims, acc_shape, out_shapes, out_specs,
            extra=(), extra_specs=(), epilogue=None, vmem_mib=48, dep=None, prefetch=None, carry=None):
    nk = grid[2]
    n_user = len(extra)
    for unread, spec in ((dep, DEP_SPEC), (carry, HBM_SPEC)):
        if unread is not None:
            extra, extra_specs = tuple(extra) + (unread,), tuple(extra_specs) + (spec,)
    n_extra, n_out = len(extra), len(out_shapes)
    n_pre = 0 if prefetch is None else 1
    aliases = {} if carry is None else {n_pre + 2 + n_extra - 1: 0}

    def body(*refs):
        refs = refs[n_pre:]
        a_ref, b_ref = refs[0], refs[1]
        ex = refs[2:2 + n_user]
        outs = refs[2 + n_extra:2 + n_extra + n_out]
        def lhs():
            av = a_ref[...]
            return av if av.dtype == BF16 else av.astype(BF16)

        if nk == 1 and epilogue is None:
            outs[0][...] = _dot(lhs(), b_ref[...], dims).astype(outs[0].dtype)
            return
        acc = refs[-1]
        k = pl.program_id(2)

        @pl.when(k == 0)
        def _():
            acc[...] = jnp.zeros_like(acc)

        av = lhs()
        if len(b_ref.shape) == 3:
            span = b_ref.shape[2]
            acc[...] += sum(_dot(av[:, g * span:(g + 1) * span], b_ref[g], dims) for g in range(b_ref.shape[0]))
        else:
            acc[...] += _dot(av, b_ref[...], dims)

        @pl.when(k == nk - 1)
        def _():
            if epilogue is None:
                outs[0][...] = acc[...].astype(outs[0].dtype)
            else:
                epilogue(acc, ex, outs)

    direct = nk == 1 and epilogue is None
    scratch = [] if direct else [pltpu.VMEM(acc_shape, F32)]
    params = _params(("parallel", "parallel", "arbitrary"), vmem_mib)
    if prefetch is None:
        return pl.pallas_call(
            body, name=name, grid=grid, out_shape=list(out_shapes),
            in_specs=[a_spec, b_spec, *extra_specs], out_specs=list(out_specs),
            scratch_shapes=scratch, compiler_params=params, input_output_aliases=aliases,
        )(a, b, *extra)
    return pl.pallas_call(
        body, name=name, out_shape=list(out_shapes),
        grid_spec=pltpu.PrefetchScalarGridSpec(
            num_scalar_prefetch=1, grid=grid, in_specs=[a_spec, b_spec, *extra_specs],
            out_specs=list(out_specs), scratch_shapes=scratch),
        compiler_params=params, input_output_aliases=aliases,
    )(prefetch, a, b, *extra)


def _tile(n, want):
    t = min(n, want)
    while n % t:
        t //= 2
    return t


def _rel_buckets(dil):
    order = BLOCK_ORDER[dil]
    qi = jnp.asarray(HEAD + order)
    kj = jnp.asarray(np.concatenate([order, HEAD + order]))
    delta = qi[:, None] - kj[None, :]
    band = (delta >= 0) & (delta <= HEAD)
    dist = jnp.clip(delta, 0, None) * dil
    max_exact = NUM_BUCKETS // 2
    dd = jnp.maximum(dist, 1).astype(F32)
    large = max_exact + (jnp.log(dd / max_exact) / math.log(MAX_DISTANCE / max_exact)
                         * (NUM_BUCKETS - max_exact)).astype(jnp.int32)
    large = jnp.minimum(large, NUM_BUCKETS - 1)
    bucket = jnp.where(dist < max_exact, dist, large)
    return jnp.where(band, bucket, -1).astype(jnp.int32)


def _bias_build(rel_bias, buckets, n_heads):
    nd = buckets.shape[0]

    def body(rb_ref, bk_ref, o_ref):
        for c in range(nd):
            def per_head(h, carry, c=c):
                bk = bk_ref[c]
                acc = jnp.where(bk < 0, NEG_INF, 0.0).astype(F32)
                for b in range(NUM_BUCKETS):
                    acc = jnp.where(bk == b, rb_ref[b, h], acc)
                o_ref[c, h] = acc
                return carry

            lax.fori_loop(0, n_heads, per_head, 0)

    return pl.pallas_call(
        body, name="bias_build",
        out_shape=jax.ShapeDtypeStruct((nd, n_heads, HEAD, 2 * HEAD), F32),
        in_specs=[pl.BlockSpec(memory_space=pltpu.SMEM), pl.BlockSpec(memory_space=pltpu.VMEM)],
        out_specs=pl.BlockSpec(memory_space=pltpu.VMEM),
    )(rel_bias, buckets)


def _bias_grad(ds_all, buckets, n_heads):
    nd = buckets.shape[0]
    pairs = HEAD * 2 * HEAD

    def body(ds_ref, bk_ref, o_ref):
        rows = lax.broadcasted_iota(jnp.int32, (NUM_BUCKETS, pairs), 0)
        tot = jnp.zeros((n_heads, NUM_BUCKETS), F32)
        for c in range(nd):
            onehot = (rows == bk_ref[c]).astype(BF16)
            ds = ds_ref[c]
            hi = ds.astype(BF16)
            lo = (ds - hi.astype(F32)).astype(BF16)
            tot = tot + _dot(hi, onehot, NT_DIMS) + _dot(lo, onehot, NT_DIMS)
        o_ref[...] = tot

    out = pl.pallas_call(
        body, name="bias_grad",
        out_shape=jax.ShapeDtypeStruct((n_heads, NUM_BUCKETS), F32),
        in_specs=[pl.BlockSpec(memory_space=pltpu.VMEM), pl.BlockSpec(memory_space=pltpu.VMEM)],
        out_specs=pl.BlockSpec(memory_space=pltpu.VMEM),
        compiler_params=pltpu.CompilerParams(vmem_limit_bytes=40 * MIB),
    )(ds_all.reshape(nd, n_heads, pairs), buckets.reshape(nd, 1, pairs))
    return out.T


def _qkv_prep(proj, g_q, g_k, w, dep=None):
    s = proj.shape[0]
    n_heads = w // HEAD
    tm = HEAD

    def body(q_ref, k_ref, gq_ref, gk_ref, qn_ref, kn_ref):
        gq = gq_ref[...] * (HEAD ** -0.5)
        gk = gk_ref[...]
        for h in range(n_heads):
            sl = slice(h * HEAD, (h + 1) * HEAD)
            q = q_ref[:, sl]
            k = k_ref[:, sl]
            qn_ref[:, sl] = q * _rms(q) * gq
            kn_ref[:, sl] = k * _rms(k) * gk

    seg = lambda j: pl.BlockSpec((tm, w), lambda i, j=j: (i, j))
    vec = pl.BlockSpec((1, HEAD), lambda i: (0, 0))
    out = pl.BlockSpec((tm, w), lambda i: (i, 0))
    in_specs = [seg(3), seg(4), vec, vec]
    operands = [proj, proj, g_q, g_k]
    if dep is not None:
        body = _drop_arg(body, len(operands))
        in_specs.append(DEP_SPEC)
        operands.append(dep)
    return pl.pallas_call(
        body, name="qkv_prep", grid=(s // tm,),
        out_shape=[jax.ShapeDtypeStruct((s, w), F32)] * 2,
        in_specs=in_specs, out_specs=[out, out],
        compiler_params=_params(("arbitrary",), 40),
    )(*operands)


class _BlockView:
    def __init__(self, s, dil):
        assert s % (HEAD * dil) == 0 and dil in BLOCK_ORDER
        self.nb = s // (HEAD * dil)
        if dil == 1:
            self.lead, self.block = (s,), (HEAD,)
            self.index = lambda r, n: (n,)
        elif dil == 4:
            self.lead, self.block = (s // 512, 4, 4, 4, SUB), (None, 4, 4, None, SUB)
            self.index = lambda r, n: (n, 0, 0, r, 0)
        else:
            self.lead, self.block = (s // 2048, 16, 16, SUB), (None, 16, None, SUB)
            self.index = lambda r, n: (n, 0, r, 0)

    def view(self, t):
        return t.reshape(self.lead + (t.shape[-1],))

    def spec(self, width, block_of, column=0):
        return pl.BlockSpec(self.block + (width,), lambda r, n: self.index(r, block_of(r, n)) + (column,))


def _rows(ref, lanes=slice(None)):
    v = ref[(slice(None),) * (len(ref.shape) - 1) + (lanes,)]
    return v.reshape(HEAD, v.shape[-1])


def _set_rows(ref, lanes, value):
    ref[(slice(None),) * (len(ref.shape) - 1) + (lanes,)] = value.reshape(ref.shape[:-1] + (value.shape[-1],))


V_SEGMENT = 5


def _attn_fwd(qn, kn, proj, bias, dil, name, dep=None):
    s, w = qn.shape
    n_heads = w // HEAD
    bv = _BlockView(s, dil)

    def body(q_ref, kc_ref, vc_ref, bias_ref, o_ref, lse_ref, s_scr, e_scr, lse_scr, inv_scr, k_prev, v_prev):
        n = pl.program_id(1)
        heads = [slice(h * HEAD, (h + 1) * HEAD) for h in range(n_heads)]
        lse_scr[...] = jnp.zeros_like(lse_scr)

        @pl.when(n == 0)
        def _():
            k_prev[...] = jnp.zeros_like(k_prev)
            v_prev[...] = jnp.zeros_like(v_prev)

        for h, sl in enumerate(heads):
            q = _rows(q_ref, sl).astype(BF16)
            s_p = _dot(q, k_prev[:, sl], NT_DIMS) + bias_ref[h, :, :HEAD]
            s_scr[h, :, :HEAD] = jnp.where(n > 0, s_p, NEG_INF)
            s_scr[h, :, HEAD:] = _dot(q, _rows(kc_ref, sl).astype(BF16), NT_DIMS) + bias_ref[h, :, HEAD:]
        for h in range(n_heads):
            sc = s_scr[h]
            m = jnp.max(sc, axis=-1, keepdims=True)
            e = jnp.exp(sc - m)
            den = jnp.sum(e, axis=-1, keepdims=True)
            e_scr[h] = e.astype(BF16)
            lse_scr[:, h:h + 1] = m + jnp.log(den)
            inv_scr[:, h:h + 1] = 1.0 / den
        for h, sl in enumerate(heads):
            v_cur = _rows(vc_ref, sl).astype(BF16)
            o = _dot(e_scr[h, :, :HEAD], v_prev[:, sl]) + _dot(e_scr[h, :, HEAD:], v_cur)
            _set_rows(o_ref, sl, o * inv_scr[:, h:h + 1])
            v_prev[:, sl] = v_cur
            k_prev[:, sl] = _rows(kc_ref, sl).astype(BF16)
        _set_rows(lse_ref, slice(None), lse_scr[...])

    cur = bv.spec(w, lambda r, n: n)
    in_specs = [cur, cur, bv.spec(w, lambda r, n: n, V_SEGMENT),
                pl.BlockSpec((n_heads, HEAD, 2 * HEAD), lambda r, n: (0, 0, 0))]
    operands = [bv.view(qn), bv.view(kn), bv.view(proj), bias]
    if dep is not None:
        body = _drop_arg(body, len(operands))
        in_specs.append(DEP_SPEC)
        operands.append(dep)
    o, lse = pl.pallas_call(
        body, name=name, grid=(dil, bv.nb),
        out_shape=[jax.ShapeDtypeStruct(bv.lead + (w,), F32), jax.ShapeDtypeStruct(bv.lead + (HEAD,), F32)],
        in_specs=in_specs,
        out_specs=[cur, bv.spec(HEAD, lambda r, n: n)],
        scratch_shapes=[pltpu.VMEM((n_heads, HEAD, 2 * HEAD), F32), pltpu.VMEM((n_heads, HEAD, 2 * HEAD), BF16),
                        pltpu.VMEM((HEAD, HEAD), F32), pltpu.VMEM((HEAD, HEAD), F32),
                        pltpu.VMEM((HEAD, w), BF16), pltpu.VMEM((HEAD, w), BF16)],
        compiler_params=_params(("arbitrary", "arbitrary"), 48),
    )(*operands)
    return o.reshape(s, w), lse.reshape(s, HEAD)


def _attn_bwd(qn, kn, proj, dyb, lse, delta, bias, dil, name, running=None, dep=None):
    s, w = qn.shape
    n_heads = w // HEAD
    bv = _BlockView(s, dil)
    nb = bv.nb

    n_run = 0 if running is None else 3

    def body(q_ref, kc_ref, kp_ref, vc_ref, vp_ref, dy_ref, lse_ref, dl_ref, bias_ref, *rest):
        so_far = rest[:n_run]
        dq_ref, dk_ref, dv_ref, ds_ref, carry_k, carry_v, s_scr, dp_scr, p_scr, dsb_scr, k_cur, v_cur = rest[n_run:]
        base = (lambda i, sl: _rows(so_far[i], sl)) if n_run else (lambda i, sl: 0.0)
        r = pl.program_id(0)
        step = pl.program_id(1)
        blk = nb - 1 - step

        @pl.when((r == 0) & (step == 0))
        def _():
            ds_ref[...] = jnp.zeros_like(ds_ref)

        @pl.when(step == 0)
        def _():
            carry_k[...] = jnp.zeros_like(carry_k)
            carry_v[...] = jnp.zeros_like(carry_v)
            k_cur[...] = _rows(kc_ref).astype(BF16)
            v_cur[...] = _rows(vc_ref).astype(BF16)

        heads = [slice(h * HEAD, (h + 1) * HEAD) for h in range(n_heads)]
        tots = _rows(lse_ref)
        dls = _rows(dl_ref)
        for h, sl in enumerate(heads):
            q, dy = _rows(q_ref, sl).astype(BF16), _rows(dy_ref, sl).astype(BF16)
            kp, kc = _rows(kp_ref, sl).astype(BF16), k_cur[:, sl]
            vp, vc = _rows(vp_ref, sl).astype(BF16), v_cur[:, sl]
            s_p = _dot(q, kp, NT_DIMS) + bias_ref[h, :, :HEAD]
            s_scr[h, :, :HEAD] = jnp.where(blk > 0, s_p, NEG_INF)
            s_scr[h, :, HEAD:] = _dot(q, kc, NT_DIMS) + bias_ref[h, :, HEAD:]
            dp_scr[h, :, :HEAD] = _dot(dy, vp, NT_DIMS)
            dp_scr[h, :, HEAD:] = _dot(dy, vc, NT_DIMS)
        for h in range(n_heads):
            prob = jnp.exp(s_scr[h] - tots[:, h:h + 1])
            ds = prob * (dp_scr[h] - dls[:, h:h + 1])
            ds_ref[h] += ds
            p_scr[h] = prob.astype(BF16)
            dsb_scr[h] = ds.astype(BF16)
        for h, sl in enumerate(heads):
            q, dy = _rows(q_ref, sl).astype(BF16), _rows(dy_ref, sl).astype(BF16)
            kp, kc = _rows(kp_ref, sl).astype(BF16), k_cur[:, sl]
            ds_pb, ds_cb = dsb_scr[h, :, :HEAD], dsb_scr[h, :, HEAD:]
            _set_rows(dq_ref, sl, _dot(ds_pb, kp) + _dot(ds_cb, kc) + base(0, sl))
            _set_rows(dk_ref, sl, _dot(ds_cb, q, TN_DIMS) + carry_k[:, sl] + base(1, sl))
            carry_k[:, sl] = _dot(ds_pb, q, TN_DIMS)
            _set_rows(dv_ref, sl, _dot(p_scr[h, :, HEAD:], dy, TN_DIMS) + carry_v[:, sl] + base(2, sl))
            carry_v[:, sl] = _dot(p_scr[h, :, :HEAD], dy, TN_DIMS)
            k_cur[:, sl] = kp
            v_cur[:, sl] = _rows(vp_ref, sl).astype(BF16)

    cur = bv.spec(w, lambda r, n: nb - 1 - n)
    last = bv.spec(w, lambda r, n: nb - 1)
    prev = bv.spec(w, lambda r, n: jnp.maximum(nb - 2 - n, 0))
    stat = bv.spec(HEAD, lambda r, n: nb - 1 - n)
    whole = pl.BlockSpec((n_heads, HEAD, 2 * HEAD), lambda r, n: (0, 0, 0))
    big = jax.ShapeDtypeStruct(bv.lead + (w,), F32)
    v_last = bv.spec(w, lambda r, n: nb - 1, V_SEGMENT)
    v_prev = bv.spec(w, lambda r, n: jnp.maximum(nb - 2 - n, 0), V_SEGMENT)
    in_specs = [cur, last, prev, v_last, v_prev, cur, stat, stat, whole]
    operands = [bv.view(qn), bv.view(kn), bv.view(kn), bv.view(proj), bv.view(proj), bv.view(dyb), bv.view(lse),
                bv.view(delta), bias]
    aliases = {}
    if running is not None:
        aliases = {len(operands) + i: i for i in range(3)}
        in_specs += [cur] * 3
        operands += [bv.view(t) for t in running]
    if dep is not None:
        body = _drop_arg(body, len(operands))
        in_specs.append(DEP_SPEC)
        operands.append(dep)
    dq, dk, dv, ds = pl.pallas_call(
        body, name=name, grid=(dil, nb),
        out_shape=[big, big, big, jax.ShapeDtypeStruct((n_heads, HEAD, 2 * HEAD), F32)],
        in_specs=in_specs, out_specs=[cur, cur, cur, whole], input_output_aliases=aliases,
        scratch_shapes=[pltpu.VMEM((HEAD, w), F32), pltpu.VMEM((HEAD, w), F32),
                        pltpu.VMEM((n_heads, HEAD, 2 * HEAD), F32), pltpu.VMEM((n_heads, HEAD, 2 * HEAD), F32),
                        pltpu.VMEM((n_heads, HEAD, 2 * HEAD), BF16), pltpu.VMEM((n_heads, HEAD, 2 * HEAD), BF16),
                        pltpu.VMEM((HEAD, w), BF16), pltpu.VMEM((HEAD, w), BF16)],
        compiler_params=_params(("arbitrary", "arbitrary"), 56),
    )(*operands)
    return dq.reshape(s, w), dk.reshape(s, w), dv.reshape(s, w), ds


def _qkv_bwd(dproj, proj, dq, dk, dv, g_q, g_k, w):
    s = proj.shape[0]
    n_heads = w // HEAD
    tm = HEAD

    def body(dproj_hbm, q_ref, k_ref, gq_ref, gk_ref, dq_ref, dk_ref, dv_ref, out_ref, dgq_ref, dgk_ref):
        i = pl.program_id(0)
        gq = gq_ref[...] * (HEAD ** -0.5)
        gk = gk_ref[...]
        acc_q = jnp.zeros((1, HEAD), F32)
        acc_k = jnp.zeros((1, HEAD), F32)
        for h in range(n_heads):
            sl = slice(h * HEAD, (h + 1) * HEAD)
            q, k = q_ref[:, sl], k_ref[:, sl]
            dqn, dkn = dq_ref[:, sl], dk_ref[:, sl]
            rq, rk = _rms(q), _rms(k)
            out_ref[:, h * HEAD:(h + 1) * HEAD] = _rms_bwd(dqn, q, rq, gq).astype(BF16)
            out_ref[:, w + h * HEAD:w + (h + 1) * HEAD] = _rms_bwd(dkn, k, rk, gk).astype(BF16)
            acc_q += jnp.sum(dqn * q * rq, axis=0, keepdims=True)
            acc_k += jnp.sum(dkn * k * rk, axis=0, keepdims=True)
        out_ref[:, 2 * w:] = dv_ref[...].astype(BF16)

        @pl.when(i == 0)
        def _():
            dgq_ref[...] = jnp.zeros_like(dgq_ref)
            dgk_ref[...] = jnp.zeros_like(dgk_ref)

        dgq_ref[...] += acc_q * (HEAD ** -0.5)
        dgk_ref[...] += acc_k

    seg = lambda j: pl.BlockSpec((tm, w), lambda i, j=j: (i, j))
    vec = pl.BlockSpec((1, HEAD), lambda i: (0, 0))
    row = pl.BlockSpec((tm, w), lambda i: (i, 0))
    return pl.pallas_call(
        body, name="qkv_bwd", grid=(s // tm,),
        out_shape=[jax.ShapeDtypeStruct(dproj.shape, BF16),
                   jax.ShapeDtypeStruct((1, HEAD), F32), jax.ShapeDtypeStruct((1, HEAD), F32)],
        in_specs=[pl.BlockSpec(memory_space=pl.ANY), seg(3), seg(4), vec, vec] + [row] * 3,
        out_specs=[pl.BlockSpec((tm, 3 * w), lambda i: (i, 1)), vec, vec],
        input_output_aliases={0: 0},
        compiler_params=_params(("arbitrary",), 48),
    )(dproj, proj, proj, g_q, g_k, dq, dk, dv)


def _mixer_a(u, gv, ws_ref, bst_ref, lng, lnb, z_scr, ln_scr):
    n_groups = u.shape[1] // HEAD
    mu = jnp.mean(gv, axis=-1, keepdims=True)
    xc = gv - mu
    rs = lax.rsqrt(jnp.mean(xc * xc, axis=-1, keepdims=True) + EPS)
    xhat = xc * rs
    ln_scr[...] = (xhat * lng + lnb).astype(BF16)
    causal = _causal_mask()
    for g in range(n_groups):
        sl = slice(g * HEAD, (g + 1) * HEAD)
        wm = jnp.where(causal, ws_ref[g], 0.0).astype(BF16)
        z_scr[:, sl] = _dot(wm, ln_scr[:, sl]) + bst_ref[:, g:g + 1]
    return u, xhat, rs


def _causal_mask():
    token = lambda r: 16 * (r % SUB) + r // SUB
    row = lax.broadcasted_iota(jnp.int32, (HEAD, HEAD), 0)
    col = lax.broadcasted_iota(jnp.int32, (HEAD, HEAD), 1)
    return token(col) <= token(row)


def _merge_b(o_refs, lse_refs, yb_scr):
    n_heads = yb_scr.shape[1] // HEAD
    lses = [t[...] for t in lse_refs]
    m = jnp.maximum(jnp.maximum(lses[0], lses[1]), lses[2])
    tot = m + jnp.log(sum(jnp.exp(t - m) for t in lses))
    alphas = [jnp.exp(t - tot) for t in lses]
    for h in range(n_heads):
        sl = slice(h * HEAD, (h + 1) * HEAD)
        yb_scr[:, sl] = sum(a[:, h:h + 1] * o[:, sl].astype(F32) for a, o in zip(alphas, o_refs))
    return tot


def _mix_fwd(proj, outs, lses, w_s, bst, ln_g, ln_b, g_a, g_b, w, dep=None):
    s = proj.shape[0]
    n_groups = w // HEAD

    def body(au_ref, av_ref, az_ref, bz_ref, o1, o2, o3, l1, l2, l3, ws_ref, bst_ref,
             lng_ref, lnb_ref, ga_ref, gb_ref, p_ref, y_ref, z_scr, ln_scr, yb_scr):
        u, _, _ = _mixer_a(_gelu(au_ref[...]), _gelu(av_ref[...]), ws_ref, bst_ref, lng_ref[...], lnb_ref[...],
                           z_scr, ln_scr)
        ya = u * z_scr[...]
        silu_a, _ = _silu_and_grad(az_ref[...])
        perm = p_ref[...]
        y_ref[:, :w] = _dot(perm, (ya * _rms(ya) * ga_ref[...] * silu_a).astype(BF16), TN_DIMS).astype(BF16)
        _merge_b((o1, o2, o3), (l1, l2, l3), yb_scr)
        yb = yb_scr[...]
        silu_b, _ = _silu_and_grad(bz_ref[...])
        y_ref[:, w:] = _dot(perm, (yb * _rms(yb) * gb_ref[...] * silu_b).astype(BF16), TN_DIMS).astype(BF16)

    seg = lambda j: pl.BlockSpec((HEAD, w), lambda i, j=j: (i, j))
    row = pl.BlockSpec((HEAD, w), lambda i: (i, 0))
    stat = pl.BlockSpec((HEAD, HEAD), lambda i: (i, 0))
    vec = pl.BlockSpec((1, w), lambda i: (0, 0))
    in_specs = [seg(0), seg(1), seg(2), seg(6), row, row, row, stat, stat, stat,
                pl.BlockSpec((n_groups, HEAD, HEAD), lambda i: (0, 0, 0)),
                pl.BlockSpec((HEAD, n_groups), lambda i: (0, 0)), vec, vec, vec, vec,
                pl.BlockSpec((HEAD, HEAD), lambda i: (0, 0))]
    operands = [proj, proj, proj, proj, *outs, *lses, w_s, bst, ln_g, ln_b, g_a, g_b, _chunk_perm()]
    if dep is not None:
        body = _drop_arg(body, len(operands))
        in_specs.append(DEP_SPEC)
        operands.append(dep)
    return pl.pallas_call(
        body, name="mix_fwd", grid=(s // HEAD,),
        out_shape=jax.ShapeDtypeStruct((s, 2 * w), BF16), in_specs=in_specs,
        out_specs=pl.BlockSpec((HEAD, 2 * w), lambda i: (i, 0)),
        scratch_shapes=[pltpu.VMEM((HEAD, w), F32), pltpu.VMEM((HEAD, w), BF16), pltpu.VMEM((HEAD, w), F32)],
        compiler_params=_params(("arbitrary",), 48),
    )(*operands)


def _mix_bwd(proj, dy, outs, lses, w_s, bst, ln_g, ln_b, g_a, g_b, w):
    s = proj.shape[0]
    n_groups = w // HEAD

    def body(au_ref, av_ref, az_ref, bz_ref, dy_ref, o1, o2, o3, l1, l2, l3, ws_ref, bst_ref,
             lng_ref, lnb_ref, ga_ref, gb_ref,
             dproj_ref, dyb_ref, tot_ref, dl_ref, dws_ref, dbst_ref, dlng_ref, dlnb_ref, dga_ref, dgb_ref,
             z_scr, ln_scr, yb_scr, dz_scr, dln_scr):
        i = pl.program_id(0)

        @pl.when(i == 0)
        def _():
            for t in (dws_ref, dbst_ref, dlng_ref, dlnb_ref, dga_ref, dgb_ref):
                t[...] = jnp.zeros_like(t)

        az = az_ref[...]
        lng = lng_ref[...]
        u, du_dau = _gelu_and_grad(au_ref[...])
        gv, dgv_dav = _gelu_and_grad(av_ref[...])
        u, xhat, rs = _mixer_a(u, gv, ws_ref, bst_ref, lng, lnb_ref[...], z_scr, ln_scr)
        z = z_scr[...]
        ya = u * z
        ra = _rms(ya)
        silu_a, dsilu_a = _silu_and_grad(az)
        dya_all = dy_ref[:, :w].astype(F32)
        na = ya * ra * ga_ref[...]
        dna = dya_all * silu_a
        dproj_ref[:, 2 * w:3 * w] = (dya_all * na * dsilu_a).astype(BF16)
        dga_ref[...] += jnp.sum(dna * ya * ra, axis=0, keepdims=True)
        dya = _rms_bwd(dna, ya, ra, ga_ref[...])
        dproj_ref[:, :w] = (dya * z * du_dau).astype(BF16)
        dz_scr[...] = (dya * u).astype(BF16)

        causal = _causal_mask()
        for g in range(n_groups):
            sl = slice(g * HEAD, (g + 1) * HEAD)
            wm = jnp.where(causal, ws_ref[g], 0.0).astype(BF16)
            dz = dz_scr[:, sl]
            dln_scr[:, sl] = _dot(wm, dz, TN_DIMS)
            dws_ref[g] += jnp.where(causal, _dot(dz, ln_scr[:, sl], NT_DIMS), 0.0)
            dbst_ref[:, g:g + 1] += jnp.sum(dz.astype(F32), axis=-1, keepdims=True)
        dln = dln_scr[...]
        dlng_ref[...] += jnp.sum(dln * xhat, axis=0, keepdims=True)
        dlnb_ref[...] += jnp.sum(dln, axis=0, keepdims=True)
        gy = dln * lng
        dgv = rs * (gy - jnp.mean(gy, axis=-1, keepdims=True)
                    - xhat * jnp.mean(gy * xhat, axis=-1, keepdims=True))
        dproj_ref[:, w:2 * w] = (dgv * dgv_dav).astype(BF16)
        dproj_ref[:, 3 * w:6 * w] = jnp.zeros((HEAD, 3 * w), BF16)

        tot_ref[...] = _merge_b((o1, o2, o3), (l1, l2, l3), yb_scr)
        yb = yb_scr[...]
        rb = _rms(yb)
        bz = bz_ref[...]
        silu_b, dsilu_b = _silu_and_grad(bz)
        dyb_all = dy_ref[:, w:].astype(F32)
        dnb = dyb_all * silu_b
        dproj_ref[:, 6 * w:] = (dyb_all * yb * rb * gb_ref[...] * dsilu_b).astype(BF16)
        dgb_ref[...] += jnp.sum(dnb * yb * rb, axis=0, keepdims=True)
        dyb = _rms_bwd(dnb, yb, rb, gb_ref[...])
        dyb_ref[...] = dyb
        prod = dyb * yb
        dl_ref[...] = _lane_pick(
            [jnp.sum(prod[:, h * HEAD:(h + 1) * HEAD], axis=-1, keepdims=True) for h in range(n_groups)], HEAD)

    seg = lambda j: pl.BlockSpec((HEAD, w), lambda i, j=j: (i, j))
    row_w = pl.BlockSpec((HEAD, w), lambda i: (i, 0))
    stat = pl.BlockSpec((HEAD, HEAD), lambda i: (i, 0))
    vec = pl.BlockSpec((1, w), lambda i: (0, 0))
    ws_spec = pl.BlockSpec((n_groups, HEAD, HEAD), lambda i: (0, 0, 0))
    bst_spec = pl.BlockSpec((HEAD, n_groups), lambda i: (0, 0))
    vec_shape = jax.ShapeDtypeStruct((1, w), F32)
    return pl.pallas_call(
        body, name="mix_bwd", grid=(s // HEAD,),
        out_shape=[jax.ShapeDtypeStruct((s, N_SEG * w), BF16), jax.ShapeDtypeStruct((s, w), F32),
                   jax.ShapeDtypeStruct((s, HEAD), F32), jax.ShapeDtypeStruct((s, HEAD), F32),
                   jax.ShapeDtypeStruct((n_groups, HEAD, HEAD), F32), jax.ShapeDtypeStruct((HEAD, n_groups), F32),
                   vec_shape, vec_shape, vec_shape, vec_shape],
        in_specs=[seg(0), seg(1), seg(2), seg(6), pl.BlockSpec((HEAD, 2 * w), lambda i: (i, 0)),
                  row_w, row_w, row_w, stat, stat, stat, ws_spec, bst_spec, vec, vec, vec, vec],
        out_specs=[pl.BlockSpec((HEAD, N_SEG * w), lambda i: (i, 0)), row_w, stat, stat,
                   ws_spec, bst_spec, vec, vec, vec, vec],
        scratch_shapes=[pltpu.VMEM((HEAD, w), F32), pltpu.VMEM((HEAD, w), BF16), pltpu.VMEM((HEAD, w), F32),
                        pltpu.VMEM((HEAD, w), BF16), pltpu.VMEM((HEAD, w), F32)],
        compiler_params=_params(("arbitrary",), 56),
    )(proj, proj, proj, proj, dy, *outs, *lses, w_s, bst, ln_g, ln_b, g_a, g_b)


def _ple_gate(hn2, wgate, h, p, wup_g, tgt):
    s, d = h.shape
    n, p_dim, c_up = wup_g.shape
    tm, pair = _tile(s, 512), 2
    tn = pair * c_up
    cols = n // pair
    tiles = (s // tm) * cols
    cur = lambda t: jnp.minimum(t, tiles - 1)
    prv = lambda t: jnp.maximum(t - 1, 0)

    def body(a_ref, b_ref, h_ref, p_ref, wup_ref, tgt_ref, dout_ref, dpre_ref, dup_ref, loss_ref, acc):
        @pl.when(pl.program_id(0) == 0)
        def _():
            acc[...] = jnp.zeros_like(acc)

        gate = jax.nn.sigmoid(acc[...])
        pb = p_ref[...].astype(BF16)
        up = jnp.concatenate([_dot(pb, wup_ref[g]) for g in range(pair)], axis=1)
        err = h_ref[...] + gate * up - tgt_ref[...]
        dout = err * (1.0 / d)
        dout_ref[...] = dout.astype(BF16)
        dpre_ref[...] = (dout * up * gate * (1.0 - gate)).astype(BF16)
        dup_ref[...] = (dout * gate).astype(BF16)
        part = 0.5 * jnp.sum(err * err) * (1.0 / d)
        rr = lax.broadcasted_iota(jnp.int32, (SUB, HEAD), 0)
        cc = lax.broadcasted_iota(jnp.int32, (SUB, HEAD), 1)
        loss_ref[...] = jnp.where((rr == 0) & (cc == 0), part, 0.0)
        acc[...] = _dot(a_ref[...], b_ref[...])

    rows = s // tm
    tail = pl.BlockSpec((tm, tn), lambda t: (prv(t) % rows, prv(t) // rows))
    big = jax.ShapeDtypeStruct((s, d), BF16)
    return pl.pallas_call(
        body, name="ple_gate", grid=(tiles + 1,),
        out_shape=[big, big, big, jax.ShapeDtypeStruct((rows * SUB, cols * HEAD), F32)],
        in_specs=[pl.BlockSpec((tm, d), lambda t: (cur(t) % rows, 0)),
                  pl.BlockSpec((d, tn), lambda t: (0, cur(t) // rows)),
                  tail, pl.BlockSpec((tm, p_dim), lambda t: (prv(t) % rows, 0)),
                  pl.BlockSpec((pair, p_dim, c_up), lambda t: (prv(t) // rows, 0, 0)), tail],
        out_specs=[tail, tail, tail, pl.BlockSpec((SUB, HEAD), lambda t: (prv(t) % rows, prv(t) // rows))],
        scratch_shapes=[pltpu.VMEM((tm, tn), F32)],
        compiler_params=_params(("arbitrary",), 60),
    )(hn2, wgate, h, p, wup_g, tgt)


def _local_step(x, p, tgt, small, wg, ex, while_last_travels=None):
    s, d = x.shape
    n, _, c_in = wg.buffers["w_in"].shape
    assert n == N_DEV
    d_in = n * c_in
    w = d_in // N_SEG
    n_heads = w // HEAD
    p_dim, c_up = wg.buffers["w_ple_up"].shape[1:]
    assert s % (HEAD * DILATIONS[-1]) == 0 and w % HEAD == 0 and d == n * c_up == 2 * w

    near = (2, 4)
    wg.start("gather_in_pair", ["w_in"], (1,))
    token = wg.start("gather_in_first", ["w_in"], ("first",))
    rest = ["w_out", "w_ple_gate", "w_ple_up"]

    hn = _rmsnorm_fwd(x, small["g_pre"], "pre_norm", True, dep=token)
    tm = _tile(s, 1024)

    def in_proj(shards, name, carry, dep=None):
        return _matmul(
            hn, wg.buffers["w_in"], name=name, grid=(s // tm, len(shards), 1), dims=NN_DIMS,
            prefetch=jnp.stack(shards).astype(jnp.int32),
            a_spec=pl.BlockSpec((tm, d), lambda i, j, k, sh: (i, 0)),
            b_spec=pl.BlockSpec((None, d, c_in), lambda i, j, k, sh: (sh[j], 0, 0), pipeline_mode=pl.Buffered(1)),
            acc_shape=(tm, c_in), out_shapes=[jax.ShapeDtypeStruct((s, d_in), F32)],
            out_specs=[pl.BlockSpec((tm, c_in), lambda i, j, k, sh: (i, sh[j]))], carry=carry, dep=dep,
            vmem_mib=56)[0]

    me = wg.me
    core = me & 1
    first, second, far = me ^ (4 - 2 * core), me ^ (2 + 2 * core), me ^ 6
    proj = in_proj([me], "in_proj_own", None)
    token = wg.start("gather_in_second", ["w_in"], ("second",), after=[proj])
    wg.arrived("gather_in_pair", [token])
    proj = in_proj([me ^ 1], "in_proj_sibling", proj)
    buckets = jnp.stack([_rel_buckets(dil) for dil in DILATIONS])
    bias = _bias_build(small["rel_bias"], buckets, n_heads)
    ahead = [bias] + [wg.buffers[k] for k in rest]
    for tag, mine, from_sibling in (("first", first, second ^ 1), ("second", second, first ^ 1), ("far", far, far ^ 1)):
        token = wg.forward("gather_in_" + tag, [proj] + ahead)
        ahead = []
        if tag == "second":
            token = wg.relay_start("gather_in_far", ["w_in"], after=[token])
        elif tag == "far":
            token = wg.start("gather_rest", rest, (1,) + near, after=[token])
        proj = in_proj([mine], "in_proj_" + tag, proj, dep=token)
        wg.forwarded("gather_in_" + tag, [proj])
        proj = in_proj([from_sibling], "in_proj_%s_forwarded" % tag, proj)
    win_g = wg.buffers["w_in"]

    qn, kn = _qkv_prep(proj, small["g_q"], small["g_k"], w)
    token = wg.forward("gather_rest", [qn])
    token = wg.relay_start("gather_rest_far", rest, after=[token])
    outs, lses = [], []
    for c, dil in enumerate(DILATIONS):
        o, l = _attn_fwd(qn, kn, proj, bias[c], dil, "attn_fwd_d%d" % dil, dep=token)
        outs.append(o)
        lses.append(l)
    token = wg.forward("gather_rest_far", outs)

    ws_p = small["w_s"][:, CHUNK_ORDER][:, :, CHUNK_ORDER]
    bst = small["b_s"].T[CHUNK_ORDER]
    mix_args = (outs, lses, ws_p, bst, small["ln_v_g"], small["ln_v_b"], small["g_out_a"], small["g_out_b"], w)
    y = _mix_fwd(proj, *mix_args, dep=token)
    wg.forwarded("gather_rest", [y])
    wg.forwarded("gather_rest_far", [y])
    wout_g, wgate_g, wup_g = (wg.buffers[k] for k in rest)
    wout_f = wout_g.reshape(2 * w, d)
    wgate_f = wgate_g.reshape(d, d)

    tn = _tile(d, 1024)
    tk2 = 2 * w

    def resid_epilogue(acc, ex, outs_):
        outs_[0][...] = ex[0][...] + acc[...]

    h = _matmul(
        y, wout_f, name="out_proj", grid=(s // tm, d // tn, (2 * w) // tk2), dims=NN_DIMS,
        a_spec=pl.BlockSpec((tm, tk2), lambda i, j, k: (i, k)),
        b_spec=pl.BlockSpec((tk2, tn), lambda i, j, k: (k, j)),
        acc_shape=(tm, tn), out_shapes=[jax.ShapeDtypeStruct((s, d), F32)],
        out_specs=[pl.BlockSpec((tm, tn), lambda i, j, k: (i, j))],
        extra=(x,), extra_specs=(pl.BlockSpec((tm, tn), lambda i, j, k: (i, j)),),
        epilogue=resid_epilogue, vmem_mib=56)[0]

    hn2 = _rmsnorm_fwd(h, small["g_ple"], "ple_norm", False)

    dout, dpre, dup, loss_parts = _ple_gate(hn2, wgate_f, h, p, wup_g, tgt)
    loss = jnp.sum(loss_parts)

    tks = _tile(s, 2048)
    g_wup = _matmul(
        p, dup, name="grad_w_up", grid=(1, n, s // tks), dims=TN_DIMS,
        a_spec=pl.BlockSpec((tks, p_dim), lambda i, j, k: (k, 0)),
        b_spec=pl.BlockSpec((tks, c_up), lambda i, j, k: (k, j)),
        acc_shape=(p_dim, c_up), out_shapes=[jax.ShapeDtypeStruct((n, p_dim, c_up), BF16)],
        out_specs=[pl.BlockSpec((None, p_dim, c_up), lambda i, j, k: (j, 0, 0))])[0]

    def tn_matmul(a, b, name):
        m_, n_ = a.shape[1], b.shape[1]
        bm, bn = _tile(m_, 1024), _tile(n_, 1024)
        return _matmul(
            a, b, name=name, grid=(m_ // bm, n_ // bn, 1), dims=TN_DIMS,
            a_spec=pl.BlockSpec((s, bm), lambda i, j, k: (0, i)),
            b_spec=pl.BlockSpec((s, bn), lambda i, j, k: (0, j)),
            acc_shape=(bm, bn), out_shapes=[jax.ShapeDtypeStruct((m_, n_), BF16)],
            out_specs=[pl.BlockSpec((bm, bn), lambda i, j, k: (i, j))], vmem_mib=56)[0]

    def nt_matmul(a, b, name, out_dtype, dep=None):
        k_, n_ = a.shape[1], b.shape[0]
        bm, bn, bk = _tile(s, 1024), _tile(n_, 1024), k_
        return _matmul(
            a, b, name=name, grid=(s // bm, n_ // bn, k_ // bk), dims=NT_DIMS,
            a_spec=pl.BlockSpec((bm, bk), lambda i, j, k: (i, k)),
            b_spec=pl.BlockSpec((bn, bk), lambda i, j, k: (j, k)),
            acc_shape=(bm, bn), out_shapes=[jax.ShapeDtypeStruct((s, n_), out_dtype)],
            out_specs=[pl.BlockSpec((bm, bn), lambda i, j, k: (i, j))], dep=dep, vmem_mib=56)[0]

    by_core = lambda g: g.reshape((N_CHIP, 2) + g.shape[-2:])
    g_wgate = tn_matmul(hn2, dpre, "grad_w_gate").reshape(wgate_g.shape)
    dhn2 = nt_matmul(dpre, wgate_f, "ple_gate_bwd", BF16)
    dh_b, dh_bp, dg_ple = _rmsnorm_bwd(dhn2, h, small["g_ple"], dout, "ple_norm_bwd", False, True)
    g_wout = tn_matmul(y, dh_b, "grad_w_out").reshape(wout_g.shape)

    late = ("w_out", "w_ple_gate", "w_ple_up")
    late_parts = (g_wout, g_wgate, g_wup)
    token = ex.push_pairs("pair_late", [by_core(g) for g in late_parts])
    dy = nt_matmul(dh_bp, wout_f, "out_proj_bwd", BF16, dep=token)
    (dproj, dyb, lse_tot, delta, dws, dbst, dlng, dlnb, dga, dgb) = _mix_bwd(proj, dy, *mix_args)
    both_columns, from_sibling = ex.pairs_done("pair_late", [dproj])
    pair_sums = [_pair_add(mine.reshape((N_DEV,) + mine.shape[-2:]), theirs, "pair_add_" + k, ex.core)
                 for k, mine, theirs in zip(late, both_columns, from_sibling)]
    token = ex.push_chips("chip_late", pair_sums)

    running, dss = None, []
    for c, dil in enumerate(DILATIONS):
        *running, ds = _attn_bwd(qn, kn, proj, dyb, lse_tot, delta, bias[c], dil, "attn_bwd_d%d" % dil,
                                 running=running, dep=token if c == 0 else None)
        dss.append(ds)
    d_rel = _bias_grad(jnp.stack(dss), buckets, n_heads)
    dproj, dgq, dgk = _qkv_bwd(dproj, proj, *running, small["g_q"], small["g_k"], w)
    pair_sums, landed, _ = ex.chips_done("chip_late", [dproj])
    delivered = {k: (mine, theirs) for k, mine, theirs in zip(late, pair_sums, landed)}

    token_row = np.argsort(CHUNK_ORDER)
    dws = dws[:, token_row][:, :, token_row]
    dbst = dbst[token_row]
    small_grads = {
        "w_s": dws, "b_s": dbst.T, "ln_v_g": dlng, "ln_v_b": dlnb, "g_q": dgq, "g_k": dgk,
        "rel_bias": d_rel, "g_out_a": dga, "g_out_b": dgb, "g_ple": dg_ple,
    }

    bm = _tile(d, 1024)

    def grad_w_in(core, name, dep=None):
        return _matmul(
            hn, dproj, name=name, grid=(d // bm, N_CHIP, 1), dims=TN_DIMS, prefetch=core.reshape(1),
            a_spec=pl.BlockSpec((s, bm), lambda i, j, k, core_ref: (0, i)),
            b_spec=pl.BlockSpec((s, c_in), lambda i, j, k, core_ref: (0, 2 * j + core_ref[0])),
            acc_shape=(bm, c_in), out_shapes=[jax.ShapeDtypeStruct((N_CHIP, d, c_in), BF16)],
            out_specs=[pl.BlockSpec((None, bm, c_in), lambda i, j, k, core_ref: (j, i, 0))], dep=dep,
            vmem_mib=60)[0]

    for_sibling = grad_w_in(1 - ex.core, "grad_w_in_sibling")
    token = ex.push_pairs("pair_in", [for_sibling])
    mine = grad_w_in(ex.core, "grad_w_in_mine", dep=token)
    _, from_sibling = ex.pairs_done("pair_in", [mine])
    pair_sum = _pair_add(mine, from_sibling[0], "pair_add_w_in")
    token = ex.push_chips("chip_in", [pair_sum], _pack_small(small_grads, SMALL_EARLY))

    dhn = _matmul(
        dproj, win_g, name="in_proj_bwd", grid=(s // tm, d // tn, n // 2), dims=NT_DIMS,
        a_spec=pl.BlockSpec((tm, 2 * c_in), lambda i, j, k: (i, k)),
        b_spec=pl.BlockSpec((2, tn, c_in), lambda i, j, k: (k, j, 0)),
        acc_shape=(tm, tn), out_shapes=[jax.ShapeDtypeStruct((s, d), BF16)],
        out_specs=[pl.BlockSpec((tm, tn), lambda i, j, k: (i, j))], dep=token, vmem_mib=56)[0]
    grad_x, dg_pre = _rmsnorm_bwd(dhn, x, small["g_pre"], dh_b, "pre_norm_bwd", True, False)
    extra = while_last_travels(token, delivered, dg_pre) if while_last_travels is not None else []
    pair_sums, landed, slabs = ex.chips_done("chip_in", [grad_x] + list(extra))
    delivered["w_in"] = (pair_sums[0], landed[0])
    small_grads["g_pre"] = dg_pre
    return loss, grad_x, small_grads, delivered, slabs, extra


SMALL_EARLY = ("w_s", "b_s", "ln_v_g", "ln_v_b", "g_q", "g_k", "rel_bias", "g_out_a", "g_out_b", "g_ple")
SMALL_LAST = ("g_pre",)
SMALL_NAMES = SMALL_LAST + SMALL_EARLY


def _pack_small(tree, names):
    parts = []
    for name in names:
        flat = tree[name].astype(F32).reshape(-1)
        pad = (-flat.shape[0]) % HEAD
        parts.append(jnp.pad(flat, (0, pad)) if pad else flat)
    slab = jnp.concatenate(parts).reshape(-1, HEAD)
    pad_rows = (-slab.shape[0]) % 8
    return jnp.pad(slab, ((0, pad_rows), (0, 0))) if pad_rows else slab


def _unpack_small(slab, like, names):
    flat = slab.reshape(-1)
    out, off = {}, 0
    for name in names:
        size = like[name].size
        out[name] = flat[off:off + size].reshape(like[name].shape)
        off += size + (-size) % HEAD
    return out


def _peer(k):
    x, y, c = (lax.axis_index(a) for a in AXES)
    if k == "first":
        px, py, pc = x ^ (1 - c), y ^ c, c
    elif k == "second":
        px, py, pc = x ^ c, y ^ (1 - c), c
    else:
        bits = ((k >> 2) & 1, (k >> 1) & 1, k & 1)
        px, py, pc = (1 - v if b else v for v, b in zip((x, y, c), bits))
    return (px, py, pc), 4 * px + 2 * py + pc


def _my_index():
    x, y, c = (lax.axis_index(a) for a in AXES)
    return 4 * x + 2 * y + c


N_CHIP = 4
HBM_SPEC = pl.BlockSpec(memory_space=pl.ANY)


def _remote(src, dst, send_sem, recv_sem, peer):
    return pltpu.make_async_remote_copy(src_ref=src, dst_ref=dst, send_sem=send_sem, recv_sem=recv_sem,
                                        device_id=peer, device_id_type=pl.DeviceIdType.MESH)


SEM_SPEC = pl.BlockSpec(memory_space=pltpu.SEMAPHORE)
HBM_ONLY = pl.BlockSpec(memory_space=pltpu.HBM)
DATAFLOW = pltpu.SideEffectType.DATAFLOW_SIDE_EFFECTING


def _comm_call(name, arrays, *, wait=None, start=None, after=()):
    n, n_after = len(arrays), len(after)

    def body(*refs):
        ins = refs[:n]
        pos = n
        if wait is not None:
            for cp in wait[2](ins, refs[pos], refs[pos + 1]):
                cp.wait()
            pos += 2
        outs = refs[pos + n_after:]
        if start is not None:
            for cp in start[1](ins, outs[0], outs[1]):
                cp.start()
        outs[-1][...] = jnp.zeros_like(outs[-1])

    operands = [pltpu.with_memory_space_constraint(a, pltpu.HBM) for a in arrays]
    in_specs = [HBM_ONLY] * n
    if wait is not None:
        operands += [wait[0], wait[1]]
        in_specs += [SEM_SPEC, SEM_SPEC]
    operands += list(after)
    in_specs += [HBM_SPEC] * n_after
    out_shape, out_specs = [], []
    if start is not None:
        out_shape += [pltpu.SemaphoreType.DMA((start[0],))] * 2
        out_specs += [SEM_SPEC, SEM_SPEC]
    first = len(out_shape)
    out_shape += [pltpu.HBM(a.shape, a.dtype) for a in arrays] + [jax.ShapeDtypeStruct((SUB, HEAD), F32)]
    out_specs += [HBM_ONLY] * n + [pl.BlockSpec(memory_space=pltpu.VMEM)]
    res = pl.pallas_call(
        body, name=name, out_shape=tuple(out_shape), in_specs=tuple(in_specs), out_specs=tuple(out_specs),
        input_output_aliases={i: first + i for i in range(n)},
        compiler_params=pltpu.CompilerParams(has_side_effects=DATAFLOW),
    )(*operands)
    sems = (res[0], res[1]) if start is not None else None
    return list(res[first:first + n]), sems, res[-1]


class _GradExchange:
    def __init__(self):
        x, y, c = (lax.axis_index(a) for a in AXES)
        self.core = c.astype(jnp.int32)
        self.chip = (2 * x + y).astype(jnp.int32)
        self.pending = {}

    def _pair_copies(self, n_arr):
        def make(refs, send_sems, recv_sems):
            sibling, _ = _peer(1)
            other = 1 - lax.axis_index("c")
            srcs, lands = refs[:n_arr], refs[n_arr:]
            pick = lambda ref, ch: ref.at[ch, other] if len(ref.shape) == 4 else ref.at[ch]
            return [_remote(pick(srcs[a], ch), lands[a].at[ch], send_sems.at[a * N_CHIP + ch],
                            recv_sems.at[a * N_CHIP + ch], sibling)
                    for a in range(n_arr) for ch in range(N_CHIP)]
        return make

    def _chip_copies(self, n_arr, with_slab):
        def make(refs, send_sems, recv_sems):
            x, y = lax.axis_index("x"), lax.axis_index("y")
            my_chip = 2 * x + y
            srcs, lands = refs[:n_arr], refs[n_arr:2 * n_arr]
            copies = []
            for j, k in enumerate((6, 2, 4)):
                peer, peer_idx = _peer(k)
                for a in range(n_arr):
                    copies.append(_remote(srcs[a].at[peer_idx // 2], lands[a].at[my_chip],
                                          send_sems.at[3 * a + j], recv_sems.at[3 * a + j], peer))
            if with_slab:
                slab, slab_land = refs[2 * n_arr], refs[2 * n_arr + 1]
                for k in range(1, N_DEV):
                    peer, _ = _peer(k)
                    copies.append(_remote(slab, slab_land.at[_my_index()], send_sems.at[3 * n_arr + k - 1],
                                          recv_sems.at[3 * n_arr + k - 1], peer))
            return copies
        return make

    def push_pairs(self, tag, for_sibling):
        n_arr = len(for_sibling)
        lands = [lax.empty((N_CHIP,) + a.shape[-2:], a.dtype) for a in for_sibling]
        make = self._pair_copies(n_arr)
        arrays, sems, token = _comm_call(tag + "_start", list(for_sibling) + lands, start=(n_arr * N_CHIP, make))
        self.pending[tag] = (arrays, sems, make, n_arr)
        return token

    def pairs_done(self, tag, after):
        arrays, sems, make, n_arr = self.pending.pop(tag)
        arrays, _, _ = _comm_call(tag + "_wait", arrays, wait=(sems[0], sems[1], make), after=after)
        return arrays[:n_arr], arrays[n_arr:]

    def push_chips(self, tag, pair_sums, slab=None):
        n_arr = len(pair_sums)
        arrays = list(pair_sums) + [lax.empty(a.shape, a.dtype) for a in pair_sums]
        n_copies = 3 * n_arr
        if slab is not None:
            arrays += [slab, lax.empty((N_DEV,) + slab.shape, slab.dtype)]
            n_copies += N_DEV - 1
        make = self._chip_copies(n_arr, slab is not None)
        arrays, sems, token = _comm_call(tag + "_start", arrays, start=(n_copies, make))
        self.pending[tag] = (arrays, sems, make, n_arr)
        return token

    def chips_done(self, tag, after):
        arrays, sems, make, n_arr = self.pending.pop(tag)
        arrays, _, _ = _comm_call(tag + "_wait", arrays, wait=(sems[0], sems[1], make), after=after)
        return arrays[:n_arr], arrays[n_arr:2 * n_arr], arrays[2 * n_arr:]


def _cast_place(w, name):
    r, c = w.shape
    tr = r if r * c <= MIB else 1 << ((MIB // c).bit_length() - 1)
    assert r % tr == 0

    def body(me_ref, w_ref, o_ref):
        o_ref[...] = w_ref[...].astype(BF16)

    return pl.pallas_call(
        body, name=name, out_shape=jax.ShapeDtypeStruct((N_DEV, r, c), BF16),
        grid_spec=pltpu.PrefetchScalarGridSpec(
            num_scalar_prefetch=1, grid=(r // tr,),
            in_specs=[pl.BlockSpec((tr, c), lambda i, me_ref: (i, 0))],
            out_specs=pl.BlockSpec((None, tr, c), lambda i, me_ref: (me_ref[0], i, 0))),
        compiler_params=_params(("arbitrary",), 40),
    )(_my_index().astype(jnp.int32).reshape(1), w)


class _WeightGather:
    CHIPS = (2, 4, 6)

    def __init__(self, buffers):
        self.buffers = dict(buffers)
        self.pending = {}
        self.me = _my_index().astype(jnp.int32)

    def _own_slot_to(self, peers):
        def make(refs, send_sems, recv_sems):
            me = _my_index()
            return [_remote(ref.at[me], ref.at[me], send_sems.at[len(peers) * a + j],
                            recv_sems.at[len(peers) * a + j], _peer(k)[0])
                    for a, ref in enumerate(refs) for j, k in enumerate(peers)]
        return make

    def _forward_from(self, chips):
        def make(refs, send_sems, recv_sems):
            sibling, _ = _peer(1)
            copies = []
            for a, ref in enumerate(refs):
                for j, k in enumerate(chips):
                    slot = ref.at[_peer(k)[1]]
                    copies.append(_remote(slot, slot, send_sems.at[len(chips) * a + j],
                                          recv_sems.at[len(chips) * a + j], sibling))
            return copies
        return make

    def _run(self, call, names, **kw):
        arrays, sems, token = _comm_call(call, [self.buffers[k] for k in names], **kw)
        self.buffers.update(zip(names, arrays))
        return sems, token

    def start(self, tag, names, peers, after=()):
        make = self._own_slot_to(peers)
        sems, token = self._run(tag + "_start", names, start=(len(names) * len(peers), make), after=after)
        self.pending[tag] = (names, sems, make, peers)
        return token

    @staticmethod
    def _relay(refs, send_sems, recv_sems):
        x, y, c = (lax.axis_index(a) for a in AXES)
        peer = (x ^ (1 - c), y ^ c, c)
        slot = _my_index() ^ (2 + 2 * c)
        return [_remote(ref.at[slot], ref.at[slot], send_sems.at[a], recv_sems.at[a], peer)
                for a, ref in enumerate(refs)]

    def relay_start(self, tag, names, after=()):
        sems, token = self._run(tag + "_start", names, start=(len(names), self._relay), after=after)
        self.pending[tag] = (names, sems, self._relay, (6,))
        return token

    def arrived(self, tag, after):
        names, sems, make, _ = self.pending.pop(tag)
        self._run(tag + "_wait", names, wait=(sems[0], sems[1], make), after=after)

    def forward(self, tag, after):
        names, sems, make, peers = self.pending.pop(tag)
        chips = tuple(k for k in peers if k != 1)
        onward = self._forward_from(chips)
        new_sems, token = self._run(tag + "_forward", names, wait=(sems[0], sems[1], make),
                                    start=(len(chips) * len(names), onward), after=after)
        self.pending[tag + "/fwd"] = (names, new_sems, onward)
        return token

    def forwarded(self, tag, after):
        names, sems, make = self.pending.pop(tag + "/fwd")
        self._run(tag + "_done", names, wait=(sems[0], sems[1], make), after=after)


def _pair_add(mine, theirs, name, core=None):
    _, r, c_dim = theirs.shape
    tr = r if r * c_dim <= MIB else 1 << ((MIB // c_dim).bit_length() - 1)
    assert r % tr == 0
    stride = 1 if core is None else 2
    offset = jnp.zeros((1,), jnp.int32) if core is None else core.reshape(1)

    def body(off_ref, a_ref, b_ref, o_ref):
        o_ref[...] = (a_ref[...].astype(F32) + b_ref[...].astype(F32)).astype(BF16)

    blk = (None, tr, c_dim)
    return pl.pallas_call(
        body, name=name, out_shape=jax.ShapeDtypeStruct(theirs.shape, BF16),
        grid_spec=pltpu.PrefetchScalarGridSpec(
            num_scalar_prefetch=1, grid=(N_CHIP, r // tr),
            in_specs=[pl.BlockSpec(blk, lambda ch, i, off_ref: (stride * ch + off_ref[0], i, 0)),
                      pl.BlockSpec(blk, lambda ch, i, off_ref: (ch, i, 0))],
            out_specs=pl.BlockSpec(blk, lambda ch, i, off_ref: (ch, i, 0))),
        compiler_params=_params(("arbitrary", "arbitrary"), 40),
    )(offset, mine, theirs)


def _slab_exchange(slab, name):
    def body(slab_in, slab_out, send_sems, recv_sems, local_sem):
        me = _my_index()
        local = pltpu.make_async_copy(slab_in, slab_out.at[me], local_sem)
        local.start()
        sends = []
        for k in range(1, N_DEV):
            peer, _ = _peer(k)
            sends.append(_remote(slab_in, slab_out.at[me], send_sems.at[k - 1], recv_sems.at[k - 1], peer))
        for cp in sends:
            cp.start()
        for k in range(1, N_DEV):
            peer, peer_idx = _peer(k)
            slot = slab_out.at[peer_idx]
            _remote(slot, slot, send_sems.at[k - 1], recv_sems.at[k - 1], peer).wait_recv()
        for cp in sends:
            cp.wait_send()
        local.wait()

    return pl.pallas_call(
        body, name=name, out_shape=jax.ShapeDtypeStruct((N_DEV,) + slab.shape, slab.dtype),
        in_specs=[HBM_SPEC], out_specs=HBM_SPEC,
        scratch_shapes=[pltpu.SemaphoreType.DMA((N_DEV - 1,)), pltpu.SemaphoreType.DMA((N_DEV - 1,)),
                        pltpu.SemaphoreType.DMA],
        compiler_params=pltpu.CompilerParams(has_side_effects=True),
    )(slab)


def _adamw_math(w, g, m, v):
    m = ADAM_B1 * m + (1.0 - ADAM_B1) * g
    v = ADAM_B2 * v + (1.0 - ADAM_B2) * (g * g)
    m_hat = m / (1.0 - ADAM_B1 ** ADAM_STEP)
    v_hat = v / (1.0 - ADAM_B2 ** ADAM_STEP)
    delta = -ADAM_LR * (m_hat / (jnp.sqrt(v_hat) + ADAM_EPS) + ADAM_WD * w)
    return delta, m, v


def _adamw(parts, own, place, w, m, v, name, dep=None):
    n_parts = parts.shape[0]
    r, c = w.shape
    budget = 280 * 1024
    tr = r if r * c <= budget else 1 << ((budget // c).bit_length() - 1)
    assert r % tr == 0

    def body(place_ref, p_ref, own_ref, w_ref, m_ref, v_ref, g_ref, d_ref, nm_ref, nv_ref):
        mine = own_ref[...].astype(F32)
        g = None
        for i in range(n_parts):
            term = jnp.where(place_ref[0] == i, mine, p_ref[i].astype(F32))
            g = term if g is None else g + term
        delta, nm, nv = _adamw_math(w_ref[...], g, m_ref[...], v_ref[...])
        g_ref[...] = g
        d_ref[...] = delta
        nm_ref[...] = nm
        nv_ref[...] = nv

    blk = pl.BlockSpec((tr, c), lambda i, place_ref: (i, 0))
    shape = jax.ShapeDtypeStruct((r, c), F32)
    in_specs = [pl.BlockSpec((n_parts, tr, c), lambda i, place_ref: (0, i, 0)),
                pl.BlockSpec((None, tr, c), lambda i, place_ref: (place_ref[1], i, 0)), blk, blk, blk]
    operands = [parts, own, w, m, v]
    if dep is not None:
        body = _drop_arg(body, 1 + len(operands))
        in_specs.append(pl.BlockSpec((SUB, HEAD), lambda i, place_ref: (0, 0)))
        operands.append(dep)
    return pl.pallas_call(
        body, name=name, out_shape=[shape] * 4,
        grid_spec=pltpu.PrefetchScalarGridSpec(
            num_scalar_prefetch=1, grid=(r // tr,), in_specs=in_specs, out_specs=[blk] * 4),
        compiler_params=_params(("arbitrary",), 48),
    )(place, *operands)


def kernel(x, p, g_pre, w_in, w_s, b_s, ln_v_g, ln_v_b, g_q, g_k, rel_bias, g_out_a, g_out_b, w_out, g_ple, w_ple_gate, w_ple_up, loss_target, m_g_pre, m_w_in, m_w_s, m_b_s, m_ln_v_g, m_ln_v_b, m_g_q, m_g_k, m_rel_bias, m_g_out_a, m_g_out_b, m_w_out, m_g_ple, m_w_ple_gate, m_w_ple_up, v_g_pre, v_w_in, v_w_s, v_b_s, v_ln_v_g, v_ln_v_b, v_g_q, v_g_k, v_rel_bias, v_g_out_a, v_g_out_b, v_w_out, v_g_ple, v_w_ple_gate, v_w_ple_up):
    args = dict(locals())
    small = {"g_pre": g_pre, "w_s": w_s[0], "b_s": b_s[0], "ln_v_g": ln_v_g, "ln_v_b": ln_v_b, "g_q": g_q,
             "g_k": g_k, "rel_bias": rel_bias, "g_out_a": g_out_a, "g_out_b": g_out_b, "g_ple": g_ple}
    big_names = ("w_in", "w_out", "w_ple_gate", "w_ple_up")
    big = {k: args[k][0] for k in big_names}

    wg = _WeightGather({k: _cast_place(big[k], "place_" + k) for k in big_names})
    ex = _GradExchange()
    results = {}

    def big_adamw(k, delivered, dep=None):
        mine, theirs = delivered[k]
        place = jnp.stack([ex.chip, ex.chip])
        return _adamw(theirs, mine, place, big[k], args["m_" + k][0], args["v_" + k][0], "adamw_" + k, dep=dep)

    squeeze = lambda t: {k: (t[k][0] if k in ("w_s", "b_s") else t[k]) for k in SMALL_NAMES}
    small_m = squeeze({k: args["m_" + k] for k in SMALL_NAMES})
    small_v = squeeze({k: args["v_" + k] for k in SMALL_NAMES})
    slab_place = jnp.stack([_my_index().astype(jnp.int32), jnp.zeros((), jnp.int32)])

    def small_adamw(names_, parts, own, call_name):
        packed = _adamw(parts, own[None], slab_place, _pack_small(small, names_), _pack_small(small_m, names_),
                        _pack_small(small_v, names_), call_name)
        for idx in range(4):
            tree = _unpack_small(packed[idx], small, names_)
            for k in names_:
                results.setdefault(k, [None] * 4)[idx] = tree[k].reshape(args[k].shape)
        return packed[0]

    def while_last_travels(token, delivered, dg_pre):
        done = []
        for k in big_names[1:]:
            results[k] = big_adamw(k, delivered, dep=token)
            done.append(results[k][0])
        last_slab = _pack_small({"g_pre": dg_pre}, SMALL_LAST)
        done.append(small_adamw(SMALL_LAST, _slab_exchange(last_slab, "last_exchange"), last_slab, "adamw_last"))
        return done

    loss, grad_x, small_parts, delivered, slabs, _ = _local_step(
        x[0], p[0, 0], loss_target[0], small, wg, ex, while_last_travels)
    results["w_in"] = big_adamw("w_in", delivered)
    for k in big_names:
        results[k] = [t[None] for t in results[k]]
    small_adamw(SMALL_EARLY, slabs[1], slabs[0], "adamw_small")

    names = ("g_pre", "w_in", "w_s", "b_s", "ln_v_g", "ln_v_b", "g_q", "g_k", "rel_bias", "g_out_a", "g_out_b",
             "w_out", "g_ple", "w_ple_gate", "w_ple_up")
    total = lax.psum(loss, AXES)
    out = [total, grad_x[None]]
    for idx in range(4):
        out += [results[k][idx] for k in names]
    return tuple(out)
```

```python
import math

import numpy as np
import jax
import jax.numpy as jnp
from jax import lax
from jax.experimental import pallas as pl
from jax.experimental.pallas import tpu as pltpu

F32 = jnp.float32
BF16 = jnp.bfloat16
EPS = 1e-6
NEG_INF = -1e30
HEAD = 128
DILATIONS = (1, 4, 16)
NUM_BUCKETS = 32
MAX_DISTANCE = 2048
N_SEG = 7
ADAM_LR = 0.001
ADAM_B1 = 0.9
ADAM_B2 = 0.999
ADAM_EPS = 1e-08
ADAM_WD = 0.01
ADAM_STEP = 10
AXES = ("x", "y", "c")
N_DEV = 8
MIB = 1 << 20

SUB = 8

CHUNK_ORDER = np.array([16 * (r % SUB) + r // SUB for r in range(HEAD)])
BLOCK_ORDER = {
    1: CHUNK_ORDER,
    4: np.array([32 * (r // 32) + 4 * (r % SUB) + (r // SUB) % 4 for r in range(HEAD)]),
    16: np.arange(HEAD),
}

NT_DIMS = (((1,), (1,)), ((), ()))
TN_DIMS = (((0,), (0,)), ((), ()))
NN_DIMS = (((1,), (0,)), ((), ()))


def _params(semantics, vmem_mib):
    return pltpu.CompilerParams(dimension_semantics=semantics, vmem_limit_bytes=vmem_mib * MIB)


def _gelu(a):
    return 0.5 * a * (1.0 + lax.erf(a * (2.0 ** -0.5)))


def _gelu_and_grad(a):
    cdf = 0.5 * (1.0 + lax.erf(a * (2.0 ** -0.5)))
    return a * cdf, cdf + a * jnp.exp(-0.5 * a * a) * ((2.0 * math.pi) ** -0.5)


def _silu_and_grad(a):
    s = jax.nn.sigmoid(a)
    return a * s, s * (1.0 + a * (1.0 - s))


def _rms(v):
    return lax.rsqrt(jnp.mean(v * v, axis=-1, keepdims=True) + EPS)


def _rms_bwd(dy, v, r, g):
    gy = dy * g
    return r * gy - v * (r * r * r) * jnp.mean(gy * v, axis=-1, keepdims=True)


def _dot(a, b, dims=NN_DIMS):
    return lax.dot_general(a, b, dims, preferred_element_type=F32)


def _lane_pick(cols, width):
    rows = cols[0].shape[0]
    lane = lax.broadcasted_iota(jnp.int32, (rows, width), 1)
    out = jnp.zeros((rows, width), F32)
    for h, col in enumerate(cols):
        out = jnp.where(lane == h, col, out)
    return out


def _chunk_perm():
    return jnp.asarray(np.eye(HEAD, dtype=np.float32)[CHUNK_ORDER], BF16)


def _unpermute_f32(p, v):
    hi = v.astype(BF16)
    rest = v - hi.astype(F32)
    mid = rest.astype(BF16)
    lo = (rest - mid.astype(F32)).astype(BF16)
    return _dot(p, hi, TN_DIMS) + _dot(p, mid, TN_DIMS) + _dot(p, lo, TN_DIMS)


def _by_chunk(fn, v):
    return jnp.concatenate([fn(v[c * HEAD:(c + 1) * HEAD]) for c in range(v.shape[0] // HEAD)], axis=0)


def _rmsnorm_fwd(x, g, name, permute, dep=None):
    s, d = x.shape
    tm = _tile(s, 2 * HEAD)

    def body(x_ref, g_ref, p_ref, o_ref):
        v = x_ref[...]
        out = (v * _rms(v) * g_ref[...]).astype(BF16)
        if permute:
            out = _by_chunk(lambda t: _dot(p_ref[...], t).astype(BF16), out)
        o_ref[...] = out

    in_specs = [pl.BlockSpec((tm, d), lambda i: (i, 0)), pl.BlockSpec((1, d), lambda i: (0, 0)),
                pl.BlockSpec((HEAD, HEAD), lambda i: (0, 0))]
    operands = [x, g, _chunk_perm()]
    if dep is not None:
        body = _drop_arg(body, len(operands))
        in_specs.append(DEP_SPEC)
        operands.append(dep)
    return pl.pallas_call(
        body, name=name, grid=(s // tm,),
        out_shape=jax.ShapeDtypeStruct((s, d), BF16), in_specs=in_specs,
        out_specs=pl.BlockSpec((tm, d), lambda i: (i, 0)),
        compiler_params=_params(("arbitrary",), 40),
    )(*operands)


def _rmsnorm_bwd(dy, v, g, res, name, dy_permuted, with_bf16):
    s, d = v.shape
    tm = _tile(s, 2 * HEAD)
    perm = _chunk_perm()

    def body(dy_ref, v_ref, g_ref, res_ref, p_ref, *outs):
        dg_ref = outs[-1]

        @pl.when(pl.program_id(0) == 0)
        def _():
            dg_ref[...] = jnp.zeros_like(dg_ref)

        vv, dyv = v_ref[...], dy_ref[...]
        if dy_permuted:
            undo = _unpermute_f32 if dyv.dtype == F32 else (lambda p_, t: _dot(p_, t, TN_DIMS))
            dyv = _by_chunk(lambda t: undo(p_ref[...], t), dyv)
        dyv = dyv.astype(F32)
        r = _rms(vv)
        dx = res_ref[...].astype(F32) + _rms_bwd(dyv, vv, r, g_ref[...])
        if with_bf16:
            dxb = dx.astype(BF16)
            outs[0][...] = dxb
            outs[1][...] = _by_chunk(lambda t: _dot(p_ref[...], t).astype(BF16), dxb)
        else:
            outs[0][...] = dx
        dg_ref[...] += jnp.sum(dyv * vv * r, axis=0, keepdims=True)

    row = pl.BlockSpec((tm, d), lambda i: (i, 0))
    vec = pl.BlockSpec((1, d), lambda i: (0, 0))
    if with_bf16:
        shapes = [jax.ShapeDtypeStruct((s, d), BF16)] * 2
        specs = [row, row]
    else:
        shapes = [jax.ShapeDtypeStruct((s, d), F32)]
        specs = [row]
    shapes.append(jax.ShapeDtypeStruct((1, d), F32))
    specs.append(vec)
    return pl.pallas_call(
        body, name=name, grid=(s // tm,), out_shape=shapes,
        in_specs=[row, row, vec, row, pl.BlockSpec((HEAD, HEAD), lambda i: (0, 0))], out_specs=specs,
        compiler_params=_params(("arbitrary",), 56),
    )(dy, v, g, res, perm)


DEP_SPEC = pl.BlockSpec((SUB, HEAD), lambda *_: (0, 0))


def _drop_arg(body, pos):
    return lambda *refs: body(*refs[:pos], *refs[pos + 1:])


def _matmul(a, b, *, name, grid, a_spec, b_spec, dims, acc_shape, out_shapes, out_specs,
            extra=(), extra_specs=(), epilogue=None, vmem_mib=48, dep=None, prefetch=None, carry=None):
    nk = grid[2]
    n_user = len(extra)
    for unread, spec in ((dep, DEP_SPEC), (carry, HBM_SPEC)):
        if unread is not None:
            extra, extra_specs = tuple(extra) + (unread,), tuple(extra_specs) + (spec,)
    n_extra, n_out = len(extra), len(out_shapes)
    n_pre = 0 if prefetch is None else 1
    aliases = {} if carry is None else {n_pre + 2 + n_extra - 1: 0}

    def body(*refs):
        refs = refs[n_pre:]
        a_ref, b_ref = refs[0], refs[1]
        ex = refs[2:2 + n_user]
        outs = refs[2 + n_extra:2 + n_extra + n_out]
        def lhs():
            av = a_ref[...]
            return av if av.dtype == BF16 else av.astype(BF16)

        if nk == 1 and epilogue is None:
            outs[0][...] = _dot(lhs(), b_ref[...], dims).astype(outs[0].dtype)
            return
        acc = refs[-1]
        k = pl.program_id(2)

        @pl.when(k == 0)
        def _():
            acc[...] = jnp.zeros_like(acc)

        av = lhs()
        if len(b_ref.shape) == 3:
            span = b_ref.shape[2]
            acc[...] += sum(_dot(av[:, g * span:(g + 1) * span], b_ref[g], dims) for g in range(b_ref.shape[0]))
        else:
            acc[...] += _dot(av, b_ref[...], dims)

        @pl.when(k == nk - 1)
        def _():
            if epilogue is None:
                outs[0][...] = acc[...].astype(outs[0].dtype)
            else:
                epilogue(acc, ex, outs)

    direct = nk == 1 and epilogue is None
    scratch = [] if direct else [pltpu.VMEM(acc_shape, F32)]
    params = _params(("parallel", "parallel", "arbitrary"), vmem_mib)
    if prefetch is None:
        return pl.pallas_call(
            body, name=name, grid=grid, out_shape=list(out_shapes),
            in_specs=[a_spec, b_spec, *extra_specs], out_specs=list(out_specs),
            scratch_shapes=scratch, compiler_params=params, input_output_aliases=aliases,
        )(a, b, *extra)
    return pl.pallas_call(
        body, name=name, out_shape=list(out_shapes),
        grid_spec=pltpu.PrefetchScalarGridSpec(
            num_scalar_prefetch=1, grid=grid, in_specs=[a_spec, b_spec, *extra_specs],
            out_specs=list(out_specs), scratch_shapes=scratch),
        compiler_params=params, input_output_aliases=aliases,
    )(prefetch, a, b, *extra)


def _tile(n, want):
    t = min(n, want)
    while n % t:
        t //= 2
    return t


def _rel_buckets(dil):
    order = BLOCK_ORDER[dil]
    qi = jnp.asarray(HEAD + order)
    kj = jnp.asarray(np.concatenate([order, HEAD + order]))
    delta = qi[:, None] - kj[None, :]
    band = (delta >= 0) & (delta <= HEAD)
    dist = jnp.clip(delta, 0, None) * dil
    max_exact = NUM_BUCKETS // 2
    dd = jnp.maximum(dist, 1).astype(F32)
    large = max_exact + (jnp.log(dd / max_exact) / math.log(MAX_DISTANCE / max_exact)
                         * (NUM_BUCKETS - max_exact)).astype(jnp.int32)
    large = jnp.minimum(large, NUM_BUCKETS - 1)
    bucket = jnp.where(dist < max_exact, dist, large)
    return jnp.where(band, bucket, -1).astype(jnp.int32)


def _bias_build(rel_bias, buckets, n_heads):
    nd = buckets.shape[0]

    def body(rb_ref, bk_ref, o_ref):
        for c in range(nd):
            def per_head(h, carry, c=c):
                bk = bk_ref[c]
                acc = jnp.where(bk < 0, NEG_INF, 0.0).astype(F32)
                for b in range(NUM_BUCKETS):
                    acc = jnp.where(bk == b, rb_ref[b, h], acc)
                o_ref[c, h] = acc
                return carry

            lax.fori_loop(0, n_heads, per_head, 0)

    return pl.pallas_call(
        body, name="bias_build",
        out_shape=jax.ShapeDtypeStruct((nd, n_heads, HEAD, 2 * HEAD), F32),
        in_specs=[pl.BlockSpec(memory_space=pltpu.SMEM), pl.BlockSpec(memory_space=pltpu.VMEM)],
        out_specs=pl.BlockSpec(memory_space=pltpu.VMEM),
    )(rel_bias, buckets)


def _bias_grad(ds_all, buckets, n_heads):
    nd = buckets.shape[0]
    pairs = HEAD * 2 * HEAD

    def body(ds_ref, bk_ref, o_ref):
        rows = lax.broadcasted_iota(jnp.int32, (NUM_BUCKETS, pairs), 0)
        tot = jnp.zeros((n_heads, NUM_BUCKETS), F32)
        for c in range(nd):
            onehot = (rows == bk_ref[c]).astype(BF16)
            ds = ds_ref[c]
            hi = ds.astype(BF16)
            lo = (ds - hi.astype(F32)).astype(BF16)
            tot = tot + _dot(hi, onehot, NT_DIMS) + _dot(lo, onehot, NT_DIMS)
        o_ref[...] = tot

    out = pl.pallas_call(
        body, name="bias_grad",
        out_shape=jax.ShapeDtypeStruct((n_heads, NUM_BUCKETS), F32),
        in_specs=[pl.BlockSpec(memory_space=pltpu.VMEM), pl.BlockSpec(memory_space=pltpu.VMEM)],
        out_specs=pl.BlockSpec(memory_space=pltpu.VMEM),
        compiler_params=pltpu.CompilerParams(vmem_limit_bytes=40 * MIB),
    )(ds_all.reshape(nd, n_heads, pairs), buckets.reshape(nd, 1, pairs))
    return out.T


def _qkv_prep(proj, g_q, g_k, w, dep=None):
    s = proj.shape[0]
    n_heads = w // HEAD
    tm = _tile(s, 2 * HEAD)

    def body(q_ref, k_ref, gq_ref, gk_ref, qn_ref, kn_ref):
        gq = gq_ref[...] * (HEAD ** -0.5)
        gk = gk_ref[...]
        for h in range(n_heads):
            sl = slice(h * HEAD, (h + 1) * HEAD)
            q = q_ref[:, sl]
            k = k_ref[:, sl]
            qn_ref[:, sl] = q * _rms(q) * gq
            kn_ref[:, sl] = k * _rms(k) * gk

    seg = lambda j: pl.BlockSpec((tm, w), lambda i, j=j: (i, j))
    vec = pl.BlockSpec((1, HEAD), lambda i: (0, 0))
    out = pl.BlockSpec((tm, w), lambda i: (i, 0))
    in_specs = [seg(3), seg(4), vec, vec]
    operands = [proj, proj, g_q, g_k]
    if dep is not None:
        body = _drop_arg(body, len(operands))
        in_specs.append(DEP_SPEC)
        operands.append(dep)
    return pl.pallas_call(
        body, name="qkv_prep", grid=(s // tm,),
        out_shape=[jax.ShapeDtypeStruct((s, w), F32)] * 2,
        in_specs=in_specs, out_specs=[out, out],
        compiler_params=_params(("arbitrary",), 40),
    )(*operands)


class _BlockView:
    def __init__(self, s, dil):
        assert s % (HEAD * dil) == 0 and dil in BLOCK_ORDER
        self.nb = s // (HEAD * dil)
        if dil == 1:
            self.lead, self.block = (s,), (HEAD,)
            self.index = lambda r, n: (n,)
        elif dil == 4:
            self.lead, self.block = (s // 512, 4, 4, 4, SUB), (None, 4, 4, None, SUB)
            self.index = lambda r, n: (n, 0, 0, r, 0)
        else:
            self.lead, self.block = (s // 2048, 16, 16, SUB), (None, 16, None, SUB)
            self.index = lambda r, n: (n, 0, r, 0)

    def view(self, t):
        return t.reshape(self.lead + (t.shape[-1],))

    def spec(self, width, block_of, column=0):
        return pl.BlockSpec(self.block + (width,), lambda r, n: self.index(r, block_of(r, n)) + (column,))


def _rows(ref, lanes=slice(None)):
    v = ref[(slice(None),) * (len(ref.shape) - 1) + (lanes,)]
    return v.reshape(HEAD, v.shape[-1])


def _set_rows(ref, lanes, value):
    ref[(slice(None),) * (len(ref.shape) - 1) + (lanes,)] = value.reshape(ref.shape[:-1] + (value.shape[-1],))


V_SEGMENT = 5


def _attn_fwd(qn, kn, proj, bias, dil, name, dep=None):
    s, w = qn.shape
    n_heads = w // HEAD
    bv = _BlockView(s, dil)

    def body(q_ref, kc_ref, vc_ref, bias_ref, o_ref, lse_ref, s_scr, e_scr, lse_scr, inv_scr, k_prev, v_prev):
        n = pl.program_id(1)
        heads = [slice(h * HEAD, (h + 1) * HEAD) for h in range(n_heads)]
        lse_scr[...] = jnp.zeros_like(lse_scr)

        @pl.when(n == 0)
        def _():
            k_prev[...] = jnp.zeros_like(k_prev)
            v_prev[...] = jnp.zeros_like(v_prev)

        for h, sl in enumerate(heads):
            q = _rows(q_ref, sl).astype(BF16)
            s_p = _dot(q, k_prev[:, sl], NT_DIMS) + bias_ref[h, :, :HEAD]
            s_scr[h, :, :HEAD] = jnp.where(n > 0, s_p, NEG_INF)
            s_scr[h, :, HEAD:] = _dot(q, _rows(kc_ref, sl).astype(BF16), NT_DIMS) + bias_ref[h, :, HEAD:]
        for h in range(n_heads):
            sc = s_scr[h]
            m = jnp.max(sc, axis=-1, keepdims=True)
            e = jnp.exp(sc - m)
            den = jnp.sum(e, axis=-1, keepdims=True)
            e_scr[h] = e.astype(BF16)
            lse_scr[:, h:h + 1] = m + jnp.log(den)
            inv_scr[:, h:h + 1] = 1.0 / den
        for h, sl in enumerate(heads):
            v_cur = _rows(vc_ref, sl).astype(BF16)
            o = _dot(e_scr[h, :, :HEAD], v_prev[:, sl]) + _dot(e_scr[h, :, HEAD:], v_cur)
            _set_rows(o_ref, sl, o * inv_scr[:, h:h + 1])
            v_prev[:, sl] = v_cur
            k_prev[:, sl] = _rows(kc_ref, sl).astype(BF16)
        _set_rows(lse_ref, slice(None), lse_scr[...])

    cur = bv.spec(w, lambda r, n: n)
    in_specs = [cur, cur, bv.spec(w, lambda r, n: n, V_SEGMENT),
                pl.BlockSpec((n_heads, HEAD, 2 * HEAD), lambda r, n: (0, 0, 0))]
    operands = [bv.view(qn), bv.view(kn), bv.view(proj), bias]
    if dep is not None:
        body = _drop_arg(body, len(operands))
        in_specs.append(DEP_SPEC)
        operands.append(dep)
    o, lse = pl.pallas_call(
        body, name=name, grid=(dil, bv.nb),
        out_shape=[jax.ShapeDtypeStruct(bv.lead + (w,), F32), jax.ShapeDtypeStruct(bv.lead + (HEAD,), F32)],
        in_specs=in_specs,
        out_specs=[cur, bv.spec(HEAD, lambda r, n: n)],
        scratch_shapes=[pltpu.VMEM((n_heads, HEAD, 2 * HEAD), F32), pltpu.VMEM((n_heads, HEAD, 2 * HEAD), BF16),
                        pltpu.VMEM((HEAD, HEAD), F32), pltpu.VMEM((HEAD, HEAD), F32),
                        pltpu.VMEM((HEAD, w), BF16), pltpu.VMEM((HEAD, w), BF16)],
        compiler_params=_params(("arbitrary", "arbitrary"), 48),
    )(*operands)
    return o.reshape(s, w), lse.reshape(s, HEAD)


def _attn_bwd(qn, kn, proj, dyb, lse, delta, bias, dil, name, running=None, dep=None):
    s, w = qn.shape
    n_heads = w // HEAD
    bv = _BlockView(s, dil)
    nb = bv.nb

    n_run = 0 if running is None else 3

    def body(q_ref, kc_ref, kp_ref, vc_ref, vp_ref, dy_ref, lse_ref, dl_ref, bias_ref, *rest):
        so_far = rest[:n_run]
        dq_ref, dk_ref, dv_ref, ds_ref, carry_k, carry_v, s_scr, dp_scr, p_scr, dsb_scr, k_cur, v_cur = rest[n_run:]
        base = (lambda i, sl: _rows(so_far[i], sl)) if n_run else (lambda i, sl: 0.0)
        r = pl.program_id(0)
        step = pl.program_id(1)
        blk = nb - 1 - step

        @pl.when((r == 0) & (step == 0))
        def _():
            ds_ref[...] = jnp.zeros_like(ds_ref)

        @pl.when(step == 0)
        def _():
            carry_k[...] = jnp.zeros_like(carry_k)
            carry_v[...] = jnp.zeros_like(carry_v)
            k_cur[...] = _rows(kc_ref).astype(BF16)
            v_cur[...] = _rows(vc_ref).astype(BF16)

        heads = [slice(h * HEAD, (h + 1) * HEAD) for h in range(n_heads)]
        tots = _rows(lse_ref)
        dls = _rows(dl_ref)
        for h, sl in enumerate(heads):
            q, dy = _rows(q_ref, sl).astype(BF16), _rows(dy_ref, sl).astype(BF16)
            kp, kc = _rows(kp_ref, sl).astype(BF16), k_cur[:, sl]
            vp, vc = _rows(vp_ref, sl).astype(BF16), v_cur[:, sl]
            s_p = _dot(q, kp, NT_DIMS) + bias_ref[h, :, :HEAD]
            s_scr[h, :, :HEAD] = jnp.where(blk > 0, s_p, NEG_INF)
            s_scr[h, :, HEAD:] = _dot(q, kc, NT_DIMS) + bias_ref[h, :, HEAD:]
            dp_scr[h, :, :HEAD] = _dot(dy, vp, NT_DIMS)
            dp_scr[h, :, HEAD:] = _dot(dy, vc, NT_DIMS)
        for h in range(n_heads):
            prob = jnp.exp(s_scr[h] - tots[:, h:h + 1])
            ds = prob * (dp_scr[h] - dls[:, h:h + 1])
            ds_ref[h] += ds
            p_scr[h] = prob.astype(BF16)
            dsb_scr[h] = ds.astype(BF16)
        for h, sl in enumerate(heads):
            q, dy = _rows(q_ref, sl).astype(BF16), _rows(dy_ref, sl).astype(BF16)
            kp, kc = _rows(kp_ref, sl).astype(BF16), k_cur[:, sl]
            ds_pb, ds_cb = dsb_scr[h, :, :HEAD], dsb_scr[h, :, HEAD:]
            _set_rows(dq_ref, sl, _dot(ds_pb, kp) + _dot(ds_cb, kc) + base(0, sl))
            _set_rows(dk_ref, sl, _dot(ds_cb, q, TN_DIMS) + carry_k[:, sl] + base(1, sl))
            carry_k[:, sl] = _dot(ds_pb, q, TN_DIMS)
            _set_rows(dv_ref, sl, _dot(p_scr[h, :, HEAD:], dy, TN_DIMS) + carry_v[:, sl] + base(2, sl))
            carry_v[:, sl] = _dot(p_scr[h, :, :HEAD], dy, TN_DIMS)
            k_cur[:, sl] = kp
            v_cur[:, sl] = _rows(vp_ref, sl).astype(BF16)

    cur = bv.spec(w, lambda r, n: nb - 1 - n)
    last = bv.spec(w, lambda r, n: nb - 1)
    prev = bv.spec(w, lambda r, n: jnp.maximum(nb - 2 - n, 0))
    stat = bv.spec(HEAD, lambda r, n: nb - 1 - n)
    whole = pl.BlockSpec((n_heads, HEAD, 2 * HEAD), lambda r, n: (0, 0, 0))
    big = jax.ShapeDtypeStruct(bv.lead + (w,), F32)
    v_last = bv.spec(w, lambda r, n: nb - 1, V_SEGMENT)
    v_prev = bv.spec(w, lambda r, n: jnp.maximum(nb - 2 - n, 0), V_SEGMENT)
    in_specs = [cur, last, prev, v_last, v_prev, cur, stat, stat, whole]
    operands = [bv.view(qn), bv.view(kn), bv.view(kn), bv.view(proj), bv.view(proj), bv.view(dyb), bv.view(lse),
                bv.view(delta), bias]
    aliases = {}
    if running is not None:
        aliases = {len(operands) + i: i for i in range(3)}
        in_specs += [cur] * 3
        operands += [bv.view(t) for t in running]
    if dep is not None:
        body = _drop_arg(body, len(operands))
        in_specs.append(DEP_SPEC)
        operands.append(dep)
    dq, dk, dv, ds = pl.pallas_call(
        body, name=name, grid=(dil, nb),
        out_shape=[big, big, big, jax.ShapeDtypeStruct((n_heads, HEAD, 2 * HEAD), F32)],
        in_specs=in_specs, out_specs=[cur, cur, cur, whole], input_output_aliases=aliases,
        scratch_shapes=[pltpu.VMEM((HEAD, w), F32), pltpu.VMEM((HEAD, w), F32),
                        pltpu.VMEM((n_heads, HEAD, 2 * HEAD), F32), pltpu.VMEM((n_heads, HEAD, 2 * HEAD), F32),
                        pltpu.VMEM((n_heads, HEAD, 2 * HEAD), BF16), pltpu.VMEM((n_heads, HEAD, 2 * HEAD), BF16),
                        pltpu.VMEM((HEAD, w), BF16), pltpu.VMEM((HEAD, w), BF16)],
        compiler_params=_params(("arbitrary", "arbitrary"), 56),
    )(*operands)
    return dq.reshape(s, w), dk.reshape(s, w), dv.reshape(s, w), ds


def _qkv_bwd(dproj, proj, dq, dk, dv, g_q, g_k, w):
    s = proj.shape[0]
    n_heads = w // HEAD
    tm = _tile(s, 2 * HEAD)

    def body(dproj_hbm, q_ref, k_ref, gq_ref, gk_ref, dq_ref, dk_ref, dv_ref, out_ref, dgq_ref, dgk_ref):
        i = pl.program_id(0)
        gq = gq_ref[...] * (HEAD ** -0.5)
        gk = gk_ref[...]
        acc_q = jnp.zeros((1, HEAD), F32)
        acc_k = jnp.zeros((1, HEAD), F32)
        for h in range(n_heads):
            sl = slice(h * HEAD, (h + 1) * HEAD)
            q, k = q_ref[:, sl], k_ref[:, sl]
            dqn, dkn = dq_ref[:, sl], dk_ref[:, sl]
            rq, rk = _rms(q), _rms(k)
            out_ref[:, h * HEAD:(h + 1) * HEAD] = _rms_bwd(dqn, q, rq, gq).astype(BF16)
            out_ref[:, w + h * HEAD:w + (h + 1) * HEAD] = _rms_bwd(dkn, k, rk, gk).astype(BF16)
            acc_q += jnp.sum(dqn * q * rq, axis=0, keepdims=True)
            acc_k += jnp.sum(dkn * k * rk, axis=0, keepdims=True)
        out_ref[:, 2 * w:] = dv_ref[...].astype(BF16)

        @pl.when(i == 0)
        def _():
            dgq_ref[...] = jnp.zeros_like(dgq_ref)
            dgk_ref[...] = jnp.zeros_like(dgk_ref)

        dgq_ref[...] += acc_q * (HEAD ** -0.5)
        dgk_ref[...] += acc_k

    seg = lambda j: pl.BlockSpec((tm, w), lambda i, j=j: (i, j))
    vec = pl.BlockSpec((1, HEAD), lambda i: (0, 0))
    row = pl.BlockSpec((tm, w), lambda i: (i, 0))
    return pl.pallas_call(
        body, name="qkv_bwd", grid=(s // tm,),
        out_shape=[jax.ShapeDtypeStruct(dproj.shape, BF16),
                   jax.ShapeDtypeStruct((1, HEAD), F32), jax.ShapeDtypeStruct((1, HEAD), F32)],
        in_specs=[pl.BlockSpec(memory_space=pl.ANY), seg(3), seg(4), vec, vec] + [row] * 3,
        out_specs=[pl.BlockSpec((tm, 3 * w), lambda i: (i, 1)), vec, vec],
        input_output_aliases={0: 0},
        compiler_params=_params(("arbitrary",), 48),
    )(dproj, proj, proj, g_q, g_k, dq, dk, dv)


def _mixer_a(u, gv, ws_ref, bst_ref, lng, lnb, z_scr, ln_scr):
    n_groups = u.shape[1] // HEAD
    mu = jnp.mean(gv, axis=-1, keepdims=True)
    xc = gv - mu
    rs = lax.rsqrt(jnp.mean(xc * xc, axis=-1, keepdims=True) + EPS)
    xhat = xc * rs
    ln_scr[...] = (xhat * lng + lnb).astype(BF16)
    causal = _causal_mask()
    for g in range(n_groups):
        sl = slice(g * HEAD, (g + 1) * HEAD)
        wm = jnp.where(causal, ws_ref[g], 0.0).astype(BF16)
        z_scr[:, sl] = _dot(wm, ln_scr[:, sl]) + bst_ref[:, g:g + 1]
    return u, xhat, rs


def _causal_mask():
    token = lambda r: 16 * (r % SUB) + r // SUB
    row = lax.broadcasted_iota(jnp.int32, (HEAD, HEAD), 0)
    col = lax.broadcasted_iota(jnp.int32, (HEAD, HEAD), 1)
    return token(col) <= token(row)


def _merge_b(o_refs, lse_refs, yb_scr):
    n_heads = yb_scr.shape[1] // HEAD
    lses = [t[...] for t in lse_refs]
    m = jnp.maximum(jnp.maximum(lses[0], lses[1]), lses[2])
    tot = m + jnp.log(sum(jnp.exp(t - m) for t in lses))
    alphas = [jnp.exp(t - tot) for t in lses]
    for h in range(n_heads):
        sl = slice(h * HEAD, (h + 1) * HEAD)
        yb_scr[:, sl] = sum(a[:, h:h + 1] * o[:, sl].astype(F32) for a, o in zip(alphas, o_refs))
    return tot


def _mix_fwd(proj, outs, lses, w_s, bst, ln_g, ln_b, g_a, g_b, w, dep=None):
    s = proj.shape[0]
    n_groups = w // HEAD

    def body(au_ref, av_ref, az_ref, bz_ref, o1, o2, o3, l1, l2, l3, ws_ref, bst_ref,
             lng_ref, lnb_ref, ga_ref, gb_ref, p_ref, y_ref, z_scr, ln_scr, yb_scr):
        u, _, _ = _mixer_a(_gelu(au_ref[...]), _gelu(av_ref[...]), ws_ref, bst_ref, lng_ref[...], lnb_ref[...],
                           z_scr, ln_scr)
        ya = u * z_scr[...]
        silu_a, _ = _silu_and_grad(az_ref[...])
        perm = p_ref[...]
        y_ref[:, :w] = _dot(perm, (ya * _rms(ya) * ga_ref[...] * silu_a).astype(BF16), TN_DIMS).astype(BF16)
        _merge_b((o1, o2, o3), (l1, l2, l3), yb_scr)
        yb = yb_scr[...]
        silu_b, _ = _silu_and_grad(bz_ref[...])
        y_ref[:, w:] = _dot(perm, (yb * _rms(yb) * gb_ref[...] * silu_b).astype(BF16), TN_DIMS).astype(BF16)

    seg = lambda j: pl.BlockSpec((HEAD, w), lambda i, j=j: (i, j))
    row = pl.BlockSpec((HEAD, w), lambda i: (i, 0))
    stat = pl.BlockSpec((HEAD, HEAD), lambda i: (i, 0))
    vec = pl.BlockSpec((1, w), lambda i: (0, 0))
    in_specs = [seg(0), seg(1), seg(2), seg(6), row, row, row, stat, stat, stat,
                pl.BlockSpec((n_groups, HEAD, HEAD), lambda i: (0, 0, 0)),
                pl.BlockSpec((HEAD, n_groups), lambda i: (0, 0)), vec, vec, vec, vec,
                pl.BlockSpec((HEAD, HEAD), lambda i: (0, 0))]
    operands = [proj, proj, proj, proj, *outs, *lses, w_s, bst, ln_g, ln_b, g_a, g_b, _chunk_perm()]
    if dep is not None:
        body = _drop_arg(body, len(operands))
        in_specs.append(DEP_SPEC)
        operands.append(dep)
    return pl.pallas_call(
        body, name="mix_fwd", grid=(s // HEAD,),
        out_shape=jax.ShapeDtypeStruct((s, 2 * w), BF16), in_specs=in_specs,
        out_specs=pl.BlockSpec((HEAD, 2 * w), lambda i: (i, 0)),
        scratch_shapes=[pltpu.VMEM((HEAD, w), F32), pltpu.VMEM((HEAD, w), BF16), pltpu.VMEM((HEAD, w), F32)],
        compiler_params=_params(("arbitrary",), 48),
    )(*operands)


def _mix_bwd(proj, dy, outs, lses, w_s, bst, ln_g, ln_b, g_a, g_b, w):
    s = proj.shape[0]
    n_groups = w // HEAD

    def body(au_ref, av_ref, az_ref, bz_ref, dy_ref, o1, o2, o3, l1, l2, l3, ws_ref, bst_ref,
             lng_ref, lnb_ref, ga_ref, gb_ref,
             dproj_ref, dyb_ref, tot_ref, dl_ref, dws_ref, dbst_ref, dlng_ref, dlnb_ref, dga_ref, dgb_ref,
             z_scr, ln_scr, yb_scr, dz_scr, dln_scr):
        i = pl.program_id(0)

        @pl.when(i == 0)
        def _():
            for t in (dws_ref, dbst_ref, dlng_ref, dlnb_ref, dga_ref, dgb_ref):
                t[...] = jnp.zeros_like(t)

        az = az_ref[...]
        lng = lng_ref[...]
        u, du_dau = _gelu_and_grad(au_ref[...])
        gv, dgv_dav = _gelu_and_grad(av_ref[...])
        u, xhat, rs = _mixer_a(u, gv, ws_ref, bst_ref, lng, lnb_ref[...], z_scr, ln_scr)
        z = z_scr[...]
        ya = u * z
        ra = _rms(ya)
        silu_a, dsilu_a = _silu_and_grad(az)
        dya_all = dy_ref[:, :w].astype(F32)
        na = ya * ra * ga_ref[...]
        dna = dya_all * silu_a
        dproj_ref[:, 2 * w:3 * w] = (dya_all * na * dsilu_a).astype(BF16)
        dga_ref[...] += jnp.sum(dna * ya * ra, axis=0, keepdims=True)
        dya = _rms_bwd(dna, ya, ra, ga_ref[...])
        dproj_ref[:, :w] = (dya * z * du_dau).astype(BF16)
        dz_scr[...] = (dya * u).astype(BF16)

        causal = _causal_mask()
        for g in range(n_groups):
            sl = slice(g * HEAD, (g + 1) * HEAD)
            wm = jnp.where(causal, ws_ref[g], 0.0).astype(BF16)
            dz = dz_scr[:, sl]
            dln_scr[:, sl] = _dot(wm, dz, TN_DIMS)
            dws_ref[g] += jnp.where(causal, _dot(dz, ln_scr[:, sl], NT_DIMS), 0.0)
            dbst_ref[:, g:g + 1] += jnp.sum(dz.astype(F32), axis=-1, keepdims=True)
        dln = dln_scr[...]
        dlng_ref[...] += jnp.sum(dln * xhat, axis=0, keepdims=True)
        dlnb_ref[...] += jnp.sum(dln, axis=0, keepdims=True)
        gy = dln * lng
        dgv = rs * (gy - jnp.mean(gy, axis=-1, keepdims=True)
                    - xhat * jnp.mean(gy * xhat, axis=-1, keepdims=True))
        dproj_ref[:, w:2 * w] = (dgv * dgv_dav).astype(BF16)
        dproj_ref[:, 3 * w:6 * w] = jnp.zeros((HEAD, 3 * w), BF16)

        tot_ref[...] = _merge_b((o1, o2, o3), (l1, l2, l3), yb_scr)
        yb = yb_scr[...]
        rb = _rms(yb)
        bz = bz_ref[...]
        silu_b, dsilu_b = _silu_and_grad(bz)
        dyb_all = dy_ref[:, w:].astype(F32)
        dnb = dyb_all * silu_b
        dproj_ref[:, 6 * w:] = (dyb_all * yb * rb * gb_ref[...] * dsilu_b).astype(BF16)
        dgb_ref[...] += jnp.sum(dnb * yb * rb, axis=0, keepdims=True)
        dyb = _rms_bwd(dnb, yb, rb, gb_ref[...])
        dyb_ref[...] = dyb
        prod = dyb * yb
        dl_ref[...] = _lane_pick(
            [jnp.sum(prod[:, h * HEAD:(h + 1) * HEAD], axis=-1, keepdims=True) for h in range(n_groups)], HEAD)

    seg = lambda j: pl.BlockSpec((HEAD, w), lambda i, j=j: (i, j))
    row_w = pl.BlockSpec((HEAD, w), lambda i: (i, 0))
    stat = pl.BlockSpec((HEAD, HEAD), lambda i: (i, 0))
    vec = pl.BlockSpec((1, w), lambda i: (0, 0))
    ws_spec = pl.BlockSpec((n_groups, HEAD, HEAD), lambda i: (0, 0, 0))
    bst_spec = pl.BlockSpec((HEAD, n_groups), lambda i: (0, 0))
    vec_shape = jax.ShapeDtypeStruct((1, w), F32)
    return pl.pallas_call(
        body, name="mix_bwd", grid=(s // HEAD,),
        out_shape=[jax.ShapeDtypeStruct((s, N_SEG * w), BF16), jax.ShapeDtypeStruct((s, w), F32),
                   jax.ShapeDtypeStruct((s, HEAD), F32), jax.ShapeDtypeStruct((s, HEAD), F32),
                   jax.ShapeDtypeStruct((n_groups, HEAD, HEAD), F32), jax.ShapeDtypeStruct((HEAD, n_groups), F32),
                   vec_shape, vec_shape, vec_shape, vec_shape],
        in_specs=[seg(0), seg(1), seg(2), seg(6), pl.BlockSpec((HEAD, 2 * w), lambda i: (i, 0)),
                  row_w, row_w, row_w, stat, stat, stat, ws_spec, bst_spec, vec, vec, vec, vec],
        out_specs=[pl.BlockSpec((HEAD, N_SEG * w), lambda i: (i, 0)), row_w, stat, stat,
                   ws_spec, bst_spec, vec, vec, vec, vec],
        scratch_shapes=[pltpu.VMEM((HEAD, w), F32), pltpu.VMEM((HEAD, w), BF16), pltpu.VMEM((HEAD, w), F32),
                        pltpu.VMEM((HEAD, w), BF16), pltpu.VMEM((HEAD, w), F32)],
        compiler_params=_params(("arbitrary",), 56),
    )(proj, proj, proj, proj, dy, *outs, *lses, w_s, bst, ln_g, ln_b, g_a, g_b)


def _ple_gate(hn2, wgate, h, p, wup_g, tgt):
    s, d = h.shape
    n, p_dim, c_up = wup_g.shape
    tm, pair = _tile(s, 512), 2
    tn = pair * c_up
    cols = n // pair
    tiles = (s // tm) * cols
    cur = lambda t: jnp.minimum(t, tiles - 1)
    prv = lambda t: jnp.maximum(t - 1, 0)

    def body(a_ref, b_ref, h_ref, p_ref, wup_ref, tgt_ref, dout_ref, dpre_ref, dup_ref, loss_ref, acc):
        @pl.when(pl.program_id(0) == 0)
        def _():
            acc[...] = jnp.zeros_like(acc)

        gate = jax.nn.sigmoid(acc[...])
        pb = p_ref[...].astype(BF16)
        up = jnp.concatenate([_dot(pb, wup_ref[g]) for g in range(pair)], axis=1)
        err = h_ref[...] + gate * up - tgt_ref[...]
        dout = err * (1.0 / d)
        dout_ref[...] = dout.astype(BF16)
        dpre_ref[...] = (dout * up * gate * (1.0 - gate)).astype(BF16)
        dup_ref[...] = (dout * gate).astype(BF16)
        part = 0.5 * jnp.sum(err * err) * (1.0 / d)
        rr = lax.broadcasted_iota(jnp.int32, (SUB, HEAD), 0)
        cc = lax.broadcasted_iota(jnp.int32, (SUB, HEAD), 1)
        loss_ref[...] = jnp.where((rr == 0) & (cc == 0), part, 0.0)
        acc[...] = _dot(a_ref[...], b_ref[...])

    rows = s // tm
    tail = pl.BlockSpec((tm, tn), lambda t: (prv(t) % rows, prv(t) // rows))
    big = jax.ShapeDtypeStruct((s, d), BF16)
    return pl.pallas_call(
        body, name="ple_gate", grid=(tiles + 1,),
        out_shape=[big, big, big, jax.ShapeDtypeStruct((rows * SUB, cols * HEAD), F32)],
        in_specs=[pl.BlockSpec((tm, d), lambda t: (cur(t) % rows, 0)),
                  pl.BlockSpec((d, tn), lambda t: (0, cur(t) // rows)),
                  tail, pl.BlockSpec((tm, p_dim), lambda t: (prv(t) % rows, 0)),
                  pl.BlockSpec((pair, p_dim, c_up), lambda t: (prv(t) // rows, 0, 0)), tail],
        out_specs=[tail, tail, tail, pl.BlockSpec((SUB, HEAD), lambda t: (prv(t) % rows, prv(t) // rows))],
        scratch_shapes=[pltpu.VMEM((tm, tn), F32)],
        compiler_params=_params(("arbitrary",), 60),
    )(hn2, wgate, h, p, wup_g, tgt)


def _local_step(x, p, tgt, small, wg, ex, while_last_travels=None):
    s, d = x.shape
    n, _, c_in = wg.buffers["w_in"].shape
    assert n == N_DEV
    d_in = n * c_in
    w = d_in // N_SEG
    n_heads = w // HEAD
    p_dim, c_up = wg.buffers["w_ple_up"].shape[1:]
    assert s % (HEAD * DILATIONS[-1]) == 0 and w % HEAD == 0 and d == n * c_up == 2 * w

    near = (2, 4)
    wg.start("gather_in_pair", ["w_in"], (1,))
    token = wg.start("gather_in_first", ["w_in"], ("first",))
    rest = ["w_out", "w_ple_gate", "w_ple_up"]

    hn = _rmsnorm_fwd(x, small["g_pre"], "pre_norm", True, dep=token)
    tm = _tile(s, 1024)

    def in_proj(shards, name, carry, dep=None):
        return _matmul(
            hn, wg.buffers["w_in"], name=name, grid=(s // tm, len(shards), 1), dims=NN_DIMS,
            prefetch=jnp.stack(shards).astype(jnp.int32),
            a_spec=pl.BlockSpec((tm, d), lambda i, j, k, sh: (i, 0)),
            b_spec=pl.BlockSpec((None, d, c_in), lambda i, j, k, sh: (sh[j], 0, 0), pipeline_mode=pl.Buffered(1)),
            acc_shape=(tm, c_in), out_shapes=[jax.ShapeDtypeStruct((s, d_in), F32)],
            out_specs=[pl.BlockSpec((tm, c_in), lambda i, j, k, sh: (i, sh[j]))], carry=carry, dep=dep,
            vmem_mib=56)[0]

    me = wg.me
    core = me & 1
    first, second, far = me ^ (4 - 2 * core), me ^ (2 + 2 * core), me ^ 6
    proj = in_proj([me], "in_proj_own", None)
    token = wg.start("gather_in_second", ["w_in"], ("second",), after=[proj])
    wg.arrived("gather_in_pair", [token])
    proj = in_proj([me ^ 1], "in_proj_sibling", proj)
    buckets = jnp.stack([_rel_buckets(dil) for dil in DILATIONS])
    bias = _bias_build(small["rel_bias"], buckets, n_heads)
    ahead = [bias] + [wg.buffers[k] for k in rest]
    for tag, mine, from_sibling in (("first", first, second ^ 1), ("second", second, first ^ 1), ("far", far, far ^ 1)):
        token = wg.forward("gather_in_" + tag, [proj] + ahead)
        ahead = []
        if tag == "second":
            token = wg.relay_start("gather_in_far", ["w_in"], after=[token])
        elif tag == "far":
            token = wg.start("gather_rest", rest, (1,) + near, after=[token])
        proj = in_proj([mine], "in_proj_" + tag, proj, dep=token)
        wg.forwarded("gather_in_" + tag, [proj])
        proj = in_proj([from_sibling], "in_proj_%s_forwarded" % tag, proj)
    win_g = wg.buffers["w_in"]

    qn, kn = _qkv_prep(proj, small["g_q"], small["g_k"], w)
    token = wg.forward("gather_rest", [qn])
    token = wg.relay_start("gather_rest_far", rest, after=[token])
    outs, lses = [], []
    for c, dil in enumerate(DILATIONS):
        o, l = _attn_fwd(qn, kn, proj, bias[c], dil, "attn_fwd_d%d" % dil, dep=token)
        outs.append(o)
        lses.append(l)
    token = wg.forward("gather_rest_far", outs)

    ws_p = small["w_s"][:, CHUNK_ORDER][:, :, CHUNK_ORDER]
    bst = small["b_s"].T[CHUNK_ORDER]
    mix_args = (outs, lses, ws_p, bst, small["ln_v_g"], small["ln_v_b"], small["g_out_a"], small["g_out_b"], w)
    y = _mix_fwd(proj, *mix_args, dep=token)
    wg.forwarded("gather_rest", [y])
    wg.forwarded("gather_rest_far", [y])
    wout_g, wgate_g, wup_g = (wg.buffers[k] for k in rest)
    wout_f = wout_g.reshape(2 * w, d)
    wgate_f = wgate_g.reshape(d, d)

    tn = _tile(d, 1024)
    tk2 = 2 * w

    def resid_epilogue(acc, ex, outs_):
        outs_[0][...] = ex[0][...] + acc[...]

    h = _matmul(
        y, wout_f, name="out_proj", grid=(s // tm, d // tn, (2 * w) // tk2), dims=NN_DIMS,
        a_spec=pl.BlockSpec((tm, tk2), lambda i, j, k: (i, k)),
        b_spec=pl.BlockSpec((tk2, tn), lambda i, j, k: (k, j)),
        acc_shape=(tm, tn), out_shapes=[jax.ShapeDtypeStruct((s, d), F32)],
        out_specs=[pl.BlockSpec((tm, tn), lambda i, j, k: (i, j))],
        extra=(x,), extra_specs=(pl.BlockSpec((tm, tn), lambda i, j, k: (i, j)),),
        epilogue=resid_epilogue, vmem_mib=56)[0]

    hn2 = _rmsnorm_fwd(h, small["g_ple"], "ple_norm", False)

    dout, dpre, dup, loss_parts = _ple_gate(hn2, wgate_f, h, p, wup_g, tgt)
    loss = jnp.sum(loss_parts)

    tks = _tile(s, 2048)
    g_wup = _matmul(
        p, dup, name="grad_w_up", grid=(1, n, s // tks), dims=TN_DIMS,
        a_spec=pl.BlockSpec((tks, p_dim), lambda i, j, k: (k, 0)),
        b_spec=pl.BlockSpec((tks, c_up), lambda i, j, k: (k, j)),
        acc_shape=(p_dim, c_up), out_shapes=[jax.ShapeDtypeStruct((n, p_dim, c_up), BF16)],
        out_specs=[pl.BlockSpec((None, p_dim, c_up), lambda i, j, k: (j, 0, 0))])[0]

    def tn_matmul(a, b, name):
        m_, n_ = a.shape[1], b.shape[1]
        bm, bn = _tile(m_, 1024), _tile(n_, 1024)
        return _matmul(
            a, b, name=name, grid=(m_ // bm, n_ // bn, 1), dims=TN_DIMS,
            a_spec=pl.BlockSpec((s, bm), lambda i, j, k: (0, i)),
            b_spec=pl.BlockSpec((s, bn), lambda i, j, k: (0, j)),
            acc_shape=(bm, bn), out_shapes=[jax.ShapeDtypeStruct((m_, n_), BF16)],
            out_specs=[pl.BlockSpec((bm, bn), lambda i, j, k: (i, j))], vmem_mib=56)[0]

    def nt_matmul(a, b, name, out_dtype, dep=None):
        k_, n_ = a.shape[1], b.shape[0]
        bm, bn, bk = _tile(s, 1024), _tile(n_, 1024), k_
        return _matmul(
            a, b, name=name, grid=(s // bm, n_ // bn, k_ // bk), dims=NT_DIMS,
            a_spec=pl.BlockSpec((bm, bk), lambda i, j, k: (i, k)),
            b_spec=pl.BlockSpec((bn, bk), lambda i, j, k: (j, k)),
            acc_shape=(bm, bn), out_shapes=[jax.ShapeDtypeStruct((s, n_), out_dtype)],
            out_specs=[pl.BlockSpec((bm, bn), lambda i, j, k: (i, j))], dep=dep, vmem_mib=56)[0]

    by_core = lambda g: g.reshape((N_CHIP, 2) + g.shape[-2:])
    g_wgate = tn_matmul(hn2, dpre, "grad_w_gate").reshape(wgate_g.shape)
    dhn2 = nt_matmul(dpre, wgate_f, "ple_gate_bwd", BF16)
    dh_b, dh_bp, dg_ple = _rmsnorm_bwd(dhn2, h, small["g_ple"], dout, "ple_norm_bwd", False, True)
    g_wout = tn_matmul(y, dh_b, "grad_w_out").reshape(wout_g.shape)

    late = ("w_out", "w_ple_gate", "w_ple_up")
    late_parts = (g_wout, g_wgate, g_wup)
    token = ex.push_pairs("pair_late", [by_core(g) for g in late_parts])
    dy = nt_matmul(dh_bp, wout_f, "out_proj_bwd", BF16, dep=token)
    (dproj, dyb, lse_tot, delta, dws, dbst, dlng, dlnb, dga, dgb) = _mix_bwd(proj, dy, *mix_args)
    both_columns, from_sibling = ex.pairs_done("pair_late", [dproj])
    pair_sums = [_pair_add(mine.reshape((N_DEV,) + mine.shape[-2:]), theirs, "pair_add_" + k, ex.core)
                 for k, mine, theirs in zip(late, both_columns, from_sibling)]
    token = ex.push_chips("chip_late", pair_sums)

    running, dss = None, []
    for c, dil in enumerate(DILATIONS):
        *running, ds = _attn_bwd(qn, kn, proj, dyb, lse_tot, delta, bias[c], dil, "attn_bwd_d%d" % dil,
                                 running=running, dep=token if c == 0 else None)
        dss.append(ds)
    d_rel = _bias_grad(jnp.stack(dss), buckets, n_heads)
    dproj, dgq, dgk = _qkv_bwd(dproj, proj, *running, small["g_q"], small["g_k"], w)
    pair_sums, landed, _ = ex.chips_done("chip_late", [dproj])
    delivered = {k: (mine, theirs) for k, mine, theirs in zip(late, pair_sums, landed)}

    token_row = np.argsort(CHUNK_ORDER)
    dws = dws[:, token_row][:, :, token_row]
    dbst = dbst[token_row]
    small_grads = {
        "w_s": dws, "b_s": dbst.T, "ln_v_g": dlng, "ln_v_b": dlnb, "g_q": dgq, "g_k": dgk,
        "rel_bias": d_rel, "g_out_a": dga, "g_out_b": dgb, "g_ple": dg_ple,
    }

    bm = _tile(d, 1024)

    def grad_w_in(core, name, dep=None):
        return _matmul(
            hn, dproj, name=name, grid=(d // bm, N_CHIP, 1), dims=TN_DIMS, prefetch=core.reshape(1),
            a_spec=pl.BlockSpec((s, bm), lambda i, j, k, core_ref: (0, i)),
            b_spec=pl.BlockSpec((s, c_in), lambda i, j, k, core_ref: (0, 2 * j + core_ref[0])),
            acc_shape=(bm, c_in), out_shapes=[jax.ShapeDtypeStruct((N_CHIP, d, c_in), BF16)],
            out_specs=[pl.BlockSpec((None, bm, c_in), lambda i, j, k, core_ref: (j, i, 0))], dep=dep,
            vmem_mib=60)[0]

    for_sibling = grad_w_in(1 - ex.core, "grad_w_in_sibling")
    token = ex.push_pairs("pair_in", [for_sibling])
    mine = grad_w_in(ex.core, "grad_w_in_mine", dep=token)
    _, from_sibling = ex.pairs_done("pair_in", [mine])
    pair_sum = _pair_add(mine, from_sibling[0], "pair_add_w_in")
    token = ex.push_chips("chip_in", [pair_sum], _pack_small(small_grads, SMALL_EARLY))

    dhn = _matmul(
        dproj, win_g, name="in_proj_bwd", grid=(s // tm, d // tn, n // 2), dims=NT_DIMS,
        a_spec=pl.BlockSpec((tm, 2 * c_in), lambda i, j, k: (i, k)),
        b_spec=pl.BlockSpec((2, tn, c_in), lambda i, j, k: (k, j, 0)),
        acc_shape=(tm, tn), out_shapes=[jax.ShapeDtypeStruct((s, d), BF16)],
        out_specs=[pl.BlockSpec((tm, tn), lambda i, j, k: (i, j))], dep=token, vmem_mib=56)[0]
    grad_x, dg_pre = _rmsnorm_bwd(dhn, x, small["g_pre"], dh_b, "pre_norm_bwd", True, False)
    extra = while_last_travels(token, delivered, dg_pre) if while_last_travels is not None else []
    pair_sums, landed, slabs = ex.chips_done("chip_in", [grad_x] + list(extra))
    delivered["w_in"] = (pair_sums[0], landed[0])
    small_grads["g_pre"] = dg_pre
    return loss, grad_x, small_grads, delivered, slabs, extra


SMALL_EARLY = ("w_s", "b_s", "ln_v_g", "ln_v_b", "g_q", "g_k", "rel_bias", "g_out_a", "g_out_b", "g_ple")
SMALL_LAST = ("g_pre",)
SMALL_NAMES = SMALL_LAST + SMALL_EARLY


def _pack_small(tree, names):
    parts = []
    for name in names:
        flat = tree[name].astype(F32).reshape(-1)
        pad = (-flat.shape[0]) % HEAD
        parts.append(jnp.pad(flat, (0, pad)) if pad else flat)
    slab = jnp.concatenate(parts).reshape(-1, HEAD)
    pad_rows = (-slab.shape[0]) % 8
    return jnp.pad(slab, ((0, pad_rows), (0, 0))) if pad_rows else slab


def _unpack_small(slab, like, names):
    flat = slab.reshape(-1)
    out, off = {}, 0
    for name in names:
        size = like[name].size
        out[name] = flat[off:off + size].reshape(like[name].shape)
        off += size + (-size) % HEAD
    return out


def _peer(k):
    x, y, c = (lax.axis_index(a) for a in AXES)
    if k == "first":
        px, py, pc = x ^ (1 - c), y ^ c, c
    elif k == "second":
        px, py, pc = x ^ c, y ^ (1 - c), c
    else:
        bits = ((k >> 2) & 1, (k >> 1) & 1, k & 1)
        px, py, pc = (1 - v if b else v for v, b in zip((x, y, c), bits))
    return (px, py, pc), 4 * px + 2 * py + pc


def _my_index():
    x, y, c = (lax.axis_index(a) for a in AXES)
    return 4 * x + 2 * y + c


N_CHIP = 4
HBM_SPEC = pl.BlockSpec(memory_space=pl.ANY)


def _remote(src, dst, send_sem, recv_sem, peer):
    return pltpu.make_async_remote_copy(src_ref=src, dst_ref=dst, send_sem=send_sem, recv_sem=recv_sem,
                                        device_id=peer, device_id_type=pl.DeviceIdType.MESH)


SEM_SPEC = pl.BlockSpec(memory_space=pltpu.SEMAPHORE)
HBM_ONLY = pl.BlockSpec(memory_space=pltpu.HBM)
DATAFLOW = pltpu.SideEffectType.DATAFLOW_SIDE_EFFECTING


def _comm_call(name, arrays, *, wait=None, start=None, after=()):
    n, n_after = len(arrays), len(after)

    def body(*refs):
        ins = refs[:n]
        pos = n
        if wait is not None:
            for cp in wait[2](ins, refs[pos], refs[pos + 1]):
                cp.wait()
            pos += 2
        outs = refs[pos + n_after:]
        if start is not None:
            for cp in start[1](ins, outs[0], outs[1]):
                cp.start()
        outs[-1][...] = jnp.zeros_like(outs[-1])

    operands = [pltpu.with_memory_space_constraint(a, pltpu.HBM) for a in arrays]
    in_specs = [HBM_ONLY] * n
    if wait is not None:
        operands += [wait[0], wait[1]]
        in_specs += [SEM_SPEC, SEM_SPEC]
    operands += list(after)
    in_specs += [HBM_SPEC] * n_after
    out_shape, out_specs = [], []
    if start is not None:
        out_shape += [pltpu.SemaphoreType.DMA((start[0],))] * 2
        out_specs += [SEM_SPEC, SEM_SPEC]
    first = len(out_shape)
    out_shape += [pltpu.HBM(a.shape, a.dtype) for a in arrays] + [jax.ShapeDtypeStruct((SUB, HEAD), F32)]
    out_specs += [HBM_ONLY] * n + [pl.BlockSpec(memory_space=pltpu.VMEM)]
    res = pl.pallas_call(
        body, name=name, out_shape=tuple(out_shape), in_specs=tuple(in_specs), out_specs=tuple(out_specs),
        input_output_aliases={i: first + i for i in range(n)},
        compiler_params=pltpu.CompilerParams(has_side_effects=DATAFLOW),
    )(*operands)
    sems = (res[0], res[1]) if start is not None else None
    return list(res[first:first + n]), sems, res[-1]


class _GradExchange:
    def __init__(self):
        x, y, c = (lax.axis_index(a) for a in AXES)
        self.core = c.astype(jnp.int32)
        self.chip = (2 * x + y).astype(jnp.int32)
        self.pending = {}

    def _pair_copies(self, n_arr):
        def make(refs, send_sems, recv_sems):
            sibling, _ = _peer(1)
            other = 1 - lax.axis_index("c")
            srcs, lands = refs[:n_arr], refs[n_arr:]
            pick = lambda ref, ch: ref.at[ch, other] if len(ref.shape) == 4 else ref.at[ch]
            return [_remote(pick(srcs[a], ch), lands[a].at[ch], send_sems.at[a * N_CHIP + ch],
                            recv_sems.at[a * N_CHIP + ch], sibling)
                    for a in range(n_arr) for ch in range(N_CHIP)]
        return make

    def _chip_copies(self, n_arr, with_slab):
        def make(refs, send_sems, recv_sems):
            x, y = lax.axis_index("x"), lax.axis_index("y")
            my_chip = 2 * x + y
            srcs, lands = refs[:n_arr], refs[n_arr:2 * n_arr]
            copies = []
            for j, k in enumerate((6, 2, 4)):
                peer, peer_idx = _peer(k)
                for a in range(n_arr):
                    copies.append(_remote(srcs[a].at[peer_idx // 2], lands[a].at[my_chip],
                                          send_sems.at[3 * a + j], recv_sems.at[3 * a + j], peer))
            if with_slab:
                slab, slab_land = refs[2 * n_arr], refs[2 * n_arr + 1]
                for k in range(1, N_DEV):
                    peer, _ = _peer(k)
                    copies.append(_remote(slab, slab_land.at[_my_index()], send_sems.at[3 * n_arr + k - 1],
                                          recv_sems.at[3 * n_arr + k - 1], peer))
            return copies
        return make

    def push_pairs(self, tag, for_sibling):
        n_arr = len(for_sibling)
        lands = [lax.empty((N_CHIP,) + a.shape[-2:], a.dtype) for a in for_sibling]
        make = self._pair_copies(n_arr)
        arrays, sems, token = _comm_call(tag + "_start", list(for_sibling) + lands, start=(n_arr * N_CHIP, make))
        self.pending[tag] = (arrays, sems, make, n_arr)
        return token

    def pairs_done(self, tag, after):
        arrays, sems, make, n_arr = self.pending.pop(tag)
        arrays, _, _ = _comm_call(tag + "_wait", arrays, wait=(sems[0], sems[1], make), after=after)
        return arrays[:n_arr], arrays[n_arr:]

    def push_chips(self, tag, pair_sums, slab=None):
        n_arr = len(pair_sums)
        arrays = list(pair_sums) + [lax.empty(a.shape, a.dtype) for a in pair_sums]
        n_copies = 3 * n_arr
        if slab is not None:
            arrays += [slab, lax.empty((N_DEV,) + slab.shape, slab.dtype)]
            n_copies += N_DEV - 1
        make = self._chip_copies(n_arr, slab is not None)
        arrays, sems, token = _comm_call(tag + "_start", arrays, start=(n_copies, make))
        self.pending[tag] = (arrays, sems, make, n_arr)
        return token

    def chips_done(self, tag, after):
        arrays, sems, make, n_arr = self.pending.pop(tag)
        arrays, _, _ = _comm_call(tag + "_wait", arrays, wait=(sems[0], sems[1], make), after=after)
        return arrays[:n_arr], arrays[n_arr:2 * n_arr], arrays[2 * n_arr:]


def _cast_place(w, name):
    r, c = w.shape
    tr = r if r * c <= MIB else 1 << ((MIB // c).bit_length() - 1)
    assert r % tr == 0

    def body(me_ref, w_ref, o_ref):
        o_ref[...] = w_ref[...].astype(BF16)

    return pl.pallas_call(
        body, name=name, out_shape=jax.ShapeDtypeStruct((N_DEV, r, c), BF16),
        grid_spec=pltpu.PrefetchScalarGridSpec(
            num_scalar_prefetch=1, grid=(r // tr,),
            in_specs=[pl.BlockSpec((tr, c), lambda i, me_ref: (i, 0))],
            out_specs=pl.BlockSpec((None, tr, c), lambda i, me_ref: (me_ref[0], i, 0))),
        compiler_params=_params(("arbitrary",), 40),
    )(_my_index().astype(jnp.int32).reshape(1), w)


class _WeightGather:
    CHIPS = (2, 4, 6)

    def __init__(self, buffers):
        self.buffers = dict(buffers)
        self.pending = {}
        self.me = _my_index().astype(jnp.int32)

    def _own_slot_to(self, peers):
        def make(refs, send_sems, recv_sems):
            me = _my_index()
            return [_remote(ref.at[me], ref.at[me], send_sems.at[len(peers) * a + j],
                            recv_sems.at[len(peers) * a + j], _peer(k)[0])
                    for a, ref in enumerate(refs) for j, k in enumerate(peers)]
        return make

    def _forward_from(self, chips):
        def make(refs, send_sems, recv_sems):
            sibling, _ = _peer(1)
            copies = []
            for a, ref in enumerate(refs):
                for j, k in enumerate(chips):
                    slot = ref.at[_peer(k)[1]]
                    copies.append(_remote(slot, slot, send_sems.at[len(chips) * a + j],
                                          recv_sems.at[len(chips) * a + j], sibling))
            return copies
        return make

    def _run(self, call, names, **kw):
        arrays, sems, token = _comm_call(call, [self.buffers[k] for k in names], **kw)
        self.buffers.update(zip(names, arrays))
        return sems, token

    def start(self, tag, names, peers, after=()):
        make = self._own_slot_to(peers)
        sems, token = self._run(tag + "_start", names, start=(len(names) * len(peers), make), after=after)
        self.pending[tag] = (names, sems, make, peers)
        return token

    @staticmethod
    def _relay(refs, send_sems, recv_sems):
        x, y, c = (lax.axis_index(a) for a in AXES)
        peer = (x ^ (1 - c), y ^ c, c)
        slot = _my_index() ^ (2 + 2 * c)
        return [_remote(ref.at[slot], ref.at[slot], send_sems.at[a], recv_sems.at[a], peer)
                for a, ref in enumerate(refs)]

    def relay_start(self, tag, names, after=()):
        sems, token = self._run(tag + "_start", names, start=(len(names), self._relay), after=after)
        self.pending[tag] = (names, sems, self._relay, (6,))
        return token

    def arrived(self, tag, after):
        names, sems, make, _ = self.pending.pop(tag)
        self._run(tag + "_wait", names, wait=(sems[0], sems[1], make), after=after)

    def forward(self, tag, after):
        names, sems, make, peers = self.pending.pop(tag)
        chips = tuple(k for k in peers if k != 1)
        onward = self._forward_from(chips)
        new_sems, token = self._run(tag + "_forward", names, wait=(sems[0], sems[1], make),
                                    start=(len(chips) * len(names), onward), after=after)
        self.pending[tag + "/fwd"] = (names, new_sems, onward)
        return token

    def forwarded(self, tag, after):
        names, sems, make = self.pending.pop(tag + "/fwd")
        self._run(tag + "_done", names, wait=(sems[0], sems[1], make), after=after)


def _pair_add(mine, theirs, name, core=None):
    _, r, c_dim = theirs.shape
    tr = r if r * c_dim <= MIB else 1 << ((MIB // c_dim).bit_length() - 1)
    assert r % tr == 0
    stride = 1 if core is None else 2
    offset = jnp.zeros((1,), jnp.int32) if core is None else core.reshape(1)

    def body(off_ref, a_ref, b_ref, o_ref):
        o_ref[...] = (a_ref[...].astype(F32) + b_ref[...].astype(F32)).astype(BF16)

    blk = (None, tr, c_dim)
    return pl.pallas_call(
        body, name=name, out_shape=jax.ShapeDtypeStruct(theirs.shape, BF16),
        grid_spec=pltpu.PrefetchScalarGridSpec(
            num_scalar_prefetch=1, grid=(N_CHIP, r // tr),
            in_specs=[pl.BlockSpec(blk, lambda ch, i, off_ref: (stride * ch + off_ref[0], i, 0)),
                      pl.BlockSpec(blk, lambda ch, i, off_ref: (ch, i, 0))],
            out_specs=pl.BlockSpec(blk, lambda ch, i, off_ref: (ch, i, 0))),
        compiler_params=_params(("arbitrary", "arbitrary"), 40),
    )(offset, mine, theirs)


def _slab_exchange(slab, name):
    def body(slab_in, slab_out, send_sems, recv_sems, local_sem):
        me = _my_index()
        local = pltpu.make_async_copy(slab_in, slab_out.at[me], local_sem)
        local.start()
        sends = []
        for k in range(1, N_DEV):
            peer, _ = _peer(k)
            sends.append(_remote(slab_in, slab_out.at[me], send_sems.at[k - 1], recv_sems.at[k - 1], peer))
        for cp in sends:
            cp.start()
        for k in range(1, N_DEV):
            peer, peer_idx = _peer(k)
            slot = slab_out.at[peer_idx]
            _remote(slot, slot, send_sems.at[k - 1], recv_sems.at[k - 1], peer).wait_recv()
        for cp in sends:
            cp.wait_send()
        local.wait()

    return pl.pallas_call(
        body, name=name, out_shape=jax.ShapeDtypeStruct((N_DEV,) + slab.shape, slab.dtype),
        in_specs=[HBM_SPEC], out_specs=HBM_SPEC,
        scratch_shapes=[pltpu.SemaphoreType.DMA((N_DEV - 1,)), pltpu.SemaphoreType.DMA((N_DEV - 1,)),
                        pltpu.SemaphoreType.DMA],
        compiler_params=pltpu.CompilerParams(has_side_effects=True),
    )(slab)


def _adamw_math(w, g, m, v):
    m = ADAM_B1 * m + (1.0 - ADAM_B1) * g
    v = ADAM_B2 * v + (1.0 - ADAM_B2) * (g * g)
    m_hat = m / (1.0 - ADAM_B1 ** ADAM_STEP)
    v_hat = v / (1.0 - ADAM_B2 ** ADAM_STEP)
    delta = -ADAM_LR * (m_hat / (jnp.sqrt(v_hat) + ADAM_EPS) + ADAM_WD * w)
    return delta, m, v


def _adamw(parts, own, place, w, m, v, name, dep=None):
    n_parts = parts.shape[0]
    r, c = w.shape
    budget = 280 * 1024
    tr = r if r * c <= budget else 1 << ((budget // c).bit_length() - 1)
    assert r % tr == 0

    def body(place_ref, p_ref, own_ref, w_ref, m_ref, v_ref, g_ref, d_ref, nm_ref, nv_ref):
        mine = own_ref[...].astype(F32)
        g = None
        for i in range(n_parts):
            term = jnp.where(place_ref[0] == i, mine, p_ref[i].astype(F32))
            g = term if g is None else g + term
        delta, nm, nv = _adamw_math(w_ref[...], g, m_ref[...], v_ref[...])
        g_ref[...] = g
        d_ref[...] = delta
        nm_ref[...] = nm
        nv_ref[...] = nv

    blk = pl.BlockSpec((tr, c), lambda i, place_ref: (i, 0))
    shape = jax.ShapeDtypeStruct((r, c), F32)
    in_specs = [pl.BlockSpec((n_parts, tr, c), lambda i, place_ref: (0, i, 0)),
                pl.BlockSpec((None, tr, c), lambda i, place_ref: (place_ref[1], i, 0)), blk, blk, blk]
    operands = [parts, own, w, m, v]
    if dep is not None:
        body = _drop_arg(body, 1 + len(operands))
        in_specs.append(pl.BlockSpec((SUB, HEAD), lambda i, place_ref: (0, 0)))
        operands.append(dep)
    return pl.pallas_call(
        body, name=name, out_shape=[shape] * 4,
        grid_spec=pltpu.PrefetchScalarGridSpec(
            num_scalar_prefetch=1, grid=(r // tr,), in_specs=in_specs, out_specs=[blk] * 4),
        compiler_params=_params(("arbitrary",), 48),
    )(place, *operands)


def kernel(x, p, g_pre, w_in, w_s, b_s, ln_v_g, ln_v_b, g_q, g_k, rel_bias, g_out_a, g_out_b, w_out, g_ple, w_ple_gate, w_ple_up, loss_target, m_g_pre, m_w_in, m_w_s, m_b_s, m_ln_v_g, m_ln_v_b, m_g_q, m_g_k, m_rel_bias, m_g_out_a, m_g_out_b, m_w_out, m_g_ple, m_w_ple_gate, m_w_ple_up, v_g_pre, v_w_in, v_w_s, v_b_s, v_ln_v_g, v_ln_v_b, v_g_q, v_g_k, v_rel_bias, v_g_out_a, v_g_out_b, v_w_out, v_g_ple, v_w_ple_gate, v_w_ple_up):
    args = dict(locals())
    small = {"g_pre": g_pre, "w_s": w_s[0], "b_s": b_s[0], "ln_v_g": ln_v_g, "ln_v_b": ln_v_b, "g_q": g_q,
             "g_k": g_k, "rel_bias": rel_bias, "g_out_a": g_out_a, "g_out_b": g_out_b, "g_ple": g_ple}
    big_names = ("w_in", "w_out", "w_ple_gate", "w_ple_up")
    big = {k: args[k][0] for k in big_names}

    wg = _WeightGather({k: _cast_place(big[k], "place_" + k) for k in big_names})
    ex = _GradExchange()
    results = {}

    def big_adamw(k, delivered, dep=None):
        mine, theirs = delivered[k]
        place = jnp.stack([ex.chip, ex.chip])
        return _adamw(theirs, mine, place, big[k], args["m_" + k][0], args["v_" + k][0], "adamw_" + k, dep=dep)

    squeeze = lambda t: {k: (t[k][0] if k in ("w_s", "b_s") else t[k]) for k in SMALL_NAMES}
    small_m = squeeze({k: args["m_" + k] for k in SMALL_NAMES})
    small_v = squeeze({k: args["v_" + k] for k in SMALL_NAMES})
    slab_place = jnp.stack([_my_index().astype(jnp.int32), jnp.zeros((), jnp.int32)])

    def small_adamw(names_, parts, own, call_name):
        packed = _adamw(parts, own[None], slab_place, _pack_small(small, names_), _pack_small(small_m, names_),
                        _pack_small(small_v, names_), call_name)
        for idx in range(4):
            tree = _unpack_small(packed[idx], small, names_)
            for k in names_:
                results.setdefault(k, [None] * 4)[idx] = tree[k].reshape(args[k].shape)
        return packed[0]

    def while_last_travels(token, delivered, dg_pre):
        done = []
        for k in big_names[1:]:
            results[k] = big_adamw(k, delivered, dep=token)
            done.append(results[k][0])
        last_slab = _pack_small({"g_pre": dg_pre}, SMALL_LAST)
        done.append(small_adamw(SMALL_LAST, _slab_exchange(last_slab, "last_exchange"), last_slab, "adamw_last"))
        return done

    loss, grad_x, small_parts, delivered, slabs, _ = _local_step(
        x[0], p[0, 0], loss_target[0], small, wg, ex, while_last_travels)
    results["w_in"] = big_adamw("w_in", delivered)
    for k in big_names:
        results[k] = [t[None] for t in results[k]]
    small_adamw(SMALL_EARLY, slabs[1], slabs[0], "adamw_small")

    names = ("g_pre", "w_in", "w_s", "b_s", "ln_v_g", "ln_v_b", "g_q", "g_k", "rel_bias", "g_out_a", "g_out_b",
             "w_out", "g_ple", "w_ple_gate", "w_ple_up")
    total = lax.psum(loss, AXES)
    out = [total, grad_x[None]]
    for idx in range(4):
        out += [results[k][idx] for k in names]
    return tuple(out)
```
